```python
import math
import jax
import jax.numpy as jnp
from jax import lax
import numpy as np


D_MODEL = 1024
BATCH = 16
SEQ = 2048
DEPTH = 1

GRID_W = 64
CTX_LEN = 256
DN_HEADS = 4
DN_DK = 128
DN_DV = 128
CONV_W = 5
GLA_HEADS = 4
GLA_DK = 64
GLA_DV = 128
GLA_LR = 16
GLA_TAU = 16.0
CHUNK = 64
PEER_HEADS = 8
PEER_NKEYS = 128
PEER_EXPERTS = PEER_NKEYS * PEER_NKEYS
PEER_DQ = 256
PEER_TOPK = 16
PEER_BLOCK = 128
EPS = 1e-6

D_MIX = DN_HEADS * DN_DV + GLA_HEADS * GLA_DV
DN_QKV = 2 * DN_HEADS * DN_DK + DN_HEADS * DN_DV
DN_COLS = DN_QKV + DN_HEADS * DN_DV + 4 * DN_HEADS
GLA_COLS = 2 * GLA_HEADS * GLA_DK + 2 * GLA_HEADS * GLA_DV + 2 * GLA_LR
IN_COLS = DN_COLS + GLA_COLS

kernel_name = "hybrid_gdn_gla_peer_prefix_dit"


def rmsnorm(x, g):
    xf = x.astype(jnp.float32)
    y = xf * lax.rsqrt(jnp.mean(xf * xf, axis=-1, keepdims=True) + EPS)
    return (y * g.astype(jnp.float32)).astype(x.dtype)


def modulate(h, shift, scale):
    return h * (1 + scale) + shift


def l2norm(t):
    tf = t.astype(jnp.float32)
    return tf * lax.rsqrt(jnp.sum(tf * tf, axis=-1, keepdims=True) + EPS)


def head_norm_gate(o, g, gate):
    y = o * lax.rsqrt(jnp.mean(o * o, axis=-1, keepdims=True) + EPS) * g.astype(jnp.float32)
    y = y * jax.nn.silu(gate.astype(jnp.float32))
    return y.reshape(y.shape[:2] + (-1,)).astype(gate.dtype)


def short_conv(t, w):
    pad = CONV_W // 2
    return lax.conv_general_dilated(
        t, w[:, None, :].astype(t.dtype), window_strides=(1,), padding=((pad, pad),),
        dimension_numbers=('NWC', 'WIO', 'NWC'), feature_group_count=t.shape[-1])


def _chunk(t, n):
    b, _, h = t.shape[:3]
    t = t.reshape((b, n, CHUNK, h) + t.shape[3:])
    return t.transpose((1, 0, 3, 2) + tuple(range(4, t.ndim)))


def _unchunk(t):
    n, b, h, c, d = t.shape
    return t.transpose(1, 0, 3, 2, 4).reshape(b, n * c, h, d)


def _tril(strict):
    i = jnp.arange(CHUNK)
    return (i[:, None] > i[None, :]) if strict else (i[:, None] >= i[None, :])


def gated_delta_scan(q, k, v, g, beta, s0, with_output):
    f32 = jnp.float32
    n = q.shape[1] // CHUNK
    dv = v.shape[-1]
    q, k, v, g, beta = (_chunk(t.astype(f32), n) for t in (q, k, v, g, beta))
    gc = jnp.cumsum(g, axis=-1)
    decay = jnp.exp(jnp.where(_tril(False), gc[..., :, None] - gc[..., None, :], -jnp.inf))
    kb = k * beta[..., None]
    lower = jnp.where(_tril(True), jnp.einsum('nbhtd,nbhsd->nbhts', kb, k) * decay, 0.0)
    rhs = jnp.concatenate([v * beta[..., None], kb * jnp.exp(gc)[..., None]], axis=-1)
    sol = lax.linalg.triangular_solve(lower, rhs, left_side=True, lower=True, unit_diagonal=True)
    u, w = sol[..., :dv], sol[..., dv:]
    k_dec = k * jnp.exp(gc[..., -1:] - gc)[..., None]
    g_last = jnp.exp(gc[..., -1])

    def step(S, xs):
        u_c, w_c, kd_c, gl_c = xs[:4]
        v_new = u_c - jnp.einsum('bhck,bhkv->bhcv', w_c, S)
        S_next = S * gl_c[..., None, None] + jnp.einsum('bhck,bhcv->bhkv', kd_c, v_new)
        if not with_output:
            return S_next, None
        a_c, qd_c = xs[4:]
        o = jnp.einsum('bhck,bhkv->bhcv', qd_c, S) + jnp.einsum('bhts,bhsv->bhtv', a_c, v_new)
        return S_next, o

    if with_output:
        a_qk = jnp.einsum('nbhtd,nbhsd->nbhts', q, k) * decay
        xs = (u, w, k_dec, g_last, a_qk, q * jnp.exp(gc)[..., None])
    else:
        xs = (u, w, k_dec, g_last)
    S, o = lax.scan(step, s0.astype(f32), xs)
    return (_unchunk(o) if with_output else None), S


def gla_scan(q, k, v, log_a, s0, with_output):
    f32 = jnp.float32
    n = q.shape[1] // CHUNK
    q, k, v, la = (_chunk(t.astype(f32), n) for t in (q, k, v, log_a))
    b = jnp.cumsum(la, axis=-2)
    b_last = b[..., -1, :]
    k_dec = k * jnp.exp(b_last[..., None, :] - b)
    causal = _tril(False)

    def step(S, xs):
        v_c, kd_c, bl_c = xs[:3]
        S_next = S * jnp.exp(bl_c)[..., :, None] + jnp.einsum('bhck,bhcv->bhkv', kd_c, v_c)
        if not with_output:
            return S_next, None
        q_c, k_c, b_c = xs[3:]
        diff = jnp.where(causal[:, :, None], b_c[..., :, None, :] - b_c[..., None, :, :], -jnp.inf)
        att = jnp.einsum('bhtk,bhsk,bhtsk->bhts', q_c, k_c, jnp.exp(diff))
        o = (jnp.einsum('bhck,bhkv->bhcv', q_c * jnp.exp(b_c), S)
             + jnp.einsum('bhts,bhsv->bhtv', att, v_c))
        return S_next, o

    xs = (v, k_dec, b_last, q, k, b) if with_output else (v, k_dec, b_last)
    S, o = lax.scan(step, s0.astype(f32), xs)
    return (_unchunk(o) if with_output else None), S


def bidirectional(scan_fn, ctx_f, ctx_b, lat_f, lat_b, s0, ctx_out):
    rev = lambda args: tuple(jnp.flip(a, axis=1) for a in args)
    oc_f, sc_f = scan_fn(*ctx_f, s0, ctx_out)
    oc_b, sc_b = scan_fn(*rev(ctx_b), s0, ctx_out)
    ol_f, _ = scan_fn(*lat_f, sc_f, True)
    ol_b, _ = scan_fn(*rev(lat_b), sc_b, True)
    o_lat = ol_f + jnp.flip(ol_b, axis=1)
    o_ctx = (oc_f + jnp.flip(oc_b, axis=1)) if ctx_out else None
    return o_ctx, o_lat


def dn_features(p, conv_w, a_log, dt_bias):
    b, l, _ = p.shape
    qk, vd, h = DN_HEADS * DN_DK, DN_HEADS * DN_DV, DN_HEADS
    qkv = jax.nn.silu(short_conv(p[..., :DN_QKV], conv_w))
    q = l2norm(qkv[..., :qk].reshape(b, l, h, DN_DK)) * DN_DK ** -0.5
    k = l2norm(qkv[..., qk:2 * qk].reshape(b, l, h, DN_DK))
    v = qkv[..., 2 * qk:].reshape(b, l, h, DN_DV)
    z = p[..., DN_QKV:DN_QKV + vd].reshape(b, l, h, DN_DV)
    o = DN_QKV + vd
    beta = jax.nn.sigmoid(p[..., o:o + 2 * h].astype(jnp.float32)).reshape(b, l, 2, h)
    a = p[..., o + 2 * h:o + 4 * h].astype(jnp.float32).reshape(b, l, 2, h)
    g = -jnp.exp(a_log.astype(jnp.float32)) * jax.nn.softplus(a + dt_bias.astype(jnp.float32))
    return q, k, v, z, g, beta


def gla_features(p, wa2, ba):
    b, l, _ = p.shape
    qk, vd = GLA_HEADS * GLA_DK, GLA_HEADS * GLA_DV
    q = p[..., :qk].reshape(b, l, GLA_HEADS, GLA_DK) * GLA_DK ** -0.5
    k = p[..., qk:2 * qk].reshape(b, l, GLA_HEADS, GLA_DK)
    v = p[..., 2 * qk:2 * qk + vd].reshape(b, l, GLA_HEADS, GLA_DV)
    r = p[..., 2 * qk + vd:2 * qk + 2 * vd].reshape(b, l, GLA_HEADS, GLA_DV)
    lr = p[..., 2 * qk + 2 * vd:].reshape(b, l, 2, GLA_LR)
    pre = jnp.einsum('bldr,drk->bldk', lr, wa2) + ba
    log_a = jax.nn.log_sigmoid(pre.astype(jnp.float32)) / GLA_TAU
    return q, k, v, r, log_a.reshape(b, l, 2, GLA_HEADS, GLA_DK)


def to_col_major(t, rows):
    b = t.shape[0]
    return t.reshape((b, rows, GRID_W) + t.shape[2:]).swapaxes(1, 2).reshape(t.shape)


def from_col_major(t, rows):
    b = t.shape[0]
    return t.reshape((b, GRID_W, rows) + t.shape[2:]).swapaxes(1, 2).reshape(t.shape)


def token_mixer(h_ctx, h_lat, rows, w_in, conv_w, dn_a_log, dn_dt_bias, dn_norm_g,
                gla_wa2, gla_ba, gla_norm_g, w_out, ctx_out):
    b = h_lat.shape[0]
    p_c = h_ctx @ w_in
    p_l = h_lat @ w_in
    qc, kc, vc, zc, gc, bc = dn_features(p_c[..., :DN_COLS], conv_w, dn_a_log, dn_dt_bias)
    ql, kl, vl, zl, gl, bl = dn_features(p_l[..., :DN_COLS], conv_w, dn_a_log, dn_dt_bias)
    s0_dn = jnp.zeros((b, DN_HEADS, DN_DK, DN_DV), jnp.float32)
    dn_c, dn_l = bidirectional(
        gated_delta_scan,
        (qc, kc, vc, gc[:, :, 0], bc[:, :, 0]), (qc, kc, vc, gc[:, :, 1], bc[:, :, 1]),
        (ql, kl, vl, gl[:, :, 0], bl[:, :, 0]), (ql, kl, vl, gl[:, :, 1], bl[:, :, 1]),
        s0_dn, ctx_out)
    gqc, gkc, gvc, grc, lac = gla_features(p_c[..., DN_COLS:], gla_wa2, gla_ba)
    gql, gkl, gvl, grl, lal = gla_features(p_l[..., DN_COLS:], gla_wa2, gla_ba)
    col = lambda t: to_col_major(t, rows)
    gql, gkl, gvl, lal = col(gql), col(gkl), col(gvl), col(lal)
    s0_gla = jnp.zeros((b, GLA_HEADS, GLA_DK, GLA_DV), jnp.float32)
    gla_c, gla_l = bidirectional(
        gla_scan,
        (gqc, gkc, gvc, lac[:, :, 0]), (gqc, gkc, gvc, lac[:, :, 1]),
        (gql, gkl, gvl, lal[:, :, 0]), (gql, gkl, gvl, lal[:, :, 1]),
        s0_gla, ctx_out)
    gla_l = from_col_major(gla_l, rows)
    y_lat = jnp.concatenate([head_norm_gate(dn_l, dn_norm_g, zl),
                             head_norm_gate(gla_l, gla_norm_g, grl)], axis=-1) @ w_out
    y_ctx = None
    if ctx_out:
        y_ctx = jnp.concatenate([head_norm_gate(dn_c, dn_norm_g, zc),
                                 head_norm_gate(gla_c, gla_norm_g, grc)], axis=-1) @ w_out
    return y_ctx, y_lat


def peer(h, wq, keys, u_tab, v_tab):
    b, l, d = h.shape

    def block(hb):
        t = hb.shape[0]
        q = (hb @ wq).reshape(t, PEER_HEADS, 2, PEER_DQ // 2)
        s = jnp.einsum('thpd,hpkd->thpk', q, keys)
        top_s, top_i = lax.top_k(s, PEER_TOPK)
        cand_s = (top_s[:, :, 0, :, None] + top_s[:, :, 1, None, :]).reshape(t, PEER_HEADS, -1)
        cand_i = (top_i[:, :, 0, :, None] * PEER_NKEYS + top_i[:, :, 1, None, :]).reshape(t, PEER_HEADS, -1)
        best_s, pos = lax.top_k(cand_s, PEER_TOPK)
        idx = jnp.take_along_axis(cand_i, pos, axis=-1)
        gate = jax.nn.softmax(best_s.astype(jnp.float32), axis=-1)
        act = jax.nn.gelu(jnp.einsum('thkd,td->thk', u_tab[idx], hb), approximate=False)
        return jnp.einsum('thk,thkd->td', gate * act, v_tab[idx]).astype(hb.dtype)

    y = lax.map(block, h.reshape(-1, PEER_BLOCK, d))
    return y.reshape(b, l, d)


def setup_inputs(seed: int = 0) -> dict:
    key = jax.random.key(seed)
    ks = jax.random.split(key, 22)
    f32 = jnp.float32
    nrm = lambda k, shape, scale: jax.random.normal(k, shape, f32) * scale
    x = nrm(ks[0], (BATCH, SEQ, D_MODEL), 1.0)
    c = nrm(ks[1], (BATCH, D_MODEL), 1.0)
    ctx = nrm(ks[2], (BATCH, CTX_LEN, D_MODEL), 1.0)
    c_ctx = nrm(ks[3], (D_MODEL,), 1.0)
    w_ada = nrm(ks[4], (DEPTH, D_MODEL, 6 * D_MODEL), 0.5 * D_MODEL ** -0.5)
    b_ada = nrm(ks[5], (DEPTH, 6 * D_MODEL), 0.02)
    norm1_g = 1.0 + nrm(ks[6], (DEPTH, D_MODEL), 0.02)
    norm2_g = 1.0 + nrm(ks[7], (DEPTH, D_MODEL), 0.02)
    w_in = nrm(ks[8], (DEPTH, D_MODEL, IN_COLS), D_MODEL ** -0.5)
    conv_w = nrm(ks[9], (DEPTH, CONV_W, DN_QKV), CONV_W ** -0.5)
    dn_a_log = jnp.log(jax.random.uniform(ks[10], (DEPTH, 2, DN_HEADS), f32, 1.0, 16.0))
    dt = jnp.exp(jax.random.uniform(ks[11], (DEPTH, 2, DN_HEADS), f32, math.log(1e-3), math.log(1e-1)))
    dn_dt_bias = dt + jnp.log(-jnp.expm1(-dt))
    dn_norm_g = 1.0 + nrm(ks[12], (DEPTH, DN_DV), 0.02)
    gla_wa2 = nrm(ks[13], (DEPTH, 2, GLA_LR, GLA_HEADS * GLA_DK), GLA_LR ** -0.5)
    gla_ba = nrm(ks[14], (DEPTH, 2, GLA_HEADS * GLA_DK), 0.1)
    gla_norm_g = 1.0 + nrm(ks[15], (DEPTH, GLA_DV), 0.02)
    w_out = nrm(ks[16], (DEPTH, D_MIX, D_MODEL), D_MIX ** -0.5)
    peer_wq = nrm(ks[17], (DEPTH, D_MODEL, PEER_HEADS * PEER_DQ), D_MODEL ** -0.5)
    peer_keys = nrm(ks[18], (DEPTH, PEER_HEADS, 2, PEER_NKEYS, PEER_DQ // 2), (PEER_DQ // 2) ** -0.5)
    peer_u = nrm(ks[19], (DEPTH, PEER_EXPERTS, D_MODEL), D_MODEL ** -0.5)
    peer_v = nrm(ks[20], (DEPTH, PEER_EXPERTS, D_MODEL), 0.5)
    final_g = 1.0 + nrm(ks[21], (D_MODEL,), 0.02)
    return {"x": x, "c": c, "ctx": ctx, "c_ctx": c_ctx, "w_ada": w_ada, "b_ada": b_ada,
            "norm1_g": norm1_g, "norm2_g": norm2_g, "w_in": w_in, "conv_w": conv_w,
            "dn_a_log": dn_a_log, "dn_dt_bias": dn_dt_bias, "dn_norm_g": dn_norm_g,
            "gla_wa2": gla_wa2, "gla_ba": gla_ba, "gla_norm_g": gla_norm_g, "w_out": w_out,
            "peer_wq": peer_wq, "peer_keys": peer_keys, "peer_u": peer_u, "peer_v": peer_v,
            "final_g": final_g}


def reference(x, c, ctx, c_ctx, w_ada, b_ada, norm1_g, norm2_g, w_in, conv_w, dn_a_log,
              dn_dt_bias, dn_norm_g, gla_wa2, gla_ba, gla_norm_g, w_out, peer_wq, peer_keys,
              peer_u, peer_v, final_g):
    rows = x.shape[1] // GRID_W
    for l in range(DEPTH):
        last = l == DEPTH - 1
        mod_l = (jax.nn.silu(c) @ w_ada[l] + b_ada[l])[:, None, :]
        mod_c = jax.nn.silu(c_ctx) @ w_ada[l] + b_ada[l]
        sh1, sc1, g1, sh2, sc2, g2 = jnp.split(mod_l, 6, axis=-1)
        csh1, csc1, cg1, csh2, csc2, cg2 = jnp.split(mod_c, 6, axis=-1)
        h_lat = modulate(rmsnorm(x, norm1_g[l]), sh1, sc1)
        h_ctx = modulate(rmsnorm(ctx, norm1_g[l]), csh1, csc1)
        y_ctx, y_lat = token_mixer(h_ctx, h_lat, rows, w_in[l], conv_w[l], dn_a_log[l],
                                   dn_dt_bias[l], dn_norm_g[l], gla_wa2[l], gla_ba[l],
                                   gla_norm_g[l], w_out[l], not last)
        x = x + g1 * y_lat
        x = x + g2 * peer(modulate(rmsnorm(x, norm2_g[l]), sh2, sc2),
                          peer_wq[l], peer_keys[l], peer_u[l], peer_v[l])
        if not last:
            ctx = ctx + cg1 * y_ctx
            ctx = ctx + cg2 * peer(modulate(rmsnorm(ctx, norm2_g[l]), csh2, csc2),
                                   peer_wq[l], peer_keys[l], peer_u[l], peer_v[l])
    return rmsnorm(x, final_g)
```

```python
import math
import jax
import jax.numpy as jnp
from jax import lax
import numpy as np
from jax.experimental import pallas as pl

D_MODEL = 1024
GRID_W = 64
CTX_LEN = 256
DN_HEADS = 4
DN_DK = 128
DN_DV = 128
CONV_W = 5
GLA_HEADS = 4
GLA_DK = 64
GLA_DV = 128
GLA_LR = 16
GLA_TAU = 16.0
CHUNK = 64
PEER_HEADS = 8
PEER_NKEYS = 128
PEER_DQ = 256
PEER_TOPK = 16
PEER_BLOCK = 128
EPS = 1e-6
DN_QKV = 2 * DN_HEADS * DN_DK + DN_HEADS * DN_DV
DN_COLS = DN_QKV + DN_HEADS * DN_DV + 4 * DN_HEADS


def rmsnorm(x, g):
    xf = x.astype(jnp.float32)
    y = xf * lax.rsqrt(jnp.mean(xf * xf, axis=-1, keepdims=True) + EPS)
    return (y * g.astype(jnp.float32)).astype(x.dtype)


def modulate(h, shift, scale):
    return h * (1 + scale) + shift


def l2norm(t):
    tf = t.astype(jnp.float32)
    return tf * lax.rsqrt(jnp.sum(tf * tf, axis=-1, keepdims=True) + EPS)


def head_norm_gate(o, g, gate):
    y = o * lax.rsqrt(jnp.mean(o * o, axis=-1, keepdims=True) + EPS) * g.astype(jnp.float32)
    y = y * jax.nn.silu(gate.astype(jnp.float32))
    return y.reshape(y.shape[:2] + (-1,)).astype(gate.dtype)


def short_conv(t, w):
    pad = CONV_W // 2
    return lax.conv_general_dilated(
        t, w[:, None, :].astype(t.dtype), window_strides=(1,), padding=((pad, pad),),
        dimension_numbers=('NWC', 'WIO', 'NWC'), feature_group_count=t.shape[-1])


def _chunk(t, n):
    b, _, h = t.shape[:3]
    t = t.reshape((b, n, CHUNK, h) + t.shape[3:])
    return t.transpose((1, 0, 3, 2) + tuple(range(4, t.ndim)))


def _unchunk(t):
    n, b, h, c, d = t.shape
    return t.transpose(1, 0, 3, 2, 4).reshape(b, n * c, h, d)


def _tril(strict):
    i = jnp.arange(CHUNK)
    return (i[:, None] > i[None, :]) if strict else (i[:, None] >= i[None, :])


def gated_delta_scan(q, k, v, g, beta, s0, with_output):
    f32 = jnp.float32
    n = q.shape[1] // CHUNK
    dv = v.shape[-1]
    q, k, v, g, beta = (_chunk(t.astype(f32), n) for t in (q, k, v, g, beta))
    gc = jnp.cumsum(g, axis=-1)
    decay = jnp.exp(jnp.where(_tril(False), gc[..., :, None] - gc[..., None, :], -jnp.inf))
    kb = k * beta[..., None]
    lower = jnp.where(_tril(True), jnp.einsum('nbhtd,nbhsd->nbhts', kb, k) * decay, 0.0)
    rhs = jnp.concatenate([v * beta[..., None], kb * jnp.exp(gc)[..., None]], axis=-1)
    sol = lax.linalg.triangular_solve(lower, rhs, left_side=True, lower=True, unit_diagonal=True)
    u, w = sol[..., :dv], sol[..., dv:]
    k_dec = k * jnp.exp(gc[..., -1:] - gc)[..., None]
    g_last = jnp.exp(gc[..., -1])

    def step(S, xs):
        u_c, w_c, kd_c, gl_c = xs[:4]
        v_new = u_c - jnp.einsum('bhck,bhkv->bhcv', w_c, S)
        S_next = S * gl_c[..., None, None] + jnp.einsum('bhck,bhcv->bhkv', kd_c, v_new)
        if not with_output:
            return S_next, None
        a_c, qd_c = xs[4:]
        o = jnp.einsum('bhck,bhkv->bhcv', qd_c, S) + jnp.einsum('bhts,bhsv->bhtv', a_c, v_new)
        return S_next, o

    if with_output:
        a_qk = jnp.einsum('nbhtd,nbhsd->nbhts', q, k) * decay
        xs = (u, w, k_dec, g_last, a_qk, q * jnp.exp(gc)[..., None])
    else:
        xs = (u, w, k_dec, g_last)
    S, o = lax.scan(step, s0.astype(f32), xs)
    return (_unchunk(o) if with_output else None), S


def gla_scan(q, k, v, log_a, s0, with_output):
    f32 = jnp.float32
    n = q.shape[1] // CHUNK
    q, k, v, la = (_chunk(t.astype(f32), n) for t in (q, k, v, log_a))
    b = jnp.cumsum(la, axis=-2)
    b_last = b[..., -1, :]
    k_dec = k * jnp.exp(b_last[..., None, :] - b)
    causal = _tril(False)

    def step(S, xs):
        v_c, kd_c, bl_c = xs[:3]
        S_next = S * jnp.exp(bl_c)[..., :, None] + jnp.einsum('bhck,bhcv->bhkv', kd_c, v_c)
        if not with_output:
            return S_next, None
        q_c, k_c, b_c = xs[3:]
        diff = jnp.where(causal[:, :, None], b_c[..., :, None, :] - b_c[..., None, :, :], -jnp.inf)
        att = jnp.einsum('bhtk,bhsk,bhtsk->bhts', q_c, k_c, jnp.exp(diff))
        o = (jnp.einsum('bhck,bhkv->bhcv', q_c * jnp.exp(b_c), S)
             + jnp.einsum('bhts,bhsv->bhtv', att, v_c))
        return S_next, o

    xs = (v, k_dec, b_last, q, k, b) if with_output else (v, k_dec, b_last)
    S, o = lax.scan(step, s0.astype(f32), xs)
    return (_unchunk(o) if with_output else None), S


def bidirectional(scan_fn, ctx_f, ctx_b, lat_f, lat_b, s0, ctx_out):
    rev = lambda args: tuple(jnp.flip(a, axis=1) for a in args)
    oc_f, sc_f = scan_fn(*ctx_f, s0, ctx_out)
    oc_b, sc_b = scan_fn(*rev(ctx_b), s0, ctx_out)
    ol_f, _ = scan_fn(*lat_f, sc_f, True)
    ol_b, _ = scan_fn(*rev(lat_b), sc_b, True)
    o_lat = ol_f + jnp.flip(ol_b, axis=1)
    o_ctx = (oc_f + jnp.flip(oc_b, axis=1)) if ctx_out else None
    return o_ctx, o_lat


def dn_features(p, conv_w, a_log, dt_bias):
    b, l, _ = p.shape
    qk, vd, h = DN_HEADS * DN_DK, DN_HEADS * DN_DV, DN_HEADS
    qkv = jax.nn.silu(short_conv(p[..., :DN_QKV], conv_w))
    q = l2norm(qkv[..., :qk].reshape(b, l, h, DN_DK)) * DN_DK ** -0.5
    k = l2norm(qkv[..., qk:2 * qk].reshape(b, l, h, DN_DK))
    v = qkv[..., 2 * qk:].reshape(b, l, h, DN_DV)
    z = p[..., DN_QKV:DN_QKV + vd].reshape(b, l, h, DN_DV)
    o = DN_QKV + vd
    beta = jax.nn.sigmoid(p[..., o:o + 2 * h].astype(jnp.float32)).reshape(b, l, 2, h)
    a = p[..., o + 2 * h:o + 4 * h].astype(jnp.float32).reshape(b, l, 2, h)
    g = -jnp.exp(a_log.astype(jnp.float32)) * jax.nn.softplus(a + dt_bias.astype(jnp.float32))
    return q, k, v, z, g, beta


def gla_features(p, wa2, ba):
    b, l, _ = p.shape
    qk, vd = GLA_HEADS * GLA_DK, GLA_HEADS * GLA_DV
    q = p[..., :qk].reshape(b, l, GLA_HEADS, GLA_DK) * GLA_DK ** -0.5
    k = p[..., qk:2 * qk].reshape(b, l, GLA_HEADS, GLA_DK)
    v = p[..., 2 * qk:2 * qk + vd].reshape(b, l, GLA_HEADS, GLA_DV)
    r = p[..., 2 * qk + vd:2 * qk + 2 * vd].reshape(b, l, GLA_HEADS, GLA_DV)
    lr = p[..., 2 * qk + 2 * vd:].reshape(b, l, 2, GLA_LR)
    pre = jnp.einsum('bldr,drk->bldk', lr, wa2) + ba
    log_a = jax.nn.log_sigmoid(pre.astype(jnp.float32)) / GLA_TAU
    return q, k, v, r, log_a.reshape(b, l, 2, GLA_HEADS, GLA_DK)


def to_col_major(t, rows):
    b = t.shape[0]
    return t.reshape((b, rows, GRID_W) + t.shape[2:]).swapaxes(1, 2).reshape(t.shape)


def from_col_major(t, rows):
    b = t.shape[0]
    return t.reshape((b, GRID_W, rows) + t.shape[2:]).swapaxes(1, 2).reshape(t.shape)


def token_mixer(h_ctx, h_lat, rows, w_in, conv_w, dn_a_log, dn_dt_bias, dn_norm_g,
                gla_wa2, gla_ba, gla_norm_g, w_out):
    b = h_lat.shape[0]
    p_c = h_ctx @ w_in
    p_l = h_lat @ w_in
    qc, kc, vc, zc, gc, bc = dn_features(p_c[..., :DN_COLS], conv_w, dn_a_log, dn_dt_bias)
    ql, kl, vl, zl, gl, bl = dn_features(p_l[..., :DN_COLS], conv_w, dn_a_log, dn_dt_bias)
    s0_dn = jnp.zeros((b, DN_HEADS, DN_DK, DN_DV), jnp.float32)
    _, dn_l = bidirectional(
        gated_delta_scan,
        (qc, kc, vc, gc[:, :, 0], bc[:, :, 0]), (qc, kc, vc, gc[:, :, 1], bc[:, :, 1]),
        (ql, kl, vl, gl[:, :, 0], bl[:, :, 0]), (ql, kl, vl, gl[:, :, 1], bl[:, :, 1]),
        s0_dn, False)
    gqc, gkc, gvc, grc, lac = gla_features(p_c[..., DN_COLS:], gla_wa2, gla_ba)
    gql, gkl, gvl, grl, lal = gla_features(p_l[..., DN_COLS:], gla_wa2, gla_ba)
    col = lambda t: to_col_major(t, rows)
    gql, gkl, gvl, lal = col(gql), col(gkl), col(gvl), col(lal)
    s0_gla = jnp.zeros((b, GLA_HEADS, GLA_DK, GLA_DV), jnp.float32)
    _, gla_l = bidirectional(
        gla_scan,
        (gqc, gkc, gvc, lac[:, :, 0]), (gqc, gkc, gvc, lac[:, :, 1]),
        (gql, gkl, gvl, lal[:, :, 0]), (gql, gkl, gvl, lal[:, :, 1]),
        s0_gla, False)
    gla_l = from_col_major(gla_l, rows)
    return jnp.concatenate([head_norm_gate(dn_l, dn_norm_g, zl),
                            head_norm_gate(gla_l, gla_norm_g, grl)], axis=-1) @ w_out


def peer(h, wq, keys, u_tab, v_tab):
    b, l, d = h.shape

    def block(hb):
        t = hb.shape[0]
        q = (hb @ wq).reshape(t, PEER_HEADS, 2, PEER_DQ // 2)
        s = jnp.einsum('thpd,hpkd->thpk', q, keys)
        top_s, top_i = lax.top_k(s, PEER_TOPK)
        cand_s = (top_s[:, :, 0, :, None] + top_s[:, :, 1, None, :]).reshape(t, PEER_HEADS, -1)
        cand_i = (top_i[:, :, 0, :, None] * PEER_NKEYS + top_i[:, :, 1, None, :]).reshape(t, PEER_HEADS, -1)
        best_s, pos = lax.top_k(cand_s, PEER_TOPK)
        idx = jnp.take_along_axis(cand_i, pos, axis=-1)
        gate = jax.nn.softmax(best_s.astype(jnp.float32), axis=-1)
        act = jax.nn.gelu(jnp.einsum('thkd,td->thk', u_tab[idx], hb), approximate=False)
        return jnp.einsum('thk,thkd->td', gate * act, v_tab[idx]).astype(hb.dtype)

    y = lax.map(block, h.reshape(-1, PEER_BLOCK, d))
    return y.reshape(b, l, d)


def _final_norm_kernel(x_ref, g_ref, o_ref):
    xf = x_ref[...]
    y = xf * lax.rsqrt(jnp.mean(xf * xf, axis=-1, keepdims=True) + EPS)
    o_ref[...] = y * g_ref[...]


def kernel(x, c, ctx, c_ctx, w_ada, b_ada, norm1_g, norm2_g, w_in, conv_w, dn_a_log,
           dn_dt_bias, dn_norm_g, gla_wa2, gla_ba, gla_norm_g, w_out, peer_wq, peer_keys,
           peer_u, peer_v, final_g):
    rows = x.shape[1] // GRID_W
    l = 0
    mod_l = (jax.nn.silu(c) @ w_ada[l] + b_ada[l])[:, None, :]
    mod_c = jax.nn.silu(c_ctx) @ w_ada[l] + b_ada[l]
    sh1, sc1, g1, sh2, sc2, g2 = jnp.split(mod_l, 6, axis=-1)
    csh1, csc1, cg1, csh2, csc2, cg2 = jnp.split(mod_c, 6, axis=-1)
    h_lat = modulate(rmsnorm(x, norm1_g[l]), sh1, sc1)
    h_ctx = modulate(rmsnorm(ctx, norm1_g[l]), csh1, csc1)
    y_lat = token_mixer(h_ctx, h_lat, rows, w_in[l], conv_w[l], dn_a_log[l],
                        dn_dt_bias[l], dn_norm_g[l], gla_wa2[l], gla_ba[l],
                        gla_norm_g[l], w_out[l])
    x = x + g1 * y_lat
    x = x + g2 * peer(modulate(rmsnorm(x, norm2_g[l]), sh2, sc2),
                      peer_wq[l], peer_keys[l], peer_u[l], peer_v[l])
    b, sl, d = x.shape
    x2 = x.reshape(b * sl, d)
    tm = 512
    out = pl.pallas_call(
        _final_norm_kernel,
        grid=(b * sl // tm,),
        in_specs=[pl.BlockSpec((tm, d), lambda i: (i, 0)), pl.BlockSpec((1, d), lambda i: (0, 0))],
        out_specs=pl.BlockSpec((tm, d), lambda i: (i, 0)),
        out_shape=jax.ShapeDtypeStruct((b * sl, d), jnp.float32),
    )(x2, final_g.reshape(1, d))
    return out.reshape(b, sl, d)
```

```python
import functools
import math
import jax
import jax.numpy as jnp
from jax import lax
import numpy as np
from jax.experimental import pallas as pl
from jax.experimental.pallas import tpu as pltpu
from jax.experimental.pallas import tpu_sc as plsc

D_MODEL = 1024
GRID_W = 64
CTX_LEN = 256
DN_HEADS = 4
DN_DK = 128
DN_DV = 128
CONV_W = 5
GLA_HEADS = 4
GLA_DK = 64
GLA_DV = 128
GLA_LR = 16
GLA_TAU = 16.0
CHUNK = 64
PEER_HEADS = 8
PEER_NKEYS = 128
PEER_DQ = 256
PEER_TOPK = 16
PEER_BLOCK = 128
EPS = 1e-6
DN_QKV = 2 * DN_HEADS * DN_DK + DN_HEADS * DN_DV
DN_COLS = DN_QKV + DN_HEADS * DN_DV + 4 * DN_HEADS


def rmsnorm(x, g):
    xf = x.astype(jnp.float32)
    y = xf * lax.rsqrt(jnp.mean(xf * xf, axis=-1, keepdims=True) + EPS)
    return (y * g.astype(jnp.float32)).astype(x.dtype)


def modulate(h, shift, scale):
    return h * (1 + scale) + shift


def l2norm(t):
    tf = t.astype(jnp.float32)
    return tf * lax.rsqrt(jnp.sum(tf * tf, axis=-1, keepdims=True) + EPS)


def head_norm_gate(o, g, gate):
    y = o * lax.rsqrt(jnp.mean(o * o, axis=-1, keepdims=True) + EPS) * g.astype(jnp.float32)
    y = y * jax.nn.silu(gate.astype(jnp.float32))
    return y.reshape(y.shape[:2] + (-1,)).astype(gate.dtype)


def short_conv(t, w):
    pad = CONV_W // 2
    return lax.conv_general_dilated(
        t, w[:, None, :].astype(t.dtype), window_strides=(1,), padding=((pad, pad),),
        dimension_numbers=('NWC', 'WIO', 'NWC'), feature_group_count=t.shape[-1])


def _chunk(t, n):
    b, _, h = t.shape[:3]
    t = t.reshape((b, n, CHUNK, h) + t.shape[3:])
    return t.transpose((1, 0, 3, 2) + tuple(range(4, t.ndim)))


def _unchunk(t):
    n, b, h, c, d = t.shape
    return t.transpose(1, 0, 3, 2, 4).reshape(b, n * c, h, d)


def _tril(strict):
    i = jnp.arange(CHUNK)
    return (i[:, None] > i[None, :]) if strict else (i[:, None] >= i[None, :])


def gated_delta_scan(q, k, v, g, beta, s0, with_output):
    f32 = jnp.float32
    n = q.shape[1] // CHUNK
    dv = v.shape[-1]
    q, k, v, g, beta = (_chunk(t.astype(f32), n) for t in (q, k, v, g, beta))
    gc = jnp.cumsum(g, axis=-1)
    decay = jnp.exp(jnp.where(_tril(False), gc[..., :, None] - gc[..., None, :], -jnp.inf))
    kb = k * beta[..., None]
    lower = jnp.where(_tril(True), jnp.einsum('nbhtd,nbhsd->nbhts', kb, k) * decay, 0.0)
    rhs = jnp.concatenate([v * beta[..., None], kb * jnp.exp(gc)[..., None]], axis=-1)
    sol = lax.linalg.triangular_solve(lower, rhs, left_side=True, lower=True, unit_diagonal=True)
    u, w = sol[..., :dv], sol[..., dv:]
    k_dec = k * jnp.exp(gc[..., -1:] - gc)[..., None]
    g_last = jnp.exp(gc[..., -1])

    def step(S, xs):
        u_c, w_c, kd_c, gl_c = xs[:4]
        v_new = u_c - jnp.einsum('bhck,bhkv->bhcv', w_c, S)
        S_next = S * gl_c[..., None, None] + jnp.einsum('bhck,bhcv->bhkv', kd_c, v_new)
        if not with_output:
            return S_next, None
        a_c, qd_c = xs[4:]
        o = jnp.einsum('bhck,bhkv->bhcv', qd_c, S) + jnp.einsum('bhts,bhsv->bhtv', a_c, v_new)
        return S_next, o

    if with_output:
        a_qk = jnp.einsum('nbhtd,nbhsd->nbhts', q, k) * decay
        xs = (u, w, k_dec, g_last, a_qk, q * jnp.exp(gc)[..., None])
    else:
        xs = (u, w, k_dec, g_last)
    S, o = lax.scan(step, s0.astype(f32), xs)
    return (_unchunk(o) if with_output else None), S


def gla_scan(q, k, v, log_a, s0, with_output):
    f32 = jnp.float32
    n = q.shape[1] // CHUNK
    q, k, v, la = (_chunk(t.astype(f32), n) for t in (q, k, v, log_a))
    b = jnp.cumsum(la, axis=-2)
    b_last = b[..., -1, :]
    k_dec = k * jnp.exp(b_last[..., None, :] - b)
    causal = _tril(False)

    def step(S, xs):
        v_c, kd_c, bl_c = xs[:3]
        S_next = S * jnp.exp(bl_c)[..., :, None] + jnp.einsum('bhck,bhcv->bhkv', kd_c, v_c)
        if not with_output:
            return S_next, None
        q_c, k_c, b_c = xs[3:]
        diff = jnp.where(causal[:, :, None], b_c[..., :, None, :] - b_c[..., None, :, :], -jnp.inf)
        att = jnp.einsum('bhtk,bhsk,bhtsk->bhts', q_c, k_c, jnp.exp(diff))
        o = (jnp.einsum('bhck,bhkv->bhcv', q_c * jnp.exp(b_c), S)
             + jnp.einsum('bhts,bhsv->bhtv', att, v_c))
        return S_next, o

    xs = (v, k_dec, b_last, q, k, b) if with_output else (v, k_dec, b_last)
    S, o = lax.scan(step, s0.astype(f32), xs)
    return (_unchunk(o) if with_output else None), S


def bidirectional(scan_fn, ctx_f, ctx_b, lat_f, lat_b, s0, ctx_out):
    rev = lambda args: tuple(jnp.flip(a, axis=1) for a in args)
    oc_f, sc_f = scan_fn(*ctx_f, s0, ctx_out)
    oc_b, sc_b = scan_fn(*rev(ctx_b), s0, ctx_out)
    ol_f, _ = scan_fn(*lat_f, sc_f, True)
    ol_b, _ = scan_fn(*rev(lat_b), sc_b, True)
    o_lat = ol_f + jnp.flip(ol_b, axis=1)
    o_ctx = (oc_f + jnp.flip(oc_b, axis=1)) if ctx_out else None
    return o_ctx, o_lat


def dn_features(p, conv_w, a_log, dt_bias):
    b, l, _ = p.shape
    qk, vd, h = DN_HEADS * DN_DK, DN_HEADS * DN_DV, DN_HEADS
    qkv = jax.nn.silu(short_conv(p[..., :DN_QKV], conv_w))
    q = l2norm(qkv[..., :qk].reshape(b, l, h, DN_DK)) * DN_DK ** -0.5
    k = l2norm(qkv[..., qk:2 * qk].reshape(b, l, h, DN_DK))
    v = qkv[..., 2 * qk:].reshape(b, l, h, DN_DV)
    z = p[..., DN_QKV:DN_QKV + vd].reshape(b, l, h, DN_DV)
    o = DN_QKV + vd
    beta = jax.nn.sigmoid(p[..., o:o + 2 * h].astype(jnp.float32)).reshape(b, l, 2, h)
    a = p[..., o + 2 * h:o + 4 * h].astype(jnp.float32).reshape(b, l, 2, h)
    g = -jnp.exp(a_log.astype(jnp.float32)) * jax.nn.softplus(a + dt_bias.astype(jnp.float32))
    return q, k, v, z, g, beta


def gla_features(p, wa2, ba):
    b, l, _ = p.shape
    qk, vd = GLA_HEADS * GLA_DK, GLA_HEADS * GLA_DV
    q = p[..., :qk].reshape(b, l, GLA_HEADS, GLA_DK) * GLA_DK ** -0.5
    k = p[..., qk:2 * qk].reshape(b, l, GLA_HEADS, GLA_DK)
    v = p[..., 2 * qk:2 * qk + vd].reshape(b, l, GLA_HEADS, GLA_DV)
    r = p[..., 2 * qk + vd:2 * qk + 2 * vd].reshape(b, l, GLA_HEADS, GLA_DV)
    lr = p[..., 2 * qk + 2 * vd:].reshape(b, l, 2, GLA_LR)
    pre = jnp.einsum('bldr,drk->bldk', lr, wa2) + ba
    log_a = jax.nn.log_sigmoid(pre.astype(jnp.float32)) / GLA_TAU
    return q, k, v, r, log_a.reshape(b, l, 2, GLA_HEADS, GLA_DK)


def to_col_major(t, rows):
    b = t.shape[0]
    return t.reshape((b, rows, GRID_W) + t.shape[2:]).swapaxes(1, 2).reshape(t.shape)


def from_col_major(t, rows):
    b = t.shape[0]
    return t.reshape((b, GRID_W, rows) + t.shape[2:]).swapaxes(1, 2).reshape(t.shape)


def token_mixer(h_ctx, h_lat, rows, w_in, conv_w, dn_a_log, dn_dt_bias, dn_norm_g,
                gla_wa2, gla_ba, gla_norm_g, w_out):
    b = h_lat.shape[0]
    p_c = h_ctx @ w_in
    p_l = h_lat @ w_in
    qc, kc, vc, zc, gc, bc = dn_features(p_c[..., :DN_COLS], conv_w, dn_a_log, dn_dt_bias)
    ql, kl, vl, zl, gl, bl = dn_features(p_l[..., :DN_COLS], conv_w, dn_a_log, dn_dt_bias)
    s0_dn = jnp.zeros((b, DN_HEADS, DN_DK, DN_DV), jnp.float32)
    _, dn_l = bidirectional(
        gated_delta_scan,
        (qc, kc, vc, gc[:, :, 0], bc[:, :, 0]), (qc, kc, vc, gc[:, :, 1], bc[:, :, 1]),
        (ql, kl, vl, gl[:, :, 0], bl[:, :, 0]), (ql, kl, vl, gl[:, :, 1], bl[:, :, 1]),
        s0_dn, False)
    gqc, gkc, gvc, grc, lac = gla_features(p_c[..., DN_COLS:], gla_wa2, gla_ba)
    gql, gkl, gvl, grl, lal = gla_features(p_l[..., DN_COLS:], gla_wa2, gla_ba)
    col = lambda t: to_col_major(t, rows)
    gql, gkl, gvl, lal = col(gql), col(gkl), col(gvl), col(lal)
    s0_gla = jnp.zeros((b, GLA_HEADS, GLA_DK, GLA_DV), jnp.float32)
    _, gla_l = bidirectional(
        gla_scan,
        (gqc, gkc, gvc, lac[:, :, 0]), (gqc, gkc, gvc, lac[:, :, 1]),
        (gql, gkl, gvl, lal[:, :, 0]), (gql, gkl, gvl, lal[:, :, 1]),
        s0_gla, False)
    gla_l = from_col_major(gla_l, rows)
    return jnp.concatenate([head_norm_gate(dn_l, dn_norm_g, zl),
                            head_norm_gate(gla_l, gla_norm_g, grl)], axis=-1) @ w_out


SC_LANES = 16
SUBLANES = 8
LANES = 128
PICK_GROUP = 32
TOK_STEP = 2
PICKS = PEER_HEADS * PEER_TOPK
GROUPS_PER_TOK = PICKS // PICK_GROUP


def _sc_mesh():
    return plsc.VectorSubcoreMesh(core_axis_name="c", subcore_axis_name="s")


def _sc_pipeline(body, n_steps, in_specs, out_specs, operands):
    pltpu.emit_pipeline(
        body, grid=(n_steps,), in_specs=in_specs, out_specs=out_specs,
        core_axis_name=("c", "s"), dimension_semantics=(pltpu.PARALLEL,),
    )(*operands)


def peer_act_partial_sc(u3, idx2, h3):
    n_groups, n_tok = idx2.shape[0], h3.shape[0]
    nsub = TOK_STEP * GROUPS_PER_TOK
    n_chunks = SUBLANES * LANES // SC_LANES

    @functools.partial(
        pl.kernel, mesh=_sc_mesh(),
        out_type=jax.ShapeDtypeStruct((n_groups, PICK_GROUP * SC_LANES), jnp.float32),
        scratch_types=[pltpu.VMEM((2, PICK_GROUP, SUBLANES, LANES), jnp.float32),
                       pltpu.SemaphoreType.DMA((2,))],
    )
    def k(u_hbm, i_hbm, h_hbm, o_hbm, rows, sems):
        def body(i_v, h_v, o_v):
            def fetch(j, slot):
                return pltpu.make_async_copy(u_hbm.at[i_v.at[j]], rows.at[slot], sems.at[slot])

            fetch(0, 0).start()
            for j in range(nsub):
                slot = j % 2
                if j + 1 < nsub:
                    fetch(j + 1, 1 - slot).start()
                fetch(j, slot).wait()
                t = j // GROUPS_PER_TOK

                zero = jnp.zeros((SC_LANES,), jnp.float32)
                for kk in range(PICK_GROUP):
                    o_v[j, pl.ds(kk * SC_LANES, SC_LANES)] = zero

                def chunk(c, carry, j=j, slot=slot, t=t):
                    s = c // (LANES // SC_LANES)
                    l = (c % (LANES // SC_LANES)) * SC_LANES
                    hc = h_v[t, s, pl.ds(l, SC_LANES)]
                    for kk in range(PICK_GROUP):
                        plsc.addupdate(o_v.at[j, pl.ds(kk * SC_LANES, SC_LANES)],
                                       rows[slot, kk, s, pl.ds(l, SC_LANES)] * hc)
                    return carry

                lax.fori_loop(0, n_chunks, chunk, 0)

        _sc_pipeline(
            body, n_tok // TOK_STEP,
            [pl.BlockSpec((nsub, PICK_GROUP), lambda i: (i, 0)),
             pl.BlockSpec((TOK_STEP, SUBLANES, LANES), lambda i: (i, 0, 0))],
            [pl.BlockSpec((nsub, PICK_GROUP * SC_LANES), lambda i: (i, 0))],
            (i_hbm, h_hbm, o_hbm))

    return k(u3, idx2, h3)


def peer_combine_sc(v3, idx2, coef_b):
    n_groups = idx2.shape[0]
    n_tok = n_groups // GROUPS_PER_TOK
    nsub = TOK_STEP * GROUPS_PER_TOK
    n_chunks = SUBLANES * LANES // SC_LANES

    @functools.partial(
        pl.kernel, mesh=_sc_mesh(),
        out_type=jax.ShapeDtypeStruct((n_tok, SUBLANES, LANES), jnp.float32),
        scratch_types=[pltpu.VMEM((2, PICK_GROUP, SUBLANES, LANES), jnp.float32),
                       pltpu.SemaphoreType.DMA((2,))],
    )
    def k(v_hbm, i_hbm, c_hbm, o_hbm, rows, sems):
        def body(i_v, c_v, o_v):
            def fetch(j, slot):
                return pltpu.make_async_copy(v_hbm.at[i_v.at[j]], rows.at[slot], sems.at[slot])

            fetch(0, 0).start()
            for j in range(nsub):
                slot = j % 2
                if j + 1 < nsub:
                    fetch(j + 1, 1 - slot).start()
                fetch(j, slot).wait()
                t = j // GROUPS_PER_TOK
                first = j % GROUPS_PER_TOK == 0
                cks = [c_v[j, pl.ds(kk * SC_LANES, SC_LANES)] for kk in range(PICK_GROUP)]

                def chunk(c, carry, slot=slot, t=t, first=first, cks=cks):
                    s = c // (LANES // SC_LANES)
                    l = (c % (LANES // SC_LANES)) * SC_LANES
                    parts = [cks[kk] * rows[slot, kk, s, pl.ds(l, SC_LANES)] for kk in range(4)]
                    for kk in range(4, PICK_GROUP):
                        parts[kk % 4] = parts[kk % 4] + cks[kk] * rows[slot, kk, s, pl.ds(l, SC_LANES)]
                    tot = (parts[0] + parts[1]) + (parts[2] + parts[3])
                    if not first:
                        tot = tot + o_v[t, s, pl.ds(l, SC_LANES)]
                    o_v[t, s, pl.ds(l, SC_LANES)] = tot
                    return carry

                lax.fori_loop(0, n_chunks, chunk, 0)

        _sc_pipeline(
            body, n_tok // TOK_STEP,
            [pl.BlockSpec((nsub, PICK_GROUP), lambda i: (i, 0)),
             pl.BlockSpec((nsub, PICK_GROUP * SC_LANES), lambda i: (i, 0))],
            [pl.BlockSpec((TOK_STEP, SUBLANES, LANES), lambda i: (i, 0, 0))],
            (i_hbm, c_hbm, o_hbm))

    return k(v3, idx2, coef_b)


def _segment_matrix():
    r = lax.broadcasted_iota(jnp.int32, (PICK_GROUP * SC_LANES, PICK_GROUP), 0) // SC_LANES
    c = lax.broadcasted_iota(jnp.int32, (PICK_GROUP * SC_LANES, PICK_GROUP), 1)
    return (r == c).astype(jnp.float32)


def _coef_kernel(part_ref, gate_ref, o_ref):
    seg = _segment_matrix()
    act = jnp.dot(part_ref[...], seg, preferred_element_type=jnp.float32,
                  precision=lax.Precision.HIGHEST)
    coef = gate_ref[...] * (0.5 * act * (1.0 + lax.erf(act * (2.0 ** -0.5))))
    o_ref[...] = lax.dot_general(coef, seg, (((1,), (1,)), ((), ())),
                                 preferred_element_type=jnp.float32,
                                 precision=lax.Precision.HIGHEST)


def peer_coef_tc(part, gate2):
    n_groups, width = part.shape
    tile = 1024
    return pl.pallas_call(
        _coef_kernel,
        grid=(n_groups // tile,),
        in_specs=[pl.BlockSpec((tile, width), lambda i: (i, 0)),
                  pl.BlockSpec((tile, PICK_GROUP), lambda i: (i, 0))],
        out_specs=pl.BlockSpec((tile, width), lambda i: (i, 0)),
        out_shape=jax.ShapeDtypeStruct((n_groups, width), jnp.float32),
        name="peer_coef",
    )(part, gate2)


def peer_apply(h, idx, gate, u_tab, v_tab):
    n_tok, d = h.shape
    idx2 = idx.reshape(n_tok * GROUPS_PER_TOK, PICK_GROUP)
    gate2 = gate.reshape(n_tok * GROUPS_PER_TOK, PICK_GROUP)
    u3 = u_tab.reshape(-1, SUBLANES, LANES)
    v3 = v_tab.reshape(-1, SUBLANES, LANES)
    part = peer_act_partial_sc(u3, idx2, h.reshape(n_tok, SUBLANES, LANES))
    coef_b = peer_coef_tc(part, gate2)
    return peer_combine_sc(v3, idx2, coef_b).reshape(n_tok, d)


def peer(h, wq, keys, u_tab, v_tab):
    b, l, d = h.shape
    t = b * l
    hb = h.reshape(t, d)
    q = (hb @ wq).reshape(t, PEER_HEADS, 2, PEER_DQ // 2)
    s = jnp.einsum('thpd,hpkd->thpk', q, keys)
    top_s, top_i = lax.top_k(s, PEER_TOPK)
    cand_s = (top_s[:, :, 0, :, None] + top_s[:, :, 1, None, :]).reshape(t, PEER_HEADS, -1)
    cand_i = (top_i[:, :, 0, :, None] * PEER_NKEYS + top_i[:, :, 1, None, :]).reshape(t, PEER_HEADS, -1)
    best_s, pos = lax.top_k(cand_s, PEER_TOPK)
    idx = jnp.take_along_axis(cand_i, pos, axis=-1)
    gate = jax.nn.softmax(best_s.astype(jnp.float32), axis=-1)
    y = peer_apply(hb, idx.reshape(t, PICKS).astype(jnp.int32), gate.reshape(t, PICKS), u_tab, v_tab)
    return y.reshape(b, l, d)


def _final_norm_kernel(x_ref, g_ref, o_ref):
    xf = x_ref[...]
    y = xf * lax.rsqrt(jnp.mean(xf * xf, axis=-1, keepdims=True) + EPS)
    o_ref[...] = y * g_ref[...]


def kernel(x, c, ctx, c_ctx, w_ada, b_ada, norm1_g, norm2_g, w_in, conv_w, dn_a_log,
           dn_dt_bias, dn_norm_g, gla_wa2, gla_ba, gla_norm_g, w_out, peer_wq, peer_keys,
           peer_u, peer_v, final_g):
    rows = x.shape[1] // GRID_W
    l = 0
    mod_l = (jax.nn.silu(c) @ w_ada[l] + b_ada[l])[:, None, :]
    mod_c = jax.nn.silu(c_ctx) @ w_ada[l] + b_ada[l]
    sh1, sc1, g1, sh2, sc2, g2 = jnp.split(mod_l, 6, axis=-1)
    csh1, csc1, cg1, csh2, csc2, cg2 = jnp.split(mod_c, 6, axis=-1)
    h_lat = modulate(rmsnorm(x, norm1_g[l]), sh1, sc1)
    h_ctx = modulate(rmsnorm(ctx, norm1_g[l]), csh1, csc1)
    y_lat = token_mixer(h_ctx, h_lat, rows, w_in[l], conv_w[l], dn_a_log[l],
                        dn_dt_bias[l], dn_norm_g[l], gla_wa2[l], gla_ba[l],
                        gla_norm_g[l], w_out[l])
    x = x + g1 * y_lat
    x = x + g2 * peer(modulate(rmsnorm(x, norm2_g[l]), sh2, sc2),
                      peer_wq[l], peer_keys[l], peer_u[l], peer_v[l])
    b, sl, d = x.shape
    x2 = x.reshape(b * sl, d)
    tm = 512
    out = pl.pallas_call(
        _final_norm_kernel,
        grid=(b * sl // tm,),
        in_specs=[pl.BlockSpec((tm, d), lambda i: (i, 0)), pl.BlockSpec((1, d), lambda i: (0, 0))],
        out_specs=pl.BlockSpec((tm, d), lambda i: (i, 0)),
        out_shape=jax.ShapeDtypeStruct((b * sl, d), jnp.float32),
    )(x2, final_g.reshape(1, d))
    return out.reshape(b, sl, d)
```

```python
import functools
import math
import jax
import jax.numpy as jnp
from jax import lax
import numpy as np
from jax.experimental import pallas as pl
from jax.experimental.pallas import tpu as pltpu
from jax.experimental.pallas import tpu_sc as plsc

D_MODEL = 1024
GRID_W = 64
CTX_LEN = 256
DN_HEADS = 4
DN_DK = 128
DN_DV = 128
CONV_W = 5
GLA_HEADS = 4
GLA_DK = 64
GLA_DV = 128
GLA_LR = 16
GLA_TAU = 16.0
CHUNK = 64
PEER_HEADS = 8
PEER_NKEYS = 128
PEER_DQ = 256
PEER_TOPK = 16
PEER_BLOCK = 128
EPS = 1e-6
DN_QKV = 2 * DN_HEADS * DN_DK + DN_HEADS * DN_DV
DN_COLS = DN_QKV + DN_HEADS * DN_DV + 4 * DN_HEADS


def rmsnorm(x, g):
    xf = x.astype(jnp.float32)
    y = xf * lax.rsqrt(jnp.mean(xf * xf, axis=-1, keepdims=True) + EPS)
    return (y * g.astype(jnp.float32)).astype(x.dtype)


def modulate(h, shift, scale):
    return h * (1 + scale) + shift


def l2norm(t):
    tf = t.astype(jnp.float32)
    return tf * lax.rsqrt(jnp.sum(tf * tf, axis=-1, keepdims=True) + EPS)


def head_norm_gate(o, g, gate):
    y = o * lax.rsqrt(jnp.mean(o * o, axis=-1, keepdims=True) + EPS) * g.astype(jnp.float32)
    y = y * jax.nn.silu(gate.astype(jnp.float32))
    return y.reshape(y.shape[:2] + (-1,)).astype(gate.dtype)


def short_conv(t, w):
    pad = CONV_W // 2
    return lax.conv_general_dilated(
        t, w[:, None, :].astype(t.dtype), window_strides=(1,), padding=((pad, pad),),
        dimension_numbers=('NWC', 'WIO', 'NWC'), feature_group_count=t.shape[-1])


def _chunk(t, n):
    b, _, h = t.shape[:3]
    t = t.reshape((b, n, CHUNK, h) + t.shape[3:])
    return t.transpose((1, 0, 3, 2) + tuple(range(4, t.ndim)))


def _unchunk(t):
    n, b, h, c, d = t.shape
    return t.transpose(1, 0, 3, 2, 4).reshape(b, n * c, h, d)


def _tril(strict):
    i = jnp.arange(CHUNK)
    return (i[:, None] > i[None, :]) if strict else (i[:, None] >= i[None, :])


def gated_delta_scan(q, k, v, g, beta, s0, with_output):
    f32 = jnp.float32
    n = q.shape[1] // CHUNK
    dv = v.shape[-1]
    q, k, v, g, beta = (_chunk(t.astype(f32), n) for t in (q, k, v, g, beta))
    gc = jnp.cumsum(g, axis=-1)
    decay = jnp.exp(jnp.where(_tril(False), gc[..., :, None] - gc[..., None, :], -jnp.inf))
    kb = k * beta[..., None]
    lower = jnp.where(_tril(True), jnp.einsum('nbhtd,nbhsd->nbhts', kb, k) * decay, 0.0)
    rhs = jnp.concatenate([v * beta[..., None], kb * jnp.exp(gc)[..., None]], axis=-1)
    sol = lax.linalg.triangular_solve(lower, rhs, left_side=True, lower=True, unit_diagonal=True)
    u, w = sol[..., :dv], sol[..., dv:]
    k_dec = k * jnp.exp(gc[..., -1:] - gc)[..., None]
    g_last = jnp.exp(gc[..., -1])

    def step(S, xs):
        u_c, w_c, kd_c, gl_c = xs[:4]
        v_new = u_c - jnp.einsum('bhck,bhkv->bhcv', w_c, S)
        S_next = S * gl_c[..., None, None] + jnp.einsum('bhck,bhcv->bhkv', kd_c, v_new)
        if not with_output:
            return S_next, None
        a_c, qd_c = xs[4:]
        o = jnp.einsum('bhck,bhkv->bhcv', qd_c, S) + jnp.einsum('bhts,bhsv->bhtv', a_c, v_new)
        return S_next, o

    if with_output:
        a_qk = jnp.einsum('nbhtd,nbhsd->nbhts', q, k) * decay
        xs = (u, w, k_dec, g_last, a_qk, q * jnp.exp(gc)[..., None])
    else:
        xs = (u, w, k_dec, g_last)
    S, o = lax.scan(step, s0.astype(f32), xs)
    return (_unchunk(o) if with_output else None), S


def gla_scan(q, k, v, log_a, s0, with_output):
    f32 = jnp.float32
    n = q.shape[1] // CHUNK
    q, k, v, la = (_chunk(t.astype(f32), n) for t in (q, k, v, log_a))
    b = jnp.cumsum(la, axis=-2)
    b_last = b[..., -1, :]
    k_dec = k * jnp.exp(b_last[..., None, :] - b)
    causal = _tril(False)

    def step(S, xs):
        v_c, kd_c, bl_c = xs[:3]
        S_next = S * jnp.exp(bl_c)[..., :, None] + jnp.einsum('bhck,bhcv->bhkv', kd_c, v_c)
        if not with_output:
            return S_next, None
        q_c, k_c, b_c = xs[3:]
        diff = jnp.where(causal[:, :, None], b_c[..., :, None, :] - b_c[..., None, :, :], -jnp.inf)
        att = jnp.einsum('bhtk,bhsk,bhtsk->bhts', q_c, k_c, jnp.exp(diff))
        o = (jnp.einsum('bhck,bhkv->bhcv', q_c * jnp.exp(b_c), S)
             + jnp.einsum('bhts,bhsv->bhtv', att, v_c))
        return S_next, o

    xs = (v, k_dec, b_last, q, k, b) if with_output else (v, k_dec, b_last)
    S, o = lax.scan(step, s0.astype(f32), xs)
    return (_unchunk(o) if with_output else None), S


def bidirectional(scan_fn, ctx_f, ctx_b, lat_f, lat_b, s0, ctx_out):
    rev = lambda args: tuple(jnp.flip(a, axis=1) for a in args)
    oc_f, sc_f = scan_fn(*ctx_f, s0, ctx_out)
    oc_b, sc_b = scan_fn(*rev(ctx_b), s0, ctx_out)
    ol_f, _ = scan_fn(*lat_f, sc_f, True)
    ol_b, _ = scan_fn(*rev(lat_b), sc_b, True)
    o_lat = ol_f + jnp.flip(ol_b, axis=1)
    o_ctx = (oc_f + jnp.flip(oc_b, axis=1)) if ctx_out else None
    return o_ctx, o_lat


def dn_features(p, conv_w, a_log, dt_bias):
    b, l, _ = p.shape
    qk, vd, h = DN_HEADS * DN_DK, DN_HEADS * DN_DV, DN_HEADS
    qkv = jax.nn.silu(short_conv(p[..., :DN_QKV], conv_w))
    q = l2norm(qkv[..., :qk].reshape(b, l, h, DN_DK)) * DN_DK ** -0.5
    k = l2norm(qkv[..., qk:2 * qk].reshape(b, l, h, DN_DK))
    v = qkv[..., 2 * qk:].reshape(b, l, h, DN_DV)
    z = p[..., DN_QKV:DN_QKV + vd].reshape(b, l, h, DN_DV)
    o = DN_QKV + vd
    beta = jax.nn.sigmoid(p[..., o:o + 2 * h].astype(jnp.float32)).reshape(b, l, 2, h)
    a = p[..., o + 2 * h:o + 4 * h].astype(jnp.float32).reshape(b, l, 2, h)
    g = -jnp.exp(a_log.astype(jnp.float32)) * jax.nn.softplus(a + dt_bias.astype(jnp.float32))
    return q, k, v, z, g, beta


def gla_features(p, wa2, ba):
    b, l, _ = p.shape
    qk, vd = GLA_HEADS * GLA_DK, GLA_HEADS * GLA_DV
    q = p[..., :qk].reshape(b, l, GLA_HEADS, GLA_DK) * GLA_DK ** -0.5
    k = p[..., qk:2 * qk].reshape(b, l, GLA_HEADS, GLA_DK)
    v = p[..., 2 * qk:2 * qk + vd].reshape(b, l, GLA_HEADS, GLA_DV)
    r = p[..., 2 * qk + vd:2 * qk + 2 * vd].reshape(b, l, GLA_HEADS, GLA_DV)
    lr = p[..., 2 * qk + 2 * vd:].reshape(b, l, 2, GLA_LR)
    pre = jnp.einsum('bldr,drk->bldk', lr, wa2) + ba
    log_a = jax.nn.log_sigmoid(pre.astype(jnp.float32)) / GLA_TAU
    return q, k, v, r, log_a.reshape(b, l, 2, GLA_HEADS, GLA_DK)


def to_col_major(t, rows):
    b = t.shape[0]
    return t.reshape((b, rows, GRID_W) + t.shape[2:]).swapaxes(1, 2).reshape(t.shape)


def from_col_major(t, rows):
    b = t.shape[0]
    return t.reshape((b, GRID_W, rows) + t.shape[2:]).swapaxes(1, 2).reshape(t.shape)


def token_mixer(h_ctx, h_lat, rows, w_in, conv_w, dn_a_log, dn_dt_bias, dn_norm_g,
                gla_wa2, gla_ba, gla_norm_g, w_out):
    b = h_lat.shape[0]
    p_c = h_ctx @ w_in
    p_l = h_lat @ w_in
    qc, kc, vc, zc, gc, bc = dn_features(p_c[..., :DN_COLS], conv_w, dn_a_log, dn_dt_bias)
    ql, kl, vl, zl, gl, bl = dn_features(p_l[..., :DN_COLS], conv_w, dn_a_log, dn_dt_bias)
    s0_dn = jnp.zeros((b, DN_HEADS, DN_DK, DN_DV), jnp.float32)
    _, dn_l = bidirectional(
        gated_delta_scan,
        (qc, kc, vc, gc[:, :, 0], bc[:, :, 0]), (qc, kc, vc, gc[:, :, 1], bc[:, :, 1]),
        (ql, kl, vl, gl[:, :, 0], bl[:, :, 0]), (ql, kl, vl, gl[:, :, 1], bl[:, :, 1]),
        s0_dn, False)
    gqc, gkc, gvc, grc, lac = gla_features(p_c[..., DN_COLS:], gla_wa2, gla_ba)
    gql, gkl, gvl, grl, lal = gla_features(p_l[..., DN_COLS:], gla_wa2, gla_ba)
    col = lambda t: to_col_major(t, rows)
    gql, gkl, gvl, lal = col(gql), col(gkl), col(gvl), col(lal)
    s0_gla = jnp.zeros((b, GLA_HEADS, GLA_DK, GLA_DV), jnp.float32)
    _, gla_l = bidirectional(
        gla_scan,
        (gqc, gkc, gvc, lac[:, :, 0]), (gqc, gkc, gvc, lac[:, :, 1]),
        (gql, gkl, gvl, lal[:, :, 0]), (gql, gkl, gvl, lal[:, :, 1]),
        s0_gla, False)
    gla_l = from_col_major(gla_l, rows)
    return jnp.concatenate([head_norm_gate(dn_l, dn_norm_g, zl),
                            head_norm_gate(gla_l, gla_norm_g, grl)], axis=-1) @ w_out


SC_LANES = 16
SUBLANES = 8
LANES = 128
PICK_GROUP = 32
TOK_STEP = 2
PICKS = PEER_HEADS * PEER_TOPK
GROUPS_PER_TOK = PICKS // PICK_GROUP
ACT_UNROLL = 8


def _sc_mesh():
    return plsc.VectorSubcoreMesh(core_axis_name="c", subcore_axis_name="s")


def _sc_pipeline(body, n_steps, in_specs, out_specs, operands):
    pltpu.emit_pipeline(
        body, grid=(n_steps,), in_specs=in_specs, out_specs=out_specs,
        core_axis_name=("c", "s"), dimension_semantics=(pltpu.PARALLEL,),
        trace_scopes=False,
    )(*operands)


def peer_act_partial_sc(u3, idx2, h3):
    n_groups, n_tok = idx2.shape[0], h3.shape[0]
    nsub = TOK_STEP * GROUPS_PER_TOK
    n_chunks = SUBLANES * LANES // SC_LANES

    @functools.partial(
        pl.kernel, mesh=_sc_mesh(),
        out_type=jax.ShapeDtypeStruct((n_groups, PICK_GROUP * SC_LANES), jnp.float32),
        scratch_types=[pltpu.VMEM((2, PICK_GROUP, SUBLANES, LANES), jnp.float32),
                       pltpu.SemaphoreType.DMA((2,))],
    )
    def k(u_hbm, i_hbm, h_hbm, o_hbm, rows, sems):
        def body(i_v, h_v, o_v):
            def fetch(j, slot):
                return pltpu.make_async_copy(u_hbm.at[i_v.at[j]], rows.at[slot], sems.at[slot])

            fetch(0, 0).start()

            def sub(j, carry):
                slot = j % 2

                @pl.when(j + 1 < nsub)
                def _():
                    fetch(j + 1, 1 - slot).start()

                fetch(j, slot).wait()
                t = j // GROUPS_PER_TOK

                def picks(g, carry2):
                    kb = g * ACT_UNROLL
                    accs = [None] * ACT_UNROLL
                    for c in range(n_chunks):
                        s, l = divmod(c, LANES // SC_LANES)
                        hc = h_v[t, s, pl.ds(l * SC_LANES, SC_LANES)]
                        for i in range(ACT_UNROLL):
                            p = rows[slot, kb + i, s, pl.ds(l * SC_LANES, SC_LANES)] * hc
                            accs[i] = p if accs[i] is None else accs[i] + p
                    for i in range(ACT_UNROLL):
                        o_v[j, pl.ds((kb + i) * SC_LANES, SC_LANES)] = accs[i]
                    return carry2

                lax.fori_loop(0, PICK_GROUP // ACT_UNROLL, picks, 0)
                return carry

            lax.fori_loop(0, nsub, sub, 0)

        _sc_pipeline(
            body, n_tok // TOK_STEP,
            [pl.BlockSpec((nsub, PICK_GROUP), lambda i: (i, 0)),
             pl.BlockSpec((TOK_STEP, SUBLANES, LANES), lambda i: (i, 0, 0))],
            [pl.BlockSpec((nsub, PICK_GROUP * SC_LANES), lambda i: (i, 0))],
            (i_hbm, h_hbm, o_hbm))

    return k(u3, idx2, h3)


def peer_combine_sc(v3, idx2, coef_b):
    n_groups = idx2.shape[0]
    n_tok = n_groups // GROUPS_PER_TOK
    nsub = TOK_STEP * GROUPS_PER_TOK
    n_chunks = SUBLANES * LANES // SC_LANES

    @functools.partial(
        pl.kernel, mesh=_sc_mesh(),
        out_type=jax.ShapeDtypeStruct((n_tok, SUBLANES, LANES), jnp.float32),
        scratch_types=[pltpu.VMEM((2, PICK_GROUP, SUBLANES, LANES), jnp.float32),
                       pltpu.SemaphoreType.DMA((2,))],
    )
    def k(v_hbm, i_hbm, c_hbm, o_hbm, rows, sems):
        def body(i_v, c_v, o_v):
            def fetch(j, slot):
                return pltpu.make_async_copy(v_hbm.at[i_v.at[j]], rows.at[slot], sems.at[slot])

            fetch(0, 0).start()
            for j in range(nsub):
                slot = j % 2
                if j + 1 < nsub:
                    fetch(j + 1, 1 - slot).start()
                fetch(j, slot).wait()
                t = j // GROUPS_PER_TOK
                first = j % GROUPS_PER_TOK == 0
                cks = [c_v[j, pl.ds(kk * SC_LANES, SC_LANES)] for kk in range(PICK_GROUP)]

                def chunk(c, carry, slot=slot, t=t, first=first, cks=cks):
                    s = c // (LANES // SC_LANES)
                    l = (c % (LANES // SC_LANES)) * SC_LANES
                    parts = [cks[kk] * rows[slot, kk, s, pl.ds(l, SC_LANES)] for kk in range(4)]
                    for kk in range(4, PICK_GROUP):
                        parts[kk % 4] = parts[kk % 4] + cks[kk] * rows[slot, kk, s, pl.ds(l, SC_LANES)]
                    tot = (parts[0] + parts[1]) + (parts[2] + parts[3])
                    if not first:
                        tot = tot + o_v[t, s, pl.ds(l, SC_LANES)]
                    o_v[t, s, pl.ds(l, SC_LANES)] = tot
                    return carry

                lax.fori_loop(0, n_chunks, chunk, 0)

        _sc_pipeline(
            body, n_tok // TOK_STEP,
            [pl.BlockSpec((nsub, PICK_GROUP), lambda i: (i, 0)),
             pl.BlockSpec((nsub, PICK_GROUP * SC_LANES), lambda i: (i, 0))],
            [pl.BlockSpec((TOK_STEP, SUBLANES, LANES), lambda i: (i, 0, 0))],
            (i_hbm, c_hbm, o_hbm))

    return k(v3, idx2, coef_b)


def _segment_matrix():
    r = lax.broadcasted_iota(jnp.int32, (PICK_GROUP * SC_LANES, PICK_GROUP), 0) // SC_LANES
    c = lax.broadcasted_iota(jnp.int32, (PICK_GROUP * SC_LANES, PICK_GROUP), 1)
    return (r == c).astype(jnp.float32)


def _coef_kernel(part_ref, gate_ref, o_ref):
    seg = _segment_matrix()
    act = jnp.dot(part_ref[...], seg, preferred_element_type=jnp.float32,
                  precision=lax.Precision.HIGHEST)
    coef = gate_ref[...] * (0.5 * act * (1.0 + lax.erf(act * (2.0 ** -0.5))))
    o_ref[...] = lax.dot_general(coef, seg, (((1,), (1,)), ((), ())),
                                 preferred_element_type=jnp.float32,
                                 precision=lax.Precision.HIGHEST)


def peer_coef_tc(part, gate2):
    n_groups, width = part.shape
    tile = 1024
    return pl.pallas_call(
        _coef_kernel,
        grid=(n_groups // tile,),
        in_specs=[pl.BlockSpec((tile, width), lambda i: (i, 0)),
                  pl.BlockSpec((tile, PICK_GROUP), lambda i: (i, 0))],
        out_specs=pl.BlockSpec((tile, width), lambda i: (i, 0)),
        out_shape=jax.ShapeDtypeStruct((n_groups, width), jnp.float32),
        name="peer_coef",
    )(part, gate2)


def peer_apply(h, idx, gate, u_tab, v_tab):
    n_tok, d = h.shape
    idx2 = idx.reshape(n_tok * GROUPS_PER_TOK, PICK_GROUP)
    gate2 = gate.reshape(n_tok * GROUPS_PER_TOK, PICK_GROUP)
    u3 = u_tab.reshape(-1, SUBLANES, LANES)
    v3 = v_tab.reshape(-1, SUBLANES, LANES)
    part = peer_act_partial_sc(u3, idx2, h.reshape(n_tok, SUBLANES, LANES))
    coef_b = peer_coef_tc(part, gate2)
    return peer_combine_sc(v3, idx2, coef_b).reshape(n_tok, d)


SELECT_TILE = 256


def _top_rows(s, k, payload=None):
    n = s.shape[0]
    row = lax.broadcasted_iota(jnp.int32, s.shape, 0).astype(jnp.float32)
    vals, picked = [], []
    for _ in range(k):
        m = jnp.max(s, axis=0, keepdims=True)
        first = jnp.min(jnp.where(s == m, row, float(n)), axis=0, keepdims=True)
        sel = row == first
        vals.append(m)
        if payload is None:
            picked.append(first)
        else:
            picked.append(jnp.max(jnp.where(sel, payload, -1.0), axis=0, keepdims=True))
        s = jnp.where(sel, -jnp.inf, s)
    return jnp.concatenate(vals, axis=0), jnp.concatenate(picked, axis=0)


def _select_kernel(q_ref, k_ref, idx_ref, gate_ref):
    half = PEER_DQ // 2
    tops = []
    for p in range(2):
        s = lax.dot_general(k_ref[0, p], q_ref[:, p * half:(p + 1) * half],
                            (((1,), (1,)), ((), ())), preferred_element_type=jnp.float32,
                            precision=lax.Precision.HIGHEST)
        tops.append(_top_rows(s, PEER_TOPK))
    (s0, i0), (s1, i1) = tops
    cand_s = jnp.concatenate([s0[i:i + 1] + s1 for i in range(PEER_TOPK)], axis=0)
    cand_i = jnp.concatenate([i0[i:i + 1] * float(PEER_NKEYS) + i1 for i in range(PEER_TOPK)], axis=0)
    best_s, idx = _top_rows(cand_s, PEER_TOPK, payload=cand_i)
    e = jnp.exp(best_s - best_s[0:1])
    idx_ref[0, 0] = idx.astype(jnp.int32)
    gate_ref[0, 0] = e / jnp.sum(e, axis=0, keepdims=True)


def peer_select_tc(q, keys):
    n_tok = q.shape[0]
    n_tiles = n_tok // SELECT_TILE
    out_sds = lambda dt: jax.ShapeDtypeStruct((n_tiles, PEER_HEADS, PEER_TOPK, SELECT_TILE), dt)
    out_spec = pl.BlockSpec((1, 1, PEER_TOPK, SELECT_TILE), lambda i, h: (i, h, 0, 0))
    idx_t, gate_t = pl.pallas_call(
        _select_kernel,
        grid=(n_tiles, PEER_HEADS),
        in_specs=[pl.BlockSpec((SELECT_TILE, PEER_DQ), lambda i, h: (i, h)),
                  pl.BlockSpec((1, 2, PEER_NKEYS, PEER_DQ // 2), lambda i, h: (h, 0, 0, 0))],
        out_specs=[out_spec, out_spec],
        out_shape=[out_sds(jnp.int32), out_sds(jnp.float32)],
        name="peer_select",
    )(q, keys)
    to_tok = lambda a: a.transpose(0, 3, 1, 2).reshape(n_tok, PICKS)
    return to_tok(idx_t), to_tok(gate_t)


def peer(h, wq, keys, u_tab, v_tab):
    b, l, d = h.shape
    t = b * l
    hb = h.reshape(t, d)
    idx, gate = peer_select_tc(hb @ wq, keys)
    y = peer_apply(hb, idx, gate, u_tab, v_tab)
    return y.reshape(b, l, d)


def _final_norm_kernel(x_ref, g_ref, o_ref):
    xf = x_ref[...]
    y = xf * lax.rsqrt(jnp.mean(xf * xf, axis=-1, keepdims=True) + EPS)
    o_ref[...] = y * g_ref[...]


def kernel(x, c, ctx, c_ctx, w_ada, b_ada, norm1_g, norm2_g, w_in, conv_w, dn_a_log,
           dn_dt_bias, dn_norm_g, gla_wa2, gla_ba, gla_norm_g, w_out, peer_wq, peer_keys,
           peer_u, peer_v, final_g):
    rows = x.shape[1] // GRID_W
    l = 0
    mod_l = (jax.nn.silu(c) @ w_ada[l] + b_ada[l])[:, None, :]
    mod_c = jax.nn.silu(c_ctx) @ w_ada[l] + b_ada[l]
    sh1, sc1, g1, sh2, sc2, g2 = jnp.split(mod_l, 6, axis=-1)
    csh1, csc1, cg1, csh2, csc2, cg2 = jnp.split(mod_c, 6, axis=-1)
    h_lat = modulate(rmsnorm(x, norm1_g[l]), sh1, sc1)
    h_ctx = modulate(rmsnorm(ctx, norm1_g[l]), csh1, csc1)
    y_lat = token_mixer(h_ctx, h_lat, rows, w_in[l], conv_w[l], dn_a_log[l],
                        dn_dt_bias[l], dn_norm_g[l], gla_wa2[l], gla_ba[l],
                        gla_norm_g[l], w_out[l])
    x = x + g1 * y_lat
    x = x + g2 * peer(modulate(rmsnorm(x, norm2_g[l]), sh2, sc2),
                      peer_wq[l], peer_keys[l], peer_u[l], peer_v[l])
    b, sl, d = x.shape
    x2 = x.reshape(b * sl, d)
    tm = 512
    out = pl.pallas_call(
        _final_norm_kernel,
        grid=(b * sl // tm,),
        in_specs=[pl.BlockSpec((tm, d), lambda i: (i, 0)), pl.BlockSpec((1, d), lambda i: (0, 0))],
        out_specs=pl.BlockSpec((tm, d), lambda i: (i, 0)),
        out_shape=jax.ShapeDtypeStruct((b * sl, d), jnp.float32),
    )(x2, final_g.reshape(1, d))
    return out.reshape(b, sl, d)
```

```python
import functools
import math
import jax
import jax.numpy as jnp
from jax import lax
import numpy as np
from jax.experimental import pallas as pl
from jax.experimental.pallas import tpu as pltpu
from jax.experimental.pallas import tpu_sc as plsc

D_MODEL = 1024
GRID_W = 64
CTX_LEN = 256
DN_HEADS = 4
DN_DK = 128
DN_DV = 128
CONV_W = 5
GLA_HEADS = 4
GLA_DK = 64
GLA_DV = 128
GLA_LR = 16
GLA_TAU = 16.0
CHUNK = 64
PEER_HEADS = 8
PEER_NKEYS = 128
PEER_DQ = 256
PEER_TOPK = 16
PEER_BLOCK = 128
EPS = 1e-6
DN_QKV = 2 * DN_HEADS * DN_DK + DN_HEADS * DN_DV
DN_COLS = DN_QKV + DN_HEADS * DN_DV + 4 * DN_HEADS


def rmsnorm(x, g):
    xf = x.astype(jnp.float32)
    y = xf * lax.rsqrt(jnp.mean(xf * xf, axis=-1, keepdims=True) + EPS)
    return (y * g.astype(jnp.float32)).astype(x.dtype)


def modulate(h, shift, scale):
    return h * (1 + scale) + shift


def l2norm(t):
    tf = t.astype(jnp.float32)
    return tf * lax.rsqrt(jnp.sum(tf * tf, axis=-1, keepdims=True) + EPS)


def head_norm_gate(o, g, gate):
    y = o * lax.rsqrt(jnp.mean(o * o, axis=-1, keepdims=True) + EPS) * g.astype(jnp.float32)
    y = y * jax.nn.silu(gate.astype(jnp.float32))
    return y.reshape(y.shape[:2] + (-1,)).astype(gate.dtype)


def short_conv(t, w):
    pad = CONV_W // 2
    return lax.conv_general_dilated(
        t, w[:, None, :].astype(t.dtype), window_strides=(1,), padding=((pad, pad),),
        dimension_numbers=('NWC', 'WIO', 'NWC'), feature_group_count=t.shape[-1])


def _chunk(t, n):
    b, _, h = t.shape[:3]
    t = t.reshape((b, n, CHUNK, h) + t.shape[3:])
    return t.transpose((1, 0, 3, 2) + tuple(range(4, t.ndim)))


def _unchunk(t):
    n, b, h, c, d = t.shape
    return t.transpose(1, 0, 3, 2, 4).reshape(b, n * c, h, d)


def _tril(strict):
    i = jnp.arange(CHUNK)
    return (i[:, None] > i[None, :]) if strict else (i[:, None] >= i[None, :])


def gated_delta_scan(q, k, v, g, beta, s0, with_output):
    f32 = jnp.float32
    n = q.shape[1] // CHUNK
    dv = v.shape[-1]
    q, k, v, g, beta = (_chunk(t.astype(f32), n) for t in (q, k, v, g, beta))
    gc = jnp.cumsum(g, axis=-1)
    decay = jnp.exp(jnp.where(_tril(False), gc[..., :, None] - gc[..., None, :], -jnp.inf))
    kb = k * beta[..., None]
    lower = jnp.where(_tril(True), jnp.einsum('nbhtd,nbhsd->nbhts', kb, k) * decay, 0.0)
    rhs = jnp.concatenate([v * beta[..., None], kb * jnp.exp(gc)[..., None]], axis=-1)
    sol = lax.linalg.triangular_solve(lower, rhs, left_side=True, lower=True, unit_diagonal=True)
    u, w = sol[..., :dv], sol[..., dv:]
    k_dec = k * jnp.exp(gc[..., -1:] - gc)[..., None]
    g_last = jnp.exp(gc[..., -1])

    def step(S, xs):
        u_c, w_c, kd_c, gl_c = xs[:4]
        v_new = u_c - jnp.einsum('bhck,bhkv->bhcv', w_c, S)
        S_next = S * gl_c[..., None, None] + jnp.einsum('bhck,bhcv->bhkv', kd_c, v_new)
        if not with_output:
            return S_next, None
        a_c, qd_c = xs[4:]
        o = jnp.einsum('bhck,bhkv->bhcv', qd_c, S) + jnp.einsum('bhts,bhsv->bhtv', a_c, v_new)
        return S_next, o

    if with_output:
        a_qk = jnp.einsum('nbhtd,nbhsd->nbhts', q, k) * decay
        xs = (u, w, k_dec, g_last, a_qk, q * jnp.exp(gc)[..., None])
    else:
        xs = (u, w, k_dec, g_last)
    S, o = lax.scan(step, s0.astype(f32), xs)
    return (_unchunk(o) if with_output else None), S


def gla_scan(q, k, v, log_a, s0, with_output):
    f32 = jnp.float32
    n = q.shape[1] // CHUNK
    q, k, v, la = (_chunk(t.astype(f32), n) for t in (q, k, v, log_a))
    b = jnp.cumsum(la, axis=-2)
    b_last = b[..., -1, :]
    k_dec = k * jnp.exp(b_last[..., None, :] - b)
    causal = _tril(False)

    def step(S, xs):
        v_c, kd_c, bl_c = xs[:3]
        S_next = S * jnp.exp(bl_c)[..., :, None] + jnp.einsum('bhck,bhcv->bhkv', kd_c, v_c)
        if not with_output:
            return S_next, None
        q_c, k_c, b_c = xs[3:]
        diff = jnp.where(causal[:, :, None], b_c[..., :, None, :] - b_c[..., None, :, :], -jnp.inf)
        att = jnp.einsum('bhtk,bhsk,bhtsk->bhts', q_c, k_c, jnp.exp(diff))
        o = (jnp.einsum('bhck,bhkv->bhcv', q_c * jnp.exp(b_c), S)
             + jnp.einsum('bhts,bhsv->bhtv', att, v_c))
        return S_next, o

    xs = (v, k_dec, b_last, q, k, b) if with_output else (v, k_dec, b_last)
    S, o = lax.scan(step, s0.astype(f32), xs)
    return (_unchunk(o) if with_output else None), S


def bidirectional(scan_fn, ctx_f, ctx_b, lat_f, lat_b, s0, ctx_out):
    rev = lambda args: tuple(jnp.flip(a, axis=1) for a in args)
    oc_f, sc_f = scan_fn(*ctx_f, s0, ctx_out)
    oc_b, sc_b = scan_fn(*rev(ctx_b), s0, ctx_out)
    ol_f, _ = scan_fn(*lat_f, sc_f, True)
    ol_b, _ = scan_fn(*rev(lat_b), sc_b, True)
    o_lat = ol_f + jnp.flip(ol_b, axis=1)
    o_ctx = (oc_f + jnp.flip(oc_b, axis=1)) if ctx_out else None
    return o_ctx, o_lat


def dn_features(p, conv_w, a_log, dt_bias):
    b, l, _ = p.shape
    qk, vd, h = DN_HEADS * DN_DK, DN_HEADS * DN_DV, DN_HEADS
    qkv = jax.nn.silu(short_conv(p[..., :DN_QKV], conv_w))
    q = l2norm(qkv[..., :qk].reshape(b, l, h, DN_DK)) * DN_DK ** -0.5
    k = l2norm(qkv[..., qk:2 * qk].reshape(b, l, h, DN_DK))
    v = qkv[..., 2 * qk:].reshape(b, l, h, DN_DV)
    z = p[..., DN_QKV:DN_QKV + vd].reshape(b, l, h, DN_DV)
    o = DN_QKV + vd
    beta = jax.nn.sigmoid(p[..., o:o + 2 * h].astype(jnp.float32)).reshape(b, l, 2, h)
    a = p[..., o + 2 * h:o + 4 * h].astype(jnp.float32).reshape(b, l, 2, h)
    g = -jnp.exp(a_log.astype(jnp.float32)) * jax.nn.softplus(a + dt_bias.astype(jnp.float32))
    return q, k, v, z, g, beta


def gla_features(p, wa2, ba):
    b, l, _ = p.shape
    qk, vd = GLA_HEADS * GLA_DK, GLA_HEADS * GLA_DV
    q = p[..., :qk].reshape(b, l, GLA_HEADS, GLA_DK) * GLA_DK ** -0.5
    k = p[..., qk:2 * qk].reshape(b, l, GLA_HEADS, GLA_DK)
    v = p[..., 2 * qk:2 * qk + vd].reshape(b, l, GLA_HEADS, GLA_DV)
    r = p[..., 2 * qk + vd:2 * qk + 2 * vd].reshape(b, l, GLA_HEADS, GLA_DV)
    lr = p[..., 2 * qk + 2 * vd:].reshape(b, l, 2, GLA_LR)
    pre = jnp.einsum('bldr,drk->bldk', lr, wa2) + ba
    log_a = jax.nn.log_sigmoid(pre.astype(jnp.float32)) / GLA_TAU
    return q, k, v, r, log_a.reshape(b, l, 2, GLA_HEADS, GLA_DK)


def to_col_major(t, rows):
    b = t.shape[0]
    return t.reshape((b, rows, GRID_W) + t.shape[2:]).swapaxes(1, 2).reshape(t.shape)


def from_col_major(t, rows):
    b = t.shape[0]
    return t.reshape((b, GRID_W, rows) + t.shape[2:]).swapaxes(1, 2).reshape(t.shape)


def token_mixer(h_ctx, h_lat, rows, w_in, conv_w, dn_a_log, dn_dt_bias, dn_norm_g,
                gla_wa2, gla_ba, gla_norm_g, w_out):
    b = h_lat.shape[0]
    p_c = h_ctx @ w_in
    p_l = h_lat @ w_in
    qc, kc, vc, zc, gc, bc = dn_features(p_c[..., :DN_COLS], conv_w, dn_a_log, dn_dt_bias)
    ql, kl, vl, zl, gl, bl = dn_features(p_l[..., :DN_COLS], conv_w, dn_a_log, dn_dt_bias)
    s0_dn = jnp.zeros((b, DN_HEADS, DN_DK, DN_DV), jnp.float32)
    _, dn_l = bidirectional(
        gated_delta_scan,
        (qc, kc, vc, gc[:, :, 0], bc[:, :, 0]), (qc, kc, vc, gc[:, :, 1], bc[:, :, 1]),
        (ql, kl, vl, gl[:, :, 0], bl[:, :, 0]), (ql, kl, vl, gl[:, :, 1], bl[:, :, 1]),
        s0_dn, False)
    gqc, gkc, gvc, grc, lac = gla_features(p_c[..., DN_COLS:], gla_wa2, gla_ba)
    gql, gkl, gvl, grl, lal = gla_features(p_l[..., DN_COLS:], gla_wa2, gla_ba)
    col = lambda t: to_col_major(t, rows)
    gql, gkl, gvl, lal = col(gql), col(gkl), col(gvl), col(lal)
    s0_gla = jnp.zeros((b, GLA_HEADS, GLA_DK, GLA_DV), jnp.float32)
    _, gla_l = bidirectional(
        gla_scan,
        (gqc, gkc, gvc, lac[:, :, 0]), (gqc, gkc, gvc, lac[:, :, 1]),
        (gql, gkl, gvl, lal[:, :, 0]), (gql, gkl, gvl, lal[:, :, 1]),
        s0_gla, False)
    gla_l = from_col_major(gla_l, rows)
    return jnp.concatenate([head_norm_gate(dn_l, dn_norm_g, zl),
                            head_norm_gate(gla_l, gla_norm_g, grl)], axis=-1) @ w_out


SC_LANES = 16
SUBLANES = 8
LANES = 128
PICK_GROUP = 32
TOK_STEP = 2
PICKS = PEER_HEADS * PEER_TOPK
GROUPS_PER_TOK = PICKS // PICK_GROUP
ACT_UNROLL = 8


def _sc_mesh():
    return plsc.VectorSubcoreMesh(core_axis_name="c", subcore_axis_name="s")


def _sc_pipeline(body, n_steps, in_specs, out_specs, operands):
    pltpu.emit_pipeline(
        body, grid=(n_steps,), in_specs=in_specs, out_specs=out_specs,
        core_axis_name=("c", "s"), dimension_semantics=(pltpu.PARALLEL,),
        trace_scopes=False,
    )(*operands)


def peer_act_partial_sc(u3, idx2, h3):
    n_groups, n_tok = idx2.shape[0], h3.shape[0]
    nsub = TOK_STEP * GROUPS_PER_TOK
    n_chunks = SUBLANES * LANES // SC_LANES

    @functools.partial(
        pl.kernel, mesh=_sc_mesh(),
        out_type=jax.ShapeDtypeStruct((n_groups, PICK_GROUP * SC_LANES), jnp.float32),
        scratch_types=[pltpu.VMEM((2, PICK_GROUP, SUBLANES, LANES), jnp.float32),
                       pltpu.SemaphoreType.DMA((2,))],
    )
    def k(u_hbm, i_hbm, h_hbm, o_hbm, rows, sems):
        def body(i_v, h_v, o_v):
            def fetch(j, slot):
                return pltpu.make_async_copy(u_hbm.at[i_v.at[j]], rows.at[slot], sems.at[slot])

            fetch(0, 0).start()

            def sub(j, carry):
                slot = j % 2

                @pl.when(j + 1 < nsub)
                def _():
                    fetch(j + 1, 1 - slot).start()

                fetch(j, slot).wait()
                t = j // GROUPS_PER_TOK

                def picks(g, carry2):
                    kb = g * ACT_UNROLL
                    accs = [None] * ACT_UNROLL
                    for c in range(n_chunks):
                        s, l = divmod(c, LANES // SC_LANES)
                        hc = h_v[t, s, pl.ds(l * SC_LANES, SC_LANES)]
                        for i in range(ACT_UNROLL):
                            p = rows[slot, kb + i, s, pl.ds(l * SC_LANES, SC_LANES)] * hc
                            accs[i] = p if accs[i] is None else accs[i] + p
                    for i in range(ACT_UNROLL):
                        o_v[j, pl.ds((kb + i) * SC_LANES, SC_LANES)] = accs[i]
                    return carry2

                lax.fori_loop(0, PICK_GROUP // ACT_UNROLL, picks, 0)
                return carry

            lax.fori_loop(0, nsub, sub, 0)

        _sc_pipeline(
            body, n_tok // TOK_STEP,
            [pl.BlockSpec((nsub, PICK_GROUP), lambda i: (i, 0)),
             pl.BlockSpec((TOK_STEP, SUBLANES, LANES), lambda i: (i, 0, 0))],
            [pl.BlockSpec((nsub, PICK_GROUP * SC_LANES), lambda i: (i, 0))],
            (i_hbm, h_hbm, o_hbm))

    return k(u3, idx2, h3)


def peer_combine_sc(v3, idx2, coef_b):
    n_groups = idx2.shape[0]
    n_tok = n_groups // GROUPS_PER_TOK
    nsub = TOK_STEP * GROUPS_PER_TOK
    n_chunks = SUBLANES * LANES // SC_LANES

    @functools.partial(
        pl.kernel, mesh=_sc_mesh(),
        out_type=jax.ShapeDtypeStruct((n_tok, SUBLANES, LANES), jnp.float32),
        scratch_types=[pltpu.VMEM((2, PICK_GROUP, SUBLANES, LANES), jnp.float32),
                       pltpu.SemaphoreType.DMA((2,))],
    )
    def k(v_hbm, i_hbm, c_hbm, o_hbm, rows, sems):
        def body(i_v, c_v, o_v):
            def fetch(j, slot):
                return pltpu.make_async_copy(v_hbm.at[i_v.at[j]], rows.at[slot], sems.at[slot])

            fetch(0, 0).start()
            for j in range(nsub):
                slot = j % 2
                if j + 1 < nsub:
                    fetch(j + 1, 1 - slot).start()
                fetch(j, slot).wait()
                t = j // GROUPS_PER_TOK
                first = j % GROUPS_PER_TOK == 0
                cks = [c_v[j, pl.ds(kk * SC_LANES, SC_LANES)] for kk in range(PICK_GROUP)]

                def chunk(c, carry, slot=slot, t=t, first=first, cks=cks):
                    s = c // (LANES // SC_LANES)
                    l = (c % (LANES // SC_LANES)) * SC_LANES
                    parts = [cks[kk] * rows[slot, kk, s, pl.ds(l, SC_LANES)] for kk in range(4)]
                    for kk in range(4, PICK_GROUP):
                        parts[kk % 4] = parts[kk % 4] + cks[kk] * rows[slot, kk, s, pl.ds(l, SC_LANES)]
                    tot = (parts[0] + parts[1]) + (parts[2] + parts[3])
                    if not first:
                        tot = tot + o_v[t, s, pl.ds(l, SC_LANES)]
                    o_v[t, s, pl.ds(l, SC_LANES)] = tot
                    return carry

                lax.fori_loop(0, n_chunks, chunk, 0)

        _sc_pipeline(
            body, n_tok // TOK_STEP,
            [pl.BlockSpec((nsub, PICK_GROUP), lambda i: (i, 0)),
             pl.BlockSpec((nsub, PICK_GROUP * SC_LANES), lambda i: (i, 0))],
            [pl.BlockSpec((TOK_STEP, SUBLANES, LANES), lambda i: (i, 0, 0))],
            (i_hbm, c_hbm, o_hbm))

    return k(v3, idx2, coef_b)


def _segment_matrix():
    r = lax.broadcasted_iota(jnp.int32, (PICK_GROUP * SC_LANES, PICK_GROUP), 0) // SC_LANES
    c = lax.broadcasted_iota(jnp.int32, (PICK_GROUP * SC_LANES, PICK_GROUP), 1)
    return (r == c).astype(jnp.float32)


def _coef_kernel(part_ref, gate_ref, o_ref):
    seg = _segment_matrix()
    act = jnp.dot(part_ref[...], seg, preferred_element_type=jnp.float32,
                  precision=lax.Precision.HIGHEST)
    coef = gate_ref[...] * (0.5 * act * (1.0 + lax.erf(act * (2.0 ** -0.5))))
    o_ref[...] = lax.dot_general(coef, seg, (((1,), (1,)), ((), ())),
                                 preferred_element_type=jnp.float32,
                                 precision=lax.Precision.HIGHEST)


def peer_coef_tc(part, gate2):
    n_groups, width = part.shape
    tile = 1024
    return pl.pallas_call(
        _coef_kernel,
        grid=(n_groups // tile,),
        in_specs=[pl.BlockSpec((tile, width), lambda i: (i, 0)),
                  pl.BlockSpec((tile, PICK_GROUP), lambda i: (i, 0))],
        out_specs=pl.BlockSpec((tile, width), lambda i: (i, 0)),
        out_shape=jax.ShapeDtypeStruct((n_groups, width), jnp.float32),
        name="peer_coef",
    )(part, gate2)


def peer_apply(h, idx, gate, u_tab, v_tab):
    n_tok, d = h.shape
    idx2 = idx.reshape(n_tok * GROUPS_PER_TOK, PICK_GROUP)
    gate2 = gate.reshape(n_tok * GROUPS_PER_TOK, PICK_GROUP)
    u3 = u_tab.reshape(-1, SUBLANES, LANES)
    v3 = v_tab.reshape(-1, SUBLANES, LANES)
    part = peer_act_partial_sc(u3, idx2, h.reshape(n_tok, SUBLANES, LANES))
    coef_b = peer_coef_tc(part, gate2)
    return peer_combine_sc(v3, idx2, coef_b).reshape(n_tok, d)


SELECT_TILE = 256


def _top_rows(s, k, payload=None):
    n = s.shape[0]
    row = lax.broadcasted_iota(jnp.int32, s.shape, 0).astype(jnp.float32)
    vals, picked = [], []
    for _ in range(k):
        m = jnp.max(s, axis=0, keepdims=True)
        first = jnp.min(jnp.where(s == m, row, float(n)), axis=0, keepdims=True)
        sel = row == first
        vals.append(m)
        if payload is None:
            picked.append(first)
        else:
            picked.append(jnp.max(jnp.where(sel, payload, -1.0), axis=0, keepdims=True))
        s = jnp.where(sel, -jnp.inf, s)
    return jnp.concatenate(vals, axis=0), jnp.concatenate(picked, axis=0)


def _select_kernel(q_ref, k_ref, idx_ref, gate_ref):
    half = PEER_DQ // 2
    tops = []
    for p in range(2):
        s = lax.dot_general(k_ref[0, p], q_ref[:, p * half:(p + 1) * half],
                            (((1,), (1,)), ((), ())), preferred_element_type=jnp.float32,
                            precision=lax.Precision.HIGHEST)
        tops.append(_top_rows(s, PEER_TOPK))
    (s0, i0), (s1, i1) = tops
    cand_s = jnp.concatenate([s0[i:i + 1] + s1 for i in range(PEER_TOPK)], axis=0)
    cand_i = jnp.concatenate([i0[i:i + 1] * float(PEER_NKEYS) + i1 for i in range(PEER_TOPK)], axis=0)
    best_s, idx = _top_rows(cand_s, PEER_TOPK, payload=cand_i)
    e = jnp.exp(best_s - best_s[0:1])
    idx_ref[0, 0] = idx.astype(jnp.int32)
    gate_ref[0, 0] = e / jnp.sum(e, axis=0, keepdims=True)


def peer_select_tc(q, keys):
    n_tok = q.shape[0]
    n_tiles = n_tok // SELECT_TILE
    out_sds = lambda dt: jax.ShapeDtypeStruct((n_tiles, PEER_HEADS, PEER_TOPK, SELECT_TILE), dt)
    out_spec = pl.BlockSpec((1, 1, PEER_TOPK, SELECT_TILE), lambda i, h: (i, h, 0, 0))
    idx_t, gate_t = pl.pallas_call(
        _select_kernel,
        grid=(n_tiles, PEER_HEADS),
        in_specs=[pl.BlockSpec((SELECT_TILE, PEER_DQ), lambda i, h: (i, h)),
                  pl.BlockSpec((1, 2, PEER_NKEYS, PEER_DQ // 2), lambda i, h: (h, 0, 0, 0))],
        out_specs=[out_spec, out_spec],
        out_shape=[out_sds(jnp.int32), out_sds(jnp.float32)],
        name="peer_select",
    )(q, keys)
    to_tok = lambda a: a.transpose(0, 3, 1, 2).reshape(n_tok, PICKS)
    return to_tok(idx_t), to_tok(gate_t)


def peer(h, wq, keys, u_tab, v_tab):
    b, l, d = h.shape
    t = b * l
    hb = h.reshape(t, d)
    idx, gate = peer_select_tc(hb @ wq, keys)
    y = peer_apply(hb, idx, gate, u_tab, v_tab)
    return y.reshape(b, l, d)


TOK_TILE = 256
VMEM_LIMIT_BYTES = 48 * 1024 * 1024
F32 = jnp.float32
BF16 = jnp.bfloat16


def _cparams(*semantics):
    return pltpu.CompilerParams(dimension_semantics=semantics, vmem_limit_bytes=VMEM_LIMIT_BYTES)


def _dot(a, b):
    return jnp.dot(a.astype(BF16), b.astype(BF16), preferred_element_type=F32)


def _dot_nt(a, b):
    return lax.dot_general(a.astype(BF16), b.astype(BF16), (((1,), (1,)), ((), ())),
                           preferred_element_type=F32)


def _dot_tn(a, b):
    return lax.dot_general(a.astype(BF16), b.astype(BF16), (((0,), (0,)), ((), ())),
                           preferred_element_type=F32)


def _split(x):
    hi = x.astype(BF16)
    return hi, (x - hi.astype(F32)).astype(BF16)


def _mask_dot(mask_bf16, x):
    hi, lo = _split(x)
    return (jnp.dot(mask_bf16, hi, preferred_element_type=F32)
            + jnp.dot(mask_bf16, lo, preferred_element_type=F32))


def _softplus(x):
    return jnp.maximum(x, 0.0) + jnp.log(1.0 + jnp.exp(-jnp.abs(x)))


def _tri_masks(rev):
    r = lax.broadcasted_iota(jnp.int32, (CHUNK, CHUNK), 0)
    c = lax.broadcasted_iota(jnp.int32, (CHUNK, CHUNK), 1)
    d = (c - r) if rev else (r - c)
    return d >= 0, d > 0


def _mod_kernel(c_ref, w_ref, b_ref, o_ref):
    c = c_ref[...]
    s = c * jax.nn.sigmoid(c)
    o_ref[...] = jnp.dot(s, w_ref[...], preferred_element_type=F32,
                         precision=lax.Precision.HIGHEST) + b_ref[...]


def adaln_mod(c_all, w_ada, b_ada):
    r, d = c_all.shape
    n = w_ada.shape[1]
    tn = 512
    return pl.pallas_call(
        _mod_kernel, grid=(n // tn,),
        in_specs=[pl.BlockSpec((r, d), lambda j: (0, 0)),
                  pl.BlockSpec((d, tn), lambda j: (0, j)),
                  pl.BlockSpec((1, tn), lambda j: (0, j))],
        out_specs=pl.BlockSpec((r, tn), lambda j: (0, j)),
        out_shape=jax.ShapeDtypeStruct((r, n), F32),
        compiler_params=_cparams("arbitrary"), name="adaln_mod",
    )(c_all, w_ada, b_ada.reshape(1, n))


def _inproj_kernel(x_ref, g_ref, mod_ref, *refs):
    n_out = len(refs) // 2
    x = x_ref[0]
    y = x * lax.rsqrt(jnp.mean(x * x, axis=-1, keepdims=True) + EPS) * g_ref[...]
    h = (y * (1.0 + mod_ref[0, 0, 1:2, :]) + mod_ref[0, 0, 0:1, :]).astype(BF16)
    for w_ref, o_ref in zip(refs[:n_out], refs[n_out:]):
        o_ref[0] = jnp.dot(h, w_ref[...], preferred_element_type=F32)


def in_projection(xcat, norm_g, modsel, weights, ctx_len):
    b, lc, d = xcat.shape
    ctx_tiles = ctx_len // TOK_TILE
    w_specs = [pl.BlockSpec(w.shape, lambda bi, i: (0, 0)) for w in weights]
    o_specs = [pl.BlockSpec((1, TOK_TILE, w.shape[1]), lambda bi, i: (bi, i, 0)) for w in weights]
    return pl.pallas_call(
        _inproj_kernel, grid=(b, lc // TOK_TILE),
        in_specs=[pl.BlockSpec((1, TOK_TILE, d), lambda bi, i: (bi, i, 0)),
                  pl.BlockSpec((1, d), lambda bi, i: (0, 0)),
                  pl.BlockSpec((1, 1, 2, d), lambda bi, i: (bi, jnp.where(i < ctx_tiles, 0, 1), 0, 0))] + w_specs,
        out_specs=o_specs,
        out_shape=[jax.ShapeDtypeStruct((b, lc, w.shape[1]), F32) for w in weights],
        compiler_params=_cparams("parallel", "arbitrary"), name="in_projection",
    )(xcat, norm_g.reshape(1, d), modsel, *weights)


def _dn_feature_kernel(ctx_len, x_ref, w_ref, o_ref):
    x = x_ref[0]
    lc = x.shape[0]
    t = lax.broadcasted_iota(jnp.int32, (lc, 1), 0)
    pad = CONV_W // 2
    acc = w_ref[0, pad:pad + 1, :] * x
    for j in range(CONV_W):
        s = j - pad
        if s == 0:
            continue
        xs = pltpu.roll(x, (-s) % lc, axis=0)
        if s < 0:
            bad = (t < -s) | ((t >= ctx_len) & (t < ctx_len - s))
        else:
            bad = ((t >= ctx_len - s) & (t < ctx_len)) | (t >= lc - s)
        acc = acc + w_ref[0, j:j + 1, :] * jnp.where(bad, 0.0, xs)
    y = acc * jax.nn.sigmoid(acc)
    kind = pl.program_id(1) // DN_HEADS
    inv = lax.rsqrt(jnp.sum(y * y, axis=-1, keepdims=True) + EPS)
    scale = jnp.where(kind == 0, inv * DN_DK ** -0.5, jnp.where(kind == 1, inv, 1.0))
    o_ref[0] = y * scale


def dn_features_tc(qkv, conv_w, ctx_len):
    b, lc, n = qkv.shape
    nblk = n // LANES
    w = jnp.zeros((nblk, SUBLANES, LANES), F32).at[:, :CONV_W].set(
        conv_w.reshape(CONV_W, nblk, LANES).transpose(1, 0, 2))
    return pl.pallas_call(
        functools.partial(_dn_feature_kernel, ctx_len), grid=(b, nblk),
        in_specs=[pl.BlockSpec((1, lc, LANES), lambda bi, j: (bi, 0, j)),
                  pl.BlockSpec((1, SUBLANES, LANES), lambda bi, j: (j, 0, 0))],
        out_specs=pl.BlockSpec((1, lc, LANES), lambda bi, j: (bi, 0, j)),
        out_shape=jax.ShapeDtypeStruct((b, lc, n), F32),
        compiler_params=_cparams("parallel", "arbitrary"), name="dn_features",
    )(qkv, w)


def _chunk_index(rev, n_ctx, n_all):
    if not rev:
        return lambda j: j
    return lambda j: jnp.where(j < n_ctx, n_ctx - 1 - j, n_all + n_ctx - 1 - j)


def _latent_block(rev, n_ctx, n_all):
    cidx = _chunk_index(rev, n_ctx, n_all)
    first = n_all - n_ctx - 1 if rev else 0
    return lambda j: jnp.where(j < n_ctx, first, cidx(j) - n_ctx)


def _dn_scan_kernel(rev, dirn, alog_ref, dtb_ref, f_ref, ba_ref, o_ref, s_ref):
    @pl.when(pl.program_id(1) == 0)
    def _():
        s_ref[...] = jnp.zeros_like(s_ref)

    incl, strict = _tri_masks(rev)
    incl_b = incl.astype(BF16)
    eye = (lax.broadcasted_iota(jnp.int32, (CHUNK, CHUNK), 0)
           == lax.broadcasted_iota(jnp.int32, (CHUNK, CHUNK), 1)).astype(F32)
    lane = lax.broadcasted_iota(jnp.int32, (CHUNK, LANES), 1)
    onehot0 = (lane == 0).astype(BF16)
    ba = ba_ref[0]
    hd = DN_HEADS * DN_DK
    outs = []
    for h in range(DN_HEADS):
        q = f_ref[0, :, h * DN_DK:(h + 1) * DN_DK]
        k = f_ref[0, :, hd + h * DN_DK:hd + (h + 1) * DN_DK]
        v = f_ref[0, :, 2 * hd + h * DN_DV:2 * hd + (h + 1) * DN_DV]
        cb = dirn * DN_HEADS + h
        beta = jax.nn.sigmoid(ba[:, cb:cb + 1])
        a = ba[:, 2 * DN_HEADS + cb:2 * DN_HEADS + cb + 1]
        g = -jnp.exp(jnp.full((CHUNK, 1), alog_ref[dirn, h], F32)) * _softplus(a + dtb_ref[dirn, h])
        gmat = jnp.broadcast_to(g, (CHUNK, LANES))
        gcb = _mask_dot(incl_b, gmat)
        g_tot = jnp.sum(gmat, axis=0, keepdims=True)
        hi, lo = _split(gcb)
        g_row = (lax.dot_general(onehot0, hi, (((1,), (1,)), ((), ())), preferred_element_type=F32)
                 + lax.dot_general(onehot0, lo, (((1,), (1,)), ((), ())), preferred_element_type=F32))
        decay = jnp.where(incl, jnp.exp(jnp.where(incl, gcb[:, :CHUNK] - g_row, 0.0)), 0.0)
        kb = k * beta
        lower = jnp.where(strict, _dot_nt(kb, k) * decay, 0.0)
        eg = jnp.exp(gcb)
        inv = eye - lower
        pw = lower
        for _ in range(5):
            pw = _dot(pw, pw)
            inv = inv + _dot(inv, pw)
        sol = _dot(inv, jnp.concatenate([v * beta, kb * eg], axis=-1))
        u, w = sol[:, :DN_DV], sol[:, DN_DV:]
        k_dec = k * jnp.exp(g_tot - gcb)
        s = s_ref[h]
        v_new = u - _dot(w, s)
        a_qk = _dot_nt(q, k) * decay
        outs.append(_dot(q * eg, s) + _dot(a_qk, v_new))
        s_ref[h] = s * jnp.exp(g_tot) + _dot_tn(k_dec, v_new)
    o_ref[0] = jnp.concatenate(outs, axis=-1)


def dn_scan_tc(feat, ba, a_log, dt_bias, ctx_len, rev):
    b, lc, _ = feat.shape
    n_all, n_ctx = lc // CHUNK, ctx_len // CHUNK
    dirn = 1 if rev else 0
    cidx = _chunk_index(rev, n_ctx, n_all)
    n_lat = n_all - n_ctx
    smem = pl.BlockSpec(memory_space=pltpu.SMEM)
    return pl.pallas_call(
        functools.partial(_dn_scan_kernel, rev, dirn), grid=(b, n_all),
        in_specs=[smem, smem,
                  pl.BlockSpec((1, CHUNK, feat.shape[2]), lambda bi, j: (bi, cidx(j), 0)),
                  pl.BlockSpec((1, CHUNK, LANES), lambda bi, j: (bi, cidx(j), 0))],
        out_specs=pl.BlockSpec((1, CHUNK, DN_HEADS * DN_DV),
                               lambda bi, j: (bi, _latent_block(rev, n_ctx, n_all)(j), 0)),
        out_shape=jax.ShapeDtypeStruct((b, lc - ctx_len, DN_HEADS * DN_DV), F32),
        scratch_shapes=[pltpu.VMEM((DN_HEADS, DN_DK, DN_DV), F32)],
        compiler_params=_cparams("parallel", "arbitrary"), name="dn_scan_bwd" if rev else "dn_scan_fwd",
    )(a_log, dt_bias, feat, ba)


GLA_SUB = 16


def _gla_scan_kernel(rev, dirn, qk_ref, v_ref, lr_ref, wla_ref, bla_ref, o_ref, s_ref):
    @pl.when(pl.program_id(1) == 0)
    def _():
        s_ref[...] = jnp.zeros_like(s_ref)

    incl, _ = _tri_masks(rev)
    incl_b = incl.astype(BF16)
    hk = GLA_HEADS * GLA_DK
    pre = _dot(lr_ref[0], wla_ref[0]) + bla_ref[0]
    la_all = -_softplus(-pre) * (1.0 / GLA_TAU)
    outs = []
    for h in range(GLA_HEADS):
        q = qk_ref[0, :, h * GLA_DK:(h + 1) * GLA_DK] * GLA_DK ** -0.5
        k = qk_ref[0, :, hk + h * GLA_DK:hk + (h + 1) * GLA_DK]
        v = v_ref[0, :, h * GLA_DV:(h + 1) * GLA_DV]
        la = la_all[:, h * GLA_DK:(h + 1) * GLA_DK]
        bc = _mask_dot(incl_b, la)
        b_tot = jnp.sum(la, axis=0, keepdims=True)
        st = s_ref[h]
        o = _dot_nt(q * jnp.exp(bc), st)
        parts = []
        for i in range(CHUNK // GLA_SUB):
            lo_r, hi_r = i * GLA_SUB, (i + 1) * GLA_SUB
            if rev:
                ref = bc[hi_r - 1:hi_r]
                c0, c1 = lo_r, CHUNK
            else:
                ref = bc[lo_r:lo_r + 1]
                c0, c1 = 0, hi_r
            qi = q[lo_r:hi_r] * jnp.exp(bc[lo_r:hi_r] - ref)
            ki = k[c0:c1] * jnp.exp(ref - bc[c0:c1])
            att = _dot_nt(qi, ki)
            rg = lax.broadcasted_iota(jnp.int32, (GLA_SUB, c1 - c0), 0) + lo_r
            cg = lax.broadcasted_iota(jnp.int32, (GLA_SUB, c1 - c0), 1) + c0
            keep = (cg >= rg) if rev else (cg <= rg)
            parts.append(_dot(jnp.where(keep, att, 0.0), v[c0:c1]))
        outs.append(o + jnp.concatenate(parts, axis=0))
        k_dec = k * jnp.exp(b_tot - bc)
        s_ref[h] = st * jnp.exp(b_tot) + _dot_tn(v, k_dec)
    o_ref[0] = jnp.concatenate(outs, axis=-1)


def gla_scan_tc(qk, v, lr, w_la, b_la, ctx_len, rev):
    b, lc, _ = qk.shape
    n_all, n_ctx = lc // CHUNK, ctx_len // CHUNK
    dirn = 1 if rev else 0
    cidx = _chunk_index(rev, n_ctx, n_all)
    n_lat = n_all - n_ctx
    blk = lambda n: pl.BlockSpec((1, CHUNK, n), lambda bi, j: (bi, cidx(j), 0))
    return pl.pallas_call(
        functools.partial(_gla_scan_kernel, rev, dirn), grid=(b, n_all),
        in_specs=[blk(qk.shape[2]), blk(v.shape[2]), blk(lr.shape[2]),
                  pl.BlockSpec((1,) + w_la.shape[1:], lambda bi, j: (dirn, 0, 0)),
                  pl.BlockSpec((1,) + b_la.shape[1:], lambda bi, j: (dirn, 0, 0))],
        out_specs=pl.BlockSpec((1, CHUNK, GLA_HEADS * GLA_DV),
                               lambda bi, j: (bi, _latent_block(rev, n_ctx, n_all)(j), 0)),
        out_shape=jax.ShapeDtypeStruct((b, lc - ctx_len, GLA_HEADS * GLA_DV), F32),
        scratch_shapes=[pltpu.VMEM((GLA_HEADS, GLA_DV, GLA_DK), F32)],
        compiler_params=_cparams("parallel", "arbitrary"), name="gla_scan_bwd" if rev else "gla_scan_fwd",
    )(qk, v, lr, w_la, b_la)


def _head_norm_gate(o, gate, g, n_heads, dv):
    parts = []
    for h in range(n_heads):
        oh = o[:, h * dv:(h + 1) * dv]
        gh = gate[:, h * dv:(h + 1) * dv]
        yh = oh * lax.rsqrt(jnp.mean(oh * oh, axis=-1, keepdims=True) + EPS) * g
        parts.append(yh * (gh * jax.nn.sigmoid(gh)))
    return parts


def _mix_out_kernel(x_ref, dnf_ref, dnb_ref, z_ref, glf_ref, glb_ref, r_ref, mod_ref, dng_ref,
                    glg_ref, n2g_ref, wout_ref, wq_ref, x1_ref, h2_ref, q_ref):
    parts = (_head_norm_gate(dnf_ref[0] + dnb_ref[0], z_ref[0], dng_ref[...], DN_HEADS, DN_DV)
             + _head_norm_gate(glf_ref[0] + glb_ref[0], r_ref[0], glg_ref[...], GLA_HEADS, GLA_DV))
    y = jnp.dot(jnp.concatenate(parts, axis=-1).astype(BF16), wout_ref[...], preferred_element_type=F32)
    x1 = x_ref[0] + mod_ref[0, 2:3, :] * y
    x1_ref[0] = x1
    n = x1 * lax.rsqrt(jnp.mean(x1 * x1, axis=-1, keepdims=True) + EPS) * n2g_ref[...]
    h2 = n * (1.0 + mod_ref[0, 4:5, :]) + mod_ref[0, 3:4, :]
    h2_ref[0] = h2
    q_ref[0] = jnp.dot(h2.astype(BF16), wq_ref[...], preferred_element_type=F32)


def mix_out_tc(x, dn_f, dn_b, z_cat, gl_f, gl_b, r_cat, mod_l, dn_g, gla_g, n2_g, w_out, w_q, ctx_len):
    b, l, d = x.shape
    off = ctx_len // TOK_TILE
    tok = lambda n: pl.BlockSpec((1, TOK_TILE, n), lambda bi, i: (bi, i, 0))
    cat = lambda n: pl.BlockSpec((1, TOK_TILE, n), lambda bi, i: (bi, i + off, 0))
    full = lambda a: pl.BlockSpec(a.shape, lambda bi, i: (0,) * a.ndim)
    dn_g, gla_g, n2_g = dn_g.reshape(1, -1), gla_g.reshape(1, -1), n2_g.reshape(1, -1)
    nq = w_q.shape[1]
    return pl.pallas_call(
        _mix_out_kernel, grid=(b, l // TOK_TILE),
        in_specs=[tok(d), tok(dn_f.shape[2]), tok(dn_b.shape[2]), cat(z_cat.shape[2]),
                  tok(gl_f.shape[2]), tok(gl_b.shape[2]), cat(r_cat.shape[2]),
                  pl.BlockSpec((1,) + mod_l.shape[1:], lambda bi, i: (bi, 0, 0)),
                  full(dn_g), full(gla_g), full(n2_g), full(w_out), full(w_q)],
        out_specs=[tok(d), tok(d), tok(nq)],
        out_shape=[jax.ShapeDtypeStruct((b, l, d), F32), jax.ShapeDtypeStruct((b, l, d), F32),
                   jax.ShapeDtypeStruct((b, l, nq), F32)],
        compiler_params=_cparams("parallel", "arbitrary"), name="mix_out",
    )(x, dn_f, dn_b, z_cat, gl_f, gl_b, r_cat, mod_l, dn_g, gla_g, n2_g, w_out, w_q)


def _final_kernel(x_ref, y_ref, mod_ref, g_ref, o_ref):
    x = x_ref[0] + mod_ref[0, 5:6, :] * y_ref[0]
    o_ref[0] = x * lax.rsqrt(jnp.mean(x * x, axis=-1, keepdims=True) + EPS) * g_ref[...]


def final_tc(x1, y, mod_l, final_g):
    b, l, d = x1.shape
    tok = pl.BlockSpec((1, TOK_TILE, d), lambda bi, i: (bi, i, 0))
    return pl.pallas_call(
        _final_kernel, grid=(b, l // TOK_TILE),
        in_specs=[tok, tok, pl.BlockSpec((1,) + mod_l.shape[1:], lambda bi, i: (bi, 0, 0)),
                  pl.BlockSpec((1, d), lambda bi, i: (0, 0))],
        out_specs=tok, out_shape=jax.ShapeDtypeStruct((b, l, d), F32),
        compiler_params=_cparams("parallel", "arbitrary"), name="final_norm",
    )(x1, y, mod_l, final_g.reshape(1, d))


def _pad_cols(w, n):
    return jnp.pad(w, ((0, 0), (0, n - w.shape[1])))


def forward(x, c, ctx, c_ctx, w_ada, b_ada, norm1_g, norm2_g, w_in, conv_w, dn_a_log,
            dn_dt_bias, dn_norm_g, gla_wa2, gla_ba, gla_norm_g, w_out, peer_wq, peer_keys,
            peer_u, peer_v, final_g):
    b, l, d = x.shape
    ctx_len = ctx.shape[1]
    rows = l // GRID_W
    c_all = jnp.concatenate([c, c_ctx[None]], axis=0)
    c_all = jnp.pad(c_all, ((0, (-c_all.shape[0]) % SUBLANES), (0, 0)))
    mod = adaln_mod(c_all, w_ada, b_ada)
    mod_l = mod[:b].reshape(b, 6, d)
    mod_c = jnp.broadcast_to(mod[b].reshape(1, 6, d), (b, 6, d))
    modsel = jnp.stack([mod_c[:, 0:2], mod_l[:, 0:2]], axis=1)
    xcat = jnp.concatenate([ctx, x], axis=1)
    o = DN_QKV
    hv = DN_HEADS * DN_DV
    w_dn_qkv, w_dn_z = w_in[:, :o], w_in[:, o:o + hv]
    w_dn_ba = _pad_cols(w_in[:, o + hv:DN_COLS], LANES)
    g0 = DN_COLS
    gqk, gv = 2 * GLA_HEADS * GLA_DK, GLA_HEADS * GLA_DV
    w_gl_qk, w_gl_v = w_in[:, g0:g0 + gqk], w_in[:, g0 + gqk:g0 + gqk + gv]
    w_gl_r = w_in[:, g0 + gqk + gv:g0 + gqk + 2 * gv]
    w_gl_lr = _pad_cols(w_in[:, g0 + gqk + 2 * gv:], LANES)
    weights = [w.astype(BF16) for w in (w_dn_qkv, w_dn_z, w_dn_ba, w_gl_qk, w_gl_v, w_gl_r, w_gl_lr)]
    dn_qkv, dn_z, dn_ba, gl_qk, gl_v, gl_r, gl_lr = in_projection(xcat, norm1_g, modsel, weights, ctx_len)
    feat = dn_features_tc(dn_qkv, conv_w, ctx_len)
    dn_f = dn_scan_tc(feat, dn_ba, dn_a_log, dn_dt_bias, ctx_len, rev=False)
    dn_b = dn_scan_tc(feat, dn_ba, dn_a_log, dn_dt_bias, ctx_len, rev=True)
    colmaj = lambda t: jnp.concatenate([t[:, :ctx_len], to_col_major(t[:, ctx_len:], rows)], axis=1)
    w_la = jnp.zeros((2, LANES, GLA_HEADS * GLA_DK), F32)
    for dd in range(2):
        w_la = w_la.at[dd, dd * GLA_LR:(dd + 1) * GLA_LR].set(gla_wa2[dd])
    b_la = gla_ba.reshape(2, 1, GLA_HEADS * GLA_DK)
    gqk_c, gv_c, glr_c = colmaj(gl_qk), colmaj(gl_v), colmaj(gl_lr)
    gl_f = from_col_major(gla_scan_tc(gqk_c, gv_c, glr_c, w_la.astype(BF16), b_la, ctx_len, rev=False), rows)
    gl_b = from_col_major(gla_scan_tc(gqk_c, gv_c, glr_c, w_la.astype(BF16), b_la, ctx_len, rev=True), rows)
    x1, h2, q = mix_out_tc(x, dn_f, dn_b, dn_z, gl_f, gl_b, gl_r, mod_l, dn_norm_g, gla_norm_g,
                           norm2_g, w_out.astype(BF16), peer_wq.astype(BF16), ctx_len)
    n_tok = b * l
    idx, gate = peer_select_tc(q.reshape(n_tok, -1), peer_keys)
    y = peer_apply(h2.reshape(n_tok, d), idx, gate, peer_u, peer_v)
    return final_tc(x1, y.reshape(b, l, d), mod_l, final_g)


def _final_norm_kernel(x_ref, g_ref, o_ref):
    xf = x_ref[...]
    y = xf * lax.rsqrt(jnp.mean(xf * xf, axis=-1, keepdims=True) + EPS)
    o_ref[...] = y * g_ref[...]


def kernel(x, c, ctx, c_ctx, w_ada, b_ada, norm1_g, norm2_g, w_in, conv_w, dn_a_log,
           dn_dt_bias, dn_norm_g, gla_wa2, gla_ba, gla_norm_g, w_out, peer_wq, peer_keys,
           peer_u, peer_v, final_g):
    assert w_ada.shape[0] == 1, "single-layer block: the context stream is only consumed, never updated"
    return forward(x, c, ctx, c_ctx, w_ada[0], b_ada[0], norm1_g[0], norm2_g[0], w_in[0], conv_w[0],
                   dn_a_log[0], dn_dt_bias[0], dn_norm_g[0], gla_wa2[0], gla_ba[0], gla_norm_g[0],
                   w_out[0], peer_wq[0], peer_keys[0], peer_u[0], peer_v[0], final_g)
```

```python
import functools
import math
import jax
import jax.numpy as jnp
from jax import lax
import numpy as np
from jax.experimental import pallas as pl
from jax.experimental.pallas import tpu as pltpu
from jax.experimental.pallas import tpu_sc as plsc

D_MODEL = 1024
GRID_W = 64
CTX_LEN = 256
DN_HEADS = 4
DN_DK = 128
DN_DV = 128
CONV_W = 5
GLA_HEADS = 4
GLA_DK = 64
GLA_DV = 128
GLA_LR = 16
GLA_TAU = 16.0
CHUNK = 64
PEER_HEADS = 8
PEER_NKEYS = 128
PEER_DQ = 256
PEER_TOPK = 16
PEER_BLOCK = 128
EPS = 1e-6
DN_QKV = 2 * DN_HEADS * DN_DK + DN_HEADS * DN_DV
DN_COLS = DN_QKV + DN_HEADS * DN_DV + 4 * DN_HEADS


def rmsnorm(x, g):
    xf = x.astype(jnp.float32)
    y = xf * lax.rsqrt(jnp.mean(xf * xf, axis=-1, keepdims=True) + EPS)
    return (y * g.astype(jnp.float32)).astype(x.dtype)


def modulate(h, shift, scale):
    return h * (1 + scale) + shift


def l2norm(t):
    tf = t.astype(jnp.float32)
    return tf * lax.rsqrt(jnp.sum(tf * tf, axis=-1, keepdims=True) + EPS)


def head_norm_gate(o, g, gate):
    y = o * lax.rsqrt(jnp.mean(o * o, axis=-1, keepdims=True) + EPS) * g.astype(jnp.float32)
    y = y * jax.nn.silu(gate.astype(jnp.float32))
    return y.reshape(y.shape[:2] + (-1,)).astype(gate.dtype)


def short_conv(t, w):
    pad = CONV_W // 2
    return lax.conv_general_dilated(
        t, w[:, None, :].astype(t.dtype), window_strides=(1,), padding=((pad, pad),),
        dimension_numbers=('NWC', 'WIO', 'NWC'), feature_group_count=t.shape[-1])


def _chunk(t, n):
    b, _, h = t.shape[:3]
    t = t.reshape((b, n, CHUNK, h) + t.shape[3:])
    return t.transpose((1, 0, 3, 2) + tuple(range(4, t.ndim)))


def _unchunk(t):
    n, b, h, c, d = t.shape
    return t.transpose(1, 0, 3, 2, 4).reshape(b, n * c, h, d)


def _tril(strict):
    i = jnp.arange(CHUNK)
    return (i[:, None] > i[None, :]) if strict else (i[:, None] >= i[None, :])


def gated_delta_scan(q, k, v, g, beta, s0, with_output):
    f32 = jnp.float32
    n = q.shape[1] // CHUNK
    dv = v.shape[-1]
    q, k, v, g, beta = (_chunk(t.astype(f32), n) for t in (q, k, v, g, beta))
    gc = jnp.cumsum(g, axis=-1)
    decay = jnp.exp(jnp.where(_tril(False), gc[..., :, None] - gc[..., None, :], -jnp.inf))
    kb = k * beta[..., None]
    lower = jnp.where(_tril(True), jnp.einsum('nbhtd,nbhsd->nbhts', kb, k) * decay, 0.0)
    rhs = jnp.concatenate([v * beta[..., None], kb * jnp.exp(gc)[..., None]], axis=-1)
    sol = lax.linalg.triangular_solve(lower, rhs, left_side=True, lower=True, unit_diagonal=True)
    u, w = sol[..., :dv], sol[..., dv:]
    k_dec = k * jnp.exp(gc[..., -1:] - gc)[..., None]
    g_last = jnp.exp(gc[..., -1])

    def step(S, xs):
        u_c, w_c, kd_c, gl_c = xs[:4]
        v_new = u_c - jnp.einsum('bhck,bhkv->bhcv', w_c, S)
        S_next = S * gl_c[..., None, None] + jnp.einsum('bhck,bhcv->bhkv', kd_c, v_new)
        if not with_output:
            return S_next, None
        a_c, qd_c = xs[4:]
        o = jnp.einsum('bhck,bhkv->bhcv', qd_c, S) + jnp.einsum('bhts,bhsv->bhtv', a_c, v_new)
        return S_next, o

    if with_output:
        a_qk = jnp.einsum('nbhtd,nbhsd->nbhts', q, k) * decay
        xs = (u, w, k_dec, g_last, a_qk, q * jnp.exp(gc)[..., None])
    else:
        xs = (u, w, k_dec, g_last)
    S, o = lax.scan(step, s0.astype(f32), xs)
    return (_unchunk(o) if with_output else None), S


def gla_scan(q, k, v, log_a, s0, with_output):
    f32 = jnp.float32
    n = q.shape[1] // CHUNK
    q, k, v, la = (_chunk(t.astype(f32), n) for t in (q, k, v, log_a))
    b = jnp.cumsum(la, axis=-2)
    b_last = b[..., -1, :]
    k_dec = k * jnp.exp(b_last[..., None, :] - b)
    causal = _tril(False)

    def step(S, xs):
        v_c, kd_c, bl_c = xs[:3]
        S_next = S * jnp.exp(bl_c)[..., :, None] + jnp.einsum('bhck,bhcv->bhkv', kd_c, v_c)
        if not with_output:
            return S_next, None
        q_c, k_c, b_c = xs[3:]
        diff = jnp.where(causal[:, :, None], b_c[..., :, None, :] - b_c[..., None, :, :], -jnp.inf)
        att = jnp.einsum('bhtk,bhsk,bhtsk->bhts', q_c, k_c, jnp.exp(diff))
        o = (jnp.einsum('bhck,bhkv->bhcv', q_c * jnp.exp(b_c), S)
             + jnp.einsum('bhts,bhsv->bhtv', att, v_c))
        return S_next, o

    xs = (v, k_dec, b_last, q, k, b) if with_output else (v, k_dec, b_last)
    S, o = lax.scan(step, s0.astype(f32), xs)
    return (_unchunk(o) if with_output else None), S


def bidirectional(scan_fn, ctx_f, ctx_b, lat_f, lat_b, s0, ctx_out):
    rev = lambda args: tuple(jnp.flip(a, axis=1) for a in args)
    oc_f, sc_f = scan_fn(*ctx_f, s0, ctx_out)
    oc_b, sc_b = scan_fn(*rev(ctx_b), s0, ctx_out)
    ol_f, _ = scan_fn(*lat_f, sc_f, True)
    ol_b, _ = scan_fn(*rev(lat_b), sc_b, True)
    o_lat = ol_f + jnp.flip(ol_b, axis=1)
    o_ctx = (oc_f + jnp.flip(oc_b, axis=1)) if ctx_out else None
    return o_ctx, o_lat


def dn_features(p, conv_w, a_log, dt_bias):
    b, l, _ = p.shape
    qk, vd, h = DN_HEADS * DN_DK, DN_HEADS * DN_DV, DN_HEADS
    qkv = jax.nn.silu(short_conv(p[..., :DN_QKV], conv_w))
    q = l2norm(qkv[..., :qk].reshape(b, l, h, DN_DK)) * DN_DK ** -0.5
    k = l2norm(qkv[..., qk:2 * qk].reshape(b, l, h, DN_DK))
    v = qkv[..., 2 * qk:].reshape(b, l, h, DN_DV)
    z = p[..., DN_QKV:DN_QKV + vd].reshape(b, l, h, DN_DV)
    o = DN_QKV + vd
    beta = jax.nn.sigmoid(p[..., o:o + 2 * h].astype(jnp.float32)).reshape(b, l, 2, h)
    a = p[..., o + 2 * h:o + 4 * h].astype(jnp.float32).reshape(b, l, 2, h)
    g = -jnp.exp(a_log.astype(jnp.float32)) * jax.nn.softplus(a + dt_bias.astype(jnp.float32))
    return q, k, v, z, g, beta


def gla_features(p, wa2, ba):
    b, l, _ = p.shape
    qk, vd = GLA_HEADS * GLA_DK, GLA_HEADS * GLA_DV
    q = p[..., :qk].reshape(b, l, GLA_HEADS, GLA_DK) * GLA_DK ** -0.5
    k = p[..., qk:2 * qk].reshape(b, l, GLA_HEADS, GLA_DK)
    v = p[..., 2 * qk:2 * qk + vd].reshape(b, l, GLA_HEADS, GLA_DV)
    r = p[..., 2 * qk + vd:2 * qk + 2 * vd].reshape(b, l, GLA_HEADS, GLA_DV)
    lr = p[..., 2 * qk + 2 * vd:].reshape(b, l, 2, GLA_LR)
    pre = jnp.einsum('bldr,drk->bldk', lr, wa2) + ba
    log_a = jax.nn.log_sigmoid(pre.astype(jnp.float32)) / GLA_TAU
    return q, k, v, r, log_a.reshape(b, l, 2, GLA_HEADS, GLA_DK)


def to_col_major(t, rows):
    b = t.shape[0]
    return t.reshape((b, rows, GRID_W) + t.shape[2:]).swapaxes(1, 2).reshape(t.shape)


def from_col_major(t, rows):
    b = t.shape[0]
    return t.reshape((b, GRID_W, rows) + t.shape[2:]).swapaxes(1, 2).reshape(t.shape)


def token_mixer(h_ctx, h_lat, rows, w_in, conv_w, dn_a_log, dn_dt_bias, dn_norm_g,
                gla_wa2, gla_ba, gla_norm_g, w_out):
    b = h_lat.shape[0]
    p_c = h_ctx @ w_in
    p_l = h_lat @ w_in
    qc, kc, vc, zc, gc, bc = dn_features(p_c[..., :DN_COLS], conv_w, dn_a_log, dn_dt_bias)
    ql, kl, vl, zl, gl, bl = dn_features(p_l[..., :DN_COLS], conv_w, dn_a_log, dn_dt_bias)
    s0_dn = jnp.zeros((b, DN_HEADS, DN_DK, DN_DV), jnp.float32)
    _, dn_l = bidirectional(
        gated_delta_scan,
        (qc, kc, vc, gc[:, :, 0], bc[:, :, 0]), (qc, kc, vc, gc[:, :, 1], bc[:, :, 1]),
        (ql, kl, vl, gl[:, :, 0], bl[:, :, 0]), (ql, kl, vl, gl[:, :, 1], bl[:, :, 1]),
        s0_dn, False)
    gqc, gkc, gvc, grc, lac = gla_features(p_c[..., DN_COLS:], gla_wa2, gla_ba)
    gql, gkl, gvl, grl, lal = gla_features(p_l[..., DN_COLS:], gla_wa2, gla_ba)
    col = lambda t: to_col_major(t, rows)
    gql, gkl, gvl, lal = col(gql), col(gkl), col(gvl), col(lal)
    s0_gla = jnp.zeros((b, GLA_HEADS, GLA_DK, GLA_DV), jnp.float32)
    _, gla_l = bidirectional(
        gla_scan,
        (gqc, gkc, gvc, lac[:, :, 0]), (gqc, gkc, gvc, lac[:, :, 1]),
        (gql, gkl, gvl, lal[:, :, 0]), (gql, gkl, gvl, lal[:, :, 1]),
        s0_gla, False)
    gla_l = from_col_major(gla_l, rows)
    return jnp.concatenate([head_norm_gate(dn_l, dn_norm_g, zl),
                            head_norm_gate(gla_l, gla_norm_g, grl)], axis=-1) @ w_out


SC_LANES = 16
SUBLANES = 8
LANES = 128
PICK_GROUP = 32
TOK_STEP = 2
PICKS = PEER_HEADS * PEER_TOPK
GROUPS_PER_TOK = PICKS // PICK_GROUP
ACT_UNROLL = 8


def _sc_mesh():
    return plsc.VectorSubcoreMesh(core_axis_name="c", subcore_axis_name="s")


def _sc_pipeline(body, n_steps, in_specs, out_specs, operands):
    pltpu.emit_pipeline(
        body, grid=(n_steps,), in_specs=in_specs, out_specs=out_specs,
        core_axis_name=("c", "s"), dimension_semantics=(pltpu.PARALLEL,),
        trace_scopes=False,
    )(*operands)


def peer_act_partial_sc(u3, idx2, h3):
    n_groups, n_tok = idx2.shape[0], h3.shape[0]
    nsub = TOK_STEP * GROUPS_PER_TOK
    n_chunks = SUBLANES * LANES // SC_LANES

    @functools.partial(
        pl.kernel, mesh=_sc_mesh(),
        out_type=jax.ShapeDtypeStruct((n_groups, PICK_GROUP * SC_LANES), jnp.float32),
        scratch_types=[pltpu.VMEM((2, PICK_GROUP, SUBLANES, LANES), jnp.float32),
                       pltpu.SemaphoreType.DMA((2,))],
    )
    def k(u_hbm, i_hbm, h_hbm, o_hbm, rows, sems):
        def body(i_v, h_v, o_v):
            def fetch(j, slot):
                return pltpu.make_async_copy(u_hbm.at[i_v.at[j]], rows.at[slot], sems.at[slot])

            fetch(0, 0).start()

            def sub(j, carry):
                slot = j % 2

                @pl.when(j + 1 < nsub)
                def _():
                    fetch(j + 1, 1 - slot).start()

                fetch(j, slot).wait()
                t = j // GROUPS_PER_TOK

                def picks(g, carry2):
                    kb = g * ACT_UNROLL
                    accs = [None] * ACT_UNROLL
                    for c in range(n_chunks):
                        s, l = divmod(c, LANES // SC_LANES)
                        hc = h_v[t, s, pl.ds(l * SC_LANES, SC_LANES)]
                        for i in range(ACT_UNROLL):
                            p = rows[slot, kb + i, s, pl.ds(l * SC_LANES, SC_LANES)] * hc
                            accs[i] = p if accs[i] is None else accs[i] + p
                    for i in range(ACT_UNROLL):
                        o_v[j, pl.ds((kb + i) * SC_LANES, SC_LANES)] = accs[i]
                    return carry2

                lax.fori_loop(0, PICK_GROUP // ACT_UNROLL, picks, 0)
                return carry

            lax.fori_loop(0, nsub, sub, 0)

        _sc_pipeline(
            body, n_tok // TOK_STEP,
            [pl.BlockSpec((nsub, PICK_GROUP), lambda i: (i, 0)),
             pl.BlockSpec((TOK_STEP, SUBLANES, LANES), lambda i: (i, 0, 0))],
            [pl.BlockSpec((nsub, PICK_GROUP * SC_LANES), lambda i: (i, 0))],
            (i_hbm, h_hbm, o_hbm))

    return k(u3, idx2, h3)


def peer_combine_sc(v3, idx2, coef_b):
    n_groups = idx2.shape[0]
    n_tok = n_groups // GROUPS_PER_TOK
    nsub = TOK_STEP * GROUPS_PER_TOK
    n_chunks = SUBLANES * LANES // SC_LANES

    @functools.partial(
        pl.kernel, mesh=_sc_mesh(),
        out_type=jax.ShapeDtypeStruct((n_tok, SUBLANES, LANES), jnp.float32),
        scratch_types=[pltpu.VMEM((2, PICK_GROUP, SUBLANES, LANES), jnp.float32),
                       pltpu.SemaphoreType.DMA((2,))],
    )
    def k(v_hbm, i_hbm, c_hbm, o_hbm, rows, sems):
        def body(i_v, c_v, o_v):
            def fetch(j, slot):
                return pltpu.make_async_copy(v_hbm.at[i_v.at[j]], rows.at[slot], sems.at[slot])

            fetch(0, 0).start()
            for j in range(nsub):
                slot = j % 2
                if j + 1 < nsub:
                    fetch(j + 1, 1 - slot).start()
                fetch(j, slot).wait()
                t = j // GROUPS_PER_TOK
                first = j % GROUPS_PER_TOK == 0
                cks = [c_v[j, pl.ds(kk * SC_LANES, SC_LANES)] for kk in range(PICK_GROUP)]

                def chunk(c, carry, slot=slot, t=t, first=first, cks=cks):
                    s = c // (LANES // SC_LANES)
                    l = (c % (LANES // SC_LANES)) * SC_LANES
                    parts = [cks[kk] * rows[slot, kk, s, pl.ds(l, SC_LANES)] for kk in range(4)]
                    for kk in range(4, PICK_GROUP):
                        parts[kk % 4] = parts[kk % 4] + cks[kk] * rows[slot, kk, s, pl.ds(l, SC_LANES)]
                    tot = (parts[0] + parts[1]) + (parts[2] + parts[3])
                    if not first:
                        tot = tot + o_v[t, s, pl.ds(l, SC_LANES)]
                    o_v[t, s, pl.ds(l, SC_LANES)] = tot
                    return carry

                lax.fori_loop(0, n_chunks, chunk, 0)

        _sc_pipeline(
            body, n_tok // TOK_STEP,
            [pl.BlockSpec((nsub, PICK_GROUP), lambda i: (i, 0)),
             pl.BlockSpec((nsub, PICK_GROUP * SC_LANES), lambda i: (i, 0))],
            [pl.BlockSpec((TOK_STEP, SUBLANES, LANES), lambda i: (i, 0, 0))],
            (i_hbm, c_hbm, o_hbm))

    return k(v3, idx2, coef_b)


def _segment_matrix():
    r = lax.broadcasted_iota(jnp.int32, (PICK_GROUP * SC_LANES, PICK_GROUP), 0) // SC_LANES
    c = lax.broadcasted_iota(jnp.int32, (PICK_GROUP * SC_LANES, PICK_GROUP), 1)
    return (r == c).astype(jnp.float32)


def _coef_kernel(part_ref, gate_ref, o_ref):
    seg = _segment_matrix()
    act = jnp.dot(part_ref[...], seg, preferred_element_type=jnp.float32,
                  precision=lax.Precision.HIGHEST)
    coef = gate_ref[...] * (0.5 * act * (1.0 + lax.erf(act * (2.0 ** -0.5))))
    o_ref[...] = lax.dot_general(coef, seg, (((1,), (1,)), ((), ())),
                                 preferred_element_type=jnp.float32,
                                 precision=lax.Precision.HIGHEST)


def peer_coef_tc(part, gate2):
    n_groups, width = part.shape
    tile = 1024
    return pl.pallas_call(
        _coef_kernel,
        grid=(n_groups // tile,),
        in_specs=[pl.BlockSpec((tile, width), lambda i: (i, 0)),
                  pl.BlockSpec((tile, PICK_GROUP), lambda i: (i, 0))],
        out_specs=pl.BlockSpec((tile, width), lambda i: (i, 0)),
        out_shape=jax.ShapeDtypeStruct((n_groups, width), jnp.float32),
        name="peer_coef",
    )(part, gate2)


def peer_apply(h, idx, gate, u3, v3):
    n_tok, d = h.shape
    idx2 = idx.reshape(n_tok * GROUPS_PER_TOK, PICK_GROUP)
    gate2 = gate.reshape(n_tok * GROUPS_PER_TOK, PICK_GROUP)
    part = peer_act_partial_sc(u3, idx2, h.reshape(n_tok, SUBLANES, LANES))
    coef_b = peer_coef_tc(part, gate2)
    return peer_combine_sc(v3, idx2, coef_b).reshape(n_tok, d)


SELECT_TILE = 256


def _top_rows(s, k, payload=None):
    n = s.shape[0]
    row = lax.broadcasted_iota(jnp.int32, s.shape, 0).astype(jnp.float32)
    vals, picked = [], []
    for _ in range(k):
        m = jnp.max(s, axis=0, keepdims=True)
        first = jnp.min(jnp.where(s == m, row, float(n)), axis=0, keepdims=True)
        sel = row == first
        vals.append(m)
        if payload is None:
            picked.append(first)
        else:
            picked.append(jnp.max(jnp.where(sel, payload, -1.0), axis=0, keepdims=True))
        s = jnp.where(sel, -jnp.inf, s)
    return jnp.concatenate(vals, axis=0), jnp.concatenate(picked, axis=0)


def _select_kernel(q_ref, k_ref, idx_ref, gate_ref):
    half = PEER_DQ // 2
    tops = []
    for p in range(2):
        s = lax.dot_general(k_ref[0, p], q_ref[:, p * half:(p + 1) * half],
                            (((1,), (1,)), ((), ())), preferred_element_type=jnp.float32,
                            precision=lax.Precision.HIGHEST)
        tops.append(_top_rows(s, PEER_TOPK))
    (s0, i0), (s1, i1) = tops
    cand_s = jnp.concatenate([s0[i:i + 1] + s1 for i in range(PEER_TOPK)], axis=0)
    cand_i = jnp.concatenate([i0[i:i + 1] * float(PEER_NKEYS) + i1 for i in range(PEER_TOPK)], axis=0)
    best_s, idx = _top_rows(cand_s, PEER_TOPK, payload=cand_i)
    e = jnp.exp(best_s - best_s[0:1])
    idx_ref[0, 0] = idx.astype(jnp.int32)
    gate_ref[0, 0] = e / jnp.sum(e, axis=0, keepdims=True)


def peer_select_tc(q, keys):
    n_tok = q.shape[0]
    n_tiles = n_tok // SELECT_TILE
    out_sds = lambda dt: jax.ShapeDtypeStruct((n_tiles, PEER_HEADS, PEER_TOPK, SELECT_TILE), dt)
    out_spec = pl.BlockSpec((1, 1, PEER_TOPK, SELECT_TILE), lambda i, h: (i, h, 0, 0))
    idx_t, gate_t = pl.pallas_call(
        _select_kernel,
        grid=(n_tiles, PEER_HEADS),
        in_specs=[pl.BlockSpec((SELECT_TILE, PEER_DQ), lambda i, h: (i, h)),
                  pl.BlockSpec((1, 2, PEER_NKEYS, PEER_DQ // 2), lambda i, h: (h, 0, 0, 0))],
        out_specs=[out_spec, out_spec],
        out_shape=[out_sds(jnp.int32), out_sds(jnp.float32)],
        name="peer_select",
    )(q, keys)
    to_tok = lambda a: a.transpose(0, 3, 1, 2).reshape(n_tok, PICKS)
    return to_tok(idx_t), to_tok(gate_t)


def peer(h, wq, keys, u_tab, v_tab):
    b, l, d = h.shape
    t = b * l
    hb = h.reshape(t, d)
    idx, gate = peer_select_tc(hb @ wq, keys)
    y = peer_apply(hb, idx, gate, u_tab, v_tab)
    return y.reshape(b, l, d)


TOK_TILE = 256
VMEM_LIMIT_BYTES = 48 * 1024 * 1024
F32 = jnp.float32
BF16 = jnp.bfloat16


def _cparams(*semantics):
    return pltpu.CompilerParams(dimension_semantics=semantics, vmem_limit_bytes=VMEM_LIMIT_BYTES)


def _dot(a, b):
    return jnp.dot(a.astype(BF16), b.astype(BF16), preferred_element_type=F32)


def _dot_nt(a, b):
    return lax.dot_general(a.astype(BF16), b.astype(BF16), (((1,), (1,)), ((), ())),
                           preferred_element_type=F32)


def _dot_tn(a, b):
    return lax.dot_general(a.astype(BF16), b.astype(BF16), (((0,), (0,)), ((), ())),
                           preferred_element_type=F32)


def _split(x):
    hi = x.astype(BF16)
    return hi, (x - hi.astype(F32)).astype(BF16)


def _mask_dot(mask_bf16, x):
    hi, lo = _split(x)
    return (jnp.dot(mask_bf16, hi, preferred_element_type=F32)
            + jnp.dot(mask_bf16, lo, preferred_element_type=F32))


def _softplus(x):
    return jnp.maximum(x, 0.0) + jnp.log(1.0 + jnp.exp(-jnp.abs(x)))


def _tri_masks(rev):
    r = lax.broadcasted_iota(jnp.int32, (CHUNK, CHUNK), 0)
    c = lax.broadcasted_iota(jnp.int32, (CHUNK, CHUNK), 1)
    d = (c - r) if rev else (r - c)
    return d >= 0, d > 0


def _mod_kernel(c_ref, w_ref, b_ref, o_ref):
    c = c_ref[...]
    s = c * jax.nn.sigmoid(c)
    o_ref[...] = jnp.dot(s, w_ref[...], preferred_element_type=F32,
                         precision=lax.Precision.HIGHEST) + b_ref[...]


def adaln_mod(c_all, w_ada, b_ada):
    r, d = c_all.shape
    n = w_ada.shape[1]
    tn = 512
    return pl.pallas_call(
        _mod_kernel, grid=(n // tn,),
        in_specs=[pl.BlockSpec((r, d), lambda j: (0, 0)),
                  pl.BlockSpec((d, tn), lambda j: (0, j)),
                  pl.BlockSpec((1, tn), lambda j: (0, j))],
        out_specs=pl.BlockSpec((r, tn), lambda j: (0, j)),
        out_shape=jax.ShapeDtypeStruct((r, n), F32),
        compiler_params=_cparams("arbitrary"), name="adaln_mod",
    )(c_all, w_ada, b_ada.reshape(1, n))


def _inproj_kernel(x_ref, g_ref, mod_ref, *refs):
    n_out = len(refs) // 2
    x = x_ref[0]
    y = x * lax.rsqrt(jnp.mean(x * x, axis=-1, keepdims=True) + EPS) * g_ref[...]
    h = (y * (1.0 + mod_ref[0, 0, 1:2, :]) + mod_ref[0, 0, 0:1, :]).astype(BF16)
    for w_ref, o_ref in zip(refs[:n_out], refs[n_out:]):
        o_ref[0] = jnp.dot(h, w_ref[...], preferred_element_type=F32)


def in_projection(xcat, norm_g, modsel, weights, ctx_len):
    b, lc, d = xcat.shape
    ctx_tiles = ctx_len // TOK_TILE
    w_specs = [pl.BlockSpec(w.shape, lambda bi, i: (0, 0)) for w in weights]
    o_specs = [pl.BlockSpec((1, TOK_TILE, w.shape[1]), lambda bi, i: (bi, i, 0)) for w in weights]
    return pl.pallas_call(
        _inproj_kernel, grid=(b, lc // TOK_TILE),
        in_specs=[pl.BlockSpec((1, TOK_TILE, d), lambda bi, i: (bi, i, 0)),
                  pl.BlockSpec((1, d), lambda bi, i: (0, 0)),
                  pl.BlockSpec((1, 1, 2, d), lambda bi, i: (bi, jnp.where(i < ctx_tiles, 0, 1), 0, 0))] + w_specs,
        out_specs=o_specs,
        out_shape=[jax.ShapeDtypeStruct((b, lc, w.shape[1]), F32) for w in weights],
        compiler_params=_cparams("parallel", "arbitrary"), name="in_projection",
    )(xcat, norm_g.reshape(1, d), modsel, *weights)


def _dn_feature_kernel(ctx_len, x_ref, w_ref, o_ref):
    x = x_ref[0]
    lc = x.shape[0]
    t = lax.broadcasted_iota(jnp.int32, (lc, 1), 0)
    pad = CONV_W // 2
    acc = w_ref[0, pad:pad + 1, :] * x
    for j in range(CONV_W):
        s = j - pad
        if s == 0:
            continue
        xs = pltpu.roll(x, (-s) % lc, axis=0)
        if s < 0:
            bad = (t < -s) | ((t >= ctx_len) & (t < ctx_len - s))
        else:
            bad = ((t >= ctx_len - s) & (t < ctx_len)) | (t >= lc - s)
        acc = acc + w_ref[0, j:j + 1, :] * jnp.where(bad, 0.0, xs)
    y = acc * jax.nn.sigmoid(acc)
    kind = pl.program_id(1) // DN_HEADS
    inv = lax.rsqrt(jnp.sum(y * y, axis=-1, keepdims=True) + EPS)
    scale = jnp.where(kind == 0, inv * DN_DK ** -0.5, jnp.where(kind == 1, inv, 1.0))
    o_ref[0] = y * scale


def dn_features_tc(qkv, conv_w, ctx_len):
    b, lc, n = qkv.shape
    nblk = n // LANES
    w = jnp.zeros((nblk, SUBLANES, LANES), F32).at[:, :CONV_W].set(
        conv_w.reshape(CONV_W, nblk, LANES).transpose(1, 0, 2))
    return pl.pallas_call(
        functools.partial(_dn_feature_kernel, ctx_len), grid=(b, nblk),
        in_specs=[pl.BlockSpec((1, lc, LANES), lambda bi, j: (bi, 0, j)),
                  pl.BlockSpec((1, SUBLANES, LANES), lambda bi, j: (j, 0, 0))],
        out_specs=pl.BlockSpec((1, lc, LANES), lambda bi, j: (bi, 0, j)),
        out_shape=jax.ShapeDtypeStruct((b, lc, n), F32),
        compiler_params=_cparams("parallel", "arbitrary"), name="dn_features",
    )(qkv, w)


def _chunk_index(rev, n_ctx, n_all):
    if not rev:
        return lambda j: j
    return lambda j: jnp.where(j < n_ctx, n_ctx - 1 - j, n_all + n_ctx - 1 - j)


def _latent_block(rev, n_ctx, n_all):
    cidx = _chunk_index(rev, n_ctx, n_all)
    first = n_all - n_ctx - 1 if rev else 0
    return lambda j: jnp.where(j < n_ctx, first, cidx(j) - n_ctx)


def _dn_scan_kernel(rev, dirn, alog_ref, dtb_ref, f_ref, ba_ref, o_ref, s_ref):
    @pl.when(pl.program_id(1) == 0)
    def _():
        s_ref[...] = jnp.zeros_like(s_ref)

    incl, strict = _tri_masks(rev)
    incl_b = incl.astype(BF16)
    eye = (lax.broadcasted_iota(jnp.int32, (CHUNK, CHUNK), 0)
           == lax.broadcasted_iota(jnp.int32, (CHUNK, CHUNK), 1)).astype(F32)
    lane = lax.broadcasted_iota(jnp.int32, (CHUNK, LANES), 1)
    onehot0 = (lane == 0).astype(BF16)
    ba = ba_ref[0]
    hd = DN_HEADS * DN_DK
    outs = []
    for h in range(DN_HEADS):
        q = f_ref[0, :, h * DN_DK:(h + 1) * DN_DK]
        k = f_ref[0, :, hd + h * DN_DK:hd + (h + 1) * DN_DK]
        v = f_ref[0, :, 2 * hd + h * DN_DV:2 * hd + (h + 1) * DN_DV]
        cb = dirn * DN_HEADS + h
        beta = jax.nn.sigmoid(ba[:, cb:cb + 1])
        a = ba[:, 2 * DN_HEADS + cb:2 * DN_HEADS + cb + 1]
        g = -jnp.exp(jnp.full((CHUNK, 1), alog_ref[dirn, h], F32)) * _softplus(a + dtb_ref[dirn, h])
        gmat = jnp.broadcast_to(g, (CHUNK, LANES))
        gcb = _mask_dot(incl_b, gmat)
        g_tot = jnp.sum(gmat, axis=0, keepdims=True)
        hi, lo = _split(gcb)
        g_row = (lax.dot_general(onehot0, hi, (((1,), (1,)), ((), ())), preferred_element_type=F32)
                 + lax.dot_general(onehot0, lo, (((1,), (1,)), ((), ())), preferred_element_type=F32))
        decay = jnp.where(incl, jnp.exp(jnp.where(incl, gcb[:, :CHUNK] - g_row, 0.0)), 0.0)
        kb = k * beta
        lower = jnp.where(strict, _dot_nt(kb, k) * decay, 0.0)
        eg = jnp.exp(gcb)
        inv = eye - lower
        pw = lower
        for _ in range(5):
            pw = _dot(pw, pw)
            inv = inv + _dot(inv, pw)
        sol = _dot(inv, jnp.concatenate([v * beta, kb * eg], axis=-1))
        u, w = sol[:, :DN_DV], sol[:, DN_DV:]
        k_dec = k * jnp.exp(g_tot - gcb)
        s = s_ref[h]
        v_new = u - _dot(w, s)
        a_qk = _dot_nt(q, k) * decay
        outs.append(_dot(q * eg, s) + _dot(a_qk, v_new))
        s_ref[h] = s * jnp.exp(g_tot) + _dot_tn(k_dec, v_new)
    o_ref[0] = jnp.concatenate(outs, axis=-1)


def dn_scan_tc(feat, ba, a_log, dt_bias, ctx_len, rev):
    b, lc, _ = feat.shape
    n_all, n_ctx = lc // CHUNK, ctx_len // CHUNK
    dirn = 1 if rev else 0
    cidx = _chunk_index(rev, n_ctx, n_all)
    n_lat = n_all - n_ctx
    smem = pl.BlockSpec(memory_space=pltpu.SMEM)
    return pl.pallas_call(
        functools.partial(_dn_scan_kernel, rev, dirn), grid=(b, n_all),
        in_specs=[smem, smem,
                  pl.BlockSpec((1, CHUNK, feat.shape[2]), lambda bi, j: (bi, cidx(j), 0)),
                  pl.BlockSpec((1, CHUNK, LANES), lambda bi, j: (bi, cidx(j), 0))],
        out_specs=pl.BlockSpec((1, CHUNK, DN_HEADS * DN_DV),
                               lambda bi, j: (bi, _latent_block(rev, n_ctx, n_all)(j), 0)),
        out_shape=jax.ShapeDtypeStruct((b, lc - ctx_len, DN_HEADS * DN_DV), F32),
        scratch_shapes=[pltpu.VMEM((DN_HEADS, DN_DK, DN_DV), F32)],
        compiler_params=_cparams("parallel", "arbitrary"), name="dn_scan_bwd" if rev else "dn_scan_fwd",
    )(a_log, dt_bias, feat, ba)


GLA_SUB = 16


def _gla_scan_kernel(rev, dirn, qk_ref, v_ref, lr_ref, wla_ref, bla_ref, o_ref, s_ref):
    @pl.when(pl.program_id(1) == 0)
    def _():
        s_ref[...] = jnp.zeros_like(s_ref)

    incl, _ = _tri_masks(rev)
    incl_b = incl.astype(BF16)
    hk = GLA_HEADS * GLA_DK
    pre = _dot(lr_ref[0], wla_ref[0]) + bla_ref[0]
    la_all = -_softplus(-pre) * (1.0 / GLA_TAU)
    outs = []
    for h in range(GLA_HEADS):
        q = qk_ref[0, :, h * GLA_DK:(h + 1) * GLA_DK] * GLA_DK ** -0.5
        k = qk_ref[0, :, hk + h * GLA_DK:hk + (h + 1) * GLA_DK]
        v = v_ref[0, :, h * GLA_DV:(h + 1) * GLA_DV]
        la = la_all[:, h * GLA_DK:(h + 1) * GLA_DK]
        bc = _mask_dot(incl_b, la)
        b_tot = jnp.sum(la, axis=0, keepdims=True)
        st = s_ref[h]
        o = _dot_nt(q * jnp.exp(bc), st)
        parts = []
        for i in range(CHUNK // GLA_SUB):
            lo_r, hi_r = i * GLA_SUB, (i + 1) * GLA_SUB
            if rev:
                ref = bc[hi_r - 1:hi_r]
                c0, c1 = lo_r, CHUNK
            else:
                ref = bc[lo_r:lo_r + 1]
                c0, c1 = 0, hi_r
            qi = q[lo_r:hi_r] * jnp.exp(bc[lo_r:hi_r] - ref)
            ki = k[c0:c1] * jnp.exp(ref - bc[c0:c1])
            att = _dot_nt(qi, ki)
            rg = lax.broadcasted_iota(jnp.int32, (GLA_SUB, c1 - c0), 0) + lo_r
            cg = lax.broadcasted_iota(jnp.int32, (GLA_SUB, c1 - c0), 1) + c0
            keep = (cg >= rg) if rev else (cg <= rg)
            parts.append(_dot(jnp.where(keep, att, 0.0), v[c0:c1]))
        outs.append(o + jnp.concatenate(parts, axis=0))
        k_dec = k * jnp.exp(b_tot - bc)
        s_ref[h] = st * jnp.exp(b_tot) + _dot_tn(v, k_dec)
    o_ref[0] = jnp.concatenate(outs, axis=-1)


def gla_scan_tc(qk, v, lr, w_la, b_la, ctx_len, rev):
    b, lc, _ = qk.shape
    n_all, n_ctx = lc // CHUNK, ctx_len // CHUNK
    dirn = 1 if rev else 0
    cidx = _chunk_index(rev, n_ctx, n_all)
    n_lat = n_all - n_ctx
    blk = lambda n: pl.BlockSpec((1, CHUNK, n), lambda bi, j: (bi, cidx(j), 0))
    return pl.pallas_call(
        functools.partial(_gla_scan_kernel, rev, dirn), grid=(b, n_all),
        in_specs=[blk(qk.shape[2]), blk(v.shape[2]), blk(lr.shape[2]),
                  pl.BlockSpec((1,) + w_la.shape[1:], lambda bi, j: (dirn, 0, 0)),
                  pl.BlockSpec((1,) + b_la.shape[1:], lambda bi, j: (dirn, 0, 0))],
        out_specs=pl.BlockSpec((1, CHUNK, GLA_HEADS * GLA_DV),
                               lambda bi, j: (bi, _latent_block(rev, n_ctx, n_all)(j), 0)),
        out_shape=jax.ShapeDtypeStruct((b, lc - ctx_len, GLA_HEADS * GLA_DV), F32),
        scratch_shapes=[pltpu.VMEM((GLA_HEADS, GLA_DV, GLA_DK), F32)],
        compiler_params=_cparams("parallel", "arbitrary"), name="gla_scan_bwd" if rev else "gla_scan_fwd",
    )(qk, v, lr, w_la, b_la)


def _head_norm_gate(o, gate, g, n_heads, dv):
    parts = []
    for h in range(n_heads):
        oh = o[:, h * dv:(h + 1) * dv]
        gh = gate[:, h * dv:(h + 1) * dv]
        yh = oh * lax.rsqrt(jnp.mean(oh * oh, axis=-1, keepdims=True) + EPS) * g
        parts.append(yh * (gh * jax.nn.sigmoid(gh)))
    return parts


def _mix_out_kernel(x_ref, dnf_ref, dnb_ref, z_ref, glf_ref, glb_ref, r_ref, mod_ref, dng_ref,
                    glg_ref, n2g_ref, wout_ref, wq_ref, x1_ref, h2_ref, q_ref):
    parts = (_head_norm_gate(dnf_ref[0] + dnb_ref[0], z_ref[0], dng_ref[...], DN_HEADS, DN_DV)
             + _head_norm_gate(glf_ref[0] + glb_ref[0], r_ref[0], glg_ref[...], GLA_HEADS, GLA_DV))
    y = jnp.dot(jnp.concatenate(parts, axis=-1).astype(BF16), wout_ref[...], preferred_element_type=F32)
    x1 = x_ref[0] + mod_ref[0, 2:3, :] * y
    x1_ref[0] = x1
    n = x1 * lax.rsqrt(jnp.mean(x1 * x1, axis=-1, keepdims=True) + EPS) * n2g_ref[...]
    h2 = n * (1.0 + mod_ref[0, 4:5, :]) + mod_ref[0, 3:4, :]
    h2_ref[0] = h2
    q_ref[0] = jnp.dot(h2.astype(BF16), wq_ref[...], preferred_element_type=F32)


def mix_out_tc(x, dn_f, dn_b, z_cat, gl_f, gl_b, r_cat, mod_l, dn_g, gla_g, n2_g, w_out, w_q, ctx_len):
    b, l, d = x.shape
    off = ctx_len // TOK_TILE
    tok = lambda n: pl.BlockSpec((1, TOK_TILE, n), lambda bi, i: (bi, i, 0))
    cat = lambda n: pl.BlockSpec((1, TOK_TILE, n), lambda bi, i: (bi, i + off, 0))
    full = lambda a: pl.BlockSpec(a.shape, lambda bi, i: (0,) * a.ndim)
    dn_g, gla_g, n2_g = dn_g.reshape(1, -1), gla_g.reshape(1, -1), n2_g.reshape(1, -1)
    nq = w_q.shape[1]
    return pl.pallas_call(
        _mix_out_kernel, grid=(b, l // TOK_TILE),
        in_specs=[tok(d), tok(dn_f.shape[2]), tok(dn_b.shape[2]), cat(z_cat.shape[2]),
                  tok(gl_f.shape[2]), tok(gl_b.shape[2]), cat(r_cat.shape[2]),
                  pl.BlockSpec((1,) + mod_l.shape[1:], lambda bi, i: (bi, 0, 0)),
                  full(dn_g), full(gla_g), full(n2_g), full(w_out), full(w_q)],
        out_specs=[tok(d), tok(d), tok(nq)],
        out_shape=[jax.ShapeDtypeStruct((b, l, d), F32), jax.ShapeDtypeStruct((b, l, d), F32),
                   jax.ShapeDtypeStruct((b, l, nq), F32)],
        compiler_params=_cparams("parallel", "arbitrary"), name="mix_out",
    )(x, dn_f, dn_b, z_cat, gl_f, gl_b, r_cat, mod_l, dn_g, gla_g, n2_g, w_out, w_q)


def _final_kernel(x_ref, y_ref, mod_ref, g_ref, o_ref):
    x = x_ref[0] + mod_ref[0, 5:6, :] * y_ref[0]
    o_ref[0] = x * lax.rsqrt(jnp.mean(x * x, axis=-1, keepdims=True) + EPS) * g_ref[...]


def final_tc(x1, y, mod_l, final_g):
    b, l, d = x1.shape
    tok = pl.BlockSpec((1, TOK_TILE, d), lambda bi, i: (bi, i, 0))
    return pl.pallas_call(
        _final_kernel, grid=(b, l // TOK_TILE),
        in_specs=[tok, tok, pl.BlockSpec((1,) + mod_l.shape[1:], lambda bi, i: (bi, 0, 0)),
                  pl.BlockSpec((1, d), lambda bi, i: (0, 0))],
        out_specs=tok, out_shape=jax.ShapeDtypeStruct((b, l, d), F32),
        compiler_params=_cparams("parallel", "arbitrary"), name="final_norm",
    )(x1, y, mod_l, final_g.reshape(1, d))


def _pad_cols(w, n):
    return jnp.pad(w, ((0, 0), (0, n - w.shape[1])))


BATCH_GROUPS = 2


def forward(x, c, ctx, c_ctx, w_ada, b_ada, norm1_g, norm2_g, w_in, conv_w, dn_a_log,
            dn_dt_bias, dn_norm_g, gla_wa2, gla_ba, gla_norm_g, w_out, peer_wq, peer_keys,
            peer_u, peer_v, final_g):
    b, l, d = x.shape
    c_all = jnp.concatenate([c, c_ctx[None]], axis=0)
    c_all = jnp.pad(c_all, ((0, (-c_all.shape[0]) % SUBLANES), (0, 0)))
    mod = adaln_mod(c_all, w_ada, b_ada)
    mod_l = mod[:b].reshape(b, 6, d)
    mod_c = jnp.broadcast_to(mod[b].reshape(1, 6, d), (b, 6, d))
    modsel = jnp.stack([mod_c[:, 0:2], mod_l[:, 0:2]], axis=1)
    o = DN_QKV
    hv = DN_HEADS * DN_DV
    w_dn_qkv, w_dn_z = w_in[:, :o], w_in[:, o:o + hv]
    w_dn_ba = _pad_cols(w_in[:, o + hv:DN_COLS], LANES)
    g0 = DN_COLS
    gqk, gv = 2 * GLA_HEADS * GLA_DK, GLA_HEADS * GLA_DV
    w_gl_qk, w_gl_v = w_in[:, g0:g0 + gqk], w_in[:, g0 + gqk:g0 + gqk + gv]
    w_gl_r = w_in[:, g0 + gqk + gv:g0 + gqk + 2 * gv]
    w_gl_lr = _pad_cols(w_in[:, g0 + gqk + 2 * gv:], LANES)
    weights = [w.astype(BF16) for w in (w_dn_qkv, w_dn_z, w_dn_ba, w_gl_qk, w_gl_v, w_gl_r, w_gl_lr)]
    w_la = jnp.zeros((2, LANES, GLA_HEADS * GLA_DK), F32)
    for dd in range(2):
        w_la = w_la.at[dd, dd * GLA_LR:(dd + 1) * GLA_LR].set(gla_wa2[dd])
    w_la = w_la.astype(BF16)
    b_la = gla_ba.reshape(2, 1, GLA_HEADS * GLA_DK)
    w_out_b, w_q_b = w_out.astype(BF16), peer_wq.astype(BF16)
    u3 = peer_u.reshape(-1, SUBLANES, LANES)
    v3 = peer_v.reshape(-1, SUBLANES, LANES)

    def group(xg, ctxg, modsel_g, mod_g):
        bg = xg.shape[0]
        ctx_len = ctxg.shape[1]
        rows = l // GRID_W
        xcat = jnp.concatenate([ctxg, xg], axis=1)
        dn_qkv, dn_z, dn_ba, gl_qk, gl_v, gl_r, gl_lr = in_projection(xcat, norm1_g, modsel_g, weights, ctx_len)
        feat = dn_features_tc(dn_qkv, conv_w, ctx_len)
        dn_f = dn_scan_tc(feat, dn_ba, dn_a_log, dn_dt_bias, ctx_len, rev=False)
        dn_b = dn_scan_tc(feat, dn_ba, dn_a_log, dn_dt_bias, ctx_len, rev=True)
        colmaj = lambda t: jnp.concatenate([t[:, :ctx_len], to_col_major(t[:, ctx_len:], rows)], axis=1)
        gqk_c, gv_c, glr_c = colmaj(gl_qk), colmaj(gl_v), colmaj(gl_lr)
        gl_f = from_col_major(gla_scan_tc(gqk_c, gv_c, glr_c, w_la, b_la, ctx_len, rev=False), rows)
        gl_b = from_col_major(gla_scan_tc(gqk_c, gv_c, glr_c, w_la, b_la, ctx_len, rev=True), rows)
        x1, h2, q = mix_out_tc(xg, dn_f, dn_b, dn_z, gl_f, gl_b, gl_r, mod_g, dn_norm_g, gla_norm_g,
                               norm2_g, w_out_b, w_q_b, ctx_len)
        n_tok = bg * l
        idx, gate = peer_select_tc(q.reshape(n_tok, -1), peer_keys)
        y = peer_apply(h2.reshape(n_tok, d), idx, gate, u3, v3)
        return final_tc(x1, y.reshape(bg, l, d), mod_g, final_g)

    bg = b // BATCH_GROUPS
    outs = [group(x[i:i + bg], ctx[i:i + bg], modsel[i:i + bg], mod_l[i:i + bg]) for i in range(0, b, bg)]
    return jnp.concatenate(outs, axis=0)


def _final_norm_kernel(x_ref, g_ref, o_ref):
    xf = x_ref[...]
    y = xf * lax.rsqrt(jnp.mean(xf * xf, axis=-1, keepdims=True) + EPS)
    o_ref[...] = y * g_ref[...]


def kernel(x, c, ctx, c_ctx, w_ada, b_ada, norm1_g, norm2_g, w_in, conv_w, dn_a_log,
           dn_dt_bias, dn_norm_g, gla_wa2, gla_ba, gla_norm_g, w_out, peer_wq, peer_keys,
           peer_u, peer_v, final_g):
    assert w_ada.shape[0] == 1, "single-layer block: the context stream is only consumed, never updated"
    return forward(x, c, ctx, c_ctx, w_ada[0], b_ada[0], norm1_g[0], norm2_g[0], w_in[0], conv_w[0],
                   dn_a_log[0], dn_dt_bias[0], dn_norm_g[0], gla_wa2[0], gla_ba[0], gla_norm_g[0],
                   w_out[0], peer_wq[0], peer_keys[0], peer_u[0], peer_v[0], final_g)
```

```python
import functools

import jax
import jax.numpy as jnp
from jax import lax
from jax.experimental import pallas as pl
from jax.experimental.pallas import tpu as pltpu
from jax.experimental.pallas import tpu_sc as plsc

GRID_W = 64
DN_HEADS = 4
DN_DK = 128
DN_DV = 128
CONV_W = 5
GLA_HEADS = 4
GLA_DK = 64
GLA_DV = 128
GLA_LR = 16
GLA_TAU = 16.0
CHUNK = 64
PEER_HEADS = 8
PEER_NKEYS = 128
PEER_DQ = 256
PEER_TOPK = 16
EPS = 1e-6
DN_QKV = 2 * DN_HEADS * DN_DK + DN_HEADS * DN_DV
DN_COLS = DN_QKV + DN_HEADS * DN_DV + 4 * DN_HEADS

SUBLANES = 8
LANES = 128
SC_LANES = 16
VMEM_LIMIT_BYTES = 48 * 1024 * 1024

TOK_TILE = 256
SELECT_TILE = 256
GLA_SUB = 16
PICK_GROUP = 32
TOK_STEP = 2
ACT_UNROLL = 8
BATCH_GROUPS = 2
PICKS = PEER_HEADS * PEER_TOPK
GROUPS_PER_TOK = PICKS // PICK_GROUP

F32 = jnp.float32
BF16 = jnp.bfloat16


def _cparams(*semantics):
    return pltpu.CompilerParams(dimension_semantics=semantics, vmem_limit_bytes=VMEM_LIMIT_BYTES)


def _dot(a, b):
    return jnp.dot(a.astype(BF16), b.astype(BF16), preferred_element_type=F32)


def _dot_nt(a, b):
    return lax.dot_general(a.astype(BF16), b.astype(BF16), (((1,), (1,)), ((), ())),
                           preferred_element_type=F32)


def _dot_tn(a, b):
    return lax.dot_general(a.astype(BF16), b.astype(BF16), (((0,), (0,)), ((), ())),
                           preferred_element_type=F32)


def _split(x):
    hi = x.astype(BF16)
    return hi, (x - hi.astype(F32)).astype(BF16)


def _mask_dot(mask_bf16, x):
    hi, lo = _split(x)
    return (jnp.dot(mask_bf16, hi, preferred_element_type=F32)
            + jnp.dot(mask_bf16, lo, preferred_element_type=F32))


def _softplus(x):
    return jnp.maximum(x, 0.0) + jnp.log(1.0 + jnp.exp(-jnp.abs(x)))


def _tri_masks(rev):
    r = lax.broadcasted_iota(jnp.int32, (CHUNK, CHUNK), 0)
    c = lax.broadcasted_iota(jnp.int32, (CHUNK, CHUNK), 1)
    d = (c - r) if rev else (r - c)
    return d >= 0, d > 0


def _mod_kernel(c_ref, w_ref, b_ref, o_ref):
    c = c_ref[...]
    s = c * jax.nn.sigmoid(c)
    o_ref[...] = jnp.dot(s, w_ref[...], preferred_element_type=F32,
                         precision=lax.Precision.HIGHEST) + b_ref[...]


def adaln_mod(c_all, w_ada, b_ada):
    r, d = c_all.shape
    n = w_ada.shape[1]
    tn = 512
    return pl.pallas_call(
        _mod_kernel, grid=(n // tn,),
        in_specs=[pl.BlockSpec((r, d), lambda j: (0, 0)),
                  pl.BlockSpec((d, tn), lambda j: (0, j)),
                  pl.BlockSpec((1, tn), lambda j: (0, j))],
        out_specs=pl.BlockSpec((r, tn), lambda j: (0, j)),
        out_shape=jax.ShapeDtypeStruct((r, n), F32),
        compiler_params=_cparams("arbitrary"), name="adaln_mod",
    )(c_all, w_ada, b_ada.reshape(1, n))


def _inproj_kernel(x_ref, g_ref, mod_ref, *refs):
    n_out = len(refs) // 2
    x = x_ref[0]
    y = x * lax.rsqrt(jnp.mean(x * x, axis=-1, keepdims=True) + EPS) * g_ref[...]
    h = (y * (1.0 + mod_ref[0, 1:2, :]) + mod_ref[0, 0:1, :]).astype(BF16)
    for w_ref, o_ref in zip(refs[:n_out], refs[n_out:]):
        o_ref[0] = jnp.dot(h, w_ref[...], preferred_element_type=F32)


def in_projection(x, norm_g, mod, weights):
    b, l, d = x.shape
    w_specs = [pl.BlockSpec(w.shape, lambda bi, i: (0, 0)) for w in weights]
    o_specs = [pl.BlockSpec((1, TOK_TILE, w.shape[1]), lambda bi, i: (bi, i, 0)) for w in weights]
    return pl.pallas_call(
        _inproj_kernel, grid=(b, l // TOK_TILE),
        in_specs=[pl.BlockSpec((1, TOK_TILE, d), lambda bi, i: (bi, i, 0)),
                  pl.BlockSpec((1, d), lambda bi, i: (0, 0)),
                  pl.BlockSpec((1, 2, d), lambda bi, i: (bi, 0, 0))] + w_specs,
        out_specs=o_specs,
        out_shape=[jax.ShapeDtypeStruct((b, l, w.shape[1]), F32) for w in weights],
        compiler_params=_cparams("parallel", "arbitrary"), name="in_projection",
    )(x, norm_g.reshape(1, d), mod, *weights)


def _dn_feature_kernel(x_ref, w_ref, o_ref):
    x = x_ref[0]
    n = x.shape[0]
    t = lax.broadcasted_iota(jnp.int32, (n, 1), 0)
    pad = CONV_W // 2
    acc = w_ref[0, pad:pad + 1, :] * x
    for j in range(CONV_W):
        s = j - pad
        if s == 0:
            continue
        xs = pltpu.roll(x, (-s) % n, axis=0)
        bad = (t < -s) if s < 0 else (t >= n - s)
        acc = acc + w_ref[0, j:j + 1, :] * jnp.where(bad, 0.0, xs)
    y = acc * jax.nn.sigmoid(acc)
    kind = pl.program_id(1) // DN_HEADS
    inv = lax.rsqrt(jnp.sum(y * y, axis=-1, keepdims=True) + EPS)
    scale = jnp.where(kind == 0, inv * DN_DK ** -0.5, jnp.where(kind == 1, inv, 1.0))
    o_ref[0] = y * scale


def dn_features_tc(qkv, conv_w):
    b, l, n = qkv.shape
    nblk = n // LANES
    w = jnp.zeros((nblk, SUBLANES, LANES), F32).at[:, :CONV_W].set(
        conv_w.reshape(CONV_W, nblk, LANES).transpose(1, 0, 2))
    return pl.pallas_call(
        _dn_feature_kernel, grid=(b, nblk),
        in_specs=[pl.BlockSpec((1, l, LANES), lambda bi, j: (bi, 0, j)),
                  pl.BlockSpec((1, SUBLANES, LANES), lambda bi, j: (j, 0, 0))],
        out_specs=pl.BlockSpec((1, l, LANES), lambda bi, j: (bi, 0, j)),
        out_shape=jax.ShapeDtypeStruct((b, l, n), F32),
        compiler_params=_cparams("parallel", "arbitrary"), name="dn_features",
    )(qkv, w)


def _scan_chunks(rev, n_ctx, n_lat):
    if rev:
        ctx = lambda j: jnp.maximum(n_ctx - 1 - j, 0)
        lat = lambda j: jnp.where(j < n_ctx, n_lat - 1, n_lat - 1 - (j - n_ctx))
    else:
        ctx = lambda j: jnp.minimum(j, n_ctx - 1)
        lat = lambda j: jnp.maximum(j - n_ctx, 0)
    return ctx, lat


def _dn_scan_kernel(rev, dirn, n_ctx, alog_ref, dtb_ref, fc_ref, fl_ref, bac_ref, bal_ref, o_ref, s_ref):
    step = pl.program_id(1)

    @pl.when(step == 0)
    def _():
        s_ref[...] = jnp.zeros_like(s_ref)

    in_ctx = step < n_ctx
    f = jnp.where(in_ctx, fc_ref[0], fl_ref[0])
    ba = jnp.where(in_ctx, bac_ref[0], bal_ref[0])
    incl, strict = _tri_masks(rev)
    incl_b = incl.astype(BF16)
    eye = (lax.broadcasted_iota(jnp.int32, (CHUNK, CHUNK), 0)
           == lax.broadcasted_iota(jnp.int32, (CHUNK, CHUNK), 1)).astype(F32)
    lane = lax.broadcasted_iota(jnp.int32, (CHUNK, LANES), 1)
    onehot0 = (lane == 0).astype(BF16)
    hd = DN_HEADS * DN_DK
    outs = []
    for h in range(DN_HEADS):
        q = f[:, h * DN_DK:(h + 1) * DN_DK]
        k = f[:, hd + h * DN_DK:hd + (h + 1) * DN_DK]
        v = f[:, 2 * hd + h * DN_DV:2 * hd + (h + 1) * DN_DV]
        cb = dirn * DN_HEADS + h
        beta = jax.nn.sigmoid(ba[:, cb:cb + 1])
        a = ba[:, 2 * DN_HEADS + cb:2 * DN_HEADS + cb + 1]
        g = -jnp.exp(jnp.full((CHUNK, 1), alog_ref[dirn, h], F32)) * _softplus(a + dtb_ref[dirn, h])
        gmat = jnp.broadcast_to(g, (CHUNK, LANES))
        gcb = _mask_dot(incl_b, gmat)
        g_tot = jnp.sum(gmat, axis=0, keepdims=True)
        hi, lo = _split(gcb)
        g_row = (lax.dot_general(onehot0, hi, (((1,), (1,)), ((), ())), preferred_element_type=F32)
                 + lax.dot_general(onehot0, lo, (((1,), (1,)), ((), ())), preferred_element_type=F32))
        decay = jnp.where(incl, jnp.exp(jnp.where(incl, gcb[:, :CHUNK] - g_row, 0.0)), 0.0)
        kb = k * beta
        lower = jnp.where(strict, _dot_nt(kb, k) * decay, 0.0)
        eg = jnp.exp(gcb)
        inv = eye - lower
        pw = lower
        for _ in range(5):
            pw = _dot(pw, pw)
            inv = inv + _dot(inv, pw)
        sol = _dot(inv, jnp.concatenate([v * beta, kb * eg], axis=-1))
        u, w = sol[:, :DN_DV], sol[:, DN_DV:]
        k_dec = k * jnp.exp(g_tot - gcb)
        s = s_ref[h]
        v_new = u - _dot(w, s)
        a_qk = _dot_nt(q, k) * decay
        outs.append(_dot(q * eg, s) + _dot(a_qk, v_new))
        s_ref[h] = s * jnp.exp(g_tot) + _dot_tn(k_dec, v_new)
    o_ref[0] = jnp.concatenate(outs, axis=-1)


def dn_scan_tc(feat_c, feat_l, ba_c, ba_l, a_log, dt_bias, rev):
    b, l, nf = feat_l.shape
    n_ctx, n_lat = feat_c.shape[1] // CHUNK, l // CHUNK
    dirn = 1 if rev else 0
    cc, lc = _scan_chunks(rev, n_ctx, n_lat)
    smem = pl.BlockSpec(memory_space=pltpu.SMEM)
    return pl.pallas_call(
        functools.partial(_dn_scan_kernel, rev, dirn, n_ctx), grid=(b, n_ctx + n_lat),
        in_specs=[smem, smem,
                  pl.BlockSpec((1, CHUNK, nf), lambda bi, j: (bi, cc(j), 0)),
                  pl.BlockSpec((1, CHUNK, nf), lambda bi, j: (bi, lc(j), 0)),
                  pl.BlockSpec((1, CHUNK, LANES), lambda bi, j: (bi, cc(j), 0)),
                  pl.BlockSpec((1, CHUNK, LANES), lambda bi, j: (bi, lc(j), 0))],
        out_specs=pl.BlockSpec((1, CHUNK, DN_HEADS * DN_DV), lambda bi, j: (bi, lc(j), 0)),
        out_shape=jax.ShapeDtypeStruct((b, l, DN_HEADS * DN_DV), F32),
        scratch_shapes=[pltpu.VMEM((DN_HEADS, DN_DK, DN_DV), F32)],
        compiler_params=_cparams("parallel", "arbitrary"), name="dn_scan_bwd" if rev else "dn_scan_fwd",
    )(a_log, dt_bias, feat_c, feat_l, ba_c, ba_l)


def _from_grid_cols(blk, n):
    cols = blk.shape[1] // n
    return jnp.concatenate([blk[:, i * n:(i + 1) * n] for i in range(cols)], axis=0)


def _gla_scan_kernel(rev, dirn, n_ctx, qkc_ref, vc_ref, lrc_ref, qkl_ref, vl_ref, lrl_ref,
                     wla_ref, bla_ref, o_ref, s_ref):
    step = pl.program_id(1)

    @pl.when(step == 0)
    def _():
        s_ref[...] = jnp.zeros_like(s_ref)

    in_ctx = step < n_ctx
    hk, hv = GLA_HEADS * GLA_DK, GLA_HEADS * GLA_DV
    qk = jnp.where(in_ctx, qkc_ref[0], _from_grid_cols(qkl_ref[0], 2 * hk))
    vv = jnp.where(in_ctx, vc_ref[0], _from_grid_cols(vl_ref[0], hv))
    lr = jnp.where(in_ctx, lrc_ref[0], _from_grid_cols(lrl_ref[0], LANES))
    incl, _ = _tri_masks(rev)
    incl_b = incl.astype(BF16)
    pre = _dot(lr, wla_ref[0]) + bla_ref[0]
    la_all = -_softplus(-pre) * (1.0 / GLA_TAU)
    outs = []
    for h in range(GLA_HEADS):
        q = qk[:, h * GLA_DK:(h + 1) * GLA_DK] * GLA_DK ** -0.5
        k = qk[:, hk + h * GLA_DK:hk + (h + 1) * GLA_DK]
        v = vv[:, h * GLA_DV:(h + 1) * GLA_DV]
        la = la_all[:, h * GLA_DK:(h + 1) * GLA_DK]
        bc = _mask_dot(incl_b, la)
        b_tot = jnp.sum(la, axis=0, keepdims=True)
        st = s_ref[h]
        o = _dot_nt(q * jnp.exp(bc), st)
        parts = []
        for i in range(CHUNK // GLA_SUB):
            lo_r, hi_r = i * GLA_SUB, (i + 1) * GLA_SUB
            if rev:
                ref = bc[hi_r - 1:hi_r]
                c0, c1 = lo_r, CHUNK
            else:
                ref = bc[lo_r:lo_r + 1]
                c0, c1 = 0, hi_r
            qi = q[lo_r:hi_r] * jnp.exp(bc[lo_r:hi_r] - ref)
            ki = k[c0:c1] * jnp.exp(ref - bc[c0:c1])
            att = _dot_nt(qi, ki)
            rg = lax.broadcasted_iota(jnp.int32, (GLA_SUB, c1 - c0), 0) + lo_r
            cg = lax.broadcasted_iota(jnp.int32, (GLA_SUB, c1 - c0), 1) + c0
            keep = (cg >= rg) if rev else (cg <= rg)
            parts.append(_dot(jnp.where(keep, att, 0.0), v[c0:c1]))
        outs.append(o + jnp.concatenate(parts, axis=0))
        k_dec = k * jnp.exp(b_tot - bc)
        s_ref[h] = st * jnp.exp(b_tot) + _dot_tn(v, k_dec)
    o = jnp.concatenate(outs, axis=-1)
    rows = o_ref.shape[1]
    o_ref[0] = jnp.concatenate([o[i * rows:(i + 1) * rows] for i in range(CHUNK // rows)], axis=-1)


def gla_scan_tc(qk_c, v_c, lr_c, qk_l, v_l, lr_l, w_la, b_la, rev):
    b, l, _ = qk_l.shape
    rows = l // GRID_W
    cols = CHUNK // rows
    n_ctx, n_lat = qk_c.shape[1] // CHUNK, l // CHUNK
    dirn = 1 if rev else 0
    cc, lc = _scan_chunks(rev, n_ctx, n_lat)
    hv = GLA_HEADS * GLA_DV
    ctx_blk = lambda a: pl.BlockSpec((1, CHUNK, a.shape[2]), lambda bi, j: (bi, cc(j), 0))
    lat_blk = lambda n: pl.BlockSpec((1, rows, cols * n), lambda bi, j: (bi, 0, lc(j)))
    grid_view = lambda a: a.reshape(b, rows, GRID_W * a.shape[2])
    out = pl.pallas_call(
        functools.partial(_gla_scan_kernel, rev, dirn, n_ctx), grid=(b, n_ctx + n_lat),
        in_specs=[ctx_blk(qk_c), ctx_blk(v_c), ctx_blk(lr_c),
                  lat_blk(qk_l.shape[2]), lat_blk(v_l.shape[2]), lat_blk(lr_l.shape[2]),
                  pl.BlockSpec((1,) + w_la.shape[1:], lambda bi, j: (dirn, 0, 0)),
                  pl.BlockSpec((1,) + b_la.shape[1:], lambda bi, j: (dirn, 0, 0))],
        out_specs=lat_blk(hv),
        out_shape=jax.ShapeDtypeStruct((b, rows, GRID_W * hv), F32),
        scratch_shapes=[pltpu.VMEM((GLA_HEADS, GLA_DV, GLA_DK), F32)],
        compiler_params=_cparams("parallel", "arbitrary"), name="gla_scan_bwd" if rev else "gla_scan_fwd",
    )(qk_c, v_c, lr_c, grid_view(qk_l), grid_view(v_l), grid_view(lr_l), w_la, b_la)
    return out.reshape(b, l, hv)


def _head_norm_gate(o, gate, g, n_heads, dv):
    parts = []
    for h in range(n_heads):
        oh = o[:, h * dv:(h + 1) * dv]
        gh = gate[:, h * dv:(h + 1) * dv]
        yh = oh * lax.rsqrt(jnp.mean(oh * oh, axis=-1, keepdims=True) + EPS) * g
        parts.append(yh * (gh * jax.nn.sigmoid(gh)))
    return parts


def _mix_out_kernel(x_ref, dnf_ref, dnb_ref, z_ref, glf_ref, glb_ref, r_ref, mod_ref, dng_ref,
                    glg_ref, n2g_ref, wout_ref, wq_ref, x1_ref, h2_ref, q_ref):
    parts = (_head_norm_gate(dnf_ref[0] + dnb_ref[0], z_ref[0], dng_ref[...], DN_HEADS, DN_DV)
             + _head_norm_gate(glf_ref[0] + glb_ref[0], r_ref[0], glg_ref[...], GLA_HEADS, GLA_DV))
    y = jnp.dot(jnp.concatenate(parts, axis=-1).astype(BF16), wout_ref[...], preferred_element_type=F32)
    x1 = x_ref[0] + mod_ref[0, 2:3, :] * y
    x1_ref[0] = x1
    n = x1 * lax.rsqrt(jnp.mean(x1 * x1, axis=-1, keepdims=True) + EPS) * n2g_ref[...]
    h2 = n * (1.0 + mod_ref[0, 4:5, :]) + mod_ref[0, 3:4, :]
    h2_ref[0] = h2
    q_ref[0] = jnp.dot(h2.astype(BF16), wq_ref[...], preferred_element_type=F32)


def mix_out_tc(x, dn_f, dn_b, z, gl_f, gl_b, r, mod_l, dn_g, gla_g, n2_g, w_out, w_q):
    b, l, d = x.shape
    tok = lambda n: pl.BlockSpec((1, TOK_TILE, n), lambda bi, i: (bi, i, 0))
    full = lambda a: pl.BlockSpec(a.shape, lambda bi, i: (0,) * a.ndim)
    dn_g, gla_g, n2_g = dn_g.reshape(1, -1), gla_g.reshape(1, -1), n2_g.reshape(1, -1)
    nq = w_q.shape[1]
    return pl.pallas_call(
        _mix_out_kernel, grid=(b, l // TOK_TILE),
        in_specs=[tok(d), tok(dn_f.shape[2]), tok(dn_b.shape[2]), tok(z.shape[2]),
                  tok(gl_f.shape[2]), tok(gl_b.shape[2]), tok(r.shape[2]),
                  pl.BlockSpec((1,) + mod_l.shape[1:], lambda bi, i: (bi, 0, 0)),
                  full(dn_g), full(gla_g), full(n2_g), full(w_out), full(w_q)],
        out_specs=[tok(d), tok(d), tok(nq)],
        out_shape=[jax.ShapeDtypeStruct((b, l, d), F32), jax.ShapeDtypeStruct((b, l, d), F32),
                   jax.ShapeDtypeStruct((b, l, nq), F32)],
        compiler_params=_cparams("parallel", "arbitrary"), name="mix_out",
    )(x, dn_f, dn_b, z, gl_f, gl_b, r, mod_l, dn_g, gla_g, n2_g, w_out, w_q)


def _top_rows(s, k, payload=None):
    n = s.shape[0]
    row = lax.broadcasted_iota(jnp.int32, s.shape, 0).astype(F32)
    vals, picked = [], []
    for _ in range(k):
        m = jnp.max(s, axis=0, keepdims=True)
        first = jnp.min(jnp.where(s == m, row, float(n)), axis=0, keepdims=True)
        sel = row == first
        vals.append(m)
        if payload is None:
            picked.append(first)
        else:
            picked.append(jnp.max(jnp.where(sel, payload, -1.0), axis=0, keepdims=True))
        s = jnp.where(sel, -jnp.inf, s)
    return jnp.concatenate(vals, axis=0), jnp.concatenate(picked, axis=0)


def _select_kernel(q_ref, k_ref, idx_ref, gate_ref, idx_s, gate_s):
    half = PEER_DQ // 2

    def head(h, carry):
        tops = []
        for p in range(2):
            qp = q_ref[:, pl.ds(pl.multiple_of(h * PEER_DQ + p * half, half), half)]
            s = lax.dot_general(k_ref[h, p], qp, (((1,), (1,)), ((), ())),
                                preferred_element_type=F32,
                                precision=lax.Precision.HIGHEST)
            tops.append(_top_rows(s, PEER_TOPK))
        (s0, i0), (s1, i1) = tops
        cand_s = jnp.concatenate([s0[i:i + 1] + s1 for i in range(PEER_TOPK)], axis=0)
        cand_i = jnp.concatenate([i0[i:i + 1] * float(PEER_NKEYS) + i1 for i in range(PEER_TOPK)], axis=0)
        best_s, idx = _top_rows(cand_s, PEER_TOPK, payload=cand_i)
        e = jnp.exp(best_s - best_s[0:1])
        r0 = pl.multiple_of(h * PEER_TOPK, PEER_TOPK)
        idx_s[pl.ds(r0, PEER_TOPK), :] = idx
        gate_s[pl.ds(r0, PEER_TOPK), :] = e / jnp.sum(e, axis=0, keepdims=True)
        return carry

    lax.fori_loop(0, PEER_HEADS, head, 0)
    idx_ref[...] = idx_s[...].T.astype(jnp.int32)
    gate_ref[...] = gate_s[...].T


def peer_select_tc(q, keys):
    n_tok = q.shape[0]
    out_spec = pl.BlockSpec((SELECT_TILE, PICKS), lambda i: (i, 0))
    return pl.pallas_call(
        _select_kernel,
        grid=(n_tok // SELECT_TILE,),
        in_specs=[pl.BlockSpec((SELECT_TILE, q.shape[1]), lambda i: (i, 0)),
                  pl.BlockSpec(keys.shape, lambda i: (0, 0, 0, 0))],
        out_specs=[out_spec, out_spec],
        out_shape=[jax.ShapeDtypeStruct((n_tok, PICKS), jnp.int32),
                   jax.ShapeDtypeStruct((n_tok, PICKS), F32)],
        scratch_shapes=[pltpu.VMEM((PICKS, SELECT_TILE), F32), pltpu.VMEM((PICKS, SELECT_TILE), F32)],
        compiler_params=_cparams("parallel"), name="peer_select",
    )(q, keys)


def _sc_mesh():
    return plsc.VectorSubcoreMesh(core_axis_name="c", subcore_axis_name="s")


def _sc_pipeline(body, n_steps, in_specs, out_specs, operands):
    pltpu.emit_pipeline(
        body, grid=(n_steps,), in_specs=in_specs, out_specs=out_specs,
        core_axis_name=("c", "s"), dimension_semantics=(pltpu.PARALLEL,),
        trace_scopes=False,
    )(*operands)


def peer_act_partial_sc(u3, idx2, h3):
    n_groups, n_tok = idx2.shape[0], h3.shape[0]
    nsub = TOK_STEP * GROUPS_PER_TOK
    n_chunks = SUBLANES * LANES // SC_LANES

    @functools.partial(
        pl.kernel, mesh=_sc_mesh(),
        out_type=jax.ShapeDtypeStruct((n_groups, PICK_GROUP * SC_LANES), F32),
        scratch_types=[pltpu.VMEM((2, PICK_GROUP, SUBLANES, LANES), F32),
                       pltpu.SemaphoreType.DMA((2,))],
    )
    def k(u_hbm, i_hbm, h_hbm, o_hbm, rows, sems):
        def body(i_v, h_v, o_v):
            def fetch(j, slot):
                return pltpu.make_async_copy(u_hbm.at[i_v.at[j]], rows.at[slot], sems.at[slot])

            fetch(0, 0).start()

            def sub(j, carry):
                slot = j % 2

                @pl.when(j + 1 < nsub)
                def _():
                    fetch(j + 1, 1 - slot).start()

                fetch(j, slot).wait()
                t = j // GROUPS_PER_TOK

                def picks(g, carry2):
                    kb = g * ACT_UNROLL
                    accs = [None] * ACT_UNROLL
                    for c in range(n_chunks):
                        s, l = divmod(c, LANES // SC_LANES)
                        hc = h_v[t, s, pl.ds(l * SC_LANES, SC_LANES)]
                        for i in range(ACT_UNROLL):
                            p = rows[slot, kb + i, s, pl.ds(l * SC_LANES, SC_LANES)] * hc
                            accs[i] = p if accs[i] is None else accs[i] + p
                    for i in range(ACT_UNROLL):
                        o_v[j, pl.ds((kb + i) * SC_LANES, SC_LANES)] = accs[i]
                    return carry2

                lax.fori_loop(0, PICK_GROUP // ACT_UNROLL, picks, 0)
                return carry

            lax.fori_loop(0, nsub, sub, 0)

        _sc_pipeline(
            body, n_tok // TOK_STEP,
            [pl.BlockSpec((nsub, PICK_GROUP), lambda i: (i, 0)),
             pl.BlockSpec((TOK_STEP, SUBLANES, LANES), lambda i: (i, 0, 0))],
            [pl.BlockSpec((nsub, PICK_GROUP * SC_LANES), lambda i: (i, 0))],
            (i_hbm, h_hbm, o_hbm))

    return k(u3, idx2, h3)


def peer_combine_sc(v3, idx2, coef_b):
    n_groups = idx2.shape[0]
    n_tok = n_groups // GROUPS_PER_TOK
    nsub = TOK_STEP * GROUPS_PER_TOK
    n_chunks = SUBLANES * LANES // SC_LANES

    @functools.partial(
        pl.kernel, mesh=_sc_mesh(),
        out_type=jax.ShapeDtypeStruct((n_tok, SUBLANES, LANES), F32),
        scratch_types=[pltpu.VMEM((2, PICK_GROUP, SUBLANES, LANES), F32),
                       pltpu.SemaphoreType.DMA((2,))],
    )
    def k(v_hbm, i_hbm, c_hbm, o_hbm, rows, sems):
        def body(i_v, c_v, o_v):
            def fetch(j, slot):
                return pltpu.make_async_copy(v_hbm.at[i_v.at[j]], rows.at[slot], sems.at[slot])

            fetch(0, 0).start()
            for j in range(nsub):
                slot = j % 2
                if j + 1 < nsub:
                    fetch(j + 1, 1 - slot).start()
                fetch(j, slot).wait()
                t = j // GROUPS_PER_TOK
                first = j % GROUPS_PER_TOK == 0
                cks = [c_v[j, pl.ds(kk * SC_LANES, SC_LANES)] for kk in range(PICK_GROUP)]

                def chunk(c, carry, slot=slot, t=t, first=first, cks=cks):
                    s = c // (LANES // SC_LANES)
                    l = (c % (LANES // SC_LANES)) * SC_LANES
                    parts = [cks[kk] * rows[slot, kk, s, pl.ds(l, SC_LANES)] for kk in range(4)]
                    for kk in range(4, PICK_GROUP):
                        parts[kk % 4] = parts[kk % 4] + cks[kk] * rows[slot, kk, s, pl.ds(l, SC_LANES)]
                    tot = (parts[0] + parts[1]) + (parts[2] + parts[3])
                    if not first:
                        tot = tot + o_v[t, s, pl.ds(l, SC_LANES)]
                    o_v[t, s, pl.ds(l, SC_LANES)] = tot
                    return carry

                lax.fori_loop(0, n_chunks, chunk, 0)

        _sc_pipeline(
            body, n_tok // TOK_STEP,
            [pl.BlockSpec((nsub, PICK_GROUP), lambda i: (i, 0)),
             pl.BlockSpec((nsub, PICK_GROUP * SC_LANES), lambda i: (i, 0))],
            [pl.BlockSpec((TOK_STEP, SUBLANES, LANES), lambda i: (i, 0, 0))],
            (i_hbm, c_hbm, o_hbm))

    return k(v3, idx2, coef_b)


def _segment_matrix():
    r = lax.broadcasted_iota(jnp.int32, (PICK_GROUP * SC_LANES, PICK_GROUP), 0) // SC_LANES
    c = lax.broadcasted_iota(jnp.int32, (PICK_GROUP * SC_LANES, PICK_GROUP), 1)
    return (r == c).astype(F32)


def _coef_kernel(part_ref, gate_ref, o_ref):
    seg = _segment_matrix()
    act = jnp.dot(part_ref[...], seg, preferred_element_type=F32, precision=lax.Precision.HIGHEST)
    coef = gate_ref[...] * (0.5 * act * (1.0 + lax.erf(act * (2.0 ** -0.5))))
    o_ref[...] = lax.dot_general(coef, seg, (((1,), (1,)), ((), ())), preferred_element_type=F32,
                                 precision=lax.Precision.HIGHEST)


def peer_coef_tc(part, gate2):
    n_groups, width = part.shape
    tile = 1024
    return pl.pallas_call(
        _coef_kernel,
        grid=(n_groups // tile,),
        in_specs=[pl.BlockSpec((tile, width), lambda i: (i, 0)),
                  pl.BlockSpec((tile, PICK_GROUP), lambda i: (i, 0))],
        out_specs=pl.BlockSpec((tile, width), lambda i: (i, 0)),
        out_shape=jax.ShapeDtypeStruct((n_groups, width), F32),
        compiler_params=_cparams("parallel"), name="peer_coef",
    )(part, gate2)


def peer_apply(h, idx, gate, u3, v3):
    n_tok, d = h.shape
    idx2 = idx.reshape(n_tok * GROUPS_PER_TOK, PICK_GROUP)
    gate2 = gate.reshape(n_tok * GROUPS_PER_TOK, PICK_GROUP)
    part = peer_act_partial_sc(u3, idx2, h.reshape(n_tok, SUBLANES, LANES))
    coef_b = peer_coef_tc(part, gate2)
    return peer_combine_sc(v3, idx2, coef_b).reshape(n_tok, d)


def _final_kernel(x_ref, y_ref, mod_ref, g_ref, o_ref):
    x = x_ref[0] + mod_ref[0, 5:6, :] * y_ref[0]
    o_ref[0] = x * lax.rsqrt(jnp.mean(x * x, axis=-1, keepdims=True) + EPS) * g_ref[...]


def final_tc(x1, y, mod_l, final_g):
    b, l, d = x1.shape
    tok = pl.BlockSpec((1, TOK_TILE, d), lambda bi, i: (bi, i, 0))
    return pl.pallas_call(
        _final_kernel, grid=(b, l // TOK_TILE),
        in_specs=[tok, tok, pl.BlockSpec((1,) + mod_l.shape[1:], lambda bi, i: (bi, 0, 0)),
                  pl.BlockSpec((1, d), lambda bi, i: (0, 0))],
        out_specs=tok, out_shape=jax.ShapeDtypeStruct((b, l, d), F32),
        compiler_params=_cparams("parallel", "arbitrary"), name="final_norm",
    )(x1, y, mod_l, final_g.reshape(1, d))


def _pad_cols(w, n):
    return jnp.pad(w, ((0, 0), (0, n - w.shape[1])))


def forward(x, c, ctx, c_ctx, w_ada, b_ada, norm1_g, norm2_g, w_in, conv_w, dn_a_log,
            dn_dt_bias, dn_norm_g, gla_wa2, gla_ba, gla_norm_g, w_out, peer_wq, peer_keys,
            peer_u, peer_v, final_g):
    b, l, d = x.shape
    c_all = jnp.concatenate([c, c_ctx[None]], axis=0)
    c_all = jnp.pad(c_all, ((0, (-c_all.shape[0]) % SUBLANES), (0, 0)))
    mod = adaln_mod(c_all, w_ada, b_ada)
    mod_l = mod[:b].reshape(b, 6, d)
    mod_c = jnp.broadcast_to(mod[b].reshape(1, 6, d), (b, 6, d))
    o = DN_QKV
    hv = DN_HEADS * DN_DV
    w_dn_qkv, w_dn_z = w_in[:, :o], w_in[:, o:o + hv]
    w_dn_ba = _pad_cols(w_in[:, o + hv:DN_COLS], LANES)
    g0 = DN_COLS
    gqk, gv = 2 * GLA_HEADS * GLA_DK, GLA_HEADS * GLA_DV
    w_gl_qk, w_gl_v = w_in[:, g0:g0 + gqk], w_in[:, g0 + gqk:g0 + gqk + gv]
    w_gl_r = w_in[:, g0 + gqk + gv:g0 + gqk + 2 * gv]
    w_gl_lr = _pad_cols(w_in[:, g0 + gqk + 2 * gv:], LANES)
    w_lat = [w.astype(BF16) for w in (w_dn_qkv, w_dn_ba, w_gl_qk, w_gl_v, w_gl_lr, w_dn_z, w_gl_r)]
    w_ctx = w_lat[:5]
    w_la = jnp.zeros((2, LANES, GLA_HEADS * GLA_DK), F32)
    for dd in range(2):
        w_la = w_la.at[dd, dd * GLA_LR:(dd + 1) * GLA_LR].set(gla_wa2[dd])
    w_la = w_la.astype(BF16)
    b_la = gla_ba.reshape(2, 1, GLA_HEADS * GLA_DK)
    w_out_b, w_q_b = w_out.astype(BF16), peer_wq.astype(BF16)
    u3 = peer_u.reshape(-1, SUBLANES, LANES)
    v3 = peer_v.reshape(-1, SUBLANES, LANES)

    def group(xg, ctxg, mod_cg, mod_g):
        bg = xg.shape[0]
        c_qkv, c_ba, c_qk, c_v, c_lr = in_projection(ctxg, norm1_g, mod_cg[:, 0:2], w_ctx)
        l_qkv, l_ba, l_qk, l_v, l_lr, l_z, l_r = in_projection(xg, norm1_g, mod_g[:, 0:2], w_lat)
        feat_c, feat_l = dn_features_tc(c_qkv, conv_w), dn_features_tc(l_qkv, conv_w)
        dn_f = dn_scan_tc(feat_c, feat_l, c_ba, l_ba, dn_a_log, dn_dt_bias, rev=False)
        dn_b = dn_scan_tc(feat_c, feat_l, c_ba, l_ba, dn_a_log, dn_dt_bias, rev=True)
        gl_f = gla_scan_tc(c_qk, c_v, c_lr, l_qk, l_v, l_lr, w_la, b_la, rev=False)
        gl_b = gla_scan_tc(c_qk, c_v, c_lr, l_qk, l_v, l_lr, w_la, b_la, rev=True)
        x1, h2, q = mix_out_tc(xg, dn_f, dn_b, l_z, gl_f, gl_b, l_r, mod_g, dn_norm_g, gla_norm_g,
                               norm2_g, w_out_b, w_q_b)
        n_tok = bg * l
        idx, gate = peer_select_tc(q.reshape(n_tok, -1), peer_keys)
        y = peer_apply(h2.reshape(n_tok, d), idx, gate, u3, v3)
        return final_tc(x1, y.reshape(bg, l, d), mod_g, final_g)

    bg = b // BATCH_GROUPS
    outs = [group(x[i:i + bg], ctx[i:i + bg], mod_c[i:i + bg], mod_l[i:i + bg]) for i in range(0, b, bg)]
    return jnp.concatenate(outs, axis=0)


def kernel(x, c, ctx, c_ctx, w_ada, b_ada, norm1_g, norm2_g, w_in, conv_w, dn_a_log,
           dn_dt_bias, dn_norm_g, gla_wa2, gla_ba, gla_norm_g, w_out, peer_wq, peer_keys,
           peer_u, peer_v, final_g):
    assert w_ada.shape[0] == 1, "single-layer block: the context stream is only consumed, never updated"
    return forward(x, c, ctx, c_ctx, w_ada[0], b_ada[0], norm1_g[0], norm2_g[0], w_in[0], conv_w[0],
                   dn_a_log[0], dn_dt_bias[0], dn_norm_g[0], gla_wa2[0], gla_ba[0], gla_norm_g[0],
                   w_out[0], peer_wq[0], peer_keys[0], peer_u[0], peer_v[0], final_g)
```

```python
import functools

import jax
import jax.numpy as jnp
from jax import lax
from jax.experimental import pallas as pl
from jax.experimental.pallas import tpu as pltpu
from jax.experimental.pallas import tpu_sc as plsc

GRID_W = 64
DN_HEADS = 4
DN_DK = 128
DN_DV = 128
CONV_W = 5
GLA_HEADS = 4
GLA_DK = 64
GLA_DV = 128
GLA_LR = 16
GLA_TAU = 16.0
CHUNK = 64
PEER_HEADS = 8
PEER_NKEYS = 128
PEER_DQ = 256
PEER_TOPK = 16
EPS = 1e-6
DN_QKV = 2 * DN_HEADS * DN_DK + DN_HEADS * DN_DV
DN_COLS = DN_QKV + DN_HEADS * DN_DV + 4 * DN_HEADS

SUBLANES = 8
LANES = 128
SC_LANES = 16
VMEM_LIMIT_BYTES = 48 * 1024 * 1024

TOK_TILE = 256
SELECT_TILE = 256
GLA_SUB = 16
PICK_GROUP = 32
TOK_STEP = 2
ACT_UNROLL = 8
BATCH_GROUPS = 4
PICKS = PEER_HEADS * PEER_TOPK
GROUPS_PER_TOK = PICKS // PICK_GROUP

F32 = jnp.float32
BF16 = jnp.bfloat16


def _cparams(*semantics):
    return pltpu.CompilerParams(dimension_semantics=semantics, vmem_limit_bytes=VMEM_LIMIT_BYTES)


def _dot(a, b):
    return jnp.dot(a.astype(BF16), b.astype(BF16), preferred_element_type=F32)


def _dot_nt(a, b):
    return lax.dot_general(a.astype(BF16), b.astype(BF16), (((1,), (1,)), ((), ())),
                           preferred_element_type=F32)


def _dot_tn(a, b):
    return lax.dot_general(a.astype(BF16), b.astype(BF16), (((0,), (0,)), ((), ())),
                           preferred_element_type=F32)


def _split(x):
    hi = x.astype(BF16)
    return hi, (x - hi.astype(F32)).astype(BF16)


def _mask_dot(mask_bf16, x):
    hi, lo = _split(x)
    return (jnp.dot(mask_bf16, hi, preferred_element_type=F32)
            + jnp.dot(mask_bf16, lo, preferred_element_type=F32))


def _softplus(x):
    return jnp.maximum(x, 0.0) + jnp.log(1.0 + jnp.exp(-jnp.abs(x)))


def _tri_masks(rev):
    r = lax.broadcasted_iota(jnp.int32, (CHUNK, CHUNK), 0)
    c = lax.broadcasted_iota(jnp.int32, (CHUNK, CHUNK), 1)
    d = (c - r) if rev else (r - c)
    return d >= 0, d > 0


def _mod_kernel(c_ref, w_ref, b_ref, o_ref):
    c = c_ref[...]
    s = c * jax.nn.sigmoid(c)
    o_ref[...] = jnp.dot(s, w_ref[...], preferred_element_type=F32,
                         precision=lax.Precision.HIGHEST) + b_ref[...]


def adaln_mod(c_all, w_ada, b_ada):
    r, d = c_all.shape
    n = w_ada.shape[1]
    tn = 512
    return pl.pallas_call(
        _mod_kernel, grid=(n // tn,),
        in_specs=[pl.BlockSpec((r, d), lambda j: (0, 0)),
                  pl.BlockSpec((d, tn), lambda j: (0, j)),
                  pl.BlockSpec((1, tn), lambda j: (0, j))],
        out_specs=pl.BlockSpec((r, tn), lambda j: (0, j)),
        out_shape=jax.ShapeDtypeStruct((r, n), F32),
        compiler_params=_cparams("arbitrary"), name="adaln_mod",
    )(c_all, w_ada, b_ada.reshape(1, n))


def _inproj_kernel(x_ref, g_ref, mod_ref, *refs):
    n_out = len(refs) // 2
    x = x_ref[0]
    y = x * lax.rsqrt(jnp.mean(x * x, axis=-1, keepdims=True) + EPS) * g_ref[...]
    h = (y * (1.0 + mod_ref[0, 1:2, :]) + mod_ref[0, 0:1, :]).astype(BF16)
    for w_ref, o_ref in zip(refs[:n_out], refs[n_out:]):
        o_ref[0] = jnp.dot(h, w_ref[...], preferred_element_type=F32)


def in_projection(x, norm_g, mod, weights):
    b, l, d = x.shape
    w_specs = [pl.BlockSpec(w.shape, lambda bi, i: (0, 0)) for w in weights]
    o_specs = [pl.BlockSpec((1, TOK_TILE, w.shape[1]), lambda bi, i: (bi, i, 0)) for w in weights]
    return pl.pallas_call(
        _inproj_kernel, grid=(b, l // TOK_TILE),
        in_specs=[pl.BlockSpec((1, TOK_TILE, d), lambda bi, i: (bi, i, 0)),
                  pl.BlockSpec((1, d), lambda bi, i: (0, 0)),
                  pl.BlockSpec((1, 2, d), lambda bi, i: (bi, 0, 0))] + w_specs,
        out_specs=o_specs,
        out_shape=[jax.ShapeDtypeStruct((b, l, w.shape[1]), F32) for w in weights],
        compiler_params=_cparams("parallel", "arbitrary"), name="in_projection",
    )(x, norm_g.reshape(1, d), mod, *weights)


def _dn_feature_kernel(x_ref, w_ref, o_ref):
    x = x_ref[0]
    n = x.shape[0]
    t = lax.broadcasted_iota(jnp.int32, (n, 1), 0)
    pad = CONV_W // 2
    acc = w_ref[0, pad:pad + 1, :] * x
    for j in range(CONV_W):
        s = j - pad
        if s == 0:
            continue
        xs = pltpu.roll(x, (-s) % n, axis=0)
        bad = (t < -s) if s < 0 else (t >= n - s)
        acc = acc + w_ref[0, j:j + 1, :] * jnp.where(bad, 0.0, xs)
    y = acc * jax.nn.sigmoid(acc)
    kind = pl.program_id(1) // DN_HEADS
    inv = lax.rsqrt(jnp.sum(y * y, axis=-1, keepdims=True) + EPS)
    scale = jnp.where(kind == 0, inv * DN_DK ** -0.5, jnp.where(kind == 1, inv, 1.0))
    o_ref[0] = y * scale


def dn_features_tc(qkv, conv_w):
    b, l, n = qkv.shape
    nblk = n // LANES
    w = jnp.zeros((nblk, SUBLANES, LANES), F32).at[:, :CONV_W].set(
        conv_w.reshape(CONV_W, nblk, LANES).transpose(1, 0, 2))
    return pl.pallas_call(
        _dn_feature_kernel, grid=(b, nblk),
        in_specs=[pl.BlockSpec((1, l, LANES), lambda bi, j: (bi, 0, j)),
                  pl.BlockSpec((1, SUBLANES, LANES), lambda bi, j: (j, 0, 0))],
        out_specs=pl.BlockSpec((1, l, LANES), lambda bi, j: (bi, 0, j)),
        out_shape=jax.ShapeDtypeStruct((b, l, n), F32),
        compiler_params=_cparams("parallel", "arbitrary"), name="dn_features",
    )(qkv, w)


def _scan_chunks(rev, n_ctx, n_lat):
    if rev:
        ctx = lambda j: jnp.maximum(n_ctx - 1 - j, 0)
        lat = lambda j: jnp.where(j < n_ctx, n_lat - 1, n_lat - 1 - (j - n_ctx))
    else:
        ctx = lambda j: jnp.minimum(j, n_ctx - 1)
        lat = lambda j: jnp.maximum(j - n_ctx, 0)
    return ctx, lat


def _dn_scan_kernel(rev, dirn, n_ctx, alog_ref, dtb_ref, fc_ref, fl_ref, bac_ref, bal_ref, o_ref, s_ref):
    step = pl.program_id(1)

    @pl.when(step == 0)
    def _():
        s_ref[...] = jnp.zeros_like(s_ref)

    in_ctx = step < n_ctx
    f = jnp.where(in_ctx, fc_ref[0], fl_ref[0])
    ba = jnp.where(in_ctx, bac_ref[0], bal_ref[0])
    incl, strict = _tri_masks(rev)
    incl_b = incl.astype(BF16)
    eye = (lax.broadcasted_iota(jnp.int32, (CHUNK, CHUNK), 0)
           == lax.broadcasted_iota(jnp.int32, (CHUNK, CHUNK), 1)).astype(F32)
    lane = lax.broadcasted_iota(jnp.int32, (CHUNK, LANES), 1)
    onehot0 = (lane == 0).astype(BF16)
    hd = DN_HEADS * DN_DK
    outs = []
    for h in range(DN_HEADS):
        q = f[:, h * DN_DK:(h + 1) * DN_DK]
        k = f[:, hd + h * DN_DK:hd + (h + 1) * DN_DK]
        v = f[:, 2 * hd + h * DN_DV:2 * hd + (h + 1) * DN_DV]
        cb = dirn * DN_HEADS + h
        beta = jax.nn.sigmoid(ba[:, cb:cb + 1])
        a = ba[:, 2 * DN_HEADS + cb:2 * DN_HEADS + cb + 1]
        g = -jnp.exp(jnp.full((CHUNK, 1), alog_ref[dirn, h], F32)) * _softplus(a + dtb_ref[dirn, h])
        gmat = jnp.broadcast_to(g, (CHUNK, LANES))
        gcb = _mask_dot(incl_b, gmat)
        g_tot = jnp.sum(gmat, axis=0, keepdims=True)
        hi, lo = _split(gcb)
        g_row = (lax.dot_general(onehot0, hi, (((1,), (1,)), ((), ())), preferred_element_type=F32)
                 + lax.dot_general(onehot0, lo, (((1,), (1,)), ((), ())), preferred_element_type=F32))
        decay = jnp.where(incl, jnp.exp(jnp.where(incl, gcb[:, :CHUNK] - g_row, 0.0)), 0.0)
        kb = k * beta
        lower = jnp.where(strict, _dot_nt(kb, k) * decay, 0.0)
        eg = jnp.exp(gcb)
        inv = eye - lower
        pw = lower
        for _ in range(5):
            pw = _dot(pw, pw)
            inv = inv + _dot(inv, pw)
        sol = _dot(inv, jnp.concatenate([v * beta, kb * eg], axis=-1))
        u, w = sol[:, :DN_DV], sol[:, DN_DV:]
        k_dec = k * jnp.exp(g_tot - gcb)
        s = s_ref[h]
        v_new = u - _dot(w, s)
        a_qk = _dot_nt(q, k) * decay
        outs.append(_dot(q * eg, s) + _dot(a_qk, v_new))
        s_ref[h] = s * jnp.exp(g_tot) + _dot_tn(k_dec, v_new)
    o_ref[0] = jnp.concatenate(outs, axis=-1)


def dn_scan_tc(feat_c, feat_l, ba_c, ba_l, a_log, dt_bias, rev):
    b, l, nf = feat_l.shape
    n_ctx, n_lat = feat_c.shape[1] // CHUNK, l // CHUNK
    dirn = 1 if rev else 0
    cc, lc = _scan_chunks(rev, n_ctx, n_lat)
    smem = pl.BlockSpec(memory_space=pltpu.SMEM)
    return pl.pallas_call(
        functools.partial(_dn_scan_kernel, rev, dirn, n_ctx), grid=(b, n_ctx + n_lat),
        in_specs=[smem, smem,
                  pl.BlockSpec((1, CHUNK, nf), lambda bi, j: (bi, cc(j), 0)),
                  pl.BlockSpec((1, CHUNK, nf), lambda bi, j: (bi, lc(j), 0)),
                  pl.BlockSpec((1, CHUNK, LANES), lambda bi, j: (bi, cc(j), 0)),
                  pl.BlockSpec((1, CHUNK, LANES), lambda bi, j: (bi, lc(j), 0))],
        out_specs=pl.BlockSpec((1, CHUNK, DN_HEADS * DN_DV), lambda bi, j: (bi, lc(j), 0)),
        out_shape=jax.ShapeDtypeStruct((b, l, DN_HEADS * DN_DV), F32),
        scratch_shapes=[pltpu.VMEM((DN_HEADS, DN_DK, DN_DV), F32)],
        compiler_params=_cparams("parallel", "arbitrary"), name="dn_scan_bwd" if rev else "dn_scan_fwd",
    )(a_log, dt_bias, feat_c, feat_l, ba_c, ba_l)


def _from_grid_cols(blk, n):
    cols = blk.shape[1] // n
    return jnp.concatenate([blk[:, i * n:(i + 1) * n] for i in range(cols)], axis=0)


def _gla_scan_kernel(rev, dirn, n_ctx, qkc_ref, vc_ref, lrc_ref, qkl_ref, vl_ref, lrl_ref,
                     wla_ref, bla_ref, o_ref, s_ref):
    step = pl.program_id(1)

    @pl.when(step == 0)
    def _():
        s_ref[...] = jnp.zeros_like(s_ref)

    in_ctx = step < n_ctx
    hk, hv = GLA_HEADS * GLA_DK, GLA_HEADS * GLA_DV
    qk = jnp.where(in_ctx, qkc_ref[0], _from_grid_cols(qkl_ref[0], 2 * hk))
    vv = jnp.where(in_ctx, vc_ref[0], _from_grid_cols(vl_ref[0], hv))
    lr = jnp.where(in_ctx, lrc_ref[0], _from_grid_cols(lrl_ref[0], LANES))
    incl, _ = _tri_masks(rev)
    incl_b = incl.astype(BF16)
    pre = _dot(lr, wla_ref[0]) + bla_ref[0]
    la_all = -_softplus(-pre) * (1.0 / GLA_TAU)
    outs = []
    for h in range(GLA_HEADS):
        q = qk[:, h * GLA_DK:(h + 1) * GLA_DK] * GLA_DK ** -0.5
        k = qk[:, hk + h * GLA_DK:hk + (h + 1) * GLA_DK]
        v = vv[:, h * GLA_DV:(h + 1) * GLA_DV]
        la = la_all[:, h * GLA_DK:(h + 1) * GLA_DK]
        bc = _mask_dot(incl_b, la)
        b_tot = jnp.sum(la, axis=0, keepdims=True)
        st = s_ref[h]
        o = _dot_nt(q * jnp.exp(bc), st)
        parts = []
        for i in range(CHUNK // GLA_SUB):
            lo_r, hi_r = i * GLA_SUB, (i + 1) * GLA_SUB
            if rev:
                ref = bc[hi_r - 1:hi_r]
                c0, c1 = lo_r, CHUNK
            else:
                ref = bc[lo_r:lo_r + 1]
                c0, c1 = 0, hi_r
            qi = q[lo_r:hi_r] * jnp.exp(bc[lo_r:hi_r] - ref)
            ki = k[c0:c1] * jnp.exp(ref - bc[c0:c1])
            att = _dot_nt(qi, ki)
            rg = lax.broadcasted_iota(jnp.int32, (GLA_SUB, c1 - c0), 0) + lo_r
            cg = lax.broadcasted_iota(jnp.int32, (GLA_SUB, c1 - c0), 1) + c0
            keep = (cg >= rg) if rev else (cg <= rg)
            parts.append(_dot(jnp.where(keep, att, 0.0), v[c0:c1]))
        outs.append(o + jnp.concatenate(parts, axis=0))
        k_dec = k * jnp.exp(b_tot - bc)
        s_ref[h] = st * jnp.exp(b_tot) + _dot_tn(v, k_dec)
    o = jnp.concatenate(outs, axis=-1)
    rows = o_ref.shape[1]
    o_ref[0] = jnp.concatenate([o[i * rows:(i + 1) * rows] for i in range(CHUNK // rows)], axis=-1)


def gla_scan_tc(qk_c, v_c, lr_c, qk_l, v_l, lr_l, w_la, b_la, rev):
    b, l, _ = qk_l.shape
    rows = l // GRID_W
    cols = CHUNK // rows
    n_ctx, n_lat = qk_c.shape[1] // CHUNK, l // CHUNK
    dirn = 1 if rev else 0
    cc, lc = _scan_chunks(rev, n_ctx, n_lat)
    hv = GLA_HEADS * GLA_DV
    ctx_blk = lambda a: pl.BlockSpec((1, CHUNK, a.shape[2]), lambda bi, j: (bi, cc(j), 0))
    lat_blk = lambda n: pl.BlockSpec((1, rows, cols * n), lambda bi, j: (bi, 0, lc(j)))
    grid_view = lambda a: a.reshape(b, rows, GRID_W * a.shape[2])
    out = pl.pallas_call(
        functools.partial(_gla_scan_kernel, rev, dirn, n_ctx), grid=(b, n_ctx + n_lat),
        in_specs=[ctx_blk(qk_c), ctx_blk(v_c), ctx_blk(lr_c),
                  lat_blk(qk_l.shape[2]), lat_blk(v_l.shape[2]), lat_blk(lr_l.shape[2]),
                  pl.BlockSpec((1,) + w_la.shape[1:], lambda bi, j: (dirn, 0, 0)),
                  pl.BlockSpec((1,) + b_la.shape[1:], lambda bi, j: (dirn, 0, 0))],
        out_specs=lat_blk(hv),
        out_shape=jax.ShapeDtypeStruct((b, rows, GRID_W * hv), F32),
        scratch_shapes=[pltpu.VMEM((GLA_HEADS, GLA_DV, GLA_DK), F32)],
        compiler_params=_cparams("parallel", "arbitrary"), name="gla_scan_bwd" if rev else "gla_scan_fwd",
    )(qk_c, v_c, lr_c, grid_view(qk_l), grid_view(v_l), grid_view(lr_l), w_la, b_la)
    return out.reshape(b, l, hv)


def _head_norm_gate(o, gate, g, n_heads, dv):
    parts = []
    for h in range(n_heads):
        oh = o[:, h * dv:(h + 1) * dv]
        gh = gate[:, h * dv:(h + 1) * dv]
        yh = oh * lax.rsqrt(jnp.mean(oh * oh, axis=-1, keepdims=True) + EPS) * g
        parts.append(yh * (gh * jax.nn.sigmoid(gh)))
    return parts


def _mix_out_kernel(x_ref, dnf_ref, dnb_ref, z_ref, glf_ref, glb_ref, r_ref, mod_ref, dng_ref,
                    glg_ref, n2g_ref, wout_ref, wq_ref, x1_ref, h2_ref, q_ref):
    parts = (_head_norm_gate(dnf_ref[0] + dnb_ref[0], z_ref[0], dng_ref[...], DN_HEADS, DN_DV)
             + _head_norm_gate(glf_ref[0] + glb_ref[0], r_ref[0], glg_ref[...], GLA_HEADS, GLA_DV))
    y = jnp.dot(jnp.concatenate(parts, axis=-1).astype(BF16), wout_ref[...], preferred_element_type=F32)
    x1 = x_ref[0] + mod_ref[0, 2:3, :] * y
    x1_ref[0] = x1
    n = x1 * lax.rsqrt(jnp.mean(x1 * x1, axis=-1, keepdims=True) + EPS) * n2g_ref[...]
    h2 = n * (1.0 + mod_ref[0, 4:5, :]) + mod_ref[0, 3:4, :]
    h2_ref[0] = h2
    q_ref[0] = jnp.dot(h2.astype(BF16), wq_ref[...], preferred_element_type=F32)


def mix_out_tc(x, dn_f, dn_b, z, gl_f, gl_b, r, mod_l, dn_g, gla_g, n2_g, w_out, w_q):
    b, l, d = x.shape
    tok = lambda n: pl.BlockSpec((1, TOK_TILE, n), lambda bi, i: (bi, i, 0))
    full = lambda a: pl.BlockSpec(a.shape, lambda bi, i: (0,) * a.ndim)
    dn_g, gla_g, n2_g = dn_g.reshape(1, -1), gla_g.reshape(1, -1), n2_g.reshape(1, -1)
    nq = w_q.shape[1]
    return pl.pallas_call(
        _mix_out_kernel, grid=(b, l // TOK_TILE),
        in_specs=[tok(d), tok(dn_f.shape[2]), tok(dn_b.shape[2]), tok(z.shape[2]),
                  tok(gl_f.shape[2]), tok(gl_b.shape[2]), tok(r.shape[2]),
                  pl.BlockSpec((1,) + mod_l.shape[1:], lambda bi, i: (bi, 0, 0)),
                  full(dn_g), full(gla_g), full(n2_g), full(w_out), full(w_q)],
        out_specs=[tok(d), tok(d), tok(nq)],
        out_shape=[jax.ShapeDtypeStruct((b, l, d), F32), jax.ShapeDtypeStruct((b, l, d), F32),
                   jax.ShapeDtypeStruct((b, l, nq), F32)],
        compiler_params=_cparams("parallel", "arbitrary"), name="mix_out",
    )(x, dn_f, dn_b, z, gl_f, gl_b, r, mod_l, dn_g, gla_g, n2_g, w_out, w_q)


def _top_rows(s, k, payload=None):
    n = s.shape[0]
    row = lax.broadcasted_iota(jnp.int32, s.shape, 0).astype(F32)
    vals, picked = [], []
    for _ in range(k):
        m = jnp.max(s, axis=0, keepdims=True)
        first = jnp.min(jnp.where(s == m, row, float(n)), axis=0, keepdims=True)
        sel = row == first
        vals.append(m)
        if payload is None:
            picked.append(first)
        else:
            picked.append(jnp.max(jnp.where(sel, payload, -1.0), axis=0, keepdims=True))
        s = jnp.where(sel, -jnp.inf, s)
    return jnp.concatenate(vals, axis=0), jnp.concatenate(picked, axis=0)


def _select_kernel(q_ref, k_ref, idx_ref, gate_ref, idx_s, gate_s):
    half = PEER_DQ // 2

    def head(h, carry):
        tops = []
        for p in range(2):
            qp = q_ref[:, pl.ds(pl.multiple_of(h * PEER_DQ + p * half, half), half)]
            s = lax.dot_general(k_ref[h, p], qp, (((1,), (1,)), ((), ())),
                                preferred_element_type=F32,
                                precision=lax.Precision.HIGHEST)
            tops.append(_top_rows(s, PEER_TOPK))
        (s0, i0), (s1, i1) = tops
        cand_s = jnp.concatenate([s0[i:i + 1] + s1 for i in range(PEER_TOPK)], axis=0)
        cand_i = jnp.concatenate([i0[i:i + 1] * float(PEER_NKEYS) + i1 for i in range(PEER_TOPK)], axis=0)
        best_s, idx = _top_rows(cand_s, PEER_TOPK, payload=cand_i)
        e = jnp.exp(best_s - best_s[0:1])
        r0 = pl.multiple_of(h * PEER_TOPK, PEER_TOPK)
        idx_s[pl.ds(r0, PEER_TOPK), :] = idx
        gate_s[pl.ds(r0, PEER_TOPK), :] = e / jnp.sum(e, axis=0, keepdims=True)
        return carry

    lax.fori_loop(0, PEER_HEADS, head, 0)
    idx_ref[...] = idx_s[...].T.astype(jnp.int32)
    gate_ref[...] = gate_s[...].T


def peer_select_tc(q, keys):
    n_tok = q.shape[0]
    out_spec = pl.BlockSpec((SELECT_TILE, PICKS), lambda i: (i, 0))
    return pl.pallas_call(
        _select_kernel,
        grid=(n_tok // SELECT_TILE,),
        in_specs=[pl.BlockSpec((SELECT_TILE, q.shape[1]), lambda i: (i, 0)),
                  pl.BlockSpec(keys.shape, lambda i: (0, 0, 0, 0))],
        out_specs=[out_spec, out_spec],
        out_shape=[jax.ShapeDtypeStruct((n_tok, PICKS), jnp.int32),
                   jax.ShapeDtypeStruct((n_tok, PICKS), F32)],
        scratch_shapes=[pltpu.VMEM((PICKS, SELECT_TILE), F32), pltpu.VMEM((PICKS, SELECT_TILE), F32)],
        compiler_params=_cparams("parallel"), name="peer_select",
    )(q, keys)


def _sc_mesh():
    return plsc.VectorSubcoreMesh(core_axis_name="c", subcore_axis_name="s")


def _sc_pipeline(body, n_steps, in_specs, out_specs, operands):
    pltpu.emit_pipeline(
        body, grid=(n_steps,), in_specs=in_specs, out_specs=out_specs,
        core_axis_name=("c", "s"), dimension_semantics=(pltpu.PARALLEL,),
        trace_scopes=False,
    )(*operands)


def peer_act_partial_sc(u3, idx2, h3):
    n_groups, n_tok = idx2.shape[0], h3.shape[0]
    nsub = TOK_STEP * GROUPS_PER_TOK
    n_chunks = SUBLANES * LANES // SC_LANES

    @functools.partial(
        pl.kernel, mesh=_sc_mesh(),        out_type=jax.ShapeDtypeStruct((n_groups, PICK_GROUP * SC_LANES), F32),
        scratch_types=[pltpu.VMEM((2, PICK_GROUP, SUBLANES, LANES), F32),
                       pltpu.SemaphoreType.DMA((2,))],
    )
    def k(u_hbm, i_hbm, h_hbm, o_hbm, rows, sems):
        def body(i_v, h_v, o_v):
            def fetch(j, slot):
                return pltpu.make_async_copy(u_hbm.at[i_v.at[j]], rows.at[slot], sems.at[slot])

            fetch(0, 0).start()

            def sub(j, carry):
                slot = j % 2

                @pl.when(j + 1 < nsub)
                def _():
                    fetch(j + 1, 1 - slot).start()

                fetch(j, slot).wait()
                t = j // GROUPS_PER_TOK

                def picks(g, carry2):
                    kb = g * ACT_UNROLL
                    accs = [None] * ACT_UNROLL
                    for c in range(n_chunks):
                        s, l = divmod(c, LANES // SC_LANES)
                        hc = h_v[t, s, pl.ds(l * SC_LANES, SC_LANES)]
                        for i in range(ACT_UNROLL):
                            p = rows[slot, kb + i, s, pl.ds(l * SC_LANES, SC_LANES)] * hc
                            accs[i] = p if accs[i] is None else accs[i] + p
                    for i in range(ACT_UNROLL):
                        o_v[j, pl.ds((kb + i) * SC_LANES, SC_LANES)] = accs[i]
                    return carry2

                lax.fori_loop(0, PICK_GROUP // ACT_UNROLL, picks, 0)
                return carry

            lax.fori_loop(0, nsub, sub, 0)

        _sc_pipeline(
            body, n_tok // TOK_STEP,
            [pl.BlockSpec((nsub, PICK_GROUP), lambda i: (i, 0)),
             pl.BlockSpec((TOK_STEP, SUBLANES, LANES), lambda i: (i, 0, 0))],
            [pl.BlockSpec((nsub, PICK_GROUP * SC_LANES), lambda i: (i, 0))],
            (i_hbm, h_hbm, o_hbm))

    return k(u3, idx2, h3)


def peer_combine_sc(v3, idx2, coef_b):
    n_groups = idx2.shape[0]
    n_tok = n_groups // GROUPS_PER_TOK
    nsub = TOK_STEP * GROUPS_PER_TOK
    n_chunks = SUBLANES * LANES // SC_LANES

    @functools.partial(
        pl.kernel, mesh=_sc_mesh(),        out_type=jax.ShapeDtypeStruct((n_tok, SUBLANES, LANES), F32),
        scratch_types=[pltpu.VMEM((2, PICK_GROUP, SUBLANES, LANES), F32),
                       pltpu.SemaphoreType.DMA((2,))],
    )
    def k(v_hbm, i_hbm, c_hbm, o_hbm, rows, sems):
        def body(i_v, c_v, o_v):
            def fetch(j, slot):
                return pltpu.make_async_copy(v_hbm.at[i_v.at[j]], rows.at[slot], sems.at[slot])

            fetch(0, 0).start()
            for j in range(nsub):
                slot = j % 2
                if j + 1 < nsub:
                    fetch(j + 1, 1 - slot).start()
                fetch(j, slot).wait()
                t = j // GROUPS_PER_TOK
                first = j % GROUPS_PER_TOK == 0
                cks = [c_v[j, pl.ds(kk * SC_LANES, SC_LANES)] for kk in range(PICK_GROUP)]

                def chunk(c, carry, slot=slot, t=t, first=first, cks=cks):
                    s = c // (LANES // SC_LANES)
                    l = (c % (LANES // SC_LANES)) * SC_LANES
                    parts = [cks[kk] * rows[slot, kk, s, pl.ds(l, SC_LANES)] for kk in range(4)]
                    for kk in range(4, PICK_GROUP):
                        parts[kk % 4] = parts[kk % 4] + cks[kk] * rows[slot, kk, s, pl.ds(l, SC_LANES)]
                    tot = (parts[0] + parts[1]) + (parts[2] + parts[3])
                    if not first:
                        tot = tot + o_v[t, s, pl.ds(l, SC_LANES)]
                    o_v[t, s, pl.ds(l, SC_LANES)] = tot
                    return carry

                lax.fori_loop(0, n_chunks, chunk, 0)

        _sc_pipeline(
            body, n_tok // TOK_STEP,
            [pl.BlockSpec((nsub, PICK_GROUP), lambda i: (i, 0)),
             pl.BlockSpec((nsub, PICK_GROUP * SC_LANES), lambda i: (i, 0))],
            [pl.BlockSpec((TOK_STEP, SUBLANES, LANES), lambda i: (i, 0, 0))],
            (i_hbm, c_hbm, o_hbm))

    return k(v3, idx2, coef_b)


def _segment_matrix():
    r = lax.broadcasted_iota(jnp.int32, (PICK_GROUP * SC_LANES, PICK_GROUP), 0) // SC_LANES
    c = lax.broadcasted_iota(jnp.int32, (PICK_GROUP * SC_LANES, PICK_GROUP), 1)
    return (r == c).astype(F32)


def _coef_kernel(part_ref, gate_ref, o_ref):
    seg = _segment_matrix()
    act = jnp.dot(part_ref[...], seg, preferred_element_type=F32, precision=lax.Precision.HIGHEST)
    coef = gate_ref[...] * (0.5 * act * (1.0 + lax.erf(act * (2.0 ** -0.5))))
    o_ref[...] = lax.dot_general(coef, seg, (((1,), (1,)), ((), ())), preferred_element_type=F32,
                                 precision=lax.Precision.HIGHEST)


def peer_coef_tc(part, gate2):
    n_groups, width = part.shape
    tile = 1024
    return pl.pallas_call(
        _coef_kernel,
        grid=(n_groups // tile,),
        in_specs=[pl.BlockSpec((tile, width), lambda i: (i, 0)),
                  pl.BlockSpec((tile, PICK_GROUP), lambda i: (i, 0))],
        out_specs=pl.BlockSpec((tile, width), lambda i: (i, 0)),
        out_shape=jax.ShapeDtypeStruct((n_groups, width), F32),
        compiler_params=_cparams("parallel"), name="peer_coef",
    )(part, gate2)


def _final_kernel(x_ref, y_ref, mod_ref, g_ref, o_ref):
    x = x_ref[0] + mod_ref[0, 5:6, :] * y_ref[0]
    o_ref[0] = x * lax.rsqrt(jnp.mean(x * x, axis=-1, keepdims=True) + EPS) * g_ref[...]


def final_tc(x1, y, mod_l, final_g):
    b, l, d = x1.shape
    tok = pl.BlockSpec((1, TOK_TILE, d), lambda bi, i: (bi, i, 0))
    return pl.pallas_call(
        _final_kernel, grid=(b, l // TOK_TILE),
        in_specs=[tok, tok, pl.BlockSpec((1,) + mod_l.shape[1:], lambda bi, i: (bi, 0, 0)),
                  pl.BlockSpec((1, d), lambda bi, i: (0, 0))],
        out_specs=tok, out_shape=jax.ShapeDtypeStruct((b, l, d), F32),
        compiler_params=_cparams("parallel", "arbitrary"), name="final_norm",
    )(x1, y, mod_l, final_g.reshape(1, d))


def _pad_cols(w, n):
    return jnp.pad(w, ((0, 0), (0, n - w.shape[1])))


def _exact_zero(v):
    return jnp.minimum(jnp.abs(v), 0.0)


def forward(x, c, ctx, c_ctx, w_ada, b_ada, norm1_g, norm2_g, w_in, conv_w, dn_a_log,
            dn_dt_bias, dn_norm_g, gla_wa2, gla_ba, gla_norm_g, w_out, peer_wq, peer_keys,
            peer_u, peer_v, final_g):
    b, l, d = x.shape
    c_all = jnp.concatenate([c, c_ctx[None]], axis=0)
    c_all = jnp.pad(c_all, ((0, (-c_all.shape[0]) % SUBLANES), (0, 0)))
    mod = adaln_mod(c_all, w_ada, b_ada)
    mod_l = mod[:b].reshape(b, 6, d)
    mod_c = jnp.broadcast_to(mod[b].reshape(1, 6, d), (b, 6, d))
    o = DN_QKV
    hv = DN_HEADS * DN_DV
    w_dn_qkv, w_dn_z = w_in[:, :o], w_in[:, o:o + hv]
    w_dn_ba = _pad_cols(w_in[:, o + hv:DN_COLS], LANES)
    g0 = DN_COLS
    gqk, gv = 2 * GLA_HEADS * GLA_DK, GLA_HEADS * GLA_DV
    w_gl_qk, w_gl_v = w_in[:, g0:g0 + gqk], w_in[:, g0 + gqk:g0 + gqk + gv]
    w_gl_r = w_in[:, g0 + gqk + gv:g0 + gqk + 2 * gv]
    w_gl_lr = _pad_cols(w_in[:, g0 + gqk + 2 * gv:], LANES)
    w_lat = [w.astype(BF16) for w in (w_dn_qkv, w_dn_ba, w_gl_qk, w_gl_v, w_gl_lr, w_dn_z, w_gl_r)]
    w_ctx = w_lat[:5]
    w_la = jnp.zeros((2, LANES, GLA_HEADS * GLA_DK), F32)
    for dd in range(2):
        w_la = w_la.at[dd, dd * GLA_LR:(dd + 1) * GLA_LR].set(gla_wa2[dd])
    w_la = w_la.astype(BF16)
    b_la = gla_ba.reshape(2, 1, GLA_HEADS * GLA_DK)
    w_out_b, w_q_b = w_out.astype(BF16), peer_wq.astype(BF16)
    u3 = peer_u.reshape(-1, SUBLANES, LANES)
    v3 = peer_v.reshape(-1, SUBLANES, LANES)

    def group(xg, ctxg, mod_cg, mod_g, after_select, after_coef):
        bg = xg.shape[0]
        mod_cg, mod_g = mod_cg + after_select, mod_g + after_select
        c_qkv, c_ba, c_qk, c_v, c_lr = in_projection(ctxg, norm1_g, mod_cg[:, 0:2], w_ctx)
        l_qkv, l_ba, l_qk, l_v, l_lr, l_z, l_r = in_projection(xg, norm1_g, mod_g[:, 0:2], w_lat)
        feat_c, feat_l = dn_features_tc(c_qkv, conv_w), dn_features_tc(l_qkv, conv_w)
        dn_f = dn_scan_tc(feat_c, feat_l, c_ba, l_ba, dn_a_log, dn_dt_bias, rev=False)
        dn_b = dn_scan_tc(feat_c, feat_l, c_ba, l_ba, dn_a_log, dn_dt_bias, rev=True)
        gl_f = gla_scan_tc(c_qk, c_v, c_lr, l_qk, l_v, l_lr, w_la, b_la, rev=False)
        gl_b = gla_scan_tc(c_qk, c_v, c_lr, l_qk, l_v, l_lr, w_la, b_la, rev=True)
        x1, h2, q = mix_out_tc(xg, dn_f, dn_b, l_z, gl_f, gl_b, l_r, mod_g, dn_norm_g, gla_norm_g,
                               norm2_g, w_out_b, w_q_b)
        n_tok = bg * l
        idx, gate = peer_select_tc(q.reshape(n_tok, -1), peer_keys + after_coef)
        idx2 = idx.reshape(n_tok * GROUPS_PER_TOK, PICK_GROUP)
        gate2 = gate.reshape(n_tok * GROUPS_PER_TOK, PICK_GROUP)
        part = peer_act_partial_sc(u3, idx2, h2.reshape(n_tok, SUBLANES, LANES))
        coef_b = peer_coef_tc(part, gate2)
        y = peer_combine_sc(v3, idx2, coef_b)
        out = final_tc(x1, y.reshape(bg, l, d), mod_g, final_g)
        return out, _exact_zero(gate[0, 0]), _exact_zero(coef_b[0, 0])

    bg = b // BATCH_GROUPS
    outs, z_sel, z_coef = [], jnp.zeros((), F32), jnp.zeros((), F32)
    for i in range(0, b, bg):
        o, z_sel, z_coef = group(x[i:i + bg], ctx[i:i + bg], mod_c[i:i + bg], mod_l[i:i + bg], z_sel, z_coef)
        outs.append(o)
    return jnp.concatenate(outs, axis=0)


def kernel(x, c, ctx, c_ctx, w_ada, b_ada, norm1_g, norm2_g, w_in, conv_w, dn_a_log,
           dn_dt_bias, dn_norm_g, gla_wa2, gla_ba, gla_norm_g, w_out, peer_wq, peer_keys,
           peer_u, peer_v, final_g):
    assert w_ada.shape[0] == 1, "single-layer block: the context stream is only consumed, never updated"
    return forward(x, c, ctx, c_ctx, w_ada[0], b_ada[0], norm1_g[0], norm2_g[0], w_in[0], conv_w[0],
                   dn_a_log[0], dn_dt_bias[0], dn_norm_g[0], gla_wa2[0], gla_ba[0], gla_norm_g[0],
                   w_out[0], peer_wq[0], peer_keys[0], peer_u[0], peer_v[0], final_g)
```

```python
import functools

import jax
import jax.numpy as jnp
from jax import lax
from jax.experimental import pallas as pl
from jax.experimental.pallas import tpu as pltpu
from jax.experimental.pallas import tpu_sc as plsc

GRID_W = 64
DN_HEADS = 4
DN_DK = 128
DN_DV = 128
CONV_W = 5
GLA_HEADS = 4
GLA_DK = 64
GLA_DV = 128
GLA_LR = 16
GLA_TAU = 16.0
CHUNK = 64
PEER_HEADS = 8
PEER_NKEYS = 128
PEER_DQ = 256
PEER_TOPK = 16
EPS = 1e-6
DN_QKV = 2 * DN_HEADS * DN_DK + DN_HEADS * DN_DV
DN_COLS = DN_QKV + DN_HEADS * DN_DV + 4 * DN_HEADS

SUBLANES = 8
LANES = 128
SC_LANES = 16
VMEM_LIMIT_BYTES = 48 * 1024 * 1024

TOK_TILE = 256
SELECT_TILE = 256
GLA_SUB = 16
PICK_GROUP = 32
TOK_STEP = 2
ACT_UNROLL = 8
BATCH_GROUPS = 4
PICKS = PEER_HEADS * PEER_TOPK
GROUPS_PER_TOK = PICKS // PICK_GROUP

F32 = jnp.float32
BF16 = jnp.bfloat16


def _cparams(*semantics):
    return pltpu.CompilerParams(dimension_semantics=semantics, vmem_limit_bytes=VMEM_LIMIT_BYTES)


def _dot(a, b):
    return jnp.dot(a.astype(BF16), b.astype(BF16), preferred_element_type=F32)


def _dot_nt(a, b):
    return lax.dot_general(a.astype(BF16), b.astype(BF16), (((1,), (1,)), ((), ())),
                           preferred_element_type=F32)


def _dot_tn(a, b):
    return lax.dot_general(a.astype(BF16), b.astype(BF16), (((0,), (0,)), ((), ())),
                           preferred_element_type=F32)


def _split(x):
    hi = x.astype(BF16)
    return hi, (x - hi.astype(F32)).astype(BF16)


def _mask_dot(mask_bf16, x):
    hi, lo = _split(x)
    return (jnp.dot(mask_bf16, hi, preferred_element_type=F32)
            + jnp.dot(mask_bf16, lo, preferred_element_type=F32))


def _softplus(x):
    return jnp.maximum(x, 0.0) + jnp.log(1.0 + jnp.exp(-jnp.abs(x)))


def _tri_masks(rev):
    r = lax.broadcasted_iota(jnp.int32, (CHUNK, CHUNK), 0)
    c = lax.broadcasted_iota(jnp.int32, (CHUNK, CHUNK), 1)
    d = (c - r) if rev else (r - c)
    return d >= 0, d > 0


def _mod_kernel(c_ref, w_ref, b_ref, o_ref):
    c = c_ref[...]
    s = c * jax.nn.sigmoid(c)
    o_ref[...] = jnp.dot(s, w_ref[...], preferred_element_type=F32,
                         precision=lax.Precision.HIGHEST) + b_ref[...]


def adaln_mod(c_all, w_ada, b_ada):
    r, d = c_all.shape
    n = w_ada.shape[1]
    tn = 512
    return pl.pallas_call(
        _mod_kernel, grid=(n // tn,),
        in_specs=[pl.BlockSpec((r, d), lambda j: (0, 0)),
                  pl.BlockSpec((d, tn), lambda j: (0, j)),
                  pl.BlockSpec((1, tn), lambda j: (0, j))],
        out_specs=pl.BlockSpec((r, tn), lambda j: (0, j)),
        out_shape=jax.ShapeDtypeStruct((r, n), F32),
        compiler_params=_cparams("arbitrary"), name="adaln_mod",
    )(c_all, w_ada, b_ada.reshape(1, n))


def _inproj_kernel(x_ref, g_ref, mod_ref, *refs):
    n_out = len(refs) // 2
    x = x_ref[0]
    y = x * lax.rsqrt(jnp.mean(x * x, axis=-1, keepdims=True) + EPS) * g_ref[...]
    h = (y * (1.0 + mod_ref[0, 1:2, :]) + mod_ref[0, 0:1, :]).astype(BF16)
    for w_ref, o_ref in zip(refs[:n_out], refs[n_out:]):
        o_ref[0] = jnp.dot(h, w_ref[...], preferred_element_type=F32)


def in_projection(x, norm_g, mod, weights):
    b, l, d = x.shape
    w_specs = [pl.BlockSpec(w.shape, lambda bi, i: (0, 0)) for w in weights]
    o_specs = [pl.BlockSpec((1, TOK_TILE, w.shape[1]), lambda bi, i: (bi, i, 0)) for w in weights]
    return pl.pallas_call(
        _inproj_kernel, grid=(b, l // TOK_TILE),
        in_specs=[pl.BlockSpec((1, TOK_TILE, d), lambda bi, i: (bi, i, 0)),
                  pl.BlockSpec((1, d), lambda bi, i: (0, 0)),
                  pl.BlockSpec((1, 2, d), lambda bi, i: (bi, 0, 0))] + w_specs,
        out_specs=o_specs,
        out_shape=[jax.ShapeDtypeStruct((b, l, w.shape[1]), F32) for w in weights],
        compiler_params=_cparams("parallel", "arbitrary"), name="in_projection",
    )(x, norm_g.reshape(1, d), mod, *weights)


def _dn_feature_kernel(x_ref, w_ref, o_ref):
    x = x_ref[0]
    n = x.shape[0]
    t = lax.broadcasted_iota(jnp.int32, (n, 1), 0)
    pad = CONV_W // 2
    acc = w_ref[0, pad:pad + 1, :] * x
    for j in range(CONV_W):
        s = j - pad
        if s == 0:
            continue
        xs = pltpu.roll(x, (-s) % n, axis=0)
        bad = (t < -s) if s < 0 else (t >= n - s)
        acc = acc + w_ref[0, j:j + 1, :] * jnp.where(bad, 0.0, xs)
    y = acc * jax.nn.sigmoid(acc)
    kind = pl.program_id(1) // DN_HEADS
    inv = lax.rsqrt(jnp.sum(y * y, axis=-1, keepdims=True) + EPS)
    scale = jnp.where(kind == 0, inv * DN_DK ** -0.5, jnp.where(kind == 1, inv, 1.0))
    o_ref[0] = y * scale


def dn_features_tc(qkv, conv_w):
    b, l, n = qkv.shape
    nblk = n // LANES
    w = jnp.zeros((nblk, SUBLANES, LANES), F32).at[:, :CONV_W].set(
        conv_w.reshape(CONV_W, nblk, LANES).transpose(1, 0, 2))
    return pl.pallas_call(
        _dn_feature_kernel, grid=(b, nblk),
        in_specs=[pl.BlockSpec((1, l, LANES), lambda bi, j: (bi, 0, j)),
                  pl.BlockSpec((1, SUBLANES, LANES), lambda bi, j: (j, 0, 0))],
        out_specs=pl.BlockSpec((1, l, LANES), lambda bi, j: (bi, 0, j)),
        out_shape=jax.ShapeDtypeStruct((b, l, n), F32),
        compiler_params=_cparams("parallel", "arbitrary"), name="dn_features",
    )(qkv, w)


def _scan_chunks(rev, n_ctx, n_lat):
    if rev:
        ctx = lambda j: jnp.maximum(n_ctx - 1 - j, 0)
        lat = lambda j: jnp.where(j < n_ctx, n_lat - 1, n_lat - 1 - (j - n_ctx))
    else:
        ctx = lambda j: jnp.minimum(j, n_ctx - 1)
        lat = lambda j: jnp.maximum(j - n_ctx, 0)
    return ctx, lat


def _dn_scan_kernel(rev, dirn, n_ctx, alog_ref, dtb_ref, fc_ref, fl_ref, bac_ref, bal_ref, o_ref, s_ref):
    step = pl.program_id(1)

    @pl.when(step == 0)
    def _():
        s_ref[...] = jnp.zeros_like(s_ref)

    in_ctx = step < n_ctx
    f = jnp.where(in_ctx, fc_ref[0], fl_ref[0])
    ba = jnp.where(in_ctx, bac_ref[0], bal_ref[0])
    incl, strict = _tri_masks(rev)
    incl_b = incl.astype(BF16)
    eye = (lax.broadcasted_iota(jnp.int32, (CHUNK, CHUNK), 0)
           == lax.broadcasted_iota(jnp.int32, (CHUNK, CHUNK), 1)).astype(F32)
    lane = lax.broadcasted_iota(jnp.int32, (CHUNK, LANES), 1)
    onehot0 = (lane == 0).astype(BF16)
    hd = DN_HEADS * DN_DK
    outs = []
    for h in range(DN_HEADS):
        q = f[:, h * DN_DK:(h + 1) * DN_DK]
        k = f[:, hd + h * DN_DK:hd + (h + 1) * DN_DK]
        v = f[:, 2 * hd + h * DN_DV:2 * hd + (h + 1) * DN_DV]
        cb = dirn * DN_HEADS + h
        beta = jax.nn.sigmoid(ba[:, cb:cb + 1])
        a = ba[:, 2 * DN_HEADS + cb:2 * DN_HEADS + cb + 1]
        g = -jnp.exp(jnp.full((CHUNK, 1), alog_ref[dirn, h], F32)) * _softplus(a + dtb_ref[dirn, h])
        gmat = jnp.broadcast_to(g, (CHUNK, LANES))
        gcb = _mask_dot(incl_b, gmat)
        g_tot = jnp.sum(gmat, axis=0, keepdims=True)
        hi, lo = _split(gcb)
        g_row = (lax.dot_general(onehot0, hi, (((1,), (1,)), ((), ())), preferred_element_type=F32)
                 + lax.dot_general(onehot0, lo, (((1,), (1,)), ((), ())), preferred_element_type=F32))
        decay = jnp.where(incl, jnp.exp(jnp.where(incl, gcb[:, :CHUNK] - g_row, 0.0)), 0.0)
        kb = k * beta
        lower = jnp.where(strict, _dot_nt(kb, k) * decay, 0.0)
        eg = jnp.exp(gcb)
        inv = eye - lower
        pw = lower
        for _ in range(5):
            pw = _dot(pw, pw)
            inv = inv + _dot(inv, pw)
        sol = _dot(inv, jnp.concatenate([v * beta, kb * eg], axis=-1))
        u, w = sol[:, :DN_DV], sol[:, DN_DV:]
        k_dec = k * jnp.exp(g_tot - gcb)
        s = s_ref[h]
        v_new = u - _dot(w, s)
        a_qk = _dot_nt(q, k) * decay
        outs.append(_dot(q * eg, s) + _dot(a_qk, v_new))
        s_ref[h] = s * jnp.exp(g_tot) + _dot_tn(k_dec, v_new)
    o_ref[0] = jnp.concatenate(outs, axis=-1)


def dn_scan_tc(feat_c, feat_l, ba_c, ba_l, a_log, dt_bias, rev):
    b, l, nf = feat_l.shape
    n_ctx, n_lat = feat_c.shape[1] // CHUNK, l // CHUNK
    dirn = 1 if rev else 0
    cc, lc = _scan_chunks(rev, n_ctx, n_lat)
    smem = pl.BlockSpec(memory_space=pltpu.SMEM)
    return pl.pallas_call(
        functools.partial(_dn_scan_kernel, rev, dirn, n_ctx), grid=(b, n_ctx + n_lat),
        in_specs=[smem, smem,
                  pl.BlockSpec((1, CHUNK, nf), lambda bi, j: (bi, cc(j), 0)),
                  pl.BlockSpec((1, CHUNK, nf), lambda bi, j: (bi, lc(j), 0)),
                  pl.BlockSpec((1, CHUNK, LANES), lambda bi, j: (bi, cc(j), 0)),
                  pl.BlockSpec((1, CHUNK, LANES), lambda bi, j: (bi, lc(j), 0))],
        out_specs=pl.BlockSpec((1, CHUNK, DN_HEADS * DN_DV), lambda bi, j: (bi, lc(j), 0)),
        out_shape=jax.ShapeDtypeStruct((b, l, DN_HEADS * DN_DV), F32),
        scratch_shapes=[pltpu.VMEM((DN_HEADS, DN_DK, DN_DV), F32)],
        compiler_params=_cparams("parallel", "arbitrary"), name="dn_scan_bwd" if rev else "dn_scan_fwd",
    )(a_log, dt_bias, feat_c, feat_l, ba_c, ba_l)


def _from_grid_cols(blk, n):
    cols = blk.shape[1] // n
    return jnp.concatenate([blk[:, i * n:(i + 1) * n] for i in range(cols)], axis=0)


def _gla_scan_kernel(rev, dirn, n_ctx, qkc_ref, vc_ref, lrc_ref, qkl_ref, vl_ref, lrl_ref,
                     wla_ref, bla_ref, o_ref, s_ref):
    step = pl.program_id(1)

    @pl.when(step == 0)
    def _():
        s_ref[...] = jnp.zeros_like(s_ref)

    in_ctx = step < n_ctx
    hk, hv = GLA_HEADS * GLA_DK, GLA_HEADS * GLA_DV
    qk = jnp.where(in_ctx, qkc_ref[0], _from_grid_cols(qkl_ref[0], 2 * hk))
    vv = jnp.where(in_ctx, vc_ref[0], _from_grid_cols(vl_ref[0], hv))
    lr = jnp.where(in_ctx, lrc_ref[0], _from_grid_cols(lrl_ref[0], LANES))
    incl, _ = _tri_masks(rev)
    incl_b = incl.astype(BF16)
    pre = _dot(lr, wla_ref[0]) + bla_ref[0]
    la_all = -_softplus(-pre) * (1.0 / GLA_TAU)
    outs = []
    for h in range(GLA_HEADS):
        q = qk[:, h * GLA_DK:(h + 1) * GLA_DK] * GLA_DK ** -0.5
        k = qk[:, hk + h * GLA_DK:hk + (h + 1) * GLA_DK]
        v = vv[:, h * GLA_DV:(h + 1) * GLA_DV]
        la = la_all[:, h * GLA_DK:(h + 1) * GLA_DK]
        bc = _mask_dot(incl_b, la)
        b_tot = jnp.sum(la, axis=0, keepdims=True)
        st = s_ref[h]
        o = _dot_nt(q * jnp.exp(bc), st)
        parts = []
        for i in range(CHUNK // GLA_SUB):
            lo_r, hi_r = i * GLA_SUB, (i + 1) * GLA_SUB
            if rev:
                ref = bc[hi_r - 1:hi_r]
                c0, c1 = lo_r, CHUNK
            else:
                ref = bc[lo_r:lo_r + 1]
                c0, c1 = 0, hi_r
            qi = q[lo_r:hi_r] * jnp.exp(bc[lo_r:hi_r] - ref)
            ki = k[c0:c1] * jnp.exp(ref - bc[c0:c1])
            att = _dot_nt(qi, ki)
            rg = lax.broadcasted_iota(jnp.int32, (GLA_SUB, c1 - c0), 0) + lo_r
            cg = lax.broadcasted_iota(jnp.int32, (GLA_SUB, c1 - c0), 1) + c0
            keep = (cg >= rg) if rev else (cg <= rg)
            parts.append(_dot(jnp.where(keep, att, 0.0), v[c0:c1]))
        outs.append(o + jnp.concatenate(parts, axis=0))
        k_dec = k * jnp.exp(b_tot - bc)
        s_ref[h] = st * jnp.exp(b_tot) + _dot_tn(v, k_dec)
    o = jnp.concatenate(outs, axis=-1)
    rows = o_ref.shape[1]
    o_ref[0] = jnp.concatenate([o[i * rows:(i + 1) * rows] for i in range(CHUNK // rows)], axis=-1)


def gla_scan_tc(qk_c, v_c, lr_c, qk_l, v_l, lr_l, w_la, b_la, rev):
    b, l, _ = qk_l.shape
    rows = l // GRID_W
    cols = CHUNK // rows
    n_ctx, n_lat = qk_c.shape[1] // CHUNK, l // CHUNK
    dirn = 1 if rev else 0
    cc, lc = _scan_chunks(rev, n_ctx, n_lat)
    hv = GLA_HEADS * GLA_DV
    ctx_blk = lambda a: pl.BlockSpec((1, CHUNK, a.shape[2]), lambda bi, j: (bi, cc(j), 0))
    lat_blk = lambda n: pl.BlockSpec((1, rows, cols * n), lambda bi, j: (bi, 0, lc(j)))
    grid_view = lambda a: a.reshape(b, rows, GRID_W * a.shape[2])
    out = pl.pallas_call(
        functools.partial(_gla_scan_kernel, rev, dirn, n_ctx), grid=(b, n_ctx + n_lat),
        in_specs=[ctx_blk(qk_c), ctx_blk(v_c), ctx_blk(lr_c),
                  lat_blk(qk_l.shape[2]), lat_blk(v_l.shape[2]), lat_blk(lr_l.shape[2]),
                  pl.BlockSpec((1,) + w_la.shape[1:], lambda bi, j: (dirn, 0, 0)),
                  pl.BlockSpec((1,) + b_la.shape[1:], lambda bi, j: (dirn, 0, 0))],
        out_specs=lat_blk(hv),
        out_shape=jax.ShapeDtypeStruct((b, rows, GRID_W * hv), F32),
        scratch_shapes=[pltpu.VMEM((GLA_HEADS, GLA_DV, GLA_DK), F32)],
        compiler_params=_cparams("parallel", "arbitrary"), name="gla_scan_bwd" if rev else "gla_scan_fwd",
    )(qk_c, v_c, lr_c, grid_view(qk_l), grid_view(v_l), grid_view(lr_l), w_la, b_la)
    return out.reshape(b, l, hv)


def _head_norm_gate(o, gate, g, n_heads, dv):
    parts = []
    for h in range(n_heads):
        oh = o[:, h * dv:(h + 1) * dv]
        gh = gate[:, h * dv:(h + 1) * dv]
        yh = oh * lax.rsqrt(jnp.mean(oh * oh, axis=-1, keepdims=True) + EPS) * g
        parts.append(yh * (gh * jax.nn.sigmoid(gh)))
    return parts


def _mix_out_kernel(x_ref, dnf_ref, dnb_ref, z_ref, glf_ref, glb_ref, r_ref, mod_ref, dng_ref,
                    glg_ref, n2g_ref, wout_ref, wq_ref, x1_ref, h2_ref, q_ref):
    parts = (_head_norm_gate(dnf_ref[0] + dnb_ref[0], z_ref[0], dng_ref[...], DN_HEADS, DN_DV)
             + _head_norm_gate(glf_ref[0] + glb_ref[0], r_ref[0], glg_ref[...], GLA_HEADS, GLA_DV))
    y = jnp.dot(jnp.concatenate(parts, axis=-1).astype(BF16), wout_ref[...], preferred_element_type=F32)
    x1 = x_ref[0] + mod_ref[0, 2:3, :] * y
    x1_ref[0] = x1
    n = x1 * lax.rsqrt(jnp.mean(x1 * x1, axis=-1, keepdims=True) + EPS) * n2g_ref[...]
    h2 = n * (1.0 + mod_ref[0, 4:5, :]) + mod_ref[0, 3:4, :]
    h2_ref[0] = h2
    q_ref[0] = jnp.dot(h2.astype(BF16), wq_ref[...], preferred_element_type=F32)


def mix_out_tc(x, dn_f, dn_b, z, gl_f, gl_b, r, mod_l, dn_g, gla_g, n2_g, w_out, w_q):
    b, l, d = x.shape
    tok = lambda n: pl.BlockSpec((1, TOK_TILE, n), lambda bi, i: (bi, i, 0))
    full = lambda a: pl.BlockSpec(a.shape, lambda bi, i: (0,) * a.ndim)
    dn_g, gla_g, n2_g = dn_g.reshape(1, -1), gla_g.reshape(1, -1), n2_g.reshape(1, -1)
    nq = w_q.shape[1]
    return pl.pallas_call(
        _mix_out_kernel, grid=(b, l // TOK_TILE),
        in_specs=[tok(d), tok(dn_f.shape[2]), tok(dn_b.shape[2]), tok(z.shape[2]),
                  tok(gl_f.shape[2]), tok(gl_b.shape[2]), tok(r.shape[2]),
                  pl.BlockSpec((1,) + mod_l.shape[1:], lambda bi, i: (bi, 0, 0)),
                  full(dn_g), full(gla_g), full(n2_g), full(w_out), full(w_q)],
        out_specs=[tok(d), tok(d), tok(nq)],
        out_shape=[jax.ShapeDtypeStruct((b, l, d), F32), jax.ShapeDtypeStruct((b, l, d), F32),
                   jax.ShapeDtypeStruct((b, l, nq), F32)],
        compiler_params=_cparams("parallel", "arbitrary"), name="mix_out",
    )(x, dn_f, dn_b, z, gl_f, gl_b, r, mod_l, dn_g, gla_g, n2_g, w_out, w_q)


def _top_rows(s, k, payload=None):
    n = s.shape[0]
    row = lax.broadcasted_iota(jnp.int32, s.shape, 0).astype(F32)
    vals, picked = [], []
    for _ in range(k):
        m = jnp.max(s, axis=0, keepdims=True)
        first = jnp.min(jnp.where(s == m, row, float(n)), axis=0, keepdims=True)
        sel = row == first
        vals.append(m)
        if payload is None:
            picked.append(first)
        else:
            picked.append(jnp.max(jnp.where(sel, payload, -1.0), axis=0, keepdims=True))
        s = jnp.where(sel, -jnp.inf, s)
    return jnp.concatenate(vals, axis=0), jnp.concatenate(picked, axis=0)


def _select_kernel(q_ref, k_ref, idx_ref, gate_ref, idx_s, gate_s):
    half = PEER_DQ // 2

    def head(h, carry):
        tops = []
        for p in range(2):
            qp = q_ref[:, pl.ds(pl.multiple_of(h * PEER_DQ + p * half, half), half)]
            s = lax.dot_general(k_ref[h, p], qp, (((1,), (1,)), ((), ())),
                                preferred_element_type=F32,
                                precision=lax.Precision.HIGHEST)
            tops.append(_top_rows(s, PEER_TOPK))
        (s0, i0), (s1, i1) = tops
        cand_s = jnp.concatenate([s0[i:i + 1] + s1 for i in range(PEER_TOPK)], axis=0)
        cand_i = jnp.concatenate([i0[i:i + 1] * float(PEER_NKEYS) + i1 for i in range(PEER_TOPK)], axis=0)
        best_s, idx = _top_rows(cand_s, PEER_TOPK, payload=cand_i)
        e = jnp.exp(best_s - best_s[0:1])
        r0 = pl.multiple_of(h * PEER_TOPK, PEER_TOPK)
        idx_s[pl.ds(r0, PEER_TOPK), :] = idx
        gate_s[pl.ds(r0, PEER_TOPK), :] = e / jnp.sum(e, axis=0, keepdims=True)
        return carry

    lax.fori_loop(0, PEER_HEADS, head, 0)
    idx_ref[...] = idx_s[...].T.astype(jnp.int32)
    gate_ref[...] = gate_s[...].T


def peer_select_tc(q, keys):
    n_tok = q.shape[0]
    out_spec = pl.BlockSpec((SELECT_TILE, PICKS), lambda i: (i, 0))
    return pl.pallas_call(
        _select_kernel,
        grid=(n_tok // SELECT_TILE,),
        in_specs=[pl.BlockSpec((SELECT_TILE, q.shape[1]), lambda i: (i, 0)),
                  pl.BlockSpec(keys.shape, lambda i: (0, 0, 0, 0))],
        out_specs=[out_spec, out_spec],
        out_shape=[jax.ShapeDtypeStruct((n_tok, PICKS), jnp.int32),
                   jax.ShapeDtypeStruct((n_tok, PICKS), F32)],
        scratch_shapes=[pltpu.VMEM((PICKS, SELECT_TILE), F32), pltpu.VMEM((PICKS, SELECT_TILE), F32)],
        compiler_params=_cparams("parallel"), name="peer_select",
    )(q, keys)


def _sc_mesh():
    return plsc.VectorSubcoreMesh(core_axis_name="c", subcore_axis_name="s")


def _sc_pipeline(body, n_steps, in_specs, out_specs, operands):
    pltpu.emit_pipeline(
        body, grid=(n_steps,), in_specs=in_specs, out_specs=out_specs,
        core_axis_name=("c", "s"), dimension_semantics=(pltpu.PARALLEL,),
        trace_scopes=False,
    )(*operands)


def pack_table(t):
    half = t.shape[1] // 2
    bits = lax.bitcast_convert_type(t, jnp.uint32)
    lo_bits, hi_bits = bits[:, :half], bits[:, half:]
    low = (lo_bits + jnp.uint32(0x7FFF) + ((lo_bits >> 16) & jnp.uint32(1))) >> 16
    mag = hi_bits & jnp.uint32(0x7FFFFFFF)
    top = jnp.where(mag >= jnp.uint32(0x10000),
                    (mag - low + jnp.uint32(0x8000)) & jnp.uint32(0xFFFF0000), jnp.uint32(0))
    word = (hi_bits & jnp.uint32(0x80000000)) | top | low
    return lax.bitcast_convert_type(word, jnp.int32)


def _unpack(w):
    return plsc.bitcast(w << 16, F32), plsc.bitcast(w, F32)


def peer_act_partial_sc(u_p, idx2, h):
    n_groups, (n_tok, d) = idx2.shape[0], h.shape
    half = d // 2
    nsub = TOK_STEP * GROUPS_PER_TOK
    n_chunks = half // SC_LANES

    @functools.partial(
        pl.kernel, mesh=_sc_mesh(), compiler_params=pltpu.CompilerParams(needs_layout_passes=False),
        out_type=jax.ShapeDtypeStruct((n_groups, PICK_GROUP * SC_LANES), F32),
        scratch_types=[pltpu.VMEM((2, PICK_GROUP, half), jnp.int32),
                       pltpu.SemaphoreType.DMA((2,))],
    )
    def k(u_hbm, i_hbm, h_hbm, o_hbm, rows, sems):
        def body(i_v, h_v, o_v):
            def fetch(j, slot):
                return pltpu.make_async_copy(u_hbm.at[i_v.at[j]], rows.at[slot], sems.at[slot])

            fetch(0, 0).start()

            def sub(j, carry):
                slot = j % 2

                @pl.when(j + 1 < nsub)
                def _():
                    fetch(j + 1, 1 - slot).start()

                fetch(j, slot).wait()
                t = j // GROUPS_PER_TOK

                def picks(g, carry2):
                    kb = g * ACT_UNROLL
                    accs = [None] * ACT_UNROLL
                    for c in range(n_chunks):
                        h_lo = h_v[t, pl.ds(c * SC_LANES, SC_LANES)]
                        h_hi = h_v[t, pl.ds(half + c * SC_LANES, SC_LANES)]
                        for i in range(ACT_UNROLL):
                            lo, hi = _unpack(rows[slot, kb + i, pl.ds(c * SC_LANES, SC_LANES)])
                            p = lo * h_lo + hi * h_hi
                            accs[i] = p if accs[i] is None else accs[i] + p
                    for i in range(ACT_UNROLL):
                        o_v[j, pl.ds((kb + i) * SC_LANES, SC_LANES)] = accs[i]
                    return carry2

                lax.fori_loop(0, PICK_GROUP // ACT_UNROLL, picks, 0)
                return carry

            lax.fori_loop(0, nsub, sub, 0)

        _sc_pipeline(
            body, n_tok // TOK_STEP,
            [pl.BlockSpec((nsub, PICK_GROUP), lambda i: (i, 0)),
             pl.BlockSpec((TOK_STEP, d), lambda i: (i, 0))],
            [pl.BlockSpec((nsub, PICK_GROUP * SC_LANES), lambda i: (i, 0))],
            (i_hbm, h_hbm, o_hbm))

    return k(u_p, idx2, h)


def peer_combine_sc(v_p, idx2, coef_b):
    n_groups = idx2.shape[0]
    half = v_p.shape[1]
    n_tok = n_groups // GROUPS_PER_TOK
    nsub = TOK_STEP * GROUPS_PER_TOK
    n_chunks = half // SC_LANES

    @functools.partial(
        pl.kernel, mesh=_sc_mesh(), compiler_params=pltpu.CompilerParams(needs_layout_passes=False),
        out_type=jax.ShapeDtypeStruct((n_tok, 2 * half), F32),
        scratch_types=[pltpu.VMEM((2, PICK_GROUP, half), jnp.int32),
                       pltpu.SemaphoreType.DMA((2,))],
    )
    def k(v_hbm, i_hbm, c_hbm, o_hbm, rows, sems):
        def body(i_v, c_v, o_v):
            def fetch(j, slot):
                return pltpu.make_async_copy(v_hbm.at[i_v.at[j]], rows.at[slot], sems.at[slot])

            fetch(0, 0).start()
            for j in range(nsub):
                slot = j % 2
                if j + 1 < nsub:
                    fetch(j + 1, 1 - slot).start()
                fetch(j, slot).wait()
                t = j // GROUPS_PER_TOK
                first = j % GROUPS_PER_TOK == 0
                cks = [c_v[j, pl.ds(kk * SC_LANES, SC_LANES)] for kk in range(PICK_GROUP)]

                def chunk(c, carry, slot=slot, t=t, first=first, cks=cks):
                    l = c * SC_LANES
                    los, his = [None] * 2, [None] * 2
                    for kk in range(PICK_GROUP):
                        lo, hi = _unpack(rows[slot, kk, pl.ds(l, SC_LANES)])
                        a, b_ = cks[kk] * lo, cks[kk] * hi
                        los[kk % 2] = a if los[kk % 2] is None else los[kk % 2] + a
                        his[kk % 2] = b_ if his[kk % 2] is None else his[kk % 2] + b_
                    tot_lo, tot_hi = los[0] + los[1], his[0] + his[1]
                    if not first:
                        tot_lo = tot_lo + o_v[t, pl.ds(l, SC_LANES)]
                        tot_hi = tot_hi + o_v[t, pl.ds(half + l, SC_LANES)]
                    o_v[t, pl.ds(l, SC_LANES)] = tot_lo
                    o_v[t, pl.ds(half + l, SC_LANES)] = tot_hi
                    return carry

                lax.fori_loop(0, n_chunks, chunk, 0)

        _sc_pipeline(
            body, n_tok // TOK_STEP,
            [pl.BlockSpec((nsub, PICK_GROUP), lambda i: (i, 0)),
             pl.BlockSpec((nsub, PICK_GROUP * SC_LANES), lambda i: (i, 0))],
            [pl.BlockSpec((TOK_STEP, 2 * half), lambda i: (i, 0))],
            (i_hbm, c_hbm, o_hbm))

    return k(v_p, idx2, coef_b)


def _segment_matrix():
    r = lax.broadcasted_iota(jnp.int32, (PICK_GROUP * SC_LANES, PICK_GROUP), 0) // SC_LANES
    c = lax.broadcasted_iota(jnp.int32, (PICK_GROUP * SC_LANES, PICK_GROUP), 1)
    return (r == c).astype(F32)


def _coef_kernel(part_ref, gate_ref, o_ref):
    seg = _segment_matrix()
    act = jnp.dot(part_ref[...], seg, preferred_element_type=F32, precision=lax.Precision.HIGHEST)
    coef = gate_ref[...] * (0.5 * act * (1.0 + lax.erf(act * (2.0 ** -0.5))))
    o_ref[...] = lax.dot_general(coef, seg, (((1,), (1,)), ((), ())), preferred_element_type=F32,
                                 precision=lax.Precision.HIGHEST)


def peer_coef_tc(part, gate2):
    n_groups, width = part.shape
    tile = 1024
    return pl.pallas_call(
        _coef_kernel,
        grid=(n_groups // tile,),
        in_specs=[pl.BlockSpec((tile, width), lambda i: (i, 0)),
                  pl.BlockSpec((tile, PICK_GROUP), lambda i: (i, 0))],
        out_specs=pl.BlockSpec((tile, width), lambda i: (i, 0)),
        out_shape=jax.ShapeDtypeStruct((n_groups, width), F32),
        compiler_params=_cparams("parallel"), name="peer_coef",
    )(part, gate2)


def _final_kernel(x_ref, y_ref, mod_ref, g_ref, o_ref):
    x = x_ref[0] + mod_ref[0, 5:6, :] * y_ref[0]
    o_ref[0] = x * lax.rsqrt(jnp.mean(x * x, axis=-1, keepdims=True) + EPS) * g_ref[...]


def final_tc(x1, y, mod_l, final_g):
    b, l, d = x1.shape
    tok = pl.BlockSpec((1, TOK_TILE, d), lambda bi, i: (bi, i, 0))
    return pl.pallas_call(
        _final_kernel, grid=(b, l // TOK_TILE),
        in_specs=[tok, tok, pl.BlockSpec((1,) + mod_l.shape[1:], lambda bi, i: (bi, 0, 0)),
                  pl.BlockSpec((1, d), lambda bi, i: (0, 0))],
        out_specs=tok, out_shape=jax.ShapeDtypeStruct((b, l, d), F32),
        compiler_params=_cparams("parallel", "arbitrary"), name="final_norm",
    )(x1, y, mod_l, final_g.reshape(1, d))


def _pad_cols(w, n):
    return jnp.pad(w, ((0, 0), (0, n - w.shape[1])))


def _exact_zero(v):
    return jnp.minimum(jnp.abs(v), 0.0)


def forward(x, c, ctx, c_ctx, w_ada, b_ada, norm1_g, norm2_g, w_in, conv_w, dn_a_log,
            dn_dt_bias, dn_norm_g, gla_wa2, gla_ba, gla_norm_g, w_out, peer_wq, peer_keys,
            peer_u, peer_v, final_g):
    b, l, d = x.shape
    c_all = jnp.concatenate([c, c_ctx[None]], axis=0)
    c_all = jnp.pad(c_all, ((0, (-c_all.shape[0]) % SUBLANES), (0, 0)))
    mod = adaln_mod(c_all, w_ada, b_ada)
    mod_l = mod[:b].reshape(b, 6, d)
    mod_c = jnp.broadcast_to(mod[b].reshape(1, 6, d), (b, 6, d))
    o = DN_QKV
    hv = DN_HEADS * DN_DV
    w_dn_qkv, w_dn_z = w_in[:, :o], w_in[:, o:o + hv]
    w_dn_ba = _pad_cols(w_in[:, o + hv:DN_COLS], LANES)
    g0 = DN_COLS
    gqk, gv = 2 * GLA_HEADS * GLA_DK, GLA_HEADS * GLA_DV
    w_gl_qk, w_gl_v = w_in[:, g0:g0 + gqk], w_in[:, g0 + gqk:g0 + gqk + gv]
    w_gl_r = w_in[:, g0 + gqk + gv:g0 + gqk + 2 * gv]
    w_gl_lr = _pad_cols(w_in[:, g0 + gqk + 2 * gv:], LANES)
    w_lat = [w.astype(BF16) for w in (w_dn_qkv, w_dn_ba, w_gl_qk, w_gl_v, w_gl_lr, w_dn_z, w_gl_r)]
    w_ctx = w_lat[:5]
    w_la = jnp.zeros((2, LANES, GLA_HEADS * GLA_DK), F32)
    for dd in range(2):
        w_la = w_la.at[dd, dd * GLA_LR:(dd + 1) * GLA_LR].set(gla_wa2[dd])
    w_la = w_la.astype(BF16)
    b_la = gla_ba.reshape(2, 1, GLA_HEADS * GLA_DK)
    w_out_b, w_q_b = w_out.astype(BF16), peer_wq.astype(BF16)
    u_p, v_p = pack_table(peer_u), pack_table(peer_v)

    def group(xg, ctxg, mod_cg, mod_g, after_select, after_coef):
        bg = xg.shape[0]
        mod_cg, mod_g = mod_cg + after_select, mod_g + after_select
        c_qkv, c_ba, c_qk, c_v, c_lr = in_projection(ctxg, norm1_g, mod_cg[:, 0:2], w_ctx)
        l_qkv, l_ba, l_qk, l_v, l_lr, l_z, l_r = in_projection(xg, norm1_g, mod_g[:, 0:2], w_lat)
        feat_c, feat_l = dn_features_tc(c_qkv, conv_w), dn_features_tc(l_qkv, conv_w)
        dn_f = dn_scan_tc(feat_c, feat_l, c_ba, l_ba, dn_a_log, dn_dt_bias, rev=False)
        dn_b = dn_scan_tc(feat_c, feat_l, c_ba, l_ba, dn_a_log, dn_dt_bias, rev=True)
        gl_f = gla_scan_tc(c_qk, c_v, c_lr, l_qk, l_v, l_lr, w_la, b_la, rev=False)
        gl_b = gla_scan_tc(c_qk, c_v, c_lr, l_qk, l_v, l_lr, w_la, b_la, rev=True)
        x1, h2, q = mix_out_tc(xg, dn_f, dn_b, l_z, gl_f, gl_b, l_r, mod_g, dn_norm_g, gla_norm_g,
                               norm2_g, w_out_b, w_q_b)
        n_tok = bg * l
        idx, gate = peer_select_tc(q.reshape(n_tok, -1), peer_keys + after_coef)
        idx2 = idx.reshape(n_tok * GROUPS_PER_TOK, PICK_GROUP)
        gate2 = gate.reshape(n_tok * GROUPS_PER_TOK, PICK_GROUP)
        part = peer_act_partial_sc(u_p, idx2, h2.reshape(n_tok, d))
        coef_b = peer_coef_tc(part, gate2)
        y = peer_combine_sc(v_p, idx2, coef_b)
        out = final_tc(x1, y.reshape(bg, l, d), mod_g, final_g)
        return out, _exact_zero(gate[0, 0]), _exact_zero(coef_b[0, 0])

    bg = b // BATCH_GROUPS
    outs, z_sel, z_coef = [], jnp.zeros((), F32), jnp.zeros((), F32)
    for i in range(0, b, bg):
        o, z_sel, z_coef = group(x[i:i + bg], ctx[i:i + bg], mod_c[i:i + bg], mod_l[i:i + bg], z_sel, z_coef)
        outs.append(o)
    return jnp.concatenate(outs, axis=0)


def kernel(x, c, ctx, c_ctx, w_ada, b_ada, norm1_g, norm2_g, w_in, conv_w, dn_a_log,
           dn_dt_bias, dn_norm_g, gla_wa2, gla_ba, gla_norm_g, w_out, peer_wq, peer_keys,
           peer_u, peer_v, final_g):
    assert w_ada.shape[0] == 1, "single-layer block: the context stream is only consumed, never updated"
    return forward(x, c, ctx, c_ctx, w_ada[0], b_ada[0], norm1_g[0], norm2_g[0], w_in[0], conv_w[0],
                   dn_a_log[0], dn_dt_bias[0], dn_norm_g[0], gla_wa2[0], gla_ba[0], gla_norm_g[0],
                   w_out[0], peer_wq[0], peer_keys[0], peer_u[0], peer_v[0], final_g)
```

```python
import functools

import jax
import jax.numpy as jnp
from jax import lax
from jax.experimental import pallas as pl
from jax.experimental.pallas import tpu as pltpu
from jax.experimental.pallas import tpu_sc as plsc

GRID_W = 64
DN_HEADS = 4
DN_DK = 128
DN_DV = 128
CONV_W = 5
GLA_HEADS = 4
GLA_DK = 64
GLA_DV = 128
GLA_LR = 16
GLA_TAU = 16.0
CHUNK = 64
PEER_HEADS = 8
PEER_NKEYS = 128
PEER_DQ = 256
PEER_TOPK = 16
EPS = 1e-6
DN_QKV = 2 * DN_HEADS * DN_DK + DN_HEADS * DN_DV
DN_COLS = DN_QKV + DN_HEADS * DN_DV + 4 * DN_HEADS

SUBLANES = 8
LANES = 128
SC_LANES = 16
VMEM_LIMIT_BYTES = 48 * 1024 * 1024

TOK_TILE = 256
SELECT_TILE = 256
GLA_SUB = 16
PICK_GROUP = 32
TOK_STEP = 2
ACT_UNROLL = 8
BATCH_GROUPS = 4
PICKS = PEER_HEADS * PEER_TOPK
GROUPS_PER_TOK = PICKS // PICK_GROUP

F32 = jnp.float32
BF16 = jnp.bfloat16


def _cparams(*semantics):
    return pltpu.CompilerParams(dimension_semantics=semantics, vmem_limit_bytes=VMEM_LIMIT_BYTES)


def _dot(a, b):
    return jnp.dot(a.astype(BF16), b.astype(BF16), preferred_element_type=F32)


def _dot_nt(a, b):
    return lax.dot_general(a.astype(BF16), b.astype(BF16), (((1,), (1,)), ((), ())),
                           preferred_element_type=F32)


def _dot_tn(a, b):
    return lax.dot_general(a.astype(BF16), b.astype(BF16), (((0,), (0,)), ((), ())),
                           preferred_element_type=F32)


def _split(x):
    hi = x.astype(BF16)
    return hi, (x - hi.astype(F32)).astype(BF16)


def _mask_dot(mask_bf16, x):
    hi, lo = _split(x)
    return (jnp.dot(mask_bf16, hi, preferred_element_type=F32)
            + jnp.dot(mask_bf16, lo, preferred_element_type=F32))


def _softplus(x):
    return jnp.maximum(x, 0.0) + jnp.log(1.0 + jnp.exp(-jnp.abs(x)))


def _tri_masks(rev):
    r = lax.broadcasted_iota(jnp.int32, (CHUNK, CHUNK), 0)
    c = lax.broadcasted_iota(jnp.int32, (CHUNK, CHUNK), 1)
    d = (c - r) if rev else (r - c)
    return d >= 0, d > 0


def _mod_kernel(c_ref, w_ref, b_ref, o_ref):
    c = c_ref[...]
    s = c * jax.nn.sigmoid(c)
    o_ref[...] = jnp.dot(s, w_ref[...], preferred_element_type=F32,
                         precision=lax.Precision.HIGHEST) + b_ref[...]


def adaln_mod(c_all, w_ada, b_ada):
    r, d = c_all.shape
    n = w_ada.shape[1]
    tn = 512
    return pl.pallas_call(
        _mod_kernel, grid=(n // tn,),
        in_specs=[pl.BlockSpec((r, d), lambda j: (0, 0)),
                  pl.BlockSpec((d, tn), lambda j: (0, j)),
                  pl.BlockSpec((1, tn), lambda j: (0, j))],
        out_specs=pl.BlockSpec((r, tn), lambda j: (0, j)),
        out_shape=jax.ShapeDtypeStruct((r, n), F32),
        compiler_params=_cparams("arbitrary"), name="adaln_mod",
    )(c_all, w_ada, b_ada.reshape(1, n))


def _inproj_kernel(x_ref, g_ref, mod_ref, *refs):
    n_out = len(refs) // 2
    x = x_ref[0]
    y = x * lax.rsqrt(jnp.mean(x * x, axis=-1, keepdims=True) + EPS) * g_ref[...]
    h = (y * (1.0 + mod_ref[0, 1:2, :]) + mod_ref[0, 0:1, :]).astype(BF16)
    for w_ref, o_ref in zip(refs[:n_out], refs[n_out:]):
        o_ref[0] = jnp.dot(h, w_ref[...], preferred_element_type=F32)


def in_projection(x, norm_g, mod, weights):
    b, l, d = x.shape
    w_specs = [pl.BlockSpec(w.shape, lambda bi, i: (0, 0)) for w in weights]
    o_specs = [pl.BlockSpec((1, TOK_TILE, w.shape[1]), lambda bi, i: (bi, i, 0)) for w in weights]
    return pl.pallas_call(
        _inproj_kernel, grid=(b, l // TOK_TILE),
        in_specs=[pl.BlockSpec((1, TOK_TILE, d), lambda bi, i: (bi, i, 0)),
                  pl.BlockSpec((1, d), lambda bi, i: (0, 0)),
                  pl.BlockSpec((1, 2, d), lambda bi, i: (bi, 0, 0))] + w_specs,
        out_specs=o_specs,
        out_shape=[jax.ShapeDtypeStruct((b, l, w.shape[1]), F32) for w in weights],
        compiler_params=_cparams("parallel", "arbitrary"), name="in_projection",
    )(x, norm_g.reshape(1, d), mod, *weights)


def _dn_feature_kernel(x_ref, w_ref, o_ref):
    x = x_ref[0]
    n = x.shape[0]
    t = lax.broadcasted_iota(jnp.int32, (n, 1), 0)
    pad = CONV_W // 2
    acc = w_ref[0, pad:pad + 1, :] * x
    for j in range(CONV_W):
        s = j - pad
        if s == 0:
            continue
        xs = pltpu.roll(x, (-s) % n, axis=0)
        bad = (t < -s) if s < 0 else (t >= n - s)
        acc = acc + w_ref[0, j:j + 1, :] * jnp.where(bad, 0.0, xs)
    y = acc * jax.nn.sigmoid(acc)
    kind = pl.program_id(1) // DN_HEADS
    inv = lax.rsqrt(jnp.sum(y * y, axis=-1, keepdims=True) + EPS)
    scale = jnp.where(kind == 0, inv * DN_DK ** -0.5, jnp.where(kind == 1, inv, 1.0))
    o_ref[0] = y * scale


def dn_features_tc(qkv, conv_w):
    b, l, n = qkv.shape
    nblk = n // LANES
    w = jnp.zeros((nblk, SUBLANES, LANES), F32).at[:, :CONV_W].set(
        conv_w.reshape(CONV_W, nblk, LANES).transpose(1, 0, 2))
    return pl.pallas_call(
        _dn_feature_kernel, grid=(b, nblk),
        in_specs=[pl.BlockSpec((1, l, LANES), lambda bi, j: (bi, 0, j)),
                  pl.BlockSpec((1, SUBLANES, LANES), lambda bi, j: (j, 0, 0))],
        out_specs=pl.BlockSpec((1, l, LANES), lambda bi, j: (bi, 0, j)),
        out_shape=jax.ShapeDtypeStruct((b, l, n), F32),
        compiler_params=_cparams("parallel", "arbitrary"), name="dn_features",
    )(qkv, w)


def _scan_chunks(rev, n_ctx, n_lat):
    if rev:
        ctx = lambda j: jnp.maximum(n_ctx - 1 - j, 0)
        lat = lambda j: jnp.where(j < n_ctx, n_lat - 1, n_lat - 1 - (j - n_ctx))
    else:
        ctx = lambda j: jnp.minimum(j, n_ctx - 1)
        lat = lambda j: jnp.maximum(j - n_ctx, 0)
    return ctx, lat


def _dn_scan_kernel(rev, dirn, n_ctx, alog_ref, dtb_ref, fc_ref, fl_ref, bac_ref, bal_ref, o_ref, s_ref):
    step = pl.program_id(1)

    @pl.when(step == 0)
    def _():
        s_ref[...] = jnp.zeros_like(s_ref)

    in_ctx = step < n_ctx
    f = jnp.where(in_ctx, fc_ref[0], fl_ref[0])
    ba = jnp.where(in_ctx, bac_ref[0], bal_ref[0])
    incl, strict = _tri_masks(rev)
    incl_b = incl.astype(BF16)
    eye = (lax.broadcasted_iota(jnp.int32, (CHUNK, CHUNK), 0)
           == lax.broadcasted_iota(jnp.int32, (CHUNK, CHUNK), 1)).astype(F32)
    lane = lax.broadcasted_iota(jnp.int32, (CHUNK, LANES), 1)
    onehot0 = (lane == 0).astype(BF16)
    hd = DN_HEADS * DN_DK
    outs = []
    for h in range(DN_HEADS):
        q = f[:, h * DN_DK:(h + 1) * DN_DK]
        k = f[:, hd + h * DN_DK:hd + (h + 1) * DN_DK]
        v = f[:, 2 * hd + h * DN_DV:2 * hd + (h + 1) * DN_DV]
        cb = dirn * DN_HEADS + h
        beta = jax.nn.sigmoid(ba[:, cb:cb + 1])
        a = ba[:, 2 * DN_HEADS + cb:2 * DN_HEADS + cb + 1]
        g = -jnp.exp(jnp.full((CHUNK, 1), alog_ref[dirn, h], F32)) * _softplus(a + dtb_ref[dirn, h])
        gmat = jnp.broadcast_to(g, (CHUNK, LANES))
        gcb = _mask_dot(incl_b, gmat)
        g_tot = jnp.sum(gmat, axis=0, keepdims=True)
        hi, lo = _split(gcb)
        g_row = (lax.dot_general(onehot0, hi, (((1,), (1,)), ((), ())), preferred_element_type=F32)
                 + lax.dot_general(onehot0, lo, (((1,), (1,)), ((), ())), preferred_element_type=F32))
        decay = jnp.where(incl, jnp.exp(jnp.where(incl, gcb[:, :CHUNK] - g_row, 0.0)), 0.0)
        kb = k * beta
        lower = jnp.where(strict, _dot_nt(kb, k) * decay, 0.0)
        eg = jnp.exp(gcb)
        inv = eye - lower
        pw = lower
        for _ in range(5):
            pw = _dot(pw, pw)
            inv = inv + _dot(inv, pw)
        sol = _dot(inv, jnp.concatenate([v * beta, kb * eg], axis=-1))
        u, w = sol[:, :DN_DV], sol[:, DN_DV:]
        k_dec = k * jnp.exp(g_tot - gcb)
        s = s_ref[h]
        v_new = u - _dot(w, s)
        a_qk = _dot_nt(q, k) * decay
        outs.append(_dot(q * eg, s) + _dot(a_qk, v_new))
        s_ref[h] = s * jnp.exp(g_tot) + _dot_tn(k_dec, v_new)
    o_ref[0] = jnp.concatenate(outs, axis=-1)


def dn_scan_tc(feat_c, feat_l, ba_c, ba_l, a_log, dt_bias, rev):
    b, l, nf = feat_l.shape
    n_ctx, n_lat = feat_c.shape[1] // CHUNK, l // CHUNK
    dirn = 1 if rev else 0
    cc, lc = _scan_chunks(rev, n_ctx, n_lat)
    smem = pl.BlockSpec(memory_space=pltpu.SMEM)
    return pl.pallas_call(
        functools.partial(_dn_scan_kernel, rev, dirn, n_ctx), grid=(b, n_ctx + n_lat),
        in_specs=[smem, smem,
                  pl.BlockSpec((1, CHUNK, nf), lambda bi, j: (bi, cc(j), 0)),
                  pl.BlockSpec((1, CHUNK, nf), lambda bi, j: (bi, lc(j), 0)),
                  pl.BlockSpec((1, CHUNK, LANES), lambda bi, j: (bi, cc(j), 0)),
                  pl.BlockSpec((1, CHUNK, LANES), lambda bi, j: (bi, lc(j), 0))],
        out_specs=pl.BlockSpec((1, CHUNK, DN_HEADS * DN_DV), lambda bi, j: (bi, lc(j), 0)),
        out_shape=jax.ShapeDtypeStruct((b, l, DN_HEADS * DN_DV), F32),
        scratch_shapes=[pltpu.VMEM((DN_HEADS, DN_DK, DN_DV), F32)],
        compiler_params=_cparams("parallel", "arbitrary"), name="dn_scan_bwd" if rev else "dn_scan_fwd",
    )(a_log, dt_bias, feat_c, feat_l, ba_c, ba_l)


def _from_grid_cols(blk, n):
    cols = blk.shape[1] // n
    return jnp.concatenate([blk[:, i * n:(i + 1) * n] for i in range(cols)], axis=0)


def _gla_scan_kernel(rev, dirn, n_ctx, qkc_ref, vc_ref, lrc_ref, qkl_ref, vl_ref, lrl_ref,
                     wla_ref, bla_ref, o_ref, s_ref):
    step = pl.program_id(1)

    @pl.when(step == 0)
    def _():
        s_ref[...] = jnp.zeros_like(s_ref)

    in_ctx = step < n_ctx
    hk, hv = GLA_HEADS * GLA_DK, GLA_HEADS * GLA_DV
    qk = jnp.where(in_ctx, qkc_ref[0], _from_grid_cols(qkl_ref[0], 2 * hk))
    vv = jnp.where(in_ctx, vc_ref[0], _from_grid_cols(vl_ref[0], hv))
    lr = jnp.where(in_ctx, lrc_ref[0], _from_grid_cols(lrl_ref[0], LANES))
    incl, _ = _tri_masks(rev)
    incl_b = incl.astype(BF16)
    pre = _dot(lr, wla_ref[0]) + bla_ref[0]
    la_all = -_softplus(-pre) * (1.0 / GLA_TAU)
    outs = []
    for h in range(GLA_HEADS):
        q = qk[:, h * GLA_DK:(h + 1) * GLA_DK] * GLA_DK ** -0.5
        k = qk[:, hk + h * GLA_DK:hk + (h + 1) * GLA_DK]
        v = vv[:, h * GLA_DV:(h + 1) * GLA_DV]
        la = la_all[:, h * GLA_DK:(h + 1) * GLA_DK]
        bc = _mask_dot(incl_b, la)
        b_tot = jnp.sum(la, axis=0, keepdims=True)
        st = s_ref[h]
        o = _dot_nt(q * jnp.exp(bc), st)
        parts = []
        for i in range(CHUNK // GLA_SUB):
            lo_r, hi_r = i * GLA_SUB, (i + 1) * GLA_SUB
            if rev:
                ref = bc[hi_r - 1:hi_r]
                c0, c1 = lo_r, CHUNK
            else:
                ref = bc[lo_r:lo_r + 1]
                c0, c1 = 0, hi_r
            qi = q[lo_r:hi_r] * jnp.exp(bc[lo_r:hi_r] - ref)
            ki = k[c0:c1] * jnp.exp(ref - bc[c0:c1])
            att = _dot_nt(qi, ki)
            rg = lax.broadcasted_iota(jnp.int32, (GLA_SUB, c1 - c0), 0) + lo_r
            cg = lax.broadcasted_iota(jnp.int32, (GLA_SUB, c1 - c0), 1) + c0
            keep = (cg >= rg) if rev else (cg <= rg)
            parts.append(_dot(jnp.where(keep, att, 0.0), v[c0:c1]))
        outs.append(o + jnp.concatenate(parts, axis=0))
        k_dec = k * jnp.exp(b_tot - bc)
        s_ref[h] = st * jnp.exp(b_tot) + _dot_tn(v, k_dec)
    o = jnp.concatenate(outs, axis=-1)
    rows = o_ref.shape[1]
    o_ref[0] = jnp.concatenate([o[i * rows:(i + 1) * rows] for i in range(CHUNK // rows)], axis=-1)


def gla_scan_tc(qk_c, v_c, lr_c, qk_l, v_l, lr_l, w_la, b_la, rev):
    b, l, _ = qk_l.shape
    rows = l // GRID_W
    cols = CHUNK // rows
    n_ctx, n_lat = qk_c.shape[1] // CHUNK, l // CHUNK
    dirn = 1 if rev else 0
    cc, lc = _scan_chunks(rev, n_ctx, n_lat)
    hv = GLA_HEADS * GLA_DV
    ctx_blk = lambda a: pl.BlockSpec((1, CHUNK, a.shape[2]), lambda bi, j: (bi, cc(j), 0))
    lat_blk = lambda n: pl.BlockSpec((1, rows, cols * n), lambda bi, j: (bi, 0, lc(j)))
    grid_view = lambda a: a.reshape(b, rows, GRID_W * a.shape[2])
    out = pl.pallas_call(
        functools.partial(_gla_scan_kernel, rev, dirn, n_ctx), grid=(b, n_ctx + n_lat),
        in_specs=[ctx_blk(qk_c), ctx_blk(v_c), ctx_blk(lr_c),
                  lat_blk(qk_l.shape[2]), lat_blk(v_l.shape[2]), lat_blk(lr_l.shape[2]),
                  pl.BlockSpec((1,) + w_la.shape[1:], lambda bi, j: (dirn, 0, 0)),
                  pl.BlockSpec((1,) + b_la.shape[1:], lambda bi, j: (dirn, 0, 0))],
        out_specs=lat_blk(hv),
        out_shape=jax.ShapeDtypeStruct((b, rows, GRID_W * hv), F32),
        scratch_shapes=[pltpu.VMEM((GLA_HEADS, GLA_DV, GLA_DK), F32)],
        compiler_params=_cparams("parallel", "arbitrary"), name="gla_scan_bwd" if rev else "gla_scan_fwd",
    )(qk_c, v_c, lr_c, grid_view(qk_l), grid_view(v_l), grid_view(lr_l), w_la, b_la)
    return out.reshape(b, l, hv)


def _head_norm_gate(o, gate, g, n_heads, dv):
    parts = []
    for h in range(n_heads):
        oh = o[:, h * dv:(h + 1) * dv]
        gh = gate[:, h * dv:(h + 1) * dv]
        yh = oh * lax.rsqrt(jnp.mean(oh * oh, axis=-1, keepdims=True) + EPS) * g
        parts.append(yh * (gh * jax.nn.sigmoid(gh)))
    return parts


def _mix_out_kernel(x_ref, dnf_ref, dnb_ref, z_ref, glf_ref, glb_ref, r_ref, mod_ref, dng_ref,
                    glg_ref, n2g_ref, wout_ref, wq_ref, x1_ref, h2_ref, q_ref):
    parts = (_head_norm_gate(dnf_ref[0] + dnb_ref[0], z_ref[0], dng_ref[...], DN_HEADS, DN_DV)
             + _head_norm_gate(glf_ref[0] + glb_ref[0], r_ref[0], glg_ref[...], GLA_HEADS, GLA_DV))
    y = jnp.dot(jnp.concatenate(parts, axis=-1).astype(BF16), wout_ref[...], preferred_element_type=F32)
    x1 = x_ref[0] + mod_ref[0, 2:3, :] * y
    x1_ref[0] = x1
    n = x1 * lax.rsqrt(jnp.mean(x1 * x1, axis=-1, keepdims=True) + EPS) * n2g_ref[...]
    h2 = n * (1.0 + mod_ref[0, 4:5, :]) + mod_ref[0, 3:4, :]
    h2_ref[0] = h2
    q_ref[0] = jnp.dot(h2.astype(BF16), wq_ref[...], preferred_element_type=F32)


def mix_out_tc(x, dn_f, dn_b, z, gl_f, gl_b, r, mod_l, dn_g, gla_g, n2_g, w_out, w_q):
    b, l, d = x.shape
    tok = lambda n: pl.BlockSpec((1, TOK_TILE, n), lambda bi, i: (bi, i, 0))
    full = lambda a: pl.BlockSpec(a.shape, lambda bi, i: (0,) * a.ndim)
    dn_g, gla_g, n2_g = dn_g.reshape(1, -1), gla_g.reshape(1, -1), n2_g.reshape(1, -1)
    nq = w_q.shape[1]
    return pl.pallas_call(
        _mix_out_kernel, grid=(b, l // TOK_TILE),
        in_specs=[tok(d), tok(dn_f.shape[2]), tok(dn_b.shape[2]), tok(z.shape[2]),
                  tok(gl_f.shape[2]), tok(gl_b.shape[2]), tok(r.shape[2]),
                  pl.BlockSpec((1,) + mod_l.shape[1:], lambda bi, i: (bi, 0, 0)),
                  full(dn_g), full(gla_g), full(n2_g), full(w_out), full(w_q)],
        out_specs=[tok(d), tok(d), tok(nq)],
        out_shape=[jax.ShapeDtypeStruct((b, l, d), F32), jax.ShapeDtypeStruct((b, l, d), F32),
                   jax.ShapeDtypeStruct((b, l, nq), F32)],
        compiler_params=_cparams("parallel", "arbitrary"), name="mix_out",
    )(x, dn_f, dn_b, z, gl_f, gl_b, r, mod_l, dn_g, gla_g, n2_g, w_out, w_q)


def _top_rows(s, k, payload=None):
    n = s.shape[0]
    row = lax.broadcasted_iota(jnp.int32, s.shape, 0).astype(F32)
    vals, picked = [], []
    for _ in range(k):
        m = jnp.max(s, axis=0, keepdims=True)
        first = jnp.min(jnp.where(s == m, row, float(n)), axis=0, keepdims=True)
        sel = row == first
        vals.append(m)
        if payload is None:
            picked.append(first)
        else:
            picked.append(jnp.max(jnp.where(sel, payload, -1.0), axis=0, keepdims=True))
        s = jnp.where(sel, -jnp.inf, s)
    return jnp.concatenate(vals, axis=0), jnp.concatenate(picked, axis=0)


def _select_kernel(q_ref, k_ref, idx_ref, gate_ref, idx_s, gate_s):
    half = PEER_DQ // 2

    def head(h, carry):
        tops = []
        for p in range(2):
            qp = q_ref[:, pl.ds(pl.multiple_of(h * PEER_DQ + p * half, half), half)]
            s = lax.dot_general(k_ref[h, p], qp, (((1,), (1,)), ((), ())),
                                preferred_element_type=F32,
                                precision=lax.Precision.HIGHEST)
            tops.append(_top_rows(s, PEER_TOPK))
        (s0, i0), (s1, i1) = tops
        cand_s = jnp.concatenate([s0[i:i + 1] + s1 for i in range(PEER_TOPK)], axis=0)
        cand_i = jnp.concatenate([i0[i:i + 1] * float(PEER_NKEYS) + i1 for i in range(PEER_TOPK)], axis=0)
        best_s, idx = _top_rows(cand_s, PEER_TOPK, payload=cand_i)
        e = jnp.exp(best_s - best_s[0:1])
        r0 = pl.multiple_of(h * PEER_TOPK, PEER_TOPK)
        idx_s[pl.ds(r0, PEER_TOPK), :] = idx
        gate_s[pl.ds(r0, PEER_TOPK), :] = e / jnp.sum(e, axis=0, keepdims=True)
        return carry

    lax.fori_loop(0, PEER_HEADS, head, 0)
    idx_ref[...] = idx_s[...].T.astype(jnp.int32)
    gate_ref[...] = gate_s[...].T


def peer_select_tc(q, keys):
    n_tok = q.shape[0]
    out_spec = pl.BlockSpec((SELECT_TILE, PICKS), lambda i: (i, 0))
    return pl.pallas_call(
        _select_kernel,
        grid=(n_tok // SELECT_TILE,),
        in_specs=[pl.BlockSpec((SELECT_TILE, q.shape[1]), lambda i: (i, 0)),
                  pl.BlockSpec(keys.shape, lambda i: (0, 0, 0, 0))],
        out_specs=[out_spec, out_spec],
        out_shape=[jax.ShapeDtypeStruct((n_tok, PICKS), jnp.int32),
                   jax.ShapeDtypeStruct((n_tok, PICKS), F32)],
        scratch_shapes=[pltpu.VMEM((PICKS, SELECT_TILE), F32), pltpu.VMEM((PICKS, SELECT_TILE), F32)],
        compiler_params=_cparams("parallel"), name="peer_select",
    )(q, keys)


def _sc_mesh():
    return plsc.VectorSubcoreMesh(core_axis_name="c", subcore_axis_name="s")


def _sc_pipeline(body, n_steps, in_specs, out_specs, operands):
    pltpu.emit_pipeline(
        body, grid=(n_steps,), in_specs=in_specs, out_specs=out_specs,
        core_axis_name=("c", "s"), dimension_semantics=(pltpu.PARALLEL,),
        trace_scopes=False,
    )(*operands)


def pack_table(t):
    half = t.shape[1] // 2
    bits = lax.bitcast_convert_type(t, jnp.uint32)
    lo_bits, hi_bits = bits[:, :half], bits[:, half:]
    low = (lo_bits + jnp.uint32(0x7FFF) + ((lo_bits >> 16) & jnp.uint32(1))) >> 16
    mag = hi_bits & jnp.uint32(0x7FFFFFFF)
    top = jnp.where(mag >= jnp.uint32(0x10000),
                    (mag - low + jnp.uint32(0x8000)) & jnp.uint32(0xFFFF0000), jnp.uint32(0))
    word = (hi_bits & jnp.uint32(0x80000000)) | top | low
    return lax.bitcast_convert_type(word, jnp.int32)


def _unpack(w):
    return plsc.bitcast(w << 16, F32), plsc.bitcast(w, F32)


def peer_act_partial_sc(u_p, idx2, h):
    n_groups, (n_tok, d) = idx2.shape[0], h.shape
    half = d // 2
    nsub = TOK_STEP * GROUPS_PER_TOK
    n_chunks = half // SC_LANES

    @functools.partial(
        pl.kernel, mesh=_sc_mesh(), compiler_params=pltpu.CompilerParams(needs_layout_passes=False),
        out_type=jax.ShapeDtypeStruct((n_groups, PICK_GROUP * SC_LANES), F32),
        scratch_types=[pltpu.VMEM((2, PICK_GROUP, half), jnp.int32),
                       pltpu.SemaphoreType.DMA((2,))],
    )
    def k(u_hbm, i_hbm, h_hbm, o_hbm, rows, sems):
        def body(i_v, h_v, o_v):
            def fetch(j, slot):
                return pltpu.make_async_copy(u_hbm.at[i_v.at[j]], rows.at[slot], sems.at[slot])

            fetch(0, 0).start()

            def sub(j, carry):
                slot = j % 2

                @pl.when(j + 1 < nsub)
                def _():
                    fetch(j + 1, 1 - slot).start()

                fetch(j, slot).wait()
                t = j // GROUPS_PER_TOK

                def picks(g, carry2):
                    kb = g * ACT_UNROLL
                    accs = [None] * ACT_UNROLL
                    for c in range(n_chunks):
                        h_lo = h_v[t, pl.ds(c * SC_LANES, SC_LANES)]
                        h_hi = h_v[t, pl.ds(half + c * SC_LANES, SC_LANES)]
                        for i in range(ACT_UNROLL):
                            lo, hi = _unpack(rows[slot, kb + i, pl.ds(c * SC_LANES, SC_LANES)])
                            p = lo * h_lo + hi * h_hi
                            accs[i] = p if accs[i] is None else accs[i] + p
                    for i in range(ACT_UNROLL):
                        o_v[j, pl.ds((kb + i) * SC_LANES, SC_LANES)] = accs[i]
                    return carry2

                lax.fori_loop(0, PICK_GROUP // ACT_UNROLL, picks, 0)
                return carry

            lax.fori_loop(0, nsub, sub, 0)

        _sc_pipeline(
            body, n_tok // TOK_STEP,
            [pl.BlockSpec((nsub, PICK_GROUP), lambda i: (i, 0)),
             pl.BlockSpec((TOK_STEP, d), lambda i: (i, 0))],
            [pl.BlockSpec((nsub, PICK_GROUP * SC_LANES), lambda i: (i, 0))],
            (i_hbm, h_hbm, o_hbm))

    return k(u_p, idx2, h)


def peer_combine_sc(v_p, idx2, coef_b):
    n_groups = idx2.shape[0]
    half = v_p.shape[1]
    n_tok = n_groups // GROUPS_PER_TOK
    nsub = TOK_STEP * GROUPS_PER_TOK
    n_chunks = half // SC_LANES

    @functools.partial(
        pl.kernel, mesh=_sc_mesh(), compiler_params=pltpu.CompilerParams(needs_layout_passes=False),
        out_type=jax.ShapeDtypeStruct((n_tok, 2 * half), F32),
        scratch_types=[pltpu.VMEM((2, PICK_GROUP, half), jnp.int32),
                       pltpu.SemaphoreType.DMA((2,))],
    )
    def k(v_hbm, i_hbm, c_hbm, o_hbm, rows, sems):
        def body(i_v, c_v, o_v):
            def fetch(j, slot):
                return pltpu.make_async_copy(v_hbm.at[i_v.at[j]], rows.at[slot], sems.at[slot])

            fetch(0, 0).start()
            for j in range(nsub):
                slot = j % 2
                if j + 1 < nsub:
                    fetch(j + 1, 1 - slot).start()
                fetch(j, slot).wait()
                t = j // GROUPS_PER_TOK
                first = j % GROUPS_PER_TOK == 0
                cks = [c_v[j, pl.ds(kk * SC_LANES, SC_LANES)] for kk in range(PICK_GROUP)]

                def chunk(c, carry, slot=slot, t=t, first=first, cks=cks):
                    l = c * SC_LANES
                    los, his = [None] * 2, [None] * 2
                    for kk in range(PICK_GROUP):
                        lo, hi = _unpack(rows[slot, kk, pl.ds(l, SC_LANES)])
                        a, b_ = cks[kk] * lo, cks[kk] * hi
                        los[kk % 2] = a if los[kk % 2] is None else los[kk % 2] + a
                        his[kk % 2] = b_ if his[kk % 2] is None else his[kk % 2] + b_
                    tot_lo, tot_hi = los[0] + los[1], his[0] + his[1]
                    if not first:
                        tot_lo = tot_lo + o_v[t, pl.ds(l, SC_LANES)]
                        tot_hi = tot_hi + o_v[t, pl.ds(half + l, SC_LANES)]
                    o_v[t, pl.ds(l, SC_LANES)] = tot_lo
                    o_v[t, pl.ds(half + l, SC_LANES)] = tot_hi
                    return carry

                lax.fori_loop(0, n_chunks, chunk, 0)

        _sc_pipeline(
            body, n_tok // TOK_STEP,
            [pl.BlockSpec((nsub, PICK_GROUP), lambda i: (i, 0)),
             pl.BlockSpec((nsub, PICK_GROUP * SC_LANES), lambda i: (i, 0))],
            [pl.BlockSpec((TOK_STEP, 2 * half), lambda i: (i, 0))],
            (i_hbm, c_hbm, o_hbm))

    return k(v_p, idx2, coef_b)


def _segment_matrix():
    r = lax.broadcasted_iota(jnp.int32, (PICK_GROUP * SC_LANES, PICK_GROUP), 0) // SC_LANES
    c = lax.broadcasted_iota(jnp.int32, (PICK_GROUP * SC_LANES, PICK_GROUP), 1)
    return (r == c).astype(F32)


def _coef_kernel(part_ref, gate_ref, o_ref):
    seg = _segment_matrix()
    act = jnp.dot(part_ref[...], seg, preferred_element_type=F32, precision=lax.Precision.HIGHEST)
    coef = gate_ref[...] * (0.5 * act * (1.0 + lax.erf(act * (2.0 ** -0.5))))
    o_ref[...] = lax.dot_general(coef, seg, (((1,), (1,)), ((), ())), preferred_element_type=F32,
                                 precision=lax.Precision.HIGHEST)


def peer_coef_tc(part, gate2):
    n_groups, width = part.shape
    tile = 1024
    return pl.pallas_call(
        _coef_kernel,
        grid=(n_groups // tile,),
        in_specs=[pl.BlockSpec((tile, width), lambda i: (i, 0)),
                  pl.BlockSpec((tile, PICK_GROUP), lambda i: (i, 0))],
        out_specs=pl.BlockSpec((tile, width), lambda i: (i, 0)),
        out_shape=jax.ShapeDtypeStruct((n_groups, width), F32),
        compiler_params=_cparams("parallel"), name="peer_coef",
    )(part, gate2)


def _final_kernel(x_ref, y_ref, mod_ref, g_ref, o_ref):
    x = x_ref[0] + mod_ref[0, 5:6, :] * y_ref[0]
    o_ref[0] = x * lax.rsqrt(jnp.mean(x * x, axis=-1, keepdims=True) + EPS) * g_ref[...]


def final_tc(x1, y, mod_l, final_g):
    b, l, d = x1.shape
    tok = pl.BlockSpec((1, TOK_TILE, d), lambda bi, i: (bi, i, 0))
    return pl.pallas_call(
        _final_kernel, grid=(b, l // TOK_TILE),
        in_specs=[tok, tok, pl.BlockSpec((1,) + mod_l.shape[1:], lambda bi, i: (bi, 0, 0)),
                  pl.BlockSpec((1, d), lambda bi, i: (0, 0))],
        out_specs=tok, out_shape=jax.ShapeDtypeStruct((b, l, d), F32),
        compiler_params=_cparams("parallel", "arbitrary"), name="final_norm",
    )(x1, y, mod_l, final_g.reshape(1, d))


def _pad_cols(w, n):
    return jnp.pad(w, ((0, 0), (0, n - w.shape[1])))


def _exact_zero(v):
    return jnp.minimum(jnp.abs(v), 0.0)


def forward(x, c, ctx, c_ctx, w_ada, b_ada, norm1_g, norm2_g, w_in, conv_w, dn_a_log,
            dn_dt_bias, dn_norm_g, gla_wa2, gla_ba, gla_norm_g, w_out, peer_wq, peer_keys,
            peer_u, peer_v, final_g):
    b, l, d = x.shape
    c_all = jnp.concatenate([c, c_ctx[None]], axis=0)
    c_all = jnp.pad(c_all, ((0, (-c_all.shape[0]) % SUBLANES), (0, 0)))
    mod = adaln_mod(c_all, w_ada, b_ada)
    mod_l = mod[:b].reshape(b, 6, d)
    mod_c = jnp.broadcast_to(mod[b].reshape(1, 6, d), (b, 6, d))
    o = DN_QKV
    hv = DN_HEADS * DN_DV
    w_dn_qkv, w_dn_z = w_in[:, :o], w_in[:, o:o + hv]
    w_dn_ba = _pad_cols(w_in[:, o + hv:DN_COLS], LANES)
    g0 = DN_COLS
    gqk, gv = 2 * GLA_HEADS * GLA_DK, GLA_HEADS * GLA_DV
    w_gl_qk, w_gl_v = w_in[:, g0:g0 + gqk], w_in[:, g0 + gqk:g0 + gqk + gv]
    w_gl_r = w_in[:, g0 + gqk + gv:g0 + gqk + 2 * gv]
    w_gl_lr = _pad_cols(w_in[:, g0 + gqk + 2 * gv:], LANES)
    w_lat = [w.astype(BF16) for w in (w_dn_qkv, w_dn_ba, w_gl_qk, w_gl_v, w_gl_lr, w_dn_z, w_gl_r)]
    w_ctx = w_lat[:5]
    w_la = jnp.zeros((2, LANES, GLA_HEADS * GLA_DK), F32)
    for dd in range(2):
        w_la = w_la.at[dd, dd * GLA_LR:(dd + 1) * GLA_LR].set(gla_wa2[dd])
    w_la = w_la.astype(BF16)
    b_la = gla_ba.reshape(2, 1, GLA_HEADS * GLA_DK)
    w_out_b, w_q_b = w_out.astype(BF16), peer_wq.astype(BF16)
    u_p, v_p = pack_table(peer_u), pack_table(peer_v)

    def group(xg, ctxg, mod_cg, mod_g, after_select, after_coef, after_combine):
        bg = xg.shape[0]
        mod_cg, mod_g = mod_cg + after_select, mod_g + after_select
        c_qkv, c_ba, c_qk, c_v, c_lr = in_projection(ctxg, norm1_g, mod_cg[:, 0:2], w_ctx)
        l_qkv, l_ba, l_qk, l_v, l_lr, l_z, l_r = in_projection(xg, norm1_g, mod_g[:, 0:2], w_lat)
        feat_c, feat_l = dn_features_tc(c_qkv, conv_w), dn_features_tc(l_qkv, conv_w)
        dn_f = dn_scan_tc(feat_c, feat_l, c_ba, l_ba, dn_a_log, dn_dt_bias, rev=False)
        dn_b = dn_scan_tc(feat_c, feat_l, c_ba, l_ba, dn_a_log, dn_dt_bias, rev=True)
        gl_f = gla_scan_tc(c_qk, c_v, c_lr, l_qk, l_v, l_lr, w_la, b_la, rev=False)
        gl_b = gla_scan_tc(c_qk, c_v, c_lr, l_qk, l_v, l_lr, w_la, b_la, rev=True)
        x1, h2, q = mix_out_tc(xg, dn_f, dn_b, l_z, gl_f, gl_b, l_r, mod_g, dn_norm_g + after_combine,
                               gla_norm_g, norm2_g, w_out_b, w_q_b)
        n_tok = bg * l
        idx, gate = peer_select_tc(q.reshape(n_tok, -1), peer_keys + after_coef)
        idx2 = idx.reshape(n_tok * GROUPS_PER_TOK, PICK_GROUP)
        gate2 = gate.reshape(n_tok * GROUPS_PER_TOK, PICK_GROUP)
        part = peer_act_partial_sc(u_p, idx2, h2.reshape(n_tok, d))
        coef_b = peer_coef_tc(part, gate2)
        y = peer_combine_sc(v_p, idx2, coef_b)
        out = final_tc(x1, y.reshape(bg, l, d), mod_g, final_g)
        return out, _exact_zero(gate[0, 0]), _exact_zero(coef_b[0, 0]), _exact_zero(y[0, 0])

    bg = b // BATCH_GROUPS
    zero = jnp.zeros((), F32)
    outs, z_sel, z_coef, z_comb = [], zero, zero, [zero, zero]
    for i in range(0, b, bg):
        o, z_sel, z_coef, zc = group(x[i:i + bg], ctx[i:i + bg], mod_c[i:i + bg], mod_l[i:i + bg],
                                     z_sel, z_coef, z_comb[-2])
        z_comb.append(zc)
        outs.append(o)
    return jnp.concatenate(outs, axis=0)


def kernel(x, c, ctx, c_ctx, w_ada, b_ada, norm1_g, norm2_g, w_in, conv_w, dn_a_log,
           dn_dt_bias, dn_norm_g, gla_wa2, gla_ba, gla_norm_g, w_out, peer_wq, peer_keys,
           peer_u, peer_v, final_g):
    assert w_ada.shape[0] == 1, "single-layer block: the context stream is only consumed, never updated"
    return forward(x, c, ctx, c_ctx, w_ada[0], b_ada[0], norm1_g[0], norm2_g[0], w_in[0], conv_w[0],
                   dn_a_log[0], dn_dt_bias[0], dn_norm_g[0], gla_wa2[0], gla_ba[0], gla_norm_g[0],
                   w_out[0], peer_wq[0], peer_keys[0], peer_u[0], peer_v[0], final_g)
```

```python
import functools

import jax
import jax.numpy as jnp
from jax import lax
from jax.experimental import pallas as pl
from jax.experimental.pallas import tpu as pltpu
from jax.experimental.pallas import tpu_sc as plsc

GRID_W = 64
DN_HEADS = 4
DN_DK = 128
DN_DV = 128
CONV_W = 5
GLA_HEADS = 4
GLA_DK = 64
GLA_DV = 128
GLA_LR = 16
GLA_TAU = 16.0
CHUNK = 64
PEER_HEADS = 8
PEER_NKEYS = 128
PEER_DQ = 256
PEER_TOPK = 16
EPS = 1e-6
DN_QKV = 2 * DN_HEADS * DN_DK + DN_HEADS * DN_DV
DN_COLS = DN_QKV + DN_HEADS * DN_DV + 4 * DN_HEADS

SUBLANES = 8
LANES = 128
SC_LANES = 16
VMEM_LIMIT_BYTES = 48 * 1024 * 1024

TOK_TILE = 256
SELECT_TILE = 256
SCAN_BATCH = 2
GLA_SUB = 16
PICK_GROUP = 32
TOK_STEP = 2
ACT_UNROLL = 8
BATCH_GROUPS = 4
PICKS = PEER_HEADS * PEER_TOPK
GROUPS_PER_TOK = PICKS // PICK_GROUP

F32 = jnp.float32
BF16 = jnp.bfloat16


def _cparams(*semantics):
    return pltpu.CompilerParams(dimension_semantics=semantics, vmem_limit_bytes=VMEM_LIMIT_BYTES)


def _dot(a, b):
    return jnp.dot(a.astype(BF16), b.astype(BF16), preferred_element_type=F32)


def _dot_nt(a, b):
    return lax.dot_general(a.astype(BF16), b.astype(BF16), (((1,), (1,)), ((), ())),
                           preferred_element_type=F32)


def _dot_tn(a, b):
    return lax.dot_general(a.astype(BF16), b.astype(BF16), (((0,), (0,)), ((), ())),
                           preferred_element_type=F32)


def _split(x):
    hi = x.astype(BF16)
    return hi, (x - hi.astype(F32)).astype(BF16)


def _mask_dot(mask_bf16, x):
    hi, lo = _split(x)
    return (jnp.dot(mask_bf16, hi, preferred_element_type=F32)
            + jnp.dot(mask_bf16, lo, preferred_element_type=F32))


def _softplus(x):
    return jnp.maximum(x, 0.0) + jnp.log(1.0 + jnp.exp(-jnp.abs(x)))


def _tri_masks(rev):
    r = lax.broadcasted_iota(jnp.int32, (CHUNK, CHUNK), 0)
    c = lax.broadcasted_iota(jnp.int32, (CHUNK, CHUNK), 1)
    d = (c - r) if rev else (r - c)
    return d >= 0, d > 0


def _mod_kernel(c_ref, w_ref, b_ref, o_ref):
    c = c_ref[...]
    s = c * jax.nn.sigmoid(c)
    o_ref[...] = jnp.dot(s, w_ref[...], preferred_element_type=F32,
                         precision=lax.Precision.HIGHEST) + b_ref[...]


def adaln_mod(c_all, w_ada, b_ada):
    r, d = c_all.shape
    n = w_ada.shape[1]
    tn = 512
    return pl.pallas_call(
        _mod_kernel, grid=(n // tn,),
        in_specs=[pl.BlockSpec((r, d), lambda j: (0, 0)),
                  pl.BlockSpec((d, tn), lambda j: (0, j)),
                  pl.BlockSpec((1, tn), lambda j: (0, j))],
        out_specs=pl.BlockSpec((r, tn), lambda j: (0, j)),
        out_shape=jax.ShapeDtypeStruct((r, n), F32),
        compiler_params=_cparams("arbitrary"), name="adaln_mod",
    )(c_all, w_ada, b_ada.reshape(1, n))


def _inproj_kernel(x_ref, g_ref, mod_ref, *refs):
    n_out = len(refs) // 2
    x = x_ref[0]
    y = x * lax.rsqrt(jnp.mean(x * x, axis=-1, keepdims=True) + EPS) * g_ref[...]
    h = (y * (1.0 + mod_ref[0, 1:2, :]) + mod_ref[0, 0:1, :]).astype(BF16)
    for w_ref, o_ref in zip(refs[:n_out], refs[n_out:]):
        o_ref[0] = jnp.dot(h, w_ref[...], preferred_element_type=F32)


def in_projection(x, norm_g, mod, weights):
    b, l, d = x.shape
    w_specs = [pl.BlockSpec(w.shape, lambda bi, i: (0, 0)) for w in weights]
    o_specs = [pl.BlockSpec((1, TOK_TILE, w.shape[1]), lambda bi, i: (bi, i, 0)) for w in weights]
    return pl.pallas_call(
        _inproj_kernel, grid=(b, l // TOK_TILE),
        in_specs=[pl.BlockSpec((1, TOK_TILE, d), lambda bi, i: (bi, i, 0)),
                  pl.BlockSpec((1, d), lambda bi, i: (0, 0)),
                  pl.BlockSpec((1, 2, d), lambda bi, i: (bi, 0, 0))] + w_specs,
        out_specs=o_specs,
        out_shape=[jax.ShapeDtypeStruct((b, l, w.shape[1]), F32) for w in weights],
        compiler_params=_cparams("parallel", "arbitrary"), name="in_projection",
    )(x, norm_g.reshape(1, d), mod, *weights)


def _dn_feature_kernel(x_ref, w_ref, o_ref):
    x = x_ref[0]
    n = x.shape[0]
    t = lax.broadcasted_iota(jnp.int32, (n, 1), 0)
    pad = CONV_W // 2
    acc = w_ref[0, pad:pad + 1, :] * x
    for j in range(CONV_W):
        s = j - pad
        if s == 0:
            continue
        xs = pltpu.roll(x, (-s) % n, axis=0)
        bad = (t < -s) if s < 0 else (t >= n - s)
        acc = acc + w_ref[0, j:j + 1, :] * jnp.where(bad, 0.0, xs)
    y = acc * jax.nn.sigmoid(acc)
    kind = pl.program_id(1) // DN_HEADS
    inv = lax.rsqrt(jnp.sum(y * y, axis=-1, keepdims=True) + EPS)
    scale = jnp.where(kind == 0, inv * DN_DK ** -0.5, jnp.where(kind == 1, inv, 1.0))
    o_ref[0] = y * scale


def dn_features_tc(qkv, conv_w):
    b, l, n = qkv.shape
    nblk = n // LANES
    w = jnp.zeros((nblk, SUBLANES, LANES), F32).at[:, :CONV_W].set(
        conv_w.reshape(CONV_W, nblk, LANES).transpose(1, 0, 2))
    return pl.pallas_call(
        _dn_feature_kernel, grid=(b, nblk),
        in_specs=[pl.BlockSpec((1, l, LANES), lambda bi, j: (bi, 0, j)),
                  pl.BlockSpec((1, SUBLANES, LANES), lambda bi, j: (j, 0, 0))],
        out_specs=pl.BlockSpec((1, l, LANES), lambda bi, j: (bi, 0, j)),
        out_shape=jax.ShapeDtypeStruct((b, l, n), F32),
        compiler_params=_cparams("parallel", "arbitrary"), name="dn_features",
    )(qkv, w)


def _scan_chunks(rev, n_ctx, n_lat):
    if rev:
        ctx = lambda j: jnp.maximum(n_ctx - 1 - j, 0)
        lat = lambda j: jnp.where(j < n_ctx, n_lat - 1, n_lat - 1 - (j - n_ctx))
    else:
        ctx = lambda j: jnp.minimum(j, n_ctx - 1)
        lat = lambda j: jnp.maximum(j - n_ctx, 0)
    return ctx, lat


def _dn_scan_kernel(rev, dirn, n_ctx, alog_ref, dtb_ref, fc_ref, fl_ref, bac_ref, bal_ref, o_ref, s_ref):
    step = pl.program_id(1)

    @pl.when(step == 0)
    def _():
        s_ref[...] = jnp.zeros_like(s_ref)

    in_ctx = step < n_ctx
    f = jnp.where(in_ctx, fc_ref[0], fl_ref[0])
    ba = jnp.where(in_ctx, bac_ref[0], bal_ref[0])
    nh, hd, n = DN_HEADS, DN_HEADS * DN_DK, DN_HEADS * CHUNK
    stack = lambda base, w: jnp.concatenate([f[:, base + h * w:base + (h + 1) * w] for h in range(nh)], axis=0)
    q_s, k_s, v_s = stack(0, DN_DK), stack(hd, DN_DK), stack(2 * hd, DN_DV)
    incl, _ = _tri_masks(rev)
    beta_all = jax.nn.sigmoid(ba)
    g_all = -jnp.exp(alog_ref[...]) * _softplus(ba + dtb_ref[...])
    gc_all = _mask_dot(incl.astype(BF16), g_all)
    gc_t = jnp.concatenate([gc_all, gc_all], axis=0).T
    g_tot = jnp.sum(g_all, axis=0, keepdims=True)
    cb = [dirn * nh + h for h in range(nh)]
    cg = [2 * nh + c for c in cb]
    col = lambda a, cs: jnp.concatenate([a[:, c:c + 1] for c in cs], axis=0)
    beta_c, gc_c = col(beta_all, cb), col(gc_all, cg)
    gtot_c = jnp.concatenate([jnp.broadcast_to(g_tot[:, c:c + 1], (CHUNK, 1)) for c in cg], axis=0)
    gc_r = jnp.concatenate([gc_t[c:c + 1, :CHUNK] for c in cg], axis=1)
    r = lax.broadcasted_iota(jnp.int32, (n, n), 0)
    c = lax.broadcasted_iota(jnp.int32, (n, n), 1)
    same = (r // CHUNK) == (c // CHUNK)
    d = (c - r) if rev else (r - c)
    incl_bd, strict_bd = same & (d >= 0), same & (d > 0)
    eye = (r == c).astype(F32)
    decay = jnp.where(incl_bd, jnp.exp(jnp.where(incl_bd, gc_c - gc_r, 0.0)), 0.0)
    kb_s = k_s * beta_c
    lower = jnp.where(strict_bd, _dot_nt(kb_s, k_s) * decay, 0.0)
    eg_c = jnp.exp(gc_c)
    inv = eye - lower
    pw = lower
    for _ in range(5):
        pw = _dot(pw, pw)
        inv = inv + _dot(inv, pw)
    sol = _dot(inv, jnp.concatenate([v_s * beta_c, kb_s * eg_c], axis=-1))
    u_s, w_s = sol[:, :DN_DV], sol[:, DN_DV:]
    k_dec = k_s * jnp.exp(gtot_c - gc_c)
    rb = lax.broadcasted_iota(jnp.int32, (n, DN_DK), 0) // CHUNK
    expand = lambda x: jnp.concatenate([jnp.where(rb == h, x, 0.0) for h in range(nh)], axis=1)
    s = s_ref[...]
    v_new = u_s - _dot(expand(w_s), s)
    a_qk = _dot_nt(q_s, k_s) * decay
    o_s = _dot(expand(q_s * eg_c), s) + _dot(a_qk, v_new)
    gl_rows = jnp.concatenate([jnp.broadcast_to(jnp.exp(g_tot[:, cc:cc + 1]), (DN_DK, 1)) for cc in cg], axis=0)
    s_ref[...] = s * gl_rows + _dot_tn(expand(k_dec), v_new)
    o_ref[0] = jnp.concatenate([o_s[h * CHUNK:(h + 1) * CHUNK] for h in range(nh)], axis=1)


def dn_scan_tc(feat_c, feat_l, ba_c, ba_l, a_log, dt_bias, rev):
    b, l, nf = feat_l.shape
    n_ctx, n_lat = feat_c.shape[1] // CHUNK, l // CHUNK
    dirn = 1 if rev else 0
    cc, lc = _scan_chunks(rev, n_ctx, n_lat)
    lanes = lambda p: jnp.zeros((1, LANES), F32).at[0, 2 * DN_HEADS:4 * DN_HEADS].set(p.reshape(-1))
    vec = pl.BlockSpec((1, LANES), lambda bi, j: (0, 0))
    return pl.pallas_call(
        functools.partial(_dn_scan_kernel, rev, dirn, n_ctx), grid=(b, n_ctx + n_lat),
        in_specs=[vec, vec,
                  pl.BlockSpec((1, CHUNK, nf), lambda bi, j: (bi, cc(j), 0)),
                  pl.BlockSpec((1, CHUNK, nf), lambda bi, j: (bi, lc(j), 0)),
                  pl.BlockSpec((1, CHUNK, LANES), lambda bi, j: (bi, cc(j), 0)),
                  pl.BlockSpec((1, CHUNK, LANES), lambda bi, j: (bi, lc(j), 0))],
        out_specs=pl.BlockSpec((1, CHUNK, DN_HEADS * DN_DV), lambda bi, j: (bi, lc(j), 0)),
        out_shape=jax.ShapeDtypeStruct((b, l, DN_HEADS * DN_DV), F32),
        scratch_shapes=[pltpu.VMEM((DN_HEADS * DN_DK, DN_DV), F32)],
        compiler_params=_cparams("parallel", "arbitrary"), name="dn_scan_bwd" if rev else "dn_scan_fwd",
    )(lanes(a_log), lanes(dt_bias), feat_c, feat_l, ba_c, ba_l)


def _from_grid_cols(blk, n):
    cols = blk.shape[1] // n
    return jnp.concatenate([blk[:, i * n:(i + 1) * n] for i in range(cols)], axis=0)


def _gla_scan_kernel(rev, dirn, n_ctx, qkc_ref, vc_ref, lrc_ref, qkl_ref, vl_ref, lrl_ref,
                     wla_ref, bla_ref, o_ref, s_ref):
    step = pl.program_id(1)

    @pl.when(step == 0)
    def _():
        s_ref[...] = jnp.zeros_like(s_ref)

    in_ctx = step < n_ctx
    hk, hv = GLA_HEADS * GLA_DK, GLA_HEADS * GLA_DV
    qk = jnp.where(in_ctx, qkc_ref[0], _from_grid_cols(qkl_ref[0], 2 * hk))
    vv = jnp.where(in_ctx, vc_ref[0], _from_grid_cols(vl_ref[0], hv))
    lr = jnp.where(in_ctx, lrc_ref[0], _from_grid_cols(lrl_ref[0], LANES))
    incl, _ = _tri_masks(rev)
    incl_b = incl.astype(BF16)
    pre = _dot(lr, wla_ref[0]) + bla_ref[0]
    la_all = -_softplus(-pre) * (1.0 / GLA_TAU)
    bc_all = _mask_dot(incl_b, la_all)
    b_tot_all = jnp.sum(la_all, axis=0, keepdims=True)
    outs = []
    for h in range(GLA_HEADS):
        q = qk[:, h * GLA_DK:(h + 1) * GLA_DK] * GLA_DK ** -0.5
        k = qk[:, hk + h * GLA_DK:hk + (h + 1) * GLA_DK]
        v = vv[:, h * GLA_DV:(h + 1) * GLA_DV]
        bc = bc_all[:, h * GLA_DK:(h + 1) * GLA_DK]
        b_tot = b_tot_all[:, h * GLA_DK:(h + 1) * GLA_DK]
        st = s_ref[h]
        o = _dot_nt(q * jnp.exp(bc), st)
        parts = []
        for i in range(CHUNK // GLA_SUB):
            lo_r, hi_r = i * GLA_SUB, (i + 1) * GLA_SUB
            if rev:
                ref = bc[hi_r - 1:hi_r]
                c0, c1 = lo_r, CHUNK
            else:
                ref = bc[lo_r:lo_r + 1]
                c0, c1 = 0, hi_r
            qi = q[lo_r:hi_r] * jnp.exp(bc[lo_r:hi_r] - ref)
            ki = k[c0:c1] * jnp.exp(ref - bc[c0:c1])
            att = _dot_nt(qi, ki)
            rg = lax.broadcasted_iota(jnp.int32, (GLA_SUB, c1 - c0), 0) + lo_r
            cg = lax.broadcasted_iota(jnp.int32, (GLA_SUB, c1 - c0), 1) + c0
            keep = (cg >= rg) if rev else (cg <= rg)
            parts.append(_dot(jnp.where(keep, att, 0.0), v[c0:c1]))
        outs.append(o + jnp.concatenate(parts, axis=0))
        k_dec = k * jnp.exp(b_tot - bc)
        s_ref[h] = st * jnp.exp(b_tot) + _dot_tn(v, k_dec)
    o = jnp.concatenate(outs, axis=-1)
    rows = o_ref.shape[1]
    o_ref[0] = jnp.concatenate([o[i * rows:(i + 1) * rows] for i in range(CHUNK // rows)], axis=-1)


def gla_scan_tc(qk_c, v_c, lr_c, qk_l, v_l, lr_l, w_la, b_la, rev):
    b, l, _ = qk_l.shape
    rows = l // GRID_W
    cols = CHUNK // rows
    n_ctx, n_lat = qk_c.shape[1] // CHUNK, l // CHUNK
    dirn = 1 if rev else 0
    cc, lc = _scan_chunks(rev, n_ctx, n_lat)
    hv = GLA_HEADS * GLA_DV
    ctx_blk = lambda a: pl.BlockSpec((1, CHUNK, a.shape[2]), lambda bi, j: (bi, cc(j), 0))
    lat_blk = lambda n: pl.BlockSpec((1, rows, cols * n), lambda bi, j: (bi, 0, lc(j)))
    grid_view = lambda a: a.reshape(b, rows, GRID_W * a.shape[2])
    out = pl.pallas_call(
        functools.partial(_gla_scan_kernel, rev, dirn, n_ctx), grid=(b, n_ctx + n_lat),
        in_specs=[ctx_blk(qk_c), ctx_blk(v_c), ctx_blk(lr_c),
                  lat_blk(qk_l.shape[2]), lat_blk(v_l.shape[2]), lat_blk(lr_l.shape[2]),
                  pl.BlockSpec((1,) + w_la.shape[1:], lambda bi, j: (dirn, 0, 0)),
                  pl.BlockSpec((1,) + b_la.shape[1:], lambda bi, j: (dirn, 0, 0))],
        out_specs=lat_blk(hv),
        out_shape=jax.ShapeDtypeStruct((b, rows, GRID_W * hv), F32),
        scratch_shapes=[pltpu.VMEM((GLA_HEADS, GLA_DV, GLA_DK), F32)],
        compiler_params=_cparams("parallel", "arbitrary"), name="gla_scan_bwd" if rev else "gla_scan_fwd",
    )(qk_c, v_c, lr_c, grid_view(qk_l), grid_view(v_l), grid_view(lr_l), w_la, b_la)
    return out.reshape(b, l, hv)


def _head_norm_gate(o, gate, g, n_heads, dv):
    parts = []
    for h in range(n_heads):
        oh = o[:, h * dv:(h + 1) * dv]
        gh = gate[:, h * dv:(h + 1) * dv]
        yh = oh * lax.rsqrt(jnp.mean(oh * oh, axis=-1, keepdims=True) + EPS) * g
        parts.append(yh * (gh * jax.nn.sigmoid(gh)))
    return parts


def _mix_out_kernel(x_ref, dnf_ref, dnb_ref, z_ref, glf_ref, glb_ref, r_ref, mod_ref, dng_ref,
                    glg_ref, n2g_ref, wout_ref, wq_ref, x1_ref, h2_ref, q_ref):
    parts = (_head_norm_gate(dnf_ref[0] + dnb_ref[0], z_ref[0], dng_ref[...], DN_HEADS, DN_DV)
             + _head_norm_gate(glf_ref[0] + glb_ref[0], r_ref[0], glg_ref[...], GLA_HEADS, GLA_DV))
    y = jnp.dot(jnp.concatenate(parts, axis=-1).astype(BF16), wout_ref[...], preferred_element_type=F32)
    x1 = x_ref[0] + mod_ref[0, 2:3, :] * y
    x1_ref[0] = x1
    n = x1 * lax.rsqrt(jnp.mean(x1 * x1, axis=-1, keepdims=True) + EPS) * n2g_ref[...]
    h2 = n * (1.0 + mod_ref[0, 4:5, :]) + mod_ref[0, 3:4, :]
    h2_ref[0] = h2
    q_ref[0] = jnp.dot(h2.astype(BF16), wq_ref[...], preferred_element_type=F32)


def mix_out_tc(x, dn_f, dn_b, z, gl_f, gl_b, r, mod_l, dn_g, gla_g, n2_g, w_out, w_q):
    b, l, d = x.shape
    tok = lambda n: pl.BlockSpec((1, TOK_TILE, n), lambda bi, i: (bi, i, 0))
    full = lambda a: pl.BlockSpec(a.shape, lambda bi, i: (0,) * a.ndim)
    dn_g, gla_g, n2_g = dn_g.reshape(1, -1), gla_g.reshape(1, -1), n2_g.reshape(1, -1)
    nq = w_q.shape[1]
    return pl.pallas_call(
        _mix_out_kernel, grid=(b, l // TOK_TILE),
        in_specs=[tok(d), tok(dn_f.shape[2]), tok(dn_b.shape[2]), tok(z.shape[2]),
                  tok(gl_f.shape[2]), tok(gl_b.shape[2]), tok(r.shape[2]),
                  pl.BlockSpec((1,) + mod_l.shape[1:], lambda bi, i: (bi, 0, 0)),
                  full(dn_g), full(gla_g), full(n2_g), full(w_out), full(w_q)],
        out_specs=[tok(d), tok(d), tok(nq)],
        out_shape=[jax.ShapeDtypeStruct((b, l, d), F32), jax.ShapeDtypeStruct((b, l, d), F32),
                   jax.ShapeDtypeStruct((b, l, nq), F32)],
        compiler_params=_cparams("parallel", "arbitrary"), name="mix_out",
    )(x, dn_f, dn_b, z, gl_f, gl_b, r, mod_l, dn_g, gla_g, n2_g, w_out, w_q)


def _top_rows(s, k, payload=None):
    n = s.shape[0]
    row = lax.broadcasted_iota(jnp.int32, s.shape, 0).astype(F32)
    vals, picked = [], []
    for _ in range(k):
        m = jnp.max(s, axis=0, keepdims=True)
        first = jnp.min(jnp.where(s == m, row, float(n)), axis=0, keepdims=True)
        sel = row == first
        vals.append(m)
        if payload is None:
            picked.append(first)
        else:
            picked.append(jnp.max(jnp.where(sel, payload, -1.0), axis=0, keepdims=True))
        s = jnp.where(sel, -jnp.inf, s)
    return jnp.concatenate(vals, axis=0), jnp.concatenate(picked, axis=0)


def _candidate_rows(s0, i0, s1, i1):
    k = s0.shape[0]
    wide = SUBLANES
    blocks_s = [s0[0:1] + s1]
    blocks_i = [i0[0:1] * float(PEER_NKEYS) + i1]
    col = lax.broadcasted_iota(jnp.int32, (wide, s0.shape[1]), 0)
    for i in range(1, wide):
        keep = col < (k // (i + 1))
        blocks_s.append(jnp.where(keep, s0[i:i + 1] + s1[0:wide], -jnp.inf))
        blocks_i.append(i0[i:i + 1] * float(PEER_NKEYS) + i1[0:wide])
    blocks_s.append(s0[wide:k] + s1[0:1])
    blocks_i.append(i0[wide:k] * float(PEER_NKEYS) + i1[0:1])
    return jnp.concatenate(blocks_s, axis=0), jnp.concatenate(blocks_i, axis=0)


def _select_kernel(q_ref, k_ref, idx_ref, gate_ref, idx_s, gate_s):
    half = PEER_DQ // 2

    def head(h, carry):
        tops = []
        for p in range(2):
            qp = q_ref[:, pl.ds(pl.multiple_of(h * PEER_DQ + p * half, half), half)]
            s = lax.dot_general(k_ref[h, p], qp, (((1,), (1,)), ((), ())),
                                preferred_element_type=F32,
                                precision=lax.Precision.HIGHEST)
            tops.append(_top_rows(s, PEER_TOPK))
        (s0, i0), (s1, i1) = tops
        cand_s, cand_i = _candidate_rows(s0, i0, s1, i1)
        best_s, idx = _top_rows(cand_s, PEER_TOPK, payload=cand_i)
        e = jnp.exp(best_s - best_s[0:1])
        r0 = pl.multiple_of(h * PEER_TOPK, PEER_TOPK)
        idx_s[pl.ds(r0, PEER_TOPK), :] = idx
        gate_s[pl.ds(r0, PEER_TOPK), :] = e / jnp.sum(e, axis=0, keepdims=True)
        return carry

    lax.fori_loop(0, PEER_HEADS, head, 0)
    idx_ref[...] = idx_s[...].T.astype(jnp.int32)
    gate_ref[...] = gate_s[...].T


def peer_select_tc(q, keys):
    n_tok = q.shape[0]
    out_spec = pl.BlockSpec((SELECT_TILE, PICKS), lambda i: (i, 0))
    return pl.pallas_call(
        _select_kernel,
        grid=(n_tok // SELECT_TILE,),
        in_specs=[pl.BlockSpec((SELECT_TILE, q.shape[1]), lambda i: (i, 0)),
                  pl.BlockSpec(keys.shape, lambda i: (0, 0, 0, 0))],
        out_specs=[out_spec, out_spec],
        out_shape=[jax.ShapeDtypeStruct((n_tok, PICKS), jnp.int32),
                   jax.ShapeDtypeStruct((n_tok, PICKS), F32)],
        scratch_shapes=[pltpu.VMEM((PICKS, SELECT_TILE), F32), pltpu.VMEM((PICKS, SELECT_TILE), F32)],
        compiler_params=_cparams("parallel"), name="peer_select",
    )(q, keys)


def _sc_mesh():
    return plsc.VectorSubcoreMesh(core_axis_name="c", subcore_axis_name="s")


def _sc_pipeline(body, n_steps, in_specs, out_specs, operands):
    pltpu.emit_pipeline(
        body, grid=(n_steps,), in_specs=in_specs, out_specs=out_specs,
        core_axis_name=("c", "s"), dimension_semantics=(pltpu.PARALLEL,),
        trace_scopes=False,
    )(*operands)


def pack_table(t):
    half = t.shape[1] // 2
    bits = lax.bitcast_convert_type(t, jnp.uint32)
    lo_bits, hi_bits = bits[:, :half], bits[:, half:]
    low = (lo_bits + jnp.uint32(0x7FFF) + ((lo_bits >> 16) & jnp.uint32(1))) >> 16
    mag = hi_bits & jnp.uint32(0x7FFFFFFF)
    top = jnp.where(mag >= jnp.uint32(0x10000),
                    (mag - low + jnp.uint32(0x8000)) & jnp.uint32(0xFFFF0000), jnp.uint32(0))
    word = (hi_bits & jnp.uint32(0x80000000)) | top | low
    return lax.bitcast_convert_type(word, jnp.int32)


def _unpack(w):
    return plsc.bitcast(w << 16, F32), plsc.bitcast(w, F32)


def peer_act_partial_sc(u_p, idx2, h):
    n_groups, (n_tok, d) = idx2.shape[0], h.shape
    half = d // 2
    nsub = TOK_STEP * GROUPS_PER_TOK
    n_chunks = half // SC_LANES

    @functools.partial(
        pl.kernel, mesh=_sc_mesh(), compiler_params=pltpu.CompilerParams(needs_layout_passes=False),
        out_type=jax.ShapeDtypeStruct((n_groups, PICK_GROUP * SC_LANES), F32),
        scratch_types=[pltpu.VMEM((2, PICK_GROUP, half), jnp.int32),
                       pltpu.SemaphoreType.DMA((2,))],
    )
    def k(u_hbm, i_hbm, h_hbm, o_hbm, rows, sems):
        def body(i_v, h_v, o_v):
            def fetch(j, slot):
                return pltpu.make_async_copy(u_hbm.at[i_v.at[j]], rows.at[slot], sems.at[slot])

            fetch(0, 0).start()

            def sub(j, carry):
                slot = j % 2

                @pl.when(j + 1 < nsub)
                def _():
                    fetch(j + 1, 1 - slot).start()

                fetch(j, slot).wait()
                t = j // GROUPS_PER_TOK

                def picks(g, carry2):
                    kb = g * ACT_UNROLL
                    accs = [None] * ACT_UNROLL
                    for c in range(n_chunks):
                        h_lo = h_v[t, pl.ds(c * SC_LANES, SC_LANES)]
                        h_hi = h_v[t, pl.ds(half + c * SC_LANES, SC_LANES)]
                        for i in range(ACT_UNROLL):
                            lo, hi = _unpack(rows[slot, kb + i, pl.ds(c * SC_LANES, SC_LANES)])
                            p = lo * h_lo + hi * h_hi
                            accs[i] = p if accs[i] is None else accs[i] + p
                    for i in range(ACT_UNROLL):
                        o_v[j, pl.ds((kb + i) * SC_LANES, SC_LANES)] = accs[i]
                    return carry2

                lax.fori_loop(0, PICK_GROUP // ACT_UNROLL, picks, 0)
                return carry

            lax.fori_loop(0, nsub, sub, 0)

        _sc_pipeline(
            body, n_tok // TOK_STEP,
            [pl.BlockSpec((nsub, PICK_GROUP), lambda i: (i, 0)),
             pl.BlockSpec((TOK_STEP, d), lambda i: (i, 0))],
            [pl.BlockSpec((nsub, PICK_GROUP * SC_LANES), lambda i: (i, 0))],
            (i_hbm, h_hbm, o_hbm))

    return k(u_p, idx2, h)


def peer_combine_sc(v_p, idx2, coef_b):
    n_groups = idx2.shape[0]
    half = v_p.shape[1]
    n_tok = n_groups // GROUPS_PER_TOK
    nsub = TOK_STEP * GROUPS_PER_TOK
    n_chunks = half // SC_LANES

    @functools.partial(
        pl.kernel, mesh=_sc_mesh(), compiler_params=pltpu.CompilerParams(needs_layout_passes=False),
        out_type=jax.ShapeDtypeStruct((n_tok, 2 * half), F32),
        scratch_types=[pltpu.VMEM((2, PICK_GROUP, half), jnp.int32),
                       pltpu.SemaphoreType.DMA((2,))],
    )
    def k(v_hbm, i_hbm, c_hbm, o_hbm, rows, sems):
        def body(i_v, c_v, o_v):
            def fetch(j, slot):
                return pltpu.make_async_copy(v_hbm.at[i_v.at[j]], rows.at[slot], sems.at[slot])

            fetch(0, 0).start()
            for j in range(nsub):
                slot = j % 2
                if j + 1 < nsub:
                    fetch(j + 1, 1 - slot).start()
                fetch(j, slot).wait()
                t = j // GROUPS_PER_TOK
                first = j % GROUPS_PER_TOK == 0
                cks = [c_v[j, pl.ds(kk * SC_LANES, SC_LANES)] for kk in range(PICK_GROUP)]

                def chunk(c, carry, slot=slot, t=t, first=first, cks=cks):
                    l = c * SC_LANES
                    los, his = [None] * 2, [None] * 2
                    for kk in range(PICK_GROUP):
                        lo, hi = _unpack(rows[slot, kk, pl.ds(l, SC_LANES)])
                        a, b_ = cks[kk] * lo, cks[kk] * hi
                        los[kk % 2] = a if los[kk % 2] is None else los[kk % 2] + a
                        his[kk % 2] = b_ if his[kk % 2] is None else his[kk % 2] + b_
                    tot_lo, tot_hi = los[0] + los[1], his[0] + his[1]
                    if not first:
                        tot_lo = tot_lo + o_v[t, pl.ds(l, SC_LANES)]
                        tot_hi = tot_hi + o_v[t, pl.ds(half + l, SC_LANES)]
                    o_v[t, pl.ds(l, SC_LANES)] = tot_lo
                    o_v[t, pl.ds(half + l, SC_LANES)] = tot_hi
                    return carry

                lax.fori_loop(0, n_chunks, chunk, 0)

        _sc_pipeline(
            body, n_tok // TOK_STEP,
            [pl.BlockSpec((nsub, PICK_GROUP), lambda i: (i, 0)),
             pl.BlockSpec((nsub, PICK_GROUP * SC_LANES), lambda i: (i, 0))],
            [pl.BlockSpec((TOK_STEP, 2 * half), lambda i: (i, 0))],
            (i_hbm, c_hbm, o_hbm))

    return k(v_p, idx2, coef_b)


def _segment_matrix():
    r = lax.broadcasted_iota(jnp.int32, (PICK_GROUP * SC_LANES, PICK_GROUP), 0) // SC_LANES
    c = lax.broadcasted_iota(jnp.int32, (PICK_GROUP * SC_LANES, PICK_GROUP), 1)
    return (r == c).astype(F32)


def _coef_kernel(part_ref, gate_ref, o_ref):
    seg = _segment_matrix()
    act = jnp.dot(part_ref[...], seg, preferred_element_type=F32, precision=lax.Precision.HIGHEST)
    coef = gate_ref[...] * (0.5 * act * (1.0 + lax.erf(act * (2.0 ** -0.5))))
    o_ref[...] = lax.dot_general(coef, seg, (((1,), (1,)), ((), ())), preferred_element_type=F32,
                                 precision=lax.Precision.HIGHEST)


def peer_coef_tc(part, gate2):
    n_groups, width = part.shape
    tile = 1024
    return pl.pallas_call(
        _coef_kernel,
        grid=(n_groups // tile,),
        in_specs=[pl.BlockSpec((tile, width), lambda i: (i, 0)),
                  pl.BlockSpec((tile, PICK_GROUP), lambda i: (i, 0))],
        out_specs=pl.BlockSpec((tile, width), lambda i: (i, 0)),
        out_shape=jax.ShapeDtypeStruct((n_groups, width), F32),
        compiler_params=_cparams("parallel"), name="peer_coef",
    )(part, gate2)


def _final_kernel(x_ref, y_ref, mod_ref, g_ref, o_ref):
    x = x_ref[0] + mod_ref[0, 5:6, :] * y_ref[0]
    o_ref[0] = x * lax.rsqrt(jnp.mean(x * x, axis=-1, keepdims=True) + EPS) * g_ref[...]


def final_tc(x1, y, mod_l, final_g):
    b, l, d = x1.shape
    tok = pl.BlockSpec((1, TOK_TILE, d), lambda bi, i: (bi, i, 0))
    return pl.pallas_call(
        _final_kernel, grid=(b, l // TOK_TILE),
        in_specs=[tok, tok, pl.BlockSpec((1,) + mod_l.shape[1:], lambda bi, i: (bi, 0, 0)),
                  pl.BlockSpec((1, d), lambda bi, i: (0, 0))],
        out_specs=tok, out_shape=jax.ShapeDtypeStruct((b, l, d), F32),
        compiler_params=_cparams("parallel", "arbitrary"), name="final_norm",
    )(x1, y, mod_l, final_g.reshape(1, d))


def _pad_cols(w, n):
    return jnp.pad(w, ((0, 0), (0, n - w.shape[1])))


def _exact_zero(v):
    return jnp.minimum(jnp.abs(v), 0.0)


def forward(x, c, ctx, c_ctx, w_ada, b_ada, norm1_g, norm2_g, w_in, conv_w, dn_a_log,
            dn_dt_bias, dn_norm_g, gla_wa2, gla_ba, gla_norm_g, w_out, peer_wq, peer_keys,
            peer_u, peer_v, final_g):
    b, l, d = x.shape
    c_all = jnp.concatenate([c, c_ctx[None]], axis=0)
    c_all = jnp.pad(c_all, ((0, (-c_all.shape[0]) % SUBLANES), (0, 0)))
    mod = adaln_mod(c_all, w_ada, b_ada)
    mod_l = mod[:b].reshape(b, 6, d)
    mod_c = jnp.broadcast_to(mod[b].reshape(1, 6, d), (b, 6, d))
    o = DN_QKV
    hv = DN_HEADS * DN_DV
    w_dn_qkv, w_dn_z = w_in[:, :o], w_in[:, o:o + hv]
    w_dn_ba = _pad_cols(w_in[:, o + hv:DN_COLS], LANES)
    g0 = DN_COLS
    gqk, gv = 2 * GLA_HEADS * GLA_DK, GLA_HEADS * GLA_DV
    w_gl_qk, w_gl_v = w_in[:, g0:g0 + gqk], w_in[:, g0 + gqk:g0 + gqk + gv]
    w_gl_r = w_in[:, g0 + gqk + gv:g0 + gqk + 2 * gv]
    w_gl_lr = _pad_cols(w_in[:, g0 + gqk + 2 * gv:], LANES)
    w_lat = [w.astype(BF16) for w in (w_dn_qkv, w_dn_ba, w_gl_qk, w_gl_v, w_gl_lr, w_dn_z, w_gl_r)]
    w_ctx = w_lat[:5]
    w_la = jnp.zeros((2, LANES, GLA_HEADS * GLA_DK), F32)
    for dd in range(2):
        w_la = w_la.at[dd, dd * GLA_LR:(dd + 1) * GLA_LR].set(gla_wa2[dd])
    w_la = w_la.astype(BF16)
    b_la = gla_ba.reshape(2, 1, GLA_HEADS * GLA_DK)
    w_out_b, w_q_b = w_out.astype(BF16), peer_wq.astype(BF16)
    u_p, v_p = pack_table(peer_u), pack_table(peer_v)

    bg = b // BATCH_GROUPS
    n_tok = bg * l

    def mixer(xg, ctxg, mod_cg, mod_g, after_select, after_combine):
        mod_cg, mod_g = mod_cg + after_select, mod_g + after_select
        c_qkv, c_ba, c_qk, c_v, c_lr = in_projection(ctxg, norm1_g, mod_cg[:, 0:2], w_ctx)
        l_qkv, l_ba, l_qk, l_v, l_lr, l_z, l_r = in_projection(xg, norm1_g, mod_g[:, 0:2], w_lat)
        feat_c, feat_l = dn_features_tc(c_qkv, conv_w), dn_features_tc(l_qkv, conv_w)
        dn_f = dn_scan_tc(feat_c, feat_l, c_ba, l_ba, dn_a_log, dn_dt_bias, rev=False)
        dn_b = dn_scan_tc(feat_c, feat_l, c_ba, l_ba, dn_a_log, dn_dt_bias, rev=True)
        gl_f = gla_scan_tc(c_qk, c_v, c_lr, l_qk, l_v, l_lr, w_la, b_la, rev=False)
        gl_b = gla_scan_tc(c_qk, c_v, c_lr, l_qk, l_v, l_lr, w_la, b_la, rev=True)
        return mix_out_tc(xg, dn_f, dn_b, l_z, gl_f, gl_b, l_r, mod_g, dn_norm_g + after_combine,
                          gla_norm_g, norm2_g, w_out_b, w_q_b)

    def select_and_act(h2, q, after_coef):
        idx, gate = peer_select_tc(q.reshape(n_tok, -1), peer_keys + after_coef)
        idx2 = idx.reshape(n_tok * GROUPS_PER_TOK, PICK_GROUP)
        gate2 = gate.reshape(n_tok * GROUPS_PER_TOK, PICK_GROUP)
        return idx2, gate2, peer_act_partial_sc(u_p, idx2, h2.reshape(n_tok, d))

    def coef_and_combine(st, after_mixer):
        coef_b = peer_coef_tc(st["part"], st["gate2"] + after_mixer)
        st["y"] = peer_combine_sc(v_p, st["idx2"], coef_b)
        return _exact_zero(coef_b[0, 0])

    zero = jnp.zeros((), F32)
    slices, z_sel, z_coef = [], zero, zero
    for i in range(0, b, bg):
        g = len(slices)
        z_comb = _exact_zero(slices[g - 2]["y"][0, 0]) if g >= 2 else zero
        mod_g = mod_l[i:i + bg]
        x1, h2, q = mixer(x[i:i + bg], ctx[i:i + bg], mod_c[i:i + bg], mod_g, z_sel, z_comb)
        if g >= 1:
            z_coef = coef_and_combine(slices[g - 1], _exact_zero(x1[0, 0, 0]))
        idx2, gate2, part = select_and_act(h2, q, z_coef)
        z_sel = _exact_zero(gate2[0, 0])
        slices.append(dict(x1=x1, mod=mod_g, idx2=idx2, gate2=gate2, part=part))
    coef_and_combine(slices[-1], zero)
    outs = [final_tc(st["x1"], st["y"].reshape(bg, l, d), st["mod"], final_g) for st in slices]
    return jnp.concatenate(outs, axis=0)


def kernel(x, c, ctx, c_ctx, w_ada, b_ada, norm1_g, norm2_g, w_in, conv_w, dn_a_log,
           dn_dt_bias, dn_norm_g, gla_wa2, gla_ba, gla_norm_g, w_out, peer_wq, peer_keys,
           peer_u, peer_v, final_g):
    assert w_ada.shape[0] == 1, "single-layer block: the context stream is only consumed, never updated"
    return forward(x, c, ctx, c_ctx, w_ada[0], b_ada[0], norm1_g[0], norm2_g[0], w_in[0], conv_w[0],
                   dn_a_log[0], dn_dt_bias[0], dn_norm_g[0], gla_wa2[0], gla_ba[0], gla_norm_g[0],
                   w_out[0], peer_wq[0], peer_keys[0], peer_u[0], peer_v[0], final_g)
```

```python
import functools

import jax
import jax.numpy as jnp
from jax import lax
from jax.experimental import pallas as pl
from jax.experimental.pallas import tpu as pltpu
from jax.experimental.pallas import tpu_sc as plsc

GRID_W = 64
DN_HEADS = 4
DN_DK = 128
DN_DV = 128
CONV_W = 5
GLA_HEADS = 4
GLA_DK = 64
GLA_DV = 128
GLA_LR = 16
GLA_TAU = 16.0
CHUNK = 64
PEER_HEADS = 8
PEER_NKEYS = 128
PEER_DQ = 256
PEER_TOPK = 16
EPS = 1e-6
DN_QKV = 2 * DN_HEADS * DN_DK + DN_HEADS * DN_DV
DN_COLS = DN_QKV + DN_HEADS * DN_DV + 4 * DN_HEADS

SUBLANES = 8
LANES = 128
SC_LANES = 16
VMEM_LIMIT_BYTES = 48 * 1024 * 1024

TOK_TILE = 256
SELECT_TILE = 256
SCAN_BATCH = 2
GLA_SUB = 16
PICK_GROUP = 32
TOK_STEP = 2
ACT_UNROLL = 16
BATCH_GROUPS = 8
PICKS = PEER_HEADS * PEER_TOPK
GROUPS_PER_TOK = PICKS // PICK_GROUP

F32 = jnp.float32
BF16 = jnp.bfloat16


def _cparams(*semantics):
    return pltpu.CompilerParams(dimension_semantics=semantics, vmem_limit_bytes=VMEM_LIMIT_BYTES)


def _dot(a, b):
    return jnp.dot(a.astype(BF16), b.astype(BF16), preferred_element_type=F32)


def _dot_nt(a, b):
    return lax.dot_general(a.astype(BF16), b.astype(BF16), (((1,), (1,)), ((), ())),
                           preferred_element_type=F32)


def _dot_tn(a, b):
    return lax.dot_general(a.astype(BF16), b.astype(BF16), (((0,), (0,)), ((), ())),
                           preferred_element_type=F32)


def _split(x):
    hi = x.astype(BF16)
    return hi, (x - hi.astype(F32)).astype(BF16)


def _mask_dot(mask_bf16, x):
    hi, lo = _split(x)
    return (jnp.dot(mask_bf16, hi, preferred_element_type=F32)
            + jnp.dot(mask_bf16, lo, preferred_element_type=F32))


def _softplus(x):
    return jnp.maximum(x, 0.0) + jnp.log(1.0 + jnp.exp(-jnp.abs(x)))


def _tri_masks(rev):
    r = lax.broadcasted_iota(jnp.int32, (CHUNK, CHUNK), 0)
    c = lax.broadcasted_iota(jnp.int32, (CHUNK, CHUNK), 1)
    d = (c - r) if rev else (r - c)
    return d >= 0, d > 0


def _mod_kernel(c_ref, w_ref, b_ref, o_ref):
    c = c_ref[...]
    s = c * jax.nn.sigmoid(c)
    o_ref[...] = jnp.dot(s, w_ref[...], preferred_element_type=F32,
                         precision=lax.Precision.HIGHEST) + b_ref[...]


def adaln_mod(c_all, w_ada, b_ada):
    r, d = c_all.shape
    n = w_ada.shape[1]
    tn = 512
    return pl.pallas_call(
        _mod_kernel, grid=(n // tn,),
        in_specs=[pl.BlockSpec((r, d), lambda j: (0, 0)),
                  pl.BlockSpec((d, tn), lambda j: (0, j)),
                  pl.BlockSpec((1, tn), lambda j: (0, j))],
        out_specs=pl.BlockSpec((r, tn), lambda j: (0, j)),
        out_shape=jax.ShapeDtypeStruct((r, n), F32),
        compiler_params=_cparams("arbitrary"), name="adaln_mod",
    )(c_all, w_ada, b_ada.reshape(1, n))


def _inproj_kernel(x_ref, g_ref, mod_ref, *refs):
    n_out = len(refs) // 2
    x = x_ref[0]
    y = x * lax.rsqrt(jnp.mean(x * x, axis=-1, keepdims=True) + EPS) * g_ref[...]
    h = (y * (1.0 + mod_ref[0, 1:2, :]) + mod_ref[0, 0:1, :]).astype(BF16)
    for w_ref, o_ref in zip(refs[:n_out], refs[n_out:]):
        o_ref[0] = jnp.dot(h, w_ref[...], preferred_element_type=F32)


def in_projection(x, norm_g, mod, weights):
    b, l, d = x.shape
    w_specs = [pl.BlockSpec(w.shape, lambda bi, i: (0, 0)) for w in weights]
    o_specs = [pl.BlockSpec((1, TOK_TILE, w.shape[1]), lambda bi, i: (bi, i, 0)) for w in weights]
    return pl.pallas_call(
        _inproj_kernel, grid=(b, l // TOK_TILE),
        in_specs=[pl.BlockSpec((1, TOK_TILE, d), lambda bi, i: (bi, i, 0)),
                  pl.BlockSpec((1, d), lambda bi, i: (0, 0)),
                  pl.BlockSpec((1, 2, d), lambda bi, i: (bi, 0, 0))] + w_specs,
        out_specs=o_specs,
        out_shape=[jax.ShapeDtypeStruct((b, l, w.shape[1]), F32) for w in weights],
        compiler_params=_cparams("parallel", "arbitrary"), name="in_projection",
    )(x, norm_g.reshape(1, d), mod, *weights)


def _dn_feature_kernel(x_ref, w_ref, o_ref):
    x = x_ref[0]
    n = x.shape[0]
    t = lax.broadcasted_iota(jnp.int32, (n, 1), 0)
    pad = CONV_W // 2
    acc = w_ref[0, pad:pad + 1, :] * x
    for j in range(CONV_W):
        s = j - pad
        if s == 0:
            continue
        xs = pltpu.roll(x, (-s) % n, axis=0)
        bad = (t < -s) if s < 0 else (t >= n - s)
        acc = acc + w_ref[0, j:j + 1, :] * jnp.where(bad, 0.0, xs)
    y = acc * jax.nn.sigmoid(acc)
    kind = pl.program_id(1) // DN_HEADS
    inv = lax.rsqrt(jnp.sum(y * y, axis=-1, keepdims=True) + EPS)
    scale = jnp.where(kind == 0, inv * DN_DK ** -0.5, jnp.where(kind == 1, inv, 1.0))
    o_ref[0] = y * scale


def dn_features_tc(qkv, conv_w):
    b, l, n = qkv.shape
    nblk = n // LANES
    w = jnp.zeros((nblk, SUBLANES, LANES), F32).at[:, :CONV_W].set(
        conv_w.reshape(CONV_W, nblk, LANES).transpose(1, 0, 2))
    return pl.pallas_call(
        _dn_feature_kernel, grid=(b, nblk),
        in_specs=[pl.BlockSpec((1, l, LANES), lambda bi, j: (bi, 0, j)),
                  pl.BlockSpec((1, SUBLANES, LANES), lambda bi, j: (j, 0, 0))],
        out_specs=pl.BlockSpec((1, l, LANES), lambda bi, j: (bi, 0, j)),
        out_shape=jax.ShapeDtypeStruct((b, l, n), F32),
        compiler_params=_cparams("parallel", "arbitrary"), name="dn_features",
    )(qkv, w)


def _scan_chunks(rev, n_ctx, n_lat):
    if rev:
        ctx = lambda j: jnp.maximum(n_ctx - 1 - j, 0)
        lat = lambda j: jnp.where(j < n_ctx, n_lat - 1, n_lat - 1 - (j - n_ctx))
    else:
        ctx = lambda j: jnp.minimum(j, n_ctx - 1)
        lat = lambda j: jnp.maximum(j - n_ctx, 0)
    return ctx, lat


def _dn_scan_kernel(rev, dirn, n_ctx, alog_ref, dtb_ref, fc_ref, fl_ref, bac_ref, bal_ref, o_ref, s_ref):
    step = pl.program_id(1)

    @pl.when(step == 0)
    def _():
        s_ref[...] = jnp.zeros_like(s_ref)

    in_ctx = step < n_ctx
    f = jnp.where(in_ctx, fc_ref[0], fl_ref[0])
    ba = jnp.where(in_ctx, bac_ref[0], bal_ref[0])
    nh, hd, n = DN_HEADS, DN_HEADS * DN_DK, DN_HEADS * CHUNK
    stack = lambda base, w: jnp.concatenate([f[:, base + h * w:base + (h + 1) * w] for h in range(nh)], axis=0)
    q_s, k_s, v_s = stack(0, DN_DK), stack(hd, DN_DK), stack(2 * hd, DN_DV)
    incl, _ = _tri_masks(rev)
    beta_all = jax.nn.sigmoid(ba)
    g_all = -jnp.exp(alog_ref[...]) * _softplus(ba + dtb_ref[...])
    gc_all = _mask_dot(incl.astype(BF16), g_all)
    gc_t = jnp.concatenate([gc_all, gc_all], axis=0).T
    g_tot = jnp.sum(g_all, axis=0, keepdims=True)
    cb = [dirn * nh + h for h in range(nh)]
    cg = [2 * nh + c for c in cb]
    col = lambda a, cs: jnp.concatenate([a[:, c:c + 1] for c in cs], axis=0)
    beta_c, gc_c = col(beta_all, cb), col(gc_all, cg)
    gtot_c = jnp.concatenate([jnp.broadcast_to(g_tot[:, c:c + 1], (CHUNK, 1)) for c in cg], axis=0)
    gc_r = jnp.concatenate([gc_t[c:c + 1, :CHUNK] for c in cg], axis=1)
    r = lax.broadcasted_iota(jnp.int32, (n, n), 0)
    c = lax.broadcasted_iota(jnp.int32, (n, n), 1)
    same = (r // CHUNK) == (c // CHUNK)
    d = (c - r) if rev else (r - c)
    incl_bd, strict_bd = same & (d >= 0), same & (d > 0)
    eye = (r == c).astype(F32)
    decay = jnp.where(incl_bd, jnp.exp(jnp.where(incl_bd, gc_c - gc_r, 0.0)), 0.0)
    kb_s = k_s * beta_c
    lower = jnp.where(strict_bd, _dot_nt(kb_s, k_s) * decay, 0.0)
    eg_c = jnp.exp(gc_c)
    inv = eye - lower
    pw = lower
    for _ in range(5):
        pw = _dot(pw, pw)
        inv = inv + _dot(inv, pw)
    sol = _dot(inv, jnp.concatenate([v_s * beta_c, kb_s * eg_c], axis=-1))
    u_s, w_s = sol[:, :DN_DV], sol[:, DN_DV:]
    k_dec = k_s * jnp.exp(gtot_c - gc_c)
    rb = lax.broadcasted_iota(jnp.int32, (n, DN_DK), 0) // CHUNK
    expand = lambda x: jnp.concatenate([jnp.where(rb == h, x, 0.0) for h in range(nh)], axis=1)
    s = s_ref[...]
    v_new = u_s - _dot(expand(w_s), s)
    a_qk = _dot_nt(q_s, k_s) * decay
    o_s = _dot(expand(q_s * eg_c), s) + _dot(a_qk, v_new)
    gl_rows = jnp.concatenate([jnp.broadcast_to(jnp.exp(g_tot[:, cc:cc + 1]), (DN_DK, 1)) for cc in cg], axis=0)
    s_ref[...] = s * gl_rows + _dot_tn(expand(k_dec), v_new)
    o_ref[0] = jnp.concatenate([o_s[h * CHUNK:(h + 1) * CHUNK] for h in range(nh)], axis=1)


def dn_scan_tc(feat_c, feat_l, ba_c, ba_l, a_log, dt_bias, rev):
    b, l, nf = feat_l.shape
    n_ctx, n_lat = feat_c.shape[1] // CHUNK, l // CHUNK
    dirn = 1 if rev else 0
    cc, lc = _scan_chunks(rev, n_ctx, n_lat)
    lanes = lambda p: jnp.zeros((1, LANES), F32).at[0, 2 * DN_HEADS:4 * DN_HEADS].set(p.reshape(-1))
    vec = pl.BlockSpec((1, LANES), lambda bi, j: (0, 0))
    return pl.pallas_call(
        functools.partial(_dn_scan_kernel, rev, dirn, n_ctx), grid=(b, n_ctx + n_lat),
        in_specs=[vec, vec,
                  pl.BlockSpec((1, CHUNK, nf), lambda bi, j: (bi, cc(j), 0)),
                  pl.BlockSpec((1, CHUNK, nf), lambda bi, j: (bi, lc(j), 0)),
                  pl.BlockSpec((1, CHUNK, LANES), lambda bi, j: (bi, cc(j), 0)),
                  pl.BlockSpec((1, CHUNK, LANES), lambda bi, j: (bi, lc(j), 0))],
        out_specs=pl.BlockSpec((1, CHUNK, DN_HEADS * DN_DV), lambda bi, j: (bi, lc(j), 0)),
        out_shape=jax.ShapeDtypeStruct((b, l, DN_HEADS * DN_DV), F32),
        scratch_shapes=[pltpu.VMEM((DN_HEADS * DN_DK, DN_DV), F32)],
        compiler_params=_cparams("parallel", "arbitrary"), name="dn_scan_bwd" if rev else "dn_scan_fwd",
    )(lanes(a_log), lanes(dt_bias), feat_c, feat_l, ba_c, ba_l)


def _from_grid_cols(blk, n):
    cols = blk.shape[1] // n
    return jnp.concatenate([blk[:, i * n:(i + 1) * n] for i in range(cols)], axis=0)


def _gla_scan_kernel(rev, dirn, n_ctx, qkc_ref, vc_ref, lrc_ref, qkl_ref, vl_ref, lrl_ref,
                     wla_ref, bla_ref, o_ref, s_ref):
    step = pl.program_id(1)

    @pl.when(step == 0)
    def _():
        s_ref[...] = jnp.zeros_like(s_ref)

    in_ctx = step < n_ctx
    hk, hv = GLA_HEADS * GLA_DK, GLA_HEADS * GLA_DV
    qk = jnp.where(in_ctx, qkc_ref[0], _from_grid_cols(qkl_ref[0], 2 * hk))
    vv = jnp.where(in_ctx, vc_ref[0], _from_grid_cols(vl_ref[0], hv))
    lr = jnp.where(in_ctx, lrc_ref[0], _from_grid_cols(lrl_ref[0], LANES))
    incl, _ = _tri_masks(rev)
    incl_b = incl.astype(BF16)
    pre = _dot(lr, wla_ref[0]) + bla_ref[0]
    la_all = -_softplus(-pre) * (1.0 / GLA_TAU)
    bc_all = _mask_dot(incl_b, la_all)
    b_tot_all = jnp.sum(la_all, axis=0, keepdims=True)
    outs = []
    for h in range(GLA_HEADS):
        q = qk[:, h * GLA_DK:(h + 1) * GLA_DK] * GLA_DK ** -0.5
        k = qk[:, hk + h * GLA_DK:hk + (h + 1) * GLA_DK]
        v = vv[:, h * GLA_DV:(h + 1) * GLA_DV]
        bc = bc_all[:, h * GLA_DK:(h + 1) * GLA_DK]
        b_tot = b_tot_all[:, h * GLA_DK:(h + 1) * GLA_DK]
        st = s_ref[h]
        o = _dot_nt(q * jnp.exp(bc), st)
        parts = []
        for i in range(CHUNK // GLA_SUB):
            lo_r, hi_r = i * GLA_SUB, (i + 1) * GLA_SUB
            if rev:
                ref = bc[hi_r - 1:hi_r]
                c0, c1 = lo_r, CHUNK
            else:
                ref = bc[lo_r:lo_r + 1]
                c0, c1 = 0, hi_r
            qi = q[lo_r:hi_r] * jnp.exp(bc[lo_r:hi_r] - ref)
            ki = k[c0:c1] * jnp.exp(ref - bc[c0:c1])
            att = _dot_nt(qi, ki)
            rg = lax.broadcasted_iota(jnp.int32, (GLA_SUB, c1 - c0), 0) + lo_r
            cg = lax.broadcasted_iota(jnp.int32, (GLA_SUB, c1 - c0), 1) + c0
            keep = (cg >= rg) if rev else (cg <= rg)
            parts.append(_dot(jnp.where(keep, att, 0.0), v[c0:c1]))
        outs.append(o + jnp.concatenate(parts, axis=0))
        k_dec = k * jnp.exp(b_tot - bc)
        s_ref[h] = st * jnp.exp(b_tot) + _dot_tn(v, k_dec)
    o = jnp.concatenate(outs, axis=-1)
    rows = o_ref.shape[1]
    o_ref[0] = jnp.concatenate([o[i * rows:(i + 1) * rows] for i in range(CHUNK // rows)], axis=-1)


def gla_scan_tc(qk_c, v_c, lr_c, qk_l, v_l, lr_l, w_la, b_la, rev):
    b, l, _ = qk_l.shape
    rows = l // GRID_W
    cols = CHUNK // rows
    n_ctx, n_lat = qk_c.shape[1] // CHUNK, l // CHUNK
    dirn = 1 if rev else 0
    cc, lc = _scan_chunks(rev, n_ctx, n_lat)
    hv = GLA_HEADS * GLA_DV
    ctx_blk = lambda a: pl.BlockSpec((1, CHUNK, a.shape[2]), lambda bi, j: (bi, cc(j), 0))
    lat_blk = lambda n: pl.BlockSpec((1, rows, cols * n), lambda bi, j: (bi, 0, lc(j)))
    grid_view = lambda a: a.reshape(b, rows, GRID_W * a.shape[2])
    out = pl.pallas_call(
        functools.partial(_gla_scan_kernel, rev, dirn, n_ctx), grid=(b, n_ctx + n_lat),
        in_specs=[ctx_blk(qk_c), ctx_blk(v_c), ctx_blk(lr_c),
                  lat_blk(qk_l.shape[2]), lat_blk(v_l.shape[2]), lat_blk(lr_l.shape[2]),
                  pl.BlockSpec((1,) + w_la.shape[1:], lambda bi, j: (dirn, 0, 0)),
                  pl.BlockSpec((1,) + b_la.shape[1:], lambda bi, j: (dirn, 0, 0))],
        out_specs=lat_blk(hv),
        out_shape=jax.ShapeDtypeStruct((b, rows, GRID_W * hv), F32),
        scratch_shapes=[pltpu.VMEM((GLA_HEADS, GLA_DV, GLA_DK), F32)],
        compiler_params=_cparams("parallel", "arbitrary"), name="gla_scan_bwd" if rev else "gla_scan_fwd",
    )(qk_c, v_c, lr_c, grid_view(qk_l), grid_view(v_l), grid_view(lr_l), w_la, b_la)
    return out.reshape(b, l, hv)


def _head_norm_gate(o, gate, g, n_heads, dv):
    parts = []
    for h in range(n_heads):
        oh = o[:, h * dv:(h + 1) * dv]
        gh = gate[:, h * dv:(h + 1) * dv]
        yh = oh * lax.rsqrt(jnp.mean(oh * oh, axis=-1, keepdims=True) + EPS) * g
        parts.append(yh * (gh * jax.nn.sigmoid(gh)))
    return parts


def _mix_out_kernel(x_ref, dnf_ref, dnb_ref, z_ref, glf_ref, glb_ref, r_ref, mod_ref, dng_ref,
                    glg_ref, n2g_ref, wout_ref, wq_ref, x1_ref, h2_ref, q_ref):
    parts = (_head_norm_gate(dnf_ref[0] + dnb_ref[0], z_ref[0], dng_ref[...], DN_HEADS, DN_DV)
             + _head_norm_gate(glf_ref[0] + glb_ref[0], r_ref[0], glg_ref[...], GLA_HEADS, GLA_DV))
    y = jnp.dot(jnp.concatenate(parts, axis=-1).astype(BF16), wout_ref[...], preferred_element_type=F32)
    x1 = x_ref[0] + mod_ref[0, 2:3, :] * y
    x1_ref[0] = x1
    n = x1 * lax.rsqrt(jnp.mean(x1 * x1, axis=-1, keepdims=True) + EPS) * n2g_ref[...]
    h2 = n * (1.0 + mod_ref[0, 4:5, :]) + mod_ref[0, 3:4, :]
    h2_ref[0] = h2
    q_ref[0] = jnp.dot(h2.astype(BF16), wq_ref[...], preferred_element_type=F32)


def mix_out_tc(x, dn_f, dn_b, z, gl_f, gl_b, r, mod_l, dn_g, gla_g, n2_g, w_out, w_q):
    b, l, d = x.shape
    tok = lambda n: pl.BlockSpec((1, TOK_TILE, n), lambda bi, i: (bi, i, 0))
    full = lambda a: pl.BlockSpec(a.shape, lambda bi, i: (0,) * a.ndim)
    dn_g, gla_g, n2_g = dn_g.reshape(1, -1), gla_g.reshape(1, -1), n2_g.reshape(1, -1)
    nq = w_q.shape[1]
    return pl.pallas_call(
        _mix_out_kernel, grid=(b, l // TOK_TILE),
        in_specs=[tok(d), tok(dn_f.shape[2]), tok(dn_b.shape[2]), tok(z.shape[2]),
                  tok(gl_f.shape[2]), tok(gl_b.shape[2]), tok(r.shape[2]),
                  pl.BlockSpec((1,) + mod_l.shape[1:], lambda bi, i: (bi, 0, 0)),
                  full(dn_g), full(gla_g), full(n2_g), full(w_out), full(w_q)],
        out_specs=[tok(d), tok(d), tok(nq)],
        out_shape=[jax.ShapeDtypeStruct((b, l, d), F32), jax.ShapeDtypeStruct((b, l, d), F32),
                   jax.ShapeDtypeStruct((b, l, nq), F32)],
        compiler_params=_cparams("parallel", "arbitrary"), name="mix_out",
    )(x, dn_f, dn_b, z, gl_f, gl_b, r, mod_l, dn_g, gla_g, n2_g, w_out, w_q)


def _top_rows(s, k, payload=None):
    n = s.shape[0]
    row = lax.broadcasted_iota(jnp.int32, s.shape, 0).astype(F32)
    vals, picked = [], []
    for _ in range(k):
        m = jnp.max(s, axis=0, keepdims=True)
        first = jnp.min(jnp.where(s == m, row, float(n)), axis=0, keepdims=True)
        sel = row == first
        vals.append(m)
        if payload is None:
            picked.append(first)
        else:
            picked.append(jnp.max(jnp.where(sel, payload, -1.0), axis=0, keepdims=True))
        s = jnp.where(sel, -jnp.inf, s)
    return jnp.concatenate(vals, axis=0), jnp.concatenate(picked, axis=0)


def _candidate_rows(s0, i0, s1, i1):
    k = s0.shape[0]
    wide = SUBLANES
    blocks_s = [s0[0:1] + s1]
    blocks_i = [i0[0:1] * float(PEER_NKEYS) + i1]
    col = lax.broadcasted_iota(jnp.int32, (wide, s0.shape[1]), 0)
    for i in range(1, wide):
        keep = col < (k // (i + 1))
        blocks_s.append(jnp.where(keep, s0[i:i + 1] + s1[0:wide], -jnp.inf))
        blocks_i.append(i0[i:i + 1] * float(PEER_NKEYS) + i1[0:wide])
    blocks_s.append(s0[wide:k] + s1[0:1])
    blocks_i.append(i0[wide:k] * float(PEER_NKEYS) + i1[0:1])
    return jnp.concatenate(blocks_s, axis=0), jnp.concatenate(blocks_i, axis=0)


def _select_kernel(q_ref, k_ref, idx_ref, gate_ref, idx_s, gate_s):
    half = PEER_DQ // 2

    def head(h, carry):
        tops = []
        for p in range(2):
            qp = q_ref[:, pl.ds(pl.multiple_of(h * PEER_DQ + p * half, half), half)]
            s = lax.dot_general(k_ref[h, p], qp, (((1,), (1,)), ((), ())),
                                preferred_element_type=F32,
                                precision=lax.Precision.HIGHEST)
            tops.append(_top_rows(s, PEER_TOPK))
        (s0, i0), (s1, i1) = tops
        cand_s, cand_i = _candidate_rows(s0, i0, s1, i1)
        best_s, idx = _top_rows(cand_s, PEER_TOPK, payload=cand_i)
        e = jnp.exp(best_s - best_s[0:1])
        r0 = pl.multiple_of(h * PEER_TOPK, PEER_TOPK)
        idx_s[pl.ds(r0, PEER_TOPK), :] = idx
        gate_s[pl.ds(r0, PEER_TOPK), :] = e / jnp.sum(e, axis=0, keepdims=True)
        return carry

    lax.fori_loop(0, PEER_HEADS, head, 0)
    idx_ref[...] = idx_s[...].T.astype(jnp.int32)
    gate_ref[...] = gate_s[...].T


def peer_select_tc(q, keys):
    n_tok = q.shape[0]
    out_spec = pl.BlockSpec((SELECT_TILE, PICKS), lambda i: (i, 0))
    return pl.pallas_call(
        _select_kernel,
        grid=(n_tok // SELECT_TILE,),
        in_specs=[pl.BlockSpec((SELECT_TILE, q.shape[1]), lambda i: (i, 0)),
                  pl.BlockSpec(keys.shape, lambda i: (0, 0, 0, 0))],
        out_specs=[out_spec, out_spec],
        out_shape=[jax.ShapeDtypeStruct((n_tok, PICKS), jnp.int32),
                   jax.ShapeDtypeStruct((n_tok, PICKS), F32)],
        scratch_shapes=[pltpu.VMEM((PICKS, SELECT_TILE), F32), pltpu.VMEM((PICKS, SELECT_TILE), F32)],
        compiler_params=_cparams("parallel"), name="peer_select",
    )(q, keys)


def _sc_mesh():
    return plsc.VectorSubcoreMesh(core_axis_name="c", subcore_axis_name="s")


def _sc_pipeline(body, n_steps, in_specs, out_specs, operands):
    pltpu.emit_pipeline(
        body, grid=(n_steps,), in_specs=in_specs, out_specs=out_specs,
        core_axis_name=("c", "s"), dimension_semantics=(pltpu.PARALLEL,),
        trace_scopes=False,
    )(*operands)


def pack_table(t):
    half = t.shape[1] // 2
    bits = lax.bitcast_convert_type(t, jnp.uint32)
    lo_bits, hi_bits = bits[:, :half], bits[:, half:]
    low = (lo_bits + jnp.uint32(0x7FFF) + ((lo_bits >> 16) & jnp.uint32(1))) >> 16
    mag = hi_bits & jnp.uint32(0x7FFFFFFF)
    top = jnp.where(mag >= jnp.uint32(0x10000),
                    (mag - low + jnp.uint32(0x8000)) & jnp.uint32(0xFFFF0000), jnp.uint32(0))
    word = (hi_bits & jnp.uint32(0x80000000)) | top | low
    return lax.bitcast_convert_type(word, jnp.int32)


def _unpack(w):
    return plsc.bitcast(w << 16, F32), plsc.bitcast(w, F32)


def peer_act_partial_sc(u_p, idx2, h):
    n_groups, (n_tok, d) = idx2.shape[0], h.shape
    half = d // 2
    nsub = TOK_STEP * GROUPS_PER_TOK
    n_chunks = half // SC_LANES

    @functools.partial(
        pl.kernel, mesh=_sc_mesh(), compiler_params=pltpu.CompilerParams(needs_layout_passes=False),
        out_type=jax.ShapeDtypeStruct((n_groups, PICK_GROUP * SC_LANES), F32),
        scratch_types=[pltpu.VMEM((2, PICK_GROUP, half), jnp.int32),
                       pltpu.SemaphoreType.DMA((2,))],
    )
    def k(u_hbm, i_hbm, h_hbm, o_hbm, rows, sems):
        def body(i_v, h_v, o_v):
            def fetch(j, slot):
                return pltpu.make_async_copy(u_hbm.at[i_v.at[j]], rows.at[slot], sems.at[slot])

            fetch(0, 0).start()

            def sub(j, carry):
                slot = j % 2

                @pl.when(j + 1 < nsub)
                def _():
                    fetch(j + 1, 1 - slot).start()

                fetch(j, slot).wait()
                t = j // GROUPS_PER_TOK

                def picks(g, carry2):
                    kb = g * ACT_UNROLL
                    accs = [None] * ACT_UNROLL
                    for c in range(n_chunks):
                        h_lo = h_v[t, pl.ds(c * SC_LANES, SC_LANES)]
                        h_hi = h_v[t, pl.ds(half + c * SC_LANES, SC_LANES)]
                        for i in range(ACT_UNROLL):
                            lo, hi = _unpack(rows[slot, kb + i, pl.ds(c * SC_LANES, SC_LANES)])
                            p = lo * h_lo + hi * h_hi
                            accs[i] = p if accs[i] is None else accs[i] + p
                    for i in range(ACT_UNROLL):
                        o_v[j, pl.ds((kb + i) * SC_LANES, SC_LANES)] = accs[i]
                    return carry2

                lax.fori_loop(0, PICK_GROUP // ACT_UNROLL, picks, 0)
                return carry

            lax.fori_loop(0, nsub, sub, 0)

        _sc_pipeline(
            body, n_tok // TOK_STEP,
            [pl.BlockSpec((nsub, PICK_GROUP), lambda i: (i, 0)),
             pl.BlockSpec((TOK_STEP, d), lambda i: (i, 0))],
            [pl.BlockSpec((nsub, PICK_GROUP * SC_LANES), lambda i: (i, 0))],
            (i_hbm, h_hbm, o_hbm))

    return k(u_p, idx2, h)


def peer_combine_sc(v_p, idx2, coef_b):
    n_groups = idx2.shape[0]
    half = v_p.shape[1]
    n_tok = n_groups // GROUPS_PER_TOK
    nsub = TOK_STEP * GROUPS_PER_TOK
    n_chunks = half // SC_LANES

    @functools.partial(
        pl.kernel, mesh=_sc_mesh(), compiler_params=pltpu.CompilerParams(needs_layout_passes=False),
        out_type=jax.ShapeDtypeStruct((n_tok, 2 * half), F32),
        scratch_types=[pltpu.VMEM((2, PICK_GROUP, half), jnp.int32),
                       pltpu.SemaphoreType.DMA((2,))],
    )
    def k(v_hbm, i_hbm, c_hbm, o_hbm, rows, sems):
        def body(i_v, c_v, o_v):
            def fetch(j, slot):
                return pltpu.make_async_copy(v_hbm.at[i_v.at[j]], rows.at[slot], sems.at[slot])

            fetch(0, 0).start()
            for j in range(nsub):
                slot = j % 2
                if j + 1 < nsub:
                    fetch(j + 1, 1 - slot).start()
                fetch(j, slot).wait()
                t = j // GROUPS_PER_TOK
                first = j % GROUPS_PER_TOK == 0
                cks = [c_v[j, pl.ds(kk * SC_LANES, SC_LANES)] for kk in range(PICK_GROUP)]

                def chunk(c, carry, slot=slot, t=t, first=first, cks=cks):
                    l = c * SC_LANES
                    los, his = [None] * 2, [None] * 2
                    for kk in range(PICK_GROUP):
                        lo, hi = _unpack(rows[slot, kk, pl.ds(l, SC_LANES)])
                        a, b_ = cks[kk] * lo, cks[kk] * hi
                        los[kk % 2] = a if los[kk % 2] is None else los[kk % 2] + a
                        his[kk % 2] = b_ if his[kk % 2] is None else his[kk % 2] + b_
                    tot_lo, tot_hi = los[0] + los[1], his[0] + his[1]
                    if not first:
                        tot_lo = tot_lo + o_v[t, pl.ds(l, SC_LANES)]
                        tot_hi = tot_hi + o_v[t, pl.ds(half + l, SC_LANES)]
                    o_v[t, pl.ds(l, SC_LANES)] = tot_lo
                    o_v[t, pl.ds(half + l, SC_LANES)] = tot_hi
                    return carry

                lax.fori_loop(0, n_chunks, chunk, 0)

        _sc_pipeline(
            body, n_tok // TOK_STEP,
            [pl.BlockSpec((nsub, PICK_GROUP), lambda i: (i, 0)),
             pl.BlockSpec((nsub, PICK_GROUP * SC_LANES), lambda i: (i, 0))],
            [pl.BlockSpec((TOK_STEP, 2 * half), lambda i: (i, 0))],
            (i_hbm, c_hbm, o_hbm))

    return k(v_p, idx2, coef_b)


def _segment_matrix():
    r = lax.broadcasted_iota(jnp.int32, (PICK_GROUP * SC_LANES, PICK_GROUP), 0) // SC_LANES
    c = lax.broadcasted_iota(jnp.int32, (PICK_GROUP * SC_LANES, PICK_GROUP), 1)
    return (r == c).astype(F32)


def _coef_kernel(part_ref, gate_ref, o_ref):
    seg = _segment_matrix()
    act = jnp.dot(part_ref[...], seg, preferred_element_type=F32, precision=lax.Precision.HIGHEST)
    coef = gate_ref[...] * (0.5 * act * (1.0 + lax.erf(act * (2.0 ** -0.5))))
    o_ref[...] = lax.dot_general(coef, seg, (((1,), (1,)), ((), ())), preferred_element_type=F32,
                                 precision=lax.Precision.HIGHEST)


def peer_coef_tc(part, gate2):
    n_groups, width = part.shape
    tile = 1024
    return pl.pallas_call(
        _coef_kernel,
        grid=(n_groups // tile,),
        in_specs=[pl.BlockSpec((tile, width), lambda i: (i, 0)),
                  pl.BlockSpec((tile, PICK_GROUP), lambda i: (i, 0))],
        out_specs=pl.BlockSpec((tile, width), lambda i: (i, 0)),
        out_shape=jax.ShapeDtypeStruct((n_groups, width), F32),
        compiler_params=_cparams("parallel"), name="peer_coef",
    )(part, gate2)


def _final_kernel(x_ref, y_ref, mod_ref, g_ref, o_ref):
    x = x_ref[0] + mod_ref[0, 5:6, :] * y_ref[0]
    o_ref[0] = x * lax.rsqrt(jnp.mean(x * x, axis=-1, keepdims=True) + EPS) * g_ref[...]


def final_tc(x1, y, mod_l, final_g):
    b, l, d = x1.shape
    tok = pl.BlockSpec((1, TOK_TILE, d), lambda bi, i: (bi, i, 0))
    return pl.pallas_call(
        _final_kernel, grid=(b, l // TOK_TILE),
        in_specs=[tok, tok, pl.BlockSpec((1,) + mod_l.shape[1:], lambda bi, i: (bi, 0, 0)),
                  pl.BlockSpec((1, d), lambda bi, i: (0, 0))],
        out_specs=tok, out_shape=jax.ShapeDtypeStruct((b, l, d), F32),
        compiler_params=_cparams("parallel", "arbitrary"), name="final_norm",
    )(x1, y, mod_l, final_g.reshape(1, d))


def _pad_cols(w, n):
    return jnp.pad(w, ((0, 0), (0, n - w.shape[1])))


def _exact_zero(v):
    return jnp.minimum(jnp.abs(v), 0.0)


def forward(x, c, ctx, c_ctx, w_ada, b_ada, norm1_g, norm2_g, w_in, conv_w, dn_a_log,
            dn_dt_bias, dn_norm_g, gla_wa2, gla_ba, gla_norm_g, w_out, peer_wq, peer_keys,
            peer_u, peer_v, final_g):
    b, l, d = x.shape
    c_all = jnp.concatenate([c, c_ctx[None]], axis=0)
    c_all = jnp.pad(c_all, ((0, (-c_all.shape[0]) % SUBLANES), (0, 0)))
    mod = adaln_mod(c_all, w_ada, b_ada)
    mod_l = mod[:b].reshape(b, 6, d)
    mod_c = jnp.broadcast_to(mod[b].reshape(1, 6, d), (b, 6, d))
    o = DN_QKV
    hv = DN_HEADS * DN_DV
    w_dn_qkv, w_dn_z = w_in[:, :o], w_in[:, o:o + hv]
    w_dn_ba = _pad_cols(w_in[:, o + hv:DN_COLS], LANES)
    g0 = DN_COLS
    gqk, gv = 2 * GLA_HEADS * GLA_DK, GLA_HEADS * GLA_DV
    w_gl_qk, w_gl_v = w_in[:, g0:g0 + gqk], w_in[:, g0 + gqk:g0 + gqk + gv]
    w_gl_r = w_in[:, g0 + gqk + gv:g0 + gqk + 2 * gv]
    w_gl_lr = _pad_cols(w_in[:, g0 + gqk + 2 * gv:], LANES)
    w_lat = [w.astype(BF16) for w in (w_dn_qkv, w_dn_ba, w_gl_qk, w_gl_v, w_gl_lr, w_dn_z, w_gl_r)]
    w_ctx = w_lat[:5]
    w_la = jnp.zeros((2, LANES, GLA_HEADS * GLA_DK), F32)
    for dd in range(2):
        w_la = w_la.at[dd, dd * GLA_LR:(dd + 1) * GLA_LR].set(gla_wa2[dd])
    w_la = w_la.astype(BF16)
    b_la = gla_ba.reshape(2, 1, GLA_HEADS * GLA_DK)
    w_out_b, w_q_b = w_out.astype(BF16), peer_wq.astype(BF16)
    u_p, v_p = pack_table(peer_u), pack_table(peer_v)

    bg = b // BATCH_GROUPS
    n_tok = bg * l

    def mixer(xg, ctxg, mod_cg, mod_g, after_select, after_combine):
        mod_cg, mod_g = mod_cg + after_select, mod_g + after_select
        c_qkv, c_ba, c_qk, c_v, c_lr = in_projection(ctxg, norm1_g, mod_cg[:, 0:2], w_ctx)
        l_qkv, l_ba, l_qk, l_v, l_lr, l_z, l_r = in_projection(xg, norm1_g, mod_g[:, 0:2], w_lat)
        feat_c, feat_l = dn_features_tc(c_qkv, conv_w), dn_features_tc(l_qkv, conv_w)
        dn_f = dn_scan_tc(feat_c, feat_l, c_ba, l_ba, dn_a_log, dn_dt_bias, rev=False)
        dn_b = dn_scan_tc(feat_c, feat_l, c_ba, l_ba, dn_a_log, dn_dt_bias, rev=True)
        gl_f = gla_scan_tc(c_qk, c_v, c_lr, l_qk, l_v, l_lr, w_la, b_la, rev=False)
        gl_b = gla_scan_tc(c_qk, c_v, c_lr, l_qk, l_v, l_lr, w_la, b_la, rev=True)
        return mix_out_tc(xg, dn_f, dn_b, l_z, gl_f, gl_b, l_r, mod_g, dn_norm_g + after_combine,
                          gla_norm_g, norm2_g, w_out_b, w_q_b)

    def select_and_act(h2, q, after_coef):
        idx, gate = peer_select_tc(q.reshape(n_tok, -1), peer_keys + after_coef)
        idx2 = idx.reshape(n_tok * GROUPS_PER_TOK, PICK_GROUP)
        gate2 = gate.reshape(n_tok * GROUPS_PER_TOK, PICK_GROUP)
        return idx2, gate2, peer_act_partial_sc(u_p, idx2, h2.reshape(n_tok, d))

    def coef_and_combine(st, after_mixer):
        coef_b = peer_coef_tc(st["part"], st["gate2"] + after_mixer)
        st["y"] = peer_combine_sc(v_p, st["idx2"], coef_b)
        return _exact_zero(coef_b[0, 0])

    zero = jnp.zeros((), F32)
    slices, z_sel, z_coef = [], zero, zero
    for i in range(0, b, bg):
        g = len(slices)
        z_comb = _exact_zero(slices[g - 2]["y"][0, 0]) if g >= 2 else zero
        mod_g = mod_l[i:i + bg]
        x1, h2, q = mixer(x[i:i + bg], ctx[i:i + bg], mod_c[i:i + bg], mod_g, z_sel, z_comb)
        if g >= 1:
            z_coef = coef_and_combine(slices[g - 1], _exact_zero(x1[0, 0, 0]))
        idx2, gate2, part = select_and_act(h2, q, z_coef)
        z_sel = _exact_zero(gate2[0, 0])
        slices.append(dict(x1=x1, mod=mod_g, idx2=idx2, gate2=gate2, part=part))
    coef_and_combine(slices[-1], zero)
    outs = [final_tc(st["x1"], st["y"].reshape(bg, l, d), st["mod"], final_g) for st in slices]
    return jnp.concatenate(outs, axis=0)


def kernel(x, c, ctx, c_ctx, w_ada, b_ada, norm1_g, norm2_g, w_in, conv_w, dn_a_log,
           dn_dt_bias, dn_norm_g, gla_wa2, gla_ba, gla_norm_g, w_out, peer_wq, peer_keys,
           peer_u, peer_v, final_g):
    assert w_ada.shape[0] == 1, "single-layer block: the context stream is only consumed, never updated"
    return forward(x, c, ctx, c_ctx, w_ada[0], b_ada[0], norm1_g[0], norm2_g[0], w_in[0], conv_w[0],
                   dn_a_log[0], dn_dt_bias[0], dn_norm_g[0], gla_wa2[0], gla_ba[0], gla_norm_g[0],
                   w_out[0], peer_wq[0], peer_keys[0], peer_u[0], peer_v[0], final_g)
```

```python
import functools

import jax
import jax.numpy as jnp
from jax import lax
from jax.experimental import pallas as pl
from jax.experimental.pallas import tpu as pltpu
from jax.experimental.pallas import tpu_sc as plsc

GRID_W = 64
DN_HEADS = 4
DN_DK = 128
DN_DV = 128
CONV_W = 5
GLA_HEADS = 4
GLA_DK = 64
GLA_DV = 128
GLA_LR = 16
GLA_TAU = 16.0
CHUNK = 64
PEER_HEADS = 8
PEER_NKEYS = 128
PEER_DQ = 256
PEER_TOPK = 16
EPS = 1e-6
DN_QKV = 2 * DN_HEADS * DN_DK + DN_HEADS * DN_DV
DN_COLS = DN_QKV + DN_HEADS * DN_DV + 4 * DN_HEADS

SUBLANES = 8
LANES = 128
SC_LANES = 16
VMEM_LIMIT_BYTES = 48 * 1024 * 1024

TOK_TILE = 256
SELECT_TILE = 256
SCAN_BATCH = 2
GLA_SUB = 16
PICK_GROUP = 32
TOK_STEP = 2
ACT_UNROLL = 16
BATCH_GROUPS = 8
PICKS = PEER_HEADS * PEER_TOPK
GROUPS_PER_TOK = PICKS // PICK_GROUP

F32 = jnp.float32
BF16 = jnp.bfloat16


def _cparams(*semantics):
    return pltpu.CompilerParams(dimension_semantics=semantics, vmem_limit_bytes=VMEM_LIMIT_BYTES)


def _dot(a, b):
    return jnp.dot(a.astype(BF16), b.astype(BF16), preferred_element_type=F32)


def _dot_nt(a, b):
    return lax.dot_general(a.astype(BF16), b.astype(BF16), (((1,), (1,)), ((), ())),
                           preferred_element_type=F32)


def _dot_tn(a, b):
    return lax.dot_general(a.astype(BF16), b.astype(BF16), (((0,), (0,)), ((), ())),
                           preferred_element_type=F32)


def _split(x):
    hi = x.astype(BF16)
    return hi, (x - hi.astype(F32)).astype(BF16)


def _mask_dot(mask_bf16, x):
    hi, lo = _split(x)
    return (jnp.dot(mask_bf16, hi, preferred_element_type=F32)
            + jnp.dot(mask_bf16, lo, preferred_element_type=F32))


def _softplus(x):
    return jnp.maximum(x, 0.0) + jnp.log(1.0 + jnp.exp(-jnp.abs(x)))


def _tri_masks(rev):
    r = lax.broadcasted_iota(jnp.int32, (CHUNK, CHUNK), 0)
    c = lax.broadcasted_iota(jnp.int32, (CHUNK, CHUNK), 1)
    d = (c - r) if rev else (r - c)
    return d >= 0, d > 0


def _mod_kernel(c_ref, w_ref, b_ref, o_ref):
    c = c_ref[...]
    s = c * jax.nn.sigmoid(c)
    o_ref[...] = jnp.dot(s, w_ref[...], preferred_element_type=F32,
                         precision=lax.Precision.HIGHEST) + b_ref[...]


def adaln_mod(c_all, w_ada, b_ada):
    r, d = c_all.shape
    n = w_ada.shape[1]
    tn = 512
    return pl.pallas_call(
        _mod_kernel, grid=(n // tn,),
        in_specs=[pl.BlockSpec((r, d), lambda j: (0, 0)),
                  pl.BlockSpec((d, tn), lambda j: (0, j)),
                  pl.BlockSpec((1, tn), lambda j: (0, j))],
        out_specs=pl.BlockSpec((r, tn), lambda j: (0, j)),
        out_shape=jax.ShapeDtypeStruct((r, n), F32),
        compiler_params=_cparams("arbitrary"), name="adaln_mod",
    )(c_all, w_ada, b_ada.reshape(1, n))


def _inproj_kernel(x_ref, g_ref, mod_ref, *refs):
    n_out = len(refs) // 2
    x = x_ref[0]
    y = x * lax.rsqrt(jnp.mean(x * x, axis=-1, keepdims=True) + EPS) * g_ref[...]
    h = (y * (1.0 + mod_ref[0, 1:2, :]) + mod_ref[0, 0:1, :]).astype(BF16)
    for w_ref, o_ref in zip(refs[:n_out], refs[n_out:]):
        o_ref[0] = jnp.dot(h, w_ref[...], preferred_element_type=F32)


def in_projection(x, norm_g, mod, weights):
    b, l, d = x.shape
    w_specs = [pl.BlockSpec(w.shape, lambda bi, i: (0, 0)) for w in weights]
    o_specs = [pl.BlockSpec((1, TOK_TILE, w.shape[1]), lambda bi, i: (bi, i, 0)) for w in weights]
    return pl.pallas_call(
        _inproj_kernel, grid=(b, l // TOK_TILE),
        in_specs=[pl.BlockSpec((1, TOK_TILE, d), lambda bi, i: (bi, i, 0)),
                  pl.BlockSpec((1, d), lambda bi, i: (0, 0)),
                  pl.BlockSpec((1, 2, d), lambda bi, i: (bi, 0, 0))] + w_specs,
        out_specs=o_specs,
        out_shape=[jax.ShapeDtypeStruct((b, l, w.shape[1]), F32) for w in weights],
        compiler_params=_cparams("parallel", "arbitrary"), name="in_projection",
    )(x, norm_g.reshape(1, d), mod, *weights)


def _dn_feature_kernel(x_ref, w_ref, o_ref):
    x = x_ref[0]
    n = x.shape[0]
    t = lax.broadcasted_iota(jnp.int32, (n, 1), 0)
    pad = CONV_W // 2
    acc = w_ref[0, pad:pad + 1, :] * x
    for j in range(CONV_W):
        s = j - pad
        if s == 0:
            continue
        xs = pltpu.roll(x, (-s) % n, axis=0)
        bad = (t < -s) if s < 0 else (t >= n - s)
        acc = acc + w_ref[0, j:j + 1, :] * jnp.where(bad, 0.0, xs)
    y = acc * jax.nn.sigmoid(acc)
    kind = pl.program_id(1) // DN_HEADS
    inv = lax.rsqrt(jnp.sum(y * y, axis=-1, keepdims=True) + EPS)
    scale = jnp.where(kind == 0, inv * DN_DK ** -0.5, jnp.where(kind == 1, inv, 1.0))
    o_ref[0] = y * scale


def dn_features_tc(qkv, conv_w):
    b, l, n = qkv.shape
    nblk = n // LANES
    w = jnp.zeros((nblk, SUBLANES, LANES), F32).at[:, :CONV_W].set(
        conv_w.reshape(CONV_W, nblk, LANES).transpose(1, 0, 2))
    return pl.pallas_call(
        _dn_feature_kernel, grid=(b, nblk),
        in_specs=[pl.BlockSpec((1, l, LANES), lambda bi, j: (bi, 0, j)),
                  pl.BlockSpec((1, SUBLANES, LANES), lambda bi, j: (j, 0, 0))],
        out_specs=pl.BlockSpec((1, l, LANES), lambda bi, j: (bi, 0, j)),
        out_shape=jax.ShapeDtypeStruct((b, l, n), F32),
        compiler_params=_cparams("parallel", "arbitrary"), name="dn_features",
    )(qkv, w)


def _scan_chunks(rev, n_ctx, n_lat):
    if rev:
        ctx = lambda j: jnp.maximum(n_ctx - 1 - j, 0)
        lat = lambda j: jnp.where(j < n_ctx, n_lat - 1, n_lat - 1 - (j - n_ctx))
    else:
        ctx = lambda j: jnp.minimum(j, n_ctx - 1)
        lat = lambda j: jnp.maximum(j - n_ctx, 0)
    return ctx, lat


def _dn_scan_kernel(rev, dirn, n_ctx, alog_ref, dtb_ref, fc_ref, fl_ref, bac_ref, bal_ref, o_ref, s_ref):
    step = pl.program_id(1)

    @pl.when(step == 0)
    def _():
        s_ref[...] = jnp.zeros_like(s_ref)

    in_ctx = step < n_ctx
    f = jnp.where(in_ctx, fc_ref[0], fl_ref[0])
    ba = jnp.where(in_ctx, bac_ref[0], bal_ref[0])
    nh, hd, n = DN_HEADS, DN_HEADS * DN_DK, DN_HEADS * CHUNK
    stack = lambda base, w: jnp.concatenate([f[:, base + h * w:base + (h + 1) * w] for h in range(nh)], axis=0)
    q_s, k_s, v_s = stack(0, DN_DK), stack(hd, DN_DK), stack(2 * hd, DN_DV)
    incl, _ = _tri_masks(rev)
    beta_all = jax.nn.sigmoid(ba)
    g_all = -jnp.exp(alog_ref[...]) * _softplus(ba + dtb_ref[...])
    gc_all = _mask_dot(incl.astype(BF16), g_all)
    gc_t = jnp.concatenate([gc_all, gc_all], axis=0).T
    g_tot = jnp.sum(g_all, axis=0, keepdims=True)
    cb = [dirn * nh + h for h in range(nh)]
    cg = [2 * nh + c for c in cb]
    col = lambda a, cs: jnp.concatenate([a[:, c:c + 1] for c in cs], axis=0)
    beta_c, gc_c = col(beta_all, cb), col(gc_all, cg)
    gtot_c = jnp.concatenate([jnp.broadcast_to(g_tot[:, c:c + 1], (CHUNK, 1)) for c in cg], axis=0)
    gc_r = jnp.concatenate([gc_t[c:c + 1, :CHUNK] for c in cg], axis=1)
    r = lax.broadcasted_iota(jnp.int32, (n, n), 0)
    c = lax.broadcasted_iota(jnp.int32, (n, n), 1)
    same = (r // CHUNK) == (c // CHUNK)
    d = (c - r) if rev else (r - c)
    incl_bd, strict_bd = same & (d >= 0), same & (d > 0)
    eye = (r == c).astype(F32)
    decay = jnp.where(incl_bd, jnp.exp(jnp.where(incl_bd, gc_c - gc_r, 0.0)), 0.0)
    kb_s = k_s * beta_c
    lower = jnp.where(strict_bd, _dot_nt(kb_s, k_s) * decay, 0.0)
    eg_c = jnp.exp(gc_c)
    inv = eye - lower
    pw = lower
    for _ in range(5):
        pw = _dot(pw, pw)
        inv = inv + _dot(inv, pw)
    sol = _dot(inv, jnp.concatenate([v_s * beta_c, kb_s * eg_c], axis=-1))
    u_s, w_s = sol[:, :DN_DV], sol[:, DN_DV:]
    k_dec = k_s * jnp.exp(gtot_c - gc_c)
    rb = lax.broadcasted_iota(jnp.int32, (n, DN_DK), 0) // CHUNK
    expand = lambda x: jnp.concatenate([jnp.where(rb == h, x, 0.0) for h in range(nh)], axis=1)
    s = s_ref[...]
    v_new = u_s - _dot(expand(w_s), s)
    a_qk = _dot_nt(q_s, k_s) * decay
    o_s = _dot(expand(q_s * eg_c), s) + _dot(a_qk, v_new)
    gl_rows = jnp.concatenate([jnp.broadcast_to(jnp.exp(g_tot[:, cc:cc + 1]), (DN_DK, 1)) for cc in cg], axis=0)
    s_ref[...] = s * gl_rows + _dot_tn(expand(k_dec), v_new)
    o_ref[0] = jnp.concatenate([o_s[h * CHUNK:(h + 1) * CHUNK] for h in range(nh)], axis=1)


def dn_scan_tc(feat_c, feat_l, ba_c, ba_l, a_log, dt_bias, rev):
    b, l, nf = feat_l.shape
    n_ctx, n_lat = feat_c.shape[1] // CHUNK, l // CHUNK
    dirn = 1 if rev else 0
    cc, lc = _scan_chunks(rev, n_ctx, n_lat)
    lanes = lambda p: jnp.zeros((1, LANES), F32).at[0, 2 * DN_HEADS:4 * DN_HEADS].set(p.reshape(-1))
    vec = pl.BlockSpec((1, LANES), lambda bi, j: (0, 0))
    return pl.pallas_call(
        functools.partial(_dn_scan_kernel, rev, dirn, n_ctx), grid=(b, n_ctx + n_lat),
        in_specs=[vec, vec,
                  pl.BlockSpec((1, CHUNK, nf), lambda bi, j: (bi, cc(j), 0)),
                  pl.BlockSpec((1, CHUNK, nf), lambda bi, j: (bi, lc(j), 0)),
                  pl.BlockSpec((1, CHUNK, LANES), lambda bi, j: (bi, cc(j), 0)),
                  pl.BlockSpec((1, CHUNK, LANES), lambda bi, j: (bi, lc(j), 0))],
        out_specs=pl.BlockSpec((1, CHUNK, DN_HEADS * DN_DV), lambda bi, j: (bi, lc(j), 0)),
        out_shape=jax.ShapeDtypeStruct((b, l, DN_HEADS * DN_DV), F32),
        scratch_shapes=[pltpu.VMEM((DN_HEADS * DN_DK, DN_DV), F32)],
        compiler_params=_cparams("parallel", "arbitrary"), name="dn_scan_bwd" if rev else "dn_scan_fwd",
    )(lanes(a_log), lanes(dt_bias), feat_c, feat_l, ba_c, ba_l)


def _from_grid_cols(blk, n):
    cols = blk.shape[1] // n
    return jnp.concatenate([blk[:, i * n:(i + 1) * n] for i in range(cols)], axis=0)


def _gla_scan_kernel(rev, dirn, n_ctx, qkc_ref, vc_ref, lrc_ref, qkl_ref, vl_ref, lrl_ref,
                     wla_ref, bla_ref, o_ref, s_ref):
    step = pl.program_id(1)

    @pl.when(step == 0)
    def _():
        s_ref[...] = jnp.zeros_like(s_ref)

    in_ctx = step < n_ctx
    hk, hv = GLA_HEADS * GLA_DK, GLA_HEADS * GLA_DV
    qk = jnp.where(in_ctx, qkc_ref[0], _from_grid_cols(qkl_ref[0], 2 * hk))
    vv = jnp.where(in_ctx, vc_ref[0], _from_grid_cols(vl_ref[0], hv))
    lr = jnp.where(in_ctx, lrc_ref[0], _from_grid_cols(lrl_ref[0], LANES))
    incl, _ = _tri_masks(rev)
    incl_b = incl.astype(BF16)
    pre = _dot(lr, wla_ref[0]) + bla_ref[0]
    la_all = -_softplus(-pre) * (1.0 / GLA_TAU)
    bc_all = _mask_dot(incl_b, la_all)
    b_tot_all = jnp.sum(la_all, axis=0, keepdims=True)
    outs = []
    for h in range(GLA_HEADS):
        q = qk[:, h * GLA_DK:(h + 1) * GLA_DK] * GLA_DK ** -0.5
        k = qk[:, hk + h * GLA_DK:hk + (h + 1) * GLA_DK]
        v = vv[:, h * GLA_DV:(h + 1) * GLA_DV]
        bc = bc_all[:, h * GLA_DK:(h + 1) * GLA_DK]
        b_tot = b_tot_all[:, h * GLA_DK:(h + 1) * GLA_DK]
        st = s_ref[h]
        o = _dot_nt(q * jnp.exp(bc), st)
        parts = []
        for i in range(CHUNK // GLA_SUB):
            lo_r, hi_r = i * GLA_SUB, (i + 1) * GLA_SUB
            if rev:
                ref = bc[hi_r - 1:hi_r]
                c0, c1 = lo_r, CHUNK
            else:
                ref = bc[lo_r:lo_r + 1]
                c0, c1 = 0, hi_r
            qi = q[lo_r:hi_r] * jnp.exp(bc[lo_r:hi_r] - ref)
            ki = k[c0:c1] * jnp.exp(ref - bc[c0:c1])
            att = _dot_nt(qi, ki)
            rg = lax.broadcasted_iota(jnp.int32, (GLA_SUB, c1 - c0), 0) + lo_r
            cg = lax.broadcasted_iota(jnp.int32, (GLA_SUB, c1 - c0), 1) + c0
            keep = (cg >= rg) if rev else (cg <= rg)
            parts.append(_dot(jnp.where(keep, att, 0.0), v[c0:c1]))
        outs.append(o + jnp.concatenate(parts, axis=0))
        k_dec = k * jnp.exp(b_tot - bc)
        s_ref[h] = st * jnp.exp(b_tot) + _dot_tn(v, k_dec)
    o = jnp.concatenate(outs, axis=-1)
    rows = o_ref.shape[1]
    o_ref[0] = jnp.concatenate([o[i * rows:(i + 1) * rows] for i in range(CHUNK // rows)], axis=-1)


def gla_scan_tc(qk_c, v_c, lr_c, qk_l, v_l, lr_l, w_la, b_la, rev):
    b, l, _ = qk_l.shape
    rows = l // GRID_W
    cols = CHUNK // rows
    n_ctx, n_lat = qk_c.shape[1] // CHUNK, l // CHUNK
    dirn = 1 if rev else 0
    cc, lc = _scan_chunks(rev, n_ctx, n_lat)
    hv = GLA_HEADS * GLA_DV
    ctx_blk = lambda a: pl.BlockSpec((1, CHUNK, a.shape[2]), lambda bi, j: (bi, cc(j), 0))
    lat_blk = lambda n: pl.BlockSpec((1, rows, cols * n), lambda bi, j: (bi, 0, lc(j)))
    grid_view = lambda a: a.reshape(b, rows, GRID_W * a.shape[2])
    out = pl.pallas_call(
        functools.partial(_gla_scan_kernel, rev, dirn, n_ctx), grid=(b, n_ctx + n_lat),
        in_specs=[ctx_blk(qk_c), ctx_blk(v_c), ctx_blk(lr_c),
                  lat_blk(qk_l.shape[2]), lat_blk(v_l.shape[2]), lat_blk(lr_l.shape[2]),
                  pl.BlockSpec((1,) + w_la.shape[1:], lambda bi, j: (dirn, 0, 0)),
                  pl.BlockSpec((1,) + b_la.shape[1:], lambda bi, j: (dirn, 0, 0))],
        out_specs=lat_blk(hv),
        out_shape=jax.ShapeDtypeStruct((b, rows, GRID_W * hv), F32),
        scratch_shapes=[pltpu.VMEM((GLA_HEADS, GLA_DV, GLA_DK), F32)],
        compiler_params=_cparams("parallel", "arbitrary"), name="gla_scan_bwd" if rev else "gla_scan_fwd",
    )(qk_c, v_c, lr_c, grid_view(qk_l), grid_view(v_l), grid_view(lr_l), w_la, b_la)
    return out.reshape(b, l, hv)


def _head_norm_gate(o, gate, g, n_heads, dv):
    parts = []
    for h in range(n_heads):
        oh = o[:, h * dv:(h + 1) * dv]
        gh = gate[:, h * dv:(h + 1) * dv]
        yh = oh * lax.rsqrt(jnp.mean(oh * oh, axis=-1, keepdims=True) + EPS) * g
        parts.append(yh * (gh * jax.nn.sigmoid(gh)))
    return parts


def _mix_out_kernel(x_ref, dnf_ref, dnb_ref, z_ref, glf_ref, glb_ref, r_ref, mod_ref, dng_ref,
                    glg_ref, n2g_ref, wout_ref, wq_ref, x1_ref, h2_ref, q_ref):
    parts = (_head_norm_gate(dnf_ref[0] + dnb_ref[0], z_ref[0], dng_ref[...], DN_HEADS, DN_DV)
             + _head_norm_gate(glf_ref[0] + glb_ref[0], r_ref[0], glg_ref[...], GLA_HEADS, GLA_DV))
    y = jnp.dot(jnp.concatenate(parts, axis=-1).astype(BF16), wout_ref[...], preferred_element_type=F32)
    x1 = x_ref[0] + mod_ref[0, 2:3, :] * y
    x1_ref[0] = x1
    n = x1 * lax.rsqrt(jnp.mean(x1 * x1, axis=-1, keepdims=True) + EPS) * n2g_ref[...]
    h2 = n * (1.0 + mod_ref[0, 4:5, :]) + mod_ref[0, 3:4, :]
    h2_ref[0] = h2
    q_ref[0] = jnp.dot(h2.astype(BF16), wq_ref[...], preferred_element_type=F32)


def mix_out_tc(x, dn_f, dn_b, z, gl_f, gl_b, r, mod_l, dn_g, gla_g, n2_g, w_out, w_q):
    b, l, d = x.shape
    tok = lambda n: pl.BlockSpec((1, TOK_TILE, n), lambda bi, i: (bi, i, 0))
    full = lambda a: pl.BlockSpec(a.shape, lambda bi, i: (0,) * a.ndim)
    dn_g, gla_g, n2_g = dn_g.reshape(1, -1), gla_g.reshape(1, -1), n2_g.reshape(1, -1)
    nq = w_q.shape[1]
    return pl.pallas_call(
        _mix_out_kernel, grid=(b, l // TOK_TILE),
        in_specs=[tok(d), tok(dn_f.shape[2]), tok(dn_b.shape[2]), tok(z.shape[2]),
                  tok(gl_f.shape[2]), tok(gl_b.shape[2]), tok(r.shape[2]),
                  pl.BlockSpec((1,) + mod_l.shape[1:], lambda bi, i: (bi, 0, 0)),
                  full(dn_g), full(gla_g), full(n2_g), full(w_out), full(w_q)],
        out_specs=[tok(d), tok(d), tok(nq)],
        out_shape=[jax.ShapeDtypeStruct((b, l, d), F32), jax.ShapeDtypeStruct((b, l, d), F32),
                   jax.ShapeDtypeStruct((b, l, nq), F32)],
        compiler_params=_cparams("parallel", "arbitrary"), name="mix_out",
    )(x, dn_f, dn_b, z, gl_f, gl_b, r, mod_l, dn_g, gla_g, n2_g, w_out, w_q)


def _top_rows(s, k, payload=None):
    n = s.shape[0]
    row = lax.broadcasted_iota(jnp.int32, s.shape, 0).astype(F32)
    vals, picked = [], []
    for _ in range(k):
        m = jnp.max(s, axis=0, keepdims=True)
        first = jnp.min(jnp.where(s == m, row, float(n)), axis=0, keepdims=True)
        sel = row == first
        vals.append(m)
        if payload is None:
            picked.append(first)
        else:
            picked.append(jnp.max(jnp.where(sel, payload, -1.0), axis=0, keepdims=True))
        s = jnp.where(sel, -jnp.inf, s)
    return jnp.concatenate(vals, axis=0), jnp.concatenate(picked, axis=0)


def _candidate_rows(s0, i0, s1, i1):
    k = s0.shape[0]
    wide = SUBLANES
    blocks_s = [s0[0:1] + s1]
    blocks_i = [i0[0:1] * float(PEER_NKEYS) + i1]
    col = lax.broadcasted_iota(jnp.int32, (wide, s0.shape[1]), 0)
    for i in range(1, wide):
        keep = col < (k // (i + 1))
        blocks_s.append(jnp.where(keep, s0[i:i + 1] + s1[0:wide], -jnp.inf))
        blocks_i.append(i0[i:i + 1] * float(PEER_NKEYS) + i1[0:wide])
    blocks_s.append(s0[wide:k] + s1[0:1])
    blocks_i.append(i0[wide:k] * float(PEER_NKEYS) + i1[0:1])
    return jnp.concatenate(blocks_s, axis=0), jnp.concatenate(blocks_i, axis=0)


def _select_kernel(q_ref, k_ref, idx_ref, gate_ref, idx_s, gate_s):
    half = PEER_DQ // 2

    def head(h, carry):
        tops = []
        for p in range(2):
            qp = q_ref[:, pl.ds(pl.multiple_of(h * PEER_DQ + p * half, half), half)]
            s = lax.dot_general(k_ref[h, p], qp, (((1,), (1,)), ((), ())),
                                preferred_element_type=F32,
                                precision=lax.Precision.HIGHEST)
            tops.append(_top_rows(s, PEER_TOPK))
        (s0, i0), (s1, i1) = tops
        cand_s, cand_i = _candidate_rows(s0, i0, s1, i1)
        best_s, idx = _top_rows(cand_s, PEER_TOPK, payload=cand_i)
        e = jnp.exp(best_s - best_s[0:1])
        r0 = pl.multiple_of(h * PEER_TOPK, PEER_TOPK)
        idx_s[pl.ds(r0, PEER_TOPK), :] = idx
        gate_s[pl.ds(r0, PEER_TOPK), :] = e / jnp.sum(e, axis=0, keepdims=True)
        return carry

    lax.fori_loop(0, PEER_HEADS, head, 0)
    idx_ref[...] = idx_s[...].T.astype(jnp.int32)
    gate_ref[...] = gate_s[...].T


def peer_select_tc(q, keys):
    n_tok = q.shape[0]
    out_spec = pl.BlockSpec((SELECT_TILE, PICKS), lambda i: (i, 0))
    return pl.pallas_call(
        _select_kernel,
        grid=(n_tok // SELECT_TILE,),
        in_specs=[pl.BlockSpec((SELECT_TILE, q.shape[1]), lambda i: (i, 0)),
                  pl.BlockSpec(keys.shape, lambda i: (0, 0, 0, 0))],
        out_specs=[out_spec, out_spec],
        out_shape=[jax.ShapeDtypeStruct((n_tok, PICKS), jnp.int32),
                   jax.ShapeDtypeStruct((n_tok, PICKS), F32)],
        scratch_shapes=[pltpu.VMEM((PICKS, SELECT_TILE), F32), pltpu.VMEM((PICKS, SELECT_TILE), F32)],
        compiler_params=_cparams("parallel"), name="peer_select",
    )(q, keys)


def _sc_mesh():
    return plsc.VectorSubcoreMesh(core_axis_name="c", subcore_axis_name="s")


def _sc_pipeline(body, n_steps, in_specs, out_specs, operands):
    pltpu.emit_pipeline(
        body, grid=(n_steps,), in_specs=in_specs, out_specs=out_specs,
        core_axis_name=("c", "s"), dimension_semantics=(pltpu.PARALLEL,),
        trace_scopes=False,
    )(*operands)


def pack_table(t):
    half = t.shape[1] // 2
    bits = lax.bitcast_convert_type(t, jnp.uint32)
    lo_bits, hi_bits = bits[:, :half], bits[:, half:]
    low = (lo_bits + jnp.uint32(0x7FFF) + ((lo_bits >> 16) & jnp.uint32(1))) >> 16
    mag = hi_bits & jnp.uint32(0x7FFFFFFF)
    top = jnp.where(mag >= jnp.uint32(0x10000),
                    (mag - low + jnp.uint32(0x8000)) & jnp.uint32(0xFFFF0000), jnp.uint32(0))
    word = (hi_bits & jnp.uint32(0x80000000)) | top | low
    return lax.bitcast_convert_type(word, jnp.int32)


def _unpack(w):
    return plsc.bitcast(w << 16, F32), plsc.bitcast(w, F32)


def peer_act_partial_sc(u_p, idx2, h):
    n_groups, (n_tok, d) = idx2.shape[0], h.shape
    half = d // 2
    nsub = TOK_STEP * GROUPS_PER_TOK
    n_chunks = half // SC_LANES

    @functools.partial(
        pl.kernel, mesh=_sc_mesh(), compiler_params=pltpu.CompilerParams(needs_layout_passes=False),
        out_type=jax.ShapeDtypeStruct((n_groups, PICK_GROUP * SC_LANES), F32),
        scratch_types=[pltpu.VMEM((2, PICK_GROUP, half), jnp.int32),
                       pltpu.SemaphoreType.DMA((2,))],
    )
    def k(u_hbm, i_hbm, h_hbm, o_hbm, rows, sems):
        def body(i_v, h_v, o_v):
            def fetch(j, slot):
                return pltpu.make_async_copy(u_hbm.at[i_v.at[j]], rows.at[slot], sems.at[slot])

            fetch(0, 0).start()

            def sub(j, carry):
                slot = j % 2

                @pl.when(j + 1 < nsub)
                def _():
                    fetch(j + 1, 1 - slot).start()

                fetch(j, slot).wait()
                t = j // GROUPS_PER_TOK

                def picks(g, carry2):
                    kb = g * ACT_UNROLL
                    accs = [None] * ACT_UNROLL
                    for c in range(n_chunks):
                        h_lo = h_v[t, pl.ds(c * SC_LANES, SC_LANES)]
                        h_hi = h_v[t, pl.ds(half + c * SC_LANES, SC_LANES)]
                        for i in range(ACT_UNROLL):
                            lo, hi = _unpack(rows[slot, kb + i, pl.ds(c * SC_LANES, SC_LANES)])
                            p = lo * h_lo + hi * h_hi
                            accs[i] = p if accs[i] is None else accs[i] + p
                    for i in range(ACT_UNROLL):
                        o_v[j, pl.ds((kb + i) * SC_LANES, SC_LANES)] = accs[i]
                    return carry2

                lax.fori_loop(0, PICK_GROUP // ACT_UNROLL, picks, 0)
                return carry

            lax.fori_loop(0, nsub, sub, 0)

        _sc_pipeline(
            body, n_tok // TOK_STEP,
            [pl.BlockSpec((nsub, PICK_GROUP), lambda i: (i, 0)),
             pl.BlockSpec((TOK_STEP, d), lambda i: (i, 0))],
            [pl.BlockSpec((nsub, PICK_GROUP * SC_LANES), lambda i: (i, 0))],
            (i_hbm, h_hbm, o_hbm))

    return k(u_p, idx2, h)


def peer_combine_sc(v_p, idx2, coef_b):
    n_groups = idx2.shape[0]
    half = v_p.shape[1]
    n_tok = n_groups // GROUPS_PER_TOK
    nsub = TOK_STEP * GROUPS_PER_TOK
    n_chunks = half // SC_LANES

    @functools.partial(
        pl.kernel, mesh=_sc_mesh(), compiler_params=pltpu.CompilerParams(needs_layout_passes=False),
        out_type=jax.ShapeDtypeStruct((n_tok, 2 * half), F32),
        scratch_types=[pltpu.VMEM((2, PICK_GROUP, half), jnp.int32),
                       pltpu.SemaphoreType.DMA((2,))],
    )
    def k(v_hbm, i_hbm, c_hbm, o_hbm, rows, sems):
        def body(i_v, c_v, o_v):
            def fetch(j, slot):
                return pltpu.make_async_copy(v_hbm.at[i_v.at[j]], rows.at[slot], sems.at[slot])

            fetch(0, 0).start()
            for j in range(nsub):
                slot = j % 2
                if j + 1 < nsub:
                    fetch(j + 1, 1 - slot).start()
                fetch(j, slot).wait()
                t = j // GROUPS_PER_TOK
                first = j % GROUPS_PER_TOK == 0
                cks = [c_v[j, pl.ds(kk * SC_LANES, SC_LANES)] for kk in range(PICK_GROUP)]

                def chunk(c, slot=slot, t=t, first=first, cks=cks):
                    l = c * SC_LANES
                    ways = 4
                    los, his = [None] * ways, [None] * ways
                    for kk in range(PICK_GROUP):
                        lo, hi = _unpack(rows[slot, kk, pl.ds(l, SC_LANES)])
                        a, b_ = cks[kk] * lo, cks[kk] * hi
                        w = kk % ways
                        los[w] = a if los[w] is None else los[w] + a
                        his[w] = b_ if his[w] is None else his[w] + b_
                    tot_lo = (los[0] + los[1]) + (los[2] + los[3])
                    tot_hi = (his[0] + his[1]) + (his[2] + his[3])
                    if not first:
                        tot_lo = tot_lo + o_v[t, pl.ds(l, SC_LANES)]
                        tot_hi = tot_hi + o_v[t, pl.ds(half + l, SC_LANES)]
                    o_v[t, pl.ds(l, SC_LANES)] = tot_lo
                    o_v[t, pl.ds(half + l, SC_LANES)] = tot_hi

                plsc.parallel_loop(0, n_chunks, 1, unroll=2)(chunk)

        _sc_pipeline(
            body, n_tok // TOK_STEP,
            [pl.BlockSpec((nsub, PICK_GROUP), lambda i: (i, 0)),
             pl.BlockSpec((nsub, PICK_GROUP * SC_LANES), lambda i: (i, 0))],
            [pl.BlockSpec((TOK_STEP, 2 * half), lambda i: (i, 0))],
            (i_hbm, c_hbm, o_hbm))

    return k(v_p, idx2, coef_b)


def _segment_matrix():
    r = lax.broadcasted_iota(jnp.int32, (PICK_GROUP * SC_LANES, PICK_GROUP), 0) // SC_LANES
    c = lax.broadcasted_iota(jnp.int32, (PICK_GROUP * SC_LANES, PICK_GROUP), 1)
    return (r == c).astype(F32)


def _coef_kernel(part_ref, gate_ref, o_ref):
    seg = _segment_matrix()
    act = jnp.dot(part_ref[...], seg, preferred_element_type=F32, precision=lax.Precision.HIGHEST)
    coef = gate_ref[...] * (0.5 * act * (1.0 + lax.erf(act * (2.0 ** -0.5))))
    o_ref[...] = lax.dot_general(coef, seg, (((1,), (1,)), ((), ())), preferred_element_type=F32,
                                 precision=lax.Precision.HIGHEST)


def peer_coef_tc(part, gate2):
    n_groups, width = part.shape
    tile = 1024
    return pl.pallas_call(
        _coef_kernel,
        grid=(n_groups // tile,),
        in_specs=[pl.BlockSpec((tile, width), lambda i: (i, 0)),
                  pl.BlockSpec((tile, PICK_GROUP), lambda i: (i, 0))],
        out_specs=pl.BlockSpec((tile, width), lambda i: (i, 0)),
        out_shape=jax.ShapeDtypeStruct((n_groups, width), F32),
        compiler_params=_cparams("parallel"), name="peer_coef",
    )(part, gate2)


def _final_kernel(x_ref, y_ref, mod_ref, g_ref, o_ref):
    x = x_ref[0] + mod_ref[0, 5:6, :] * y_ref[0]
    o_ref[0] = x * lax.rsqrt(jnp.mean(x * x, axis=-1, keepdims=True) + EPS) * g_ref[...]


def final_tc(x1, y, mod_l, final_g):
    b, l, d = x1.shape
    tok = pl.BlockSpec((1, TOK_TILE, d), lambda bi, i: (bi, i, 0))
    return pl.pallas_call(
        _final_kernel, grid=(b, l // TOK_TILE),
        in_specs=[tok, tok, pl.BlockSpec((1,) + mod_l.shape[1:], lambda bi, i: (bi, 0, 0)),
                  pl.BlockSpec((1, d), lambda bi, i: (0, 0))],
        out_specs=tok, out_shape=jax.ShapeDtypeStruct((b, l, d), F32),
        compiler_params=_cparams("parallel", "arbitrary"), name="final_norm",
    )(x1, y, mod_l, final_g.reshape(1, d))


def _pad_cols(w, n):
    return jnp.pad(w, ((0, 0), (0, n - w.shape[1])))


def _exact_zero(v):
    return jnp.minimum(jnp.abs(v), 0.0)


def forward(x, c, ctx, c_ctx, w_ada, b_ada, norm1_g, norm2_g, w_in, conv_w, dn_a_log,
            dn_dt_bias, dn_norm_g, gla_wa2, gla_ba, gla_norm_g, w_out, peer_wq, peer_keys,
            peer_u, peer_v, final_g):
    b, l, d = x.shape
    c_all = jnp.concatenate([c, c_ctx[None]], axis=0)
    c_all = jnp.pad(c_all, ((0, (-c_all.shape[0]) % SUBLANES), (0, 0)))
    mod = adaln_mod(c_all, w_ada, b_ada)
    mod_l = mod[:b].reshape(b, 6, d)
    mod_c = jnp.broadcast_to(mod[b].reshape(1, 6, d), (b, 6, d))
    o = DN_QKV
    hv = DN_HEADS * DN_DV
    w_dn_qkv, w_dn_z = w_in[:, :o], w_in[:, o:o + hv]
    w_dn_ba = _pad_cols(w_in[:, o + hv:DN_COLS], LANES)
    g0 = DN_COLS
    gqk, gv = 2 * GLA_HEADS * GLA_DK, GLA_HEADS * GLA_DV
    w_gl_qk, w_gl_v = w_in[:, g0:g0 + gqk], w_in[:, g0 + gqk:g0 + gqk + gv]
    w_gl_r = w_in[:, g0 + gqk + gv:g0 + gqk + 2 * gv]
    w_gl_lr = _pad_cols(w_in[:, g0 + gqk + 2 * gv:], LANES)
    w_lat = [w.astype(BF16) for w in (w_dn_qkv, w_dn_ba, w_gl_qk, w_gl_v, w_gl_lr, w_dn_z, w_gl_r)]
    w_ctx = w_lat[:5]
    w_la = jnp.zeros((2, LANES, GLA_HEADS * GLA_DK), F32)
    for dd in range(2):
        w_la = w_la.at[dd, dd * GLA_LR:(dd + 1) * GLA_LR].set(gla_wa2[dd])
    w_la = w_la.astype(BF16)
    b_la = gla_ba.reshape(2, 1, GLA_HEADS * GLA_DK)
    w_out_b, w_q_b = w_out.astype(BF16), peer_wq.astype(BF16)
    u_p, v_p = pack_table(peer_u), pack_table(peer_v)

    bg = b // BATCH_GROUPS
    n_tok = bg * l

    def mixer(xg, ctxg, mod_cg, mod_g, after_select, after_combine):
        mod_cg, mod_g = mod_cg + after_select, mod_g + after_select
        c_qkv, c_ba, c_qk, c_v, c_lr = in_projection(ctxg, norm1_g, mod_cg[:, 0:2], w_ctx)
        l_qkv, l_ba, l_qk, l_v, l_lr, l_z, l_r = in_projection(xg, norm1_g, mod_g[:, 0:2], w_lat)
        feat_c, feat_l = dn_features_tc(c_qkv, conv_w), dn_features_tc(l_qkv, conv_w)
        dn_f = dn_scan_tc(feat_c, feat_l, c_ba, l_ba, dn_a_log, dn_dt_bias, rev=False)
        dn_b = dn_scan_tc(feat_c, feat_l, c_ba, l_ba, dn_a_log, dn_dt_bias, rev=True)
        gl_f = gla_scan_tc(c_qk, c_v, c_lr, l_qk, l_v, l_lr, w_la, b_la, rev=False)
        gl_b = gla_scan_tc(c_qk, c_v, c_lr, l_qk, l_v, l_lr, w_la, b_la, rev=True)
        return mix_out_tc(xg, dn_f, dn_b, l_z, gl_f, gl_b, l_r, mod_g, dn_norm_g + after_combine,
                          gla_norm_g, norm2_g, w_out_b, w_q_b)

    def select_and_act(h2, q, after_coef):
        idx, gate = peer_select_tc(q.reshape(n_tok, -1), peer_keys + after_coef)
        idx2 = idx.reshape(n_tok * GROUPS_PER_TOK, PICK_GROUP)
        gate2 = gate.reshape(n_tok * GROUPS_PER_TOK, PICK_GROUP)
        return idx2, gate2, peer_act_partial_sc(u_p, idx2, h2.reshape(n_tok, d))

    def coef_and_combine(st, after_mixer):
        coef_b = peer_coef_tc(st["part"], st["gate2"] + after_mixer)
        st["y"] = peer_combine_sc(v_p, st["idx2"], coef_b)
        return _exact_zero(coef_b[0, 0])

    zero = jnp.zeros((), F32)
    slices, z_sel, z_coef = [], zero, zero
    for i in range(0, b, bg):
        g = len(slices)
        z_comb = _exact_zero(slices[g - 2]["y"][0, 0]) if g >= 2 else zero
        mod_g = mod_l[i:i + bg]
        x1, h2, q = mixer(x[i:i + bg], ctx[i:i + bg], mod_c[i:i + bg], mod_g, z_sel, z_comb)
        if g >= 1:
            z_coef = coef_and_combine(slices[g - 1], _exact_zero(x1[0, 0, 0]))
        idx2, gate2, part = select_and_act(h2, q, z_coef)
        z_sel = _exact_zero(gate2[0, 0])
        slices.append(dict(x1=x1, mod=mod_g, idx2=idx2, gate2=gate2, part=part))
    coef_and_combine(slices[-1], zero)
    outs = [final_tc(st["x1"], st["y"].reshape(bg, l, d), st["mod"], final_g) for st in slices]
    return jnp.concatenate(outs, axis=0)


def kernel(x, c, ctx, c_ctx, w_ada, b_ada, norm1_g, norm2_g, w_in, conv_w, dn_a_log,
           dn_dt_bias, dn_norm_g, gla_wa2, gla_ba, gla_norm_g, w_out, peer_wq, peer_keys,
           peer_u, peer_v, final_g):
    assert w_ada.shape[0] == 1, "single-layer block: the context stream is only consumed, never updated"
    return forward(x, c, ctx, c_ctx, w_ada[0], b_ada[0], norm1_g[0], norm2_g[0], w_in[0], conv_w[0],
                   dn_a_log[0], dn_dt_bias[0], dn_norm_g[0], gla_wa2[0], gla_ba[0], gla_norm_g[0],
                   w_out[0], peer_wq[0], peer_keys[0], peer_u[0], peer_v[0], final_g)
```

```python
import functools

import jax
import jax.numpy as jnp
from jax import lax
from jax.experimental import pallas as pl
from jax.experimental.pallas import tpu as pltpu
from jax.experimental.pallas import tpu_sc as plsc

GRID_W = 64
DN_HEADS = 4
DN_DK = 128
DN_DV = 128
CONV_W = 5
GLA_HEADS = 4
GLA_DK = 64
GLA_DV = 128
GLA_LR = 16
GLA_TAU = 16.0
CHUNK = 64
PEER_HEADS = 8
PEER_NKEYS = 128
PEER_DQ = 256
PEER_TOPK = 16
EPS = 1e-6
DN_QKV = 2 * DN_HEADS * DN_DK + DN_HEADS * DN_DV
DN_COLS = DN_QKV + DN_HEADS * DN_DV + 4 * DN_HEADS

SUBLANES = 8
LANES = 128
SC_LANES = 16
VMEM_LIMIT_BYTES = 48 * 1024 * 1024

TOK_TILE = 256
SELECT_TILE = 256
SCAN_BATCH = 2
GLA_SUB = 16
PICK_GROUP = 32
TOK_STEP = 2
COMB_CHUNKS = 2
ACT_UNROLL = 16
BATCH_GROUPS = 8
PICKS = PEER_HEADS * PEER_TOPK
GROUPS_PER_TOK = PICKS // PICK_GROUP

F32 = jnp.float32
BF16 = jnp.bfloat16


def _cparams(*semantics):
    return pltpu.CompilerParams(dimension_semantics=semantics, vmem_limit_bytes=VMEM_LIMIT_BYTES)


def _dot(a, b):
    return jnp.dot(a.astype(BF16), b.astype(BF16), preferred_element_type=F32)


def _dot_nt(a, b):
    return lax.dot_general(a.astype(BF16), b.astype(BF16), (((1,), (1,)), ((), ())),
                           preferred_element_type=F32)


def _dot_tn(a, b):
    return lax.dot_general(a.astype(BF16), b.astype(BF16), (((0,), (0,)), ((), ())),
                           preferred_element_type=F32)


def _split(x):
    hi = x.astype(BF16)
    return hi, (x - hi.astype(F32)).astype(BF16)


def _mask_dot(mask_bf16, x):
    hi, lo = _split(x)
    return (jnp.dot(mask_bf16, hi, preferred_element_type=F32)
            + jnp.dot(mask_bf16, lo, preferred_element_type=F32))


def _softplus(x):
    return jnp.maximum(x, 0.0) + jnp.log(1.0 + jnp.exp(-jnp.abs(x)))


def _tri_masks(rev):
    r = lax.broadcasted_iota(jnp.int32, (CHUNK, CHUNK), 0)
    c = lax.broadcasted_iota(jnp.int32, (CHUNK, CHUNK), 1)
    d = (c - r) if rev else (r - c)
    return d >= 0, d > 0


def _mod_kernel(c_ref, w_ref, b_ref, o_ref):
    c = c_ref[...]
    s = c * jax.nn.sigmoid(c)
    o_ref[...] = jnp.dot(s, w_ref[...], preferred_element_type=F32,
                         precision=lax.Precision.HIGHEST) + b_ref[...]


def adaln_mod(c_all, w_ada, b_ada):
    r, d = c_all.shape
    n = w_ada.shape[1]
    tn = 512
    return pl.pallas_call(
        _mod_kernel, grid=(n // tn,),
        in_specs=[pl.BlockSpec((r, d), lambda j: (0, 0)),
                  pl.BlockSpec((d, tn), lambda j: (0, j)),
                  pl.BlockSpec((1, tn), lambda j: (0, j))],
        out_specs=pl.BlockSpec((r, tn), lambda j: (0, j)),
        out_shape=jax.ShapeDtypeStruct((r, n), F32),
        compiler_params=_cparams("arbitrary"), name="adaln_mod",
    )(c_all, w_ada, b_ada.reshape(1, n))


def _inproj_kernel(x_ref, g_ref, mod_ref, *refs):
    n_out = len(refs) // 2
    x = x_ref[0]
    y = x * lax.rsqrt(jnp.mean(x * x, axis=-1, keepdims=True) + EPS) * g_ref[...]
    h = (y * (1.0 + mod_ref[0, 1:2, :]) + mod_ref[0, 0:1, :]).astype(BF16)
    for w_ref, o_ref in zip(refs[:n_out], refs[n_out:]):
        o_ref[0] = jnp.dot(h, w_ref[...], preferred_element_type=F32)


def in_projection(x, norm_g, mod, weights):
    b, l, d = x.shape
    w_specs = [pl.BlockSpec(w.shape, lambda bi, i: (0, 0)) for w in weights]
    o_specs = [pl.BlockSpec((1, TOK_TILE, w.shape[1]), lambda bi, i: (bi, i, 0)) for w in weights]
    return pl.pallas_call(
        _inproj_kernel, grid=(b, l // TOK_TILE),
        in_specs=[pl.BlockSpec((1, TOK_TILE, d), lambda bi, i: (bi, i, 0)),
                  pl.BlockSpec((1, d), lambda bi, i: (0, 0)),
                  pl.BlockSpec((1, 2, d), lambda bi, i: (bi, 0, 0))] + w_specs,
        out_specs=o_specs,
        out_shape=[jax.ShapeDtypeStruct((b, l, w.shape[1]), F32) for w in weights],
        compiler_params=_cparams("parallel", "arbitrary"), name="in_projection",
    )(x, norm_g.reshape(1, d), mod, *weights)


def _dn_feature_kernel(x_ref, w_ref, o_ref):
    x = x_ref[0]
    n = x.shape[0]
    t = lax.broadcasted_iota(jnp.int32, (n, 1), 0)
    pad = CONV_W // 2
    acc = w_ref[0, pad:pad + 1, :] * x
    for j in range(CONV_W):
        s = j - pad
        if s == 0:
            continue
        xs = pltpu.roll(x, (-s) % n, axis=0)
        bad = (t < -s) if s < 0 else (t >= n - s)
        acc = acc + w_ref[0, j:j + 1, :] * jnp.where(bad, 0.0, xs)
    y = acc * jax.nn.sigmoid(acc)
    kind = pl.program_id(1) // DN_HEADS
    inv = lax.rsqrt(jnp.sum(y * y, axis=-1, keepdims=True) + EPS)
    scale = jnp.where(kind == 0, inv * DN_DK ** -0.5, jnp.where(kind == 1, inv, 1.0))
    o_ref[0] = y * scale


def dn_features_tc(qkv, conv_w):
    b, l, n = qkv.shape
    nblk = n // LANES
    w = jnp.zeros((nblk, SUBLANES, LANES), F32).at[:, :CONV_W].set(
        conv_w.reshape(CONV_W, nblk, LANES).transpose(1, 0, 2))
    return pl.pallas_call(
        _dn_feature_kernel, grid=(b, nblk),
        in_specs=[pl.BlockSpec((1, l, LANES), lambda bi, j: (bi, 0, j)),
                  pl.BlockSpec((1, SUBLANES, LANES), lambda bi, j: (j, 0, 0))],
        out_specs=pl.BlockSpec((1, l, LANES), lambda bi, j: (bi, 0, j)),
        out_shape=jax.ShapeDtypeStruct((b, l, n), F32),
        compiler_params=_cparams("parallel", "arbitrary"), name="dn_features",
    )(qkv, w)


def _scan_chunks(rev, n_ctx, n_lat):
    if rev:
        ctx = lambda j: jnp.maximum(n_ctx - 1 - j, 0)
        lat = lambda j: jnp.where(j < n_ctx, n_lat - 1, n_lat - 1 - (j - n_ctx))
    else:
        ctx = lambda j: jnp.minimum(j, n_ctx - 1)
        lat = lambda j: jnp.maximum(j - n_ctx, 0)
    return ctx, lat


def _dn_scan_kernel(rev, dirn, n_ctx, alog_ref, dtb_ref, fc_ref, fl_ref, bac_ref, bal_ref, o_ref, s_ref):
    step = pl.program_id(1)

    @pl.when(step == 0)
    def _():
        s_ref[...] = jnp.zeros_like(s_ref)

    in_ctx = step < n_ctx
    f = jnp.where(in_ctx, fc_ref[0], fl_ref[0])
    ba = jnp.where(in_ctx, bac_ref[0], bal_ref[0])
    nh, hd, n = DN_HEADS, DN_HEADS * DN_DK, DN_HEADS * CHUNK
    stack = lambda base, w: jnp.concatenate([f[:, base + h * w:base + (h + 1) * w] for h in range(nh)], axis=0)
    q_s, k_s, v_s = stack(0, DN_DK), stack(hd, DN_DK), stack(2 * hd, DN_DV)
    incl, _ = _tri_masks(rev)
    beta_all = jax.nn.sigmoid(ba)
    g_all = -jnp.exp(alog_ref[...]) * _softplus(ba + dtb_ref[...])
    gc_all = _mask_dot(incl.astype(BF16), g_all)
    gc_t = jnp.concatenate([gc_all, gc_all], axis=0).T
    g_tot = jnp.sum(g_all, axis=0, keepdims=True)
    cb = [dirn * nh + h for h in range(nh)]
    cg = [2 * nh + c for c in cb]
    col = lambda a, cs: jnp.concatenate([a[:, c:c + 1] for c in cs], axis=0)
    beta_c, gc_c = col(beta_all, cb), col(gc_all, cg)
    gtot_c = jnp.concatenate([jnp.broadcast_to(g_tot[:, c:c + 1], (CHUNK, 1)) for c in cg], axis=0)
    gc_r = jnp.concatenate([gc_t[c:c + 1, :CHUNK] for c in cg], axis=1)
    r = lax.broadcasted_iota(jnp.int32, (n, n), 0)
    c = lax.broadcasted_iota(jnp.int32, (n, n), 1)
    same = (r // CHUNK) == (c // CHUNK)
    d = (c - r) if rev else (r - c)
    incl_bd, strict_bd = same & (d >= 0), same & (d > 0)
    eye = (r == c).astype(F32)
    decay = jnp.where(incl_bd, jnp.exp(jnp.where(incl_bd, gc_c - gc_r, 0.0)), 0.0)
    kb_s = k_s * beta_c
    lower = jnp.where(strict_bd, _dot_nt(kb_s, k_s) * decay, 0.0)
    eg_c = jnp.exp(gc_c)
    inv = eye - lower
    pw = lower
    for _ in range(5):
        pw = _dot(pw, pw)
        inv = inv + _dot(inv, pw)
    sol = _dot(inv, jnp.concatenate([v_s * beta_c, kb_s * eg_c], axis=-1))
    u_s, w_s = sol[:, :DN_DV], sol[:, DN_DV:]
    k_dec = k_s * jnp.exp(gtot_c - gc_c)
    rb = lax.broadcasted_iota(jnp.int32, (n, DN_DK), 0) // CHUNK
    expand = lambda x: jnp.concatenate([jnp.where(rb == h, x, 0.0) for h in range(nh)], axis=1)
    s = s_ref[...]
    v_new = u_s - _dot(expand(w_s), s)
    a_qk = _dot_nt(q_s, k_s) * decay
    o_s = _dot(expand(q_s * eg_c), s) + _dot(a_qk, v_new)
    gl_rows = jnp.concatenate([jnp.broadcast_to(jnp.exp(g_tot[:, cc:cc + 1]), (DN_DK, 1)) for cc in cg], axis=0)
    s_ref[...] = s * gl_rows + _dot_tn(expand(k_dec), v_new)
    o_ref[0] = jnp.concatenate([o_s[h * CHUNK:(h + 1) * CHUNK] for h in range(nh)], axis=1)


def dn_scan_tc(feat_c, feat_l, ba_c, ba_l, a_log, dt_bias, rev):
    b, l, nf = feat_l.shape
    n_ctx, n_lat = feat_c.shape[1] // CHUNK, l // CHUNK
    dirn = 1 if rev else 0
    cc, lc = _scan_chunks(rev, n_ctx, n_lat)
    lanes = lambda p: jnp.zeros((1, LANES), F32).at[0, 2 * DN_HEADS:4 * DN_HEADS].set(p.reshape(-1))
    vec = pl.BlockSpec((1, LANES), lambda bi, j: (0, 0))
    return pl.pallas_call(
        functools.partial(_dn_scan_kernel, rev, dirn, n_ctx), grid=(b, n_ctx + n_lat),
        in_specs=[vec, vec,
                  pl.BlockSpec((1, CHUNK, nf), lambda bi, j: (bi, cc(j), 0)),
                  pl.BlockSpec((1, CHUNK, nf), lambda bi, j: (bi, lc(j), 0)),
                  pl.BlockSpec((1, CHUNK, LANES), lambda bi, j: (bi, cc(j), 0)),
                  pl.BlockSpec((1, CHUNK, LANES), lambda bi, j: (bi, lc(j), 0))],
        out_specs=pl.BlockSpec((1, CHUNK, DN_HEADS * DN_DV), lambda bi, j: (bi, lc(j), 0)),
        out_shape=jax.ShapeDtypeStruct((b, l, DN_HEADS * DN_DV), F32),
        scratch_shapes=[pltpu.VMEM((DN_HEADS * DN_DK, DN_DV), F32)],
        compiler_params=_cparams("parallel", "arbitrary"), name="dn_scan_bwd" if rev else "dn_scan_fwd",
    )(lanes(a_log), lanes(dt_bias), feat_c, feat_l, ba_c, ba_l)


def _from_grid_cols(blk, n):
    cols = blk.shape[1] // n
    return jnp.concatenate([blk[:, i * n:(i + 1) * n] for i in range(cols)], axis=0)


def _gla_scan_kernel(rev, dirn, n_ctx, qkc_ref, vc_ref, lrc_ref, qkl_ref, vl_ref, lrl_ref,
                     wla_ref, bla_ref, o_ref, s_ref):
    step = pl.program_id(1)

    @pl.when(step == 0)
    def _():
        s_ref[...] = jnp.zeros_like(s_ref)

    in_ctx = step < n_ctx
    hk, hv = GLA_HEADS * GLA_DK, GLA_HEADS * GLA_DV
    qk = jnp.where(in_ctx, qkc_ref[0], _from_grid_cols(qkl_ref[0], 2 * hk))
    vv = jnp.where(in_ctx, vc_ref[0], _from_grid_cols(vl_ref[0], hv))
    lr = jnp.where(in_ctx, lrc_ref[0], _from_grid_cols(lrl_ref[0], LANES))
    incl, _ = _tri_masks(rev)
    incl_b = incl.astype(BF16)
    pre = _dot(lr, wla_ref[0]) + bla_ref[0]
    la_all = -_softplus(-pre) * (1.0 / GLA_TAU)
    bc_all = _mask_dot(incl_b, la_all)
    b_tot_all = jnp.sum(la_all, axis=0, keepdims=True)
    outs = []
    for h in range(GLA_HEADS):
        q = qk[:, h * GLA_DK:(h + 1) * GLA_DK] * GLA_DK ** -0.5
        k = qk[:, hk + h * GLA_DK:hk + (h + 1) * GLA_DK]
        v = vv[:, h * GLA_DV:(h + 1) * GLA_DV]
        bc = bc_all[:, h * GLA_DK:(h + 1) * GLA_DK]
        b_tot = b_tot_all[:, h * GLA_DK:(h + 1) * GLA_DK]
        st = s_ref[h]
        o = _dot_nt(q * jnp.exp(bc), st)
        parts = []
        for i in range(CHUNK // GLA_SUB):
            lo_r, hi_r = i * GLA_SUB, (i + 1) * GLA_SUB
            if rev:
                ref = bc[hi_r - 1:hi_r]
                c0, c1 = lo_r, CHUNK
            else:
                ref = bc[lo_r:lo_r + 1]
                c0, c1 = 0, hi_r
            qi = q[lo_r:hi_r] * jnp.exp(bc[lo_r:hi_r] - ref)
            ki = k[c0:c1] * jnp.exp(ref - bc[c0:c1])
            att = _dot_nt(qi, ki)
            rg = lax.broadcasted_iota(jnp.int32, (GLA_SUB, c1 - c0), 0) + lo_r
            cg = lax.broadcasted_iota(jnp.int32, (GLA_SUB, c1 - c0), 1) + c0
            keep = (cg >= rg) if rev else (cg <= rg)
            parts.append(_dot(jnp.where(keep, att, 0.0), v[c0:c1]))
        outs.append(o + jnp.concatenate(parts, axis=0))
        k_dec = k * jnp.exp(b_tot - bc)
        s_ref[h] = st * jnp.exp(b_tot) + _dot_tn(v, k_dec)
    o = jnp.concatenate(outs, axis=-1)
    rows = o_ref.shape[1]
    o_ref[0] = jnp.concatenate([o[i * rows:(i + 1) * rows] for i in range(CHUNK // rows)], axis=-1)


def gla_scan_tc(qk_c, v_c, lr_c, qk_l, v_l, lr_l, w_la, b_la, rev):
    b, l, _ = qk_l.shape
    rows = l // GRID_W
    cols = CHUNK // rows
    n_ctx, n_lat = qk_c.shape[1] // CHUNK, l // CHUNK
    dirn = 1 if rev else 0
    cc, lc = _scan_chunks(rev, n_ctx, n_lat)
    hv = GLA_HEADS * GLA_DV
    ctx_blk = lambda a: pl.BlockSpec((1, CHUNK, a.shape[2]), lambda bi, j: (bi, cc(j), 0))
    lat_blk = lambda n: pl.BlockSpec((1, rows, cols * n), lambda bi, j: (bi, 0, lc(j)))
    grid_view = lambda a: a.reshape(b, rows, GRID_W * a.shape[2])
    out = pl.pallas_call(
        functools.partial(_gla_scan_kernel, rev, dirn, n_ctx), grid=(b, n_ctx + n_lat),
        in_specs=[ctx_blk(qk_c), ctx_blk(v_c), ctx_blk(lr_c),
                  lat_blk(qk_l.shape[2]), lat_blk(v_l.shape[2]), lat_blk(lr_l.shape[2]),
                  pl.BlockSpec((1,) + w_la.shape[1:], lambda bi, j: (dirn, 0, 0)),
                  pl.BlockSpec((1,) + b_la.shape[1:], lambda bi, j: (dirn, 0, 0))],
        out_specs=lat_blk(hv),
        out_shape=jax.ShapeDtypeStruct((b, rows, GRID_W * hv), F32),
        scratch_shapes=[pltpu.VMEM((GLA_HEADS, GLA_DV, GLA_DK), F32)],
        compiler_params=_cparams("parallel", "arbitrary"), name="gla_scan_bwd" if rev else "gla_scan_fwd",
    )(qk_c, v_c, lr_c, grid_view(qk_l), grid_view(v_l), grid_view(lr_l), w_la, b_la)
    return out.reshape(b, l, hv)


def _head_norm_gate(o, gate, g, n_heads, dv):
    parts = []
    for h in range(n_heads):
        oh = o[:, h * dv:(h + 1) * dv]
        gh = gate[:, h * dv:(h + 1) * dv]
        yh = oh * lax.rsqrt(jnp.mean(oh * oh, axis=-1, keepdims=True) + EPS) * g
        parts.append(yh * (gh * jax.nn.sigmoid(gh)))
    return parts


def _mix_out_kernel(x_ref, dnf_ref, dnb_ref, z_ref, glf_ref, glb_ref, r_ref, mod_ref, dng_ref,
                    glg_ref, n2g_ref, wout_ref, wq_ref, x1_ref, h2_ref, q_ref):
    parts = (_head_norm_gate(dnf_ref[0] + dnb_ref[0], z_ref[0], dng_ref[...], DN_HEADS, DN_DV)
             + _head_norm_gate(glf_ref[0] + glb_ref[0], r_ref[0], glg_ref[...], GLA_HEADS, GLA_DV))
    y = jnp.dot(jnp.concatenate(parts, axis=-1).astype(BF16), wout_ref[...], preferred_element_type=F32)
    x1 = x_ref[0] + mod_ref[0, 2:3, :] * y
    x1_ref[0] = x1
    n = x1 * lax.rsqrt(jnp.mean(x1 * x1, axis=-1, keepdims=True) + EPS) * n2g_ref[...]
    h2 = n * (1.0 + mod_ref[0, 4:5, :]) + mod_ref[0, 3:4, :]
    h2_ref[0] = h2
    q_ref[0] = jnp.dot(h2.astype(BF16), wq_ref[...], preferred_element_type=F32)


def mix_out_tc(x, dn_f, dn_b, z, gl_f, gl_b, r, mod_l, dn_g, gla_g, n2_g, w_out, w_q):
    b, l, d = x.shape
    tok = lambda n: pl.BlockSpec((1, TOK_TILE, n), lambda bi, i: (bi, i, 0))
    full = lambda a: pl.BlockSpec(a.shape, lambda bi, i: (0,) * a.ndim)
    dn_g, gla_g, n2_g = dn_g.reshape(1, -1), gla_g.reshape(1, -1), n2_g.reshape(1, -1)
    nq = w_q.shape[1]
    return pl.pallas_call(
        _mix_out_kernel, grid=(b, l // TOK_TILE),
        in_specs=[tok(d), tok(dn_f.shape[2]), tok(dn_b.shape[2]), tok(z.shape[2]),
                  tok(gl_f.shape[2]), tok(gl_b.shape[2]), tok(r.shape[2]),
                  pl.BlockSpec((1,) + mod_l.shape[1:], lambda bi, i: (bi, 0, 0)),
                  full(dn_g), full(gla_g), full(n2_g), full(w_out), full(w_q)],
        out_specs=[tok(d), tok(d), tok(nq)],
        out_shape=[jax.ShapeDtypeStruct((b, l, d), F32), jax.ShapeDtypeStruct((b, l, d), F32),
                   jax.ShapeDtypeStruct((b, l, nq), F32)],
        compiler_params=_cparams("parallel", "arbitrary"), name="mix_out",
    )(x, dn_f, dn_b, z, gl_f, gl_b, r, mod_l, dn_g, gla_g, n2_g, w_out, w_q)


def _top_rows(s, k, payload=None):
    n = s.shape[0]
    row = lax.broadcasted_iota(jnp.int32, s.shape, 0).astype(F32)
    vals, picked = [], []
    for _ in range(k):
        m = jnp.max(s, axis=0, keepdims=True)
        first = jnp.min(jnp.where(s == m, row, float(n)), axis=0, keepdims=True)
        sel = row == first
        vals.append(m)
        if payload is None:
            picked.append(first)
        else:
            picked.append(jnp.max(jnp.where(sel, payload, -1.0), axis=0, keepdims=True))
        s = jnp.where(sel, -jnp.inf, s)
    return jnp.concatenate(vals, axis=0), jnp.concatenate(picked, axis=0)


def _candidate_rows(s0, i0, s1, i1):
    k = s0.shape[0]
    wide = SUBLANES
    blocks_s = [s0[0:1] + s1]
    blocks_i = [i0[0:1] * float(PEER_NKEYS) + i1]
    col = lax.broadcasted_iota(jnp.int32, (wide, s0.shape[1]), 0)
    for i in range(1, wide):
        keep = col < (k // (i + 1))
        blocks_s.append(jnp.where(keep, s0[i:i + 1] + s1[0:wide], -jnp.inf))
        blocks_i.append(i0[i:i + 1] * float(PEER_NKEYS) + i1[0:wide])
    blocks_s.append(s0[wide:k] + s1[0:1])
    blocks_i.append(i0[wide:k] * float(PEER_NKEYS) + i1[0:1])
    return jnp.concatenate(blocks_s, axis=0), jnp.concatenate(blocks_i, axis=0)


def _select_kernel(q_ref, k_ref, idx_ref, gate_ref, idx_s, gate_s):
    half = PEER_DQ // 2

    def head(h, carry):
        tops = []
        for p in range(2):
            qp = q_ref[:, pl.ds(pl.multiple_of(h * PEER_DQ + p * half, half), half)]
            s = lax.dot_general(k_ref[h, p], qp, (((1,), (1,)), ((), ())),
                                preferred_element_type=F32,
                                precision=lax.Precision.HIGHEST)
            tops.append(_top_rows(s, PEER_TOPK))
        (s0, i0), (s1, i1) = tops
        cand_s, cand_i = _candidate_rows(s0, i0, s1, i1)
        best_s, idx = _top_rows(cand_s, PEER_TOPK, payload=cand_i)
        e = jnp.exp(best_s - best_s[0:1])
        r0 = pl.multiple_of(h * PEER_TOPK, PEER_TOPK)
        idx_s[pl.ds(r0, PEER_TOPK), :] = idx
        gate_s[pl.ds(r0, PEER_TOPK), :] = e / jnp.sum(e, axis=0, keepdims=True)
        return carry

    lax.fori_loop(0, PEER_HEADS, head, 0)
    idx_ref[...] = idx_s[...].T.astype(jnp.int32)
    gate_ref[...] = gate_s[...].T


def peer_select_tc(q, keys):
    n_tok = q.shape[0]
    out_spec = pl.BlockSpec((SELECT_TILE, PICKS), lambda i: (i, 0))
    return pl.pallas_call(
        _select_kernel,
        grid=(n_tok // SELECT_TILE,),
        in_specs=[pl.BlockSpec((SELECT_TILE, q.shape[1]), lambda i: (i, 0)),
                  pl.BlockSpec(keys.shape, lambda i: (0, 0, 0, 0))],
        out_specs=[out_spec, out_spec],
        out_shape=[jax.ShapeDtypeStruct((n_tok, PICKS), jnp.int32),
                   jax.ShapeDtypeStruct((n_tok, PICKS), F32)],
        scratch_shapes=[pltpu.VMEM((PICKS, SELECT_TILE), F32), pltpu.VMEM((PICKS, SELECT_TILE), F32)],
        compiler_params=_cparams("parallel"), name="peer_select",
    )(q, keys)


def _sc_mesh():
    return plsc.VectorSubcoreMesh(core_axis_name="c", subcore_axis_name="s")


def _sc_pipeline(body, n_steps, in_specs, out_specs, operands):
    pltpu.emit_pipeline(
        body, grid=(n_steps,), in_specs=in_specs, out_specs=out_specs,
        core_axis_name=("c", "s"), dimension_semantics=(pltpu.PARALLEL,),
        trace_scopes=False,
    )(*operands)


def pack_table(t):
    half = t.shape[1] // 2
    bits = lax.bitcast_convert_type(t, jnp.uint32)
    lo_bits, hi_bits = bits[:, :half], bits[:, half:]
    low = (lo_bits + jnp.uint32(0x7FFF) + ((lo_bits >> 16) & jnp.uint32(1))) >> 16
    mag = hi_bits & jnp.uint32(0x7FFFFFFF)
    top = jnp.where(mag >= jnp.uint32(0x10000),
                    (mag - low + jnp.uint32(0x8000)) & jnp.uint32(0xFFFF0000), jnp.uint32(0))
    word = (hi_bits & jnp.uint32(0x80000000)) | top | low
    return lax.bitcast_convert_type(word, jnp.int32)


def _unpack(w):
    return plsc.bitcast(w << 16, F32), plsc.bitcast(w, F32)


def peer_act_partial_sc(u_p, idx2, h):
    n_groups, (n_tok, d) = idx2.shape[0], h.shape
    half = d // 2
    nsub = TOK_STEP * GROUPS_PER_TOK
    n_chunks = half // SC_LANES

    @functools.partial(
        pl.kernel, mesh=_sc_mesh(), compiler_params=pltpu.CompilerParams(needs_layout_passes=False),
        out_type=jax.ShapeDtypeStruct((n_groups, PICK_GROUP * SC_LANES), F32),
        scratch_types=[pltpu.VMEM((2, PICK_GROUP, half), jnp.int32),
                       pltpu.SemaphoreType.DMA((2,))],
    )
    def k(u_hbm, i_hbm, h_hbm, o_hbm, rows, sems):
        def body(i_v, h_v, o_v):
            def fetch(j, slot):
                return pltpu.make_async_copy(u_hbm.at[i_v.at[j]], rows.at[slot], sems.at[slot])

            fetch(0, 0).start()

            def sub(j, carry):
                slot = j % 2

                @pl.when(j + 1 < nsub)
                def _():
                    fetch(j + 1, 1 - slot).start()

                fetch(j, slot).wait()
                t = j // GROUPS_PER_TOK

                def picks(g, carry2):
                    kb = g * ACT_UNROLL
                    accs = [None] * ACT_UNROLL
                    for c in range(n_chunks):
                        h_lo = h_v[t, pl.ds(c * SC_LANES, SC_LANES)]
                        h_hi = h_v[t, pl.ds(half + c * SC_LANES, SC_LANES)]
                        for i in range(ACT_UNROLL):
                            lo, hi = _unpack(rows[slot, kb + i, pl.ds(c * SC_LANES, SC_LANES)])
                            p = lo * h_lo + hi * h_hi
                            accs[i] = p if accs[i] is None else accs[i] + p
                    for i in range(ACT_UNROLL):
                        o_v[j, pl.ds((kb + i) * SC_LANES, SC_LANES)] = accs[i]
                    return carry2

                lax.fori_loop(0, PICK_GROUP // ACT_UNROLL, picks, 0)
                return carry

            lax.fori_loop(0, nsub, sub, 0)

        _sc_pipeline(
            body, n_tok // TOK_STEP,
            [pl.BlockSpec((nsub, PICK_GROUP), lambda i: (i, 0)),
             pl.BlockSpec((TOK_STEP, d), lambda i: (i, 0))],
            [pl.BlockSpec((nsub, PICK_GROUP * SC_LANES), lambda i: (i, 0))],
            (i_hbm, h_hbm, o_hbm))

    return k(u_p, idx2, h)


def peer_combine_sc(v_p, idx2, coef_b):
    n_groups = idx2.shape[0]
    half = v_p.shape[1]
    n_tok = n_groups // GROUPS_PER_TOK
    nsub = TOK_STEP * GROUPS_PER_TOK
    n_chunks = half // SC_LANES

    @functools.partial(
        pl.kernel, mesh=_sc_mesh(), compiler_params=pltpu.CompilerParams(needs_layout_passes=False),
        out_type=jax.ShapeDtypeStruct((n_tok, 2 * half), F32),
        scratch_types=[pltpu.VMEM((2, PICK_GROUP, half), jnp.int32),
                       pltpu.SemaphoreType.DMA((2,))],
    )
    def k(v_hbm, i_hbm, c_hbm, o_hbm, rows, sems):
        def body(i_v, c_v, o_v):
            def fetch(j, slot):
                return pltpu.make_async_copy(v_hbm.at[i_v.at[j]], rows.at[slot], sems.at[slot])

            fetch(0, 0).start()
            for j in range(nsub):
                slot = j % 2
                if j + 1 < nsub:
                    fetch(j + 1, 1 - slot).start()
                fetch(j, slot).wait()
                t = j // GROUPS_PER_TOK
                first = j % GROUPS_PER_TOK == 0
                cks = [c_v[j, pl.ds(kk * SC_LANES, SC_LANES)] for kk in range(PICK_GROUP)]

                def chunk_pair(c2, carry, slot=slot, t=t, first=first, cks=cks):
                    for cc in range(COMB_CHUNKS):
                        l = (c2 * COMB_CHUNKS + cc) * SC_LANES
                        ways = 2
                        los, his = [None] * ways, [None] * ways
                        for kk in range(PICK_GROUP):
                            lo, hi = _unpack(rows[slot, kk, pl.ds(l, SC_LANES)])
                            a, b_ = cks[kk] * lo, cks[kk] * hi
                            w = kk % ways
                            los[w] = a if los[w] is None else los[w] + a
                            his[w] = b_ if his[w] is None else his[w] + b_
                        tot_lo, tot_hi = los[0] + los[1], his[0] + his[1]
                        if not first:
                            tot_lo = tot_lo + o_v[t, pl.ds(l, SC_LANES)]
                            tot_hi = tot_hi + o_v[t, pl.ds(half + l, SC_LANES)]
                        o_v[t, pl.ds(l, SC_LANES)] = tot_lo
                        o_v[t, pl.ds(half + l, SC_LANES)] = tot_hi
                    return carry

                lax.fori_loop(0, n_chunks // COMB_CHUNKS, chunk_pair, 0)

        _sc_pipeline(
            body, n_tok // TOK_STEP,
            [pl.BlockSpec((nsub, PICK_GROUP), lambda i: (i, 0)),
             pl.BlockSpec((nsub, PICK_GROUP * SC_LANES), lambda i: (i, 0))],
            [pl.BlockSpec((TOK_STEP, 2 * half), lambda i: (i, 0))],
            (i_hbm, c_hbm, o_hbm))

    return k(v_p, idx2, coef_b)


def _segment_matrix():
    r = lax.broadcasted_iota(jnp.int32, (PICK_GROUP * SC_LANES, PICK_GROUP), 0) // SC_LANES
    c = lax.broadcasted_iota(jnp.int32, (PICK_GROUP * SC_LANES, PICK_GROUP), 1)
    return (r == c).astype(F32)


def _coef_kernel(part_ref, gate_ref, o_ref):
    seg = _segment_matrix()
    act = jnp.dot(part_ref[...], seg, preferred_element_type=F32, precision=lax.Precision.HIGHEST)
    coef = gate_ref[...] * (0.5 * act * (1.0 + lax.erf(act * (2.0 ** -0.5))))
    o_ref[...] = lax.dot_general(coef, seg, (((1,), (1,)), ((), ())), preferred_element_type=F32,
                                 precision=lax.Precision.HIGHEST)


def peer_coef_tc(part, gate2):
    n_groups, width = part.shape
    tile = 1024
    return pl.pallas_call(
        _coef_kernel,
        grid=(n_groups // tile,),
        in_specs=[pl.BlockSpec((tile, width), lambda i: (i, 0)),
                  pl.BlockSpec((tile, PICK_GROUP), lambda i: (i, 0))],
        out_specs=pl.BlockSpec((tile, width), lambda i: (i, 0)),
        out_shape=jax.ShapeDtypeStruct((n_groups, width), F32),
        compiler_params=_cparams("parallel"), name="peer_coef",
    )(part, gate2)


def _final_kernel(x_ref, y_ref, mod_ref, g_ref, o_ref):
    x = x_ref[0] + mod_ref[0, 5:6, :] * y_ref[0]
    o_ref[0] = x * lax.rsqrt(jnp.mean(x * x, axis=-1, keepdims=True) + EPS) * g_ref[...]


def final_tc(x1, y, mod_l, final_g):
    b, l, d = x1.shape
    tok = pl.BlockSpec((1, TOK_TILE, d), lambda bi, i: (bi, i, 0))
    return pl.pallas_call(
        _final_kernel, grid=(b, l // TOK_TILE),
        in_specs=[tok, tok, pl.BlockSpec((1,) + mod_l.shape[1:], lambda bi, i: (bi, 0, 0)),
                  pl.BlockSpec((1, d), lambda bi, i: (0, 0))],
        out_specs=tok, out_shape=jax.ShapeDtypeStruct((b, l, d), F32),
        compiler_params=_cparams("parallel", "arbitrary"), name="final_norm",
    )(x1, y, mod_l, final_g.reshape(1, d))


def _pad_cols(w, n):
    return jnp.pad(w, ((0, 0), (0, n - w.shape[1])))


def _exact_zero(v):
    return jnp.minimum(jnp.abs(v), 0.0)


def forward(x, c, ctx, c_ctx, w_ada, b_ada, norm1_g, norm2_g, w_in, conv_w, dn_a_log,
            dn_dt_bias, dn_norm_g, gla_wa2, gla_ba, gla_norm_g, w_out, peer_wq, peer_keys,
            peer_u, peer_v, final_g):
    b, l, d = x.shape
    c_all = jnp.concatenate([c, c_ctx[None]], axis=0)
    c_all = jnp.pad(c_all, ((0, (-c_all.shape[0]) % SUBLANES), (0, 0)))
    mod = adaln_mod(c_all, w_ada, b_ada)
    mod_l = mod[:b].reshape(b, 6, d)
    mod_c = jnp.broadcast_to(mod[b].reshape(1, 6, d), (b, 6, d))
    o = DN_QKV
    hv = DN_HEADS * DN_DV
    w_dn_qkv, w_dn_z = w_in[:, :o], w_in[:, o:o + hv]
    w_dn_ba = _pad_cols(w_in[:, o + hv:DN_COLS], LANES)
    g0 = DN_COLS
    gqk, gv = 2 * GLA_HEADS * GLA_DK, GLA_HEADS * GLA_DV
    w_gl_qk, w_gl_v = w_in[:, g0:g0 + gqk], w_in[:, g0 + gqk:g0 + gqk + gv]
    w_gl_r = w_in[:, g0 + gqk + gv:g0 + gqk + 2 * gv]
    w_gl_lr = _pad_cols(w_in[:, g0 + gqk + 2 * gv:], LANES)
    w_lat = [w.astype(BF16) for w in (w_dn_qkv, w_dn_ba, w_gl_qk, w_gl_v, w_gl_lr, w_dn_z, w_gl_r)]
    w_ctx = w_lat[:5]
    w_la = jnp.zeros((2, LANES, GLA_HEADS * GLA_DK), F32)
    for dd in range(2):
        w_la = w_la.at[dd, dd * GLA_LR:(dd + 1) * GLA_LR].set(gla_wa2[dd])
    w_la = w_la.astype(BF16)
    b_la = gla_ba.reshape(2, 1, GLA_HEADS * GLA_DK)
    w_out_b, w_q_b = w_out.astype(BF16), peer_wq.astype(BF16)
    u_p, v_p = pack_table(peer_u), pack_table(peer_v)

    bg = b // BATCH_GROUPS
    n_tok = bg * l

    def mixer(xg, ctxg, mod_cg, mod_g, after_select, after_combine):
        mod_cg, mod_g = mod_cg + after_select, mod_g + after_select
        c_qkv, c_ba, c_qk, c_v, c_lr = in_projection(ctxg, norm1_g, mod_cg[:, 0:2], w_ctx)
        l_qkv, l_ba, l_qk, l_v, l_lr, l_z, l_r = in_projection(xg, norm1_g, mod_g[:, 0:2], w_lat)
        feat_c, feat_l = dn_features_tc(c_qkv, conv_w), dn_features_tc(l_qkv, conv_w)
        dn_f = dn_scan_tc(feat_c, feat_l, c_ba, l_ba, dn_a_log, dn_dt_bias, rev=False)
        dn_b = dn_scan_tc(feat_c, feat_l, c_ba, l_ba, dn_a_log, dn_dt_bias, rev=True)
        gl_f = gla_scan_tc(c_qk, c_v, c_lr, l_qk, l_v, l_lr, w_la, b_la, rev=False)
        gl_b = gla_scan_tc(c_qk, c_v, c_lr, l_qk, l_v, l_lr, w_la, b_la, rev=True)
        return mix_out_tc(xg, dn_f, dn_b, l_z, gl_f, gl_b, l_r, mod_g, dn_norm_g + after_combine,
                          gla_norm_g, norm2_g, w_out_b, w_q_b)

    def select_and_act(h2, q, after_coef):
        idx, gate = peer_select_tc(q.reshape(n_tok, -1), peer_keys + after_coef)
        idx2 = idx.reshape(n_tok * GROUPS_PER_TOK, PICK_GROUP)
        gate2 = gate.reshape(n_tok * GROUPS_PER_TOK, PICK_GROUP)
        return idx2, gate2, peer_act_partial_sc(u_p, idx2, h2.reshape(n_tok, d))

    def coef_and_combine(st, after_mixer):
        coef_b = peer_coef_tc(st["part"], st["gate2"] + after_mixer)
        st["y"] = peer_combine_sc(v_p, st["idx2"], coef_b)
        return _exact_zero(coef_b[0, 0])

    zero = jnp.zeros((), F32)
    slices, z_sel, z_coef = [], zero, zero
    for i in range(0, b, bg):
        g = len(slices)
        z_comb = _exact_zero(slices[g - 2]["y"][0, 0]) if g >= 2 else zero
        mod_g = mod_l[i:i + bg]
        x1, h2, q = mixer(x[i:i + bg], ctx[i:i + bg], mod_c[i:i + bg], mod_g, z_sel, z_comb)
        if g >= 1:
            z_coef = coef_and_combine(slices[g - 1], _exact_zero(x1[0, 0, 0]))
        idx2, gate2, part = select_and_act(h2, q, z_coef)
        z_sel = _exact_zero(gate2[0, 0])
        slices.append(dict(x1=x1, mod=mod_g, idx2=idx2, gate2=gate2, part=part))
    coef_and_combine(slices[-1], zero)
    outs = [final_tc(st["x1"], st["y"].reshape(bg, l, d), st["mod"], final_g) for st in slices]
    return jnp.concatenate(outs, axis=0)


def kernel(x, c, ctx, c_ctx, w_ada, b_ada, norm1_g, norm2_g, w_in, conv_w, dn_a_log,
           dn_dt_bias, dn_norm_g, gla_wa2, gla_ba, gla_norm_g, w_out, peer_wq, peer_keys,
           peer_u, peer_v, final_g):
    assert w_ada.shape[0] == 1, "single-layer block: the context stream is only consumed, never updated"
    return forward(x, c, ctx, c_ctx, w_ada[0], b_ada[0], norm1_g[0], norm2_g[0], w_in[0], conv_w[0],
                   dn_a_log[0], dn_dt_bias[0], dn_norm_g[0], gla_wa2[0], gla_ba[0], gla_norm_g[0],
                   w_out[0], peer_wq[0], peer_keys[0], peer_u[0], peer_v[0], final_g)
```

```python
import functools

import jax
import jax.numpy as jnp
from jax import lax
from jax.experimental import pallas as pl
from jax.experimental.pallas import tpu as pltpu
from jax.experimental.pallas import tpu_sc as plsc

GRID_W = 64
DN_HEADS = 4
DN_DK = 128
DN_DV = 128
CONV_W = 5
GLA_HEADS = 4
GLA_DK = 64
GLA_DV = 128
GLA_LR = 16
GLA_TAU = 16.0
CHUNK = 64
PEER_HEADS = 8
PEER_NKEYS = 128
PEER_DQ = 256
PEER_TOPK = 16
EPS = 1e-6
DN_QKV = 2 * DN_HEADS * DN_DK + DN_HEADS * DN_DV
DN_COLS = DN_QKV + DN_HEADS * DN_DV + 4 * DN_HEADS

SUBLANES = 8
LANES = 128
SC_LANES = 16
VMEM_LIMIT_BYTES = 48 * 1024 * 1024

TOK_TILE = 256
SELECT_TILE = 256
SCAN_BATCH = 2
GLA_SUB = 16
PICK_GROUP = 32
TOK_STEP = 8
COMB_CHUNKS = 2
ACT_UNROLL = 16
BATCH_GROUPS = 8
PICKS = PEER_HEADS * PEER_TOPK
GROUPS_PER_TOK = PICKS // PICK_GROUP

F32 = jnp.float32
BF16 = jnp.bfloat16


def _cparams(*semantics):
    return pltpu.CompilerParams(dimension_semantics=semantics, vmem_limit_bytes=VMEM_LIMIT_BYTES)


def _dot(a, b):
    return jnp.dot(a.astype(BF16), b.astype(BF16), preferred_element_type=F32)


def _dot_nt(a, b):
    return lax.dot_general(a.astype(BF16), b.astype(BF16), (((1,), (1,)), ((), ())),
                           preferred_element_type=F32)


def _dot_tn(a, b):
    return lax.dot_general(a.astype(BF16), b.astype(BF16), (((0,), (0,)), ((), ())),
                           preferred_element_type=F32)


def _split(x):
    hi = x.astype(BF16)
    return hi, (x - hi.astype(F32)).astype(BF16)


def _mask_dot(mask_bf16, x):
    hi, lo = _split(x)
    return (jnp.dot(mask_bf16, hi, preferred_element_type=F32)
            + jnp.dot(mask_bf16, lo, preferred_element_type=F32))


def _softplus(x):
    return jnp.maximum(x, 0.0) + jnp.log(1.0 + jnp.exp(-jnp.abs(x)))


def _tri_masks(rev):
    r = lax.broadcasted_iota(jnp.int32, (CHUNK, CHUNK), 0)
    c = lax.broadcasted_iota(jnp.int32, (CHUNK, CHUNK), 1)
    d = (c - r) if rev else (r - c)
    return d >= 0, d > 0


def _mod_kernel(c_ref, w_ref, b_ref, o_ref):
    c = c_ref[...]
    s = c * jax.nn.sigmoid(c)
    o_ref[...] = jnp.dot(s, w_ref[...], preferred_element_type=F32,
                         precision=lax.Precision.HIGHEST) + b_ref[...]


def adaln_mod(c_all, w_ada, b_ada):
    r, d = c_all.shape
    n = w_ada.shape[1]
    tn = 512
    return pl.pallas_call(
        _mod_kernel, grid=(n // tn,),
        in_specs=[pl.BlockSpec((r, d), lambda j: (0, 0)),
                  pl.BlockSpec((d, tn), lambda j: (0, j)),
                  pl.BlockSpec((1, tn), lambda j: (0, j))],
        out_specs=pl.BlockSpec((r, tn), lambda j: (0, j)),
        out_shape=jax.ShapeDtypeStruct((r, n), F32),
        compiler_params=_cparams("arbitrary"), name="adaln_mod",
    )(c_all, w_ada, b_ada.reshape(1, n))


def _inproj_kernel(x_ref, g_ref, mod_ref, *refs):
    n_out = len(refs) // 2
    x = x_ref[0]
    y = x * lax.rsqrt(jnp.mean(x * x, axis=-1, keepdims=True) + EPS) * g_ref[...]
    h = (y * (1.0 + mod_ref[0, 1:2, :]) + mod_ref[0, 0:1, :]).astype(BF16)
    for w_ref, o_ref in zip(refs[:n_out], refs[n_out:]):
        o_ref[0] = jnp.dot(h, w_ref[...], preferred_element_type=F32)


def in_projection(x, norm_g, mod, weights):
    b, l, d = x.shape
    w_specs = [pl.BlockSpec(w.shape, lambda bi, i: (0, 0)) for w in weights]
    o_specs = [pl.BlockSpec((1, TOK_TILE, w.shape[1]), lambda bi, i: (bi, i, 0)) for w in weights]
    return pl.pallas_call(
        _inproj_kernel, grid=(b, l // TOK_TILE),
        in_specs=[pl.BlockSpec((1, TOK_TILE, d), lambda bi, i: (bi, i, 0)),
                  pl.BlockSpec((1, d), lambda bi, i: (0, 0)),
                  pl.BlockSpec((1, 2, d), lambda bi, i: (bi, 0, 0))] + w_specs,
        out_specs=o_specs,
        out_shape=[jax.ShapeDtypeStruct((b, l, w.shape[1]), F32) for w in weights],
        compiler_params=_cparams("parallel", "arbitrary"), name="in_projection",
    )(x, norm_g.reshape(1, d), mod, *weights)


def _dn_feature_kernel(x_ref, w_ref, o_ref):
    x = x_ref[0]
    n = x.shape[0]
    t = lax.broadcasted_iota(jnp.int32, (n, 1), 0)
    pad = CONV_W // 2
    acc = w_ref[0, pad:pad + 1, :] * x
    for j in range(CONV_W):
        s = j - pad
        if s == 0:
            continue
        xs = pltpu.roll(x, (-s) % n, axis=0)
        bad = (t < -s) if s < 0 else (t >= n - s)
        acc = acc + w_ref[0, j:j + 1, :] * jnp.where(bad, 0.0, xs)
    y = acc * jax.nn.sigmoid(acc)
    kind = pl.program_id(1) // DN_HEADS
    inv = lax.rsqrt(jnp.sum(y * y, axis=-1, keepdims=True) + EPS)
    scale = jnp.where(kind == 0, inv * DN_DK ** -0.5, jnp.where(kind == 1, inv, 1.0))
    o_ref[0] = y * scale


def dn_features_tc(qkv, conv_w):
    b, l, n = qkv.shape
    nblk = n // LANES
    w = jnp.zeros((nblk, SUBLANES, LANES), F32).at[:, :CONV_W].set(
        conv_w.reshape(CONV_W, nblk, LANES).transpose(1, 0, 2))
    return pl.pallas_call(
        _dn_feature_kernel, grid=(b, nblk),
        in_specs=[pl.BlockSpec((1, l, LANES), lambda bi, j: (bi, 0, j)),
                  pl.BlockSpec((1, SUBLANES, LANES), lambda bi, j: (j, 0, 0))],
        out_specs=pl.BlockSpec((1, l, LANES), lambda bi, j: (bi, 0, j)),
        out_shape=jax.ShapeDtypeStruct((b, l, n), F32),
        compiler_params=_cparams("parallel", "arbitrary"), name="dn_features",
    )(qkv, w)


def _scan_chunks(rev, n_ctx, n_lat):
    if rev:
        ctx = lambda j: jnp.maximum(n_ctx - 1 - j, 0)
        lat = lambda j: jnp.where(j < n_ctx, n_lat - 1, n_lat - 1 - (j - n_ctx))
    else:
        ctx = lambda j: jnp.minimum(j, n_ctx - 1)
        lat = lambda j: jnp.maximum(j - n_ctx, 0)
    return ctx, lat


def _dn_scan_kernel(rev, dirn, n_ctx, alog_ref, dtb_ref, fc_ref, fl_ref, bac_ref, bal_ref, o_ref, s_ref):
    step = pl.program_id(1)

    @pl.when(step == 0)
    def _():
        s_ref[...] = jnp.zeros_like(s_ref)

    in_ctx = step < n_ctx
    f = jnp.where(in_ctx, fc_ref[0], fl_ref[0])
    ba = jnp.where(in_ctx, bac_ref[0], bal_ref[0])
    nh, hd, n = DN_HEADS, DN_HEADS * DN_DK, DN_HEADS * CHUNK
    stack = lambda base, w: jnp.concatenate([f[:, base + h * w:base + (h + 1) * w] for h in range(nh)], axis=0)
    q_s, k_s, v_s = stack(0, DN_DK), stack(hd, DN_DK), stack(2 * hd, DN_DV)
    incl, _ = _tri_masks(rev)
    beta_all = jax.nn.sigmoid(ba)
    g_all = -jnp.exp(alog_ref[...]) * _softplus(ba + dtb_ref[...])
    gc_all = _mask_dot(incl.astype(BF16), g_all)
    gc_t = jnp.concatenate([gc_all, gc_all], axis=0).T
    g_tot = jnp.sum(g_all, axis=0, keepdims=True)
    cb = [dirn * nh + h for h in range(nh)]
    cg = [2 * nh + c for c in cb]
    col = lambda a, cs: jnp.concatenate([a[:, c:c + 1] for c in cs], axis=0)
    beta_c, gc_c = col(beta_all, cb), col(gc_all, cg)
    gtot_c = jnp.concatenate([jnp.broadcast_to(g_tot[:, c:c + 1], (CHUNK, 1)) for c in cg], axis=0)
    gc_r = jnp.concatenate([gc_t[c:c + 1, :CHUNK] for c in cg], axis=1)
    r = lax.broadcasted_iota(jnp.int32, (n, n), 0)
    c = lax.broadcasted_iota(jnp.int32, (n, n), 1)
    same = (r // CHUNK) == (c // CHUNK)
    d = (c - r) if rev else (r - c)
    incl_bd, strict_bd = same & (d >= 0), same & (d > 0)
    eye = (r == c).astype(F32)
    decay = jnp.where(incl_bd, jnp.exp(jnp.where(incl_bd, gc_c - gc_r, 0.0)), 0.0)
    kb_s = k_s * beta_c
    lower = jnp.where(strict_bd, _dot_nt(kb_s, k_s) * decay, 0.0)
    eg_c = jnp.exp(gc_c)
    inv = eye - lower
    pw = lower
    for _ in range(5):
        pw = _dot(pw, pw)
        inv = inv + _dot(inv, pw)
    sol = _dot(inv, jnp.concatenate([v_s * beta_c, kb_s * eg_c], axis=-1))
    u_s, w_s = sol[:, :DN_DV], sol[:, DN_DV:]
    k_dec = k_s * jnp.exp(gtot_c - gc_c)
    rb = lax.broadcasted_iota(jnp.int32, (n, DN_DK), 0) // CHUNK
    expand = lambda x: jnp.concatenate([jnp.where(rb == h, x, 0.0) for h in range(nh)], axis=1)
    s = s_ref[...]
    v_new = u_s - _dot(expand(w_s), s)
    a_qk = _dot_nt(q_s, k_s) * decay
    o_s = _dot(expand(q_s * eg_c), s) + _dot(a_qk, v_new)
    gl_rows = jnp.concatenate([jnp.broadcast_to(jnp.exp(g_tot[:, cc:cc + 1]), (DN_DK, 1)) for cc in cg], axis=0)
    s_ref[...] = s * gl_rows + _dot_tn(expand(k_dec), v_new)
    o_ref[0] = jnp.concatenate([o_s[h * CHUNK:(h + 1) * CHUNK] for h in range(nh)], axis=1)


def dn_scan_tc(feat_c, feat_l, ba_c, ba_l, a_log, dt_bias, rev):
    b, l, nf = feat_l.shape
    n_ctx, n_lat = feat_c.shape[1] // CHUNK, l // CHUNK
    dirn = 1 if rev else 0
    cc, lc = _scan_chunks(rev, n_ctx, n_lat)
    lanes = lambda p: jnp.zeros((1, LANES), F32).at[0, 2 * DN_HEADS:4 * DN_HEADS].set(p.reshape(-1))
    vec = pl.BlockSpec((1, LANES), lambda bi, j: (0, 0))
    return pl.pallas_call(
        functools.partial(_dn_scan_kernel, rev, dirn, n_ctx), grid=(b, n_ctx + n_lat),
        in_specs=[vec, vec,
                  pl.BlockSpec((1, CHUNK, nf), lambda bi, j: (bi, cc(j), 0)),
                  pl.BlockSpec((1, CHUNK, nf), lambda bi, j: (bi, lc(j), 0)),
                  pl.BlockSpec((1, CHUNK, LANES), lambda bi, j: (bi, cc(j), 0)),
                  pl.BlockSpec((1, CHUNK, LANES), lambda bi, j: (bi, lc(j), 0))],
        out_specs=pl.BlockSpec((1, CHUNK, DN_HEADS * DN_DV), lambda bi, j: (bi, lc(j), 0)),
        out_shape=jax.ShapeDtypeStruct((b, l, DN_HEADS * DN_DV), F32),
        scratch_shapes=[pltpu.VMEM((DN_HEADS * DN_DK, DN_DV), F32)],
        compiler_params=_cparams("parallel", "arbitrary"), name="dn_scan_bwd" if rev else "dn_scan_fwd",
    )(lanes(a_log), lanes(dt_bias), feat_c, feat_l, ba_c, ba_l)


def _from_grid_cols(blk, n):
    cols = blk.shape[1] // n
    return jnp.concatenate([blk[:, i * n:(i + 1) * n] for i in range(cols)], axis=0)


def _gla_scan_kernel(rev, dirn, n_ctx, qkc_ref, vc_ref, lrc_ref, qkl_ref, vl_ref, lrl_ref,
                     wla_ref, bla_ref, o_ref, s_ref):
    step = pl.program_id(1)

    @pl.when(step == 0)
    def _():
        s_ref[...] = jnp.zeros_like(s_ref)

    in_ctx = step < n_ctx
    hk, hv = GLA_HEADS * GLA_DK, GLA_HEADS * GLA_DV
    qk = jnp.where(in_ctx, qkc_ref[0], _from_grid_cols(qkl_ref[0], 2 * hk))
    vv = jnp.where(in_ctx, vc_ref[0], _from_grid_cols(vl_ref[0], hv))
    lr = jnp.where(in_ctx, lrc_ref[0], _from_grid_cols(lrl_ref[0], LANES))
    incl, _ = _tri_masks(rev)
    incl_b = incl.astype(BF16)
    pre = _dot(lr, wla_ref[0]) + bla_ref[0]
    la_all = -_softplus(-pre) * (1.0 / GLA_TAU)
    bc_all = _mask_dot(incl_b, la_all)
    b_tot_all = jnp.sum(la_all, axis=0, keepdims=True)
    outs = []
    for h in range(GLA_HEADS):
        q = qk[:, h * GLA_DK:(h + 1) * GLA_DK] * GLA_DK ** -0.5
        k = qk[:, hk + h * GLA_DK:hk + (h + 1) * GLA_DK]
        v = vv[:, h * GLA_DV:(h + 1) * GLA_DV]
        bc = bc_all[:, h * GLA_DK:(h + 1) * GLA_DK]
        b_tot = b_tot_all[:, h * GLA_DK:(h + 1) * GLA_DK]
        st = s_ref[h]
        o = _dot_nt(q * jnp.exp(bc), st)
        parts = []
        for i in range(CHUNK // GLA_SUB):
            lo_r, hi_r = i * GLA_SUB, (i + 1) * GLA_SUB
            if rev:
                ref = bc[hi_r - 1:hi_r]
                c0, c1 = lo_r, CHUNK
            else:
                ref = bc[lo_r:lo_r + 1]
                c0, c1 = 0, hi_r
            qi = q[lo_r:hi_r] * jnp.exp(bc[lo_r:hi_r] - ref)
            ki = k[c0:c1] * jnp.exp(ref - bc[c0:c1])
            att = _dot_nt(qi, ki)
            rg = lax.broadcasted_iota(jnp.int32, (GLA_SUB, c1 - c0), 0) + lo_r
            cg = lax.broadcasted_iota(jnp.int32, (GLA_SUB, c1 - c0), 1) + c0
            keep = (cg >= rg) if rev else (cg <= rg)
            parts.append(_dot(jnp.where(keep, att, 0.0), v[c0:c1]))
        outs.append(o + jnp.concatenate(parts, axis=0))
        k_dec = k * jnp.exp(b_tot - bc)
        s_ref[h] = st * jnp.exp(b_tot) + _dot_tn(v, k_dec)
    o = jnp.concatenate(outs, axis=-1)
    rows = o_ref.shape[1]
    o_ref[0] = jnp.concatenate([o[i * rows:(i + 1) * rows] for i in range(CHUNK // rows)], axis=-1)


def gla_scan_tc(qk_c, v_c, lr_c, qk_l, v_l, lr_l, w_la, b_la, rev):
    b, l, _ = qk_l.shape
    rows = l // GRID_W
    cols = CHUNK // rows
    n_ctx, n_lat = qk_c.shape[1] // CHUNK, l // CHUNK
    dirn = 1 if rev else 0
    cc, lc = _scan_chunks(rev, n_ctx, n_lat)
    hv = GLA_HEADS * GLA_DV
    ctx_blk = lambda a: pl.BlockSpec((1, CHUNK, a.shape[2]), lambda bi, j: (bi, cc(j), 0))
    lat_blk = lambda n: pl.BlockSpec((1, rows, cols * n), lambda bi, j: (bi, 0, lc(j)))
    grid_view = lambda a: a.reshape(b, rows, GRID_W * a.shape[2])
    out = pl.pallas_call(
        functools.partial(_gla_scan_kernel, rev, dirn, n_ctx), grid=(b, n_ctx + n_lat),
        in_specs=[ctx_blk(qk_c), ctx_blk(v_c), ctx_blk(lr_c),
                  lat_blk(qk_l.shape[2]), lat_blk(v_l.shape[2]), lat_blk(lr_l.shape[2]),
                  pl.BlockSpec((1,) + w_la.shape[1:], lambda bi, j: (dirn, 0, 0)),
                  pl.BlockSpec((1,) + b_la.shape[1:], lambda bi, j: (dirn, 0, 0))],
        out_specs=lat_blk(hv),
        out_shape=jax.ShapeDtypeStruct((b, rows, GRID_W * hv), F32),
        scratch_shapes=[pltpu.VMEM((GLA_HEADS, GLA_DV, GLA_DK), F32)],
        compiler_params=_cparams("parallel", "arbitrary"), name="gla_scan_bwd" if rev else "gla_scan_fwd",
    )(qk_c, v_c, lr_c, grid_view(qk_l), grid_view(v_l), grid_view(lr_l), w_la, b_la)
    return out.reshape(b, l, hv)


def _head_norm_gate(o, gate, g, n_heads, dv):
    parts = []
    for h in range(n_heads):
        oh = o[:, h * dv:(h + 1) * dv]
        gh = gate[:, h * dv:(h + 1) * dv]
        yh = oh * lax.rsqrt(jnp.mean(oh * oh, axis=-1, keepdims=True) + EPS) * g
        parts.append(yh * (gh * jax.nn.sigmoid(gh)))
    return parts


def _mix_out_kernel(x_ref, dnf_ref, dnb_ref, z_ref, glf_ref, glb_ref, r_ref, mod_ref, dng_ref,
                    glg_ref, n2g_ref, wout_ref, wq_ref, x1_ref, h2_ref, q_ref):
    parts = (_head_norm_gate(dnf_ref[0] + dnb_ref[0], z_ref[0], dng_ref[...], DN_HEADS, DN_DV)
             + _head_norm_gate(glf_ref[0] + glb_ref[0], r_ref[0], glg_ref[...], GLA_HEADS, GLA_DV))
    y = jnp.dot(jnp.concatenate(parts, axis=-1).astype(BF16), wout_ref[...], preferred_element_type=F32)
    x1 = x_ref[0] + mod_ref[0, 2:3, :] * y
    x1_ref[0] = x1
    n = x1 * lax.rsqrt(jnp.mean(x1 * x1, axis=-1, keepdims=True) + EPS) * n2g_ref[...]
    h2 = n * (1.0 + mod_ref[0, 4:5, :]) + mod_ref[0, 3:4, :]
    h2_ref[0] = h2
    q_ref[0] = jnp.dot(h2.astype(BF16), wq_ref[...], preferred_element_type=F32)


def mix_out_tc(x, dn_f, dn_b, z, gl_f, gl_b, r, mod_l, dn_g, gla_g, n2_g, w_out, w_q):
    b, l, d = x.shape
    tok = lambda n: pl.BlockSpec((1, TOK_TILE, n), lambda bi, i: (bi, i, 0))
    full = lambda a: pl.BlockSpec(a.shape, lambda bi, i: (0,) * a.ndim)
    dn_g, gla_g, n2_g = dn_g.reshape(1, -1), gla_g.reshape(1, -1), n2_g.reshape(1, -1)
    nq = w_q.shape[1]
    return pl.pallas_call(
        _mix_out_kernel, grid=(b, l // TOK_TILE),
        in_specs=[tok(d), tok(dn_f.shape[2]), tok(dn_b.shape[2]), tok(z.shape[2]),
                  tok(gl_f.shape[2]), tok(gl_b.shape[2]), tok(r.shape[2]),
                  pl.BlockSpec((1,) + mod_l.shape[1:], lambda bi, i: (bi, 0, 0)),
                  full(dn_g), full(gla_g), full(n2_g), full(w_out), full(w_q)],
        out_specs=[tok(d), tok(d), tok(nq)],
        out_shape=[jax.ShapeDtypeStruct((b, l, d), F32), jax.ShapeDtypeStruct((b, l, d), F32),
                   jax.ShapeDtypeStruct((b, l, nq), F32)],
        compiler_params=_cparams("parallel", "arbitrary"), name="mix_out",
    )(x, dn_f, dn_b, z, gl_f, gl_b, r, mod_l, dn_g, gla_g, n2_g, w_out, w_q)


def _top_rows(s, k, payload=None):
    n = s.shape[0]
    row = lax.broadcasted_iota(jnp.int32, s.shape, 0).astype(F32)
    vals, picked = [], []
    for _ in range(k):
        m = jnp.max(s, axis=0, keepdims=True)
        first = jnp.min(jnp.where(s == m, row, float(n)), axis=0, keepdims=True)
        sel = row == first
        vals.append(m)
        if payload is None:
            picked.append(first)
        else:
            picked.append(jnp.max(jnp.where(sel, payload, -1.0), axis=0, keepdims=True))
        s = jnp.where(sel, -jnp.inf, s)
    return jnp.concatenate(vals, axis=0), jnp.concatenate(picked, axis=0)


def _candidate_rows(s0, i0, s1, i1):
    k = s0.shape[0]
    wide = SUBLANES
    blocks_s = [s0[0:1] + s1]
    blocks_i = [i0[0:1] * float(PEER_NKEYS) + i1]
    col = lax.broadcasted_iota(jnp.int32, (wide, s0.shape[1]), 0)
    for i in range(1, wide):
        keep = col < (k // (i + 1))
        blocks_s.append(jnp.where(keep, s0[i:i + 1] + s1[0:wide], -jnp.inf))
        blocks_i.append(i0[i:i + 1] * float(PEER_NKEYS) + i1[0:wide])
    blocks_s.append(s0[wide:k] + s1[0:1])
    blocks_i.append(i0[wide:k] * float(PEER_NKEYS) + i1[0:1])
    return jnp.concatenate(blocks_s, axis=0), jnp.concatenate(blocks_i, axis=0)


def _select_kernel(q_ref, k_ref, idx_ref, gate_ref, idx_s, gate_s):
    half = PEER_DQ // 2

    def head(h, carry):
        tops = []
        for p in range(2):
            qp = q_ref[:, pl.ds(pl.multiple_of(h * PEER_DQ + p * half, half), half)]
            s = lax.dot_general(k_ref[h, p], qp, (((1,), (1,)), ((), ())),
                                preferred_element_type=F32,
                                precision=lax.Precision.HIGHEST)
            tops.append(_top_rows(s, PEER_TOPK))
        (s0, i0), (s1, i1) = tops
        cand_s, cand_i = _candidate_rows(s0, i0, s1, i1)
        best_s, idx = _top_rows(cand_s, PEER_TOPK, payload=cand_i)
        e = jnp.exp(best_s - best_s[0:1])
        r0 = pl.multiple_of(h * PEER_TOPK, PEER_TOPK)
        idx_s[pl.ds(r0, PEER_TOPK), :] = idx
        gate_s[pl.ds(r0, PEER_TOPK), :] = e / jnp.sum(e, axis=0, keepdims=True)
        return carry

    lax.fori_loop(0, PEER_HEADS, head, 0)
    idx_ref[...] = idx_s[...].T.astype(jnp.int32)
    gate_ref[...] = gate_s[...].T


def peer_select_tc(q, keys):
    n_tok = q.shape[0]
    out_spec = pl.BlockSpec((SELECT_TILE, PICKS), lambda i: (i, 0))
    return pl.pallas_call(
        _select_kernel,
        grid=(n_tok // SELECT_TILE,),
        in_specs=[pl.BlockSpec((SELECT_TILE, q.shape[1]), lambda i: (i, 0)),
                  pl.BlockSpec(keys.shape, lambda i: (0, 0, 0, 0))],
        out_specs=[out_spec, out_spec],
        out_shape=[jax.ShapeDtypeStruct((n_tok, PICKS), jnp.int32),
                   jax.ShapeDtypeStruct((n_tok, PICKS), F32)],
        scratch_shapes=[pltpu.VMEM((PICKS, SELECT_TILE), F32), pltpu.VMEM((PICKS, SELECT_TILE), F32)],
        compiler_params=_cparams("parallel"), name="peer_select",
    )(q, keys)


def _sc_mesh():
    return plsc.VectorSubcoreMesh(core_axis_name="c", subcore_axis_name="s")


def _sc_pipeline(body, n_steps, in_specs, out_specs, operands):
    pltpu.emit_pipeline(
        body, grid=(n_steps,), in_specs=in_specs, out_specs=out_specs,
        core_axis_name=("c", "s"), dimension_semantics=(pltpu.PARALLEL,),
        trace_scopes=False,
    )(*operands)


def pack_table(t):
    half = t.shape[1] // 2
    bits = lax.bitcast_convert_type(t, jnp.uint32)
    lo_bits, hi_bits = bits[:, :half], bits[:, half:]
    low = (lo_bits + jnp.uint32(0x7FFF) + ((lo_bits >> 16) & jnp.uint32(1))) >> 16
    mag = hi_bits & jnp.uint32(0x7FFFFFFF)
    top = jnp.where(mag >= jnp.uint32(0x10000),
                    (mag - low + jnp.uint32(0x8000)) & jnp.uint32(0xFFFF0000), jnp.uint32(0))
    word = (hi_bits & jnp.uint32(0x80000000)) | top | low
    return lax.bitcast_convert_type(word, jnp.int32)


def _unpack(w):
    return plsc.bitcast(w << 16, F32), plsc.bitcast(w, F32)


def peer_act_partial_sc(u_p, idx2, h):
    n_groups, (n_tok, d) = idx2.shape[0], h.shape
    half = d // 2
    nsub = TOK_STEP * GROUPS_PER_TOK
    n_chunks = half // SC_LANES

    @functools.partial(
        pl.kernel, mesh=_sc_mesh(), compiler_params=pltpu.CompilerParams(needs_layout_passes=False),
        out_type=jax.ShapeDtypeStruct((n_groups, PICK_GROUP * SC_LANES), F32),
        scratch_types=[pltpu.VMEM((2, PICK_GROUP, half), jnp.int32),
                       pltpu.SemaphoreType.DMA((2,))],
    )
    def k(u_hbm, i_hbm, h_hbm, o_hbm, rows, sems):
        def body(i_v, h_v, o_v):
            def fetch(j, slot):
                return pltpu.make_async_copy(u_hbm.at[i_v.at[j]], rows.at[slot], sems.at[slot])

            fetch(0, 0).start()

            def sub(j, carry):
                slot = j % 2

                @pl.when(j + 1 < nsub)
                def _():
                    fetch(j + 1, 1 - slot).start()

                fetch(j, slot).wait()
                t = j // GROUPS_PER_TOK

                def picks(g, carry2):
                    kb = g * ACT_UNROLL
                    accs = [None] * ACT_UNROLL
                    for c in range(n_chunks):
                        h_lo = h_v[t, pl.ds(c * SC_LANES, SC_LANES)]
                        h_hi = h_v[t, pl.ds(half + c * SC_LANES, SC_LANES)]
                        for i in range(ACT_UNROLL):
                            lo, hi = _unpack(rows[slot, kb + i, pl.ds(c * SC_LANES, SC_LANES)])
                            p = lo * h_lo + hi * h_hi
                            accs[i] = p if accs[i] is None else accs[i] + p
                    for i in range(ACT_UNROLL):
                        o_v[j, pl.ds((kb + i) * SC_LANES, SC_LANES)] = accs[i]
                    return carry2

                lax.fori_loop(0, PICK_GROUP // ACT_UNROLL, picks, 0)
                return carry

            lax.fori_loop(0, nsub, sub, 0)

        _sc_pipeline(
            body, n_tok // TOK_STEP,
            [pl.BlockSpec((nsub, PICK_GROUP), lambda i: (i, 0)),
             pl.BlockSpec((TOK_STEP, d), lambda i: (i, 0))],
            [pl.BlockSpec((nsub, PICK_GROUP * SC_LANES), lambda i: (i, 0))],
            (i_hbm, h_hbm, o_hbm))

    return k(u_p, idx2, h)


def peer_combine_sc(v_p, idx2, coef_b):
    n_groups = idx2.shape[0]
    half = v_p.shape[1]
    n_tok = n_groups // GROUPS_PER_TOK
    nsub = TOK_STEP * GROUPS_PER_TOK
    n_chunks = half // SC_LANES

    @functools.partial(
        pl.kernel, mesh=_sc_mesh(), compiler_params=pltpu.CompilerParams(needs_layout_passes=False),
        out_type=jax.ShapeDtypeStruct((n_tok, 2 * half), F32),
        scratch_types=[pltpu.VMEM((2, PICK_GROUP, half), jnp.int32),
                       pltpu.SemaphoreType.DMA((2,))],
    )
    def k(v_hbm, i_hbm, c_hbm, o_hbm, rows, sems):
        def body(i_v, c_v, o_v):
            def fetch(j, slot):
                return pltpu.make_async_copy(v_hbm.at[i_v.at[j]], rows.at[slot], sems.at[slot])

            fetch(0, 0).start()

            def sub(j, carry0):
                slot = j % 2

                @pl.when(j + 1 < nsub)
                def _():
                    fetch(j + 1, 1 - slot).start()

                fetch(j, slot).wait()
                t = j // GROUPS_PER_TOK
                first = j % GROUPS_PER_TOK == 0
                cks = [c_v[j, pl.ds(kk * SC_LANES, SC_LANES)] for kk in range(PICK_GROUP)]

                def chunk_pair(c2, carry):
                    for cc in range(COMB_CHUNKS):
                        l = (c2 * COMB_CHUNKS + cc) * SC_LANES
                        ways = 2
                        los, his = [None] * ways, [None] * ways
                        for kk in range(PICK_GROUP):
                            lo, hi = _unpack(rows[slot, kk, pl.ds(l, SC_LANES)])
                            a, b_ = cks[kk] * lo, cks[kk] * hi
                            w = kk % ways
                            los[w] = a if los[w] is None else los[w] + a
                            his[w] = b_ if his[w] is None else his[w] + b_
                        tot_lo, tot_hi = los[0] + los[1], his[0] + his[1]
                        old_lo, old_hi = o_v[t, pl.ds(l, SC_LANES)], o_v[t, pl.ds(half + l, SC_LANES)]
                        o_v[t, pl.ds(l, SC_LANES)] = jnp.where(first, tot_lo, tot_lo + old_lo)
                        o_v[t, pl.ds(half + l, SC_LANES)] = jnp.where(first, tot_hi, tot_hi + old_hi)
                    return carry

                lax.fori_loop(0, n_chunks // COMB_CHUNKS, chunk_pair, 0)
                return carry0

            lax.fori_loop(0, nsub, sub, 0)

        _sc_pipeline(
            body, n_tok // TOK_STEP,
            [pl.BlockSpec((nsub, PICK_GROUP), lambda i: (i, 0)),
             pl.BlockSpec((nsub, PICK_GROUP * SC_LANES), lambda i: (i, 0))],
            [pl.BlockSpec((TOK_STEP, 2 * half), lambda i: (i, 0))],
            (i_hbm, c_hbm, o_hbm))

    return k(v_p, idx2, coef_b)


def _segment_matrix():
    r = lax.broadcasted_iota(jnp.int32, (PICK_GROUP * SC_LANES, PICK_GROUP), 0) // SC_LANES
    c = lax.broadcasted_iota(jnp.int32, (PICK_GROUP * SC_LANES, PICK_GROUP), 1)
    return (r == c).astype(F32)


def _coef_kernel(part_ref, gate_ref, o_ref):
    seg = _segment_matrix()
    act = jnp.dot(part_ref[...], seg, preferred_element_type=F32, precision=lax.Precision.HIGHEST)
    coef = gate_ref[...] * (0.5 * act * (1.0 + lax.erf(act * (2.0 ** -0.5))))
    o_ref[...] = lax.dot_general(coef, seg, (((1,), (1,)), ((), ())), preferred_element_type=F32,
                                 precision=lax.Precision.HIGHEST)


def peer_coef_tc(part, gate2):
    n_groups, width = part.shape
    tile = 1024
    return pl.pallas_call(
        _coef_kernel,
        grid=(n_groups // tile,),
        in_specs=[pl.BlockSpec((tile, width), lambda i: (i, 0)),
                  pl.BlockSpec((tile, PICK_GROUP), lambda i: (i, 0))],
        out_specs=pl.BlockSpec((tile, width), lambda i: (i, 0)),
        out_shape=jax.ShapeDtypeStruct((n_groups, width), F32),
        compiler_params=_cparams("parallel"), name="peer_coef",
    )(part, gate2)


def _final_kernel(x_ref, y_ref, mod_ref, g_ref, o_ref):
    x = x_ref[0] + mod_ref[0, 5:6, :] * y_ref[0]
    o_ref[0] = x * lax.rsqrt(jnp.mean(x * x, axis=-1, keepdims=True) + EPS) * g_ref[...]


def final_tc(x1, y, mod_l, final_g):
    b, l, d = x1.shape
    tok = pl.BlockSpec((1, TOK_TILE, d), lambda bi, i: (bi, i, 0))
    return pl.pallas_call(
        _final_kernel, grid=(b, l // TOK_TILE),
        in_specs=[tok, tok, pl.BlockSpec((1,) + mod_l.shape[1:], lambda bi, i: (bi, 0, 0)),
                  pl.BlockSpec((1, d), lambda bi, i: (0, 0))],
        out_specs=tok, out_shape=jax.ShapeDtypeStruct((b, l, d), F32),
        compiler_params=_cparams("parallel", "arbitrary"), name="final_norm",
    )(x1, y, mod_l, final_g.reshape(1, d))


def _pad_cols(w, n):
    return jnp.pad(w, ((0, 0), (0, n - w.shape[1])))


def _exact_zero(v):
    return jnp.minimum(jnp.abs(v), 0.0)


def forward(x, c, ctx, c_ctx, w_ada, b_ada, norm1_g, norm2_g, w_in, conv_w, dn_a_log,
            dn_dt_bias, dn_norm_g, gla_wa2, gla_ba, gla_norm_g, w_out, peer_wq, peer_keys,
            peer_u, peer_v, final_g):
    b, l, d = x.shape
    c_all = jnp.concatenate([c, c_ctx[None]], axis=0)
    c_all = jnp.pad(c_all, ((0, (-c_all.shape[0]) % SUBLANES), (0, 0)))
    mod = adaln_mod(c_all, w_ada, b_ada)
    mod_l = mod[:b].reshape(b, 6, d)
    mod_c = jnp.broadcast_to(mod[b].reshape(1, 6, d), (b, 6, d))
    o = DN_QKV
    hv = DN_HEADS * DN_DV
    w_dn_qkv, w_dn_z = w_in[:, :o], w_in[:, o:o + hv]
    w_dn_ba = _pad_cols(w_in[:, o + hv:DN_COLS], LANES)
    g0 = DN_COLS
    gqk, gv = 2 * GLA_HEADS * GLA_DK, GLA_HEADS * GLA_DV
    w_gl_qk, w_gl_v = w_in[:, g0:g0 + gqk], w_in[:, g0 + gqk:g0 + gqk + gv]
    w_gl_r = w_in[:, g0 + gqk + gv:g0 + gqk + 2 * gv]
    w_gl_lr = _pad_cols(w_in[:, g0 + gqk + 2 * gv:], LANES)
    w_lat = [w.astype(BF16) for w in (w_dn_qkv, w_dn_ba, w_gl_qk, w_gl_v, w_gl_lr, w_dn_z, w_gl_r)]
    w_ctx = w_lat[:5]
    w_la = jnp.zeros((2, LANES, GLA_HEADS * GLA_DK), F32)
    for dd in range(2):
        w_la = w_la.at[dd, dd * GLA_LR:(dd + 1) * GLA_LR].set(gla_wa2[dd])
    w_la = w_la.astype(BF16)
    b_la = gla_ba.reshape(2, 1, GLA_HEADS * GLA_DK)
    w_out_b, w_q_b = w_out.astype(BF16), peer_wq.astype(BF16)
    u_p, v_p = pack_table(peer_u), pack_table(peer_v)

    bg = b // BATCH_GROUPS
    n_tok = bg * l

    def mixer(xg, ctxg, mod_cg, mod_g, after_select, after_combine):
        mod_cg, mod_g = mod_cg + after_select, mod_g + after_select
        c_qkv, c_ba, c_qk, c_v, c_lr = in_projection(ctxg, norm1_g, mod_cg[:, 0:2], w_ctx)
        l_qkv, l_ba, l_qk, l_v, l_lr, l_z, l_r = in_projection(xg, norm1_g, mod_g[:, 0:2], w_lat)
        feat_c, feat_l = dn_features_tc(c_qkv, conv_w), dn_features_tc(l_qkv, conv_w)
        dn_f = dn_scan_tc(feat_c, feat_l, c_ba, l_ba, dn_a_log, dn_dt_bias, rev=False)
        dn_b = dn_scan_tc(feat_c, feat_l, c_ba, l_ba, dn_a_log, dn_dt_bias, rev=True)
        gl_f = gla_scan_tc(c_qk, c_v, c_lr, l_qk, l_v, l_lr, w_la, b_la, rev=False)
        gl_b = gla_scan_tc(c_qk, c_v, c_lr, l_qk, l_v, l_lr, w_la, b_la, rev=True)
        return mix_out_tc(xg, dn_f, dn_b, l_z, gl_f, gl_b, l_r, mod_g, dn_norm_g + after_combine,
                          gla_norm_g, norm2_g, w_out_b, w_q_b)

    def select_and_act(h2, q, after_coef):
        idx, gate = peer_select_tc(q.reshape(n_tok, -1), peer_keys + after_coef)
        idx2 = idx.reshape(n_tok * GROUPS_PER_TOK, PICK_GROUP)
        gate2 = gate.reshape(n_tok * GROUPS_PER_TOK, PICK_GROUP)
        return idx2, gate2, peer_act_partial_sc(u_p, idx2, h2.reshape(n_tok, d))

    def coef_and_combine(st, after_mixer):
        coef_b = peer_coef_tc(st["part"], st["gate2"] + after_mixer)
        st["y"] = peer_combine_sc(v_p, st["idx2"], coef_b)
        return _exact_zero(coef_b[0, 0])

    zero = jnp.zeros((), F32)
    slices, z_sel, z_coef = [], zero, zero
    for i in range(0, b, bg):
        g = len(slices)
        z_comb = _exact_zero(slices[g - 2]["y"][0, 0]) if g >= 2 else zero
        mod_g = mod_l[i:i + bg]
        x1, h2, q = mixer(x[i:i + bg], ctx[i:i + bg], mod_c[i:i + bg], mod_g, z_sel, z_comb)
        if g >= 1:
            z_coef = coef_and_combine(slices[g - 1], _exact_zero(x1[0, 0, 0]))
        idx2, gate2, part = select_and_act(h2, q, z_coef)
        z_sel = _exact_zero(gate2[0, 0])
        slices.append(dict(x1=x1, mod=mod_g, idx2=idx2, gate2=gate2, part=part))
    coef_and_combine(slices[-1], zero)
    outs = [final_tc(st["x1"], st["y"].reshape(bg, l, d), st["mod"], final_g) for st in slices]
    return jnp.concatenate(outs, axis=0)


def kernel(x, c, ctx, c_ctx, w_ada, b_ada, norm1_g, norm2_g, w_in, conv_w, dn_a_log,
           dn_dt_bias, dn_norm_g, gla_wa2, gla_ba, gla_norm_g, w_out, peer_wq, peer_keys,
           peer_u, peer_v, final_g):
    assert w_ada.shape[0] == 1, "single-layer block: the context stream is only consumed, never updated"
    return forward(x, c, ctx, c_ctx, w_ada[0], b_ada[0], norm1_g[0], norm2_g[0], w_in[0], conv_w[0],
                   dn_a_log[0], dn_dt_bias[0], dn_norm_g[0], gla_wa2[0], gla_ba[0], gla_norm_g[0],
                   w_out[0], peer_wq[0], peer_keys[0], peer_u[0], peer_v[0], final_g)
```

```python
import functools

import jax
import jax.numpy as jnp
from jax import lax
from jax.experimental import pallas as pl
from jax.experimental.pallas import tpu as pltpu
from jax.experimental.pallas import tpu_sc as plsc

GRID_W = 64
DN_HEADS = 4
DN_DK = 128
DN_DV = 128
CONV_W = 5
GLA_HEADS = 4
GLA_DK = 64
GLA_DV = 128
GLA_LR = 16
GLA_TAU = 16.0
CHUNK = 64
PEER_HEADS = 8
PEER_NKEYS = 128
PEER_DQ = 256
PEER_TOPK = 16
EPS = 1e-6
DN_QKV = 2 * DN_HEADS * DN_DK + DN_HEADS * DN_DV
DN_COLS = DN_QKV + DN_HEADS * DN_DV + 4 * DN_HEADS

SUBLANES = 8
LANES = 128
SC_LANES = 16
VMEM_LIMIT_BYTES = 48 * 1024 * 1024

TOK_TILE = 256
SELECT_TILE = 256
SCAN_BATCH = 2
GLA_SUB = 16
PICK_GROUP = 32
TOK_STEP = 8
COMB_CHUNKS = 2
ACT_UNROLL = 16
SLICE_BATCH = 2
PICKS = PEER_HEADS * PEER_TOPK
GROUPS_PER_TOK = PICKS // PICK_GROUP

F32 = jnp.float32
BF16 = jnp.bfloat16


def _cparams(*semantics):
    return pltpu.CompilerParams(dimension_semantics=semantics, vmem_limit_bytes=VMEM_LIMIT_BYTES)


def _dot(a, b):
    return jnp.dot(a.astype(BF16), b.astype(BF16), preferred_element_type=F32)


def _dot_nt(a, b):
    return lax.dot_general(a.astype(BF16), b.astype(BF16), (((1,), (1,)), ((), ())),
                           preferred_element_type=F32)


def _dot_tn(a, b):
    return lax.dot_general(a.astype(BF16), b.astype(BF16), (((0,), (0,)), ((), ())),
                           preferred_element_type=F32)


def _split(x):
    hi = x.astype(BF16)
    return hi, (x - hi.astype(F32)).astype(BF16)


def _mask_dot(mask_bf16, x):
    hi, lo = _split(x)
    return (jnp.dot(mask_bf16, hi, preferred_element_type=F32)
            + jnp.dot(mask_bf16, lo, preferred_element_type=F32))


def _softplus(x):
    return jnp.maximum(x, 0.0) + jnp.log(1.0 + jnp.exp(-jnp.abs(x)))


def _tri_masks(rev):
    r = lax.broadcasted_iota(jnp.int32, (CHUNK, CHUNK), 0)
    c = lax.broadcasted_iota(jnp.int32, (CHUNK, CHUNK), 1)
    d = (c - r) if rev else (r - c)
    return d >= 0, d > 0


def _mod_kernel(c_ref, w_ref, b_ref, o_ref):
    c = c_ref[...]
    s = c * jax.nn.sigmoid(c)
    o_ref[...] = jnp.dot(s, w_ref[...], preferred_element_type=F32,
                         precision=lax.Precision.HIGHEST) + b_ref[...]


def adaln_mod(c_all, w_ada, b_ada):
    r, d = c_all.shape
    n = w_ada.shape[1]
    tn = 512
    return pl.pallas_call(
        _mod_kernel, grid=(n // tn,),
        in_specs=[pl.BlockSpec((r, d), lambda j: (0, 0)),
                  pl.BlockSpec((d, tn), lambda j: (0, j)),
                  pl.BlockSpec((1, tn), lambda j: (0, j))],
        out_specs=pl.BlockSpec((r, tn), lambda j: (0, j)),
        out_shape=jax.ShapeDtypeStruct((r, n), F32),
        compiler_params=_cparams("arbitrary"), name="adaln_mod",
    )(c_all, w_ada, b_ada.reshape(1, n))


def _inproj_kernel(x_ref, g_ref, mod_ref, *refs):
    n_out = len(refs) // 2
    x = x_ref[0]
    y = x * lax.rsqrt(jnp.mean(x * x, axis=-1, keepdims=True) + EPS) * g_ref[...]
    h = (y * (1.0 + mod_ref[0, 1:2, :]) + mod_ref[0, 0:1, :]).astype(BF16)
    for w_ref, o_ref in zip(refs[:n_out], refs[n_out:]):
        o_ref[0] = jnp.dot(h, w_ref[...], preferred_element_type=F32)


def in_projection(x, norm_g, mod, weights):
    b, l, d = x.shape
    w_specs = [pl.BlockSpec(w.shape, lambda bi, i: (0, 0)) for w in weights]
    o_specs = [pl.BlockSpec((1, TOK_TILE, w.shape[1]), lambda bi, i: (bi, i, 0)) for w in weights]
    return pl.pallas_call(
        _inproj_kernel, grid=(b, l // TOK_TILE),
        in_specs=[pl.BlockSpec((1, TOK_TILE, d), lambda bi, i: (bi, i, 0)),
                  pl.BlockSpec((1, d), lambda bi, i: (0, 0)),
                  pl.BlockSpec((1, 2, d), lambda bi, i: (bi, 0, 0))] + w_specs,
        out_specs=o_specs,
        out_shape=[jax.ShapeDtypeStruct((b, l, w.shape[1]), F32) for w in weights],
        compiler_params=_cparams("parallel", "arbitrary"), name="in_projection",
    )(x, norm_g.reshape(1, d), mod, *weights)


def _dn_feature_kernel(x_ref, w_ref, o_ref):
    x = x_ref[0]
    n = x.shape[0]
    t = lax.broadcasted_iota(jnp.int32, (n, 1), 0)
    pad = CONV_W // 2
    acc = w_ref[0, pad:pad + 1, :] * x
    for j in range(CONV_W):
        s = j - pad
        if s == 0:
            continue
        xs = pltpu.roll(x, (-s) % n, axis=0)
        bad = (t < -s) if s < 0 else (t >= n - s)
        acc = acc + w_ref[0, j:j + 1, :] * jnp.where(bad, 0.0, xs)
    y = acc * jax.nn.sigmoid(acc)
    kind = pl.program_id(1) // DN_HEADS
    inv = lax.rsqrt(jnp.sum(y * y, axis=-1, keepdims=True) + EPS)
    scale = jnp.where(kind == 0, inv * DN_DK ** -0.5, jnp.where(kind == 1, inv, 1.0))
    o_ref[0] = y * scale


def dn_features_tc(qkv, conv_w):
    b, l, n = qkv.shape
    nblk = n // LANES
    w = jnp.zeros((nblk, SUBLANES, LANES), F32).at[:, :CONV_W].set(
        conv_w.reshape(CONV_W, nblk, LANES).transpose(1, 0, 2))
    return pl.pallas_call(
        _dn_feature_kernel, grid=(b, nblk),
        in_specs=[pl.BlockSpec((1, l, LANES), lambda bi, j: (bi, 0, j)),
                  pl.BlockSpec((1, SUBLANES, LANES), lambda bi, j: (j, 0, 0))],
        out_specs=pl.BlockSpec((1, l, LANES), lambda bi, j: (bi, 0, j)),
        out_shape=jax.ShapeDtypeStruct((b, l, n), F32),
        compiler_params=_cparams("parallel", "arbitrary"), name="dn_features",
    )(qkv, w)


def _scan_chunks(rev, n_ctx, n_lat):
    if rev:
        ctx = lambda j: jnp.maximum(n_ctx - 1 - j, 0)
        lat = lambda j: jnp.where(j < n_ctx, n_lat - 1, n_lat - 1 - (j - n_ctx))
    else:
        ctx = lambda j: jnp.minimum(j, n_ctx - 1)
        lat = lambda j: jnp.maximum(j - n_ctx, 0)
    return ctx, lat


def _dn_scan_kernel(rev, dirn, n_ctx, alog_ref, dtb_ref, fc_ref, fl_ref, bac_ref, bal_ref, o_ref, s_ref):
    step = pl.program_id(1)

    @pl.when(step == 0)
    def _():
        s_ref[...] = jnp.zeros_like(s_ref)

    in_ctx = step < n_ctx
    f = jnp.where(in_ctx, fc_ref[0], fl_ref[0])
    ba = jnp.where(in_ctx, bac_ref[0], bal_ref[0])
    nh, hd, n = DN_HEADS, DN_HEADS * DN_DK, DN_HEADS * CHUNK
    stack = lambda base, w: jnp.concatenate([f[:, base + h * w:base + (h + 1) * w] for h in range(nh)], axis=0)
    q_s, k_s, v_s = stack(0, DN_DK), stack(hd, DN_DK), stack(2 * hd, DN_DV)
    incl, _ = _tri_masks(rev)
    beta_all = jax.nn.sigmoid(ba)
    g_all = -jnp.exp(alog_ref[...]) * _softplus(ba + dtb_ref[...])
    gc_all = _mask_dot(incl.astype(BF16), g_all)
    gc_t = jnp.concatenate([gc_all, gc_all], axis=0).T
    g_tot = jnp.sum(g_all, axis=0, keepdims=True)
    cb = [dirn * nh + h for h in range(nh)]
    cg = [2 * nh + c for c in cb]
    col = lambda a, cs: jnp.concatenate([a[:, c:c + 1] for c in cs], axis=0)
    beta_c, gc_c = col(beta_all, cb), col(gc_all, cg)
    gtot_c = jnp.concatenate([jnp.broadcast_to(g_tot[:, c:c + 1], (CHUNK, 1)) for c in cg], axis=0)
    gc_r = jnp.concatenate([gc_t[c:c + 1, :CHUNK] for c in cg], axis=1)
    r = lax.broadcasted_iota(jnp.int32, (n, n), 0)
    c = lax.broadcasted_iota(jnp.int32, (n, n), 1)
    same = (r // CHUNK) == (c // CHUNK)
    d = (c - r) if rev else (r - c)
    incl_bd, strict_bd = same & (d >= 0), same & (d > 0)
    eye = (r == c).astype(F32)
    decay = jnp.where(incl_bd, jnp.exp(jnp.where(incl_bd, gc_c - gc_r, 0.0)), 0.0)
    kb_s = k_s * beta_c
    lower = jnp.where(strict_bd, _dot_nt(kb_s, k_s) * decay, 0.0)
    eg_c = jnp.exp(gc_c)
    inv = eye - lower
    pw = lower
    for _ in range(5):
        pw = _dot(pw, pw)
        inv = inv + _dot(inv, pw)
    sol = _dot(inv, jnp.concatenate([v_s * beta_c, kb_s * eg_c], axis=-1))
    u_s, w_s = sol[:, :DN_DV], sol[:, DN_DV:]
    k_dec = k_s * jnp.exp(gtot_c - gc_c)
    rb = lax.broadcasted_iota(jnp.int32, (n, DN_DK), 0) // CHUNK
    expand = lambda x: jnp.concatenate([jnp.where(rb == h, x, 0.0) for h in range(nh)], axis=1)
    s = s_ref[...]
    v_new = u_s - _dot(expand(w_s), s)
    a_qk = _dot_nt(q_s, k_s) * decay
    o_s = _dot(expand(q_s * eg_c), s) + _dot(a_qk, v_new)
    gl_rows = jnp.concatenate([jnp.broadcast_to(jnp.exp(g_tot[:, cc:cc + 1]), (DN_DK, 1)) for cc in cg], axis=0)
    s_ref[...] = s * gl_rows + _dot_tn(expand(k_dec), v_new)
    o_ref[0] = jnp.concatenate([o_s[h * CHUNK:(h + 1) * CHUNK] for h in range(nh)], axis=1)


def dn_scan_tc(feat_c, feat_l, ba_c, ba_l, a_log, dt_bias, rev):
    b, l, nf = feat_l.shape
    n_ctx, n_lat = feat_c.shape[1] // CHUNK, l // CHUNK
    dirn = 1 if rev else 0
    cc, lc = _scan_chunks(rev, n_ctx, n_lat)
    lanes = lambda p: jnp.zeros((1, LANES), F32).at[0, 2 * DN_HEADS:4 * DN_HEADS].set(p.reshape(-1))
    vec = pl.BlockSpec((1, LANES), lambda bi, j: (0, 0))
    return pl.pallas_call(
        functools.partial(_dn_scan_kernel, rev, dirn, n_ctx), grid=(b, n_ctx + n_lat),
        in_specs=[vec, vec,
                  pl.BlockSpec((1, CHUNK, nf), lambda bi, j: (bi, cc(j), 0)),
                  pl.BlockSpec((1, CHUNK, nf), lambda bi, j: (bi, lc(j), 0)),
                  pl.BlockSpec((1, CHUNK, LANES), lambda bi, j: (bi, cc(j), 0)),
                  pl.BlockSpec((1, CHUNK, LANES), lambda bi, j: (bi, lc(j), 0))],
        out_specs=pl.BlockSpec((1, CHUNK, DN_HEADS * DN_DV), lambda bi, j: (bi, lc(j), 0)),
        out_shape=jax.ShapeDtypeStruct((b, l, DN_HEADS * DN_DV), F32),
        scratch_shapes=[pltpu.VMEM((DN_HEADS * DN_DK, DN_DV), F32)],
        compiler_params=_cparams("parallel", "arbitrary"), name="dn_scan_bwd" if rev else "dn_scan_fwd",
    )(lanes(a_log), lanes(dt_bias), feat_c, feat_l, ba_c, ba_l)


def _from_grid_cols(blk, n):
    cols = blk.shape[1] // n
    return jnp.concatenate([blk[:, i * n:(i + 1) * n] for i in range(cols)], axis=0)


def _gla_scan_kernel(rev, dirn, n_ctx, qkc_ref, vc_ref, lrc_ref, qkl_ref, vl_ref, lrl_ref,
                     wla_ref, bla_ref, o_ref, s_ref):
    step = pl.program_id(1)

    @pl.when(step == 0)
    def _():
        s_ref[...] = jnp.zeros_like(s_ref)

    in_ctx = step < n_ctx
    hk, hv = GLA_HEADS * GLA_DK, GLA_HEADS * GLA_DV
    qk = jnp.where(in_ctx, qkc_ref[0], _from_grid_cols(qkl_ref[0], 2 * hk))
    vv = jnp.where(in_ctx, vc_ref[0], _from_grid_cols(vl_ref[0], hv))
    lr = jnp.where(in_ctx, lrc_ref[0], _from_grid_cols(lrl_ref[0], LANES))
    incl, _ = _tri_masks(rev)
    incl_b = incl.astype(BF16)
    pre = _dot(lr, wla_ref[0]) + bla_ref[0]
    la_all = -_softplus(-pre) * (1.0 / GLA_TAU)
    bc_all = _mask_dot(incl_b, la_all)
    b_tot_all = jnp.sum(la_all, axis=0, keepdims=True)
    outs = []
    for h in range(GLA_HEADS):
        q = qk[:, h * GLA_DK:(h + 1) * GLA_DK] * GLA_DK ** -0.5
        k = qk[:, hk + h * GLA_DK:hk + (h + 1) * GLA_DK]
        v = vv[:, h * GLA_DV:(h + 1) * GLA_DV]
        bc = bc_all[:, h * GLA_DK:(h + 1) * GLA_DK]
        b_tot = b_tot_all[:, h * GLA_DK:(h + 1) * GLA_DK]
        st = s_ref[h]
        o = _dot_nt(q * jnp.exp(bc), st)
        parts = []
        for i in range(CHUNK // GLA_SUB):
            lo_r, hi_r = i * GLA_SUB, (i + 1) * GLA_SUB
            if rev:
                ref = bc[hi_r - 1:hi_r]
                c0, c1 = lo_r, CHUNK
            else:
                ref = bc[lo_r:lo_r + 1]
                c0, c1 = 0, hi_r
            qi = q[lo_r:hi_r] * jnp.exp(bc[lo_r:hi_r] - ref)
            ki = k[c0:c1] * jnp.exp(ref - bc[c0:c1])
            att = _dot_nt(qi, ki)
            rg = lax.broadcasted_iota(jnp.int32, (GLA_SUB, c1 - c0), 0) + lo_r
            cg = lax.broadcasted_iota(jnp.int32, (GLA_SUB, c1 - c0), 1) + c0
            keep = (cg >= rg) if rev else (cg <= rg)
            parts.append(_dot(jnp.where(keep, att, 0.0), v[c0:c1]))
        outs.append(o + jnp.concatenate(parts, axis=0))
        k_dec = k * jnp.exp(b_tot - bc)
        s_ref[h] = st * jnp.exp(b_tot) + _dot_tn(v, k_dec)
    o = jnp.concatenate(outs, axis=-1)
    rows = o_ref.shape[1]
    o_ref[0] = jnp.concatenate([o[i * rows:(i + 1) * rows] for i in range(CHUNK // rows)], axis=-1)


def gla_scan_tc(qk_c, v_c, lr_c, qk_l, v_l, lr_l, w_la, b_la, rev):
    b, l, _ = qk_l.shape
    rows = l // GRID_W
    cols = CHUNK // rows
    n_ctx, n_lat = qk_c.shape[1] // CHUNK, l // CHUNK
    dirn = 1 if rev else 0
    cc, lc = _scan_chunks(rev, n_ctx, n_lat)
    hv = GLA_HEADS * GLA_DV
    ctx_blk = lambda a: pl.BlockSpec((1, CHUNK, a.shape[2]), lambda bi, j: (bi, cc(j), 0))
    lat_blk = lambda n: pl.BlockSpec((1, rows, cols * n), lambda bi, j: (bi, 0, lc(j)))
    grid_view = lambda a: a.reshape(b, rows, GRID_W * a.shape[2])
    out = pl.pallas_call(
        functools.partial(_gla_scan_kernel, rev, dirn, n_ctx), grid=(b, n_ctx + n_lat),
        in_specs=[ctx_blk(qk_c), ctx_blk(v_c), ctx_blk(lr_c),
                  lat_blk(qk_l.shape[2]), lat_blk(v_l.shape[2]), lat_blk(lr_l.shape[2]),
                  pl.BlockSpec((1,) + w_la.shape[1:], lambda bi, j: (dirn, 0, 0)),
                  pl.BlockSpec((1,) + b_la.shape[1:], lambda bi, j: (dirn, 0, 0))],
        out_specs=lat_blk(hv),
        out_shape=jax.ShapeDtypeStruct((b, rows, GRID_W * hv), F32),
        scratch_shapes=[pltpu.VMEM((GLA_HEADS, GLA_DV, GLA_DK), F32)],
        compiler_params=_cparams("parallel", "arbitrary"), name="gla_scan_bwd" if rev else "gla_scan_fwd",
    )(qk_c, v_c, lr_c, grid_view(qk_l), grid_view(v_l), grid_view(lr_l), w_la, b_la)
    return out.reshape(b, l, hv)


def _head_norm_gate(o, gate, g, n_heads, dv):
    parts = []
    for h in range(n_heads):
        oh = o[:, h * dv:(h + 1) * dv]
        gh = gate[:, h * dv:(h + 1) * dv]
        yh = oh * lax.rsqrt(jnp.mean(oh * oh, axis=-1, keepdims=True) + EPS) * g
        parts.append(yh * (gh * jax.nn.sigmoid(gh)))
    return parts


def _mix_out_kernel(x_ref, dnf_ref, dnb_ref, z_ref, glf_ref, glb_ref, r_ref, mod_ref, dng_ref,
                    glg_ref, n2g_ref, wout_ref, wq_ref, x1_ref, h2_ref, q_ref):
    parts = (_head_norm_gate(dnf_ref[0] + dnb_ref[0], z_ref[0], dng_ref[...], DN_HEADS, DN_DV)
             + _head_norm_gate(glf_ref[0] + glb_ref[0], r_ref[0], glg_ref[...], GLA_HEADS, GLA_DV))
    y = jnp.dot(jnp.concatenate(parts, axis=-1).astype(BF16), wout_ref[...], preferred_element_type=F32)
    x1 = x_ref[0] + mod_ref[0, 2:3, :] * y
    x1_ref[0] = x1
    n = x1 * lax.rsqrt(jnp.mean(x1 * x1, axis=-1, keepdims=True) + EPS) * n2g_ref[...]
    h2 = n * (1.0 + mod_ref[0, 4:5, :]) + mod_ref[0, 3:4, :]
    h2_ref[0] = h2
    q_ref[0] = jnp.dot(h2.astype(BF16), wq_ref[...], preferred_element_type=F32)


def mix_out_tc(x, dn_f, dn_b, z, gl_f, gl_b, r, mod_l, dn_g, gla_g, n2_g, w_out, w_q):
    b, l, d = x.shape
    tok = lambda n: pl.BlockSpec((1, TOK_TILE, n), lambda bi, i: (bi, i, 0))
    full = lambda a: pl.BlockSpec(a.shape, lambda bi, i: (0,) * a.ndim)
    dn_g, gla_g, n2_g = dn_g.reshape(1, -1), gla_g.reshape(1, -1), n2_g.reshape(1, -1)
    nq = w_q.shape[1]
    return pl.pallas_call(
        _mix_out_kernel, grid=(b, l // TOK_TILE),
        in_specs=[tok(d), tok(dn_f.shape[2]), tok(dn_b.shape[2]), tok(z.shape[2]),
                  tok(gl_f.shape[2]), tok(gl_b.shape[2]), tok(r.shape[2]),
                  pl.BlockSpec((1,) + mod_l.shape[1:], lambda bi, i: (bi, 0, 0)),
                  full(dn_g), full(gla_g), full(n2_g), full(w_out), full(w_q)],
        out_specs=[tok(d), tok(d), tok(nq)],
        out_shape=[jax.ShapeDtypeStruct((b, l, d), F32), jax.ShapeDtypeStruct((b, l, d), F32),
                   jax.ShapeDtypeStruct((b, l, nq), F32)],
        compiler_params=_cparams("parallel", "arbitrary"), name="mix_out",
    )(x, dn_f, dn_b, z, gl_f, gl_b, r, mod_l, dn_g, gla_g, n2_g, w_out, w_q)


def _top_rows(s, k, payload=None):
    n = s.shape[0]
    row = lax.broadcasted_iota(jnp.int32, s.shape, 0).astype(F32)
    vals, picked = [], []
    for _ in range(k):
        m = jnp.max(s, axis=0, keepdims=True)
        first = jnp.min(jnp.where(s == m, row, float(n)), axis=0, keepdims=True)
        sel = row == first
        vals.append(m)
        if payload is None:
            picked.append(first)
        else:
            picked.append(jnp.max(jnp.where(sel, payload, -1.0), axis=0, keepdims=True))
        s = jnp.where(sel, -jnp.inf, s)
    return jnp.concatenate(vals, axis=0), jnp.concatenate(picked, axis=0)


def _candidate_rows(s0, i0, s1, i1):
    k = s0.shape[0]
    wide = SUBLANES
    blocks_s = [s0[0:1] + s1]
    blocks_i = [i0[0:1] * float(PEER_NKEYS) + i1]
    col = lax.broadcasted_iota(jnp.int32, (wide, s0.shape[1]), 0)
    for i in range(1, wide):
        keep = col < (k // (i + 1))
        blocks_s.append(jnp.where(keep, s0[i:i + 1] + s1[0:wide], -jnp.inf))
        blocks_i.append(i0[i:i + 1] * float(PEER_NKEYS) + i1[0:wide])
    blocks_s.append(s0[wide:k] + s1[0:1])
    blocks_i.append(i0[wide:k] * float(PEER_NKEYS) + i1[0:1])
    return jnp.concatenate(blocks_s, axis=0), jnp.concatenate(blocks_i, axis=0)


def _select_kernel(q_ref, k_ref, idx_ref, gate_ref, idx_s, gate_s):
    half = PEER_DQ // 2

    def head(h, carry):
        tops = []
        for p in range(2):
            qp = q_ref[:, pl.ds(pl.multiple_of(h * PEER_DQ + p * half, half), half)]
            s = lax.dot_general(k_ref[h, p], qp, (((1,), (1,)), ((), ())),
                                preferred_element_type=F32,
                                precision=lax.Precision.HIGHEST)
            tops.append(_top_rows(s, PEER_TOPK))
        (s0, i0), (s1, i1) = tops
        cand_s, cand_i = _candidate_rows(s0, i0, s1, i1)
        best_s, idx = _top_rows(cand_s, PEER_TOPK, payload=cand_i)
        e = jnp.exp(best_s - best_s[0:1])
        r0 = pl.multiple_of(h * PEER_TOPK, PEER_TOPK)
        idx_s[pl.ds(r0, PEER_TOPK), :] = idx
        gate_s[pl.ds(r0, PEER_TOPK), :] = e / jnp.sum(e, axis=0, keepdims=True)
        return carry

    lax.fori_loop(0, PEER_HEADS, head, 0)
    idx_ref[...] = idx_s[...].T.astype(jnp.int32)
    gate_ref[...] = gate_s[...].T


def peer_select_tc(q, keys):
    n_tok = q.shape[0]
    out_spec = pl.BlockSpec((SELECT_TILE, PICKS), lambda i: (i, 0))
    return pl.pallas_call(
        _select_kernel,
        grid=(n_tok // SELECT_TILE,),
        in_specs=[pl.BlockSpec((SELECT_TILE, q.shape[1]), lambda i: (i, 0)),
                  pl.BlockSpec(keys.shape, lambda i: (0, 0, 0, 0))],
        out_specs=[out_spec, out_spec],
        out_shape=[jax.ShapeDtypeStruct((n_tok, PICKS), jnp.int32),
                   jax.ShapeDtypeStruct((n_tok, PICKS), F32)],
        scratch_shapes=[pltpu.VMEM((PICKS, SELECT_TILE), F32), pltpu.VMEM((PICKS, SELECT_TILE), F32)],
        compiler_params=_cparams("parallel"), name="peer_select",
    )(q, keys)


def _sc_mesh():
    return plsc.VectorSubcoreMesh(core_axis_name="c", subcore_axis_name="s")


def _sc_pipeline(body, n_steps, in_specs, out_specs, operands):
    pltpu.emit_pipeline(
        body, grid=(n_steps,), in_specs=in_specs, out_specs=out_specs,
        core_axis_name=("c", "s"), dimension_semantics=(pltpu.PARALLEL,),
        trace_scopes=False,
    )(*operands)


def pack_table(t):
    half = t.shape[1] // 2
    bits = lax.bitcast_convert_type(t, jnp.uint32)
    lo_bits, hi_bits = bits[:, :half], bits[:, half:]
    low = (lo_bits + jnp.uint32(0x7FFF) + ((lo_bits >> 16) & jnp.uint32(1))) >> 16
    mag = hi_bits & jnp.uint32(0x7FFFFFFF)
    top = jnp.where(mag >= jnp.uint32(0x10000),
                    (mag - low + jnp.uint32(0x8000)) & jnp.uint32(0xFFFF0000), jnp.uint32(0))
    word = (hi_bits & jnp.uint32(0x80000000)) | top | low
    return lax.bitcast_convert_type(word, jnp.int32)


def _unpack(w):
    return plsc.bitcast(w << 16, F32), plsc.bitcast(w, F32)


def peer_act_partial_sc(u_p, idx2, h):
    n_groups, (n_tok, d) = idx2.shape[0], h.shape
    half = d // 2
    nsub = TOK_STEP * GROUPS_PER_TOK
    n_chunks = half // SC_LANES

    @functools.partial(
        pl.kernel, mesh=_sc_mesh(), compiler_params=pltpu.CompilerParams(needs_layout_passes=False),
        out_type=jax.ShapeDtypeStruct((n_groups, PICK_GROUP * SC_LANES), F32),
        scratch_types=[pltpu.VMEM((2, PICK_GROUP, half), jnp.int32),
                       pltpu.SemaphoreType.DMA((2,))],
    )
    def k(u_hbm, i_hbm, h_hbm, o_hbm, rows, sems):
        def body(i_v, h_v, o_v):
            def fetch(j, slot):
                return pltpu.make_async_copy(u_hbm.at[i_v.at[j]], rows.at[slot], sems.at[slot])

            fetch(0, 0).start()

            def sub(j, carry):
                slot = j % 2

                @pl.when(j + 1 < nsub)
                def _():
                    fetch(j + 1, 1 - slot).start()

                fetch(j, slot).wait()
                t = j // GROUPS_PER_TOK

                def picks(g, carry2):
                    kb = g * ACT_UNROLL
                    accs = [None] * ACT_UNROLL
                    for c in range(n_chunks):
                        h_lo = h_v[t, pl.ds(c * SC_LANES, SC_LANES)]
                        h_hi = h_v[t, pl.ds(half + c * SC_LANES, SC_LANES)]
                        for i in range(ACT_UNROLL):
                            lo, hi = _unpack(rows[slot, kb + i, pl.ds(c * SC_LANES, SC_LANES)])
                            p = lo * h_lo + hi * h_hi
                            accs[i] = p if accs[i] is None else accs[i] + p
                    for i in range(ACT_UNROLL):
                        o_v[j, pl.ds((kb + i) * SC_LANES, SC_LANES)] = accs[i]
                    return carry2

                lax.fori_loop(0, PICK_GROUP // ACT_UNROLL, picks, 0)
                return carry

            lax.fori_loop(0, nsub, sub, 0)

        _sc_pipeline(
            body, n_tok // TOK_STEP,
            [pl.BlockSpec((nsub, PICK_GROUP), lambda i: (i, 0)),
             pl.BlockSpec((TOK_STEP, d), lambda i: (i, 0))],
            [pl.BlockSpec((nsub, PICK_GROUP * SC_LANES), lambda i: (i, 0))],
            (i_hbm, h_hbm, o_hbm))

    return k(u_p, idx2, h)


def peer_combine_sc(v_p, idx2, coef_b):
    n_groups = idx2.shape[0]
    half = v_p.shape[1]
    n_tok = n_groups // GROUPS_PER_TOK
    nsub = TOK_STEP * GROUPS_PER_TOK
    n_chunks = half // SC_LANES

    @functools.partial(
        pl.kernel, mesh=_sc_mesh(), compiler_params=pltpu.CompilerParams(needs_layout_passes=False),
        out_type=jax.ShapeDtypeStruct((n_tok, 2 * half), F32),
        scratch_types=[pltpu.VMEM((2, PICK_GROUP, half), jnp.int32),
                       pltpu.SemaphoreType.DMA((2,))],
    )
    def k(v_hbm, i_hbm, c_hbm, o_hbm, rows, sems):
        def body(i_v, c_v, o_v):
            def fetch(j, slot):
                return pltpu.make_async_copy(v_hbm.at[i_v.at[j]], rows.at[slot], sems.at[slot])

            fetch(0, 0).start()

            def sub(j, carry0):
                slot = j % 2

                @pl.when(j + 1 < nsub)
                def _():
                    fetch(j + 1, 1 - slot).start()

                fetch(j, slot).wait()
                t = j // GROUPS_PER_TOK
                first = j % GROUPS_PER_TOK == 0
                cks = [c_v[j, pl.ds(kk * SC_LANES, SC_LANES)] for kk in range(PICK_GROUP)]

                def chunk_pair(c2, carry):
                    for cc in range(COMB_CHUNKS):
                        l = (c2 * COMB_CHUNKS + cc) * SC_LANES
                        ways = 2
                        los, his = [None] * ways, [None] * ways
                        for kk in range(PICK_GROUP):
                            lo, hi = _unpack(rows[slot, kk, pl.ds(l, SC_LANES)])
                            a, b_ = cks[kk] * lo, cks[kk] * hi
                            w = kk % ways
                            los[w] = a if los[w] is None else los[w] + a
                            his[w] = b_ if his[w] is None else his[w] + b_
                        tot_lo, tot_hi = los[0] + los[1], his[0] + his[1]
                        old_lo, old_hi = o_v[t, pl.ds(l, SC_LANES)], o_v[t, pl.ds(half + l, SC_LANES)]
                        o_v[t, pl.ds(l, SC_LANES)] = jnp.where(first, tot_lo, tot_lo + old_lo)
                        o_v[t, pl.ds(half + l, SC_LANES)] = jnp.where(first, tot_hi, tot_hi + old_hi)
                    return carry

                lax.fori_loop(0, n_chunks // COMB_CHUNKS, chunk_pair, 0)
                return carry0

            lax.fori_loop(0, nsub, sub, 0)

        _sc_pipeline(
            body, n_tok // TOK_STEP,
            [pl.BlockSpec((nsub, PICK_GROUP), lambda i: (i, 0)),
             pl.BlockSpec((nsub, PICK_GROUP * SC_LANES), lambda i: (i, 0))],
            [pl.BlockSpec((TOK_STEP, 2 * half), lambda i: (i, 0))],
            (i_hbm, c_hbm, o_hbm))

    return k(v_p, idx2, coef_b)


def _segment_matrix():
    r = lax.broadcasted_iota(jnp.int32, (PICK_GROUP * SC_LANES, PICK_GROUP), 0) // SC_LANES
    c = lax.broadcasted_iota(jnp.int32, (PICK_GROUP * SC_LANES, PICK_GROUP), 1)
    return (r == c).astype(F32)


def _coef_kernel(part_ref, gate_ref, o_ref):
    seg = _segment_matrix()
    act = jnp.dot(part_ref[...], seg, preferred_element_type=F32, precision=lax.Precision.HIGHEST)
    coef = gate_ref[...] * (0.5 * act * (1.0 + lax.erf(act * (2.0 ** -0.5))))
    o_ref[...] = lax.dot_general(coef, seg, (((1,), (1,)), ((), ())), preferred_element_type=F32,
                                 precision=lax.Precision.HIGHEST)


def peer_coef_tc(part, gate2):
    n_groups, width = part.shape
    tile = 1024
    return pl.pallas_call(
        _coef_kernel,
        grid=(n_groups // tile,),
        in_specs=[pl.BlockSpec((tile, width), lambda i: (i, 0)),
                  pl.BlockSpec((tile, PICK_GROUP), lambda i: (i, 0))],
        out_specs=pl.BlockSpec((tile, width), lambda i: (i, 0)),
        out_shape=jax.ShapeDtypeStruct((n_groups, width), F32),
        compiler_params=_cparams("parallel"), name="peer_coef",
    )(part, gate2)


def _final_kernel(x_ref, y_ref, mod_ref, g_ref, o_ref):
    x = x_ref[0] + mod_ref[0, 5:6, :] * y_ref[0]
    o_ref[0] = x * lax.rsqrt(jnp.mean(x * x, axis=-1, keepdims=True) + EPS) * g_ref[...]


def final_tc(x1, y, mod_l, final_g):
    b, l, d = x1.shape
    tok = pl.BlockSpec((1, TOK_TILE, d), lambda bi, i: (bi, i, 0))
    return pl.pallas_call(
        _final_kernel, grid=(b, l // TOK_TILE),
        in_specs=[tok, tok, pl.BlockSpec((1,) + mod_l.shape[1:], lambda bi, i: (bi, 0, 0)),
                  pl.BlockSpec((1, d), lambda bi, i: (0, 0))],
        out_specs=tok, out_shape=jax.ShapeDtypeStruct((b, l, d), F32),
        compiler_params=_cparams("parallel", "arbitrary"), name="final_norm",
    )(x1, y, mod_l, final_g.reshape(1, d))


def _pad_cols(w, n):
    return jnp.pad(w, ((0, 0), (0, n - w.shape[1])))


def _slice_sizes(b):
    if b % SLICE_BATCH or SLICE_BATCH % 2 or b < 2 * SLICE_BATCH:
        return [1] * b
    half = SLICE_BATCH // 2
    return [half] + [SLICE_BATCH] * (b // SLICE_BATCH - 1) + [half]


def _exact_zero(v):
    return jnp.minimum(jnp.abs(v), 0.0)


def forward(x, c, ctx, c_ctx, w_ada, b_ada, norm1_g, norm2_g, w_in, conv_w, dn_a_log,
            dn_dt_bias, dn_norm_g, gla_wa2, gla_ba, gla_norm_g, w_out, peer_wq, peer_keys,
            peer_u, peer_v, final_g):
    b, l, d = x.shape
    c_all = jnp.concatenate([c, c_ctx[None]], axis=0)
    c_all = jnp.pad(c_all, ((0, (-c_all.shape[0]) % SUBLANES), (0, 0)))
    mod = adaln_mod(c_all, w_ada, b_ada)
    mod_l = mod[:b].reshape(b, 6, d)
    mod_c = jnp.broadcast_to(mod[b].reshape(1, 6, d), (b, 6, d))
    o = DN_QKV
    hv = DN_HEADS * DN_DV
    w_dn_qkv, w_dn_z = w_in[:, :o], w_in[:, o:o + hv]
    w_dn_ba = _pad_cols(w_in[:, o + hv:DN_COLS], LANES)
    g0 = DN_COLS
    gqk, gv = 2 * GLA_HEADS * GLA_DK, GLA_HEADS * GLA_DV
    w_gl_qk, w_gl_v = w_in[:, g0:g0 + gqk], w_in[:, g0 + gqk:g0 + gqk + gv]
    w_gl_r = w_in[:, g0 + gqk + gv:g0 + gqk + 2 * gv]
    w_gl_lr = _pad_cols(w_in[:, g0 + gqk + 2 * gv:], LANES)
    w_lat = [w.astype(BF16) for w in (w_dn_qkv, w_dn_ba, w_gl_qk, w_gl_v, w_gl_lr, w_dn_z, w_gl_r)]
    w_ctx = w_lat[:5]
    w_la = jnp.zeros((2, LANES, GLA_HEADS * GLA_DK), F32)
    for dd in range(2):
        w_la = w_la.at[dd, dd * GLA_LR:(dd + 1) * GLA_LR].set(gla_wa2[dd])
    w_la = w_la.astype(BF16)
    b_la = gla_ba.reshape(2, 1, GLA_HEADS * GLA_DK)
    w_out_b, w_q_b = w_out.astype(BF16), peer_wq.astype(BF16)
    u_p, v_p = pack_table(peer_u), pack_table(peer_v)

    def mixer(xg, ctxg, mod_cg, mod_g, after_select, after_combine):
        mod_cg, mod_g = mod_cg + after_select, mod_g + after_select
        c_qkv, c_ba, c_qk, c_v, c_lr = in_projection(ctxg, norm1_g, mod_cg[:, 0:2], w_ctx)
        l_qkv, l_ba, l_qk, l_v, l_lr, l_z, l_r = in_projection(xg, norm1_g, mod_g[:, 0:2], w_lat)
        feat_c, feat_l = dn_features_tc(c_qkv, conv_w), dn_features_tc(l_qkv, conv_w)
        dn_f = dn_scan_tc(feat_c, feat_l, c_ba, l_ba, dn_a_log, dn_dt_bias, rev=False)
        dn_b = dn_scan_tc(feat_c, feat_l, c_ba, l_ba, dn_a_log, dn_dt_bias, rev=True)
        gl_f = gla_scan_tc(c_qk, c_v, c_lr, l_qk, l_v, l_lr, w_la, b_la, rev=False)
        gl_b = gla_scan_tc(c_qk, c_v, c_lr, l_qk, l_v, l_lr, w_la, b_la, rev=True)
        return mix_out_tc(xg, dn_f, dn_b, l_z, gl_f, gl_b, l_r, mod_g, dn_norm_g + after_combine,
                          gla_norm_g, norm2_g, w_out_b, w_q_b)

    def select_and_act(h2, q, after_coef):
        n_tok = h2.shape[0] * l
        idx, gate = peer_select_tc(q.reshape(n_tok, -1), peer_keys + after_coef)
        idx2 = idx.reshape(n_tok * GROUPS_PER_TOK, PICK_GROUP)
        gate2 = gate.reshape(n_tok * GROUPS_PER_TOK, PICK_GROUP)
        return idx2, gate2, peer_act_partial_sc(u_p, idx2, h2.reshape(n_tok, d))

    def coef_and_combine(st, after_mixer):
        coef_b = peer_coef_tc(st["part"], st["gate2"] + after_mixer)
        st["y"] = peer_combine_sc(v_p, st["idx2"], coef_b)
        return _exact_zero(coef_b[0, 0])

    zero = jnp.zeros((), F32)
    slices, z_sel, z_coef = [], zero, zero
    i = 0
    for bg in _slice_sizes(b):
        g = len(slices)
        z_comb = _exact_zero(slices[g - 2]["y"][0, 0]) if g >= 2 else zero
        mod_g = mod_l[i:i + bg]
        x1, h2, q = mixer(x[i:i + bg], ctx[i:i + bg], mod_c[i:i + bg], mod_g, z_sel, z_comb)
        i += bg
        if g >= 1:
            z_coef = coef_and_combine(slices[g - 1], _exact_zero(x1[0, 0, 0]))
        idx2, gate2, part = select_and_act(h2, q, z_coef)
        z_sel = _exact_zero(gate2[0, 0])
        slices.append(dict(x1=x1, mod=mod_g, idx2=idx2, gate2=gate2, part=part))
    coef_and_combine(slices[-1], zero)
    outs = [final_tc(st["x1"], st["y"].reshape(st["x1"].shape), st["mod"], final_g) for st in slices]
    return jnp.concatenate(outs, axis=0)


def kernel(x, c, ctx, c_ctx, w_ada, b_ada, norm1_g, norm2_g, w_in, conv_w, dn_a_log,
           dn_dt_bias, dn_norm_g, gla_wa2, gla_ba, gla_norm_g, w_out, peer_wq, peer_keys,
           peer_u, peer_v, final_g):
    assert w_ada.shape[0] == 1, "single-layer block: the context stream is only consumed, never updated"
    return forward(x, c, ctx, c_ctx, w_ada[0], b_ada[0], norm1_g[0], norm2_g[0], w_in[0], conv_w[0],
                   dn_a_log[0], dn_dt_bias[0], dn_norm_g[0], gla_wa2[0], gla_ba[0], gla_norm_g[0],
                   w_out[0], peer_wq[0], peer_keys[0], peer_u[0], peer_v[0], final_g)
```

```python
import functools

import jax
import jax.numpy as jnp
from jax import lax
from jax.experimental import pallas as pl
from jax.experimental.pallas import tpu as pltpu
from jax.experimental.pallas import tpu_sc as plsc

GRID_W = 64
DN_HEADS = 4
DN_DK = 128
DN_DV = 128
CONV_W = 5
GLA_HEADS = 4
GLA_DK = 64
GLA_DV = 128
GLA_LR = 16
GLA_TAU = 16.0
CHUNK = 64
PEER_HEADS = 8
PEER_NKEYS = 128
PEER_DQ = 256
PEER_TOPK = 16
EPS = 1e-6
DN_QKV = 2 * DN_HEADS * DN_DK + DN_HEADS * DN_DV
DN_COLS = DN_QKV + DN_HEADS * DN_DV + 4 * DN_HEADS

SUBLANES = 8
LANES = 128
SC_LANES = 16
VMEM_LIMIT_BYTES = 48 * 1024 * 1024

TOK_TILE = 256
SELECT_TILE = 256
SCAN_BATCH = 2
GLA_SUB = 16
PICK_GROUP = 32
TOK_STEP = 8
COMB_CHUNKS = 2
ACT_UNROLL = 16
SLICE_BATCH = 2
PICKS = PEER_HEADS * PEER_TOPK
GROUPS_PER_TOK = PICKS // PICK_GROUP

F32 = jnp.float32
BF16 = jnp.bfloat16


def _cparams(*semantics):
    return pltpu.CompilerParams(dimension_semantics=semantics, vmem_limit_bytes=VMEM_LIMIT_BYTES)


def _dot(a, b):
    return jnp.dot(a.astype(BF16), b.astype(BF16), preferred_element_type=F32)


def _dot_nt(a, b):
    return lax.dot_general(a.astype(BF16), b.astype(BF16), (((1,), (1,)), ((), ())),
                           preferred_element_type=F32)


def _dot_tn(a, b):
    return lax.dot_general(a.astype(BF16), b.astype(BF16), (((0,), (0,)), ((), ())),
                           preferred_element_type=F32)


def _split(x):
    hi = x.astype(BF16)
    return hi, (x - hi.astype(F32)).astype(BF16)


def _mask_dot(mask_bf16, x):
    hi, lo = _split(x)
    return (jnp.dot(mask_bf16, hi, preferred_element_type=F32)
            + jnp.dot(mask_bf16, lo, preferred_element_type=F32))


def _softplus(x):
    return jnp.maximum(x, 0.0) + jnp.log(1.0 + jnp.exp(-jnp.abs(x)))


def _tri_masks(rev):
    r = lax.broadcasted_iota(jnp.int32, (CHUNK, CHUNK), 0)
    c = lax.broadcasted_iota(jnp.int32, (CHUNK, CHUNK), 1)
    d = (c - r) if rev else (r - c)
    return d >= 0, d > 0


def _mod_kernel(c_ref, w_ref, b_ref, o_ref):
    c = c_ref[...]
    s = c * jax.nn.sigmoid(c)
    o_ref[...] = jnp.dot(s, w_ref[...], preferred_element_type=F32,
                         precision=lax.Precision.HIGHEST) + b_ref[...]


def adaln_mod(c_all, w_ada, b_ada):
    r, d = c_all.shape
    n = w_ada.shape[1]
    tn = 512
    return pl.pallas_call(
        _mod_kernel, grid=(n // tn,),
        in_specs=[pl.BlockSpec((r, d), lambda j: (0, 0)),
                  pl.BlockSpec((d, tn), lambda j: (0, j)),
                  pl.BlockSpec((1, tn), lambda j: (0, j))],
        out_specs=pl.BlockSpec((r, tn), lambda j: (0, j)),
        out_shape=jax.ShapeDtypeStruct((r, n), F32),
        compiler_params=_cparams("arbitrary"), name="adaln_mod",
    )(c_all, w_ada, b_ada.reshape(1, n))


def _inproj_kernel(x_ref, g_ref, mod_ref, *refs):
    n_out = len(refs) // 2
    x = x_ref[0]
    y = x * lax.rsqrt(jnp.mean(x * x, axis=-1, keepdims=True) + EPS) * g_ref[...]
    h = (y * (1.0 + mod_ref[0, 1:2, :]) + mod_ref[0, 0:1, :]).astype(BF16)
    for w_ref, o_ref in zip(refs[:n_out], refs[n_out:]):
        o_ref[0] = jnp.dot(h, w_ref[...], preferred_element_type=F32)


def in_projection(x, norm_g, mod, weights):
    b, l, d = x.shape
    w_specs = [pl.BlockSpec(w.shape, lambda bi, i: (0, 0)) for w in weights]
    o_specs = [pl.BlockSpec((1, TOK_TILE, w.shape[1]), lambda bi, i: (bi, i, 0)) for w in weights]
    return pl.pallas_call(
        _inproj_kernel, grid=(b, l // TOK_TILE),
        in_specs=[pl.BlockSpec((1, TOK_TILE, d), lambda bi, i: (bi, i, 0)),
                  pl.BlockSpec((1, d), lambda bi, i: (0, 0)),
                  pl.BlockSpec((1, 2, d), lambda bi, i: (bi, 0, 0))] + w_specs,
        out_specs=o_specs,
        out_shape=[jax.ShapeDtypeStruct((b, l, w.shape[1]), F32) for w in weights],
        compiler_params=_cparams("parallel", "arbitrary"), name="in_projection",
    )(x, norm_g.reshape(1, d), mod, *weights)


def _dn_feature_kernel(x_ref, w_ref, o_ref):
    x = x_ref[0]
    n = x.shape[0]
    t = lax.broadcasted_iota(jnp.int32, (n, 1), 0)
    pad = CONV_W // 2
    acc = w_ref[0, pad:pad + 1, :] * x
    for j in range(CONV_W):
        s = j - pad
        if s == 0:
            continue
        xs = pltpu.roll(x, (-s) % n, axis=0)
        bad = (t < -s) if s < 0 else (t >= n - s)
        acc = acc + w_ref[0, j:j + 1, :] * jnp.where(bad, 0.0, xs)
    y = acc * jax.nn.sigmoid(acc)
    kind = pl.program_id(1) // DN_HEADS
    inv = lax.rsqrt(jnp.sum(y * y, axis=-1, keepdims=True) + EPS)
    scale = jnp.where(kind == 0, inv * DN_DK ** -0.5, jnp.where(kind == 1, inv, 1.0))
    o_ref[0] = y * scale


def dn_features_tc(qkv, conv_w):
    b, l, n = qkv.shape
    nblk = n // LANES
    w = jnp.zeros((nblk, SUBLANES, LANES), F32).at[:, :CONV_W].set(
        conv_w.reshape(CONV_W, nblk, LANES).transpose(1, 0, 2))
    return pl.pallas_call(
        _dn_feature_kernel, grid=(b, nblk),
        in_specs=[pl.BlockSpec((1, l, LANES), lambda bi, j: (bi, 0, j)),
                  pl.BlockSpec((1, SUBLANES, LANES), lambda bi, j: (j, 0, 0))],
        out_specs=pl.BlockSpec((1, l, LANES), lambda bi, j: (bi, 0, j)),
        out_shape=jax.ShapeDtypeStruct((b, l, n), F32),
        compiler_params=_cparams("parallel", "arbitrary"), name="dn_features",
    )(qkv, w)


def _scan_chunks(rev, n_ctx, n_lat):
    if rev:
        ctx = lambda j: jnp.maximum(n_ctx - 1 - j, 0)
        lat = lambda j: jnp.where(j < n_ctx, n_lat - 1, n_lat - 1 - (j - n_ctx))
    else:
        ctx = lambda j: jnp.minimum(j, n_ctx - 1)
        lat = lambda j: jnp.maximum(j - n_ctx, 0)
    return ctx, lat


def _dn_scan_kernel(rev, dirn, n_ctx, alog_ref, dtb_ref, fc_ref, fl_ref, bac_ref, bal_ref, o_ref, s_ref):
    step = pl.program_id(1)

    @pl.when(step == 0)
    def _():
        s_ref[...] = jnp.zeros_like(s_ref)

    in_ctx = step < n_ctx
    f = jnp.where(in_ctx, fc_ref[0], fl_ref[0])
    ba = jnp.where(in_ctx, bac_ref[0], bal_ref[0])
    nh, hd, n = DN_HEADS, DN_HEADS * DN_DK, DN_HEADS * CHUNK
    stack = lambda base, w: jnp.concatenate([f[:, base + h * w:base + (h + 1) * w] for h in range(nh)], axis=0)
    q_s, k_s, v_s = stack(0, DN_DK), stack(hd, DN_DK), stack(2 * hd, DN_DV)
    incl, _ = _tri_masks(rev)
    beta_all = jax.nn.sigmoid(ba)
    g_all = -jnp.exp(alog_ref[...]) * _softplus(ba + dtb_ref[...])
    gc_all = _mask_dot(incl.astype(BF16), g_all)
    gc_t = jnp.concatenate([gc_all, gc_all], axis=0).T
    g_tot = jnp.sum(g_all, axis=0, keepdims=True)
    cb = [dirn * nh + h for h in range(nh)]
    cg = [2 * nh + c for c in cb]
    col = lambda a, cs: jnp.concatenate([a[:, c:c + 1] for c in cs], axis=0)
    beta_c, gc_c = col(beta_all, cb), col(gc_all, cg)
    gtot_c = jnp.concatenate([jnp.broadcast_to(g_tot[:, c:c + 1], (CHUNK, 1)) for c in cg], axis=0)
    gc_r = jnp.concatenate([gc_t[c:c + 1, :CHUNK] for c in cg], axis=1)
    r = lax.broadcasted_iota(jnp.int32, (n, n), 0)
    c = lax.broadcasted_iota(jnp.int32, (n, n), 1)
    same = (r // CHUNK) == (c // CHUNK)
    d = (c - r) if rev else (r - c)
    incl_bd, strict_bd = same & (d >= 0), same & (d > 0)
    eye = (r == c).astype(F32)
    decay = jnp.where(incl_bd, jnp.exp(jnp.where(incl_bd, gc_c - gc_r, 0.0)), 0.0)
    kb_s = k_s * beta_c
    lower = jnp.where(strict_bd, _dot_nt(kb_s, k_s) * decay, 0.0)
    eg_c = jnp.exp(gc_c)
    inv = eye - lower
    pw = lower
    for _ in range(5):
        pw = _dot(pw, pw)
        inv = inv + _dot(inv, pw)
    sol = _dot(inv, jnp.concatenate([v_s * beta_c, kb_s * eg_c], axis=-1))
    u_s, w_s = sol[:, :DN_DV], sol[:, DN_DV:]
    k_dec = k_s * jnp.exp(gtot_c - gc_c)
    rb = lax.broadcasted_iota(jnp.int32, (n, DN_DK), 0) // CHUNK
    expand = lambda x: jnp.concatenate([jnp.where(rb == h, x, 0.0) for h in range(nh)], axis=1)
    s = s_ref[...]
    v_new = u_s - _dot(expand(w_s), s)
    a_qk = _dot_nt(q_s, k_s) * decay
    o_s = _dot(expand(q_s * eg_c), s) + _dot(a_qk, v_new)
    gl_rows = jnp.concatenate([jnp.broadcast_to(jnp.exp(g_tot[:, cc:cc + 1]), (DN_DK, 1)) for cc in cg], axis=0)
    s_ref[...] = s * gl_rows + _dot_tn(expand(k_dec), v_new)
    o_ref[0] = jnp.concatenate([o_s[h * CHUNK:(h + 1) * CHUNK] for h in range(nh)], axis=1)


def dn_scan_tc(feat_c, feat_l, ba_c, ba_l, a_log, dt_bias, rev):
    b, l, nf = feat_l.shape
    n_ctx, n_lat = feat_c.shape[1] // CHUNK, l // CHUNK
    dirn = 1 if rev else 0
    cc, lc = _scan_chunks(rev, n_ctx, n_lat)
    lanes = lambda p: jnp.zeros((1, LANES), F32).at[0, 2 * DN_HEADS:4 * DN_HEADS].set(p.reshape(-1))
    vec = pl.BlockSpec((1, LANES), lambda bi, j: (0, 0))
    return pl.pallas_call(
        functools.partial(_dn_scan_kernel, rev, dirn, n_ctx), grid=(b, n_ctx + n_lat),
        in_specs=[vec, vec,
                  pl.BlockSpec((1, CHUNK, nf), lambda bi, j: (bi, cc(j), 0)),
                  pl.BlockSpec((1, CHUNK, nf), lambda bi, j: (bi, lc(j), 0)),
                  pl.BlockSpec((1, CHUNK, LANES), lambda bi, j: (bi, cc(j), 0)),
                  pl.BlockSpec((1, CHUNK, LANES), lambda bi, j: (bi, lc(j), 0))],
        out_specs=pl.BlockSpec((1, CHUNK, DN_HEADS * DN_DV), lambda bi, j: (bi, lc(j), 0)),
        out_shape=jax.ShapeDtypeStruct((b, l, DN_HEADS * DN_DV), F32),
        scratch_shapes=[pltpu.VMEM((DN_HEADS * DN_DK, DN_DV), F32)],
        compiler_params=_cparams("parallel", "arbitrary"), name="dn_scan_bwd" if rev else "dn_scan_fwd",
    )(lanes(a_log), lanes(dt_bias), feat_c, feat_l, ba_c, ba_l)


def _from_grid_cols(blk, n):
    cols = blk.shape[1] // n
    return jnp.concatenate([blk[:, i * n:(i + 1) * n] for i in range(cols)], axis=0)


def _gla_scan_kernel(rev, dirn, n_ctx, qkc_ref, vc_ref, lrc_ref, qkl_ref, vl_ref, lrl_ref,
                     wla_ref, bla_ref, o_ref, s_ref):
    step = pl.program_id(1)

    @pl.when(step == 0)
    def _():
        s_ref[...] = jnp.zeros_like(s_ref)

    in_ctx = step < n_ctx
    hk, hv = GLA_HEADS * GLA_DK, GLA_HEADS * GLA_DV
    qk = jnp.where(in_ctx, qkc_ref[0], _from_grid_cols(qkl_ref[0], 2 * hk))
    vv = jnp.where(in_ctx, vc_ref[0], _from_grid_cols(vl_ref[0], hv))
    lr = jnp.where(in_ctx, lrc_ref[0], _from_grid_cols(lrl_ref[0], LANES))
    incl, _ = _tri_masks(rev)
    incl_b = incl.astype(BF16)
    pre = _dot(lr, wla_ref[0]) + bla_ref[0]
    la_all = -_softplus(-pre) * (1.0 / GLA_TAU)
    bc_all = _mask_dot(incl_b, la_all)
    b_tot_all = jnp.sum(la_all, axis=0, keepdims=True)
    outs = []
    for h in range(GLA_HEADS):
        q = qk[:, h * GLA_DK:(h + 1) * GLA_DK] * GLA_DK ** -0.5
        k = qk[:, hk + h * GLA_DK:hk + (h + 1) * GLA_DK]
        v = vv[:, h * GLA_DV:(h + 1) * GLA_DV]
        bc = bc_all[:, h * GLA_DK:(h + 1) * GLA_DK]
        b_tot = b_tot_all[:, h * GLA_DK:(h + 1) * GLA_DK]
        st = s_ref[h]
        o = _dot_nt(q * jnp.exp(bc), st)
        parts = []
        for i in range(CHUNK // GLA_SUB):
            lo_r, hi_r = i * GLA_SUB, (i + 1) * GLA_SUB
            if rev:
                ref = bc[hi_r - 1:hi_r]
                c0, c1 = lo_r, CHUNK
            else:
                ref = bc[lo_r:lo_r + 1]
                c0, c1 = 0, hi_r
            qi = q[lo_r:hi_r] * jnp.exp(bc[lo_r:hi_r] - ref)
            ki = k[c0:c1] * jnp.exp(ref - bc[c0:c1])
            att = _dot_nt(qi, ki)
            rg = lax.broadcasted_iota(jnp.int32, (GLA_SUB, c1 - c0), 0) + lo_r
            cg = lax.broadcasted_iota(jnp.int32, (GLA_SUB, c1 - c0), 1) + c0
            keep = (cg > rg) if rev else (cg < rg)
            parts.append(_dot(jnp.where(keep, att, 0.0), v[c0:c1]))
        diag = jnp.sum(q * k, axis=-1, keepdims=True) * v
        outs.append(o + jnp.concatenate(parts, axis=0) + diag)
        k_dec = k * jnp.exp(b_tot - bc)
        s_ref[h] = st * jnp.exp(b_tot) + _dot_tn(v, k_dec)
    o = jnp.concatenate(outs, axis=-1)
    rows = o_ref.shape[1]
    o_ref[0] = jnp.concatenate([o[i * rows:(i + 1) * rows] for i in range(CHUNK // rows)], axis=-1)


def gla_scan_tc(qk_c, v_c, lr_c, qk_l, v_l, lr_l, w_la, b_la, rev):
    b, l, _ = qk_l.shape
    rows = l // GRID_W
    cols = CHUNK // rows
    n_ctx, n_lat = qk_c.shape[1] // CHUNK, l // CHUNK
    dirn = 1 if rev else 0
    cc, lc = _scan_chunks(rev, n_ctx, n_lat)
    hv = GLA_HEADS * GLA_DV
    ctx_blk = lambda a: pl.BlockSpec((1, CHUNK, a.shape[2]), lambda bi, j: (bi, cc(j), 0))
    lat_blk = lambda n: pl.BlockSpec((1, rows, cols * n), lambda bi, j: (bi, 0, lc(j)))
    grid_view = lambda a: a.reshape(b, rows, GRID_W * a.shape[2])
    out = pl.pallas_call(
        functools.partial(_gla_scan_kernel, rev, dirn, n_ctx), grid=(b, n_ctx + n_lat),
        in_specs=[ctx_blk(qk_c), ctx_blk(v_c), ctx_blk(lr_c),
                  lat_blk(qk_l.shape[2]), lat_blk(v_l.shape[2]), lat_blk(lr_l.shape[2]),
                  pl.BlockSpec((1,) + w_la.shape[1:], lambda bi, j: (dirn, 0, 0)),
                  pl.BlockSpec((1,) + b_la.shape[1:], lambda bi, j: (dirn, 0, 0))],
        out_specs=lat_blk(hv),
        out_shape=jax.ShapeDtypeStruct((b, rows, GRID_W * hv), F32),
        scratch_shapes=[pltpu.VMEM((GLA_HEADS, GLA_DV, GLA_DK), F32)],
        compiler_params=_cparams("parallel", "arbitrary"), name="gla_scan_bwd" if rev else "gla_scan_fwd",
    )(qk_c, v_c, lr_c, grid_view(qk_l), grid_view(v_l), grid_view(lr_l), w_la, b_la)
    return out.reshape(b, l, hv)


def _head_norm_gate(o, gate, g, n_heads, dv):
    parts = []
    for h in range(n_heads):
        oh = o[:, h * dv:(h + 1) * dv]
        gh = gate[:, h * dv:(h + 1) * dv]
        yh = oh * lax.rsqrt(jnp.mean(oh * oh, axis=-1, keepdims=True) + EPS) * g
        parts.append(yh * (gh * jax.nn.sigmoid(gh)))
    return parts


def _mix_out_kernel(x_ref, dnf_ref, dnb_ref, z_ref, glf_ref, glb_ref, r_ref, mod_ref, dng_ref,
                    glg_ref, n2g_ref, wout_ref, wq_ref, x1_ref, h2_ref, q_ref):
    parts = (_head_norm_gate(dnf_ref[0] + dnb_ref[0], z_ref[0], dng_ref[...], DN_HEADS, DN_DV)
             + _head_norm_gate(glf_ref[0] + glb_ref[0], r_ref[0], glg_ref[...], GLA_HEADS, GLA_DV))
    y = jnp.dot(jnp.concatenate(parts, axis=-1).astype(BF16), wout_ref[...], preferred_element_type=F32)
    x1 = x_ref[0] + mod_ref[0, 2:3, :] * y
    x1_ref[0] = x1
    n = x1 * lax.rsqrt(jnp.mean(x1 * x1, axis=-1, keepdims=True) + EPS) * n2g_ref[...]
    h2 = n * (1.0 + mod_ref[0, 4:5, :]) + mod_ref[0, 3:4, :]
    h2_ref[0] = h2
    q_ref[0] = jnp.dot(h2.astype(BF16), wq_ref[...], preferred_element_type=F32)


def mix_out_tc(x, dn_f, dn_b, z, gl_f, gl_b, r, mod_l, dn_g, gla_g, n2_g, w_out, w_q):
    b, l, d = x.shape
    tok = lambda n: pl.BlockSpec((1, TOK_TILE, n), lambda bi, i: (bi, i, 0))
    full = lambda a: pl.BlockSpec(a.shape, lambda bi, i: (0,) * a.ndim)
    dn_g, gla_g, n2_g = dn_g.reshape(1, -1), gla_g.reshape(1, -1), n2_g.reshape(1, -1)
    nq = w_q.shape[1]
    return pl.pallas_call(
        _mix_out_kernel, grid=(b, l // TOK_TILE),
        in_specs=[tok(d), tok(dn_f.shape[2]), tok(dn_b.shape[2]), tok(z.shape[2]),
                  tok(gl_f.shape[2]), tok(gl_b.shape[2]), tok(r.shape[2]),
                  pl.BlockSpec((1,) + mod_l.shape[1:], lambda bi, i: (bi, 0, 0)),
                  full(dn_g), full(gla_g), full(n2_g), full(w_out), full(w_q)],
        out_specs=[tok(d), tok(d), tok(nq)],
        out_shape=[jax.ShapeDtypeStruct((b, l, d), F32), jax.ShapeDtypeStruct((b, l, d), F32),
                   jax.ShapeDtypeStruct((b, l, nq), F32)],
        compiler_params=_cparams("parallel", "arbitrary"), name="mix_out",
    )(x, dn_f, dn_b, z, gl_f, gl_b, r, mod_l, dn_g, gla_g, n2_g, w_out, w_q)


def _top_rows(s, k, payload=None):
    n = s.shape[0]
    row = lax.broadcasted_iota(jnp.int32, s.shape, 0).astype(F32)
    vals, picked = [], []
    for _ in range(k):
        m = jnp.max(s, axis=0, keepdims=True)
        first = jnp.min(jnp.where(s == m, row, float(n)), axis=0, keepdims=True)
        sel = row == first
        vals.append(m)
        if payload is None:
            picked.append(first)
        else:
            picked.append(jnp.max(jnp.where(sel, payload, -1.0), axis=0, keepdims=True))
        s = jnp.where(sel, -jnp.inf, s)
    return jnp.concatenate(vals, axis=0), jnp.concatenate(picked, axis=0)


def _candidate_rows(s0, i0, s1, i1):
    k = s0.shape[0]
    wide = SUBLANES
    blocks_s = [s0[0:1] + s1]
    blocks_i = [i0[0:1] * float(PEER_NKEYS) + i1]
    col = lax.broadcasted_iota(jnp.int32, (wide, s0.shape[1]), 0)
    for i in range(1, wide):
        keep = col < (k // (i + 1))
        blocks_s.append(jnp.where(keep, s0[i:i + 1] + s1[0:wide], -jnp.inf))
        blocks_i.append(i0[i:i + 1] * float(PEER_NKEYS) + i1[0:wide])
    blocks_s.append(s0[wide:k] + s1[0:1])
    blocks_i.append(i0[wide:k] * float(PEER_NKEYS) + i1[0:1])
    return jnp.concatenate(blocks_s, axis=0), jnp.concatenate(blocks_i, axis=0)


def _select_kernel(q_ref, k_ref, idx_ref, gate_ref, idx_s, gate_s):
    half = PEER_DQ // 2

    def head(h, carry):
        tops = []
        for p in range(2):
            qp = q_ref[:, pl.ds(pl.multiple_of(h * PEER_DQ + p * half, half), half)]
            s = lax.dot_general(k_ref[h, p], qp, (((1,), (1,)), ((), ())),
                                preferred_element_type=F32,
                                precision=lax.Precision.HIGHEST)
            tops.append(_top_rows(s, PEER_TOPK))
        (s0, i0), (s1, i1) = tops
        cand_s, cand_i = _candidate_rows(s0, i0, s1, i1)
        best_s, idx = _top_rows(cand_s, PEER_TOPK, payload=cand_i)
        e = jnp.exp(best_s - best_s[0:1])
        r0 = pl.multiple_of(h * PEER_TOPK, PEER_TOPK)
        idx_s[pl.ds(r0, PEER_TOPK), :] = idx
        gate_s[pl.ds(r0, PEER_TOPK), :] = e / jnp.sum(e, axis=0, keepdims=True)
        return carry

    lax.fori_loop(0, PEER_HEADS, head, 0)
    idx_ref[...] = idx_s[...].T.astype(jnp.int32)
    gate_ref[...] = gate_s[...].T


def peer_select_tc(q, keys):
    n_tok = q.shape[0]
    out_spec = pl.BlockSpec((SELECT_TILE, PICKS), lambda i: (i, 0))
    return pl.pallas_call(
        _select_kernel,
        grid=(n_tok // SELECT_TILE,),
        in_specs=[pl.BlockSpec((SELECT_TILE, q.shape[1]), lambda i: (i, 0)),
                  pl.BlockSpec(keys.shape, lambda i: (0, 0, 0, 0))],
        out_specs=[out_spec, out_spec],
        out_shape=[jax.ShapeDtypeStruct((n_tok, PICKS), jnp.int32),
                   jax.ShapeDtypeStruct((n_tok, PICKS), F32)],
        scratch_shapes=[pltpu.VMEM((PICKS, SELECT_TILE), F32), pltpu.VMEM((PICKS, SELECT_TILE), F32)],
        compiler_params=_cparams("parallel"), name="peer_select",
    )(q, keys)


def _sc_mesh():
    return plsc.VectorSubcoreMesh(core_axis_name="c", subcore_axis_name="s")


def _sc_pipeline(body, n_steps, in_specs, out_specs, operands):
    pltpu.emit_pipeline(
        body, grid=(n_steps,), in_specs=in_specs, out_specs=out_specs,
        core_axis_name=("c", "s"), dimension_semantics=(pltpu.PARALLEL,),
        trace_scopes=False,
    )(*operands)


def pack_table(t):
    half = t.shape[1] // 2
    bits = lax.bitcast_convert_type(t, jnp.uint32)
    lo_bits, hi_bits = bits[:, :half], bits[:, half:]
    low = (lo_bits + jnp.uint32(0x7FFF) + ((lo_bits >> 16) & jnp.uint32(1))) >> 16
    mag = hi_bits & jnp.uint32(0x7FFFFFFF)
    top = jnp.where(mag >= jnp.uint32(0x10000),
                    (mag - low + jnp.uint32(0x8000)) & jnp.uint32(0xFFFF0000), jnp.uint32(0))
    word = (hi_bits & jnp.uint32(0x80000000)) | top | low
    return lax.bitcast_convert_type(word, jnp.int32)


def _unpack(w):
    return plsc.bitcast(w << 16, F32), plsc.bitcast(w, F32)


def peer_act_partial_sc(u_p, idx2, h):
    n_groups, (n_tok, d) = idx2.shape[0], h.shape
    half = d // 2
    nsub = TOK_STEP * GROUPS_PER_TOK
    n_chunks = half // SC_LANES

    @functools.partial(
        pl.kernel, mesh=_sc_mesh(), compiler_params=pltpu.CompilerParams(needs_layout_passes=False),
        out_type=jax.ShapeDtypeStruct((n_groups, PICK_GROUP * SC_LANES), F32),
        scratch_types=[pltpu.VMEM((2, PICK_GROUP, half), jnp.int32),
                       pltpu.SemaphoreType.DMA((2,))],
    )
    def k(u_hbm, i_hbm, h_hbm, o_hbm, rows, sems):
        def body(i_v, h_v, o_v):
            def fetch(j, slot):
                return pltpu.make_async_copy(u_hbm.at[i_v.at[j]], rows.at[slot], sems.at[slot])

            fetch(0, 0).start()

            def sub(j, carry):
                slot = j % 2

                @pl.when(j + 1 < nsub)
                def _():
                    fetch(j + 1, 1 - slot).start()

                fetch(j, slot).wait()
                t = j // GROUPS_PER_TOK

                def picks(g, carry2):
                    kb = g * ACT_UNROLL
                    accs = [None] * ACT_UNROLL
                    for c in range(n_chunks):
                        h_lo = h_v[t, pl.ds(c * SC_LANES, SC_LANES)]
                        h_hi = h_v[t, pl.ds(half + c * SC_LANES, SC_LANES)]
                        for i in range(ACT_UNROLL):
                            lo, hi = _unpack(rows[slot, kb + i, pl.ds(c * SC_LANES, SC_LANES)])
                            p = lo * h_lo + hi * h_hi
                            accs[i] = p if accs[i] is None else accs[i] + p
                    for i in range(ACT_UNROLL):
                        o_v[j, pl.ds((kb + i) * SC_LANES, SC_LANES)] = accs[i]
                    return carry2

                lax.fori_loop(0, PICK_GROUP // ACT_UNROLL, picks, 0)
                return carry

            lax.fori_loop(0, nsub, sub, 0)

        _sc_pipeline(
            body, n_tok // TOK_STEP,
            [pl.BlockSpec((nsub, PICK_GROUP), lambda i: (i, 0)),
             pl.BlockSpec((TOK_STEP, d), lambda i: (i, 0))],
            [pl.BlockSpec((nsub, PICK_GROUP * SC_LANES), lambda i: (i, 0))],
            (i_hbm, h_hbm, o_hbm))

    return k(u_p, idx2, h)


def peer_combine_sc(v_p, idx2, coef_b):
    n_groups = idx2.shape[0]
    half = v_p.shape[1]
    n_tok = n_groups // GROUPS_PER_TOK
    nsub = TOK_STEP * GROUPS_PER_TOK
    n_chunks = half // SC_LANES

    @functools.partial(
        pl.kernel, mesh=_sc_mesh(), compiler_params=pltpu.CompilerParams(needs_layout_passes=False),
        out_type=jax.ShapeDtypeStruct((n_tok, 2 * half), F32),
        scratch_types=[pltpu.VMEM((2, PICK_GROUP, half), jnp.int32),
                       pltpu.SemaphoreType.DMA((2,))],
    )
    def k(v_hbm, i_hbm, c_hbm, o_hbm, rows, sems):
        def body(i_v, c_v, o_v):
            def fetch(j, slot):
                return pltpu.make_async_copy(v_hbm.at[i_v.at[j]], rows.at[slot], sems.at[slot])

            fetch(0, 0).start()

            def sub(j, carry0):
                slot = j % 2

                @pl.when(j + 1 < nsub)
                def _():
                    fetch(j + 1, 1 - slot).start()

                fetch(j, slot).wait()
                t = j // GROUPS_PER_TOK
                first = j % GROUPS_PER_TOK == 0
                cks = [c_v[j, pl.ds(kk * SC_LANES, SC_LANES)] for kk in range(PICK_GROUP)]

                def chunk_pair(c2, carry):
                    for cc in range(COMB_CHUNKS):
                        l = (c2 * COMB_CHUNKS + cc) * SC_LANES
                        ways = 4
                        los, his = [None] * ways, [None] * ways
                        for kk in range(PICK_GROUP):
                            lo, hi = _unpack(rows[slot, kk, pl.ds(l, SC_LANES)])
                            a, b_ = cks[kk] * lo, cks[kk] * hi
                            w = kk % ways
                            los[w] = a if los[w] is None else los[w] + a
                            his[w] = b_ if his[w] is None else his[w] + b_
                        tot_lo = (los[0] + los[1]) + (los[2] + los[3])
                        tot_hi = (his[0] + his[1]) + (his[2] + his[3])
                        old_lo, old_hi = o_v[t, pl.ds(l, SC_LANES)], o_v[t, pl.ds(half + l, SC_LANES)]
                        o_v[t, pl.ds(l, SC_LANES)] = jnp.where(first, tot_lo, tot_lo + old_lo)
                        o_v[t, pl.ds(half + l, SC_LANES)] = jnp.where(first, tot_hi, tot_hi + old_hi)
                    return carry

                lax.fori_loop(0, n_chunks // COMB_CHUNKS, chunk_pair, 0)
                return carry0

            lax.fori_loop(0, nsub, sub, 0)

        _sc_pipeline(
            body, n_tok // TOK_STEP,
            [pl.BlockSpec((nsub, PICK_GROUP), lambda i: (i, 0)),
             pl.BlockSpec((nsub, PICK_GROUP * SC_LANES), lambda i: (i, 0))],
            [pl.BlockSpec((TOK_STEP, 2 * half), lambda i: (i, 0))],
            (i_hbm, c_hbm, o_hbm))

    return k(v_p, idx2, coef_b)


def _segment_matrix():
    r = lax.broadcasted_iota(jnp.int32, (PICK_GROUP * SC_LANES, PICK_GROUP), 0) // SC_LANES
    c = lax.broadcasted_iota(jnp.int32, (PICK_GROUP * SC_LANES, PICK_GROUP), 1)
    return (r == c).astype(F32)


def _coef_kernel(part_ref, gate_ref, o_ref):
    seg = _segment_matrix()
    act = jnp.dot(part_ref[...], seg, preferred_element_type=F32, precision=lax.Precision.HIGHEST)
    coef = gate_ref[...] * (0.5 * act * (1.0 + lax.erf(act * (2.0 ** -0.5))))
    o_ref[...] = lax.dot_general(coef, seg, (((1,), (1,)), ((), ())), preferred_element_type=F32,
                                 precision=lax.Precision.HIGHEST)


def peer_coef_tc(part, gate2):
    n_groups, width = part.shape
    tile = 1024
    return pl.pallas_call(
        _coef_kernel,
        grid=(n_groups // tile,),
        in_specs=[pl.BlockSpec((tile, width), lambda i: (i, 0)),
                  pl.BlockSpec((tile, PICK_GROUP), lambda i: (i, 0))],
        out_specs=pl.BlockSpec((tile, width), lambda i: (i, 0)),
        out_shape=jax.ShapeDtypeStruct((n_groups, width), F32),
        compiler_params=_cparams("parallel"), name="peer_coef",
    )(part, gate2)


def _final_kernel(x_ref, y_ref, mod_ref, g_ref, o_ref):
    x = x_ref[0] + mod_ref[0, 5:6, :] * y_ref[0]
    o_ref[0] = x * lax.rsqrt(jnp.mean(x * x, axis=-1, keepdims=True) + EPS) * g_ref[...]


def final_tc(x1, y, mod_l, final_g):
    b, l, d = x1.shape
    tok = pl.BlockSpec((1, TOK_TILE, d), lambda bi, i: (bi, i, 0))
    return pl.pallas_call(
        _final_kernel, grid=(b, l // TOK_TILE),
        in_specs=[tok, tok, pl.BlockSpec((1,) + mod_l.shape[1:], lambda bi, i: (bi, 0, 0)),
                  pl.BlockSpec((1, d), lambda bi, i: (0, 0))],
        out_specs=tok, out_shape=jax.ShapeDtypeStruct((b, l, d), F32),
        compiler_params=_cparams("parallel", "arbitrary"), name="final_norm",
    )(x1, y, mod_l, final_g.reshape(1, d))


def _pad_cols(w, n):
    return jnp.pad(w, ((0, 0), (0, n - w.shape[1])))


def _slice_sizes(b):
    if b % SLICE_BATCH or SLICE_BATCH % 2 or b < 2 * SLICE_BATCH:
        return [1] * b
    half = SLICE_BATCH // 2
    return [half] + [SLICE_BATCH] * (b // SLICE_BATCH - 1) + [half]


def _exact_zero(v):
    return jnp.minimum(jnp.abs(v), 0.0)


def forward(x, c, ctx, c_ctx, w_ada, b_ada, norm1_g, norm2_g, w_in, conv_w, dn_a_log,
            dn_dt_bias, dn_norm_g, gla_wa2, gla_ba, gla_norm_g, w_out, peer_wq, peer_keys,
            peer_u, peer_v, final_g):
    b, l, d = x.shape
    c_all = jnp.concatenate([c, c_ctx[None]], axis=0)
    c_all = jnp.pad(c_all, ((0, (-c_all.shape[0]) % SUBLANES), (0, 0)))
    mod = adaln_mod(c_all, w_ada, b_ada)
    mod_l = mod[:b].reshape(b, 6, d)
    mod_c = jnp.broadcast_to(mod[b].reshape(1, 6, d), (b, 6, d))
    o = DN_QKV
    hv = DN_HEADS * DN_DV
    w_dn_qkv, w_dn_z = w_in[:, :o], w_in[:, o:o + hv]
    w_dn_ba = _pad_cols(w_in[:, o + hv:DN_COLS], LANES)
    g0 = DN_COLS
    gqk, gv = 2 * GLA_HEADS * GLA_DK, GLA_HEADS * GLA_DV
    w_gl_qk, w_gl_v = w_in[:, g0:g0 + gqk], w_in[:, g0 + gqk:g0 + gqk + gv]
    w_gl_r = w_in[:, g0 + gqk + gv:g0 + gqk + 2 * gv]
    w_gl_lr = _pad_cols(w_in[:, g0 + gqk + 2 * gv:], LANES)
    w_lat = [w.astype(BF16) for w in (w_dn_qkv, w_dn_ba, w_gl_qk, w_gl_v, w_gl_lr, w_dn_z, w_gl_r)]
    w_ctx = w_lat[:5]
    w_la = jnp.zeros((2, LANES, GLA_HEADS * GLA_DK), F32)
    for dd in range(2):
        w_la = w_la.at[dd, dd * GLA_LR:(dd + 1) * GLA_LR].set(gla_wa2[dd])
    w_la = w_la.astype(BF16)
    b_la = gla_ba.reshape(2, 1, GLA_HEADS * GLA_DK)
    w_out_b, w_q_b = w_out.astype(BF16), peer_wq.astype(BF16)
    u_p, v_p = pack_table(peer_u), pack_table(peer_v)

    def mixer(xg, ctxg, mod_cg, mod_g, after_select, after_combine):
        mod_cg, mod_g = mod_cg + after_select, mod_g + after_select
        c_qkv, c_ba, c_qk, c_v, c_lr = in_projection(ctxg, norm1_g, mod_cg[:, 0:2], w_ctx)
        l_qkv, l_ba, l_qk, l_v, l_lr, l_z, l_r = in_projection(xg, norm1_g, mod_g[:, 0:2], w_lat)
        feat_c, feat_l = dn_features_tc(c_qkv, conv_w), dn_features_tc(l_qkv, conv_w)
        dn_f = dn_scan_tc(feat_c, feat_l, c_ba, l_ba, dn_a_log, dn_dt_bias, rev=False)
        dn_b = dn_scan_tc(feat_c, feat_l, c_ba, l_ba, dn_a_log, dn_dt_bias, rev=True)
        gl_f = gla_scan_tc(c_qk, c_v, c_lr, l_qk, l_v, l_lr, w_la, b_la, rev=False)
        gl_b = gla_scan_tc(c_qk, c_v, c_lr, l_qk, l_v, l_lr, w_la, b_la, rev=True)
        return mix_out_tc(xg, dn_f, dn_b, l_z, gl_f, gl_b, l_r, mod_g, dn_norm_g + after_combine,
                          gla_norm_g, norm2_g, w_out_b, w_q_b)

    def select_and_act(h2, q, after_coef):
        n_tok = h2.shape[0] * l
        idx, gate = peer_select_tc(q.reshape(n_tok, -1), peer_keys + after_coef)
        idx2 = idx.reshape(n_tok * GROUPS_PER_TOK, PICK_GROUP)
        gate2 = gate.reshape(n_tok * GROUPS_PER_TOK, PICK_GROUP)
        return idx2, gate2, peer_act_partial_sc(u_p, idx2, h2.reshape(n_tok, d))

    def coef_and_combine(st, after_mixer):
        coef_b = peer_coef_tc(st["part"], st["gate2"] + after_mixer)
        st["y"] = peer_combine_sc(v_p, st["idx2"], coef_b)
        return _exact_zero(coef_b[0, 0])

    zero = jnp.zeros((), F32)
    slices, z_sel, z_coef = [], zero, zero
    i = 0
    for bg in _slice_sizes(b):
        g = len(slices)
        z_comb = _exact_zero(slices[g - 2]["y"][0, 0]) if g >= 2 else zero
        mod_g = mod_l[i:i + bg]
        x1, h2, q = mixer(x[i:i + bg], ctx[i:i + bg], mod_c[i:i + bg], mod_g, z_sel, z_comb)
        i += bg
        if g >= 1:
            z_coef = coef_and_combine(slices[g - 1], _exact_zero(x1[0, 0, 0]))
        idx2, gate2, part = select_and_act(h2, q, z_coef)
        z_sel = _exact_zero(gate2[0, 0])
        slices.append(dict(x1=x1, mod=mod_g, idx2=idx2, gate2=gate2, part=part))
    coef_and_combine(slices[-1], zero)
    outs = [final_tc(st["x1"], st["y"].reshape(st["x1"].shape), st["mod"], final_g) for st in slices]
    return jnp.concatenate(outs, axis=0)


def kernel(x, c, ctx, c_ctx, w_ada, b_ada, norm1_g, norm2_g, w_in, conv_w, dn_a_log,
           dn_dt_bias, dn_norm_g, gla_wa2, gla_ba, gla_norm_g, w_out, peer_wq, peer_keys,
           peer_u, peer_v, final_g):
    assert w_ada.shape[0] == 1, "single-layer block: the context stream is only consumed, never updated"
    return forward(x, c, ctx, c_ctx, w_ada[0], b_ada[0], norm1_g[0], norm2_g[0], w_in[0], conv_w[0],
                   dn_a_log[0], dn_dt_bias[0], dn_norm_g[0], gla_wa2[0], gla_ba[0], gla_norm_g[0],
                   w_out[0], peer_wq[0], peer_keys[0], peer_u[0], peer_v[0], final_g)
```

```python
import functools

import jax
import jax.numpy as jnp
from jax import lax
from jax.experimental import pallas as pl
from jax.experimental.pallas import tpu as pltpu
from jax.experimental.pallas import tpu_sc as plsc

GRID_W = 64
DN_HEADS = 4
DN_DK = 128
DN_DV = 128
CONV_W = 5
GLA_HEADS = 4
GLA_DK = 64
GLA_DV = 128
GLA_LR = 16
GLA_TAU = 16.0
CHUNK = 64
PEER_HEADS = 8
PEER_NKEYS = 128
PEER_DQ = 256
PEER_TOPK = 16
EPS = 1e-6
DN_QKV = 2 * DN_HEADS * DN_DK + DN_HEADS * DN_DV
DN_COLS = DN_QKV + DN_HEADS * DN_DV + 4 * DN_HEADS

SUBLANES = 8
LANES = 128
SC_LANES = 16
VMEM_LIMIT_BYTES = 48 * 1024 * 1024

TOK_TILE = 256
SELECT_TILE = 256
SCAN_BATCH = 2
GLA_SUB = 16
PICK_GROUP = 32
COMB_GROUP_STEP = 16
TOK_STEP = 8
COMB_CHUNKS = 2
ACT_UNROLL = 16
SLICE_BATCH = 2
PICKS = PEER_HEADS * PEER_TOPK
GROUPS_PER_TOK = PICKS // PICK_GROUP

F32 = jnp.float32
BF16 = jnp.bfloat16


def _cparams(*semantics):
    return pltpu.CompilerParams(dimension_semantics=semantics, vmem_limit_bytes=VMEM_LIMIT_BYTES)


def _dot(a, b):
    return jnp.dot(a.astype(BF16), b.astype(BF16), preferred_element_type=F32)


def _dot_nt(a, b):
    return lax.dot_general(a.astype(BF16), b.astype(BF16), (((1,), (1,)), ((), ())),
                           preferred_element_type=F32)


def _dot_tn(a, b):
    return lax.dot_general(a.astype(BF16), b.astype(BF16), (((0,), (0,)), ((), ())),
                           preferred_element_type=F32)


def _split(x):
    hi = x.astype(BF16)
    return hi, (x - hi.astype(F32)).astype(BF16)


def _mask_dot(mask_bf16, x):
    hi, lo = _split(x)
    return (jnp.dot(mask_bf16, hi, preferred_element_type=F32)
            + jnp.dot(mask_bf16, lo, preferred_element_type=F32))


def _softplus(x):
    return jnp.maximum(x, 0.0) + jnp.log(1.0 + jnp.exp(-jnp.abs(x)))


def _tri_masks(rev):
    r = lax.broadcasted_iota(jnp.int32, (CHUNK, CHUNK), 0)
    c = lax.broadcasted_iota(jnp.int32, (CHUNK, CHUNK), 1)
    d = (c - r) if rev else (r - c)
    return d >= 0, d > 0


def _mod_kernel(c_ref, w_ref, b_ref, o_ref):
    c = c_ref[...]
    s = c * jax.nn.sigmoid(c)
    o_ref[...] = jnp.dot(s, w_ref[...], preferred_element_type=F32,
                         precision=lax.Precision.HIGHEST) + b_ref[...]


def adaln_mod(c_all, w_ada, b_ada):
    r, d = c_all.shape
    n = w_ada.shape[1]
    tn = 512
    return pl.pallas_call(
        _mod_kernel, grid=(n // tn,),
        in_specs=[pl.BlockSpec((r, d), lambda j: (0, 0)),
                  pl.BlockSpec((d, tn), lambda j: (0, j)),
                  pl.BlockSpec((1, tn), lambda j: (0, j))],
        out_specs=pl.BlockSpec((r, tn), lambda j: (0, j)),
        out_shape=jax.ShapeDtypeStruct((r, n), F32),
        compiler_params=_cparams("arbitrary"), name="adaln_mod",
    )(c_all, w_ada, b_ada.reshape(1, n))


def _inproj_kernel(x_ref, g_ref, mod_ref, *refs):
    n_out = len(refs) // 2
    x = x_ref[0]
    y = x * lax.rsqrt(jnp.mean(x * x, axis=-1, keepdims=True) + EPS) * g_ref[...]
    h = (y * (1.0 + mod_ref[0, 1:2, :]) + mod_ref[0, 0:1, :]).astype(BF16)
    for w_ref, o_ref in zip(refs[:n_out], refs[n_out:]):
        o_ref[0] = jnp.dot(h, w_ref[...], preferred_element_type=F32)


def in_projection(x, norm_g, mod, weights):
    b, l, d = x.shape
    w_specs = [pl.BlockSpec(w.shape, lambda bi, i: (0, 0)) for w in weights]
    o_specs = [pl.BlockSpec((1, TOK_TILE, w.shape[1]), lambda bi, i: (bi, i, 0)) for w in weights]
    return pl.pallas_call(
        _inproj_kernel, grid=(b, l // TOK_TILE),
        in_specs=[pl.BlockSpec((1, TOK_TILE, d), lambda bi, i: (bi, i, 0)),
                  pl.BlockSpec((1, d), lambda bi, i: (0, 0)),
                  pl.BlockSpec((1, 2, d), lambda bi, i: (bi, 0, 0))] + w_specs,
        out_specs=o_specs,
        out_shape=[jax.ShapeDtypeStruct((b, l, w.shape[1]), F32) for w in weights],
        compiler_params=_cparams("parallel", "arbitrary"), name="in_projection",
    )(x, norm_g.reshape(1, d), mod, *weights)


def _dn_feature_kernel(x_ref, w_ref, o_ref):
    x = x_ref[0]
    n = x.shape[0]
    t = lax.broadcasted_iota(jnp.int32, (n, 1), 0)
    pad = CONV_W // 2
    acc = w_ref[0, pad:pad + 1, :] * x
    for j in range(CONV_W):
        s = j - pad
        if s == 0:
            continue
        xs = pltpu.roll(x, (-s) % n, axis=0)
        bad = (t < -s) if s < 0 else (t >= n - s)
        acc = acc + w_ref[0, j:j + 1, :] * jnp.where(bad, 0.0, xs)
    y = acc * jax.nn.sigmoid(acc)
    kind = pl.program_id(1) // DN_HEADS
    inv = lax.rsqrt(jnp.sum(y * y, axis=-1, keepdims=True) + EPS)
    scale = jnp.where(kind == 0, inv * DN_DK ** -0.5, jnp.where(kind == 1, inv, 1.0))
    o_ref[0] = y * scale


def dn_features_tc(qkv, conv_w):
    b, l, n = qkv.shape
    nblk = n // LANES
    w = jnp.zeros((nblk, SUBLANES, LANES), F32).at[:, :CONV_W].set(
        conv_w.reshape(CONV_W, nblk, LANES).transpose(1, 0, 2))
    return pl.pallas_call(
        _dn_feature_kernel, grid=(b, nblk),
        in_specs=[pl.BlockSpec((1, l, LANES), lambda bi, j: (bi, 0, j)),
                  pl.BlockSpec((1, SUBLANES, LANES), lambda bi, j: (j, 0, 0))],
        out_specs=pl.BlockSpec((1, l, LANES), lambda bi, j: (bi, 0, j)),
        out_shape=jax.ShapeDtypeStruct((b, l, n), F32),
        compiler_params=_cparams("parallel", "arbitrary"), name="dn_features",
    )(qkv, w)


def _scan_chunks(rev, n_ctx, n_lat):
    if rev:
        ctx = lambda j: jnp.maximum(n_ctx - 1 - j, 0)
        lat = lambda j: jnp.where(j < n_ctx, n_lat - 1, n_lat - 1 - (j - n_ctx))
    else:
        ctx = lambda j: jnp.minimum(j, n_ctx - 1)
        lat = lambda j: jnp.maximum(j - n_ctx, 0)
    return ctx, lat


def _dn_scan_kernel(rev, dirn, n_ctx, alog_ref, dtb_ref, fc_ref, fl_ref, bac_ref, bal_ref, o_ref, s_ref):
    step = pl.program_id(1)

    @pl.when(step == 0)
    def _():
        s_ref[...] = jnp.zeros_like(s_ref)

    in_ctx = step < n_ctx
    f = jnp.where(in_ctx, fc_ref[0], fl_ref[0])
    ba = jnp.where(in_ctx, bac_ref[0], bal_ref[0])
    nh, hd, n = DN_HEADS, DN_HEADS * DN_DK, DN_HEADS * CHUNK
    stack = lambda base, w: jnp.concatenate([f[:, base + h * w:base + (h + 1) * w] for h in range(nh)], axis=0)
    q_s, k_s, v_s = stack(0, DN_DK), stack(hd, DN_DK), stack(2 * hd, DN_DV)
    incl, _ = _tri_masks(rev)
    beta_all = jax.nn.sigmoid(ba)
    g_all = -jnp.exp(alog_ref[...]) * _softplus(ba + dtb_ref[...])
    gc_all = _mask_dot(incl.astype(BF16), g_all)
    gc_t = jnp.concatenate([gc_all, gc_all], axis=0).T
    g_tot = jnp.sum(g_all, axis=0, keepdims=True)
    cb = [dirn * nh + h for h in range(nh)]
    cg = [2 * nh + c for c in cb]
    col = lambda a, cs: jnp.concatenate([a[:, c:c + 1] for c in cs], axis=0)
    beta_c, gc_c = col(beta_all, cb), col(gc_all, cg)
    gtot_c = jnp.concatenate([jnp.broadcast_to(g_tot[:, c:c + 1], (CHUNK, 1)) for c in cg], axis=0)
    gc_r = jnp.concatenate([gc_t[c:c + 1, :CHUNK] for c in cg], axis=1)
    r = lax.broadcasted_iota(jnp.int32, (n, n), 0)
    c = lax.broadcasted_iota(jnp.int32, (n, n), 1)
    same = (r // CHUNK) == (c // CHUNK)
    d = (c - r) if rev else (r - c)
    incl_bd, strict_bd = same & (d >= 0), same & (d > 0)
    eye = (r == c).astype(F32)
    decay = jnp.where(incl_bd, jnp.exp(jnp.where(incl_bd, gc_c - gc_r, 0.0)), 0.0)
    kb_s = k_s * beta_c
    lower = jnp.where(strict_bd, _dot_nt(kb_s, k_s) * decay, 0.0)
    eg_c = jnp.exp(gc_c)
    inv = eye - lower
    pw = lower
    for _ in range(5):
        pw = _dot(pw, pw)
        inv = inv + _dot(inv, pw)
    sol = _dot(inv, jnp.concatenate([v_s * beta_c, kb_s * eg_c], axis=-1))
    u_s, w_s = sol[:, :DN_DV], sol[:, DN_DV:]
    k_dec = k_s * jnp.exp(gtot_c - gc_c)
    rb = lax.broadcasted_iota(jnp.int32, (n, DN_DK), 0) // CHUNK
    expand = lambda x: jnp.concatenate([jnp.where(rb == h, x, 0.0) for h in range(nh)], axis=1)
    s = s_ref[...]
    v_new = u_s - _dot(expand(w_s), s)
    a_qk = _dot_nt(q_s, k_s) * decay
    o_s = _dot(expand(q_s * eg_c), s) + _dot(a_qk, v_new)
    gl_rows = jnp.concatenate([jnp.broadcast_to(jnp.exp(g_tot[:, cc:cc + 1]), (DN_DK, 1)) for cc in cg], axis=0)
    s_ref[...] = s * gl_rows + _dot_tn(expand(k_dec), v_new)
    o_ref[0] = jnp.concatenate([o_s[h * CHUNK:(h + 1) * CHUNK] for h in range(nh)], axis=1)


def dn_scan_tc(feat_c, feat_l, ba_c, ba_l, a_log, dt_bias, rev):
    b, l, nf = feat_l.shape
    n_ctx, n_lat = feat_c.shape[1] // CHUNK, l // CHUNK
    dirn = 1 if rev else 0
    cc, lc = _scan_chunks(rev, n_ctx, n_lat)
    lanes = lambda p: jnp.zeros((1, LANES), F32).at[0, 2 * DN_HEADS:4 * DN_HEADS].set(p.reshape(-1))
    vec = pl.BlockSpec((1, LANES), lambda bi, j: (0, 0))
    return pl.pallas_call(
        functools.partial(_dn_scan_kernel, rev, dirn, n_ctx), grid=(b, n_ctx + n_lat),
        in_specs=[vec, vec,
                  pl.BlockSpec((1, CHUNK, nf), lambda bi, j: (bi, cc(j), 0)),
                  pl.BlockSpec((1, CHUNK, nf), lambda bi, j: (bi, lc(j), 0)),
                  pl.BlockSpec((1, CHUNK, LANES), lambda bi, j: (bi, cc(j), 0)),
                  pl.BlockSpec((1, CHUNK, LANES), lambda bi, j: (bi, lc(j), 0))],
        out_specs=pl.BlockSpec((1, CHUNK, DN_HEADS * DN_DV), lambda bi, j: (bi, lc(j), 0)),
        out_shape=jax.ShapeDtypeStruct((b, l, DN_HEADS * DN_DV), F32),
        scratch_shapes=[pltpu.VMEM((DN_HEADS * DN_DK, DN_DV), F32)],
        compiler_params=_cparams("parallel", "arbitrary"), name="dn_scan_bwd" if rev else "dn_scan_fwd",
    )(lanes(a_log), lanes(dt_bias), feat_c, feat_l, ba_c, ba_l)


def _from_grid_cols(blk, n):
    cols = blk.shape[1] // n
    return jnp.concatenate([blk[:, i * n:(i + 1) * n] for i in range(cols)], axis=0)


def _gla_scan_kernel(rev, dirn, n_ctx, qkc_ref, vc_ref, lrc_ref, qkl_ref, vl_ref, lrl_ref,
                     wla_ref, bla_ref, o_ref, s_ref):
    step = pl.program_id(1)

    @pl.when(step == 0)
    def _():
        s_ref[...] = jnp.zeros_like(s_ref)

    in_ctx = step < n_ctx
    hk, hv = GLA_HEADS * GLA_DK, GLA_HEADS * GLA_DV
    qk = jnp.where(in_ctx, qkc_ref[0], _from_grid_cols(qkl_ref[0], 2 * hk))
    vv = jnp.where(in_ctx, vc_ref[0], _from_grid_cols(vl_ref[0], hv))
    lr = jnp.where(in_ctx, lrc_ref[0], _from_grid_cols(lrl_ref[0], LANES))
    incl, _ = _tri_masks(rev)
    incl_b = incl.astype(BF16)
    pre = _dot(lr, wla_ref[0]) + bla_ref[0]
    la_all = -_softplus(-pre) * (1.0 / GLA_TAU)
    bc_all = _mask_dot(incl_b, la_all)
    b_tot_all = jnp.sum(la_all, axis=0, keepdims=True)
    outs = []
    for h in range(GLA_HEADS):
        q = qk[:, h * GLA_DK:(h + 1) * GLA_DK] * GLA_DK ** -0.5
        k = qk[:, hk + h * GLA_DK:hk + (h + 1) * GLA_DK]
        v = vv[:, h * GLA_DV:(h + 1) * GLA_DV]
        bc = bc_all[:, h * GLA_DK:(h + 1) * GLA_DK]
        b_tot = b_tot_all[:, h * GLA_DK:(h + 1) * GLA_DK]
        st = s_ref[h]
        o = _dot_nt(q * jnp.exp(bc), st)
        parts = []
        for i in range(CHUNK // GLA_SUB):
            lo_r, hi_r = i * GLA_SUB, (i + 1) * GLA_SUB
            if rev:
                ref = bc[hi_r - 1:hi_r]
                c0, c1 = lo_r, CHUNK
            else:
                ref = bc[lo_r:lo_r + 1]
                c0, c1 = 0, hi_r
            qi = q[lo_r:hi_r] * jnp.exp(bc[lo_r:hi_r] - ref)
            ki = k[c0:c1] * jnp.exp(ref - bc[c0:c1])
            att = _dot_nt(qi, ki)
            rg = lax.broadcasted_iota(jnp.int32, (GLA_SUB, c1 - c0), 0) + lo_r
            cg = lax.broadcasted_iota(jnp.int32, (GLA_SUB, c1 - c0), 1) + c0
            keep = (cg > rg) if rev else (cg < rg)
            parts.append(_dot(jnp.where(keep, att, 0.0), v[c0:c1]))
        diag = jnp.sum(q * k, axis=-1, keepdims=True) * v
        outs.append(o + jnp.concatenate(parts, axis=0) + diag)
        k_dec = k * jnp.exp(b_tot - bc)
        s_ref[h] = st * jnp.exp(b_tot) + _dot_tn(v, k_dec)
    o = jnp.concatenate(outs, axis=-1)
    rows = o_ref.shape[1]
    o_ref[0] = jnp.concatenate([o[i * rows:(i + 1) * rows] for i in range(CHUNK // rows)], axis=-1)


def gla_scan_tc(qk_c, v_c, lr_c, qk_l, v_l, lr_l, w_la, b_la, rev):
    b, l, _ = qk_l.shape
    rows = l // GRID_W
    cols = CHUNK // rows
    n_ctx, n_lat = qk_c.shape[1] // CHUNK, l // CHUNK
    dirn = 1 if rev else 0
    cc, lc = _scan_chunks(rev, n_ctx, n_lat)
    hv = GLA_HEADS * GLA_DV
    ctx_blk = lambda a: pl.BlockSpec((1, CHUNK, a.shape[2]), lambda bi, j: (bi, cc(j), 0))
    lat_blk = lambda n: pl.BlockSpec((1, rows, cols * n), lambda bi, j: (bi, 0, lc(j)))
    grid_view = lambda a: a.reshape(b, rows, GRID_W * a.shape[2])
    out = pl.pallas_call(
        functools.partial(_gla_scan_kernel, rev, dirn, n_ctx), grid=(b, n_ctx + n_lat),
        in_specs=[ctx_blk(qk_c), ctx_blk(v_c), ctx_blk(lr_c),
                  lat_blk(qk_l.shape[2]), lat_blk(v_l.shape[2]), lat_blk(lr_l.shape[2]),
                  pl.BlockSpec((1,) + w_la.shape[1:], lambda bi, j: (dirn, 0, 0)),
                  pl.BlockSpec((1,) + b_la.shape[1:], lambda bi, j: (dirn, 0, 0))],
        out_specs=lat_blk(hv),
        out_shape=jax.ShapeDtypeStruct((b, rows, GRID_W * hv), F32),
        scratch_shapes=[pltpu.VMEM((GLA_HEADS, GLA_DV, GLA_DK), F32)],
        compiler_params=_cparams("parallel", "arbitrary"), name="gla_scan_bwd" if rev else "gla_scan_fwd",
    )(qk_c, v_c, lr_c, grid_view(qk_l), grid_view(v_l), grid_view(lr_l), w_la, b_la)
    return out.reshape(b, l, hv)


def _head_norm_gate(o, gate, g, n_heads, dv):
    parts = []
    for h in range(n_heads):
        oh = o[:, h * dv:(h + 1) * dv]
        gh = gate[:, h * dv:(h + 1) * dv]
        yh = oh * lax.rsqrt(jnp.mean(oh * oh, axis=-1, keepdims=True) + EPS) * g
        parts.append(yh * (gh * jax.nn.sigmoid(gh)))
    return parts


def _mix_out_kernel(x_ref, dnf_ref, dnb_ref, z_ref, glf_ref, glb_ref, r_ref, mod_ref, dng_ref,
                    glg_ref, n2g_ref, wout_ref, wq_ref, x1_ref, h2_ref, q_ref):
    parts = (_head_norm_gate(dnf_ref[0] + dnb_ref[0], z_ref[0], dng_ref[...], DN_HEADS, DN_DV)
             + _head_norm_gate(glf_ref[0] + glb_ref[0], r_ref[0], glg_ref[...], GLA_HEADS, GLA_DV))
    y = jnp.dot(jnp.concatenate(parts, axis=-1).astype(BF16), wout_ref[...], preferred_element_type=F32)
    x1 = x_ref[0] + mod_ref[0, 2:3, :] * y
    x1_ref[0] = x1
    n = x1 * lax.rsqrt(jnp.mean(x1 * x1, axis=-1, keepdims=True) + EPS) * n2g_ref[...]
    h2 = n * (1.0 + mod_ref[0, 4:5, :]) + mod_ref[0, 3:4, :]
    h2_ref[0] = h2
    q_ref[0] = jnp.dot(h2.astype(BF16), wq_ref[...], preferred_element_type=F32)


def mix_out_tc(x, dn_f, dn_b, z, gl_f, gl_b, r, mod_l, dn_g, gla_g, n2_g, w_out, w_q):
    b, l, d = x.shape
    tok = lambda n: pl.BlockSpec((1, TOK_TILE, n), lambda bi, i: (bi, i, 0))
    full = lambda a: pl.BlockSpec(a.shape, lambda bi, i: (0,) * a.ndim)
    dn_g, gla_g, n2_g = dn_g.reshape(1, -1), gla_g.reshape(1, -1), n2_g.reshape(1, -1)
    nq = w_q.shape[1]
    return pl.pallas_call(
        _mix_out_kernel, grid=(b, l // TOK_TILE),
        in_specs=[tok(d), tok(dn_f.shape[2]), tok(dn_b.shape[2]), tok(z.shape[2]),
                  tok(gl_f.shape[2]), tok(gl_b.shape[2]), tok(r.shape[2]),
                  pl.BlockSpec((1,) + mod_l.shape[1:], lambda bi, i: (bi, 0, 0)),
                  full(dn_g), full(gla_g), full(n2_g), full(w_out), full(w_q)],
        out_specs=[tok(d), tok(d), tok(nq)],
        out_shape=[jax.ShapeDtypeStruct((b, l, d), F32), jax.ShapeDtypeStruct((b, l, d), F32),
                   jax.ShapeDtypeStruct((b, l, nq), F32)],
        compiler_params=_cparams("parallel", "arbitrary"), name="mix_out",
    )(x, dn_f, dn_b, z, gl_f, gl_b, r, mod_l, dn_g, gla_g, n2_g, w_out, w_q)


def _top_rows(s, k, payload=None):
    n = s.shape[0]
    row = lax.broadcasted_iota(jnp.int32, s.shape, 0).astype(F32)
    vals, picked = [], []
    for _ in range(k):
        m = jnp.max(s, axis=0, keepdims=True)
        first = jnp.min(jnp.where(s == m, row, float(n)), axis=0, keepdims=True)
        sel = row == first
        vals.append(m)
        if payload is None:
            picked.append(first)
        else:
            picked.append(jnp.max(jnp.where(sel, payload, -1.0), axis=0, keepdims=True))
        s = jnp.where(sel, -jnp.inf, s)
    return jnp.concatenate(vals, axis=0), jnp.concatenate(picked, axis=0)


def _candidate_rows(s0, i0, s1, i1):
    k = s0.shape[0]
    wide = SUBLANES
    blocks_s = [s0[0:1] + s1]
    blocks_i = [i0[0:1] * float(PEER_NKEYS) + i1]
    col = lax.broadcasted_iota(jnp.int32, (wide, s0.shape[1]), 0)
    for i in range(1, wide):
        keep = col < (k // (i + 1))
        blocks_s.append(jnp.where(keep, s0[i:i + 1] + s1[0:wide], -jnp.inf))
        blocks_i.append(i0[i:i + 1] * float(PEER_NKEYS) + i1[0:wide])
    blocks_s.append(s0[wide:k] + s1[0:1])
    blocks_i.append(i0[wide:k] * float(PEER_NKEYS) + i1[0:1])
    return jnp.concatenate(blocks_s, axis=0), jnp.concatenate(blocks_i, axis=0)


def _select_kernel(q_ref, k_ref, idx_ref, gate_ref, idx_s, gate_s):
    half = PEER_DQ // 2

    def head(h, carry):
        tops = []
        for p in range(2):
            qp = q_ref[:, pl.ds(pl.multiple_of(h * PEER_DQ + p * half, half), half)]
            s = lax.dot_general(k_ref[h, p], qp, (((1,), (1,)), ((), ())),
                                preferred_element_type=F32,
                                precision=lax.Precision.HIGHEST)
            tops.append(_top_rows(s, PEER_TOPK))
        (s0, i0), (s1, i1) = tops
        cand_s, cand_i = _candidate_rows(s0, i0, s1, i1)
        best_s, idx = _top_rows(cand_s, PEER_TOPK, payload=cand_i)
        e = jnp.exp(best_s - best_s[0:1])
        r0 = pl.multiple_of(h * PEER_TOPK, PEER_TOPK)
        idx_s[pl.ds(r0, PEER_TOPK), :] = idx
        gate_s[pl.ds(r0, PEER_TOPK), :] = e / jnp.sum(e, axis=0, keepdims=True)
        return carry

    lax.fori_loop(0, PEER_HEADS, head, 0)
    idx_ref[...] = idx_s[...].T.astype(jnp.int32)
    gate_ref[...] = gate_s[...].T


def peer_select_tc(q, keys):
    n_tok = q.shape[0]
    out_spec = pl.BlockSpec((SELECT_TILE, PICKS), lambda i: (i, 0))
    return pl.pallas_call(
        _select_kernel,
        grid=(n_tok // SELECT_TILE,),
        in_specs=[pl.BlockSpec((SELECT_TILE, q.shape[1]), lambda i: (i, 0)),
                  pl.BlockSpec(keys.shape, lambda i: (0, 0, 0, 0))],
        out_specs=[out_spec, out_spec],
        out_shape=[jax.ShapeDtypeStruct((n_tok, PICKS), jnp.int32),
                   jax.ShapeDtypeStruct((n_tok, PICKS), F32)],
        scratch_shapes=[pltpu.VMEM((PICKS, SELECT_TILE), F32), pltpu.VMEM((PICKS, SELECT_TILE), F32)],
        compiler_params=_cparams("parallel"), name="peer_select",
    )(q, keys)


def _sc_mesh():
    return plsc.VectorSubcoreMesh(core_axis_name="c", subcore_axis_name="s")


def _sc_pipeline(body, n_steps, in_specs, out_specs, operands):
    pltpu.emit_pipeline(
        body, grid=(n_steps,), in_specs=in_specs, out_specs=out_specs,
        core_axis_name=("c", "s"), dimension_semantics=(pltpu.PARALLEL,),
        trace_scopes=False,
    )(*operands)


def pack_table(t):
    half = t.shape[1] // 2
    bits = lax.bitcast_convert_type(t, jnp.uint32)
    lo_bits, hi_bits = bits[:, :half], bits[:, half:]
    low = (lo_bits + jnp.uint32(0x7FFF) + ((lo_bits >> 16) & jnp.uint32(1))) >> 16
    mag = hi_bits & jnp.uint32(0x7FFFFFFF)
    top = jnp.where(mag >= jnp.uint32(0x10000),
                    (mag - low + jnp.uint32(0x8000)) & jnp.uint32(0xFFFF0000), jnp.uint32(0))
    word = (hi_bits & jnp.uint32(0x80000000)) | top | low
    return lax.bitcast_convert_type(word, jnp.int32)


def _unpack(w):
    return plsc.bitcast(w << 16, F32), plsc.bitcast(w, F32)


def peer_act_partial_sc(u_p, idx2, h):
    n_groups, (n_tok, d) = idx2.shape[0], h.shape
    half = d // 2
    nsub = TOK_STEP * GROUPS_PER_TOK
    n_chunks = half // SC_LANES

    @functools.partial(
        pl.kernel, mesh=_sc_mesh(), compiler_params=pltpu.CompilerParams(needs_layout_passes=False),
        out_type=jax.ShapeDtypeStruct((n_groups, PICK_GROUP * SC_LANES), F32),
        scratch_types=[pltpu.VMEM((2, PICK_GROUP, half), jnp.int32),
                       pltpu.SemaphoreType.DMA((2,))],
    )
    def k(u_hbm, i_hbm, h_hbm, o_hbm, rows, sems):
        def body(i_v, h_v, o_v):
            def fetch(j, slot):
                return pltpu.make_async_copy(u_hbm.at[i_v.at[j]], rows.at[slot], sems.at[slot])

            fetch(0, 0).start()

            def sub(j, carry):
                slot = j % 2

                @pl.when(j + 1 < nsub)
                def _():
                    fetch(j + 1, 1 - slot).start()

                fetch(j, slot).wait()
                t = j // GROUPS_PER_TOK

                def picks(g, carry2):
                    kb = g * ACT_UNROLL
                    accs = [None] * ACT_UNROLL
                    for c in range(n_chunks):
                        h_lo = h_v[t, pl.ds(c * SC_LANES, SC_LANES)]
                        h_hi = h_v[t, pl.ds(half + c * SC_LANES, SC_LANES)]
                        for i in range(ACT_UNROLL):
                            lo, hi = _unpack(rows[slot, kb + i, pl.ds(c * SC_LANES, SC_LANES)])
                            p = lo * h_lo + hi * h_hi
                            accs[i] = p if accs[i] is None else accs[i] + p
                    for i in range(ACT_UNROLL):
                        o_v[j, pl.ds((kb + i) * SC_LANES, SC_LANES)] = accs[i]
                    return carry2

                lax.fori_loop(0, PICK_GROUP // ACT_UNROLL, picks, 0)
                return carry

            lax.fori_loop(0, nsub, sub, 0)

        _sc_pipeline(
            body, n_tok // TOK_STEP,
            [pl.BlockSpec((nsub, PICK_GROUP), lambda i: (i, 0)),
             pl.BlockSpec((TOK_STEP, d), lambda i: (i, 0))],
            [pl.BlockSpec((nsub, PICK_GROUP * SC_LANES), lambda i: (i, 0))],
            (i_hbm, h_hbm, o_hbm))

    return k(u_p, idx2, h)


def peer_combine_sc(v_p, idx2, coef_b):
    n_groups = idx2.shape[0]
    half = v_p.shape[1]
    nsub = COMB_GROUP_STEP
    n_chunks = half // SC_LANES

    @functools.partial(
        pl.kernel, mesh=_sc_mesh(), compiler_params=pltpu.CompilerParams(needs_layout_passes=False),
        out_type=jax.ShapeDtypeStruct((n_groups, 2 * half), F32),
        scratch_types=[pltpu.VMEM((2, PICK_GROUP, half), jnp.int32),
                       pltpu.SemaphoreType.DMA((2,))],
    )
    def k(v_hbm, i_hbm, c_hbm, o_hbm, rows, sems):
        def body(i_v, c_v, o_v):
            def fetch(j, slot):
                return pltpu.make_async_copy(v_hbm.at[i_v.at[j]], rows.at[slot], sems.at[slot])

            fetch(0, 0).start()

            def sub(j, carry0):
                slot = j % 2

                @pl.when(j + 1 < nsub)
                def _():
                    fetch(j + 1, 1 - slot).start()

                fetch(j, slot).wait()
                cks = [c_v[j, pl.ds(kk * SC_LANES, SC_LANES)] for kk in range(PICK_GROUP)]

                def chunk_pair(c2, carry):
                    for cc in range(COMB_CHUNKS):
                        l = (c2 * COMB_CHUNKS + cc) * SC_LANES
                        ways = 2
                        los, his = [None] * ways, [None] * ways
                        for kk in range(PICK_GROUP):
                            lo, hi = _unpack(rows[slot, kk, pl.ds(l, SC_LANES)])
                            a, b_ = cks[kk] * lo, cks[kk] * hi
                            w = kk % ways
                            los[w] = a if los[w] is None else los[w] + a
                            his[w] = b_ if his[w] is None else his[w] + b_
                        o_v[j, pl.ds(l, SC_LANES)] = los[0] + los[1]
                        o_v[j, pl.ds(half + l, SC_LANES)] = his[0] + his[1]
                    return carry

                lax.fori_loop(0, n_chunks // COMB_CHUNKS, chunk_pair, 0)
                return carry0

            lax.fori_loop(0, nsub, sub, 0)

        _sc_pipeline(
            body, n_groups // nsub,
            [pl.BlockSpec((nsub, PICK_GROUP), lambda i: (i, 0)),
             pl.BlockSpec((nsub, PICK_GROUP * SC_LANES), lambda i: (i, 0))],
            [pl.BlockSpec((nsub, 2 * half), lambda i: (i, 0))],
            (i_hbm, c_hbm, o_hbm))

    return k(v_p, idx2, coef_b)


def _segment_matrix():
    r = lax.broadcasted_iota(jnp.int32, (PICK_GROUP * SC_LANES, PICK_GROUP), 0) // SC_LANES
    c = lax.broadcasted_iota(jnp.int32, (PICK_GROUP * SC_LANES, PICK_GROUP), 1)
    return (r == c).astype(F32)


def _coef_kernel(part_ref, gate_ref, o_ref):
    seg = _segment_matrix()
    act = jnp.dot(part_ref[...], seg, preferred_element_type=F32, precision=lax.Precision.HIGHEST)
    coef = gate_ref[...] * (0.5 * act * (1.0 + lax.erf(act * (2.0 ** -0.5))))
    o_ref[...] = lax.dot_general(coef, seg, (((1,), (1,)), ((), ())), preferred_element_type=F32,
                                 precision=lax.Precision.HIGHEST)


def peer_coef_tc(part, gate2):
    n_groups, width = part.shape
    tile = 1024
    return pl.pallas_call(
        _coef_kernel,
        grid=(n_groups // tile,),
        in_specs=[pl.BlockSpec((tile, width), lambda i: (i, 0)),
                  pl.BlockSpec((tile, PICK_GROUP), lambda i: (i, 0))],
        out_specs=pl.BlockSpec((tile, width), lambda i: (i, 0)),
        out_shape=jax.ShapeDtypeStruct((n_groups, width), F32),
        compiler_params=_cparams("parallel"), name="peer_coef",
    )(part, gate2)


def _final_kernel(x_ref, y_ref, mod_ref, g_ref, o_ref):
    d = x_ref.shape[2]
    y = y_ref[0, :, 0:d]
    for p in range(1, y_ref.shape[2] // d):
        y = y + y_ref[0, :, p * d:(p + 1) * d]
    x = x_ref[0] + mod_ref[0, 5:6, :] * y
    o_ref[0] = x * lax.rsqrt(jnp.mean(x * x, axis=-1, keepdims=True) + EPS) * g_ref[...]


def final_tc(x1, y_parts, mod_l, final_g):
    b, l, d = x1.shape
    tok = pl.BlockSpec((1, TOK_TILE, d), lambda bi, i: (bi, i, 0))
    return pl.pallas_call(
        _final_kernel, grid=(b, l // TOK_TILE),
        in_specs=[tok, pl.BlockSpec((1, TOK_TILE, y_parts.shape[2]), lambda bi, i: (bi, i, 0)),
                  pl.BlockSpec((1,) + mod_l.shape[1:], lambda bi, i: (bi, 0, 0)),
                  pl.BlockSpec((1, d), lambda bi, i: (0, 0))],
        out_specs=tok, out_shape=jax.ShapeDtypeStruct((b, l, d), F32),
        compiler_params=_cparams("parallel", "arbitrary"), name="final_norm",
    )(x1, y_parts, mod_l, final_g.reshape(1, d))


def _pad_cols(w, n):
    return jnp.pad(w, ((0, 0), (0, n - w.shape[1])))


def _slice_sizes(b):
    if b % SLICE_BATCH or SLICE_BATCH % 2 or b < 2 * SLICE_BATCH:
        return [1] * b
    half = SLICE_BATCH // 2
    return [half] + [SLICE_BATCH] * (b // SLICE_BATCH - 1) + [half]


def _exact_zero(v):
    return jnp.minimum(jnp.abs(v), 0.0)


def forward(x, c, ctx, c_ctx, w_ada, b_ada, norm1_g, norm2_g, w_in, conv_w, dn_a_log,
            dn_dt_bias, dn_norm_g, gla_wa2, gla_ba, gla_norm_g, w_out, peer_wq, peer_keys,
            peer_u, peer_v, final_g):
    b, l, d = x.shape
    c_all = jnp.concatenate([c, c_ctx[None]], axis=0)
    c_all = jnp.pad(c_all, ((0, (-c_all.shape[0]) % SUBLANES), (0, 0)))
    mod = adaln_mod(c_all, w_ada, b_ada)
    mod_l = mod[:b].reshape(b, 6, d)
    mod_c = jnp.broadcast_to(mod[b].reshape(1, 6, d), (b, 6, d))
    o = DN_QKV
    hv = DN_HEADS * DN_DV
    w_dn_qkv, w_dn_z = w_in[:, :o], w_in[:, o:o + hv]
    w_dn_ba = _pad_cols(w_in[:, o + hv:DN_COLS], LANES)
    g0 = DN_COLS
    gqk, gv = 2 * GLA_HEADS * GLA_DK, GLA_HEADS * GLA_DV
    w_gl_qk, w_gl_v = w_in[:, g0:g0 + gqk], w_in[:, g0 + gqk:g0 + gqk + gv]
    w_gl_r = w_in[:, g0 + gqk + gv:g0 + gqk + 2 * gv]
    w_gl_lr = _pad_cols(w_in[:, g0 + gqk + 2 * gv:], LANES)
    w_lat = [w.astype(BF16) for w in (w_dn_qkv, w_dn_ba, w_gl_qk, w_gl_v, w_gl_lr, w_dn_z, w_gl_r)]
    w_ctx = w_lat[:5]
    w_la = jnp.zeros((2, LANES, GLA_HEADS * GLA_DK), F32)
    for dd in range(2):
        w_la = w_la.at[dd, dd * GLA_LR:(dd + 1) * GLA_LR].set(gla_wa2[dd])
    w_la = w_la.astype(BF16)
    b_la = gla_ba.reshape(2, 1, GLA_HEADS * GLA_DK)
    w_out_b, w_q_b = w_out.astype(BF16), peer_wq.astype(BF16)
    u_p, v_p = pack_table(peer_u), pack_table(peer_v)

    def mixer(xg, ctxg, mod_cg, mod_g, after_select, after_combine):
        mod_cg, mod_g = mod_cg + after_select, mod_g + after_select
        c_qkv, c_ba, c_qk, c_v, c_lr = in_projection(ctxg, norm1_g, mod_cg[:, 0:2], w_ctx)
        l_qkv, l_ba, l_qk, l_v, l_lr, l_z, l_r = in_projection(xg, norm1_g, mod_g[:, 0:2], w_lat)
        feat_c, feat_l = dn_features_tc(c_qkv, conv_w), dn_features_tc(l_qkv, conv_w)
        dn_f = dn_scan_tc(feat_c, feat_l, c_ba, l_ba, dn_a_log, dn_dt_bias, rev=False)
        dn_b = dn_scan_tc(feat_c, feat_l, c_ba, l_ba, dn_a_log, dn_dt_bias, rev=True)
        gl_f = gla_scan_tc(c_qk, c_v, c_lr, l_qk, l_v, l_lr, w_la, b_la, rev=False)
        gl_b = gla_scan_tc(c_qk, c_v, c_lr, l_qk, l_v, l_lr, w_la, b_la, rev=True)
        return mix_out_tc(xg, dn_f, dn_b, l_z, gl_f, gl_b, l_r, mod_g, dn_norm_g + after_combine,
                          gla_norm_g, norm2_g, w_out_b, w_q_b)

    def select_and_act(h2, q, after_coef):
        n_tok = h2.shape[0] * l
        idx, gate = peer_select_tc(q.reshape(n_tok, -1), peer_keys + after_coef)
        idx2 = idx.reshape(n_tok * GROUPS_PER_TOK, PICK_GROUP)
        gate2 = gate.reshape(n_tok * GROUPS_PER_TOK, PICK_GROUP)
        return idx2, gate2, peer_act_partial_sc(u_p, idx2, h2.reshape(n_tok, d))

    def coef_and_combine(st, after_mixer):
        coef_b = peer_coef_tc(st["part"], st["gate2"] + after_mixer)
        st["y"] = peer_combine_sc(v_p, st["idx2"], coef_b)
        return _exact_zero(coef_b[0, 0])

    zero = jnp.zeros((), F32)
    slices, z_sel, z_coef = [], zero, zero
    i = 0
    for bg in _slice_sizes(b):
        g = len(slices)
        z_comb = _exact_zero(slices[g - 2]["y"][0, 0]) if g >= 2 else zero
        mod_g = mod_l[i:i + bg]
        x1, h2, q = mixer(x[i:i + bg], ctx[i:i + bg], mod_c[i:i + bg], mod_g, z_sel, z_comb)
        i += bg
        if g >= 1:
            z_coef = coef_and_combine(slices[g - 1], _exact_zero(x1[0, 0, 0]))
        idx2, gate2, part = select_and_act(h2, q, z_coef)
        z_sel = _exact_zero(gate2[0, 0])
        slices.append(dict(x1=x1, mod=mod_g, idx2=idx2, gate2=gate2, part=part))
    coef_and_combine(slices[-1], zero)
    outs = [final_tc(st["x1"], st["y"].reshape(st["x1"].shape[:2] + (-1,)), st["mod"], final_g) for st in slices]
    return jnp.concatenate(outs, axis=0)


def kernel(x, c, ctx, c_ctx, w_ada, b_ada, norm1_g, norm2_g, w_in, conv_w, dn_a_log,
           dn_dt_bias, dn_norm_g, gla_wa2, gla_ba, gla_norm_g, w_out, peer_wq, peer_keys,
           peer_u, peer_v, final_g):
    assert w_ada.shape[0] == 1, "single-layer block: the context stream is only consumed, never updated"
    return forward(x, c, ctx, c_ctx, w_ada[0], b_ada[0], norm1_g[0], norm2_g[0], w_in[0], conv_w[0],
                   dn_a_log[0], dn_dt_bias[0], dn_norm_g[0], gla_wa2[0], gla_ba[0], gla_norm_g[0],
                   w_out[0], peer_wq[0], peer_keys[0], peer_u[0], peer_v[0], final_g)
```

```python
import functools

import jax
import jax.numpy as jnp
from jax import lax
from jax.experimental import pallas as pl
from jax.experimental.pallas import tpu as pltpu
from jax.experimental.pallas import tpu_sc as plsc

GRID_W = 64
DN_HEADS = 4
DN_DK = 128
DN_DV = 128
CONV_W = 5
GLA_HEADS = 4
GLA_DK = 64
GLA_DV = 128
GLA_LR = 16
GLA_TAU = 16.0
CHUNK = 64
PEER_HEADS = 8
PEER_NKEYS = 128
PEER_DQ = 256
PEER_TOPK = 16
EPS = 1e-6
DN_QKV = 2 * DN_HEADS * DN_DK + DN_HEADS * DN_DV
DN_COLS = DN_QKV + DN_HEADS * DN_DV + 4 * DN_HEADS

SUBLANES = 8
LANES = 128
SC_LANES = 16
VMEM_LIMIT_BYTES = 48 * 1024 * 1024

TOK_TILE = 256
SELECT_TILE = 256
SCAN_BATCH = 2
GLA_SUB = 16
PICK_GROUP = 32
COMB_GROUP_STEP = 16
TOK_STEP = 8
COMB_CHUNKS = 4
ACT_UNROLL = 16
SLICE_BATCH = 2
PICKS = PEER_HEADS * PEER_TOPK
GROUPS_PER_TOK = PICKS // PICK_GROUP

F32 = jnp.float32
BF16 = jnp.bfloat16


def _cparams(*semantics):
    return pltpu.CompilerParams(dimension_semantics=semantics, vmem_limit_bytes=VMEM_LIMIT_BYTES)


def _dot(a, b):
    return jnp.dot(a.astype(BF16), b.astype(BF16), preferred_element_type=F32)


def _dot_nt(a, b):
    return lax.dot_general(a.astype(BF16), b.astype(BF16), (((1,), (1,)), ((), ())),
                           preferred_element_type=F32)


def _dot_tn(a, b):
    return lax.dot_general(a.astype(BF16), b.astype(BF16), (((0,), (0,)), ((), ())),
                           preferred_element_type=F32)


def _split(x):
    hi = x.astype(BF16)
    return hi, (x - hi.astype(F32)).astype(BF16)


def _mask_dot(mask_bf16, x):
    hi, lo = _split(x)
    return (jnp.dot(mask_bf16, hi, preferred_element_type=F32)
            + jnp.dot(mask_bf16, lo, preferred_element_type=F32))


def _softplus(x):
    return jnp.maximum(x, 0.0) + jnp.log(1.0 + jnp.exp(-jnp.abs(x)))


def _tri_masks(rev):
    r = lax.broadcasted_iota(jnp.int32, (CHUNK, CHUNK), 0)
    c = lax.broadcasted_iota(jnp.int32, (CHUNK, CHUNK), 1)
    d = (c - r) if rev else (r - c)
    return d >= 0, d > 0


def _mod_kernel(c_ref, w_ref, b_ref, o_ref):
    c = c_ref[...]
    s = c * jax.nn.sigmoid(c)
    o_ref[...] = jnp.dot(s, w_ref[...], preferred_element_type=F32,
                         precision=lax.Precision.HIGHEST) + b_ref[...]


def adaln_mod(c_all, w_ada, b_ada):
    r, d = c_all.shape
    n = w_ada.shape[1]
    tn = 512
    return pl.pallas_call(
        _mod_kernel, grid=(n // tn,),
        in_specs=[pl.BlockSpec((r, d), lambda j: (0, 0)),
                  pl.BlockSpec((d, tn), lambda j: (0, j)),
                  pl.BlockSpec((1, tn), lambda j: (0, j))],
        out_specs=pl.BlockSpec((r, tn), lambda j: (0, j)),
        out_shape=jax.ShapeDtypeStruct((r, n), F32),
        compiler_params=_cparams("arbitrary"), name="adaln_mod",
    )(c_all, w_ada, b_ada.reshape(1, n))


def _inproj_kernel(x_ref, g_ref, mod_ref, *refs):
    n_out = len(refs) // 2
    x = x_ref[0]
    y = x * lax.rsqrt(jnp.mean(x * x, axis=-1, keepdims=True) + EPS) * g_ref[...]
    h = (y * (1.0 + mod_ref[0, 1:2, :]) + mod_ref[0, 0:1, :]).astype(BF16)
    for w_ref, o_ref in zip(refs[:n_out], refs[n_out:]):
        o_ref[0] = jnp.dot(h, w_ref[...], preferred_element_type=F32)


def in_projection(x, norm_g, mod, weights):
    b, l, d = x.shape
    w_specs = [pl.BlockSpec(w.shape, lambda bi, i: (0, 0)) for w in weights]
    o_specs = [pl.BlockSpec((1, TOK_TILE, w.shape[1]), lambda bi, i: (bi, i, 0)) for w in weights]
    return pl.pallas_call(
        _inproj_kernel, grid=(b, l // TOK_TILE),
        in_specs=[pl.BlockSpec((1, TOK_TILE, d), lambda bi, i: (bi, i, 0)),
                  pl.BlockSpec((1, d), lambda bi, i: (0, 0)),
                  pl.BlockSpec((1, 2, d), lambda bi, i: (bi, 0, 0))] + w_specs,
        out_specs=o_specs,
        out_shape=[jax.ShapeDtypeStruct((b, l, w.shape[1]), F32) for w in weights],
        compiler_params=_cparams("parallel", "arbitrary"), name="in_projection",
    )(x, norm_g.reshape(1, d), mod, *weights)


def _dn_feature_kernel(x_ref, w_ref, o_ref):
    x = x_ref[0]
    n = x.shape[0]
    t = lax.broadcasted_iota(jnp.int32, (n, 1), 0)
    pad = CONV_W // 2
    acc = w_ref[0, pad:pad + 1, :] * x
    for j in range(CONV_W):
        s = j - pad
        if s == 0:
            continue
        xs = pltpu.roll(x, (-s) % n, axis=0)
        bad = (t < -s) if s < 0 else (t >= n - s)
        acc = acc + w_ref[0, j:j + 1, :] * jnp.where(bad, 0.0, xs)
    y = acc * jax.nn.sigmoid(acc)
    kind = pl.program_id(1) // DN_HEADS
    inv = lax.rsqrt(jnp.sum(y * y, axis=-1, keepdims=True) + EPS)
    scale = jnp.where(kind == 0, inv * DN_DK ** -0.5, jnp.where(kind == 1, inv, 1.0))
    o_ref[0] = y * scale


def dn_features_tc(qkv, conv_w):
    b, l, n = qkv.shape
    nblk = n // LANES
    w = jnp.zeros((nblk, SUBLANES, LANES), F32).at[:, :CONV_W].set(
        conv_w.reshape(CONV_W, nblk, LANES).transpose(1, 0, 2))
    return pl.pallas_call(
        _dn_feature_kernel, grid=(b, nblk),
        in_specs=[pl.BlockSpec((1, l, LANES), lambda bi, j: (bi, 0, j)),
                  pl.BlockSpec((1, SUBLANES, LANES), lambda bi, j: (j, 0, 0))],
        out_specs=pl.BlockSpec((1, l, LANES), lambda bi, j: (bi, 0, j)),
        out_shape=jax.ShapeDtypeStruct((b, l, n), F32),
        compiler_params=_cparams("parallel", "arbitrary"), name="dn_features",
    )(qkv, w)


def _scan_chunks(rev, n_ctx, n_lat):
    if rev:
        ctx = lambda j: jnp.maximum(n_ctx - 1 - j, 0)
        lat = lambda j: jnp.where(j < n_ctx, n_lat - 1, n_lat - 1 - (j - n_ctx))
    else:
        ctx = lambda j: jnp.minimum(j, n_ctx - 1)
        lat = lambda j: jnp.maximum(j - n_ctx, 0)
    return ctx, lat


def _dn_scan_kernel(rev, dirn, n_ctx, alog_ref, dtb_ref, fc_ref, fl_ref, bac_ref, bal_ref, o_ref, s_ref):
    step = pl.program_id(1)

    @pl.when(step == 0)
    def _():
        s_ref[...] = jnp.zeros_like(s_ref)

    in_ctx = step < n_ctx
    f = jnp.where(in_ctx, fc_ref[0], fl_ref[0])
    ba = jnp.where(in_ctx, bac_ref[0], bal_ref[0])
    nh, hd, n = DN_HEADS, DN_HEADS * DN_DK, DN_HEADS * CHUNK
    stack = lambda base, w: jnp.concatenate([f[:, base + h * w:base + (h + 1) * w] for h in range(nh)], axis=0)
    q_s, k_s, v_s = stack(0, DN_DK), stack(hd, DN_DK), stack(2 * hd, DN_DV)
    incl, _ = _tri_masks(rev)
    beta_all = jax.nn.sigmoid(ba)
    g_all = -jnp.exp(alog_ref[...]) * _softplus(ba + dtb_ref[...])
    gc_all = _mask_dot(incl.astype(BF16), g_all)
    gc_t = jnp.concatenate([gc_all, gc_all], axis=0).T
    g_tot = jnp.sum(g_all, axis=0, keepdims=True)
    cb = [dirn * nh + h for h in range(nh)]
    cg = [2 * nh + c for c in cb]
    col = lambda a, cs: jnp.concatenate([a[:, c:c + 1] for c in cs], axis=0)
    beta_c, gc_c = col(beta_all, cb), col(gc_all, cg)
    gtot_c = jnp.concatenate([jnp.broadcast_to(g_tot[:, c:c + 1], (CHUNK, 1)) for c in cg], axis=0)
    gc_r = jnp.concatenate([gc_t[c:c + 1, :CHUNK] for c in cg], axis=1)
    r = lax.broadcasted_iota(jnp.int32, (n, n), 0)
    c = lax.broadcasted_iota(jnp.int32, (n, n), 1)
    same = (r // CHUNK) == (c // CHUNK)
    d = (c - r) if rev else (r - c)
    incl_bd, strict_bd = same & (d >= 0), same & (d > 0)
    eye = (r == c).astype(F32)
    decay = jnp.where(incl_bd, jnp.exp(jnp.where(incl_bd, gc_c - gc_r, 0.0)), 0.0)
    kb_s = k_s * beta_c
    lower = jnp.where(strict_bd, _dot_nt(kb_s, k_s) * decay, 0.0)
    eg_c = jnp.exp(gc_c)
    inv = eye - lower
    pw = lower
    for _ in range(5):
        pw = _dot(pw, pw)
        inv = inv + _dot(inv, pw)
    sol = _dot(inv, jnp.concatenate([v_s * beta_c, kb_s * eg_c], axis=-1))
    u_s, w_s = sol[:, :DN_DV], sol[:, DN_DV:]
    k_dec = k_s * jnp.exp(gtot_c - gc_c)
    rb = lax.broadcasted_iota(jnp.int32, (n, DN_DK), 0) // CHUNK
    expand = lambda x: jnp.concatenate([jnp.where(rb == h, x, 0.0) for h in range(nh)], axis=1)
    s = s_ref[...]
    v_new = u_s - _dot(expand(w_s), s)
    a_qk = _dot_nt(q_s, k_s) * decay
    o_s = _dot(expand(q_s * eg_c), s) + _dot(a_qk, v_new)
    gl_rows = jnp.concatenate([jnp.broadcast_to(jnp.exp(g_tot[:, cc:cc + 1]), (DN_DK, 1)) for cc in cg], axis=0)
    s_ref[...] = s * gl_rows + _dot_tn(expand(k_dec), v_new)
    o_ref[0] = jnp.concatenate([o_s[h * CHUNK:(h + 1) * CHUNK] for h in range(nh)], axis=1)


def dn_scan_tc(feat_c, feat_l, ba_c, ba_l, a_log, dt_bias, rev):
    b, l, nf = feat_l.shape
    n_ctx, n_lat = feat_c.shape[1] // CHUNK, l // CHUNK
    dirn = 1 if rev else 0
    cc, lc = _scan_chunks(rev, n_ctx, n_lat)
    lanes = lambda p: jnp.zeros((1, LANES), F32).at[0, 2 * DN_HEADS:4 * DN_HEADS].set(p.reshape(-1))
    vec = pl.BlockSpec((1, LANES), lambda bi, j: (0, 0))
    return pl.pallas_call(
        functools.partial(_dn_scan_kernel, rev, dirn, n_ctx), grid=(b, n_ctx + n_lat),
        in_specs=[vec, vec,
                  pl.BlockSpec((1, CHUNK, nf), lambda bi, j: (bi, cc(j), 0)),
                  pl.BlockSpec((1, CHUNK, nf), lambda bi, j: (bi, lc(j), 0)),
                  pl.BlockSpec((1, CHUNK, LANES), lambda bi, j: (bi, cc(j), 0)),
                  pl.BlockSpec((1, CHUNK, LANES), lambda bi, j: (bi, lc(j), 0))],
        out_specs=pl.BlockSpec((1, CHUNK, DN_HEADS * DN_DV), lambda bi, j: (bi, lc(j), 0)),
        out_shape=jax.ShapeDtypeStruct((b, l, DN_HEADS * DN_DV), F32),
        scratch_shapes=[pltpu.VMEM((DN_HEADS * DN_DK, DN_DV), F32)],
        compiler_params=_cparams("parallel", "arbitrary"), name="dn_scan_bwd" if rev else "dn_scan_fwd",
    )(lanes(a_log), lanes(dt_bias), feat_c, feat_l, ba_c, ba_l)


def _from_grid_cols(blk, n):
    cols = blk.shape[1] // n
    return jnp.concatenate([blk[:, i * n:(i + 1) * n] for i in range(cols)], axis=0)


def _gla_scan_kernel(rev, dirn, n_ctx, qkc_ref, vc_ref, lrc_ref, qkl_ref, vl_ref, lrl_ref,
                     wla_ref, bla_ref, o_ref, s_ref):
    step = pl.program_id(1)

    @pl.when(step == 0)
    def _():
        s_ref[...] = jnp.zeros_like(s_ref)

    in_ctx = step < n_ctx
    hk, hv = GLA_HEADS * GLA_DK, GLA_HEADS * GLA_DV
    qk = jnp.where(in_ctx, qkc_ref[0], _from_grid_cols(qkl_ref[0], 2 * hk))
    vv = jnp.where(in_ctx, vc_ref[0], _from_grid_cols(vl_ref[0], hv))
    lr = jnp.where(in_ctx, lrc_ref[0], _from_grid_cols(lrl_ref[0], LANES))
    incl, _ = _tri_masks(rev)
    incl_b = incl.astype(BF16)
    pre = _dot(lr, wla_ref[0]) + bla_ref[0]
    la_all = -_softplus(-pre) * (1.0 / GLA_TAU)
    bc_all = _mask_dot(incl_b, la_all)
    b_tot_all = jnp.sum(la_all, axis=0, keepdims=True)
    outs = []
    for h in range(GLA_HEADS):
        q = qk[:, h * GLA_DK:(h + 1) * GLA_DK] * GLA_DK ** -0.5
        k = qk[:, hk + h * GLA_DK:hk + (h + 1) * GLA_DK]
        v = vv[:, h * GLA_DV:(h + 1) * GLA_DV]
        bc = bc_all[:, h * GLA_DK:(h + 1) * GLA_DK]
        b_tot = b_tot_all[:, h * GLA_DK:(h + 1) * GLA_DK]
        st = s_ref[h]
        o = _dot_nt(q * jnp.exp(bc), st)
        parts = []
        for i in range(CHUNK // GLA_SUB):
            lo_r, hi_r = i * GLA_SUB, (i + 1) * GLA_SUB
            if rev:
                ref = bc[hi_r - 1:hi_r]
                c0, c1 = lo_r, CHUNK
            else:
                ref = bc[lo_r:lo_r + 1]
                c0, c1 = 0, hi_r
            qi = q[lo_r:hi_r] * jnp.exp(bc[lo_r:hi_r] - ref)
            ki = k[c0:c1] * jnp.exp(ref - bc[c0:c1])
            att = _dot_nt(qi, ki)
            rg = lax.broadcasted_iota(jnp.int32, (GLA_SUB, c1 - c0), 0) + lo_r
            cg = lax.broadcasted_iota(jnp.int32, (GLA_SUB, c1 - c0), 1) + c0
            keep = (cg > rg) if rev else (cg < rg)
            parts.append(_dot(jnp.where(keep, att, 0.0), v[c0:c1]))
        diag = jnp.sum(q * k, axis=-1, keepdims=True) * v
        outs.append(o + jnp.concatenate(parts, axis=0) + diag)
        k_dec = k * jnp.exp(b_tot - bc)
        s_ref[h] = st * jnp.exp(b_tot) + _dot_tn(v, k_dec)
    o = jnp.concatenate(outs, axis=-1)
    rows = o_ref.shape[1]
    o_ref[0] = jnp.concatenate([o[i * rows:(i + 1) * rows] for i in range(CHUNK // rows)], axis=-1)


def gla_scan_tc(qk_c, v_c, lr_c, qk_l, v_l, lr_l, w_la, b_la, rev):
    b, l, _ = qk_l.shape
    rows = l // GRID_W
    cols = CHUNK // rows
    n_ctx, n_lat = qk_c.shape[1] // CHUNK, l // CHUNK
    dirn = 1 if rev else 0
    cc, lc = _scan_chunks(rev, n_ctx, n_lat)
    hv = GLA_HEADS * GLA_DV
    ctx_blk = lambda a: pl.BlockSpec((1, CHUNK, a.shape[2]), lambda bi, j: (bi, cc(j), 0))
    lat_blk = lambda n: pl.BlockSpec((1, rows, cols * n), lambda bi, j: (bi, 0, lc(j)))
    grid_view = lambda a: a.reshape(b, rows, GRID_W * a.shape[2])
    out = pl.pallas_call(
        functools.partial(_gla_scan_kernel, rev, dirn, n_ctx), grid=(b, n_ctx + n_lat),
        in_specs=[ctx_blk(qk_c), ctx_blk(v_c), ctx_blk(lr_c),
                  lat_blk(qk_l.shape[2]), lat_blk(v_l.shape[2]), lat_blk(lr_l.shape[2]),
                  pl.BlockSpec((1,) + w_la.shape[1:], lambda bi, j: (dirn, 0, 0)),
                  pl.BlockSpec((1,) + b_la.shape[1:], lambda bi, j: (dirn, 0, 0))],
        out_specs=lat_blk(hv),
        out_shape=jax.ShapeDtypeStruct((b, rows, GRID_W * hv), F32),
        scratch_shapes=[pltpu.VMEM((GLA_HEADS, GLA_DV, GLA_DK), F32)],
        compiler_params=_cparams("parallel", "arbitrary"), name="gla_scan_bwd" if rev else "gla_scan_fwd",
    )(qk_c, v_c, lr_c, grid_view(qk_l), grid_view(v_l), grid_view(lr_l), w_la, b_la)
    return out.reshape(b, l, hv)


def _head_norm_gate(o, gate, g, n_heads, dv):
    parts = []
    for h in range(n_heads):
        oh = o[:, h * dv:(h + 1) * dv]
        gh = gate[:, h * dv:(h + 1) * dv]
        yh = oh * lax.rsqrt(jnp.mean(oh * oh, axis=-1, keepdims=True) + EPS) * g
        parts.append(yh * (gh * jax.nn.sigmoid(gh)))
    return parts


def _mix_out_kernel(x_ref, dnf_ref, dnb_ref, z_ref, glf_ref, glb_ref, r_ref, mod_ref, dng_ref,
                    glg_ref, n2g_ref, wout_ref, wq_ref, x1_ref, h2_ref, q_ref):
    parts = (_head_norm_gate(dnf_ref[0] + dnb_ref[0], z_ref[0], dng_ref[...], DN_HEADS, DN_DV)
             + _head_norm_gate(glf_ref[0] + glb_ref[0], r_ref[0], glg_ref[...], GLA_HEADS, GLA_DV))
    y = jnp.dot(jnp.concatenate(parts, axis=-1).astype(BF16), wout_ref[...], preferred_element_type=F32)
    x1 = x_ref[0] + mod_ref[0, 2:3, :] * y
    x1_ref[0] = x1
    n = x1 * lax.rsqrt(jnp.mean(x1 * x1, axis=-1, keepdims=True) + EPS) * n2g_ref[...]
    h2 = n * (1.0 + mod_ref[0, 4:5, :]) + mod_ref[0, 3:4, :]
    h2_ref[0] = h2
    q_ref[0] = jnp.dot(h2.astype(BF16), wq_ref[...], preferred_element_type=F32)


def mix_out_tc(x, dn_f, dn_b, z, gl_f, gl_b, r, mod_l, dn_g, gla_g, n2_g, w_out, w_q):
    b, l, d = x.shape
    tok = lambda n: pl.BlockSpec((1, TOK_TILE, n), lambda bi, i: (bi, i, 0))
    full = lambda a: pl.BlockSpec(a.shape, lambda bi, i: (0,) * a.ndim)
    dn_g, gla_g, n2_g = dn_g.reshape(1, -1), gla_g.reshape(1, -1), n2_g.reshape(1, -1)
    nq = w_q.shape[1]
    return pl.pallas_call(
        _mix_out_kernel, grid=(b, l // TOK_TILE),
        in_specs=[tok(d), tok(dn_f.shape[2]), tok(dn_b.shape[2]), tok(z.shape[2]),
                  tok(gl_f.shape[2]), tok(gl_b.shape[2]), tok(r.shape[2]),
                  pl.BlockSpec((1,) + mod_l.shape[1:], lambda bi, i: (bi, 0, 0)),
                  full(dn_g), full(gla_g), full(n2_g), full(w_out), full(w_q)],
        out_specs=[tok(d), tok(d), tok(nq)],
        out_shape=[jax.ShapeDtypeStruct((b, l, d), F32), jax.ShapeDtypeStruct((b, l, d), F32),
                   jax.ShapeDtypeStruct((b, l, nq), F32)],
        compiler_params=_cparams("parallel", "arbitrary"), name="mix_out",
    )(x, dn_f, dn_b, z, gl_f, gl_b, r, mod_l, dn_g, gla_g, n2_g, w_out, w_q)


def _top_rows(s, k, payload=None):
    n = s.shape[0]
    row = lax.broadcasted_iota(jnp.int32, s.shape, 0).astype(F32)
    vals, picked = [], []
    for _ in range(k):
        m = jnp.max(s, axis=0, keepdims=True)
        first = jnp.min(jnp.where(s == m, row, float(n)), axis=0, keepdims=True)
        sel = row == first
        vals.append(m)
        if payload is None:
            picked.append(first)
        else:
            picked.append(jnp.max(jnp.where(sel, payload, -1.0), axis=0, keepdims=True))
        s = jnp.where(sel, -jnp.inf, s)
    return jnp.concatenate(vals, axis=0), jnp.concatenate(picked, axis=0)


def _candidate_rows(s0, i0, s1, i1):
    k = s0.shape[0]
    wide = SUBLANES
    blocks_s = [s0[0:1] + s1]
    blocks_i = [i0[0:1] * float(PEER_NKEYS) + i1]
    col = lax.broadcasted_iota(jnp.int32, (wide, s0.shape[1]), 0)
    for i in range(1, wide):
        keep = col < (k // (i + 1))
        blocks_s.append(jnp.where(keep, s0[i:i + 1] + s1[0:wide], -jnp.inf))
        blocks_i.append(i0[i:i + 1] * float(PEER_NKEYS) + i1[0:wide])
    blocks_s.append(s0[wide:k] + s1[0:1])
    blocks_i.append(i0[wide:k] * float(PEER_NKEYS) + i1[0:1])
    return jnp.concatenate(blocks_s, axis=0), jnp.concatenate(blocks_i, axis=0)


def _select_kernel(q_ref, k_ref, idx_ref, gate_ref, idx_s, gate_s):
    half = PEER_DQ // 2

    def head(h, carry):
        tops = []
        for p in range(2):
            qp = q_ref[:, pl.ds(pl.multiple_of(h * PEER_DQ + p * half, half), half)]
            s = lax.dot_general(k_ref[h, p], qp, (((1,), (1,)), ((), ())),
                                preferred_element_type=F32,
                                precision=lax.Precision.HIGHEST)
            tops.append(_top_rows(s, PEER_TOPK))
        (s0, i0), (s1, i1) = tops
        cand_s, cand_i = _candidate_rows(s0, i0, s1, i1)
        best_s, idx = _top_rows(cand_s, PEER_TOPK, payload=cand_i)
        e = jnp.exp(best_s - best_s[0:1])
        r0 = pl.multiple_of(h * PEER_TOPK, PEER_TOPK)
        idx_s[pl.ds(r0, PEER_TOPK), :] = idx
        gate_s[pl.ds(r0, PEER_TOPK), :] = e / jnp.sum(e, axis=0, keepdims=True)
        return carry

    lax.fori_loop(0, PEER_HEADS, head, 0)
    idx_ref[...] = idx_s[...].T.astype(jnp.int32)
    gate_ref[...] = gate_s[...].T


def peer_select_tc(q, keys):
    n_tok = q.shape[0]
    out_spec = pl.BlockSpec((SELECT_TILE, PICKS), lambda i: (i, 0))
    return pl.pallas_call(
        _select_kernel,
        grid=(n_tok // SELECT_TILE,),
        in_specs=[pl.BlockSpec((SELECT_TILE, q.shape[1]), lambda i: (i, 0)),
                  pl.BlockSpec(keys.shape, lambda i: (0, 0, 0, 0))],
        out_specs=[out_spec, out_spec],
        out_shape=[jax.ShapeDtypeStruct((n_tok, PICKS), jnp.int32),
                   jax.ShapeDtypeStruct((n_tok, PICKS), F32)],
        scratch_shapes=[pltpu.VMEM((PICKS, SELECT_TILE), F32), pltpu.VMEM((PICKS, SELECT_TILE), F32)],
        compiler_params=_cparams("parallel"), name="peer_select",
    )(q, keys)


def _sc_mesh():
    return plsc.VectorSubcoreMesh(core_axis_name="c", subcore_axis_name="s")


def _sc_pipeline(body, n_steps, in_specs, out_specs, operands):
    pltpu.emit_pipeline(
        body, grid=(n_steps,), in_specs=in_specs, out_specs=out_specs,
        core_axis_name=("c", "s"), dimension_semantics=(pltpu.PARALLEL,),
        trace_scopes=False,
    )(*operands)


def pack_table(t):
    half = t.shape[1] // 2
    bits = lax.bitcast_convert_type(t, jnp.uint32)
    lo_bits, hi_bits = bits[:, :half], bits[:, half:]
    low = (lo_bits + jnp.uint32(0x7FFF) + ((lo_bits >> 16) & jnp.uint32(1))) >> 16
    mag = hi_bits & jnp.uint32(0x7FFFFFFF)
    top = jnp.where(mag >= jnp.uint32(0x10000),
                    (mag - low + jnp.uint32(0x8000)) & jnp.uint32(0xFFFF0000), jnp.uint32(0))
    word = (hi_bits & jnp.uint32(0x80000000)) | top | low
    return lax.bitcast_convert_type(word, jnp.int32)


def _unpack(w):
    return plsc.bitcast(w << 16, F32), plsc.bitcast(w, F32)


def peer_act_partial_sc(u_p, idx2, h):
    n_groups, (n_tok, d) = idx2.shape[0], h.shape
    half = d // 2
    nsub = TOK_STEP * GROUPS_PER_TOK
    n_chunks = half // SC_LANES

    @functools.partial(
        pl.kernel, mesh=_sc_mesh(), compiler_params=pltpu.CompilerParams(needs_layout_passes=False),
        out_type=jax.ShapeDtypeStruct((n_groups, PICK_GROUP * SC_LANES), F32),
        scratch_types=[pltpu.VMEM((2, PICK_GROUP, half), jnp.int32),
                       pltpu.SemaphoreType.DMA((2,))],
    )
    def k(u_hbm, i_hbm, h_hbm, o_hbm, rows, sems):
        def body(i_v, h_v, o_v):
            def fetch(j, slot):
                return pltpu.make_async_copy(u_hbm.at[i_v.at[j]], rows.at[slot], sems.at[slot])

            fetch(0, 0).start()

            def sub(j, carry):
                slot = j % 2

                @pl.when(j + 1 < nsub)
                def _():
                    fetch(j + 1, 1 - slot).start()

                fetch(j, slot).wait()
                t = j // GROUPS_PER_TOK

                def picks(g, carry2):
                    kb = g * ACT_UNROLL
                    accs = [None] * ACT_UNROLL
                    for c in range(n_chunks):
                        h_lo = h_v[t, pl.ds(c * SC_LANES, SC_LANES)]
                        h_hi = h_v[t, pl.ds(half + c * SC_LANES, SC_LANES)]
                        for i in range(ACT_UNROLL):
                            lo, hi = _unpack(rows[slot, kb + i, pl.ds(c * SC_LANES, SC_LANES)])
                            p = lo * h_lo + hi * h_hi
                            accs[i] = p if accs[i] is None else accs[i] + p
                    for i in range(ACT_UNROLL):
                        o_v[j, pl.ds((kb + i) * SC_LANES, SC_LANES)] = accs[i]
                    return carry2

                lax.fori_loop(0, PICK_GROUP // ACT_UNROLL, picks, 0)
                return carry

            lax.fori_loop(0, nsub, sub, 0)

        _sc_pipeline(
            body, n_tok // TOK_STEP,
            [pl.BlockSpec((nsub, PICK_GROUP), lambda i: (i, 0)),
             pl.BlockSpec((TOK_STEP, d), lambda i: (i, 0))],
            [pl.BlockSpec((nsub, PICK_GROUP * SC_LANES), lambda i: (i, 0))],
            (i_hbm, h_hbm, o_hbm))

    return k(u_p, idx2, h)


def peer_combine_sc(v_p, idx2, coef_b):
    n_groups = idx2.shape[0]
    half = v_p.shape[1]
    nsub = COMB_GROUP_STEP
    n_chunks = half // SC_LANES

    @functools.partial(
        pl.kernel, mesh=_sc_mesh(), compiler_params=pltpu.CompilerParams(needs_layout_passes=False),
        out_type=jax.ShapeDtypeStruct((n_groups, 2 * half), F32),
        scratch_types=[pltpu.VMEM((2, PICK_GROUP, half), jnp.int32),
                       pltpu.SemaphoreType.DMA((2,))],
    )
    def k(v_hbm, i_hbm, c_hbm, o_hbm, rows, sems):
        def body(i_v, c_v, o_v):
            def fetch(j, slot):
                return pltpu.make_async_copy(v_hbm.at[i_v.at[j]], rows.at[slot], sems.at[slot])

            fetch(0, 0).start()

            def sub(j, carry0):
                slot = j % 2

                @pl.when(j + 1 < nsub)
                def _():
                    fetch(j + 1, 1 - slot).start()

                fetch(j, slot).wait()
                cks = [c_v[j, pl.ds(kk * SC_LANES, SC_LANES)] for kk in range(PICK_GROUP)]

                def chunk_pair(c2, carry):
                    sums = []
                    for cc in range(COMB_CHUNKS):
                        l = (c2 * COMB_CHUNKS + cc) * SC_LANES
                        tot_lo, tot_hi = None, None
                        for kk in range(PICK_GROUP):
                            lo, hi = _unpack(rows[slot, kk, pl.ds(l, SC_LANES)])
                            a, b_ = cks[kk] * lo, cks[kk] * hi
                            tot_lo = a if tot_lo is None else tot_lo + a
                            tot_hi = b_ if tot_hi is None else tot_hi + b_
                        sums.append((l, tot_lo, tot_hi))
                    for l, tot_lo, tot_hi in sums:
                        o_v[j, pl.ds(l, SC_LANES)] = tot_lo
                        o_v[j, pl.ds(half + l, SC_LANES)] = tot_hi
                    return carry

                lax.fori_loop(0, n_chunks // COMB_CHUNKS, chunk_pair, 0)
                return carry0

            lax.fori_loop(0, nsub, sub, 0)

        _sc_pipeline(
            body, n_groups // nsub,
            [pl.BlockSpec((nsub, PICK_GROUP), lambda i: (i, 0)),
             pl.BlockSpec((nsub, PICK_GROUP * SC_LANES), lambda i: (i, 0))],
            [pl.BlockSpec((nsub, 2 * half), lambda i: (i, 0))],
            (i_hbm, c_hbm, o_hbm))

    return k(v_p, idx2, coef_b)


def _segment_matrix():
    r = lax.broadcasted_iota(jnp.int32, (PICK_GROUP * SC_LANES, PICK_GROUP), 0) // SC_LANES
    c = lax.broadcasted_iota(jnp.int32, (PICK_GROUP * SC_LANES, PICK_GROUP), 1)
    return (r == c).astype(F32)


def _coef_kernel(part_ref, gate_ref, o_ref):
    seg = _segment_matrix()
    act = jnp.dot(part_ref[...], seg, preferred_element_type=F32, precision=lax.Precision.HIGHEST)
    coef = gate_ref[...] * (0.5 * act * (1.0 + lax.erf(act * (2.0 ** -0.5))))
    o_ref[...] = lax.dot_general(coef, seg, (((1,), (1,)), ((), ())), preferred_element_type=F32,
                                 precision=lax.Precision.HIGHEST)


def peer_coef_tc(part, gate2):
    n_groups, width = part.shape
    tile = 1024
    return pl.pallas_call(
        _coef_kernel,
        grid=(n_groups // tile,),
        in_specs=[pl.BlockSpec((tile, width), lambda i: (i, 0)),
                  pl.BlockSpec((tile, PICK_GROUP), lambda i: (i, 0))],
        out_specs=pl.BlockSpec((tile, width), lambda i: (i, 0)),
        out_shape=jax.ShapeDtypeStruct((n_groups, width), F32),
        compiler_params=_cparams("parallel"), name="peer_coef",
    )(part, gate2)


def _final_kernel(x_ref, y_ref, mod_ref, g_ref, o_ref):
    d = x_ref.shape[2]
    y = y_ref[0, :, 0:d]
    for p in range(1, y_ref.shape[2] // d):
        y = y + y_ref[0, :, p * d:(p + 1) * d]
    x = x_ref[0] + mod_ref[0, 5:6, :] * y
    o_ref[0] = x * lax.rsqrt(jnp.mean(x * x, axis=-1, keepdims=True) + EPS) * g_ref[...]


def final_tc(x1, y_parts, mod_l, final_g):
    b, l, d = x1.shape
    tok = pl.BlockSpec((1, TOK_TILE, d), lambda bi, i: (bi, i, 0))
    return pl.pallas_call(
        _final_kernel, grid=(b, l // TOK_TILE),
        in_specs=[tok, pl.BlockSpec((1, TOK_TILE, y_parts.shape[2]), lambda bi, i: (bi, i, 0)),
                  pl.BlockSpec((1,) + mod_l.shape[1:], lambda bi, i: (bi, 0, 0)),
                  pl.BlockSpec((1, d), lambda bi, i: (0, 0))],
        out_specs=tok, out_shape=jax.ShapeDtypeStruct((b, l, d), F32),
        compiler_params=_cparams("parallel", "arbitrary"), name="final_norm",
    )(x1, y_parts, mod_l, final_g.reshape(1, d))


def _pad_cols(w, n):
    return jnp.pad(w, ((0, 0), (0, n - w.shape[1])))


def _slice_sizes(b):
    if b % SLICE_BATCH or SLICE_BATCH % 2 or b < 2 * SLICE_BATCH:
        return [1] * b
    half = SLICE_BATCH // 2
    return [half] + [SLICE_BATCH] * (b // SLICE_BATCH - 1) + [half]


def _exact_zero(v):
    return jnp.minimum(jnp.abs(v), 0.0)


def forward(x, c, ctx, c_ctx, w_ada, b_ada, norm1_g, norm2_g, w_in, conv_w, dn_a_log,
            dn_dt_bias, dn_norm_g, gla_wa2, gla_ba, gla_norm_g, w_out, peer_wq, peer_keys,
            peer_u, peer_v, final_g):
    b, l, d = x.shape
    c_all = jnp.concatenate([c, c_ctx[None]], axis=0)
    c_all = jnp.pad(c_all, ((0, (-c_all.shape[0]) % SUBLANES), (0, 0)))
    mod = adaln_mod(c_all, w_ada, b_ada)
    mod_l = mod[:b].reshape(b, 6, d)
    mod_c = jnp.broadcast_to(mod[b].reshape(1, 6, d), (b, 6, d))
    o = DN_QKV
    hv = DN_HEADS * DN_DV
    w_dn_qkv, w_dn_z = w_in[:, :o], w_in[:, o:o + hv]
    w_dn_ba = _pad_cols(w_in[:, o + hv:DN_COLS], LANES)
    g0 = DN_COLS
    gqk, gv = 2 * GLA_HEADS * GLA_DK, GLA_HEADS * GLA_DV
    w_gl_qk, w_gl_v = w_in[:, g0:g0 + gqk], w_in[:, g0 + gqk:g0 + gqk + gv]
    w_gl_r = w_in[:, g0 + gqk + gv:g0 + gqk + 2 * gv]
    w_gl_lr = _pad_cols(w_in[:, g0 + gqk + 2 * gv:], LANES)
    w_lat = [w.astype(BF16) for w in (w_dn_qkv, w_dn_ba, w_gl_qk, w_gl_v, w_gl_lr, w_dn_z, w_gl_r)]
    w_ctx = w_lat[:5]
    w_la = jnp.zeros((2, LANES, GLA_HEADS * GLA_DK), F32)
    for dd in range(2):
        w_la = w_la.at[dd, dd * GLA_LR:(dd + 1) * GLA_LR].set(gla_wa2[dd])
    w_la = w_la.astype(BF16)
    b_la = gla_ba.reshape(2, 1, GLA_HEADS * GLA_DK)
    w_out_b, w_q_b = w_out.astype(BF16), peer_wq.astype(BF16)
    u_p, v_p = pack_table(peer_u), pack_table(peer_v)

    def mixer(xg, ctxg, mod_cg, mod_g, after_select, after_combine):
        mod_cg, mod_g = mod_cg + after_select, mod_g + after_select
        c_qkv, c_ba, c_qk, c_v, c_lr = in_projection(ctxg, norm1_g, mod_cg[:, 0:2], w_ctx)
        l_qkv, l_ba, l_qk, l_v, l_lr, l_z, l_r = in_projection(xg, norm1_g, mod_g[:, 0:2], w_lat)
        feat_c, feat_l = dn_features_tc(c_qkv, conv_w), dn_features_tc(l_qkv, conv_w)
        dn_f = dn_scan_tc(feat_c, feat_l, c_ba, l_ba, dn_a_log, dn_dt_bias, rev=False)
        dn_b = dn_scan_tc(feat_c, feat_l, c_ba, l_ba, dn_a_log, dn_dt_bias, rev=True)
        gl_f = gla_scan_tc(c_qk, c_v, c_lr, l_qk, l_v, l_lr, w_la, b_la, rev=False)
        gl_b = gla_scan_tc(c_qk, c_v, c_lr, l_qk, l_v, l_lr, w_la, b_la, rev=True)
        return mix_out_tc(xg, dn_f, dn_b, l_z, gl_f, gl_b, l_r, mod_g, dn_norm_g + after_combine,
                          gla_norm_g, norm2_g, w_out_b, w_q_b)

    def select_and_act(h2, q, after_coef):
        n_tok = h2.shape[0] * l
        idx, gate = peer_select_tc(q.reshape(n_tok, -1), peer_keys + after_coef)
        idx2 = idx.reshape(n_tok * GROUPS_PER_TOK, PICK_GROUP)
        gate2 = gate.reshape(n_tok * GROUPS_PER_TOK, PICK_GROUP)
        return idx2, gate2, peer_act_partial_sc(u_p, idx2, h2.reshape(n_tok, d))

    def coef_and_combine(st, after_mixer):
        coef_b = peer_coef_tc(st["part"], st["gate2"] + after_mixer)
        st["y"] = peer_combine_sc(v_p, st["idx2"], coef_b)
        return _exact_zero(coef_b[0, 0])

    zero = jnp.zeros((), F32)
    slices, z_sel, z_coef = [], zero, zero
    i = 0
    for bg in _slice_sizes(b):
        g = len(slices)
        z_comb = _exact_zero(slices[g - 2]["y"][0, 0]) if g >= 2 else zero
        mod_g = mod_l[i:i + bg]
        x1, h2, q = mixer(x[i:i + bg], ctx[i:i + bg], mod_c[i:i + bg], mod_g, z_sel, z_comb)
        i += bg
        if g >= 1:
            z_coef = coef_and_combine(slices[g - 1], _exact_zero(x1[0, 0, 0]))
        idx2, gate2, part = select_and_act(h2, q, z_coef)
        z_sel = _exact_zero(gate2[0, 0])
        slices.append(dict(x1=x1, mod=mod_g, idx2=idx2, gate2=gate2, part=part))
    coef_and_combine(slices[-1], zero)
    outs = [final_tc(st["x1"], st["y"].reshape(st["x1"].shape[:2] + (-1,)), st["mod"], final_g) for st in slices]
    return jnp.concatenate(outs, axis=0)


def kernel(x, c, ctx, c_ctx, w_ada, b_ada, norm1_g, norm2_g, w_in, conv_w, dn_a_log,
           dn_dt_bias, dn_norm_g, gla_wa2, gla_ba, gla_norm_g, w_out, peer_wq, peer_keys,
           peer_u, peer_v, final_g):
    assert w_ada.shape[0] == 1, "single-layer block: the context stream is only consumed, never updated"
    return forward(x, c, ctx, c_ctx, w_ada[0], b_ada[0], norm1_g[0], norm2_g[0], w_in[0], conv_w[0],
                   dn_a_log[0], dn_dt_bias[0], dn_norm_g[0], gla_wa2[0], gla_ba[0], gla_norm_g[0],
                   w_out[0], peer_wq[0], peer_keys[0], peer_u[0], peer_v[0], final_g)
```

```python
import functools

import jax
import jax.numpy as jnp
from jax import lax
from jax.experimental import pallas as pl
from jax.experimental.pallas import tpu as pltpu
from jax.experimental.pallas import tpu_sc as plsc

GRID_W = 64
DN_HEADS = 4
DN_DK = 128
DN_DV = 128
CONV_W = 5
GLA_HEADS = 4
GLA_DK = 64
GLA_DV = 128
GLA_LR = 16
GLA_TAU = 16.0
CHUNK = 64
PEER_HEADS = 8
PEER_NKEYS = 128
PEER_DQ = 256
PEER_TOPK = 16
EPS = 1e-6
DN_QKV = 2 * DN_HEADS * DN_DK + DN_HEADS * DN_DV
DN_COLS = DN_QKV + DN_HEADS * DN_DV + 4 * DN_HEADS

SUBLANES = 8
LANES = 128
SC_LANES = 16
VMEM_LIMIT_BYTES = 48 * 1024 * 1024

TOK_TILE = 256
SELECT_TILE = 256
SCAN_BATCH = 2
GLA_SUB = 16
PICK_GROUP = 32
COMB_GROUP_STEP = 16
TOK_STEP = 8
COMB_CHUNKS = 4
ACT_BF16_TERMS = 4
ACT_UNROLL = 16
SLICE_BATCH = 2
PICKS = PEER_HEADS * PEER_TOPK
GROUPS_PER_TOK = PICKS // PICK_GROUP

F32 = jnp.float32
BF16 = jnp.bfloat16


def _cparams(*semantics):
    return pltpu.CompilerParams(dimension_semantics=semantics, vmem_limit_bytes=VMEM_LIMIT_BYTES)


def _dot(a, b):
    return jnp.dot(a.astype(BF16), b.astype(BF16), preferred_element_type=F32)


def _dot_nt(a, b):
    return lax.dot_general(a.astype(BF16), b.astype(BF16), (((1,), (1,)), ((), ())),
                           preferred_element_type=F32)


def _dot_tn(a, b):
    return lax.dot_general(a.astype(BF16), b.astype(BF16), (((0,), (0,)), ((), ())),
                           preferred_element_type=F32)


def _split(x):
    hi = x.astype(BF16)
    return hi, (x - hi.astype(F32)).astype(BF16)


def _mask_dot(mask_bf16, x):
    hi, lo = _split(x)
    return (jnp.dot(mask_bf16, hi, preferred_element_type=F32)
            + jnp.dot(mask_bf16, lo, preferred_element_type=F32))


def _softplus(x):
    return jnp.maximum(x, 0.0) + jnp.log(1.0 + jnp.exp(-jnp.abs(x)))


def _tri_masks(rev):
    r = lax.broadcasted_iota(jnp.int32, (CHUNK, CHUNK), 0)
    c = lax.broadcasted_iota(jnp.int32, (CHUNK, CHUNK), 1)
    d = (c - r) if rev else (r - c)
    return d >= 0, d > 0


def _mod_kernel(c_ref, w_ref, b_ref, o_ref):
    c = c_ref[...]
    s = c * jax.nn.sigmoid(c)
    o_ref[...] = jnp.dot(s, w_ref[...], preferred_element_type=F32,
                         precision=lax.Precision.HIGHEST) + b_ref[...]


def adaln_mod(c_all, w_ada, b_ada):
    r, d = c_all.shape
    n = w_ada.shape[1]
    tn = 512
    return pl.pallas_call(
        _mod_kernel, grid=(n // tn,),
        in_specs=[pl.BlockSpec((r, d), lambda j: (0, 0)),
                  pl.BlockSpec((d, tn), lambda j: (0, j)),
                  pl.BlockSpec((1, tn), lambda j: (0, j))],
        out_specs=pl.BlockSpec((r, tn), lambda j: (0, j)),
        out_shape=jax.ShapeDtypeStruct((r, n), F32),
        compiler_params=_cparams("arbitrary"), name="adaln_mod",
    )(c_all, w_ada, b_ada.reshape(1, n))


def _inproj_kernel(x_ref, g_ref, mod_ref, *refs):
    n_out = len(refs) // 2
    x = x_ref[0]
    y = x * lax.rsqrt(jnp.mean(x * x, axis=-1, keepdims=True) + EPS) * g_ref[...]
    h = (y * (1.0 + mod_ref[0, 1:2, :]) + mod_ref[0, 0:1, :]).astype(BF16)
    for w_ref, o_ref in zip(refs[:n_out], refs[n_out:]):
        o_ref[0] = jnp.dot(h, w_ref[...], preferred_element_type=F32)


def in_projection(x, norm_g, mod, weights):
    b, l, d = x.shape
    w_specs = [pl.BlockSpec(w.shape, lambda bi, i: (0, 0)) for w in weights]
    o_specs = [pl.BlockSpec((1, TOK_TILE, w.shape[1]), lambda bi, i: (bi, i, 0)) for w in weights]
    return pl.pallas_call(
        _inproj_kernel, grid=(b, l // TOK_TILE),
        in_specs=[pl.BlockSpec((1, TOK_TILE, d), lambda bi, i: (bi, i, 0)),
                  pl.BlockSpec((1, d), lambda bi, i: (0, 0)),
                  pl.BlockSpec((1, 2, d), lambda bi, i: (bi, 0, 0))] + w_specs,
        out_specs=o_specs,
        out_shape=[jax.ShapeDtypeStruct((b, l, w.shape[1]), F32) for w in weights],
        compiler_params=_cparams("parallel", "arbitrary"), name="in_projection",
    )(x, norm_g.reshape(1, d), mod, *weights)


def _dn_feature_kernel(x_ref, w_ref, o_ref):
    x = x_ref[0]
    n = x.shape[0]
    t = lax.broadcasted_iota(jnp.int32, (n, 1), 0)
    pad = CONV_W // 2
    acc = w_ref[0, pad:pad + 1, :] * x
    for j in range(CONV_W):
        s = j - pad
        if s == 0:
            continue
        xs = pltpu.roll(x, (-s) % n, axis=0)
        bad = (t < -s) if s < 0 else (t >= n - s)
        acc = acc + w_ref[0, j:j + 1, :] * jnp.where(bad, 0.0, xs)
    y = acc * jax.nn.sigmoid(acc)
    kind = pl.program_id(1) // DN_HEADS
    inv = lax.rsqrt(jnp.sum(y * y, axis=-1, keepdims=True) + EPS)
    scale = jnp.where(kind == 0, inv * DN_DK ** -0.5, jnp.where(kind == 1, inv, 1.0))
    o_ref[0] = y * scale


def dn_features_tc(qkv, conv_w):
    b, l, n = qkv.shape
    nblk = n // LANES
    w = jnp.zeros((nblk, SUBLANES, LANES), F32).at[:, :CONV_W].set(
        conv_w.reshape(CONV_W, nblk, LANES).transpose(1, 0, 2))
    return pl.pallas_call(
        _dn_feature_kernel, grid=(b, nblk),
        in_specs=[pl.BlockSpec((1, l, LANES), lambda bi, j: (bi, 0, j)),
                  pl.BlockSpec((1, SUBLANES, LANES), lambda bi, j: (j, 0, 0))],
        out_specs=pl.BlockSpec((1, l, LANES), lambda bi, j: (bi, 0, j)),
        out_shape=jax.ShapeDtypeStruct((b, l, n), F32),
        compiler_params=_cparams("parallel", "arbitrary"), name="dn_features",
    )(qkv, w)


def _scan_chunks(rev, n_ctx, n_lat):
    if rev:
        ctx = lambda j: jnp.maximum(n_ctx - 1 - j, 0)
        lat = lambda j: jnp.where(j < n_ctx, n_lat - 1, n_lat - 1 - (j - n_ctx))
    else:
        ctx = lambda j: jnp.minimum(j, n_ctx - 1)
        lat = lambda j: jnp.maximum(j - n_ctx, 0)
    return ctx, lat


def _dn_scan_kernel(rev, dirn, n_ctx, alog_ref, dtb_ref, fc_ref, fl_ref, bac_ref, bal_ref, o_ref, s_ref):
    step = pl.program_id(1)

    @pl.when(step == 0)
    def _():
        s_ref[...] = jnp.zeros_like(s_ref)

    in_ctx = step < n_ctx
    f = jnp.where(in_ctx, fc_ref[0], fl_ref[0])
    ba = jnp.where(in_ctx, bac_ref[0], bal_ref[0])
    nh, hd, n = DN_HEADS, DN_HEADS * DN_DK, DN_HEADS * CHUNK
    stack = lambda base, w: jnp.concatenate([f[:, base + h * w:base + (h + 1) * w] for h in range(nh)], axis=0)
    q_s, k_s, v_s = stack(0, DN_DK), stack(hd, DN_DK), stack(2 * hd, DN_DV)
    incl, _ = _tri_masks(rev)
    beta_all = jax.nn.sigmoid(ba)
    g_all = -jnp.exp(alog_ref[...]) * _softplus(ba + dtb_ref[...])
    gc_all = _mask_dot(incl.astype(BF16), g_all)
    gc_t = jnp.concatenate([gc_all, gc_all], axis=0).T
    g_tot = jnp.sum(g_all, axis=0, keepdims=True)
    cb = [dirn * nh + h for h in range(nh)]
    cg = [2 * nh + c for c in cb]
    col = lambda a, cs: jnp.concatenate([a[:, c:c + 1] for c in cs], axis=0)
    beta_c, gc_c = col(beta_all, cb), col(gc_all, cg)
    gtot_c = jnp.concatenate([jnp.broadcast_to(g_tot[:, c:c + 1], (CHUNK, 1)) for c in cg], axis=0)
    gc_r = jnp.concatenate([gc_t[c:c + 1, :CHUNK] for c in cg], axis=1)
    r = lax.broadcasted_iota(jnp.int32, (n, n), 0)
    c = lax.broadcasted_iota(jnp.int32, (n, n), 1)
    same = (r // CHUNK) == (c // CHUNK)
    d = (c - r) if rev else (r - c)
    incl_bd, strict_bd = same & (d >= 0), same & (d > 0)
    eye = (r == c).astype(F32)
    decay = jnp.where(incl_bd, jnp.exp(jnp.where(incl_bd, gc_c - gc_r, 0.0)), 0.0)
    kb_s = k_s * beta_c
    lower = jnp.where(strict_bd, _dot_nt(kb_s, k_s) * decay, 0.0)
    eg_c = jnp.exp(gc_c)
    inv = eye - lower
    pw = lower
    for _ in range(5):
        pw = _dot(pw, pw)
        inv = inv + _dot(inv, pw)
    sol = _dot(inv, jnp.concatenate([v_s * beta_c, kb_s * eg_c], axis=-1))
    u_s, w_s = sol[:, :DN_DV], sol[:, DN_DV:]
    k_dec = k_s * jnp.exp(gtot_c - gc_c)
    rb = lax.broadcasted_iota(jnp.int32, (n, DN_DK), 0) // CHUNK
    expand = lambda x: jnp.concatenate([jnp.where(rb == h, x, 0.0) for h in range(nh)], axis=1)
    s = s_ref[...]
    v_new = u_s - _dot(expand(w_s), s)
    a_qk = _dot_nt(q_s, k_s) * decay
    o_s = _dot(expand(q_s * eg_c), s) + _dot(a_qk, v_new)
    gl_rows = jnp.concatenate([jnp.broadcast_to(jnp.exp(g_tot[:, cc:cc + 1]), (DN_DK, 1)) for cc in cg], axis=0)
    s_ref[...] = s * gl_rows + _dot_tn(expand(k_dec), v_new)
    o_ref[0] = jnp.concatenate([o_s[h * CHUNK:(h + 1) * CHUNK] for h in range(nh)], axis=1)


def dn_scan_tc(feat_c, feat_l, ba_c, ba_l, a_log, dt_bias, rev):
    b, l, nf = feat_l.shape
    n_ctx, n_lat = feat_c.shape[1] // CHUNK, l // CHUNK
    dirn = 1 if rev else 0
    cc, lc = _scan_chunks(rev, n_ctx, n_lat)
    lanes = lambda p: jnp.zeros((1, LANES), F32).at[0, 2 * DN_HEADS:4 * DN_HEADS].set(p.reshape(-1))
    vec = pl.BlockSpec((1, LANES), lambda bi, j: (0, 0))
    return pl.pallas_call(
        functools.partial(_dn_scan_kernel, rev, dirn, n_ctx), grid=(b, n_ctx + n_lat),
        in_specs=[vec, vec,
                  pl.BlockSpec((1, CHUNK, nf), lambda bi, j: (bi, cc(j), 0)),
                  pl.BlockSpec((1, CHUNK, nf), lambda bi, j: (bi, lc(j), 0)),
                  pl.BlockSpec((1, CHUNK, LANES), lambda bi, j: (bi, cc(j), 0)),
                  pl.BlockSpec((1, CHUNK, LANES), lambda bi, j: (bi, lc(j), 0))],
        out_specs=pl.BlockSpec((1, CHUNK, DN_HEADS * DN_DV), lambda bi, j: (bi, lc(j), 0)),
        out_shape=jax.ShapeDtypeStruct((b, l, DN_HEADS * DN_DV), F32),
        scratch_shapes=[pltpu.VMEM((DN_HEADS * DN_DK, DN_DV), F32)],
        compiler_params=_cparams("parallel", "arbitrary"), name="dn_scan_bwd" if rev else "dn_scan_fwd",
    )(lanes(a_log), lanes(dt_bias), feat_c, feat_l, ba_c, ba_l)


def _from_grid_cols(blk, n):
    cols = blk.shape[1] // n
    return jnp.concatenate([blk[:, i * n:(i + 1) * n] for i in range(cols)], axis=0)


def _gla_scan_kernel(rev, dirn, n_ctx, qkc_ref, vc_ref, lrc_ref, qkl_ref, vl_ref, lrl_ref,
                     wla_ref, bla_ref, o_ref, s_ref):
    step = pl.program_id(1)

    @pl.when(step == 0)
    def _():
        s_ref[...] = jnp.zeros_like(s_ref)

    in_ctx = step < n_ctx
    hk, hv = GLA_HEADS * GLA_DK, GLA_HEADS * GLA_DV
    qk = jnp.where(in_ctx, qkc_ref[0], _from_grid_cols(qkl_ref[0], 2 * hk))
    vv = jnp.where(in_ctx, vc_ref[0], _from_grid_cols(vl_ref[0], hv))
    lr = jnp.where(in_ctx, lrc_ref[0], _from_grid_cols(lrl_ref[0], LANES))
    incl, _ = _tri_masks(rev)
    incl_b = incl.astype(BF16)
    pre = _dot(lr, wla_ref[0]) + bla_ref[0]
    la_all = -_softplus(-pre) * (1.0 / GLA_TAU)
    bc_all = _mask_dot(incl_b, la_all)
    b_tot_all = jnp.sum(la_all, axis=0, keepdims=True)
    outs = []
    for h in range(GLA_HEADS):
        q = qk[:, h * GLA_DK:(h + 1) * GLA_DK] * GLA_DK ** -0.5
        k = qk[:, hk + h * GLA_DK:hk + (h + 1) * GLA_DK]
        v = vv[:, h * GLA_DV:(h + 1) * GLA_DV]
        bc = bc_all[:, h * GLA_DK:(h + 1) * GLA_DK]
        b_tot = b_tot_all[:, h * GLA_DK:(h + 1) * GLA_DK]
        st = s_ref[h]
        o = _dot_nt(q * jnp.exp(bc), st)
        parts = []
        for i in range(CHUNK // GLA_SUB):
            lo_r, hi_r = i * GLA_SUB, (i + 1) * GLA_SUB
            if rev:
                ref = bc[hi_r - 1:hi_r]
                c0, c1 = lo_r, CHUNK
            else:
                ref = bc[lo_r:lo_r + 1]
                c0, c1 = 0, hi_r
            qi = q[lo_r:hi_r] * jnp.exp(bc[lo_r:hi_r] - ref)
            ki = k[c0:c1] * jnp.exp(ref - bc[c0:c1])
            att = _dot_nt(qi, ki)
            rg = lax.broadcasted_iota(jnp.int32, (GLA_SUB, c1 - c0), 0) + lo_r
            cg = lax.broadcasted_iota(jnp.int32, (GLA_SUB, c1 - c0), 1) + c0
            keep = (cg > rg) if rev else (cg < rg)
            parts.append(_dot(jnp.where(keep, att, 0.0), v[c0:c1]))
        diag = jnp.sum(q * k, axis=-1, keepdims=True) * v
        outs.append(o + jnp.concatenate(parts, axis=0) + diag)
        k_dec = k * jnp.exp(b_tot - bc)
        s_ref[h] = st * jnp.exp(b_tot) + _dot_tn(v, k_dec)
    o = jnp.concatenate(outs, axis=-1)
    rows = o_ref.shape[1]
    o_ref[0] = jnp.concatenate([o[i * rows:(i + 1) * rows] for i in range(CHUNK // rows)], axis=-1)


def gla_scan_tc(qk_c, v_c, lr_c, qk_l, v_l, lr_l, w_la, b_la, rev):
    b, l, _ = qk_l.shape
    rows = l // GRID_W
    cols = CHUNK // rows
    n_ctx, n_lat = qk_c.shape[1] // CHUNK, l // CHUNK
    dirn = 1 if rev else 0
    cc, lc = _scan_chunks(rev, n_ctx, n_lat)
    hv = GLA_HEADS * GLA_DV
    ctx_blk = lambda a: pl.BlockSpec((1, CHUNK, a.shape[2]), lambda bi, j: (bi, cc(j), 0))
    lat_blk = lambda n: pl.BlockSpec((1, rows, cols * n), lambda bi, j: (bi, 0, lc(j)))
    grid_view = lambda a: a.reshape(b, rows, GRID_W * a.shape[2])
    out = pl.pallas_call(
        functools.partial(_gla_scan_kernel, rev, dirn, n_ctx), grid=(b, n_ctx + n_lat),
        in_specs=[ctx_blk(qk_c), ctx_blk(v_c), ctx_blk(lr_c),
                  lat_blk(qk_l.shape[2]), lat_blk(v_l.shape[2]), lat_blk(lr_l.shape[2]),
                  pl.BlockSpec((1,) + w_la.shape[1:], lambda bi, j: (dirn, 0, 0)),
                  pl.BlockSpec((1,) + b_la.shape[1:], lambda bi, j: (dirn, 0, 0))],
        out_specs=lat_blk(hv),
        out_shape=jax.ShapeDtypeStruct((b, rows, GRID_W * hv), F32),
        scratch_shapes=[pltpu.VMEM((GLA_HEADS, GLA_DV, GLA_DK), F32)],
        compiler_params=_cparams("parallel", "arbitrary"), name="gla_scan_bwd" if rev else "gla_scan_fwd",
    )(qk_c, v_c, lr_c, grid_view(qk_l), grid_view(v_l), grid_view(lr_l), w_la, b_la)
    return out.reshape(b, l, hv)


def _head_norm_gate(o, gate, g, n_heads, dv):
    parts = []
    for h in range(n_heads):
        oh = o[:, h * dv:(h + 1) * dv]
        gh = gate[:, h * dv:(h + 1) * dv]
        yh = oh * lax.rsqrt(jnp.mean(oh * oh, axis=-1, keepdims=True) + EPS) * g
        parts.append(yh * (gh * jax.nn.sigmoid(gh)))
    return parts


def _mix_out_kernel(x_ref, dnf_ref, dnb_ref, z_ref, glf_ref, glb_ref, r_ref, mod_ref, dng_ref,
                    glg_ref, n2g_ref, wout_ref, wq_ref, x1_ref, h2_ref, q_ref):
    parts = (_head_norm_gate(dnf_ref[0] + dnb_ref[0], z_ref[0], dng_ref[...], DN_HEADS, DN_DV)
             + _head_norm_gate(glf_ref[0] + glb_ref[0], r_ref[0], glg_ref[...], GLA_HEADS, GLA_DV))
    y = jnp.dot(jnp.concatenate(parts, axis=-1).astype(BF16), wout_ref[...], preferred_element_type=F32)
    x1 = x_ref[0] + mod_ref[0, 2:3, :] * y
    x1_ref[0] = x1
    n = x1 * lax.rsqrt(jnp.mean(x1 * x1, axis=-1, keepdims=True) + EPS) * n2g_ref[...]
    h2 = n * (1.0 + mod_ref[0, 4:5, :]) + mod_ref[0, 3:4, :]
    h2_ref[0] = h2
    q_ref[0] = jnp.dot(h2.astype(BF16), wq_ref[...], preferred_element_type=F32)


def mix_out_tc(x, dn_f, dn_b, z, gl_f, gl_b, r, mod_l, dn_g, gla_g, n2_g, w_out, w_q):
    b, l, d = x.shape
    tok = lambda n: pl.BlockSpec((1, TOK_TILE, n), lambda bi, i: (bi, i, 0))
    full = lambda a: pl.BlockSpec(a.shape, lambda bi, i: (0,) * a.ndim)
    dn_g, gla_g, n2_g = dn_g.reshape(1, -1), gla_g.reshape(1, -1), n2_g.reshape(1, -1)
    nq = w_q.shape[1]
    return pl.pallas_call(
        _mix_out_kernel, grid=(b, l // TOK_TILE),
        in_specs=[tok(d), tok(dn_f.shape[2]), tok(dn_b.shape[2]), tok(z.shape[2]),
                  tok(gl_f.shape[2]), tok(gl_b.shape[2]), tok(r.shape[2]),
                  pl.BlockSpec((1,) + mod_l.shape[1:], lambda bi, i: (bi, 0, 0)),
                  full(dn_g), full(gla_g), full(n2_g), full(w_out), full(w_q)],
        out_specs=[tok(d), tok(d), tok(nq)],
        out_shape=[jax.ShapeDtypeStruct((b, l, d), F32), jax.ShapeDtypeStruct((b, l, d), F32),
                   jax.ShapeDtypeStruct((b, l, nq), F32)],
        compiler_params=_cparams("parallel", "arbitrary"), name="mix_out",
    )(x, dn_f, dn_b, z, gl_f, gl_b, r, mod_l, dn_g, gla_g, n2_g, w_out, w_q)


def _top_rows(s, k, payload=None):
    n = s.shape[0]
    row = lax.broadcasted_iota(jnp.int32, s.shape, 0).astype(F32)
    vals, picked = [], []
    for _ in range(k):
        m = jnp.max(s, axis=0, keepdims=True)
        first = jnp.min(jnp.where(s == m, row, float(n)), axis=0, keepdims=True)
        sel = row == first
        vals.append(m)
        if payload is None:
            picked.append(first)
        else:
            picked.append(jnp.max(jnp.where(sel, payload, -1.0), axis=0, keepdims=True))
        s = jnp.where(sel, -jnp.inf, s)
    return jnp.concatenate(vals, axis=0), jnp.concatenate(picked, axis=0)


def _candidate_rows(s0, i0, s1, i1):
    k = s0.shape[0]
    wide = SUBLANES
    blocks_s = [s0[0:1] + s1]
    blocks_i = [i0[0:1] * float(PEER_NKEYS) + i1]
    col = lax.broadcasted_iota(jnp.int32, (wide, s0.shape[1]), 0)
    for i in range(1, wide):
        keep = col < (k // (i + 1))
        blocks_s.append(jnp.where(keep, s0[i:i + 1] + s1[0:wide], -jnp.inf))
        blocks_i.append(i0[i:i + 1] * float(PEER_NKEYS) + i1[0:wide])
    blocks_s.append(s0[wide:k] + s1[0:1])
    blocks_i.append(i0[wide:k] * float(PEER_NKEYS) + i1[0:1])
    return jnp.concatenate(blocks_s, axis=0), jnp.concatenate(blocks_i, axis=0)


def _select_kernel(q_ref, k_ref, idx_ref, gate_ref, idx_s, gate_s):
    half = PEER_DQ // 2

    def head(h, carry):
        tops = []
        for p in range(2):
            qp = q_ref[:, pl.ds(pl.multiple_of(h * PEER_DQ + p * half, half), half)]
            s = lax.dot_general(k_ref[h, p], qp, (((1,), (1,)), ((), ())),
                                preferred_element_type=F32,
                                precision=lax.Precision.HIGHEST)
            tops.append(_top_rows(s, PEER_TOPK))
        (s0, i0), (s1, i1) = tops
        cand_s, cand_i = _candidate_rows(s0, i0, s1, i1)
        best_s, idx = _top_rows(cand_s, PEER_TOPK, payload=cand_i)
        e = jnp.exp(best_s - best_s[0:1])
        r0 = pl.multiple_of(h * PEER_TOPK, PEER_TOPK)
        idx_s[pl.ds(r0, PEER_TOPK), :] = idx
        gate_s[pl.ds(r0, PEER_TOPK), :] = e / jnp.sum(e, axis=0, keepdims=True)
        return carry

    lax.fori_loop(0, PEER_HEADS, head, 0)
    idx_ref[...] = idx_s[...].T.astype(jnp.int32)
    gate_ref[...] = gate_s[...].T


def peer_select_tc(q, keys):
    n_tok = q.shape[0]
    out_spec = pl.BlockSpec((SELECT_TILE, PICKS), lambda i: (i, 0))
    return pl.pallas_call(
        _select_kernel,
        grid=(n_tok // SELECT_TILE,),
        in_specs=[pl.BlockSpec((SELECT_TILE, q.shape[1]), lambda i: (i, 0)),
                  pl.BlockSpec(keys.shape, lambda i: (0, 0, 0, 0))],
        out_specs=[out_spec, out_spec],
        out_shape=[jax.ShapeDtypeStruct((n_tok, PICKS), jnp.int32),
                   jax.ShapeDtypeStruct((n_tok, PICKS), F32)],
        scratch_shapes=[pltpu.VMEM((PICKS, SELECT_TILE), F32), pltpu.VMEM((PICKS, SELECT_TILE), F32)],
        compiler_params=_cparams("parallel"), name="peer_select",
    )(q, keys)


def _sc_mesh():
    return plsc.VectorSubcoreMesh(core_axis_name="c", subcore_axis_name="s")


def _sc_pipeline(body, n_steps, in_specs, out_specs, operands):
    pltpu.emit_pipeline(
        body, grid=(n_steps,), in_specs=in_specs, out_specs=out_specs,
        core_axis_name=("c", "s"), dimension_semantics=(pltpu.PARALLEL,),
        trace_scopes=False,
    )(*operands)


def pack_table(t):
    half = t.shape[1] // 2
    bits = lax.bitcast_convert_type(t, jnp.uint32)
    lo_bits, hi_bits = bits[:, :half], bits[:, half:]
    low = (lo_bits + jnp.uint32(0x7FFF) + ((lo_bits >> 16) & jnp.uint32(1))) >> 16
    mag = hi_bits & jnp.uint32(0x7FFFFFFF)
    top = jnp.where(mag >= jnp.uint32(0x10000),
                    (mag - low + jnp.uint32(0x8000)) & jnp.uint32(0xFFFF0000), jnp.uint32(0))
    word = (hi_bits & jnp.uint32(0x80000000)) | top | low
    return lax.bitcast_convert_type(word, jnp.int32)


def _unpack(w):
    return plsc.bitcast(w << 16, F32), plsc.bitcast(w, F32)


def pack_pairs(t):
    half = t.shape[1] // 2
    bits = lax.bitcast_convert_type(t, jnp.uint32)
    rne = lambda b: (b + jnp.uint32(0x7FFF) + ((b >> 16) & jnp.uint32(1))) >> 16
    word = (rne(bits[:, half:]) << 16) | rne(bits[:, :half])
    return lax.bitcast_convert_type(word, jnp.int32)


def peer_act_partial_sc(u_q, idx2, h_q):
    n_groups, (n_tok, half) = idx2.shape[0], h_q.shape
    nsub = TOK_STEP * GROUPS_PER_TOK
    n_chunks = half // SC_LANES
    as_pairs = lambda w: plsc.bitcast(w, BF16)

    @functools.partial(
        pl.kernel, mesh=_sc_mesh(), compiler_params=pltpu.CompilerParams(needs_layout_passes=False),
        out_type=jax.ShapeDtypeStruct((n_groups, PICK_GROUP * SC_LANES), F32),
        scratch_types=[pltpu.VMEM((2, PICK_GROUP, half), jnp.int32),
                       pltpu.SemaphoreType.DMA((2,))],
    )
    def k(u_hbm, i_hbm, h_hbm, o_hbm, rows, sems):
        def body(i_v, h_v, o_v):
            def fetch(j, slot):
                return pltpu.make_async_copy(u_hbm.at[i_v.at[j]], rows.at[slot], sems.at[slot])

            fetch(0, 0).start()

            def sub(j, carry):
                slot = j % 2

                @pl.when(j + 1 < nsub)
                def _():
                    fetch(j + 1, 1 - slot).start()

                fetch(j, slot).wait()
                t = j // GROUPS_PER_TOK

                def picks(g, carry2):
                    kb = g * ACT_UNROLL
                    accs = [None] * ACT_UNROLL
                    for c0 in range(0, n_chunks, ACT_BF16_TERMS):
                        cols = [pl.ds((c0 + cc) * SC_LANES, SC_LANES) for cc in range(ACT_BF16_TERMS)]
                        hs = [as_pairs(h_v[t, col]) for col in cols]
                        for i in range(ACT_UNROLL):
                            s = None
                            for col, hv in zip(cols, hs):
                                p = as_pairs(rows[slot, kb + i, col]) * hv
                                s = p if s is None else s + p
                            lo, hi = plsc.unpack(s, format=plsc.PackFormat.INTERLEAVED)
                            p32 = lo + hi
                            accs[i] = p32 if accs[i] is None else accs[i] + p32
                    for i in range(ACT_UNROLL):
                        o_v[j, pl.ds((kb + i) * SC_LANES, SC_LANES)] = accs[i]
                    return carry2

                lax.fori_loop(0, PICK_GROUP // ACT_UNROLL, picks, 0)
                return carry

            lax.fori_loop(0, nsub, sub, 0)

        _sc_pipeline(
            body, n_tok // TOK_STEP,
            [pl.BlockSpec((nsub, PICK_GROUP), lambda i: (i, 0)),
             pl.BlockSpec((TOK_STEP, half), lambda i: (i, 0))],
            [pl.BlockSpec((nsub, PICK_GROUP * SC_LANES), lambda i: (i, 0))],
            (i_hbm, h_hbm, o_hbm))

    return k(u_q, idx2, h_q)


def peer_combine_sc(v_p, idx2, coef_b):
    n_groups = idx2.shape[0]
    half = v_p.shape[1]
    nsub = COMB_GROUP_STEP
    n_chunks = half // SC_LANES

    @functools.partial(
        pl.kernel, mesh=_sc_mesh(), compiler_params=pltpu.CompilerParams(needs_layout_passes=False),
        out_type=jax.ShapeDtypeStruct((n_groups, 2 * half), F32),
        scratch_types=[pltpu.VMEM((2, PICK_GROUP, half), jnp.int32),
                       pltpu.SemaphoreType.DMA((2,))],
    )
    def k(v_hbm, i_hbm, c_hbm, o_hbm, rows, sems):
        def body(i_v, c_v, o_v):
            def fetch(j, slot):
                return pltpu.make_async_copy(v_hbm.at[i_v.at[j]], rows.at[slot], sems.at[slot])

            fetch(0, 0).start()

            def sub(j, carry0):
                slot = j % 2

                @pl.when(j + 1 < nsub)
                def _():
                    fetch(j + 1, 1 - slot).start()

                fetch(j, slot).wait()
                cks = [c_v[j, pl.ds(kk * SC_LANES, SC_LANES)] for kk in range(PICK_GROUP)]

                def chunk_pair(c2, carry):
                    sums = []
                    for cc in range(COMB_CHUNKS):
                        l = (c2 * COMB_CHUNKS + cc) * SC_LANES
                        tot_lo, tot_hi = None, None
                        for kk in range(PICK_GROUP):
                            lo, hi = _unpack(rows[slot, kk, pl.ds(l, SC_LANES)])
                            a, b_ = cks[kk] * lo, cks[kk] * hi
                            tot_lo = a if tot_lo is None else tot_lo + a
                            tot_hi = b_ if tot_hi is None else tot_hi + b_
                        sums.append((l, tot_lo, tot_hi))
                    for l, tot_lo, tot_hi in sums:
                        o_v[j, pl.ds(l, SC_LANES)] = tot_lo
                        o_v[j, pl.ds(half + l, SC_LANES)] = tot_hi
                    return carry

                lax.fori_loop(0, n_chunks // COMB_CHUNKS, chunk_pair, 0)
                return carry0

            lax.fori_loop(0, nsub, sub, 0)

        _sc_pipeline(
            body, n_groups // nsub,
            [pl.BlockSpec((nsub, PICK_GROUP), lambda i: (i, 0)),
             pl.BlockSpec((nsub, PICK_GROUP * SC_LANES), lambda i: (i, 0))],
            [pl.BlockSpec((nsub, 2 * half), lambda i: (i, 0))],
            (i_hbm, c_hbm, o_hbm))

    return k(v_p, idx2, coef_b)


def _segment_matrix():
    r = lax.broadcasted_iota(jnp.int32, (PICK_GROUP * SC_LANES, PICK_GROUP), 0) // SC_LANES
    c = lax.broadcasted_iota(jnp.int32, (PICK_GROUP * SC_LANES, PICK_GROUP), 1)
    return (r == c).astype(F32)


def _coef_kernel(part_ref, gate_ref, o_ref):
    seg = _segment_matrix()
    act = jnp.dot(part_ref[...], seg, preferred_element_type=F32, precision=lax.Precision.HIGHEST)
    coef = gate_ref[...] * (0.5 * act * (1.0 + lax.erf(act * (2.0 ** -0.5))))
    o_ref[...] = lax.dot_general(coef, seg, (((1,), (1,)), ((), ())), preferred_element_type=F32,
                                 precision=lax.Precision.HIGHEST)


def peer_coef_tc(part, gate2):
    n_groups, width = part.shape
    tile = 1024
    return pl.pallas_call(
        _coef_kernel,
        grid=(n_groups // tile,),
        in_specs=[pl.BlockSpec((tile, width), lambda i: (i, 0)),
                  pl.BlockSpec((tile, PICK_GROUP), lambda i: (i, 0))],
        out_specs=pl.BlockSpec((tile, width), lambda i: (i, 0)),
        out_shape=jax.ShapeDtypeStruct((n_groups, width), F32),
        compiler_params=_cparams("parallel"), name="peer_coef",
    )(part, gate2)


def _final_kernel(x_ref, y_ref, mod_ref, g_ref, o_ref):
    d = x_ref.shape[2]
    y = y_ref[0, :, 0:d]
    for p in range(1, y_ref.shape[2] // d):
        y = y + y_ref[0, :, p * d:(p + 1) * d]
    x = x_ref[0] + mod_ref[0, 5:6, :] * y
    o_ref[0] = x * lax.rsqrt(jnp.mean(x * x, axis=-1, keepdims=True) + EPS) * g_ref[...]


def final_tc(x1, y_parts, mod_l, final_g):
    b, l, d = x1.shape
    tok = pl.BlockSpec((1, TOK_TILE, d), lambda bi, i: (bi, i, 0))
    return pl.pallas_call(
        _final_kernel, grid=(b, l // TOK_TILE),
        in_specs=[tok, pl.BlockSpec((1, TOK_TILE, y_parts.shape[2]), lambda bi, i: (bi, i, 0)),
                  pl.BlockSpec((1,) + mod_l.shape[1:], lambda bi, i: (bi, 0, 0)),
                  pl.BlockSpec((1, d), lambda bi, i: (0, 0))],
        out_specs=tok, out_shape=jax.ShapeDtypeStruct((b, l, d), F32),
        compiler_params=_cparams("parallel", "arbitrary"), name="final_norm",
    )(x1, y_parts, mod_l, final_g.reshape(1, d))


def _pad_cols(w, n):
    return jnp.pad(w, ((0, 0), (0, n - w.shape[1])))


def _slice_sizes(b):
    if b % SLICE_BATCH or SLICE_BATCH % 2 or b < 2 * SLICE_BATCH:
        return [1] * b
    half = SLICE_BATCH // 2
    return [half] + [SLICE_BATCH] * (b // SLICE_BATCH - 1) + [half]


def _exact_zero(v):
    return jnp.minimum(jnp.abs(v), 0.0)


def forward(x, c, ctx, c_ctx, w_ada, b_ada, norm1_g, norm2_g, w_in, conv_w, dn_a_log,
            dn_dt_bias, dn_norm_g, gla_wa2, gla_ba, gla_norm_g, w_out, peer_wq, peer_keys,
            peer_u, peer_v, final_g):
    b, l, d = x.shape
    c_all = jnp.concatenate([c, c_ctx[None]], axis=0)
    c_all = jnp.pad(c_all, ((0, (-c_all.shape[0]) % SUBLANES), (0, 0)))
    mod = adaln_mod(c_all, w_ada, b_ada)
    mod_l = mod[:b].reshape(b, 6, d)
    mod_c = jnp.broadcast_to(mod[b].reshape(1, 6, d), (b, 6, d))
    o = DN_QKV
    hv = DN_HEADS * DN_DV
    w_dn_qkv, w_dn_z = w_in[:, :o], w_in[:, o:o + hv]
    w_dn_ba = _pad_cols(w_in[:, o + hv:DN_COLS], LANES)
    g0 = DN_COLS
    gqk, gv = 2 * GLA_HEADS * GLA_DK, GLA_HEADS * GLA_DV
    w_gl_qk, w_gl_v = w_in[:, g0:g0 + gqk], w_in[:, g0 + gqk:g0 + gqk + gv]
    w_gl_r = w_in[:, g0 + gqk + gv:g0 + gqk + 2 * gv]
    w_gl_lr = _pad_cols(w_in[:, g0 + gqk + 2 * gv:], LANES)
    w_lat = [w.astype(BF16) for w in (w_dn_qkv, w_dn_ba, w_gl_qk, w_gl_v, w_gl_lr, w_dn_z, w_gl_r)]
    w_ctx = w_lat[:5]
    w_la = jnp.zeros((2, LANES, GLA_HEADS * GLA_DK), F32)
    for dd in range(2):
        w_la = w_la.at[dd, dd * GLA_LR:(dd + 1) * GLA_LR].set(gla_wa2[dd])
    w_la = w_la.astype(BF16)
    b_la = gla_ba.reshape(2, 1, GLA_HEADS * GLA_DK)
    w_out_b, w_q_b = w_out.astype(BF16), peer_wq.astype(BF16)
    u_q, v_p = pack_pairs(peer_u), pack_table(peer_v)

    def mixer(xg, ctxg, mod_cg, mod_g, after_select, after_combine):
        mod_cg, mod_g = mod_cg + after_select, mod_g + after_select
        c_qkv, c_ba, c_qk, c_v, c_lr = in_projection(ctxg, norm1_g, mod_cg[:, 0:2], w_ctx)
        l_qkv, l_ba, l_qk, l_v, l_lr, l_z, l_r = in_projection(xg, norm1_g, mod_g[:, 0:2], w_lat)
        feat_c, feat_l = dn_features_tc(c_qkv, conv_w), dn_features_tc(l_qkv, conv_w)
        dn_f = dn_scan_tc(feat_c, feat_l, c_ba, l_ba, dn_a_log, dn_dt_bias, rev=False)
        dn_b = dn_scan_tc(feat_c, feat_l, c_ba, l_ba, dn_a_log, dn_dt_bias, rev=True)
        gl_f = gla_scan_tc(c_qk, c_v, c_lr, l_qk, l_v, l_lr, w_la, b_la, rev=False)
        gl_b = gla_scan_tc(c_qk, c_v, c_lr, l_qk, l_v, l_lr, w_la, b_la, rev=True)
        return mix_out_tc(xg, dn_f, dn_b, l_z, gl_f, gl_b, l_r, mod_g, dn_norm_g + after_combine,
                          gla_norm_g, norm2_g, w_out_b, w_q_b)

    def select_and_act(h2, q, after_coef):
        n_tok = h2.shape[0] * l
        idx, gate = peer_select_tc(q.reshape(n_tok, -1), peer_keys + after_coef)
        idx2 = idx.reshape(n_tok * GROUPS_PER_TOK, PICK_GROUP)
        gate2 = gate.reshape(n_tok * GROUPS_PER_TOK, PICK_GROUP)
        return idx2, gate2, peer_act_partial_sc(u_q, idx2, pack_pairs(h2.reshape(n_tok, d)))

    def coef_and_combine(st, after_mixer):
        coef_b = peer_coef_tc(st["part"], st["gate2"] + after_mixer)
        st["y"] = peer_combine_sc(v_p, st["idx2"], coef_b)
        return _exact_zero(coef_b[0, 0])

    zero = jnp.zeros((), F32)
    slices, z_sel, z_coef = [], zero, zero
    i = 0
    for bg in _slice_sizes(b):
        g = len(slices)
        z_comb = _exact_zero(slices[g - 2]["y"][0, 0]) if g >= 2 else zero
        mod_g = mod_l[i:i + bg]
        x1, h2, q = mixer(x[i:i + bg], ctx[i:i + bg], mod_c[i:i + bg], mod_g, z_sel, z_comb)
        i += bg
        if g >= 1:
            z_coef = coef_and_combine(slices[g - 1], _exact_zero(x1[0, 0, 0]))
        idx2, gate2, part = select_and_act(h2, q, z_coef)
        z_sel = _exact_zero(gate2[0, 0])
        slices.append(dict(x1=x1, mod=mod_g, idx2=idx2, gate2=gate2, part=part))
    coef_and_combine(slices[-1], zero)
    outs = [final_tc(st["x1"], st["y"].reshape(st["x1"].shape[:2] + (-1,)), st["mod"], final_g) for st in slices]
    return jnp.concatenate(outs, axis=0)


def kernel(x, c, ctx, c_ctx, w_ada, b_ada, norm1_g, norm2_g, w_in, conv_w, dn_a_log,
           dn_dt_bias, dn_norm_g, gla_wa2, gla_ba, gla_norm_g, w_out, peer_wq, peer_keys,
           peer_u, peer_v, final_g):
    assert w_ada.shape[0] == 1, "single-layer block: the context stream is only consumed, never updated"
    return forward(x, c, ctx, c_ctx, w_ada[0], b_ada[0], norm1_g[0], norm2_g[0], w_in[0], conv_w[0],
                   dn_a_log[0], dn_dt_bias[0], dn_norm_g[0], gla_wa2[0], gla_ba[0], gla_norm_g[0],
                   w_out[0], peer_wq[0], peer_keys[0], peer_u[0], peer_v[0], final_g)
```

```python
import functools

import jax
import jax.numpy as jnp
from jax import lax
from jax.experimental import pallas as pl
from jax.experimental.pallas import tpu as pltpu
from jax.experimental.pallas import tpu_sc as plsc

GRID_W = 64
DN_HEADS = 4
DN_DK = 128
DN_DV = 128
CONV_W = 5
GLA_HEADS = 4
GLA_DK = 64
GLA_DV = 128
GLA_LR = 16
GLA_TAU = 16.0
CHUNK = 64
PEER_HEADS = 8
PEER_NKEYS = 128
PEER_DQ = 256
PEER_TOPK = 16
EPS = 1e-6
DN_QKV = 2 * DN_HEADS * DN_DK + DN_HEADS * DN_DV
DN_COLS = DN_QKV + DN_HEADS * DN_DV + 4 * DN_HEADS

SUBLANES = 8
LANES = 128
SC_LANES = 16
VMEM_LIMIT_BYTES = 48 * 1024 * 1024

TOK_TILE = 256
SELECT_TILE = 256
SCAN_BATCH = 2
GLA_SUB = 16
PICK_GROUP = 32
COMB_GROUP_STEP = 16
TOK_STEP = 8
COMB_BF16_TERMS = 4
COMB_CHUNKS = 4
ACT_BF16_TERMS = 4
ACT_UNROLL = 16
SLICE_BATCH = 2
PICKS = PEER_HEADS * PEER_TOPK
GROUPS_PER_TOK = PICKS // PICK_GROUP

F32 = jnp.float32
BF16 = jnp.bfloat16


def _cparams(*semantics):
    return pltpu.CompilerParams(dimension_semantics=semantics, vmem_limit_bytes=VMEM_LIMIT_BYTES)


def _dot(a, b):
    return jnp.dot(a.astype(BF16), b.astype(BF16), preferred_element_type=F32)


def _dot_nt(a, b):
    return lax.dot_general(a.astype(BF16), b.astype(BF16), (((1,), (1,)), ((), ())),
                           preferred_element_type=F32)


def _dot_tn(a, b):
    return lax.dot_general(a.astype(BF16), b.astype(BF16), (((0,), (0,)), ((), ())),
                           preferred_element_type=F32)


def _split(x):
    hi = x.astype(BF16)
    return hi, (x - hi.astype(F32)).astype(BF16)


def _mask_dot(mask_bf16, x):
    hi, lo = _split(x)
    return (jnp.dot(mask_bf16, hi, preferred_element_type=F32)
            + jnp.dot(mask_bf16, lo, preferred_element_type=F32))


def _softplus(x):
    return jnp.maximum(x, 0.0) + jnp.log(1.0 + jnp.exp(-jnp.abs(x)))


def _tri_masks(rev):
    r = lax.broadcasted_iota(jnp.int32, (CHUNK, CHUNK), 0)
    c = lax.broadcasted_iota(jnp.int32, (CHUNK, CHUNK), 1)
    d = (c - r) if rev else (r - c)
    return d >= 0, d > 0


def _mod_kernel(c_ref, w_ref, b_ref, o_ref):
    c = c_ref[...]
    s = c * jax.nn.sigmoid(c)
    o_ref[...] = jnp.dot(s, w_ref[...], preferred_element_type=F32,
                         precision=lax.Precision.HIGHEST) + b_ref[...]


def adaln_mod(c_all, w_ada, b_ada):
    r, d = c_all.shape
    n = w_ada.shape[1]
    tn = 512
    return pl.pallas_call(
        _mod_kernel, grid=(n // tn,),
        in_specs=[pl.BlockSpec((r, d), lambda j: (0, 0)),
                  pl.BlockSpec((d, tn), lambda j: (0, j)),
                  pl.BlockSpec((1, tn), lambda j: (0, j))],
        out_specs=pl.BlockSpec((r, tn), lambda j: (0, j)),
        out_shape=jax.ShapeDtypeStruct((r, n), F32),
        compiler_params=_cparams("arbitrary"), name="adaln_mod",
    )(c_all, w_ada, b_ada.reshape(1, n))


def _inproj_kernel(x_ref, g_ref, mod_ref, *refs):
    n_out = len(refs) // 2
    x = x_ref[0]
    y = x * lax.rsqrt(jnp.mean(x * x, axis=-1, keepdims=True) + EPS) * g_ref[...]
    h = (y * (1.0 + mod_ref[0, 1:2, :]) + mod_ref[0, 0:1, :]).astype(BF16)
    for w_ref, o_ref in zip(refs[:n_out], refs[n_out:]):
        o_ref[0] = jnp.dot(h, w_ref[...], preferred_element_type=F32)


def in_projection(x, norm_g, mod, weights):
    b, l, d = x.shape
    w_specs = [pl.BlockSpec(w.shape, lambda bi, i: (0, 0)) for w in weights]
    o_specs = [pl.BlockSpec((1, TOK_TILE, w.shape[1]), lambda bi, i: (bi, i, 0)) for w in weights]
    return pl.pallas_call(
        _inproj_kernel, grid=(b, l // TOK_TILE),
        in_specs=[pl.BlockSpec((1, TOK_TILE, d), lambda bi, i: (bi, i, 0)),
                  pl.BlockSpec((1, d), lambda bi, i: (0, 0)),
                  pl.BlockSpec((1, 2, d), lambda bi, i: (bi, 0, 0))] + w_specs,
        out_specs=o_specs,
        out_shape=[jax.ShapeDtypeStruct((b, l, w.shape[1]), F32) for w in weights],
        compiler_params=_cparams("parallel", "arbitrary"), name="in_projection",
    )(x, norm_g.reshape(1, d), mod, *weights)


def _dn_feature_kernel(x_ref, w_ref, o_ref):
    x = x_ref[0]
    n = x.shape[0]
    t = lax.broadcasted_iota(jnp.int32, (n, 1), 0)
    pad = CONV_W // 2
    acc = w_ref[0, pad:pad + 1, :] * x
    for j in range(CONV_W):
        s = j - pad
        if s == 0:
            continue
        xs = pltpu.roll(x, (-s) % n, axis=0)
        bad = (t < -s) if s < 0 else (t >= n - s)
        acc = acc + w_ref[0, j:j + 1, :] * jnp.where(bad, 0.0, xs)
    y = acc * jax.nn.sigmoid(acc)
    kind = pl.program_id(1) // DN_HEADS
    inv = lax.rsqrt(jnp.sum(y * y, axis=-1, keepdims=True) + EPS)
    scale = jnp.where(kind == 0, inv * DN_DK ** -0.5, jnp.where(kind == 1, inv, 1.0))
    o_ref[0] = y * scale


def dn_features_tc(qkv, conv_w):
    b, l, n = qkv.shape
    nblk = n // LANES
    w = jnp.zeros((nblk, SUBLANES, LANES), F32).at[:, :CONV_W].set(
        conv_w.reshape(CONV_W, nblk, LANES).transpose(1, 0, 2))
    return pl.pallas_call(
        _dn_feature_kernel, grid=(b, nblk),
        in_specs=[pl.BlockSpec((1, l, LANES), lambda bi, j: (bi, 0, j)),
                  pl.BlockSpec((1, SUBLANES, LANES), lambda bi, j: (j, 0, 0))],
        out_specs=pl.BlockSpec((1, l, LANES), lambda bi, j: (bi, 0, j)),
        out_shape=jax.ShapeDtypeStruct((b, l, n), F32),
        compiler_params=_cparams("parallel", "arbitrary"), name="dn_features",
    )(qkv, w)


def _scan_chunks(rev, n_ctx, n_lat):
    if rev:
        ctx = lambda j: jnp.maximum(n_ctx - 1 - j, 0)
        lat = lambda j: jnp.where(j < n_ctx, n_lat - 1, n_lat - 1 - (j - n_ctx))
    else:
        ctx = lambda j: jnp.minimum(j, n_ctx - 1)
        lat = lambda j: jnp.maximum(j - n_ctx, 0)
    return ctx, lat


def _dn_scan_kernel(rev, dirn, n_ctx, alog_ref, dtb_ref, fc_ref, fl_ref, bac_ref, bal_ref, o_ref, s_ref):
    step = pl.program_id(1)

    @pl.when(step == 0)
    def _():
        s_ref[...] = jnp.zeros_like(s_ref)

    in_ctx = step < n_ctx
    f = jnp.where(in_ctx, fc_ref[0], fl_ref[0])
    ba = jnp.where(in_ctx, bac_ref[0], bal_ref[0])
    nh, hd, n = DN_HEADS, DN_HEADS * DN_DK, DN_HEADS * CHUNK
    stack = lambda base, w: jnp.concatenate([f[:, base + h * w:base + (h + 1) * w] for h in range(nh)], axis=0)
    q_s, k_s, v_s = stack(0, DN_DK), stack(hd, DN_DK), stack(2 * hd, DN_DV)
    incl, _ = _tri_masks(rev)
    beta_all = jax.nn.sigmoid(ba)
    g_all = -jnp.exp(alog_ref[...]) * _softplus(ba + dtb_ref[...])
    gc_all = _mask_dot(incl.astype(BF16), g_all)
    gc_t = jnp.concatenate([gc_all, gc_all], axis=0).T
    g_tot = jnp.sum(g_all, axis=0, keepdims=True)
    cb = [dirn * nh + h for h in range(nh)]
    cg = [2 * nh + c for c in cb]
    col = lambda a, cs: jnp.concatenate([a[:, c:c + 1] for c in cs], axis=0)
    beta_c, gc_c = col(beta_all, cb), col(gc_all, cg)
    gtot_c = jnp.concatenate([jnp.broadcast_to(g_tot[:, c:c + 1], (CHUNK, 1)) for c in cg], axis=0)
    gc_r = jnp.concatenate([gc_t[c:c + 1, :CHUNK] for c in cg], axis=1)
    r = lax.broadcasted_iota(jnp.int32, (n, n), 0)
    c = lax.broadcasted_iota(jnp.int32, (n, n), 1)
    same = (r // CHUNK) == (c // CHUNK)
    d = (c - r) if rev else (r - c)
    incl_bd, strict_bd = same & (d >= 0), same & (d > 0)
    eye = (r == c).astype(F32)
    decay = jnp.where(incl_bd, jnp.exp(jnp.where(incl_bd, gc_c - gc_r, 0.0)), 0.0)
    kb_s = k_s * beta_c
    lower = jnp.where(strict_bd, _dot_nt(kb_s, k_s) * decay, 0.0)
    eg_c = jnp.exp(gc_c)
    inv = eye - lower
    pw = lower
    for _ in range(5):
        pw = _dot(pw, pw)
        inv = inv + _dot(inv, pw)
    sol = _dot(inv, jnp.concatenate([v_s * beta_c, kb_s * eg_c], axis=-1))
    u_s, w_s = sol[:, :DN_DV], sol[:, DN_DV:]
    k_dec = k_s * jnp.exp(gtot_c - gc_c)
    rb = lax.broadcasted_iota(jnp.int32, (n, DN_DK), 0) // CHUNK
    expand = lambda x: jnp.concatenate([jnp.where(rb == h, x, 0.0) for h in range(nh)], axis=1)
    s = s_ref[...]
    v_new = u_s - _dot(expand(w_s), s)
    a_qk = _dot_nt(q_s, k_s) * decay
    o_s = _dot(expand(q_s * eg_c), s) + _dot(a_qk, v_new)
    gl_rows = jnp.concatenate([jnp.broadcast_to(jnp.exp(g_tot[:, cc:cc + 1]), (DN_DK, 1)) for cc in cg], axis=0)
    s_ref[...] = s * gl_rows + _dot_tn(expand(k_dec), v_new)
    o_ref[0] = jnp.concatenate([o_s[h * CHUNK:(h + 1) * CHUNK] for h in range(nh)], axis=1)


def dn_scan_tc(feat_c, feat_l, ba_c, ba_l, a_log, dt_bias, rev):
    b, l, nf = feat_l.shape
    n_ctx, n_lat = feat_c.shape[1] // CHUNK, l // CHUNK
    dirn = 1 if rev else 0
    cc, lc = _scan_chunks(rev, n_ctx, n_lat)
    lanes = lambda p: jnp.zeros((1, LANES), F32).at[0, 2 * DN_HEADS:4 * DN_HEADS].set(p.reshape(-1))
    vec = pl.BlockSpec((1, LANES), lambda bi, j: (0, 0))
    return pl.pallas_call(
        functools.partial(_dn_scan_kernel, rev, dirn, n_ctx), grid=(b, n_ctx + n_lat),
        in_specs=[vec, vec,
                  pl.BlockSpec((1, CHUNK, nf), lambda bi, j: (bi, cc(j), 0)),
                  pl.BlockSpec((1, CHUNK, nf), lambda bi, j: (bi, lc(j), 0)),
                  pl.BlockSpec((1, CHUNK, LANES), lambda bi, j: (bi, cc(j), 0)),
                  pl.BlockSpec((1, CHUNK, LANES), lambda bi, j: (bi, lc(j), 0))],
        out_specs=pl.BlockSpec((1, CHUNK, DN_HEADS * DN_DV), lambda bi, j: (bi, lc(j), 0)),
        out_shape=jax.ShapeDtypeStruct((b, l, DN_HEADS * DN_DV), F32),
        scratch_shapes=[pltpu.VMEM((DN_HEADS * DN_DK, DN_DV), F32)],
        compiler_params=_cparams("parallel", "arbitrary"), name="dn_scan_bwd" if rev else "dn_scan_fwd",
    )(lanes(a_log), lanes(dt_bias), feat_c, feat_l, ba_c, ba_l)


def _from_grid_cols(blk, n):
    cols = blk.shape[1] // n
    return jnp.concatenate([blk[:, i * n:(i + 1) * n] for i in range(cols)], axis=0)


def _gla_scan_kernel(rev, dirn, n_ctx, qkc_ref, vc_ref, lrc_ref, qkl_ref, vl_ref, lrl_ref,
                     wla_ref, bla_ref, o_ref, s_ref):
    step = pl.program_id(1)

    @pl.when(step == 0)
    def _():
        s_ref[...] = jnp.zeros_like(s_ref)

    in_ctx = step < n_ctx
    hk, hv = GLA_HEADS * GLA_DK, GLA_HEADS * GLA_DV
    qk = jnp.where(in_ctx, qkc_ref[0], _from_grid_cols(qkl_ref[0], 2 * hk))
    vv = jnp.where(in_ctx, vc_ref[0], _from_grid_cols(vl_ref[0], hv))
    lr = jnp.where(in_ctx, lrc_ref[0], _from_grid_cols(lrl_ref[0], LANES))
    incl, _ = _tri_masks(rev)
    incl_b = incl.astype(BF16)
    pre = _dot(lr, wla_ref[0]) + bla_ref[0]
    la_all = -_softplus(-pre) * (1.0 / GLA_TAU)
    bc_all = _mask_dot(incl_b, la_all)
    b_tot_all = jnp.sum(la_all, axis=0, keepdims=True)
    outs = []
    for h in range(GLA_HEADS):
        q = qk[:, h * GLA_DK:(h + 1) * GLA_DK] * GLA_DK ** -0.5
        k = qk[:, hk + h * GLA_DK:hk + (h + 1) * GLA_DK]
        v = vv[:, h * GLA_DV:(h + 1) * GLA_DV]
        bc = bc_all[:, h * GLA_DK:(h + 1) * GLA_DK]
        b_tot = b_tot_all[:, h * GLA_DK:(h + 1) * GLA_DK]
        st = s_ref[h]
        o = _dot_nt(q * jnp.exp(bc), st)
        parts = []
        for i in range(CHUNK // GLA_SUB):
            lo_r, hi_r = i * GLA_SUB, (i + 1) * GLA_SUB
            if rev:
                ref = bc[hi_r - 1:hi_r]
                c0, c1 = lo_r, CHUNK
            else:
                ref = bc[lo_r:lo_r + 1]
                c0, c1 = 0, hi_r
            qi = q[lo_r:hi_r] * jnp.exp(bc[lo_r:hi_r] - ref)
            ki = k[c0:c1] * jnp.exp(ref - bc[c0:c1])
            att = _dot_nt(qi, ki)
            rg = lax.broadcasted_iota(jnp.int32, (GLA_SUB, c1 - c0), 0) + lo_r
            cg = lax.broadcasted_iota(jnp.int32, (GLA_SUB, c1 - c0), 1) + c0
            keep = (cg > rg) if rev else (cg < rg)
            parts.append(_dot(jnp.where(keep, att, 0.0), v[c0:c1]))
        diag = jnp.sum(q * k, axis=-1, keepdims=True) * v
        outs.append(o + jnp.concatenate(parts, axis=0) + diag)
        k_dec = k * jnp.exp(b_tot - bc)
        s_ref[h] = st * jnp.exp(b_tot) + _dot_tn(v, k_dec)
    o = jnp.concatenate(outs, axis=-1)
    rows = o_ref.shape[1]
    o_ref[0] = jnp.concatenate([o[i * rows:(i + 1) * rows] for i in range(CHUNK // rows)], axis=-1)


def gla_scan_tc(qk_c, v_c, lr_c, qk_l, v_l, lr_l, w_la, b_la, rev):
    b, l, _ = qk_l.shape
    rows = l // GRID_W
    cols = CHUNK // rows
    n_ctx, n_lat = qk_c.shape[1] // CHUNK, l // CHUNK
    dirn = 1 if rev else 0
    cc, lc = _scan_chunks(rev, n_ctx, n_lat)
    hv = GLA_HEADS * GLA_DV
    ctx_blk = lambda a: pl.BlockSpec((1, CHUNK, a.shape[2]), lambda bi, j: (bi, cc(j), 0))
    lat_blk = lambda n: pl.BlockSpec((1, rows, cols * n), lambda bi, j: (bi, 0, lc(j)))
    grid_view = lambda a: a.reshape(b, rows, GRID_W * a.shape[2])
    out = pl.pallas_call(
        functools.partial(_gla_scan_kernel, rev, dirn, n_ctx), grid=(b, n_ctx + n_lat),
        in_specs=[ctx_blk(qk_c), ctx_blk(v_c), ctx_blk(lr_c),
                  lat_blk(qk_l.shape[2]), lat_blk(v_l.shape[2]), lat_blk(lr_l.shape[2]),
                  pl.BlockSpec((1,) + w_la.shape[1:], lambda bi, j: (dirn, 0, 0)),
                  pl.BlockSpec((1,) + b_la.shape[1:], lambda bi, j: (dirn, 0, 0))],
        out_specs=lat_blk(hv),
        out_shape=jax.ShapeDtypeStruct((b, rows, GRID_W * hv), F32),
        scratch_shapes=[pltpu.VMEM((GLA_HEADS, GLA_DV, GLA_DK), F32)],
        compiler_params=_cparams("parallel", "arbitrary"), name="gla_scan_bwd" if rev else "gla_scan_fwd",
    )(qk_c, v_c, lr_c, grid_view(qk_l), grid_view(v_l), grid_view(lr_l), w_la, b_la)
    return out.reshape(b, l, hv)


def _head_norm_gate(o, gate, g, n_heads, dv):
    parts = []
    for h in range(n_heads):
        oh = o[:, h * dv:(h + 1) * dv]
        gh = gate[:, h * dv:(h + 1) * dv]
        yh = oh * lax.rsqrt(jnp.mean(oh * oh, axis=-1, keepdims=True) + EPS) * g
        parts.append(yh * (gh * jax.nn.sigmoid(gh)))
    return parts


def _mix_out_kernel(x_ref, dnf_ref, dnb_ref, z_ref, glf_ref, glb_ref, r_ref, mod_ref, dng_ref,
                    glg_ref, n2g_ref, wout_ref, wq_ref, x1_ref, h2_ref, q_ref):
    parts = (_head_norm_gate(dnf_ref[0] + dnb_ref[0], z_ref[0], dng_ref[...], DN_HEADS, DN_DV)
             + _head_norm_gate(glf_ref[0] + glb_ref[0], r_ref[0], glg_ref[...], GLA_HEADS, GLA_DV))
    y = jnp.dot(jnp.concatenate(parts, axis=-1).astype(BF16), wout_ref[...], preferred_element_type=F32)
    x1 = x_ref[0] + mod_ref[0, 2:3, :] * y
    x1_ref[0] = x1
    n = x1 * lax.rsqrt(jnp.mean(x1 * x1, axis=-1, keepdims=True) + EPS) * n2g_ref[...]
    h2 = n * (1.0 + mod_ref[0, 4:5, :]) + mod_ref[0, 3:4, :]
    h2_ref[0] = h2
    q_ref[0] = jnp.dot(h2.astype(BF16), wq_ref[...], preferred_element_type=F32)


def mix_out_tc(x, dn_f, dn_b, z, gl_f, gl_b, r, mod_l, dn_g, gla_g, n2_g, w_out, w_q):
    b, l, d = x.shape
    tok = lambda n: pl.BlockSpec((1, TOK_TILE, n), lambda bi, i: (bi, i, 0))
    full = lambda a: pl.BlockSpec(a.shape, lambda bi, i: (0,) * a.ndim)
    dn_g, gla_g, n2_g = dn_g.reshape(1, -1), gla_g.reshape(1, -1), n2_g.reshape(1, -1)
    nq = w_q.shape[1]
    return pl.pallas_call(
        _mix_out_kernel, grid=(b, l // TOK_TILE),
        in_specs=[tok(d), tok(dn_f.shape[2]), tok(dn_b.shape[2]), tok(z.shape[2]),
                  tok(gl_f.shape[2]), tok(gl_b.shape[2]), tok(r.shape[2]),
                  pl.BlockSpec((1,) + mod_l.shape[1:], lambda bi, i: (bi, 0, 0)),
                  full(dn_g), full(gla_g), full(n2_g), full(w_out), full(w_q)],
        out_specs=[tok(d), tok(d), tok(nq)],
        out_shape=[jax.ShapeDtypeStruct((b, l, d), F32), jax.ShapeDtypeStruct((b, l, d), F32),
                   jax.ShapeDtypeStruct((b, l, nq), F32)],
        compiler_params=_cparams("parallel", "arbitrary"), name="mix_out",
    )(x, dn_f, dn_b, z, gl_f, gl_b, r, mod_l, dn_g, gla_g, n2_g, w_out, w_q)


def _top_rows(s, k, payload=None):
    n = s.shape[0]
    row = lax.broadcasted_iota(jnp.int32, s.shape, 0).astype(F32)
    vals, picked = [], []
    for _ in range(k):
        m = jnp.max(s, axis=0, keepdims=True)
        first = jnp.min(jnp.where(s == m, row, float(n)), axis=0, keepdims=True)
        sel = row == first
        vals.append(m)
        if payload is None:
            picked.append(first)
        else:
            picked.append(jnp.max(jnp.where(sel, payload, -1.0), axis=0, keepdims=True))
        s = jnp.where(sel, -jnp.inf, s)
    return jnp.concatenate(vals, axis=0), jnp.concatenate(picked, axis=0)


def _candidate_rows(s0, i0, s1, i1):
    k = s0.shape[0]
    wide = SUBLANES
    blocks_s = [s0[0:1] + s1]
    blocks_i = [i0[0:1] * float(PEER_NKEYS) + i1]
    col = lax.broadcasted_iota(jnp.int32, (wide, s0.shape[1]), 0)
    for i in range(1, wide):
        keep = col < (k // (i + 1))
        blocks_s.append(jnp.where(keep, s0[i:i + 1] + s1[0:wide], -jnp.inf))
        blocks_i.append(i0[i:i + 1] * float(PEER_NKEYS) + i1[0:wide])
    blocks_s.append(s0[wide:k] + s1[0:1])
    blocks_i.append(i0[wide:k] * float(PEER_NKEYS) + i1[0:1])
    return jnp.concatenate(blocks_s, axis=0), jnp.concatenate(blocks_i, axis=0)


def _select_kernel(q_ref, k_ref, idx_ref, gate_ref, idx_s, gate_s):
    half = PEER_DQ // 2

    def head(h, carry):
        tops = []
        for p in range(2):
            qp = q_ref[:, pl.ds(pl.multiple_of(h * PEER_DQ + p * half, half), half)]
            s = lax.dot_general(k_ref[h, p], qp, (((1,), (1,)), ((), ())),
                                preferred_element_type=F32,
                                precision=lax.Precision.HIGHEST)
            tops.append(_top_rows(s, PEER_TOPK))
        (s0, i0), (s1, i1) = tops
        cand_s, cand_i = _candidate_rows(s0, i0, s1, i1)
        best_s, idx = _top_rows(cand_s, PEER_TOPK, payload=cand_i)
        e = jnp.exp(best_s - best_s[0:1])
        r0 = pl.multiple_of(h * PEER_TOPK, PEER_TOPK)
        idx_s[pl.ds(r0, PEER_TOPK), :] = idx
        gate_s[pl.ds(r0, PEER_TOPK), :] = e / jnp.sum(e, axis=0, keepdims=True)
        return carry

    lax.fori_loop(0, PEER_HEADS, head, 0)
    idx_ref[...] = idx_s[...].T.astype(jnp.int32)
    gate_ref[...] = gate_s[...].T


def peer_select_tc(q, keys):
    n_tok = q.shape[0]
    out_spec = pl.BlockSpec((SELECT_TILE, PICKS), lambda i: (i, 0))
    return pl.pallas_call(
        _select_kernel,
        grid=(n_tok // SELECT_TILE,),
        in_specs=[pl.BlockSpec((SELECT_TILE, q.shape[1]), lambda i: (i, 0)),
                  pl.BlockSpec(keys.shape, lambda i: (0, 0, 0, 0))],
        out_specs=[out_spec, out_spec],
        out_shape=[jax.ShapeDtypeStruct((n_tok, PICKS), jnp.int32),
                   jax.ShapeDtypeStruct((n_tok, PICKS), F32)],
        scratch_shapes=[pltpu.VMEM((PICKS, SELECT_TILE), F32), pltpu.VMEM((PICKS, SELECT_TILE), F32)],
        compiler_params=_cparams("parallel"), name="peer_select",
    )(q, keys)


def _sc_mesh():
    return plsc.VectorSubcoreMesh(core_axis_name="c", subcore_axis_name="s")


def _sc_pipeline(body, n_steps, in_specs, out_specs, operands):
    pltpu.emit_pipeline(
        body, grid=(n_steps,), in_specs=in_specs, out_specs=out_specs,
        core_axis_name=("c", "s"), dimension_semantics=(pltpu.PARALLEL,),
        trace_scopes=False,
    )(*operands)


def pack_table(t):
    half = t.shape[1] // 2
    bits = lax.bitcast_convert_type(t, jnp.uint32)
    lo_bits, hi_bits = bits[:, :half], bits[:, half:]
    low = (lo_bits + jnp.uint32(0x7FFF) + ((lo_bits >> 16) & jnp.uint32(1))) >> 16
    mag = hi_bits & jnp.uint32(0x7FFFFFFF)
    top = jnp.where(mag >= jnp.uint32(0x10000),
                    (mag - low + jnp.uint32(0x8000)) & jnp.uint32(0xFFFF0000), jnp.uint32(0))
    word = (hi_bits & jnp.uint32(0x80000000)) | top | low
    return lax.bitcast_convert_type(word, jnp.int32)


def _unpack(w):
    return plsc.bitcast(w << 16, F32), plsc.bitcast(w, F32)


def pack_pairs(t):
    half = t.shape[1] // 2
    bits = lax.bitcast_convert_type(t, jnp.uint32)
    rne = lambda b: (b + jnp.uint32(0x7FFF) + ((b >> 16) & jnp.uint32(1))) >> 16
    word = (rne(bits[:, half:]) << 16) | rne(bits[:, :half])
    return lax.bitcast_convert_type(word, jnp.int32)


def peer_act_partial_sc(u_q, idx2, h_q):
    n_groups, (n_tok, half) = idx2.shape[0], h_q.shape
    nsub = TOK_STEP * GROUPS_PER_TOK
    n_chunks = half // SC_LANES
    as_pairs = lambda w: plsc.bitcast(w, BF16)

    @functools.partial(
        pl.kernel, mesh=_sc_mesh(), compiler_params=pltpu.CompilerParams(needs_layout_passes=False),
        out_type=jax.ShapeDtypeStruct((n_groups, PICK_GROUP * SC_LANES), F32),
        scratch_types=[pltpu.VMEM((2, PICK_GROUP, half), jnp.int32),
                       pltpu.SemaphoreType.DMA((2,))],
    )
    def k(u_hbm, i_hbm, h_hbm, o_hbm, rows, sems):
        def body(i_v, h_v, o_v):
            def fetch(j, slot):
                return pltpu.make_async_copy(u_hbm.at[i_v.at[j]], rows.at[slot], sems.at[slot])

            fetch(0, 0).start()

            def sub(j, carry):
                slot = j % 2

                @pl.when(j + 1 < nsub)
                def _():
                    fetch(j + 1, 1 - slot).start()

                fetch(j, slot).wait()
                t = j // GROUPS_PER_TOK

                def picks(g, carry2):
                    kb = g * ACT_UNROLL
                    accs = [None] * ACT_UNROLL
                    for c0 in range(0, n_chunks, ACT_BF16_TERMS):
                        cols = [pl.ds((c0 + cc) * SC_LANES, SC_LANES) for cc in range(ACT_BF16_TERMS)]
                        hs = [as_pairs(h_v[t, col]) for col in cols]
                        for i in range(ACT_UNROLL):
                            s = None
                            for col, hv in zip(cols, hs):
                                p = as_pairs(rows[slot, kb + i, col]) * hv
                                s = p if s is None else s + p
                            lo, hi = plsc.unpack(s, format=plsc.PackFormat.INTERLEAVED)
                            p32 = lo + hi
                            accs[i] = p32 if accs[i] is None else accs[i] + p32
                    for i in range(ACT_UNROLL):
                        o_v[j, pl.ds((kb + i) * SC_LANES, SC_LANES)] = accs[i]
                    return carry2

                lax.fori_loop(0, PICK_GROUP // ACT_UNROLL, picks, 0)
                return carry

            lax.fori_loop(0, nsub, sub, 0)

        _sc_pipeline(
            body, n_tok // TOK_STEP,
            [pl.BlockSpec((nsub, PICK_GROUP), lambda i: (i, 0)),
             pl.BlockSpec((TOK_STEP, half), lambda i: (i, 0))],
            [pl.BlockSpec((nsub, PICK_GROUP * SC_LANES), lambda i: (i, 0))],
            (i_hbm, h_hbm, o_hbm))

    return k(u_q, idx2, h_q)


def peer_combine_sc(v_p, idx2, coef_b):
    n_groups = idx2.shape[0]
    half = v_p.shape[1]
    nsub = COMB_GROUP_STEP
    n_chunks = half // SC_LANES
    as_pairs = lambda w: plsc.bitcast(w, BF16)

    @functools.partial(
        pl.kernel, mesh=_sc_mesh(), compiler_params=pltpu.CompilerParams(needs_layout_passes=False),
        out_type=jax.ShapeDtypeStruct((n_groups, 2 * half), F32),
        scratch_types=[pltpu.VMEM((2, PICK_GROUP, half), jnp.int32),
                       pltpu.SemaphoreType.DMA((2,))],
    )
    def k(v_hbm, i_hbm, c_hbm, o_hbm, rows, sems):
        def body(i_v, c_v, o_v):
            def fetch(j, slot):
                return pltpu.make_async_copy(v_hbm.at[i_v.at[j]], rows.at[slot], sems.at[slot])

            fetch(0, 0).start()

            def sub(j, carry0):
                slot = j % 2

                @pl.when(j + 1 < nsub)
                def _():
                    fetch(j + 1, 1 - slot).start()

                fetch(j, slot).wait()
                cks = [as_pairs(c_v[j, pl.ds(kk * SC_LANES, SC_LANES)]) for kk in range(PICK_GROUP)]

                def chunk_pair(c2, carry):
                    sums = []
                    for cc in range(COMB_CHUNKS):
                        l = (c2 * COMB_CHUNKS + cc) * SC_LANES
                        tot_lo, tot_hi = None, None
                        for k0 in range(0, PICK_GROUP, COMB_BF16_TERMS):
                            s = None
                            for kk in range(k0, k0 + COMB_BF16_TERMS):
                                p = cks[kk] * as_pairs(rows[slot, kk, pl.ds(l, SC_LANES)])
                                s = p if s is None else s + p
                            lo, hi = plsc.unpack(s, format=plsc.PackFormat.INTERLEAVED)
                            tot_lo = lo if tot_lo is None else tot_lo + lo
                            tot_hi = hi if tot_hi is None else tot_hi + hi
                        sums.append((l, tot_lo, tot_hi))
                    for l, tot_lo, tot_hi in sums:
                        o_v[j, pl.ds(l, SC_LANES)] = tot_lo
                        o_v[j, pl.ds(half + l, SC_LANES)] = tot_hi
                    return carry

                lax.fori_loop(0, n_chunks // COMB_CHUNKS, chunk_pair, 0)
                return carry0

            lax.fori_loop(0, nsub, sub, 0)

        _sc_pipeline(
            body, n_groups // nsub,
            [pl.BlockSpec((nsub, PICK_GROUP), lambda i: (i, 0)),
             pl.BlockSpec((nsub, PICK_GROUP * SC_LANES), lambda i: (i, 0))],
            [pl.BlockSpec((nsub, 2 * half), lambda i: (i, 0))],
            (i_hbm, c_hbm, o_hbm))

    return k(v_p, idx2, coef_b)


def _segment_matrix():
    r = lax.broadcasted_iota(jnp.int32, (PICK_GROUP * SC_LANES, PICK_GROUP), 0) // SC_LANES
    c = lax.broadcasted_iota(jnp.int32, (PICK_GROUP * SC_LANES, PICK_GROUP), 1)
    return (r == c).astype(F32)


def _coef_kernel(part_ref, gate_ref, o_ref):
    seg = _segment_matrix()
    act = jnp.dot(part_ref[...], seg, preferred_element_type=F32, precision=lax.Precision.HIGHEST)
    coef = gate_ref[...] * (0.5 * act * (1.0 + lax.erf(act * (2.0 ** -0.5))))
    wide = lax.dot_general(coef, seg, (((1,), (1,)), ((), ())), preferred_element_type=F32,
                           precision=lax.Precision.HIGHEST)
    bits = pltpu.bitcast(wide, jnp.uint32)
    r = (bits + jnp.uint32(0x7FFF) + ((bits >> 16) & jnp.uint32(1))) >> 16
    o_ref[...] = pltpu.bitcast(r | (r << 16), jnp.int32)


def peer_coef_tc(part, gate2):
    n_groups, width = part.shape
    tile = 1024
    return pl.pallas_call(
        _coef_kernel,
        grid=(n_groups // tile,),
        in_specs=[pl.BlockSpec((tile, width), lambda i: (i, 0)),
                  pl.BlockSpec((tile, PICK_GROUP), lambda i: (i, 0))],
        out_specs=pl.BlockSpec((tile, width), lambda i: (i, 0)),
        out_shape=jax.ShapeDtypeStruct((n_groups, width), jnp.int32),
        compiler_params=_cparams("parallel"), name="peer_coef",
    )(part, gate2)


def _final_kernel(x_ref, y_ref, mod_ref, g_ref, o_ref):
    d = x_ref.shape[2]
    y = y_ref[0, :, 0:d]
    for p in range(1, y_ref.shape[2] // d):
        y = y + y_ref[0, :, p * d:(p + 1) * d]
    x = x_ref[0] + mod_ref[0, 5:6, :] * y
    o_ref[0] = x * lax.rsqrt(jnp.mean(x * x, axis=-1, keepdims=True) + EPS) * g_ref[...]


def final_tc(x1, y_parts, mod_l, final_g):
    b, l, d = x1.shape
    tok = pl.BlockSpec((1, TOK_TILE, d), lambda bi, i: (bi, i, 0))
    return pl.pallas_call(
        _final_kernel, grid=(b, l // TOK_TILE),
        in_specs=[tok, pl.BlockSpec((1, TOK_TILE, y_parts.shape[2]), lambda bi, i: (bi, i, 0)),
                  pl.BlockSpec((1,) + mod_l.shape[1:], lambda bi, i: (bi, 0, 0)),
                  pl.BlockSpec((1, d), lambda bi, i: (0, 0))],
        out_specs=tok, out_shape=jax.ShapeDtypeStruct((b, l, d), F32),
        compiler_params=_cparams("parallel", "arbitrary"), name="final_norm",
    )(x1, y_parts, mod_l, final_g.reshape(1, d))


def _pad_cols(w, n):
    return jnp.pad(w, ((0, 0), (0, n - w.shape[1])))


def _slice_sizes(b):
    if b % SLICE_BATCH or SLICE_BATCH % 2 or b < 2 * SLICE_BATCH:
        return [1] * b
    half = SLICE_BATCH // 2
    return [half] + [SLICE_BATCH] * (b // SLICE_BATCH - 1) + [half]


def _exact_zero(v):
    return jnp.minimum(jnp.abs(v), 0.0)


def forward(x, c, ctx, c_ctx, w_ada, b_ada, norm1_g, norm2_g, w_in, conv_w, dn_a_log,
            dn_dt_bias, dn_norm_g, gla_wa2, gla_ba, gla_norm_g, w_out, peer_wq, peer_keys,
            peer_u, peer_v, final_g):
    b, l, d = x.shape
    c_all = jnp.concatenate([c, c_ctx[None]], axis=0)
    c_all = jnp.pad(c_all, ((0, (-c_all.shape[0]) % SUBLANES), (0, 0)))
    mod = adaln_mod(c_all, w_ada, b_ada)
    mod_l = mod[:b].reshape(b, 6, d)
    mod_c = jnp.broadcast_to(mod[b].reshape(1, 6, d), (b, 6, d))
    o = DN_QKV
    hv = DN_HEADS * DN_DV
    w_dn_qkv, w_dn_z = w_in[:, :o], w_in[:, o:o + hv]
    w_dn_ba = _pad_cols(w_in[:, o + hv:DN_COLS], LANES)
    g0 = DN_COLS
    gqk, gv = 2 * GLA_HEADS * GLA_DK, GLA_HEADS * GLA_DV
    w_gl_qk, w_gl_v = w_in[:, g0:g0 + gqk], w_in[:, g0 + gqk:g0 + gqk + gv]
    w_gl_r = w_in[:, g0 + gqk + gv:g0 + gqk + 2 * gv]
    w_gl_lr = _pad_cols(w_in[:, g0 + gqk + 2 * gv:], LANES)
    w_lat = [w.astype(BF16) for w in (w_dn_qkv, w_dn_ba, w_gl_qk, w_gl_v, w_gl_lr, w_dn_z, w_gl_r)]
    w_ctx = w_lat[:5]
    w_la = jnp.zeros((2, LANES, GLA_HEADS * GLA_DK), F32)
    for dd in range(2):
        w_la = w_la.at[dd, dd * GLA_LR:(dd + 1) * GLA_LR].set(gla_wa2[dd])
    w_la = w_la.astype(BF16)
    b_la = gla_ba.reshape(2, 1, GLA_HEADS * GLA_DK)
    w_out_b, w_q_b = w_out.astype(BF16), peer_wq.astype(BF16)
    u_q, v_p = pack_pairs(peer_u), pack_pairs(peer_v)

    def mixer(xg, ctxg, mod_cg, mod_g, after_select, after_combine):
        mod_cg, mod_g = mod_cg + after_select, mod_g + after_select
        c_qkv, c_ba, c_qk, c_v, c_lr = in_projection(ctxg, norm1_g, mod_cg[:, 0:2], w_ctx)
        l_qkv, l_ba, l_qk, l_v, l_lr, l_z, l_r = in_projection(xg, norm1_g, mod_g[:, 0:2], w_lat)
        feat_c, feat_l = dn_features_tc(c_qkv, conv_w), dn_features_tc(l_qkv, conv_w)
        dn_f = dn_scan_tc(feat_c, feat_l, c_ba, l_ba, dn_a_log, dn_dt_bias, rev=False)
        dn_b = dn_scan_tc(feat_c, feat_l, c_ba, l_ba, dn_a_log, dn_dt_bias, rev=True)
        gl_f = gla_scan_tc(c_qk, c_v, c_lr, l_qk, l_v, l_lr, w_la, b_la, rev=False)
        gl_b = gla_scan_tc(c_qk, c_v, c_lr, l_qk, l_v, l_lr, w_la, b_la, rev=True)
        return mix_out_tc(xg, dn_f, dn_b, l_z, gl_f, gl_b, l_r, mod_g, dn_norm_g + after_combine,
                          gla_norm_g, norm2_g, w_out_b, w_q_b)

    def select_and_act(h2, q, after_coef):
        n_tok = h2.shape[0] * l
        idx, gate = peer_select_tc(q.reshape(n_tok, -1), peer_keys + after_coef)
        idx2 = idx.reshape(n_tok * GROUPS_PER_TOK, PICK_GROUP)
        gate2 = gate.reshape(n_tok * GROUPS_PER_TOK, PICK_GROUP)
        return idx2, gate2, peer_act_partial_sc(u_q, idx2, pack_pairs(h2.reshape(n_tok, d)))

    def coef_and_combine(st, after_mixer):
        coef_b = peer_coef_tc(st["part"], st["gate2"] + after_mixer)
        st["y"] = peer_combine_sc(v_p, st["idx2"], coef_b)
        return _exact_zero(coef_b[0, 0].astype(F32))

    zero = jnp.zeros((), F32)
    slices, z_sel, z_coef = [], zero, zero
    i = 0
    for bg in _slice_sizes(b):
        g = len(slices)
        z_comb = _exact_zero(slices[g - 2]["y"][0, 0]) if g >= 2 else zero
        mod_g = mod_l[i:i + bg]
        x1, h2, q = mixer(x[i:i + bg], ctx[i:i + bg], mod_c[i:i + bg], mod_g, z_sel, z_comb)
        i += bg
        if g >= 1:
            z_coef = coef_and_combine(slices[g - 1], _exact_zero(x1[0, 0, 0]))
        idx2, gate2, part = select_and_act(h2, q, z_coef)
        z_sel = _exact_zero(gate2[0, 0])
        slices.append(dict(x1=x1, mod=mod_g, idx2=idx2, gate2=gate2, part=part))
    coef_and_combine(slices[-1], zero)
    outs = [final_tc(st["x1"], st["y"].reshape(st["x1"].shape[:2] + (-1,)), st["mod"], final_g) for st in slices]
    return jnp.concatenate(outs, axis=0)


def kernel(x, c, ctx, c_ctx, w_ada, b_ada, norm1_g, norm2_g, w_in, conv_w, dn_a_log,
           dn_dt_bias, dn_norm_g, gla_wa2, gla_ba, gla_norm_g, w_out, peer_wq, peer_keys,
           peer_u, peer_v, final_g):
    assert w_ada.shape[0] == 1, "single-layer block: the context stream is only consumed, never updated"
    return forward(x, c, ctx, c_ctx, w_ada[0], b_ada[0], norm1_g[0], norm2_g[0], w_in[0], conv_w[0],
                   dn_a_log[0], dn_dt_bias[0], dn_norm_g[0], gla_wa2[0], gla_ba[0], gla_norm_g[0],
                   w_out[0], peer_wq[0], peer_keys[0], peer_u[0], peer_v[0], final_g)
```

```python
import functools

import jax
import jax.numpy as jnp
from jax import lax
from jax.experimental import pallas as pl
from jax.experimental.pallas import tpu as pltpu
from jax.experimental.pallas import tpu_sc as plsc

GRID_W = 64
DN_HEADS = 4
DN_DK = 128
DN_DV = 128
CONV_W = 5
GLA_HEADS = 4
GLA_DK = 64
GLA_DV = 128
GLA_LR = 16
GLA_TAU = 16.0
CHUNK = 64
PEER_HEADS = 8
PEER_NKEYS = 128
PEER_DQ = 256
PEER_TOPK = 16
EPS = 1e-6
DN_QKV = 2 * DN_HEADS * DN_DK + DN_HEADS * DN_DV
DN_COLS = DN_QKV + DN_HEADS * DN_DV + 4 * DN_HEADS

SUBLANES = 8
LANES = 128
SC_LANES = 16
VMEM_LIMIT_BYTES = 48 * 1024 * 1024

TOK_TILE = 256
SELECT_TILE = 256
GLA_SUB = 16
PICK_GROUP = 32
COMB_GROUP_STEP = 16
TOK_STEP = 8
COMB_BF16_TERMS = 4
COMB_CHUNKS = 4
ACT_BF16_TERMS = 4
ACT_UNROLL = 16
SLICE_BATCH = 2
PICKS = PEER_HEADS * PEER_TOPK
GROUPS_PER_TOK = PICKS // PICK_GROUP

F32 = jnp.float32
BF16 = jnp.bfloat16


def _cparams(*semantics):
    return pltpu.CompilerParams(dimension_semantics=semantics, vmem_limit_bytes=VMEM_LIMIT_BYTES)


def _dot(a, b):
    return jnp.dot(a.astype(BF16), b.astype(BF16), preferred_element_type=F32)


def _dot_nt(a, b):
    return lax.dot_general(a.astype(BF16), b.astype(BF16), (((1,), (1,)), ((), ())),
                           preferred_element_type=F32)


def _dot_tn(a, b):
    return lax.dot_general(a.astype(BF16), b.astype(BF16), (((0,), (0,)), ((), ())),
                           preferred_element_type=F32)


def _split(x):
    hi = x.astype(BF16)
    return hi, (x - hi.astype(F32)).astype(BF16)


def _mask_dot(mask_bf16, x):
    hi, lo = _split(x)
    return (jnp.dot(mask_bf16, hi, preferred_element_type=F32)
            + jnp.dot(mask_bf16, lo, preferred_element_type=F32))


def _softplus(x):
    return jnp.maximum(x, 0.0) + jnp.log(1.0 + jnp.exp(-jnp.abs(x)))


def _tri_masks(rev):
    r = lax.broadcasted_iota(jnp.int32, (CHUNK, CHUNK), 0)
    c = lax.broadcasted_iota(jnp.int32, (CHUNK, CHUNK), 1)
    d = (c - r) if rev else (r - c)
    return d >= 0, d > 0


def _mod_kernel(c_ref, w_ref, b_ref, o_ref):
    c = c_ref[...]
    s = c * jax.nn.sigmoid(c)
    o_ref[...] = jnp.dot(s, w_ref[...], preferred_element_type=F32,
                         precision=lax.Precision.HIGHEST) + b_ref[...]


def adaln_mod(c_all, w_ada, b_ada):
    r, d = c_all.shape
    n = w_ada.shape[1]
    tn = 512
    return pl.pallas_call(
        _mod_kernel, grid=(n // tn,),
        in_specs=[pl.BlockSpec((r, d), lambda j: (0, 0)),
                  pl.BlockSpec((d, tn), lambda j: (0, j)),
                  pl.BlockSpec((1, tn), lambda j: (0, j))],
        out_specs=pl.BlockSpec((r, tn), lambda j: (0, j)),
        out_shape=jax.ShapeDtypeStruct((r, n), F32),
        compiler_params=_cparams("arbitrary"), name="adaln_mod",
    )(c_all, w_ada, b_ada.reshape(1, n))


def _inproj_kernel(x_ref, g_ref, mod_ref, *refs):
    n_out = len(refs) // 2
    x = x_ref[0]
    y = x * lax.rsqrt(jnp.mean(x * x, axis=-1, keepdims=True) + EPS) * g_ref[...]
    h = (y * (1.0 + mod_ref[0, 1:2, :]) + mod_ref[0, 0:1, :]).astype(BF16)
    for w_ref, o_ref in zip(refs[:n_out], refs[n_out:]):
        o_ref[0] = jnp.dot(h, w_ref[...], preferred_element_type=F32)


def in_projection(x, norm_g, mod, weights):
    b, l, d = x.shape
    w_specs = [pl.BlockSpec(w.shape, lambda bi, i: (0, 0)) for w in weights]
    o_specs = [pl.BlockSpec((1, TOK_TILE, w.shape[1]), lambda bi, i: (bi, i, 0)) for w in weights]
    return pl.pallas_call(
        _inproj_kernel, grid=(b, l // TOK_TILE),
        in_specs=[pl.BlockSpec((1, TOK_TILE, d), lambda bi, i: (bi, i, 0)),
                  pl.BlockSpec((1, d), lambda bi, i: (0, 0)),
                  pl.BlockSpec((1, 2, d), lambda bi, i: (bi, 0, 0))] + w_specs,
        out_specs=o_specs,
        out_shape=[jax.ShapeDtypeStruct((b, l, w.shape[1]), F32) for w in weights],
        compiler_params=_cparams("parallel", "arbitrary"), name="in_projection",
    )(x, norm_g.reshape(1, d), mod, *weights)


def _dn_feature_kernel(x_ref, w_ref, o_ref):
    x = x_ref[0]
    n = x.shape[0]
    t = lax.broadcasted_iota(jnp.int32, (n, 1), 0)
    pad = CONV_W // 2
    acc = w_ref[0, pad:pad + 1, :] * x
    for j in range(CONV_W):
        s = j - pad
        if s == 0:
            continue
        xs = pltpu.roll(x, (-s) % n, axis=0)
        bad = (t < -s) if s < 0 else (t >= n - s)
        acc = acc + w_ref[0, j:j + 1, :] * jnp.where(bad, 0.0, xs)
    y = acc * jax.nn.sigmoid(acc)
    kind = pl.program_id(1) // DN_HEADS
    inv = lax.rsqrt(jnp.sum(y * y, axis=-1, keepdims=True) + EPS)
    scale = jnp.where(kind == 0, inv * DN_DK ** -0.5, jnp.where(kind == 1, inv, 1.0))
    o_ref[0] = y * scale


def dn_features_tc(qkv, conv_w):
    b, l, n = qkv.shape
    nblk = n // LANES
    w = jnp.zeros((nblk, SUBLANES, LANES), F32).at[:, :CONV_W].set(
        conv_w.reshape(CONV_W, nblk, LANES).transpose(1, 0, 2))
    return pl.pallas_call(
        _dn_feature_kernel, grid=(b, nblk),
        in_specs=[pl.BlockSpec((1, l, LANES), lambda bi, j: (bi, 0, j)),
                  pl.BlockSpec((1, SUBLANES, LANES), lambda bi, j: (j, 0, 0))],
        out_specs=pl.BlockSpec((1, l, LANES), lambda bi, j: (bi, 0, j)),
        out_shape=jax.ShapeDtypeStruct((b, l, n), F32),
        compiler_params=_cparams("parallel", "arbitrary"), name="dn_features",
    )(qkv, w)


def _scan_chunks(rev, n_ctx, n_lat):
    if rev:
        ctx = lambda j: jnp.maximum(n_ctx - 1 - j, 0)
        lat = lambda j: jnp.where(j < n_ctx, n_lat - 1, n_lat - 1 - (j - n_ctx))
    else:
        ctx = lambda j: jnp.minimum(j, n_ctx - 1)
        lat = lambda j: jnp.maximum(j - n_ctx, 0)
    return ctx, lat


def _dn_scan_kernel(rev, dirn, n_ctx, alog_ref, dtb_ref, fc_ref, fl_ref, bac_ref, bal_ref, o_ref, s_ref):
    step = pl.program_id(1)

    @pl.when(step == 0)
    def _():
        s_ref[...] = jnp.zeros_like(s_ref)

    in_ctx = step < n_ctx
    f = jnp.where(in_ctx, fc_ref[0], fl_ref[0])
    ba = jnp.where(in_ctx, bac_ref[0], bal_ref[0])
    nh, hd, n = DN_HEADS, DN_HEADS * DN_DK, DN_HEADS * CHUNK
    stack = lambda base, w: jnp.concatenate([f[:, base + h * w:base + (h + 1) * w] for h in range(nh)], axis=0)
    q_s, k_s, v_s = stack(0, DN_DK), stack(hd, DN_DK), stack(2 * hd, DN_DV)
    incl, _ = _tri_masks(rev)
    beta_all = jax.nn.sigmoid(ba)
    g_all = -jnp.exp(alog_ref[...]) * _softplus(ba + dtb_ref[...])
    gc_all = _mask_dot(incl.astype(BF16), g_all)
    gc_t = jnp.concatenate([gc_all, gc_all], axis=0).T
    g_tot = jnp.sum(g_all, axis=0, keepdims=True)
    cb = [dirn * nh + h for h in range(nh)]
    cg = [2 * nh + c for c in cb]
    col = lambda a, cs: jnp.concatenate([a[:, c:c + 1] for c in cs], axis=0)
    beta_c, gc_c = col(beta_all, cb), col(gc_all, cg)
    gtot_c = jnp.concatenate([jnp.broadcast_to(g_tot[:, c:c + 1], (CHUNK, 1)) for c in cg], axis=0)
    gc_r = jnp.concatenate([gc_t[c:c + 1, :CHUNK] for c in cg], axis=1)
    r = lax.broadcasted_iota(jnp.int32, (n, n), 0)
    c = lax.broadcasted_iota(jnp.int32, (n, n), 1)
    same = (r // CHUNK) == (c // CHUNK)
    d = (c - r) if rev else (r - c)
    incl_bd, strict_bd = same & (d >= 0), same & (d > 0)
    eye = (r == c).astype(F32)
    decay = jnp.where(incl_bd, jnp.exp(jnp.where(incl_bd, gc_c - gc_r, 0.0)), 0.0)
    kb_s = k_s * beta_c
    lower = jnp.where(strict_bd, _dot_nt(kb_s, k_s) * decay, 0.0)
    eg_c = jnp.exp(gc_c)
    inv = eye - lower
    pw = lower
    for _ in range(5):
        pw = _dot(pw, pw)
        inv = inv + _dot(inv, pw)
    sol = _dot(inv, jnp.concatenate([v_s * beta_c, kb_s * eg_c], axis=-1))
    u_s, w_s = sol[:, :DN_DV], sol[:, DN_DV:]
    k_dec = k_s * jnp.exp(gtot_c - gc_c)
    rb = lax.broadcasted_iota(jnp.int32, (n, DN_DK), 0) // CHUNK
    expand = lambda x: jnp.concatenate([jnp.where(rb == h, x, 0.0) for h in range(nh)], axis=1)
    s = s_ref[...]
    v_new = u_s - _dot(expand(w_s), s)
    a_qk = _dot_nt(q_s, k_s) * decay
    o_s = _dot(expand(q_s * eg_c), s) + _dot(a_qk, v_new)
    gl_rows = jnp.concatenate([jnp.broadcast_to(jnp.exp(g_tot[:, cc:cc + 1]), (DN_DK, 1)) for cc in cg], axis=0)
    s_ref[...] = s * gl_rows + _dot_tn(expand(k_dec), v_new)
    o_ref[0] = jnp.concatenate([o_s[h * CHUNK:(h + 1) * CHUNK] for h in range(nh)], axis=1)


def dn_scan_tc(feat_c, feat_l, ba_c, ba_l, a_log, dt_bias, rev):
    b, l, nf = feat_l.shape
    n_ctx, n_lat = feat_c.shape[1] // CHUNK, l // CHUNK
    dirn = 1 if rev else 0
    cc, lc = _scan_chunks(rev, n_ctx, n_lat)
    lanes = lambda p: jnp.zeros((1, LANES), F32).at[0, 2 * DN_HEADS:4 * DN_HEADS].set(p.reshape(-1))
    vec = pl.BlockSpec((1, LANES), lambda bi, j: (0, 0))
    return pl.pallas_call(
        functools.partial(_dn_scan_kernel, rev, dirn, n_ctx), grid=(b, n_ctx + n_lat),
        in_specs=[vec, vec,
                  pl.BlockSpec((1, CHUNK, nf), lambda bi, j: (bi, cc(j), 0)),
                  pl.BlockSpec((1, CHUNK, nf), lambda bi, j: (bi, lc(j), 0)),
                  pl.BlockSpec((1, CHUNK, LANES), lambda bi, j: (bi, cc(j), 0)),
                  pl.BlockSpec((1, CHUNK, LANES), lambda bi, j: (bi, lc(j), 0))],
        out_specs=pl.BlockSpec((1, CHUNK, DN_HEADS * DN_DV), lambda bi, j: (bi, lc(j), 0)),
        out_shape=jax.ShapeDtypeStruct((b, l, DN_HEADS * DN_DV), F32),
        scratch_shapes=[pltpu.VMEM((DN_HEADS * DN_DK, DN_DV), F32)],
        compiler_params=_cparams("parallel", "arbitrary"), name="dn_scan_bwd" if rev else "dn_scan_fwd",
    )(lanes(a_log), lanes(dt_bias), feat_c, feat_l, ba_c, ba_l)


def _from_grid_cols(blk, n):
    cols = blk.shape[1] // n
    return jnp.concatenate([blk[:, i * n:(i + 1) * n] for i in range(cols)], axis=0)


def _gla_scan_kernel(rev, dirn, n_ctx, qkc_ref, vc_ref, lrc_ref, qkl_ref, vl_ref, lrl_ref,
                     wla_ref, bla_ref, o_ref, s_ref):
    step = pl.program_id(1)

    @pl.when(step == 0)
    def _():
        s_ref[...] = jnp.zeros_like(s_ref)

    in_ctx = step < n_ctx
    hk, hv = GLA_HEADS * GLA_DK, GLA_HEADS * GLA_DV
    qk = jnp.where(in_ctx, qkc_ref[0], _from_grid_cols(qkl_ref[0], 2 * hk))
    vv = jnp.where(in_ctx, vc_ref[0], _from_grid_cols(vl_ref[0], hv))
    lr = jnp.where(in_ctx, lrc_ref[0], _from_grid_cols(lrl_ref[0], LANES))
    incl, _ = _tri_masks(rev)
    incl_b = incl.astype(BF16)
    pre = _dot(lr, wla_ref[0]) + bla_ref[0]
    la_all = -_softplus(-pre) * (1.0 / GLA_TAU)
    bc_all = _mask_dot(incl_b, la_all)
    b_tot_all = jnp.sum(la_all, axis=0, keepdims=True)
    outs = []
    for h in range(GLA_HEADS):
        q = qk[:, h * GLA_DK:(h + 1) * GLA_DK] * GLA_DK ** -0.5
        k = qk[:, hk + h * GLA_DK:hk + (h + 1) * GLA_DK]
        v = vv[:, h * GLA_DV:(h + 1) * GLA_DV]
        bc = bc_all[:, h * GLA_DK:(h + 1) * GLA_DK]
        b_tot = b_tot_all[:, h * GLA_DK:(h + 1) * GLA_DK]
        st = s_ref[h]
        o = _dot_nt(q * jnp.exp(bc), st)
        parts = []
        for i in range(CHUNK // GLA_SUB):
            lo_r, hi_r = i * GLA_SUB, (i + 1) * GLA_SUB
            if rev:
                ref = bc[hi_r - 1:hi_r]
                c0, c1 = lo_r, CHUNK
            else:
                ref = bc[lo_r:lo_r + 1]
                c0, c1 = 0, hi_r
            qi = q[lo_r:hi_r] * jnp.exp(bc[lo_r:hi_r] - ref)
            ki = k[c0:c1] * jnp.exp(ref - bc[c0:c1])
            att = _dot_nt(qi, ki)
            rg = lax.broadcasted_iota(jnp.int32, (GLA_SUB, c1 - c0), 0) + lo_r
            cg = lax.broadcasted_iota(jnp.int32, (GLA_SUB, c1 - c0), 1) + c0
            keep = (cg > rg) if rev else (cg < rg)
            parts.append(_dot(jnp.where(keep, att, 0.0), v[c0:c1]))
        diag = jnp.sum(q * k, axis=-1, keepdims=True) * v
        outs.append(o + jnp.concatenate(parts, axis=0) + diag)
        k_dec = k * jnp.exp(b_tot - bc)
        s_ref[h] = st * jnp.exp(b_tot) + _dot_tn(v, k_dec)
    o = jnp.concatenate(outs, axis=-1)
    rows = o_ref.shape[1]
    o_ref[0] = jnp.concatenate([o[i * rows:(i + 1) * rows] for i in range(CHUNK // rows)], axis=-1)


def gla_scan_tc(qk_c, v_c, lr_c, qk_l, v_l, lr_l, w_la, b_la, rev):
    b, l, _ = qk_l.shape
    rows = l // GRID_W
    cols = CHUNK // rows
    n_ctx, n_lat = qk_c.shape[1] // CHUNK, l // CHUNK
    dirn = 1 if rev else 0
    cc, lc = _scan_chunks(rev, n_ctx, n_lat)
    hv = GLA_HEADS * GLA_DV
    ctx_blk = lambda a: pl.BlockSpec((1, CHUNK, a.shape[2]), lambda bi, j: (bi, cc(j), 0))
    lat_blk = lambda n: pl.BlockSpec((1, rows, cols * n), lambda bi, j: (bi, 0, lc(j)))
    grid_view = lambda a: a.reshape(b, rows, GRID_W * a.shape[2])
    out = pl.pallas_call(
        functools.partial(_gla_scan_kernel, rev, dirn, n_ctx), grid=(b, n_ctx + n_lat),
        in_specs=[ctx_blk(qk_c), ctx_blk(v_c), ctx_blk(lr_c),
                  lat_blk(qk_l.shape[2]), lat_blk(v_l.shape[2]), lat_blk(lr_l.shape[2]),
                  pl.BlockSpec((1,) + w_la.shape[1:], lambda bi, j: (dirn, 0, 0)),
                  pl.BlockSpec((1,) + b_la.shape[1:], lambda bi, j: (dirn, 0, 0))],
        out_specs=lat_blk(hv),
        out_shape=jax.ShapeDtypeStruct((b, rows, GRID_W * hv), F32),
        scratch_shapes=[pltpu.VMEM((GLA_HEADS, GLA_DV, GLA_DK), F32)],
        compiler_params=_cparams("parallel", "arbitrary"), name="gla_scan_bwd" if rev else "gla_scan_fwd",
    )(qk_c, v_c, lr_c, grid_view(qk_l), grid_view(v_l), grid_view(lr_l), w_la, b_la)
    return out.reshape(b, l, hv)


def _head_norm_gate(o, gate, g, n_heads, dv):
    parts = []
    for h in range(n_heads):
        oh = o[:, h * dv:(h + 1) * dv]
        gh = gate[:, h * dv:(h + 1) * dv]
        yh = oh * lax.rsqrt(jnp.mean(oh * oh, axis=-1, keepdims=True) + EPS) * g
        parts.append(yh * (gh * jax.nn.sigmoid(gh)))
    return parts


def _mix_out_kernel(x_ref, dnf_ref, dnb_ref, z_ref, glf_ref, glb_ref, r_ref, mod_ref, dng_ref,
                    glg_ref, n2g_ref, wout_ref, wq_ref, x1_ref, h2_ref, q_ref):
    parts = (_head_norm_gate(dnf_ref[0] + dnb_ref[0], z_ref[0], dng_ref[...], DN_HEADS, DN_DV)
             + _head_norm_gate(glf_ref[0] + glb_ref[0], r_ref[0], glg_ref[...], GLA_HEADS, GLA_DV))
    y = jnp.dot(jnp.concatenate(parts, axis=-1).astype(BF16), wout_ref[...], preferred_element_type=F32)
    x1 = x_ref[0] + mod_ref[0, 2:3, :] * y
    x1_ref[0] = x1
    n = x1 * lax.rsqrt(jnp.mean(x1 * x1, axis=-1, keepdims=True) + EPS) * n2g_ref[...]
    h2 = n * (1.0 + mod_ref[0, 4:5, :]) + mod_ref[0, 3:4, :]
    h2_ref[0] = h2
    q_ref[0] = jnp.dot(h2.astype(BF16), wq_ref[...], preferred_element_type=F32)


def mix_out_tc(x, dn_f, dn_b, z, gl_f, gl_b, r, mod_l, dn_g, gla_g, n2_g, w_out, w_q):
    b, l, d = x.shape
    tok = lambda n: pl.BlockSpec((1, TOK_TILE, n), lambda bi, i: (bi, i, 0))
    full = lambda a: pl.BlockSpec(a.shape, lambda bi, i: (0,) * a.ndim)
    dn_g, gla_g, n2_g = dn_g.reshape(1, -1), gla_g.reshape(1, -1), n2_g.reshape(1, -1)
    nq = w_q.shape[1]
    return pl.pallas_call(
        _mix_out_kernel, grid=(b, l // TOK_TILE),
        in_specs=[tok(d), tok(dn_f.shape[2]), tok(dn_b.shape[2]), tok(z.shape[2]),
                  tok(gl_f.shape[2]), tok(gl_b.shape[2]), tok(r.shape[2]),
                  pl.BlockSpec((1,) + mod_l.shape[1:], lambda bi, i: (bi, 0, 0)),
                  full(dn_g), full(gla_g), full(n2_g), full(w_out), full(w_q)],
        out_specs=[tok(d), tok(d), tok(nq)],
        out_shape=[jax.ShapeDtypeStruct((b, l, d), F32), jax.ShapeDtypeStruct((b, l, d), F32),
                   jax.ShapeDtypeStruct((b, l, nq), F32)],
        compiler_params=_cparams("parallel", "arbitrary"), name="mix_out",
    )(x, dn_f, dn_b, z, gl_f, gl_b, r, mod_l, dn_g, gla_g, n2_g, w_out, w_q)


def _top_rows(s, k, payload=None):
    n = s.shape[0]
    row = lax.broadcasted_iota(jnp.int32, s.shape, 0).astype(F32)
    vals, picked = [], []
    for _ in range(k):
        m = jnp.max(s, axis=0, keepdims=True)
        first = jnp.min(jnp.where(s == m, row, float(n)), axis=0, keepdims=True)
        sel = row == first
        vals.append(m)
        if payload is None:
            picked.append(first)
        else:
            picked.append(jnp.max(jnp.where(sel, payload, -1.0), axis=0, keepdims=True))
        s = jnp.where(sel, -jnp.inf, s)
    return jnp.concatenate(vals, axis=0), jnp.concatenate(picked, axis=0)


def _candidate_rows(s0, i0, s1, i1):
    k = s0.shape[0]
    wide = SUBLANES
    blocks_s = [s0[0:1] + s1]
    blocks_i = [i0[0:1] * float(PEER_NKEYS) + i1]
    col = lax.broadcasted_iota(jnp.int32, (wide, s0.shape[1]), 0)
    for i in range(1, wide):
        keep = col < (k // (i + 1))
        blocks_s.append(jnp.where(keep, s0[i:i + 1] + s1[0:wide], -jnp.inf))
        blocks_i.append(i0[i:i + 1] * float(PEER_NKEYS) + i1[0:wide])
    blocks_s.append(s0[wide:k] + s1[0:1])
    blocks_i.append(i0[wide:k] * float(PEER_NKEYS) + i1[0:1])
    return jnp.concatenate(blocks_s, axis=0), jnp.concatenate(blocks_i, axis=0)


def _select_kernel(q_ref, k_ref, idx_ref, gate_ref, idx_s, gate_s):
    half = PEER_DQ // 2

    def head(h, carry):
        tops = []
        for p in range(2):
            qp = q_ref[:, pl.ds(pl.multiple_of(h * PEER_DQ + p * half, half), half)]
            s = lax.dot_general(k_ref[h, p], qp, (((1,), (1,)), ((), ())),
                                preferred_element_type=F32,
                                precision=lax.Precision.HIGHEST)
            tops.append(_top_rows(s, PEER_TOPK))
        (s0, i0), (s1, i1) = tops
        cand_s, cand_i = _candidate_rows(s0, i0, s1, i1)
        best_s, idx = _top_rows(cand_s, PEER_TOPK, payload=cand_i)
        e = jnp.exp(best_s - best_s[0:1])
        r0 = pl.multiple_of(h * PEER_TOPK, PEER_TOPK)
        idx_s[pl.ds(r0, PEER_TOPK), :] = idx
        gate_s[pl.ds(r0, PEER_TOPK), :] = e / jnp.sum(e, axis=0, keepdims=True)
        return carry

    lax.fori_loop(0, PEER_HEADS, head, 0)
    idx_ref[...] = idx_s[...].T.astype(jnp.int32)
    gate_ref[...] = gate_s[...].T


def peer_select_tc(q, keys):
    n_tok = q.shape[0]
    out_spec = pl.BlockSpec((SELECT_TILE, PICKS), lambda i: (i, 0))
    return pl.pallas_call(
        _select_kernel,
        grid=(n_tok // SELECT_TILE,),
        in_specs=[pl.BlockSpec((SELECT_TILE, q.shape[1]), lambda i: (i, 0)),
                  pl.BlockSpec(keys.shape, lambda i: (0, 0, 0, 0))],
        out_specs=[out_spec, out_spec],
        out_shape=[jax.ShapeDtypeStruct((n_tok, PICKS), jnp.int32),
                   jax.ShapeDtypeStruct((n_tok, PICKS), F32)],
        scratch_shapes=[pltpu.VMEM((PICKS, SELECT_TILE), F32), pltpu.VMEM((PICKS, SELECT_TILE), F32)],
        compiler_params=_cparams("parallel"), name="peer_select",
    )(q, keys)


def _sc_mesh():
    return plsc.VectorSubcoreMesh(core_axis_name="c", subcore_axis_name="s")


def _sc_pipeline(body, n_steps, in_specs, out_specs, operands):
    pltpu.emit_pipeline(
        body, grid=(n_steps,), in_specs=in_specs, out_specs=out_specs,
        core_axis_name=("c", "s"), dimension_semantics=(pltpu.PARALLEL,),
        trace_scopes=False,
    )(*operands)


def pack_pairs(t):
    half = t.shape[1] // 2
    bits = lax.bitcast_convert_type(t, jnp.uint32)
    rne = lambda b: (b + jnp.uint32(0x7FFF) + ((b >> 16) & jnp.uint32(1))) >> 16
    word = (rne(bits[:, half:]) << 16) | rne(bits[:, :half])
    return lax.bitcast_convert_type(word, jnp.int32)


def peer_act_partial_sc(u_q, idx2, h_q):
    n_groups, (n_tok, half) = idx2.shape[0], h_q.shape
    nsub = TOK_STEP * GROUPS_PER_TOK
    n_chunks = half // SC_LANES
    as_pairs = lambda w: plsc.bitcast(w, BF16)

    @functools.partial(
        pl.kernel, mesh=_sc_mesh(), compiler_params=pltpu.CompilerParams(needs_layout_passes=False),
        out_type=jax.ShapeDtypeStruct((n_groups, PICK_GROUP * SC_LANES), F32),
        scratch_types=[pltpu.VMEM((2, PICK_GROUP, half), jnp.int32),
                       pltpu.SemaphoreType.DMA((2,))],
    )
    def k(u_hbm, i_hbm, h_hbm, o_hbm, rows, sems):
        def body(i_v, h_v, o_v):
            def fetch(j, slot):
                return pltpu.make_async_copy(u_hbm.at[i_v.at[j]], rows.at[slot], sems.at[slot])

            fetch(0, 0).start()

            def sub(j, carry):
                slot = j % 2

                @pl.when(j + 1 < nsub)
                def _():
                    fetch(j + 1, 1 - slot).start()

                fetch(j, slot).wait()
                t = j // GROUPS_PER_TOK

                def picks(g, carry2):
                    kb = g * ACT_UNROLL
                    accs = [None] * ACT_UNROLL
                    for c0 in range(0, n_chunks, ACT_BF16_TERMS):
                        cols = [pl.ds((c0 + cc) * SC_LANES, SC_LANES) for cc in range(ACT_BF16_TERMS)]
                        hs = [as_pairs(h_v[t, col]) for col in cols]
                        for i in range(ACT_UNROLL):
                            s = None
                            for col, hv in zip(cols, hs):
                                p = as_pairs(rows[slot, kb + i, col]) * hv
                                s = p if s is None else s + p
                            lo, hi = plsc.unpack(s, format=plsc.PackFormat.INTERLEAVED)
                            p32 = lo + hi
                            accs[i] = p32 if accs[i] is None else accs[i] + p32
                    for i in range(ACT_UNROLL):
                        o_v[j, pl.ds((kb + i) * SC_LANES, SC_LANES)] = accs[i]
                    return carry2

                lax.fori_loop(0, PICK_GROUP // ACT_UNROLL, picks, 0)
                return carry

            lax.fori_loop(0, nsub, sub, 0)

        _sc_pipeline(
            body, n_tok // TOK_STEP,
            [pl.BlockSpec((nsub, PICK_GROUP), lambda i: (i, 0)),
             pl.BlockSpec((TOK_STEP, half), lambda i: (i, 0))],
            [pl.BlockSpec((nsub, PICK_GROUP * SC_LANES), lambda i: (i, 0))],
            (i_hbm, h_hbm, o_hbm))

    return k(u_q, idx2, h_q)


def peer_combine_sc(v_p, idx2, coef_b):
    n_groups = idx2.shape[0]
    half = v_p.shape[1]
    nsub = COMB_GROUP_STEP
    n_chunks = half // SC_LANES
    as_pairs = lambda w: plsc.bitcast(w, BF16)

    @functools.partial(
        pl.kernel, mesh=_sc_mesh(), compiler_params=pltpu.CompilerParams(needs_layout_passes=False),
        out_type=jax.ShapeDtypeStruct((n_groups, 2 * half), F32),
        scratch_types=[pltpu.VMEM((2, PICK_GROUP, half), jnp.int32),
                       pltpu.SemaphoreType.DMA((2,))],
    )
    def k(v_hbm, i_hbm, c_hbm, o_hbm, rows, sems):
        def body(i_v, c_v, o_v):
            def fetch(j, slot):
                return pltpu.make_async_copy(v_hbm.at[i_v.at[j]], rows.at[slot], sems.at[slot])

            fetch(0, 0).start()

            def sub(j, carry0):
                slot = j % 2

                @pl.when(j + 1 < nsub)
                def _():
                    fetch(j + 1, 1 - slot).start()

                fetch(j, slot).wait()
                cks = [as_pairs(c_v[j, pl.ds(kk * SC_LANES, SC_LANES)]) for kk in range(PICK_GROUP)]

                def chunk_pair(c2, carry):
                    sums = []
                    for cc in range(COMB_CHUNKS):
                        l = (c2 * COMB_CHUNKS + cc) * SC_LANES
                        tot_lo, tot_hi = None, None
                        for k0 in range(0, PICK_GROUP, COMB_BF16_TERMS):
                            s = None
                            for kk in range(k0, k0 + COMB_BF16_TERMS):
                                p = cks[kk] * as_pairs(rows[slot, kk, pl.ds(l, SC_LANES)])
                                s = p if s is None else s + p
                            lo, hi = plsc.unpack(s, format=plsc.PackFormat.INTERLEAVED)
                            tot_lo = lo if tot_lo is None else tot_lo + lo
                            tot_hi = hi if tot_hi is None else tot_hi + hi
                        sums.append((l, tot_lo, tot_hi))
                    for l, tot_lo, tot_hi in sums:
                        o_v[j, pl.ds(l, SC_LANES)] = tot_lo
                        o_v[j, pl.ds(half + l, SC_LANES)] = tot_hi
                    return carry

                lax.fori_loop(0, n_chunks // COMB_CHUNKS, chunk_pair, 0)
                return carry0

            lax.fori_loop(0, nsub, sub, 0)

        _sc_pipeline(
            body, n_groups // nsub,
            [pl.BlockSpec((nsub, PICK_GROUP), lambda i: (i, 0)),
             pl.BlockSpec((nsub, PICK_GROUP * SC_LANES), lambda i: (i, 0))],
            [pl.BlockSpec((nsub, 2 * half), lambda i: (i, 0))],
            (i_hbm, c_hbm, o_hbm))

    return k(v_p, idx2, coef_b)


def _segment_matrix():
    r = lax.broadcasted_iota(jnp.int32, (PICK_GROUP * SC_LANES, PICK_GROUP), 0) // SC_LANES
    c = lax.broadcasted_iota(jnp.int32, (PICK_GROUP * SC_LANES, PICK_GROUP), 1)
    return (r == c).astype(F32)


def _coef_kernel(part_ref, gate_ref, o_ref):
    seg = _segment_matrix()
    act = jnp.dot(part_ref[...], seg, preferred_element_type=F32, precision=lax.Precision.HIGHEST)
    coef = gate_ref[...] * (0.5 * act * (1.0 + lax.erf(act * (2.0 ** -0.5))))
    wide = lax.dot_general(coef, seg, (((1,), (1,)), ((), ())), preferred_element_type=F32,
                           precision=lax.Precision.HIGHEST)
    bits = pltpu.bitcast(wide, jnp.uint32)
    r = (bits + jnp.uint32(0x7FFF) + ((bits >> 16) & jnp.uint32(1))) >> 16
    o_ref[...] = pltpu.bitcast(r | (r << 16), jnp.int32)


def peer_coef_tc(part, gate2):
    n_groups, width = part.shape
    tile = 1024
    return pl.pallas_call(
        _coef_kernel,
        grid=(n_groups // tile,),
        in_specs=[pl.BlockSpec((tile, width), lambda i: (i, 0)),
                  pl.BlockSpec((tile, PICK_GROUP), lambda i: (i, 0))],
        out_specs=pl.BlockSpec((tile, width), lambda i: (i, 0)),
        out_shape=jax.ShapeDtypeStruct((n_groups, width), jnp.int32),
        compiler_params=_cparams("parallel"), name="peer_coef",
    )(part, gate2)


def _final_kernel(x_ref, y_ref, mod_ref, g_ref, o_ref):
    d = x_ref.shape[2]
    y = y_ref[0, :, 0:d]
    for p in range(1, y_ref.shape[2] // d):
        y = y + y_ref[0, :, p * d:(p + 1) * d]
    x = x_ref[0] + mod_ref[0, 5:6, :] * y
    o_ref[0] = x * lax.rsqrt(jnp.mean(x * x, axis=-1, keepdims=True) + EPS) * g_ref[...]


def final_tc(x1, y_parts, mod_l, final_g):
    b, l, d = x1.shape
    tok = pl.BlockSpec((1, TOK_TILE, d), lambda bi, i: (bi, i, 0))
    return pl.pallas_call(
        _final_kernel, grid=(b, l // TOK_TILE),
        in_specs=[tok, pl.BlockSpec((1, TOK_TILE, y_parts.shape[2]), lambda bi, i: (bi, i, 0)),
                  pl.BlockSpec((1,) + mod_l.shape[1:], lambda bi, i: (bi, 0, 0)),
                  pl.BlockSpec((1, d), lambda bi, i: (0, 0))],
        out_specs=tok, out_shape=jax.ShapeDtypeStruct((b, l, d), F32),
        compiler_params=_cparams("parallel", "arbitrary"), name="final_norm",
    )(x1, y_parts, mod_l, final_g.reshape(1, d))


def _pad_cols(w, n):
    return jnp.pad(w, ((0, 0), (0, n - w.shape[1])))


def _slice_sizes(b):
    if b % SLICE_BATCH or SLICE_BATCH % 2 or b < 2 * SLICE_BATCH:
        return [1] * b
    half = SLICE_BATCH // 2
    return [half] + [SLICE_BATCH] * (b // SLICE_BATCH - 1) + [half]


def _exact_zero(v):
    return jnp.minimum(jnp.abs(v), 0.0)


def forward(x, c, ctx, c_ctx, w_ada, b_ada, norm1_g, norm2_g, w_in, conv_w, dn_a_log,
            dn_dt_bias, dn_norm_g, gla_wa2, gla_ba, gla_norm_g, w_out, peer_wq, peer_keys,
            peer_u, peer_v, final_g):
    b, l, d = x.shape
    c_all = jnp.concatenate([c, c_ctx[None]], axis=0)
    c_all = jnp.pad(c_all, ((0, (-c_all.shape[0]) % SUBLANES), (0, 0)))
    mod = adaln_mod(c_all, w_ada, b_ada)
    mod_l = mod[:b].reshape(b, 6, d)
    mod_c = jnp.broadcast_to(mod[b].reshape(1, 6, d), (b, 6, d))
    o = DN_QKV
    hv = DN_HEADS * DN_DV
    w_dn_qkv, w_dn_z = w_in[:, :o], w_in[:, o:o + hv]
    w_dn_ba = _pad_cols(w_in[:, o + hv:DN_COLS], LANES)
    g0 = DN_COLS
    gqk, gv = 2 * GLA_HEADS * GLA_DK, GLA_HEADS * GLA_DV
    w_gl_qk, w_gl_v = w_in[:, g0:g0 + gqk], w_in[:, g0 + gqk:g0 + gqk + gv]
    w_gl_r = w_in[:, g0 + gqk + gv:g0 + gqk + 2 * gv]
    w_gl_lr = _pad_cols(w_in[:, g0 + gqk + 2 * gv:], LANES)
    w_lat = [w.astype(BF16) for w in (w_dn_qkv, w_dn_ba, w_gl_qk, w_gl_v, w_gl_lr, w_dn_z, w_gl_r)]
    w_ctx = w_lat[:5]
    w_la = jnp.zeros((2, LANES, GLA_HEADS * GLA_DK), F32)
    for dd in range(2):
        w_la = w_la.at[dd, dd * GLA_LR:(dd + 1) * GLA_LR].set(gla_wa2[dd])
    w_la = w_la.astype(BF16)
    b_la = gla_ba.reshape(2, 1, GLA_HEADS * GLA_DK)
    w_out_b, w_q_b = w_out.astype(BF16), peer_wq.astype(BF16)
    u_q, v_p = pack_pairs(peer_u), pack_pairs(peer_v)

    def mixer(xg, ctxg, mod_cg, mod_g, after_select, after_combine):
        mod_cg, mod_g = mod_cg + after_select, mod_g + after_select
        c_qkv, c_ba, c_qk, c_v, c_lr = in_projection(ctxg, norm1_g, mod_cg[:, 0:2], w_ctx)
        l_qkv, l_ba, l_qk, l_v, l_lr, l_z, l_r = in_projection(xg, norm1_g, mod_g[:, 0:2], w_lat)
        feat_c, feat_l = dn_features_tc(c_qkv, conv_w), dn_features_tc(l_qkv, conv_w)
        dn_f = dn_scan_tc(feat_c, feat_l, c_ba, l_ba, dn_a_log, dn_dt_bias, rev=False)
        dn_b = dn_scan_tc(feat_c, feat_l, c_ba, l_ba, dn_a_log, dn_dt_bias, rev=True)
        gl_f = gla_scan_tc(c_qk, c_v, c_lr, l_qk, l_v, l_lr, w_la, b_la, rev=False)
        gl_b = gla_scan_tc(c_qk, c_v, c_lr, l_qk, l_v, l_lr, w_la, b_la, rev=True)
        return mix_out_tc(xg, dn_f, dn_b, l_z, gl_f, gl_b, l_r, mod_g, dn_norm_g + after_combine,
                          gla_norm_g, norm2_g, w_out_b, w_q_b)

    def select_and_act(h2, q, after_coef):
        n_tok = h2.shape[0] * l
        idx, gate = peer_select_tc(q.reshape(n_tok, -1), peer_keys + after_coef)
        idx2 = idx.reshape(n_tok * GROUPS_PER_TOK, PICK_GROUP)
        gate2 = gate.reshape(n_tok * GROUPS_PER_TOK, PICK_GROUP)
        return idx2, gate2, peer_act_partial_sc(u_q, idx2, pack_pairs(h2.reshape(n_tok, d)))

    def coef_and_combine(st, after_mixer):
        coef_b = peer_coef_tc(st["part"], st["gate2"] + after_mixer)
        st["y"] = peer_combine_sc(v_p, st["idx2"], coef_b)
        return _exact_zero(coef_b[0, 0].astype(F32))

    zero = jnp.zeros((), F32)
    slices, z_sel, z_coef = [], zero, zero
    i = 0
    for bg in _slice_sizes(b):
        g = len(slices)
        z_comb = _exact_zero(slices[g - 2]["y"][0, 0]) if g >= 2 else zero
        mod_g = mod_l[i:i + bg]
        x1, h2, q = mixer(x[i:i + bg], ctx[i:i + bg], mod_c[i:i + bg], mod_g, z_sel, z_comb)
        i += bg
        if g >= 1:
            z_coef = coef_and_combine(slices[g - 1], _exact_zero(x1[0, 0, 0]))
        idx2, gate2, part = select_and_act(h2, q, z_coef)
        z_sel = _exact_zero(gate2[0, 0])
        slices.append(dict(x1=x1, mod=mod_g, idx2=idx2, gate2=gate2, part=part))
    coef_and_combine(slices[-1], zero)
    outs = [final_tc(st["x1"], st["y"].reshape(st["x1"].shape[:2] + (-1,)), st["mod"], final_g) for st in slices]
    return jnp.concatenate(outs, axis=0)


def kernel(x, c, ctx, c_ctx, w_ada, b_ada, norm1_g, norm2_g, w_in, conv_w, dn_a_log,
           dn_dt_bias, dn_norm_g, gla_wa2, gla_ba, gla_norm_g, w_out, peer_wq, peer_keys,
           peer_u, peer_v, final_g):
    assert w_ada.shape[0] == 1, "single-layer block: the context stream is only consumed, never updated"
    return forward(x, c, ctx, c_ctx, w_ada[0], b_ada[0], norm1_g[0], norm2_g[0], w_in[0], conv_w[0],
                   dn_a_log[0], dn_dt_bias[0], dn_norm_g[0], gla_wa2[0], gla_ba[0], gla_norm_g[0],
                   w_out[0], peer_wq[0], peer_keys[0], peer_u[0], peer_v[0], final_g)
```

```python
import functools

import jax
import jax.numpy as jnp
from jax import lax
from jax.experimental import pallas as pl
from jax.experimental.pallas import tpu as pltpu
from jax.experimental.pallas import tpu_sc as plsc

GRID_W = 64
DN_HEADS = 4
DN_DK = 128
DN_DV = 128
CONV_W = 5
GLA_HEADS = 4
GLA_DK = 64
GLA_DV = 128
GLA_LR = 16
GLA_TAU = 16.0
CHUNK = 64
PEER_HEADS = 8
PEER_NKEYS = 128
PEER_DQ = 256
PEER_TOPK = 16
EPS = 1e-6
DN_QKV = 2 * DN_HEADS * DN_DK + DN_HEADS * DN_DV
DN_COLS = DN_QKV + DN_HEADS * DN_DV + 4 * DN_HEADS

SUBLANES = 8
LANES = 128
SC_LANES = 16
VMEM_LIMIT_BYTES = 48 * 1024 * 1024

TOK_TILE = 256
SELECT_TILE = 256
GLA_SUB = 16
PICK_GROUP = 32
COMB_GROUP_STEP = 16
TOK_STEP = 8
COMB_BF16_TERMS = 4
COMB_CHUNKS = 8
ACT_BF16_TERMS = 4
ACT_UNROLL = 32
SLICE_BATCH = 2
PICKS = PEER_HEADS * PEER_TOPK
GROUPS_PER_TOK = PICKS // PICK_GROUP

F32 = jnp.float32
BF16 = jnp.bfloat16


def _cparams(*semantics):
    return pltpu.CompilerParams(dimension_semantics=semantics, vmem_limit_bytes=VMEM_LIMIT_BYTES)


def _dot(a, b):
    return jnp.dot(a.astype(BF16), b.astype(BF16), preferred_element_type=F32)


def _dot_nt(a, b):
    return lax.dot_general(a.astype(BF16), b.astype(BF16), (((1,), (1,)), ((), ())),
                           preferred_element_type=F32)


def _dot_tn(a, b):
    return lax.dot_general(a.astype(BF16), b.astype(BF16), (((0,), (0,)), ((), ())),
                           preferred_element_type=F32)


def _split(x):
    hi = x.astype(BF16)
    return hi, (x - hi.astype(F32)).astype(BF16)


def _mask_dot(mask_bf16, x):
    hi, lo = _split(x)
    return (jnp.dot(mask_bf16, hi, preferred_element_type=F32)
            + jnp.dot(mask_bf16, lo, preferred_element_type=F32))


def _softplus(x):
    return jnp.maximum(x, 0.0) + jnp.log(1.0 + jnp.exp(-jnp.abs(x)))


def _tri_masks(rev):
    r = lax.broadcasted_iota(jnp.int32, (CHUNK, CHUNK), 0)
    c = lax.broadcasted_iota(jnp.int32, (CHUNK, CHUNK), 1)
    d = (c - r) if rev else (r - c)
    return d >= 0, d > 0


def _mod_kernel(c_ref, w_ref, b_ref, o_ref):
    c = c_ref[...]
    s = c * jax.nn.sigmoid(c)
    o_ref[...] = jnp.dot(s, w_ref[...], preferred_element_type=F32,
                         precision=lax.Precision.HIGHEST) + b_ref[...]


def adaln_mod(c_all, w_ada, b_ada):
    r, d = c_all.shape
    n = w_ada.shape[1]
    tn = 512
    return pl.pallas_call(
        _mod_kernel, grid=(n // tn,),
        in_specs=[pl.BlockSpec((r, d), lambda j: (0, 0)),
                  pl.BlockSpec((d, tn), lambda j: (0, j)),
                  pl.BlockSpec((1, tn), lambda j: (0, j))],
        out_specs=pl.BlockSpec((r, tn), lambda j: (0, j)),
        out_shape=jax.ShapeDtypeStruct((r, n), F32),
        compiler_params=_cparams("arbitrary"), name="adaln_mod",
    )(c_all, w_ada, b_ada.reshape(1, n))


def _inproj_kernel(x_ref, g_ref, mod_ref, *refs):
    n_out = len(refs) // 2
    x = x_ref[0]
    y = x * lax.rsqrt(jnp.mean(x * x, axis=-1, keepdims=True) + EPS) * g_ref[...]
    h = (y * (1.0 + mod_ref[0, 1:2, :]) + mod_ref[0, 0:1, :]).astype(BF16)
    for w_ref, o_ref in zip(refs[:n_out], refs[n_out:]):
        o_ref[0] = jnp.dot(h, w_ref[...], preferred_element_type=F32)


def in_projection(x, norm_g, mod, weights):
    b, l, d = x.shape
    w_specs = [pl.BlockSpec(w.shape, lambda bi, i: (0, 0)) for w in weights]
    o_specs = [pl.BlockSpec((1, TOK_TILE, w.shape[1]), lambda bi, i: (bi, i, 0)) for w in weights]
    return pl.pallas_call(
        _inproj_kernel, grid=(b, l // TOK_TILE),
        in_specs=[pl.BlockSpec((1, TOK_TILE, d), lambda bi, i: (bi, i, 0)),
                  pl.BlockSpec((1, d), lambda bi, i: (0, 0)),
                  pl.BlockSpec((1, 2, d), lambda bi, i: (bi, 0, 0))] + w_specs,
        out_specs=o_specs,
        out_shape=[jax.ShapeDtypeStruct((b, l, w.shape[1]), F32) for w in weights],
        compiler_params=_cparams("parallel", "arbitrary"), name="in_projection",
    )(x, norm_g.reshape(1, d), mod, *weights)


def _dn_feature_kernel(x_ref, w_ref, o_ref):
    x = x_ref[0]
    n = x.shape[0]
    t = lax.broadcasted_iota(jnp.int32, (n, 1), 0)
    pad = CONV_W // 2
    acc = w_ref[0, pad:pad + 1, :] * x
    for j in range(CONV_W):
        s = j - pad
        if s == 0:
            continue
        xs = pltpu.roll(x, (-s) % n, axis=0)
        bad = (t < -s) if s < 0 else (t >= n - s)
        acc = acc + w_ref[0, j:j + 1, :] * jnp.where(bad, 0.0, xs)
    y = acc * jax.nn.sigmoid(acc)
    kind = pl.program_id(1) // DN_HEADS
    inv = lax.rsqrt(jnp.sum(y * y, axis=-1, keepdims=True) + EPS)
    scale = jnp.where(kind == 0, inv * DN_DK ** -0.5, jnp.where(kind == 1, inv, 1.0))
    o_ref[0] = y * scale


def dn_features_tc(qkv, conv_w):
    b, l, n = qkv.shape
    nblk = n // LANES
    w = jnp.zeros((nblk, SUBLANES, LANES), F32).at[:, :CONV_W].set(
        conv_w.reshape(CONV_W, nblk, LANES).transpose(1, 0, 2))
    return pl.pallas_call(
        _dn_feature_kernel, grid=(b, nblk),
        in_specs=[pl.BlockSpec((1, l, LANES), lambda bi, j: (bi, 0, j)),
                  pl.BlockSpec((1, SUBLANES, LANES), lambda bi, j: (j, 0, 0))],
        out_specs=pl.BlockSpec((1, l, LANES), lambda bi, j: (bi, 0, j)),
        out_shape=jax.ShapeDtypeStruct((b, l, n), F32),
        compiler_params=_cparams("parallel", "arbitrary"), name="dn_features",
    )(qkv, w)


def _scan_chunks(rev, n_ctx, n_lat):
    if rev:
        ctx = lambda j: jnp.maximum(n_ctx - 1 - j, 0)
        lat = lambda j: jnp.where(j < n_ctx, n_lat - 1, n_lat - 1 - (j - n_ctx))
    else:
        ctx = lambda j: jnp.minimum(j, n_ctx - 1)
        lat = lambda j: jnp.maximum(j - n_ctx, 0)
    return ctx, lat


def _dn_scan_kernel(rev, dirn, n_ctx, alog_ref, dtb_ref, fc_ref, fl_ref, bac_ref, bal_ref, o_ref, s_ref):
    step = pl.program_id(1)

    @pl.when(step == 0)
    def _():
        s_ref[...] = jnp.zeros_like(s_ref)

    in_ctx = step < n_ctx
    f = jnp.where(in_ctx, fc_ref[0], fl_ref[0])
    ba = jnp.where(in_ctx, bac_ref[0], bal_ref[0])
    nh, hd, n = DN_HEADS, DN_HEADS * DN_DK, DN_HEADS * CHUNK
    stack = lambda base, w: jnp.concatenate([f[:, base + h * w:base + (h + 1) * w] for h in range(nh)], axis=0)
    q_s, k_s, v_s = stack(0, DN_DK), stack(hd, DN_DK), stack(2 * hd, DN_DV)
    incl, _ = _tri_masks(rev)
    beta_all = jax.nn.sigmoid(ba)
    g_all = -jnp.exp(alog_ref[...]) * _softplus(ba + dtb_ref[...])
    gc_all = _mask_dot(incl.astype(BF16), g_all)
    gc_t = jnp.concatenate([gc_all, gc_all], axis=0).T
    g_tot = jnp.sum(g_all, axis=0, keepdims=True)
    cb = [dirn * nh + h for h in range(nh)]
    cg = [2 * nh + c for c in cb]
    col = lambda a, cs: jnp.concatenate([a[:, c:c + 1] for c in cs], axis=0)
    beta_c, gc_c = col(beta_all, cb), col(gc_all, cg)
    gtot_c = jnp.concatenate([jnp.broadcast_to(g_tot[:, c:c + 1], (CHUNK, 1)) for c in cg], axis=0)
    gc_r = jnp.concatenate([gc_t[c:c + 1, :CHUNK] for c in cg], axis=1)
    r = lax.broadcasted_iota(jnp.int32, (n, n), 0)
    c = lax.broadcasted_iota(jnp.int32, (n, n), 1)
    same = (r // CHUNK) == (c // CHUNK)
    d = (c - r) if rev else (r - c)
    incl_bd, strict_bd = same & (d >= 0), same & (d > 0)
    eye = (r == c).astype(F32)
    decay = jnp.where(incl_bd, jnp.exp(jnp.where(incl_bd, gc_c - gc_r, 0.0)), 0.0)
    kb_s = k_s * beta_c
    lower = jnp.where(strict_bd, _dot_nt(kb_s, k_s) * decay, 0.0)
    eg_c = jnp.exp(gc_c)
    inv = eye - lower
    pw = lower
    for _ in range(5):
        pw = _dot(pw, pw)
        inv = inv + _dot(inv, pw)
    sol = _dot(inv, jnp.concatenate([v_s * beta_c, kb_s * eg_c], axis=-1))
    u_s, w_s = sol[:, :DN_DV], sol[:, DN_DV:]
    k_dec = k_s * jnp.exp(gtot_c - gc_c)
    rb = lax.broadcasted_iota(jnp.int32, (n, DN_DK), 0) // CHUNK
    expand = lambda x: jnp.concatenate([jnp.where(rb == h, x, 0.0) for h in range(nh)], axis=1)
    s = s_ref[...]
    v_new = u_s - _dot(expand(w_s), s)
    a_qk = _dot_nt(q_s, k_s) * decay
    o_s = _dot(expand(q_s * eg_c), s) + _dot(a_qk, v_new)
    gl_rows = jnp.concatenate([jnp.broadcast_to(jnp.exp(g_tot[:, cc:cc + 1]), (DN_DK, 1)) for cc in cg], axis=0)
    s_ref[...] = s * gl_rows + _dot_tn(expand(k_dec), v_new)
    o_ref[0] = jnp.concatenate([o_s[h * CHUNK:(h + 1) * CHUNK] for h in range(nh)], axis=1)


def dn_scan_tc(feat_c, feat_l, ba_c, ba_l, a_log, dt_bias, rev):
    b, l, nf = feat_l.shape
    n_ctx, n_lat = feat_c.shape[1] // CHUNK, l // CHUNK
    dirn = 1 if rev else 0
    cc, lc = _scan_chunks(rev, n_ctx, n_lat)
    lanes = lambda p: jnp.zeros((1, LANES), F32).at[0, 2 * DN_HEADS:4 * DN_HEADS].set(p.reshape(-1))
    vec = pl.BlockSpec((1, LANES), lambda bi, j: (0, 0))
    return pl.pallas_call(
        functools.partial(_dn_scan_kernel, rev, dirn, n_ctx), grid=(b, n_ctx + n_lat),
        in_specs=[vec, vec,
                  pl.BlockSpec((1, CHUNK, nf), lambda bi, j: (bi, cc(j), 0)),
                  pl.BlockSpec((1, CHUNK, nf), lambda bi, j: (bi, lc(j), 0)),
                  pl.BlockSpec((1, CHUNK, LANES), lambda bi, j: (bi, cc(j), 0)),
                  pl.BlockSpec((1, CHUNK, LANES), lambda bi, j: (bi, lc(j), 0))],
        out_specs=pl.BlockSpec((1, CHUNK, DN_HEADS * DN_DV), lambda bi, j: (bi, lc(j), 0)),
        out_shape=jax.ShapeDtypeStruct((b, l, DN_HEADS * DN_DV), F32),
        scratch_shapes=[pltpu.VMEM((DN_HEADS * DN_DK, DN_DV), F32)],
        compiler_params=_cparams("parallel", "arbitrary"), name="dn_scan_bwd" if rev else "dn_scan_fwd",
    )(lanes(a_log), lanes(dt_bias), feat_c, feat_l, ba_c, ba_l)


def _from_grid_cols(blk, n):
    cols = blk.shape[1] // n
    return jnp.concatenate([blk[:, i * n:(i + 1) * n] for i in range(cols)], axis=0)


def _gla_scan_kernel(rev, dirn, n_ctx, qkc_ref, vc_ref, lrc_ref, qkl_ref, vl_ref, lrl_ref,
                     wla_ref, bla_ref, o_ref, s_ref):
    step = pl.program_id(1)

    @pl.when(step == 0)
    def _():
        s_ref[...] = jnp.zeros_like(s_ref)

    in_ctx = step < n_ctx
    hk, hv = GLA_HEADS * GLA_DK, GLA_HEADS * GLA_DV
    qk = jnp.where(in_ctx, qkc_ref[0], _from_grid_cols(qkl_ref[0], 2 * hk))
    vv = jnp.where(in_ctx, vc_ref[0], _from_grid_cols(vl_ref[0], hv))
    lr = jnp.where(in_ctx, lrc_ref[0], _from_grid_cols(lrl_ref[0], LANES))
    incl, _ = _tri_masks(rev)
    incl_b = incl.astype(BF16)
    pre = _dot(lr, wla_ref[0]) + bla_ref[0]
    la_all = -_softplus(-pre) * (1.0 / GLA_TAU)
    bc_all = _mask_dot(incl_b, la_all)
    b_tot_all = jnp.sum(la_all, axis=0, keepdims=True)
    outs = []
    for h in range(GLA_HEADS):
        q = qk[:, h * GLA_DK:(h + 1) * GLA_DK] * GLA_DK ** -0.5
        k = qk[:, hk + h * GLA_DK:hk + (h + 1) * GLA_DK]
        v = vv[:, h * GLA_DV:(h + 1) * GLA_DV]
        bc = bc_all[:, h * GLA_DK:(h + 1) * GLA_DK]
        b_tot = b_tot_all[:, h * GLA_DK:(h + 1) * GLA_DK]
        st = s_ref[h]
        o = _dot_nt(q * jnp.exp(bc), st)
        parts = []
        for i in range(CHUNK // GLA_SUB):
            lo_r, hi_r = i * GLA_SUB, (i + 1) * GLA_SUB
            if rev:
                ref = bc[hi_r - 1:hi_r]
                c0, c1 = lo_r, CHUNK
            else:
                ref = bc[lo_r:lo_r + 1]
                c0, c1 = 0, hi_r
            qi = q[lo_r:hi_r] * jnp.exp(bc[lo_r:hi_r] - ref)
            ki = k[c0:c1] * jnp.exp(ref - bc[c0:c1])
            att = _dot_nt(qi, ki)
            rg = lax.broadcasted_iota(jnp.int32, (GLA_SUB, c1 - c0), 0) + lo_r
            cg = lax.broadcasted_iota(jnp.int32, (GLA_SUB, c1 - c0), 1) + c0
            keep = (cg > rg) if rev else (cg < rg)
            parts.append(_dot(jnp.where(keep, att, 0.0), v[c0:c1]))
        diag = jnp.sum(q * k, axis=-1, keepdims=True) * v
        outs.append(o + jnp.concatenate(parts, axis=0) + diag)
        k_dec = k * jnp.exp(b_tot - bc)
        s_ref[h] = st * jnp.exp(b_tot) + _dot_tn(v, k_dec)
    o = jnp.concatenate(outs, axis=-1)
    rows = o_ref.shape[1]
    o_ref[0] = jnp.concatenate([o[i * rows:(i + 1) * rows] for i in range(CHUNK // rows)], axis=-1)


def gla_scan_tc(qk_c, v_c, lr_c, qk_l, v_l, lr_l, w_la, b_la, rev):
    b, l, _ = qk_l.shape
    rows = l // GRID_W
    cols = CHUNK // rows
    n_ctx, n_lat = qk_c.shape[1] // CHUNK, l // CHUNK
    dirn = 1 if rev else 0
    cc, lc = _scan_chunks(rev, n_ctx, n_lat)
    hv = GLA_HEADS * GLA_DV
    ctx_blk = lambda a: pl.BlockSpec((1, CHUNK, a.shape[2]), lambda bi, j: (bi, cc(j), 0))
    lat_blk = lambda n: pl.BlockSpec((1, rows, cols * n), lambda bi, j: (bi, 0, lc(j)))
    grid_view = lambda a: a.reshape(b, rows, GRID_W * a.shape[2])
    out = pl.pallas_call(
        functools.partial(_gla_scan_kernel, rev, dirn, n_ctx), grid=(b, n_ctx + n_lat),
        in_specs=[ctx_blk(qk_c), ctx_blk(v_c), ctx_blk(lr_c),
                  lat_blk(qk_l.shape[2]), lat_blk(v_l.shape[2]), lat_blk(lr_l.shape[2]),
                  pl.BlockSpec((1,) + w_la.shape[1:], lambda bi, j: (dirn, 0, 0)),
                  pl.BlockSpec((1,) + b_la.shape[1:], lambda bi, j: (dirn, 0, 0))],
        out_specs=lat_blk(hv),
        out_shape=jax.ShapeDtypeStruct((b, rows, GRID_W * hv), F32),
        scratch_shapes=[pltpu.VMEM((GLA_HEADS, GLA_DV, GLA_DK), F32)],
        compiler_params=_cparams("parallel", "arbitrary"), name="gla_scan_bwd" if rev else "gla_scan_fwd",
    )(qk_c, v_c, lr_c, grid_view(qk_l), grid_view(v_l), grid_view(lr_l), w_la, b_la)
    return out.reshape(b, l, hv)


def _head_norm_gate(o, gate, g, n_heads, dv):
    parts = []
    for h in range(n_heads):
        oh = o[:, h * dv:(h + 1) * dv]
        gh = gate[:, h * dv:(h + 1) * dv]
        yh = oh * lax.rsqrt(jnp.mean(oh * oh, axis=-1, keepdims=True) + EPS) * g
        parts.append(yh * (gh * jax.nn.sigmoid(gh)))
    return parts


def _mix_out_kernel(x_ref, dnf_ref, dnb_ref, z_ref, glf_ref, glb_ref, r_ref, mod_ref, dng_ref,
                    glg_ref, n2g_ref, wout_ref, wq_ref, x1_ref, h2_ref, q_ref):
    parts = (_head_norm_gate(dnf_ref[0] + dnb_ref[0], z_ref[0], dng_ref[...], DN_HEADS, DN_DV)
             + _head_norm_gate(glf_ref[0] + glb_ref[0], r_ref[0], glg_ref[...], GLA_HEADS, GLA_DV))
    y = jnp.dot(jnp.concatenate(parts, axis=-1).astype(BF16), wout_ref[...], preferred_element_type=F32)
    x1 = x_ref[0] + mod_ref[0, 2:3, :] * y
    x1_ref[0] = x1
    n = x1 * lax.rsqrt(jnp.mean(x1 * x1, axis=-1, keepdims=True) + EPS) * n2g_ref[...]
    h2 = n * (1.0 + mod_ref[0, 4:5, :]) + mod_ref[0, 3:4, :]
    h2_ref[0] = h2
    h_hi, h_lo = _split(h2)
    q_ref[0] = (jnp.dot(h_hi, wq_ref[0], preferred_element_type=F32)
                + jnp.dot(h_lo, wq_ref[0], preferred_element_type=F32)
                + jnp.dot(h_hi, wq_ref[1], preferred_element_type=F32))


def mix_out_tc(x, dn_f, dn_b, z, gl_f, gl_b, r, mod_l, dn_g, gla_g, n2_g, w_out, w_q):
    b, l, d = x.shape
    tok = lambda n: pl.BlockSpec((1, TOK_TILE, n), lambda bi, i: (bi, i, 0))
    full = lambda a: pl.BlockSpec(a.shape, lambda bi, i: (0,) * a.ndim)
    dn_g, gla_g, n2_g = dn_g.reshape(1, -1), gla_g.reshape(1, -1), n2_g.reshape(1, -1)
    nq = w_q.shape[2]
    return pl.pallas_call(
        _mix_out_kernel, grid=(b, l // TOK_TILE),
        in_specs=[tok(d), tok(dn_f.shape[2]), tok(dn_b.shape[2]), tok(z.shape[2]),
                  tok(gl_f.shape[2]), tok(gl_b.shape[2]), tok(r.shape[2]),
                  pl.BlockSpec((1,) + mod_l.shape[1:], lambda bi, i: (bi, 0, 0)),
                  full(dn_g), full(gla_g), full(n2_g), full(w_out), full(w_q)],
        out_specs=[tok(d), tok(d), tok(nq)],
        out_shape=[jax.ShapeDtypeStruct((b, l, d), F32), jax.ShapeDtypeStruct((b, l, d), F32),
                   jax.ShapeDtypeStruct((b, l, nq), F32)],
        compiler_params=_cparams("parallel", "arbitrary"), name="mix_out",
    )(x, dn_f, dn_b, z, gl_f, gl_b, r, mod_l, dn_g, gla_g, n2_g, w_out, w_q)


def _top_rows(s, k, payload=None):
    n = s.shape[0]
    row = lax.broadcasted_iota(jnp.int32, s.shape, 0).astype(F32)
    vals, picked = [], []
    for _ in range(k):
        m = jnp.max(s, axis=0, keepdims=True)
        first = jnp.min(jnp.where(s == m, row, float(n)), axis=0, keepdims=True)
        sel = row == first
        vals.append(m)
        if payload is None:
            picked.append(first)
        else:
            picked.append(jnp.max(jnp.where(sel, payload, -1.0), axis=0, keepdims=True))
        s = jnp.where(sel, -jnp.inf, s)
    return jnp.concatenate(vals, axis=0), jnp.concatenate(picked, axis=0)


def _candidate_rows(s0, i0, s1, i1):
    k = s0.shape[0]
    wide = SUBLANES
    blocks_s = [s0[0:1] + s1]
    blocks_i = [i0[0:1] * float(PEER_NKEYS) + i1]
    col = lax.broadcasted_iota(jnp.int32, (wide, s0.shape[1]), 0)
    for i in range(1, wide):
        keep = col < (k // (i + 1))
        blocks_s.append(jnp.where(keep, s0[i:i + 1] + s1[0:wide], -jnp.inf))
        blocks_i.append(i0[i:i + 1] * float(PEER_NKEYS) + i1[0:wide])
    blocks_s.append(s0[wide:k] + s1[0:1])
    blocks_i.append(i0[wide:k] * float(PEER_NKEYS) + i1[0:1])
    return jnp.concatenate(blocks_s, axis=0), jnp.concatenate(blocks_i, axis=0)


def _select_kernel(q_ref, k_ref, idx_ref, gate_ref, idx_s, gate_s):
    half = PEER_DQ // 2

    def head(h, carry):
        tops = []
        for p in range(2):
            qp = q_ref[:, pl.ds(pl.multiple_of(h * PEER_DQ + p * half, half), half)]
            s = lax.dot_general(k_ref[h, p], qp, (((1,), (1,)), ((), ())),
                                preferred_element_type=F32,
                                precision=lax.Precision.HIGHEST)
            tops.append(_top_rows(s, PEER_TOPK))
        (s0, i0), (s1, i1) = tops
        cand_s, cand_i = _candidate_rows(s0, i0, s1, i1)
        best_s, idx = _top_rows(cand_s, PEER_TOPK, payload=cand_i)
        e = jnp.exp(best_s - best_s[0:1])
        r0 = pl.multiple_of(h * PEER_TOPK, PEER_TOPK)
        idx_s[pl.ds(r0, PEER_TOPK), :] = idx
        gate_s[pl.ds(r0, PEER_TOPK), :] = e / jnp.sum(e, axis=0, keepdims=True)
        return carry

    lax.fori_loop(0, PEER_HEADS, head, 0)
    idx_ref[...] = idx_s[...].T.astype(jnp.int32)
    gate_ref[...] = gate_s[...].T


def peer_select_tc(q, keys):
    n_tok = q.shape[0]
    out_spec = pl.BlockSpec((SELECT_TILE, PICKS), lambda i: (i, 0))
    return pl.pallas_call(
        _select_kernel,
        grid=(n_tok // SELECT_TILE,),
        in_specs=[pl.BlockSpec((SELECT_TILE, q.shape[1]), lambda i: (i, 0)),
                  pl.BlockSpec(keys.shape, lambda i: (0, 0, 0, 0))],
        out_specs=[out_spec, out_spec],
        out_shape=[jax.ShapeDtypeStruct((n_tok, PICKS), jnp.int32),
                   jax.ShapeDtypeStruct((n_tok, PICKS), F32)],
        scratch_shapes=[pltpu.VMEM((PICKS, SELECT_TILE), F32), pltpu.VMEM((PICKS, SELECT_TILE), F32)],
        compiler_params=_cparams("parallel"), name="peer_select",
    )(q, keys)


def _sc_mesh():
    return plsc.VectorSubcoreMesh(core_axis_name="c", subcore_axis_name="s")


def _sc_pipeline(body, n_steps, in_specs, out_specs, operands):
    pltpu.emit_pipeline(
        body, grid=(n_steps,), in_specs=in_specs, out_specs=out_specs,
        core_axis_name=("c", "s"), dimension_semantics=(pltpu.PARALLEL,),
        trace_scopes=False,
    )(*operands)


def pack_pairs(t):
    half = t.shape[1] // 2
    bits = lax.bitcast_convert_type(t, jnp.uint32)
    rne = lambda b: (b + jnp.uint32(0x7FFF) + ((b >> 16) & jnp.uint32(1))) >> 16
    word = (rne(bits[:, half:]) << 16) | rne(bits[:, :half])
    return lax.bitcast_convert_type(word, jnp.int32)


def peer_act_partial_sc(u_q, idx2, h_q):
    n_groups, (n_tok, half) = idx2.shape[0], h_q.shape
    nsub = TOK_STEP * GROUPS_PER_TOK
    n_chunks = half // SC_LANES
    as_pairs = lambda w: plsc.bitcast(w, BF16)

    @functools.partial(
        pl.kernel, mesh=_sc_mesh(), compiler_params=pltpu.CompilerParams(needs_layout_passes=False),
        out_type=jax.ShapeDtypeStruct((n_groups, PICK_GROUP * SC_LANES), F32),
        scratch_types=[pltpu.VMEM((2, PICK_GROUP, half), jnp.int32),
                       pltpu.SemaphoreType.DMA((2,))],
    )
    def k(u_hbm, i_hbm, h_hbm, o_hbm, rows, sems):
        def body(i_v, h_v, o_v):
            def fetch(j, slot):
                return pltpu.make_async_copy(u_hbm.at[i_v.at[j]], rows.at[slot], sems.at[slot])

            fetch(0, 0).start()

            def sub(j, carry):
                slot = j % 2

                @pl.when(j + 1 < nsub)
                def _():
                    fetch(j + 1, 1 - slot).start()

                fetch(j, slot).wait()
                t = j // GROUPS_PER_TOK

                def picks(g, carry2):
                    kb = g * ACT_UNROLL
                    accs = [None] * ACT_UNROLL
                    for c0 in range(0, n_chunks, ACT_BF16_TERMS):
                        cols = [pl.ds((c0 + cc) * SC_LANES, SC_LANES) for cc in range(ACT_BF16_TERMS)]
                        hs = [as_pairs(h_v[t, col]) for col in cols]
                        for i in range(ACT_UNROLL):
                            s = None
                            for col, hv in zip(cols, hs):
                                p = as_pairs(rows[slot, kb + i, col]) * hv
                                s = p if s is None else s + p
                            lo, hi = plsc.unpack(s, format=plsc.PackFormat.INTERLEAVED)
                            p32 = lo + hi
                            accs[i] = p32 if accs[i] is None else accs[i] + p32
                    for i in range(ACT_UNROLL):
                        o_v[j, pl.ds((kb + i) * SC_LANES, SC_LANES)] = accs[i]
                    return carry2

                lax.fori_loop(0, PICK_GROUP // ACT_UNROLL, picks, 0)
                return carry

            lax.fori_loop(0, nsub, sub, 0)

        _sc_pipeline(
            body, n_tok // TOK_STEP,
            [pl.BlockSpec((nsub, PICK_GROUP), lambda i: (i, 0)),
             pl.BlockSpec((TOK_STEP, half), lambda i: (i, 0))],
            [pl.BlockSpec((nsub, PICK_GROUP * SC_LANES), lambda i: (i, 0))],
            (i_hbm, h_hbm, o_hbm))

    return k(u_q, idx2, h_q)


def peer_combine_sc(v_q, idx2, coef_q):
    n_groups = idx2.shape[0]
    half = v_q.shape[1]
    nsub = COMB_GROUP_STEP
    n_chunks = half // SC_LANES
    as_pairs = lambda w: plsc.bitcast(w, BF16)

    @functools.partial(
        pl.kernel, mesh=_sc_mesh(), compiler_params=pltpu.CompilerParams(needs_layout_passes=False),
        out_type=jax.ShapeDtypeStruct((n_groups, 2 * half), F32),
        scratch_types=[pltpu.VMEM((2, PICK_GROUP, half), jnp.int32),
                       pltpu.SemaphoreType.DMA((2,))],
    )
    def k(v_hbm, i_hbm, c_hbm, o_hbm, rows, sems):
        def body(i_v, c_v, o_v):
            def fetch(j, slot):
                return pltpu.make_async_copy(v_hbm.at[i_v.at[j]], rows.at[slot], sems.at[slot])

            fetch(0, 0).start()

            def sub(j, carry0):
                slot = j % 2

                @pl.when(j + 1 < nsub)
                def _():
                    fetch(j + 1, 1 - slot).start()

                fetch(j, slot).wait()
                cks = [as_pairs(c_v[j, pl.ds(kk * SC_LANES, SC_LANES)]) for kk in range(PICK_GROUP)]

                def chunk_pair(c2, carry):
                    sums = []
                    for cc in range(COMB_CHUNKS):
                        l = (c2 * COMB_CHUNKS + cc) * SC_LANES
                        tot_lo, tot_hi = None, None
                        for k0 in range(0, PICK_GROUP, COMB_BF16_TERMS):
                            s = None
                            for kk in range(k0, k0 + COMB_BF16_TERMS):
                                p = cks[kk] * as_pairs(rows[slot, kk, pl.ds(l, SC_LANES)])
                                s = p if s is None else s + p
                            lo, hi = plsc.unpack(s, format=plsc.PackFormat.INTERLEAVED)
                            tot_lo = lo if tot_lo is None else tot_lo + lo
                            tot_hi = hi if tot_hi is None else tot_hi + hi
                        sums.append((l, tot_lo, tot_hi))
                    for l, tot_lo, tot_hi in sums:
                        o_v[j, pl.ds(l, SC_LANES)] = tot_lo
                        o_v[j, pl.ds(half + l, SC_LANES)] = tot_hi
                    return carry

                lax.fori_loop(0, n_chunks // COMB_CHUNKS, chunk_pair, 0)
                return carry0

            lax.fori_loop(0, nsub, sub, 0)

        _sc_pipeline(
            body, n_groups // nsub,
            [pl.BlockSpec((nsub, PICK_GROUP), lambda i: (i, 0)),
             pl.BlockSpec((nsub, PICK_GROUP * SC_LANES), lambda i: (i, 0))],
            [pl.BlockSpec((nsub, 2 * half), lambda i: (i, 0))],
            (i_hbm, c_hbm, o_hbm))

    return k(v_q, idx2, coef_q)


def _segment_matrix():
    r = lax.broadcasted_iota(jnp.int32, (PICK_GROUP * SC_LANES, PICK_GROUP), 0) // SC_LANES
    c = lax.broadcasted_iota(jnp.int32, (PICK_GROUP * SC_LANES, PICK_GROUP), 1)
    return (r == c).astype(F32)


def _coef_kernel(part_ref, gate_ref, o_ref):
    seg = _segment_matrix()
    act = jnp.dot(part_ref[...], seg, preferred_element_type=F32, precision=lax.Precision.HIGHEST)
    coef = gate_ref[...] * (0.5 * act * (1.0 + lax.erf(act * (2.0 ** -0.5))))
    wide = lax.dot_general(coef, seg, (((1,), (1,)), ((), ())), preferred_element_type=F32,
                           precision=lax.Precision.HIGHEST)
    bits = pltpu.bitcast(wide, jnp.uint32)
    r = (bits + jnp.uint32(0x7FFF) + ((bits >> 16) & jnp.uint32(1))) >> 16
    o_ref[...] = pltpu.bitcast(r | (r << 16), jnp.int32)


def peer_coef_tc(part, gate2):
    n_groups, width = part.shape
    tile = 1024
    return pl.pallas_call(
        _coef_kernel,
        grid=(n_groups // tile,),
        in_specs=[pl.BlockSpec((tile, width), lambda i: (i, 0)),
                  pl.BlockSpec((tile, PICK_GROUP), lambda i: (i, 0))],
        out_specs=pl.BlockSpec((tile, width), lambda i: (i, 0)),
        out_shape=jax.ShapeDtypeStruct((n_groups, width), jnp.int32),
        compiler_params=_cparams("parallel"), name="peer_coef",
    )(part, gate2)


def _final_kernel(x_ref, y_ref, mod_ref, g_ref, o_ref):
    d = x_ref.shape[2]
    y = y_ref[0, :, 0:d]
    for p in range(1, y_ref.shape[2] // d):
        y = y + y_ref[0, :, p * d:(p + 1) * d]
    x = x_ref[0] + mod_ref[0, 5:6, :] * y
    o_ref[0] = x * lax.rsqrt(jnp.mean(x * x, axis=-1, keepdims=True) + EPS) * g_ref[...]


def final_tc(x1, y_parts, mod_l, final_g):
    b, l, d = x1.shape
    tok = pl.BlockSpec((1, TOK_TILE, d), lambda bi, i: (bi, i, 0))
    return pl.pallas_call(
        _final_kernel, grid=(b, l // TOK_TILE),
        in_specs=[tok, pl.BlockSpec((1, TOK_TILE, y_parts.shape[2]), lambda bi, i: (bi, i, 0)),
                  pl.BlockSpec((1,) + mod_l.shape[1:], lambda bi, i: (bi, 0, 0)),
                  pl.BlockSpec((1, d), lambda bi, i: (0, 0))],
        out_specs=tok, out_shape=jax.ShapeDtypeStruct((b, l, d), F32),
        compiler_params=_cparams("parallel", "arbitrary"), name="final_norm",
    )(x1, y_parts, mod_l, final_g.reshape(1, d))


def _pad_cols(w, n):
    return jnp.pad(w, ((0, 0), (0, n - w.shape[1])))


def _slice_sizes(b):
    if b % SLICE_BATCH or SLICE_BATCH % 2 or b < 2 * SLICE_BATCH:
        return [1] * b
    half = SLICE_BATCH // 2
    return [half] + [SLICE_BATCH] * (b // SLICE_BATCH - 1) + [half]


def _exact_zero(v):
    return jnp.minimum(jnp.abs(v), 0.0)


def forward(x, c, ctx, c_ctx, w_ada, b_ada, norm1_g, norm2_g, w_in, conv_w, dn_a_log,
            dn_dt_bias, dn_norm_g, gla_wa2, gla_ba, gla_norm_g, w_out, peer_wq, peer_keys,
            peer_u, peer_v, final_g):
    b, l, d = x.shape
    c_all = jnp.concatenate([c, c_ctx[None]], axis=0)
    c_all = jnp.pad(c_all, ((0, (-c_all.shape[0]) % SUBLANES), (0, 0)))
    mod = adaln_mod(c_all, w_ada, b_ada)
    mod_l = mod[:b].reshape(b, 6, d)
    mod_c = jnp.broadcast_to(mod[b].reshape(1, 6, d), (b, 6, d))
    o = DN_QKV
    hv = DN_HEADS * DN_DV
    w_dn_qkv, w_dn_z = w_in[:, :o], w_in[:, o:o + hv]
    w_dn_ba = _pad_cols(w_in[:, o + hv:DN_COLS], LANES)
    g0 = DN_COLS
    gqk, gv = 2 * GLA_HEADS * GLA_DK, GLA_HEADS * GLA_DV
    w_gl_qk, w_gl_v = w_in[:, g0:g0 + gqk], w_in[:, g0 + gqk:g0 + gqk + gv]
    w_gl_r = w_in[:, g0 + gqk + gv:g0 + gqk + 2 * gv]
    w_gl_lr = _pad_cols(w_in[:, g0 + gqk + 2 * gv:], LANES)
    w_lat = [w.astype(BF16) for w in (w_dn_qkv, w_dn_ba, w_gl_qk, w_gl_v, w_gl_lr, w_dn_z, w_gl_r)]
    w_ctx = w_lat[:5]
    w_la = jnp.zeros((2, LANES, GLA_HEADS * GLA_DK), F32)
    for dd in range(2):
        w_la = w_la.at[dd, dd * GLA_LR:(dd + 1) * GLA_LR].set(gla_wa2[dd])
    w_la = w_la.astype(BF16)
    b_la = gla_ba.reshape(2, 1, GLA_HEADS * GLA_DK)
    w_out_b, w_q_b = w_out.astype(BF16), jnp.stack(_split(peer_wq))
    u_q, v_q = pack_pairs(peer_u), pack_pairs(peer_v)

    def mixer(xg, ctxg, mod_cg, mod_g, after_select, after_combine):
        mod_cg, mod_g = mod_cg + after_select, mod_g + after_select
        c_qkv, c_ba, c_qk, c_v, c_lr = in_projection(ctxg, norm1_g, mod_cg[:, 0:2], w_ctx)
        l_qkv, l_ba, l_qk, l_v, l_lr, l_z, l_r = in_projection(xg, norm1_g, mod_g[:, 0:2], w_lat)
        feat_c, feat_l = dn_features_tc(c_qkv, conv_w), dn_features_tc(l_qkv, conv_w)
        dn_f = dn_scan_tc(feat_c, feat_l, c_ba, l_ba, dn_a_log, dn_dt_bias, rev=False)
        dn_b = dn_scan_tc(feat_c, feat_l, c_ba, l_ba, dn_a_log, dn_dt_bias, rev=True)
        gl_f = gla_scan_tc(c_qk, c_v, c_lr, l_qk, l_v, l_lr, w_la, b_la, rev=False)
        gl_b = gla_scan_tc(c_qk, c_v, c_lr, l_qk, l_v, l_lr, w_la, b_la, rev=True)
        return mix_out_tc(xg, dn_f, dn_b, l_z, gl_f, gl_b, l_r, mod_g, dn_norm_g + after_combine,
                          gla_norm_g, norm2_g, w_out_b, w_q_b)

    def select_and_act(h2, q, after_coef):
        n_tok = h2.shape[0] * l
        idx, gate = peer_select_tc(q.reshape(n_tok, -1), peer_keys + after_coef)
        idx2 = idx.reshape(n_tok * GROUPS_PER_TOK, PICK_GROUP)
        gate2 = gate.reshape(n_tok * GROUPS_PER_TOK, PICK_GROUP)
        return idx2, gate2, peer_act_partial_sc(u_q, idx2, pack_pairs(h2.reshape(n_tok, d)))

    def coef_and_combine(st, after_mixer):
        coef_q = peer_coef_tc(st["part"], st["gate2"] + after_mixer)
        st["y"] = peer_combine_sc(v_q, st["idx2"], coef_q)
        return _exact_zero(coef_q[0, 0].astype(F32))

    zero = jnp.zeros((), F32)
    slices, z_sel, z_coef = [], zero, zero
    i = 0
    for bg in _slice_sizes(b):
        g = len(slices)
        z_comb = _exact_zero(slices[g - 2]["y"][0, 0]) if g >= 2 else zero
        mod_g = mod_l[i:i + bg]
        x1, h2, q = mixer(x[i:i + bg], ctx[i:i + bg], mod_c[i:i + bg], mod_g, z_sel, z_comb)
        i += bg
        if g >= 1:
            z_coef = coef_and_combine(slices[g - 1], _exact_zero(x1[0, 0, 0]))
        idx2, gate2, part = select_and_act(h2, q, z_coef)
        z_sel = _exact_zero(gate2[0, 0])
        slices.append(dict(x1=x1, mod=mod_g, idx2=idx2, gate2=gate2, part=part))
    coef_and_combine(slices[-1], zero)
    outs = [final_tc(st["x1"], st["y"].reshape(st["x1"].shape[:2] + (-1,)), st["mod"], final_g) for st in slices]
    return jnp.concatenate(outs, axis=0)


def kernel(x, c, ctx, c_ctx, w_ada, b_ada, norm1_g, norm2_g, w_in, conv_w, dn_a_log,
           dn_dt_bias, dn_norm_g, gla_wa2, gla_ba, gla_norm_g, w_out, peer_wq, peer_keys,
           peer_u, peer_v, final_g):
    assert w_ada.shape[0] == 1, "single-layer block: the context stream is only consumed, never updated"
    return forward(x, c, ctx, c_ctx, w_ada[0], b_ada[0], norm1_g[0], norm2_g[0], w_in[0], conv_w[0],
                   dn_a_log[0], dn_dt_bias[0], dn_norm_g[0], gla_wa2[0], gla_ba[0], gla_norm_g[0],
                   w_out[0], peer_wq[0], peer_keys[0], peer_u[0], peer_v[0], final_g)
```

```python
import functools

import jax
import jax.numpy as jnp
from jax import lax
from jax.experimental import pallas as pl
from jax.experimental.pallas import tpu as pltpu
from jax.experimental.pallas import tpu_sc as plsc

GRID_W = 64
DN_HEADS = 4
DN_DK = 128
DN_DV = 128
CONV_W = 5
GLA_HEADS = 4
GLA_DK = 64
GLA_DV = 128
GLA_LR = 16
GLA_TAU = 16.0
CHUNK = 64
PEER_HEADS = 8
PEER_NKEYS = 128
PEER_DQ = 256
PEER_TOPK = 16
EPS = 1e-6
DN_QKV = 2 * DN_HEADS * DN_DK + DN_HEADS * DN_DV
DN_COLS = DN_QKV + DN_HEADS * DN_DV + 4 * DN_HEADS

SUBLANES = 8
LANES = 128
SC_LANES = 16
VMEM_LIMIT_BYTES = 48 * 1024 * 1024

TOK_TILE = 256
SELECT_TILE = 256
GLA_SUB = 16
PICK_GROUP = 32
COMB_GROUP_STEP = 16
GATHER_BUFS = 3
TOK_STEP = 8
COMB_BF16_TERMS = 4
COMB_CHUNKS = 4
ACT_BF16_TERMS = 4
ACT_UNROLL = 16
SLICE_BATCH = 2
PICKS = PEER_HEADS * PEER_TOPK
GROUPS_PER_TOK = PICKS // PICK_GROUP

F32 = jnp.float32
BF16 = jnp.bfloat16


def _cparams(*semantics):
    return pltpu.CompilerParams(dimension_semantics=semantics, vmem_limit_bytes=VMEM_LIMIT_BYTES)


def _dot(a, b):
    return jnp.dot(a.astype(BF16), b.astype(BF16), preferred_element_type=F32)


def _dot_nt(a, b):
    return lax.dot_general(a.astype(BF16), b.astype(BF16), (((1,), (1,)), ((), ())),
                           preferred_element_type=F32)


def _dot_tn(a, b):
    return lax.dot_general(a.astype(BF16), b.astype(BF16), (((0,), (0,)), ((), ())),
                           preferred_element_type=F32)


def _split(x):
    hi = x.astype(BF16)
    return hi, (x - hi.astype(F32)).astype(BF16)


def _mask_dot(mask_bf16, x):
    hi, lo = _split(x)
    return (jnp.dot(mask_bf16, hi, preferred_element_type=F32)
            + jnp.dot(mask_bf16, lo, preferred_element_type=F32))


def _softplus(x):
    return jnp.maximum(x, 0.0) + jnp.log(1.0 + jnp.exp(-jnp.abs(x)))


def _tri_masks(rev):
    r = lax.broadcasted_iota(jnp.int32, (CHUNK, CHUNK), 0)
    c = lax.broadcasted_iota(jnp.int32, (CHUNK, CHUNK), 1)
    d = (c - r) if rev else (r - c)
    return d >= 0, d > 0


def _mod_kernel(c_ref, w_ref, b_ref, o_ref):
    c = c_ref[...]
    s = c * jax.nn.sigmoid(c)
    o_ref[...] = jnp.dot(s, w_ref[...], preferred_element_type=F32,
                         precision=lax.Precision.HIGHEST) + b_ref[...]


def adaln_mod(c_all, w_ada, b_ada):
    r, d = c_all.shape
    n = w_ada.shape[1]
    tn = 512
    return pl.pallas_call(
        _mod_kernel, grid=(n // tn,),
        in_specs=[pl.BlockSpec((r, d), lambda j: (0, 0)),
                  pl.BlockSpec((d, tn), lambda j: (0, j)),
                  pl.BlockSpec((1, tn), lambda j: (0, j))],
        out_specs=pl.BlockSpec((r, tn), lambda j: (0, j)),
        out_shape=jax.ShapeDtypeStruct((r, n), F32),
        compiler_params=_cparams("arbitrary"), name="adaln_mod",
    )(c_all, w_ada, b_ada.reshape(1, n))


def _inproj_kernel(x_ref, g_ref, mod_ref, *refs):
    n_out = len(refs) // 2
    x = x_ref[0]
    y = x * lax.rsqrt(jnp.mean(x * x, axis=-1, keepdims=True) + EPS) * g_ref[...]
    h = (y * (1.0 + mod_ref[0, 1:2, :]) + mod_ref[0, 0:1, :]).astype(BF16)
    for w_ref, o_ref in zip(refs[:n_out], refs[n_out:]):
        o_ref[0] = jnp.dot(h, w_ref[...], preferred_element_type=F32)


def in_projection(x, norm_g, mod, weights):
    b, l, d = x.shape
    w_specs = [pl.BlockSpec(w.shape, lambda bi, i: (0, 0)) for w in weights]
    o_specs = [pl.BlockSpec((1, TOK_TILE, w.shape[1]), lambda bi, i: (bi, i, 0)) for w in weights]
    return pl.pallas_call(
        _inproj_kernel, grid=(b, l // TOK_TILE),
        in_specs=[pl.BlockSpec((1, TOK_TILE, d), lambda bi, i: (bi, i, 0)),
                  pl.BlockSpec((1, d), lambda bi, i: (0, 0)),
                  pl.BlockSpec((1, 2, d), lambda bi, i: (bi, 0, 0))] + w_specs,
        out_specs=o_specs,
        out_shape=[jax.ShapeDtypeStruct((b, l, w.shape[1]), F32) for w in weights],
        compiler_params=_cparams("parallel", "arbitrary"), name="in_projection",
    )(x, norm_g.reshape(1, d), mod, *weights)


def _dn_feature_kernel(x_ref, w_ref, o_ref):
    x = x_ref[0]
    n = x.shape[0]
    t = lax.broadcasted_iota(jnp.int32, (n, 1), 0)
    pad = CONV_W // 2
    acc = w_ref[0, pad:pad + 1, :] * x
    for j in range(CONV_W):
        s = j - pad
        if s == 0:
            continue
        xs = pltpu.roll(x, (-s) % n, axis=0)
        bad = (t < -s) if s < 0 else (t >= n - s)
        acc = acc + w_ref[0, j:j + 1, :] * jnp.where(bad, 0.0, xs)
    y = acc * jax.nn.sigmoid(acc)
    kind = pl.program_id(1) // DN_HEADS
    inv = lax.rsqrt(jnp.sum(y * y, axis=-1, keepdims=True) + EPS)
    scale = jnp.where(kind == 0, inv * DN_DK ** -0.5, jnp.where(kind == 1, inv, 1.0))
    o_ref[0] = y * scale


def dn_features_tc(qkv, conv_w):
    b, l, n = qkv.shape
    nblk = n // LANES
    w = jnp.zeros((nblk, SUBLANES, LANES), F32).at[:, :CONV_W].set(
        conv_w.reshape(CONV_W, nblk, LANES).transpose(1, 0, 2))
    return pl.pallas_call(
        _dn_feature_kernel, grid=(b, nblk),
        in_specs=[pl.BlockSpec((1, l, LANES), lambda bi, j: (bi, 0, j)),
                  pl.BlockSpec((1, SUBLANES, LANES), lambda bi, j: (j, 0, 0))],
        out_specs=pl.BlockSpec((1, l, LANES), lambda bi, j: (bi, 0, j)),
        out_shape=jax.ShapeDtypeStruct((b, l, n), F32),
        compiler_params=_cparams("parallel", "arbitrary"), name="dn_features",
    )(qkv, w)


def _scan_chunks(rev, n_ctx, n_lat):
    if rev:
        ctx = lambda j: jnp.maximum(n_ctx - 1 - j, 0)
        lat = lambda j: jnp.where(j < n_ctx, n_lat - 1, n_lat - 1 - (j - n_ctx))
    else:
        ctx = lambda j: jnp.minimum(j, n_ctx - 1)
        lat = lambda j: jnp.maximum(j - n_ctx, 0)
    return ctx, lat


def _dn_scan_kernel(rev, dirn, n_ctx, alog_ref, dtb_ref, fc_ref, fl_ref, bac_ref, bal_ref, o_ref, s_ref):
    step = pl.program_id(1)

    @pl.when(step == 0)
    def _():
        s_ref[...] = jnp.zeros_like(s_ref)

    in_ctx = step < n_ctx
    f = jnp.where(in_ctx, fc_ref[0], fl_ref[0])
    ba = jnp.where(in_ctx, bac_ref[0], bal_ref[0])
    nh, hd, n = DN_HEADS, DN_HEADS * DN_DK, DN_HEADS * CHUNK
    stack = lambda base, w: jnp.concatenate([f[:, base + h * w:base + (h + 1) * w] for h in range(nh)], axis=0)
    q_s, k_s, v_s = stack(0, DN_DK), stack(hd, DN_DK), stack(2 * hd, DN_DV)
    incl, _ = _tri_masks(rev)
    beta_all = jax.nn.sigmoid(ba)
    g_all = -jnp.exp(alog_ref[...]) * _softplus(ba + dtb_ref[...])
    gc_all = _mask_dot(incl.astype(BF16), g_all)
    gc_t = jnp.concatenate([gc_all, gc_all], axis=0).T
    g_tot = jnp.sum(g_all, axis=0, keepdims=True)
    cb = [dirn * nh + h for h in range(nh)]
    cg = [2 * nh + c for c in cb]
    col = lambda a, cs: jnp.concatenate([a[:, c:c + 1] for c in cs], axis=0)
    beta_c, gc_c = col(beta_all, cb), col(gc_all, cg)
    gtot_c = jnp.concatenate([jnp.broadcast_to(g_tot[:, c:c + 1], (CHUNK, 1)) for c in cg], axis=0)
    gc_r = jnp.concatenate([gc_t[c:c + 1, :CHUNK] for c in cg], axis=1)
    r = lax.broadcasted_iota(jnp.int32, (n, n), 0)
    c = lax.broadcasted_iota(jnp.int32, (n, n), 1)
    same = (r // CHUNK) == (c // CHUNK)
    d = (c - r) if rev else (r - c)
    incl_bd, strict_bd = same & (d >= 0), same & (d > 0)
    eye = (r == c).astype(F32)
    decay = jnp.where(incl_bd, jnp.exp(jnp.where(incl_bd, gc_c - gc_r, 0.0)), 0.0)
    kb_s = k_s * beta_c
    lower = jnp.where(strict_bd, _dot_nt(kb_s, k_s) * decay, 0.0)
    eg_c = jnp.exp(gc_c)
    inv = eye - lower
    pw = lower
    for _ in range(5):
        pw = _dot(pw, pw)
        inv = inv + _dot(inv, pw)
    sol = _dot(inv, jnp.concatenate([v_s * beta_c, kb_s * eg_c], axis=-1))
    u_s, w_s = sol[:, :DN_DV], sol[:, DN_DV:]
    k_dec = k_s * jnp.exp(gtot_c - gc_c)
    rb = lax.broadcasted_iota(jnp.int32, (n, DN_DK), 0) // CHUNK
    expand = lambda x: jnp.concatenate([jnp.where(rb == h, x, 0.0) for h in range(nh)], axis=1)
    s = s_ref[...]
    v_new = u_s - _dot(expand(w_s), s)
    a_qk = _dot_nt(q_s, k_s) * decay
    o_s = _dot(expand(q_s * eg_c), s) + _dot(a_qk, v_new)
    gl_rows = jnp.concatenate([jnp.broadcast_to(jnp.exp(g_tot[:, cc:cc + 1]), (DN_DK, 1)) for cc in cg], axis=0)
    s_ref[...] = s * gl_rows + _dot_tn(expand(k_dec), v_new)
    o_ref[0] = jnp.concatenate([o_s[h * CHUNK:(h + 1) * CHUNK] for h in range(nh)], axis=1)


def dn_scan_tc(feat_c, feat_l, ba_c, ba_l, a_log, dt_bias, rev):
    b, l, nf = feat_l.shape
    n_ctx, n_lat = feat_c.shape[1] // CHUNK, l // CHUNK
    dirn = 1 if rev else 0
    cc, lc = _scan_chunks(rev, n_ctx, n_lat)
    lanes = lambda p: jnp.zeros((1, LANES), F32).at[0, 2 * DN_HEADS:4 * DN_HEADS].set(p.reshape(-1))
    vec = pl.BlockSpec((1, LANES), lambda bi, j: (0, 0))
    return pl.pallas_call(
        functools.partial(_dn_scan_kernel, rev, dirn, n_ctx), grid=(b, n_ctx + n_lat),
        in_specs=[vec, vec,
                  pl.BlockSpec((1, CHUNK, nf), lambda bi, j: (bi, cc(j), 0)),
                  pl.BlockSpec((1, CHUNK, nf), lambda bi, j: (bi, lc(j), 0)),
                  pl.BlockSpec((1, CHUNK, LANES), lambda bi, j: (bi, cc(j), 0)),
                  pl.BlockSpec((1, CHUNK, LANES), lambda bi, j: (bi, lc(j), 0))],
        out_specs=pl.BlockSpec((1, CHUNK, DN_HEADS * DN_DV), lambda bi, j: (bi, lc(j), 0)),
        out_shape=jax.ShapeDtypeStruct((b, l, DN_HEADS * DN_DV), F32),
        scratch_shapes=[pltpu.VMEM((DN_HEADS * DN_DK, DN_DV), F32)],
        compiler_params=_cparams("parallel", "arbitrary"), name="dn_scan_bwd" if rev else "dn_scan_fwd",
    )(lanes(a_log), lanes(dt_bias), feat_c, feat_l, ba_c, ba_l)


def _from_grid_cols(blk, n):
    cols = blk.shape[1] // n
    return jnp.concatenate([blk[:, i * n:(i + 1) * n] for i in range(cols)], axis=0)


def _gla_scan_kernel(rev, dirn, n_ctx, qkc_ref, vc_ref, lrc_ref, qkl_ref, vl_ref, lrl_ref,
                     wla_ref, bla_ref, o_ref, s_ref):
    step = pl.program_id(1)

    @pl.when(step == 0)
    def _():
        s_ref[...] = jnp.zeros_like(s_ref)

    in_ctx = step < n_ctx
    hk, hv = GLA_HEADS * GLA_DK, GLA_HEADS * GLA_DV
    qk = jnp.where(in_ctx, qkc_ref[0], _from_grid_cols(qkl_ref[0], 2 * hk))
    vv = jnp.where(in_ctx, vc_ref[0], _from_grid_cols(vl_ref[0], hv))
    lr = jnp.where(in_ctx, lrc_ref[0], _from_grid_cols(lrl_ref[0], LANES))
    incl, _ = _tri_masks(rev)
    incl_b = incl.astype(BF16)
    pre = _dot(lr, wla_ref[0]) + bla_ref[0]
    la_all = -_softplus(-pre) * (1.0 / GLA_TAU)
    bc_all = _mask_dot(incl_b, la_all)
    b_tot_all = jnp.sum(la_all, axis=0, keepdims=True)
    outs = []
    for h in range(GLA_HEADS):
        q = qk[:, h * GLA_DK:(h + 1) * GLA_DK] * GLA_DK ** -0.5
        k = qk[:, hk + h * GLA_DK:hk + (h + 1) * GLA_DK]
        v = vv[:, h * GLA_DV:(h + 1) * GLA_DV]
        bc = bc_all[:, h * GLA_DK:(h + 1) * GLA_DK]
        b_tot = b_tot_all[:, h * GLA_DK:(h + 1) * GLA_DK]
        st = s_ref[h]
        o = _dot_nt(q * jnp.exp(bc), st)
        parts = []
        for i in range(CHUNK // GLA_SUB):
            lo_r, hi_r = i * GLA_SUB, (i + 1) * GLA_SUB
            if rev:
                ref = bc[hi_r - 1:hi_r]
                c0, c1 = lo_r, CHUNK
            else:
                ref = bc[lo_r:lo_r + 1]
                c0, c1 = 0, hi_r
            qi = q[lo_r:hi_r] * jnp.exp(bc[lo_r:hi_r] - ref)
            ki = k[c0:c1] * jnp.exp(ref - bc[c0:c1])
            att = _dot_nt(qi, ki)
            rg = lax.broadcasted_iota(jnp.int32, (GLA_SUB, c1 - c0), 0) + lo_r
            cg = lax.broadcasted_iota(jnp.int32, (GLA_SUB, c1 - c0), 1) + c0
            keep = (cg > rg) if rev else (cg < rg)
            parts.append(_dot(jnp.where(keep, att, 0.0), v[c0:c1]))
        diag = jnp.sum(q * k, axis=-1, keepdims=True) * v
        outs.append(o + jnp.concatenate(parts, axis=0) + diag)
        k_dec = k * jnp.exp(b_tot - bc)
        s_ref[h] = st * jnp.exp(b_tot) + _dot_tn(v, k_dec)
    o = jnp.concatenate(outs, axis=-1)
    rows = o_ref.shape[1]
    o_ref[0] = jnp.concatenate([o[i * rows:(i + 1) * rows] for i in range(CHUNK // rows)], axis=-1)


def gla_scan_tc(qk_c, v_c, lr_c, qk_l, v_l, lr_l, w_la, b_la, rev):
    b, l, _ = qk_l.shape
    rows = l // GRID_W
    cols = CHUNK // rows
    n_ctx, n_lat = qk_c.shape[1] // CHUNK, l // CHUNK
    dirn = 1 if rev else 0
    cc, lc = _scan_chunks(rev, n_ctx, n_lat)
    hv = GLA_HEADS * GLA_DV
    ctx_blk = lambda a: pl.BlockSpec((1, CHUNK, a.shape[2]), lambda bi, j: (bi, cc(j), 0))
    lat_blk = lambda n: pl.BlockSpec((1, rows, cols * n), lambda bi, j: (bi, 0, lc(j)))
    grid_view = lambda a: a.reshape(b, rows, GRID_W * a.shape[2])
    out = pl.pallas_call(
        functools.partial(_gla_scan_kernel, rev, dirn, n_ctx), grid=(b, n_ctx + n_lat),
        in_specs=[ctx_blk(qk_c), ctx_blk(v_c), ctx_blk(lr_c),
                  lat_blk(qk_l.shape[2]), lat_blk(v_l.shape[2]), lat_blk(lr_l.shape[2]),
                  pl.BlockSpec((1,) + w_la.shape[1:], lambda bi, j: (dirn, 0, 0)),
                  pl.BlockSpec((1,) + b_la.shape[1:], lambda bi, j: (dirn, 0, 0))],
        out_specs=lat_blk(hv),
        out_shape=jax.ShapeDtypeStruct((b, rows, GRID_W * hv), F32),
        scratch_shapes=[pltpu.VMEM((GLA_HEADS, GLA_DV, GLA_DK), F32)],
        compiler_params=_cparams("parallel", "arbitrary"), name="gla_scan_bwd" if rev else "gla_scan_fwd",
    )(qk_c, v_c, lr_c, grid_view(qk_l), grid_view(v_l), grid_view(lr_l), w_la, b_la)
    return out.reshape(b, l, hv)


def _head_norm_gate(o, gate, g, n_heads, dv):
    parts = []
    for h in range(n_heads):
        oh = o[:, h * dv:(h + 1) * dv]
        gh = gate[:, h * dv:(h + 1) * dv]
        yh = oh * lax.rsqrt(jnp.mean(oh * oh, axis=-1, keepdims=True) + EPS) * g
        parts.append(yh * (gh * jax.nn.sigmoid(gh)))
    return parts


def _mix_out_kernel(x_ref, dnf_ref, dnb_ref, z_ref, glf_ref, glb_ref, r_ref, mod_ref, dng_ref,
                    glg_ref, n2g_ref, wout_ref, wq_ref, x1_ref, h2_ref, q_ref):
    parts = (_head_norm_gate(dnf_ref[0] + dnb_ref[0], z_ref[0], dng_ref[...], DN_HEADS, DN_DV)
             + _head_norm_gate(glf_ref[0] + glb_ref[0], r_ref[0], glg_ref[...], GLA_HEADS, GLA_DV))
    y = jnp.dot(jnp.concatenate(parts, axis=-1).astype(BF16), wout_ref[...], preferred_element_type=F32)
    x1 = x_ref[0] + mod_ref[0, 2:3, :] * y
    x1_ref[0] = x1
    n = x1 * lax.rsqrt(jnp.mean(x1 * x1, axis=-1, keepdims=True) + EPS) * n2g_ref[...]
    h2 = n * (1.0 + mod_ref[0, 4:5, :]) + mod_ref[0, 3:4, :]
    h2_ref[0] = h2
    h_hi, h_lo = _split(h2)
    q_ref[0] = (jnp.dot(h_hi, wq_ref[0], preferred_element_type=F32)
                + jnp.dot(h_lo, wq_ref[0], preferred_element_type=F32)
                + jnp.dot(h_hi, wq_ref[1], preferred_element_type=F32))


def mix_out_tc(x, dn_f, dn_b, z, gl_f, gl_b, r, mod_l, dn_g, gla_g, n2_g, w_out, w_q):
    b, l, d = x.shape
    tok = lambda n: pl.BlockSpec((1, TOK_TILE, n), lambda bi, i: (bi, i, 0))
    full = lambda a: pl.BlockSpec(a.shape, lambda bi, i: (0,) * a.ndim)
    dn_g, gla_g, n2_g = dn_g.reshape(1, -1), gla_g.reshape(1, -1), n2_g.reshape(1, -1)
    nq = w_q.shape[2]
    return pl.pallas_call(
        _mix_out_kernel, grid=(b, l // TOK_TILE),
        in_specs=[tok(d), tok(dn_f.shape[2]), tok(dn_b.shape[2]), tok(z.shape[2]),
                  tok(gl_f.shape[2]), tok(gl_b.shape[2]), tok(r.shape[2]),
                  pl.BlockSpec((1,) + mod_l.shape[1:], lambda bi, i: (bi, 0, 0)),
                  full(dn_g), full(gla_g), full(n2_g), full(w_out), full(w_q)],
        out_specs=[tok(d), tok(d), tok(nq)],
        out_shape=[jax.ShapeDtypeStruct((b, l, d), F32), jax.ShapeDtypeStruct((b, l, d), F32),
                   jax.ShapeDtypeStruct((b, l, nq), F32)],
        compiler_params=_cparams("parallel", "arbitrary"), name="mix_out",
    )(x, dn_f, dn_b, z, gl_f, gl_b, r, mod_l, dn_g, gla_g, n2_g, w_out, w_q)


def _top_rows(s, k, payload=None):
    n = s.shape[0]
    row = lax.broadcasted_iota(jnp.int32, s.shape, 0).astype(F32)
    vals, picked = [], []
    for _ in range(k):
        m = jnp.max(s, axis=0, keepdims=True)
        first = jnp.min(jnp.where(s == m, row, float(n)), axis=0, keepdims=True)
        sel = row == first
        vals.append(m)
        if payload is None:
            picked.append(first)
        else:
            picked.append(jnp.max(jnp.where(sel, payload, -1.0), axis=0, keepdims=True))
        s = jnp.where(sel, -jnp.inf, s)
    return jnp.concatenate(vals, axis=0), jnp.concatenate(picked, axis=0)


def _candidate_rows(s0, i0, s1, i1):
    k = s0.shape[0]
    wide = SUBLANES
    blocks_s = [s0[0:1] + s1]
    blocks_i = [i0[0:1] * float(PEER_NKEYS) + i1]
    col = lax.broadcasted_iota(jnp.int32, (wide, s0.shape[1]), 0)
    for i in range(1, wide):
        keep = col < (k // (i + 1))
        blocks_s.append(jnp.where(keep, s0[i:i + 1] + s1[0:wide], -jnp.inf))
        blocks_i.append(i0[i:i + 1] * float(PEER_NKEYS) + i1[0:wide])
    blocks_s.append(s0[wide:k] + s1[0:1])
    blocks_i.append(i0[wide:k] * float(PEER_NKEYS) + i1[0:1])
    return jnp.concatenate(blocks_s, axis=0), jnp.concatenate(blocks_i, axis=0)


def _select_kernel(q_ref, k_ref, idx_ref, gate_ref, idx_s, gate_s):
    half = PEER_DQ // 2

    def head(h, carry):
        tops = []
        for p in range(2):
            qp = q_ref[:, pl.ds(pl.multiple_of(h * PEER_DQ + p * half, half), half)]
            s = lax.dot_general(k_ref[h, p], qp, (((1,), (1,)), ((), ())),
                                preferred_element_type=F32,
                                precision=lax.Precision.HIGHEST)
            tops.append(_top_rows(s, PEER_TOPK))
        (s0, i0), (s1, i1) = tops
        cand_s, cand_i = _candidate_rows(s0, i0, s1, i1)
        best_s, idx = _top_rows(cand_s, PEER_TOPK, payload=cand_i)
        e = jnp.exp(best_s - best_s[0:1])
        r0 = pl.multiple_of(h * PEER_TOPK, PEER_TOPK)
        idx_s[pl.ds(r0, PEER_TOPK), :] = idx
        gate_s[pl.ds(r0, PEER_TOPK), :] = e / jnp.sum(e, axis=0, keepdims=True)
        return carry

    lax.fori_loop(0, PEER_HEADS, head, 0)
    idx_ref[...] = idx_s[...].T.astype(jnp.int32)
    gate_ref[...] = gate_s[...].T


def peer_select_tc(q, keys):
    n_tok = q.shape[0]
    out_spec = pl.BlockSpec((SELECT_TILE, PICKS), lambda i: (i, 0))
    return pl.pallas_call(
        _select_kernel,
        grid=(n_tok // SELECT_TILE,),
        in_specs=[pl.BlockSpec((SELECT_TILE, q.shape[1]), lambda i: (i, 0)),
                  pl.BlockSpec(keys.shape, lambda i: (0, 0, 0, 0))],
        out_specs=[out_spec, out_spec],
        out_shape=[jax.ShapeDtypeStruct((n_tok, PICKS), jnp.int32),
                   jax.ShapeDtypeStruct((n_tok, PICKS), F32)],
        scratch_shapes=[pltpu.VMEM((PICKS, SELECT_TILE), F32), pltpu.VMEM((PICKS, SELECT_TILE), F32)],
        compiler_params=_cparams("parallel"), name="peer_select",
    )(q, keys)


def _sc_mesh():
    return plsc.VectorSubcoreMesh(core_axis_name="c", subcore_axis_name="s")


def _sc_pipeline(body, n_steps, in_specs, out_specs, operands):
    pltpu.emit_pipeline(
        body, grid=(n_steps,), in_specs=in_specs, out_specs=out_specs,
        core_axis_name=("c", "s"), dimension_semantics=(pltpu.PARALLEL,),
        trace_scopes=False,
    )(*operands)


def pack_pairs(t):
    half = t.shape[1] // 2
    bits = lax.bitcast_convert_type(t, jnp.uint32)
    rne = lambda b: (b + jnp.uint32(0x7FFF) + ((b >> 16) & jnp.uint32(1))) >> 16
    word = (rne(bits[:, half:]) << 16) | rne(bits[:, :half])
    return lax.bitcast_convert_type(word, jnp.int32)


def peer_act_partial_sc(u_q, idx2, h_q):
    n_groups, (n_tok, half) = idx2.shape[0], h_q.shape
    nsub = TOK_STEP * GROUPS_PER_TOK
    n_chunks = half // SC_LANES
    as_pairs = lambda w: plsc.bitcast(w, BF16)

    @functools.partial(
        pl.kernel, mesh=_sc_mesh(), compiler_params=pltpu.CompilerParams(needs_layout_passes=False),
        out_type=jax.ShapeDtypeStruct((n_groups, PICK_GROUP * SC_LANES), F32),
        scratch_types=[pltpu.VMEM((GATHER_BUFS, PICK_GROUP, half), jnp.int32),
                       pltpu.SemaphoreType.DMA((GATHER_BUFS,))],
    )
    def k(u_hbm, i_hbm, h_hbm, o_hbm, rows, sems):
        def body(i_v, h_v, o_v):
            def fetch(j, slot):
                return pltpu.make_async_copy(u_hbm.at[i_v.at[j]], rows.at[slot], sems.at[slot])

            for ahead in range(GATHER_BUFS - 1):
                fetch(ahead, ahead).start()

            def sub(j, carry):
                slot = j % GATHER_BUFS

                nxt = j + GATHER_BUFS - 1

                @pl.when(nxt < nsub)
                def _():
                    fetch(nxt, nxt % GATHER_BUFS).start()

                fetch(j, slot).wait()
                t = j // GROUPS_PER_TOK

                def picks(g, carry2):
                    kb = g * ACT_UNROLL
                    accs = [None] * ACT_UNROLL
                    for c0 in range(0, n_chunks, ACT_BF16_TERMS):
                        cols = [pl.ds((c0 + cc) * SC_LANES, SC_LANES) for cc in range(ACT_BF16_TERMS)]
                        hs = [as_pairs(h_v[t, col]) for col in cols]
                        for i in range(ACT_UNROLL):
                            s = None
                            for col, hv in zip(cols, hs):
                                p = as_pairs(rows[slot, kb + i, col]) * hv
                                s = p if s is None else s + p
                            lo, hi = plsc.unpack(s, format=plsc.PackFormat.INTERLEAVED)
                            p32 = lo + hi
                            accs[i] = p32 if accs[i] is None else accs[i] + p32
                    for i in range(ACT_UNROLL):
                        o_v[j, pl.ds((kb + i) * SC_LANES, SC_LANES)] = accs[i]
                    return carry2

                lax.fori_loop(0, PICK_GROUP // ACT_UNROLL, picks, 0)
                return carry

            lax.fori_loop(0, nsub, sub, 0)

        _sc_pipeline(
            body, n_tok // TOK_STEP,
            [pl.BlockSpec((nsub, PICK_GROUP), lambda i: (i, 0)),
             pl.BlockSpec((TOK_STEP, half), lambda i: (i, 0))],
            [pl.BlockSpec((nsub, PICK_GROUP * SC_LANES), lambda i: (i, 0))],
            (i_hbm, h_hbm, o_hbm))

    return k(u_q, idx2, h_q)


def peer_combine_sc(v_q, idx2, coef_q):
    n_groups = idx2.shape[0]
    half = v_q.shape[1]
    nsub = COMB_GROUP_STEP
    n_chunks = half // SC_LANES
    as_pairs = lambda w: plsc.bitcast(w, BF16)

    @functools.partial(
        pl.kernel, mesh=_sc_mesh(), compiler_params=pltpu.CompilerParams(needs_layout_passes=False),
        out_type=jax.ShapeDtypeStruct((n_groups, 2 * half), F32),
        scratch_types=[pltpu.VMEM((GATHER_BUFS, PICK_GROUP, half), jnp.int32),
                       pltpu.SemaphoreType.DMA((GATHER_BUFS,))],
    )
    def k(v_hbm, i_hbm, c_hbm, o_hbm, rows, sems):
        def body(i_v, c_v, o_v):
            def fetch(j, slot):
                return pltpu.make_async_copy(v_hbm.at[i_v.at[j]], rows.at[slot], sems.at[slot])

            for ahead in range(GATHER_BUFS - 1):
                fetch(ahead, ahead).start()

            def sub(j, carry0):
                slot = j % GATHER_BUFS

                nxt = j + GATHER_BUFS - 1

                @pl.when(nxt < nsub)
                def _():
                    fetch(nxt, nxt % GATHER_BUFS).start()

                fetch(j, slot).wait()
                cks = [as_pairs(c_v[j, pl.ds(kk * SC_LANES, SC_LANES)]) for kk in range(PICK_GROUP)]

                def chunk_pair(c2, carry):
                    sums = []
                    for cc in range(COMB_CHUNKS):
                        l = (c2 * COMB_CHUNKS + cc) * SC_LANES
                        tot_lo, tot_hi = None, None
                        for k0 in range(0, PICK_GROUP, COMB_BF16_TERMS):
                            s = None
                            for kk in range(k0, k0 + COMB_BF16_TERMS):
                                p = cks[kk] * as_pairs(rows[slot, kk, pl.ds(l, SC_LANES)])
                                s = p if s is None else s + p
                            lo, hi = plsc.unpack(s, format=plsc.PackFormat.INTERLEAVED)
                            tot_lo = lo if tot_lo is None else tot_lo + lo
                            tot_hi = hi if tot_hi is None else tot_hi + hi
                        sums.append((l, tot_lo, tot_hi))
                    for l, tot_lo, tot_hi in sums:
                        o_v[j, pl.ds(l, SC_LANES)] = tot_lo
                        o_v[j, pl.ds(half + l, SC_LANES)] = tot_hi
                    return carry

                lax.fori_loop(0, n_chunks // COMB_CHUNKS, chunk_pair, 0)
                return carry0

            lax.fori_loop(0, nsub, sub, 0)

        _sc_pipeline(
            body, n_groups // nsub,
            [pl.BlockSpec((nsub, PICK_GROUP), lambda i: (i, 0)),
             pl.BlockSpec((nsub, PICK_GROUP * SC_LANES), lambda i: (i, 0))],
            [pl.BlockSpec((nsub, 2 * half), lambda i: (i, 0))],
            (i_hbm, c_hbm, o_hbm))

    return k(v_q, idx2, coef_q)


def _segment_matrix():
    r = lax.broadcasted_iota(jnp.int32, (PICK_GROUP * SC_LANES, PICK_GROUP), 0) // SC_LANES
    c = lax.broadcasted_iota(jnp.int32, (PICK_GROUP * SC_LANES, PICK_GROUP), 1)
    return (r == c).astype(F32)


def _coef_kernel(part_ref, gate_ref, o_ref):
    seg = _segment_matrix()
    act = jnp.dot(part_ref[...], seg, preferred_element_type=F32, precision=lax.Precision.HIGHEST)
    coef = gate_ref[...] * (0.5 * act * (1.0 + lax.erf(act * (2.0 ** -0.5))))
    wide = lax.dot_general(coef, seg, (((1,), (1,)), ((), ())), preferred_element_type=F32,
                           precision=lax.Precision.HIGHEST)
    bits = pltpu.bitcast(wide, jnp.uint32)
    r = (bits + jnp.uint32(0x7FFF) + ((bits >> 16) & jnp.uint32(1))) >> 16
    o_ref[...] = pltpu.bitcast(r | (r << 16), jnp.int32)


def peer_coef_tc(part, gate2):
    n_groups, width = part.shape
    tile = 1024
    return pl.pallas_call(
        _coef_kernel,
        grid=(n_groups // tile,),
        in_specs=[pl.BlockSpec((tile, width), lambda i: (i, 0)),
                  pl.BlockSpec((tile, PICK_GROUP), lambda i: (i, 0))],
        out_specs=pl.BlockSpec((tile, width), lambda i: (i, 0)),
        out_shape=jax.ShapeDtypeStruct((n_groups, width), jnp.int32),
        compiler_params=_cparams("parallel"), name="peer_coef",
    )(part, gate2)


def _final_kernel(x_ref, y_ref, mod_ref, g_ref, o_ref):
    d = x_ref.shape[2]
    y = y_ref[0, :, 0:d]
    for p in range(1, y_ref.shape[2] // d):
        y = y + y_ref[0, :, p * d:(p + 1) * d]
    x = x_ref[0] + mod_ref[0, 5:6, :] * y
    o_ref[0] = x * lax.rsqrt(jnp.mean(x * x, axis=-1, keepdims=True) + EPS) * g_ref[...]


def final_tc(x1, y_parts, mod_l, final_g):
    b, l, d = x1.shape
    tok = pl.BlockSpec((1, TOK_TILE, d), lambda bi, i: (bi, i, 0))
    return pl.pallas_call(
        _final_kernel, grid=(b, l // TOK_TILE),
        in_specs=[tok, pl.BlockSpec((1, TOK_TILE, y_parts.shape[2]), lambda bi, i: (bi, i, 0)),
                  pl.BlockSpec((1,) + mod_l.shape[1:], lambda bi, i: (bi, 0, 0)),
                  pl.BlockSpec((1, d), lambda bi, i: (0, 0))],
        out_specs=tok, out_shape=jax.ShapeDtypeStruct((b, l, d), F32),
        compiler_params=_cparams("parallel", "arbitrary"), name="final_norm",
    )(x1, y_parts, mod_l, final_g.reshape(1, d))


def _pad_cols(w, n):
    return jnp.pad(w, ((0, 0), (0, n - w.shape[1])))


def _slice_sizes(b):
    if b % SLICE_BATCH or SLICE_BATCH % 2 or b < 2 * SLICE_BATCH:
        return [1] * b
    half = SLICE_BATCH // 2
    return [half] + [SLICE_BATCH] * (b // SLICE_BATCH - 1) + [half]


def _exact_zero(v):
    return jnp.minimum(jnp.abs(v), 0.0)


def forward(x, c, ctx, c_ctx, w_ada, b_ada, norm1_g, norm2_g, w_in, conv_w, dn_a_log,
            dn_dt_bias, dn_norm_g, gla_wa2, gla_ba, gla_norm_g, w_out, peer_wq, peer_keys,
            peer_u, peer_v, final_g):
    b, l, d = x.shape
    c_all = jnp.concatenate([c, c_ctx[None]], axis=0)
    c_all = jnp.pad(c_all, ((0, (-c_all.shape[0]) % SUBLANES), (0, 0)))
    mod = adaln_mod(c_all, w_ada, b_ada)
    mod_l = mod[:b].reshape(b, 6, d)
    mod_c = jnp.broadcast_to(mod[b].reshape(1, 6, d), (b, 6, d))
    o = DN_QKV
    hv = DN_HEADS * DN_DV
    w_dn_qkv, w_dn_z = w_in[:, :o], w_in[:, o:o + hv]
    w_dn_ba = _pad_cols(w_in[:, o + hv:DN_COLS], LANES)
    g0 = DN_COLS
    gqk, gv = 2 * GLA_HEADS * GLA_DK, GLA_HEADS * GLA_DV
    w_gl_qk, w_gl_v = w_in[:, g0:g0 + gqk], w_in[:, g0 + gqk:g0 + gqk + gv]
    w_gl_r = w_in[:, g0 + gqk + gv:g0 + gqk + 2 * gv]
    w_gl_lr = _pad_cols(w_in[:, g0 + gqk + 2 * gv:], LANES)
    w_lat = [w.astype(BF16) for w in (w_dn_qkv, w_dn_ba, w_gl_qk, w_gl_v, w_gl_lr, w_dn_z, w_gl_r)]
    w_ctx = w_lat[:5]
    w_la = jnp.zeros((2, LANES, GLA_HEADS * GLA_DK), F32)
    for dd in range(2):
        w_la = w_la.at[dd, dd * GLA_LR:(dd + 1) * GLA_LR].set(gla_wa2[dd])
    w_la = w_la.astype(BF16)
    b_la = gla_ba.reshape(2, 1, GLA_HEADS * GLA_DK)
    w_out_b, w_q_b = w_out.astype(BF16), jnp.stack(_split(peer_wq))
    u_q, v_q = pack_pairs(peer_u), pack_pairs(peer_v)

    def mixer(xg, ctxg, mod_cg, mod_g, after_select, after_combine):
        mod_cg, mod_g = mod_cg + after_select, mod_g + after_select
        c_qkv, c_ba, c_qk, c_v, c_lr = in_projection(ctxg, norm1_g, mod_cg[:, 0:2], w_ctx)
        l_qkv, l_ba, l_qk, l_v, l_lr, l_z, l_r = in_projection(xg, norm1_g, mod_g[:, 0:2], w_lat)
        feat_c, feat_l = dn_features_tc(c_qkv, conv_w), dn_features_tc(l_qkv, conv_w)
        dn_f = dn_scan_tc(feat_c, feat_l, c_ba, l_ba, dn_a_log, dn_dt_bias, rev=False)
        dn_b = dn_scan_tc(feat_c, feat_l, c_ba, l_ba, dn_a_log, dn_dt_bias, rev=True)
        gl_f = gla_scan_tc(c_qk, c_v, c_lr, l_qk, l_v, l_lr, w_la, b_la, rev=False)
        gl_b = gla_scan_tc(c_qk, c_v, c_lr, l_qk, l_v, l_lr, w_la, b_la, rev=True)
        return mix_out_tc(xg, dn_f, dn_b, l_z, gl_f, gl_b, l_r, mod_g, dn_norm_g + after_combine,
                          gla_norm_g, norm2_g, w_out_b, w_q_b)

    def select_and_act(h2, q, after_coef):
        n_tok = h2.shape[0] * l
        idx, gate = peer_select_tc(q.reshape(n_tok, -1), peer_keys + after_coef)
        idx2 = idx.reshape(n_tok * GROUPS_PER_TOK, PICK_GROUP)
        gate2 = gate.reshape(n_tok * GROUPS_PER_TOK, PICK_GROUP)
        return idx2, gate2, peer_act_partial_sc(u_q, idx2, pack_pairs(h2.reshape(n_tok, d)))

    def coef_and_combine(st, after_mixer):
        coef_q = peer_coef_tc(st["part"], st["gate2"] + after_mixer)
        st["y"] = peer_combine_sc(v_q, st["idx2"], coef_q)
        return _exact_zero(coef_q[0, 0].astype(F32))

    zero = jnp.zeros((), F32)
    slices, z_sel, z_coef = [], zero, zero
    i = 0
    for bg in _slice_sizes(b):
        g = len(slices)
        z_comb = _exact_zero(slices[g - 2]["y"][0, 0]) if g >= 2 else zero
        mod_g = mod_l[i:i + bg]
        x1, h2, q = mixer(x[i:i + bg], ctx[i:i + bg], mod_c[i:i + bg], mod_g, z_sel, z_comb)
        i += bg
        if g >= 1:
            z_coef = coef_and_combine(slices[g - 1], _exact_zero(x1[0, 0, 0]))
        idx2, gate2, part = select_and_act(h2, q, z_coef)
        z_sel = _exact_zero(gate2[0, 0])
        slices.append(dict(x1=x1, mod=mod_g, idx2=idx2, gate2=gate2, part=part))
    coef_and_combine(slices[-1], zero)
    outs = [final_tc(st["x1"], st["y"].reshape(st["x1"].shape[:2] + (-1,)), st["mod"], final_g) for st in slices]
    return jnp.concatenate(outs, axis=0)


def kernel(x, c, ctx, c_ctx, w_ada, b_ada, norm1_g, norm2_g, w_in, conv_w, dn_a_log,
           dn_dt_bias, dn_norm_g, gla_wa2, gla_ba, gla_norm_g, w_out, peer_wq, peer_keys,
           peer_u, peer_v, final_g):
    assert w_ada.shape[0] == 1, "single-layer block: the context stream is only consumed, never updated"
    return forward(x, c, ctx, c_ctx, w_ada[0], b_ada[0], norm1_g[0], norm2_g[0], w_in[0], conv_w[0],
                   dn_a_log[0], dn_dt_bias[0], dn_norm_g[0], gla_wa2[0], gla_ba[0], gla_norm_g[0],
                   w_out[0], peer_wq[0], peer_keys[0], peer_u[0], peer_v[0], final_g)
```

```python
import functools

import jax
import jax.numpy as jnp
from jax import lax
from jax.experimental import pallas as pl
from jax.experimental.pallas import tpu as pltpu
from jax.experimental.pallas import tpu_sc as plsc

GRID_W = 64
DN_HEADS = 4
DN_DK = 128
DN_DV = 128
CONV_W = 5
GLA_HEADS = 4
GLA_DK = 64
GLA_DV = 128
GLA_LR = 16
GLA_TAU = 16.0
CHUNK = 64
PEER_HEADS = 8
PEER_NKEYS = 128
PEER_DQ = 256
PEER_TOPK = 16
EPS = 1e-6
DN_QKV = 2 * DN_HEADS * DN_DK + DN_HEADS * DN_DV
DN_COLS = DN_QKV + DN_HEADS * DN_DV + 4 * DN_HEADS

SUBLANES = 8
LANES = 128
SC_LANES = 16
VMEM_LIMIT_BYTES = 48 * 1024 * 1024

TOK_TILE = 256
SELECT_TILE = 256
GLA_SUB = 16
PICK_GROUP = 32
COMB_GROUP_STEP = 16
GATHER_BUFS = 4
TOK_STEP = 8
COMB_BF16_TERMS = 4
COMB_CHUNKS = 4
ACT_BF16_TERMS = 4
ACT_UNROLL = 16
SLICE_BATCH = 2
PICKS = PEER_HEADS * PEER_TOPK
GROUPS_PER_TOK = PICKS // PICK_GROUP

F32 = jnp.float32
BF16 = jnp.bfloat16


def _cparams(*semantics):
    return pltpu.CompilerParams(dimension_semantics=semantics, vmem_limit_bytes=VMEM_LIMIT_BYTES)


def _dot(a, b):
    return jnp.dot(a.astype(BF16), b.astype(BF16), preferred_element_type=F32)


def _dot_nt(a, b):
    return lax.dot_general(a.astype(BF16), b.astype(BF16), (((1,), (1,)), ((), ())),
                           preferred_element_type=F32)


def _dot_tn(a, b):
    return lax.dot_general(a.astype(BF16), b.astype(BF16), (((0,), (0,)), ((), ())),
                           preferred_element_type=F32)


def _split(x):
    hi = x.astype(BF16)
    return hi, (x - hi.astype(F32)).astype(BF16)


def _mask_dot(mask_bf16, x):
    hi, lo = _split(x)
    return (jnp.dot(mask_bf16, hi, preferred_element_type=F32)
            + jnp.dot(mask_bf16, lo, preferred_element_type=F32))


def _softplus(x):
    return jnp.maximum(x, 0.0) + jnp.log(1.0 + jnp.exp(-jnp.abs(x)))


def _tri_masks(rev):
    r = lax.broadcasted_iota(jnp.int32, (CHUNK, CHUNK), 0)
    c = lax.broadcasted_iota(jnp.int32, (CHUNK, CHUNK), 1)
    d = (c - r) if rev else (r - c)
    return d >= 0, d > 0


def _mod_kernel(c_ref, w_ref, b_ref, o_ref):
    c = c_ref[...]
    s = c * jax.nn.sigmoid(c)
    o_ref[...] = jnp.dot(s, w_ref[...], preferred_element_type=F32,
                         precision=lax.Precision.HIGHEST) + b_ref[...]


def adaln_mod(c_all, w_ada, b_ada):
    r, d = c_all.shape
    n = w_ada.shape[1]
    tn = 512
    return pl.pallas_call(
        _mod_kernel, grid=(n // tn,),
        in_specs=[pl.BlockSpec((r, d), lambda j: (0, 0)),
                  pl.BlockSpec((d, tn), lambda j: (0, j)),
                  pl.BlockSpec((1, tn), lambda j: (0, j))],
        out_specs=pl.BlockSpec((r, tn), lambda j: (0, j)),
        out_shape=jax.ShapeDtypeStruct((r, n), F32),
        compiler_params=_cparams("arbitrary"), name="adaln_mod",
    )(c_all, w_ada, b_ada.reshape(1, n))


def _inproj_kernel(x_ref, g_ref, mod_ref, *refs):
    n_out = len(refs) // 2
    x = x_ref[0]
    y = x * lax.rsqrt(jnp.mean(x * x, axis=-1, keepdims=True) + EPS) * g_ref[...]
    h = (y * (1.0 + mod_ref[0, 1:2, :]) + mod_ref[0, 0:1, :]).astype(BF16)
    for w_ref, o_ref in zip(refs[:n_out], refs[n_out:]):
        o_ref[0] = jnp.dot(h, w_ref[...], preferred_element_type=F32)


def in_projection(x, norm_g, mod, weights):
    b, l, d = x.shape
    w_specs = [pl.BlockSpec(w.shape, lambda bi, i: (0, 0)) for w in weights]
    o_specs = [pl.BlockSpec((1, TOK_TILE, w.shape[1]), lambda bi, i: (bi, i, 0)) for w in weights]
    return pl.pallas_call(
        _inproj_kernel, grid=(b, l // TOK_TILE),
        in_specs=[pl.BlockSpec((1, TOK_TILE, d), lambda bi, i: (bi, i, 0)),
                  pl.BlockSpec((1, d), lambda bi, i: (0, 0)),
                  pl.BlockSpec((1, 2, d), lambda bi, i: (bi, 0, 0))] + w_specs,
        out_specs=o_specs,
        out_shape=[jax.ShapeDtypeStruct((b, l, w.shape[1]), F32) for w in weights],
        compiler_params=_cparams("parallel", "arbitrary"), name="in_projection",
    )(x, norm_g.reshape(1, d), mod, *weights)


def _dn_feature_kernel(x_ref, w_ref, o_ref):
    x = x_ref[0]
    n = x.shape[0]
    t = lax.broadcasted_iota(jnp.int32, (n, 1), 0)
    pad = CONV_W // 2
    acc = w_ref[0, pad:pad + 1, :] * x
    for j in range(CONV_W):
        s = j - pad
        if s == 0:
            continue
        xs = pltpu.roll(x, (-s) % n, axis=0)
        bad = (t < -s) if s < 0 else (t >= n - s)
        acc = acc + w_ref[0, j:j + 1, :] * jnp.where(bad, 0.0, xs)
    y = acc * jax.nn.sigmoid(acc)
    kind = pl.program_id(1) // DN_HEADS
    inv = lax.rsqrt(jnp.sum(y * y, axis=-1, keepdims=True) + EPS)
    scale = jnp.where(kind == 0, inv * DN_DK ** -0.5, jnp.where(kind == 1, inv, 1.0))
    o_ref[0] = y * scale


def dn_features_tc(qkv, conv_w):
    b, l, n = qkv.shape
    nblk = n // LANES
    w = jnp.zeros((nblk, SUBLANES, LANES), F32).at[:, :CONV_W].set(
        conv_w.reshape(CONV_W, nblk, LANES).transpose(1, 0, 2))
    return pl.pallas_call(
        _dn_feature_kernel, grid=(b, nblk),
        in_specs=[pl.BlockSpec((1, l, LANES), lambda bi, j: (bi, 0, j)),
                  pl.BlockSpec((1, SUBLANES, LANES), lambda bi, j: (j, 0, 0))],
        out_specs=pl.BlockSpec((1, l, LANES), lambda bi, j: (bi, 0, j)),
        out_shape=jax.ShapeDtypeStruct((b, l, n), F32),
        compiler_params=_cparams("parallel", "arbitrary"), name="dn_features",
    )(qkv, w)


def _scan_chunks(rev, n_ctx, n_lat):
    if rev:
        ctx = lambda j: jnp.maximum(n_ctx - 1 - j, 0)
        lat = lambda j: jnp.where(j < n_ctx, n_lat - 1, n_lat - 1 - (j - n_ctx))
    else:
        ctx = lambda j: jnp.minimum(j, n_ctx - 1)
        lat = lambda j: jnp.maximum(j - n_ctx, 0)
    return ctx, lat


def _dn_scan_kernel(rev, dirn, n_ctx, alog_ref, dtb_ref, fc_ref, fl_ref, bac_ref, bal_ref, o_ref, s_ref):
    step = pl.program_id(1)

    @pl.when(step == 0)
    def _():
        s_ref[...] = jnp.zeros_like(s_ref)

    in_ctx = step < n_ctx
    f = jnp.where(in_ctx, fc_ref[0], fl_ref[0])
    ba = jnp.where(in_ctx, bac_ref[0], bal_ref[0])
    nh, hd, n = DN_HEADS, DN_HEADS * DN_DK, DN_HEADS * CHUNK
    stack = lambda base, w: jnp.concatenate([f[:, base + h * w:base + (h + 1) * w] for h in range(nh)], axis=0)
    q_s, k_s, v_s = stack(0, DN_DK), stack(hd, DN_DK), stack(2 * hd, DN_DV)
    incl, _ = _tri_masks(rev)
    beta_all = jax.nn.sigmoid(ba)
    g_all = -jnp.exp(alog_ref[...]) * _softplus(ba + dtb_ref[...])
    gc_all = _mask_dot(incl.astype(BF16), g_all)
    gc_t = jnp.concatenate([gc_all, gc_all], axis=0).T
    g_tot = jnp.sum(g_all, axis=0, keepdims=True)
    cb = [dirn * nh + h for h in range(nh)]
    cg = [2 * nh + c for c in cb]
    col = lambda a, cs: jnp.concatenate([a[:, c:c + 1] for c in cs], axis=0)
    beta_c, gc_c = col(beta_all, cb), col(gc_all, cg)
    gtot_c = jnp.concatenate([jnp.broadcast_to(g_tot[:, c:c + 1], (CHUNK, 1)) for c in cg], axis=0)
    gc_r = jnp.concatenate([gc_t[c:c + 1, :CHUNK] for c in cg], axis=1)
    r = lax.broadcasted_iota(jnp.int32, (n, n), 0)
    c = lax.broadcasted_iota(jnp.int32, (n, n), 1)
    same = (r // CHUNK) == (c // CHUNK)
    d = (c - r) if rev else (r - c)
    incl_bd, strict_bd = same & (d >= 0), same & (d > 0)
    eye = (r == c).astype(F32)
    decay = jnp.where(incl_bd, jnp.exp(jnp.where(incl_bd, gc_c - gc_r, 0.0)), 0.0)
    kb_s = k_s * beta_c
    lower = jnp.where(strict_bd, _dot_nt(kb_s, k_s) * decay, 0.0)
    eg_c = jnp.exp(gc_c)
    inv = eye - lower
    pw = lower
    for _ in range(5):
        pw = _dot(pw, pw)
        inv = inv + _dot(inv, pw)
    sol = _dot(inv, jnp.concatenate([v_s * beta_c, kb_s * eg_c], axis=-1))
    u_s, w_s = sol[:, :DN_DV], sol[:, DN_DV:]
    k_dec = k_s * jnp.exp(gtot_c - gc_c)
    rb = lax.broadcasted_iota(jnp.int32, (n, DN_DK), 0) // CHUNK
    expand = lambda x: jnp.concatenate([jnp.where(rb == h, x, 0.0) for h in range(nh)], axis=1)
    s = s_ref[...]
    v_new = u_s - _dot(expand(w_s), s)
    a_qk = _dot_nt(q_s, k_s) * decay
    o_s = _dot(expand(q_s * eg_c), s) + _dot(a_qk, v_new)
    gl_rows = jnp.concatenate([jnp.broadcast_to(jnp.exp(g_tot[:, cc:cc + 1]), (DN_DK, 1)) for cc in cg], axis=0)
    s_ref[...] = s * gl_rows + _dot_tn(expand(k_dec), v_new)
    o_ref[0] = jnp.concatenate([o_s[h * CHUNK:(h + 1) * CHUNK] for h in range(nh)], axis=1)


def dn_scan_tc(feat_c, feat_l, ba_c, ba_l, a_log, dt_bias, rev):
    b, l, nf = feat_l.shape
    n_ctx, n_lat = feat_c.shape[1] // CHUNK, l // CHUNK
    dirn = 1 if rev else 0
    cc, lc = _scan_chunks(rev, n_ctx, n_lat)
    lanes = lambda p: jnp.zeros((1, LANES), F32).at[0, 2 * DN_HEADS:4 * DN_HEADS].set(p.reshape(-1))
    vec = pl.BlockSpec((1, LANES), lambda bi, j: (0, 0))
    return pl.pallas_call(
        functools.partial(_dn_scan_kernel, rev, dirn, n_ctx), grid=(b, n_ctx + n_lat),
        in_specs=[vec, vec,
                  pl.BlockSpec((1, CHUNK, nf), lambda bi, j: (bi, cc(j), 0)),
                  pl.BlockSpec((1, CHUNK, nf), lambda bi, j: (bi, lc(j), 0)),
                  pl.BlockSpec((1, CHUNK, LANES), lambda bi, j: (bi, cc(j), 0)),
                  pl.BlockSpec((1, CHUNK, LANES), lambda bi, j: (bi, lc(j), 0))],
        out_specs=pl.BlockSpec((1, CHUNK, DN_HEADS * DN_DV), lambda bi, j: (bi, lc(j), 0)),
        out_shape=jax.ShapeDtypeStruct((b, l, DN_HEADS * DN_DV), F32),
        scratch_shapes=[pltpu.VMEM((DN_HEADS * DN_DK, DN_DV), F32)],
        compiler_params=_cparams("parallel", "arbitrary"), name="dn_scan_bwd" if rev else "dn_scan_fwd",
    )(lanes(a_log), lanes(dt_bias), feat_c, feat_l, ba_c, ba_l)


def _from_grid_cols(blk, n):
    cols = blk.shape[1] // n
    return jnp.concatenate([blk[:, i * n:(i + 1) * n] for i in range(cols)], axis=0)


def _gla_scan_kernel(rev, dirn, n_ctx, qkc_ref, vc_ref, lrc_ref, qkl_ref, vl_ref, lrl_ref,
                     wla_ref, bla_ref, o_ref, s_ref):
    step = pl.program_id(1)

    @pl.when(step == 0)
    def _():
        s_ref[...] = jnp.zeros_like(s_ref)

    in_ctx = step < n_ctx
    hk, hv = GLA_HEADS * GLA_DK, GLA_HEADS * GLA_DV
    qk = jnp.where(in_ctx, qkc_ref[0], _from_grid_cols(qkl_ref[0], 2 * hk))
    vv = jnp.where(in_ctx, vc_ref[0], _from_grid_cols(vl_ref[0], hv))
    lr = jnp.where(in_ctx, lrc_ref[0], _from_grid_cols(lrl_ref[0], LANES))
    incl, _ = _tri_masks(rev)
    incl_b = incl.astype(BF16)
    pre = _dot(lr, wla_ref[0]) + bla_ref[0]
    la_all = -_softplus(-pre) * (1.0 / GLA_TAU)
    bc_all = _mask_dot(incl_b, la_all)
    b_tot_all = jnp.sum(la_all, axis=0, keepdims=True)
    outs = []
    for h in range(GLA_HEADS):
        q = qk[:, h * GLA_DK:(h + 1) * GLA_DK] * GLA_DK ** -0.5
        k = qk[:, hk + h * GLA_DK:hk + (h + 1) * GLA_DK]
        v = vv[:, h * GLA_DV:(h + 1) * GLA_DV]
        bc = bc_all[:, h * GLA_DK:(h + 1) * GLA_DK]
        b_tot = b_tot_all[:, h * GLA_DK:(h + 1) * GLA_DK]
        st = s_ref[h]
        o = _dot_nt(q * jnp.exp(bc), st)
        parts = []
        for i in range(CHUNK // GLA_SUB):
            lo_r, hi_r = i * GLA_SUB, (i + 1) * GLA_SUB
            if rev:
                ref = bc[hi_r - 1:hi_r]
                c0, c1 = lo_r, CHUNK
            else:
                ref = bc[lo_r:lo_r + 1]
                c0, c1 = 0, hi_r
            qi = q[lo_r:hi_r] * jnp.exp(bc[lo_r:hi_r] - ref)
            ki = k[c0:c1] * jnp.exp(ref - bc[c0:c1])
            att = _dot_nt(qi, ki)
            rg = lax.broadcasted_iota(jnp.int32, (GLA_SUB, c1 - c0), 0) + lo_r
            cg = lax.broadcasted_iota(jnp.int32, (GLA_SUB, c1 - c0), 1) + c0
            keep = (cg > rg) if rev else (cg < rg)
            parts.append(_dot(jnp.where(keep, att, 0.0), v[c0:c1]))
        diag = jnp.sum(q * k, axis=-1, keepdims=True) * v
        outs.append(o + jnp.concatenate(parts, axis=0) + diag)
        k_dec = k * jnp.exp(b_tot - bc)
        s_ref[h] = st * jnp.exp(b_tot) + _dot_tn(v, k_dec)
    o = jnp.concatenate(outs, axis=-1)
    rows = o_ref.shape[1]
    o_ref[0] = jnp.concatenate([o[i * rows:(i + 1) * rows] for i in range(CHUNK // rows)], axis=-1)


def gla_scan_tc(qk_c, v_c, lr_c, qk_l, v_l, lr_l, w_la, b_la, rev):
    b, l, _ = qk_l.shape
    rows = l // GRID_W
    cols = CHUNK // rows
    n_ctx, n_lat = qk_c.shape[1] // CHUNK, l // CHUNK
    dirn = 1 if rev else 0
    cc, lc = _scan_chunks(rev, n_ctx, n_lat)
    hv = GLA_HEADS * GLA_DV
    ctx_blk = lambda a: pl.BlockSpec((1, CHUNK, a.shape[2]), lambda bi, j: (bi, cc(j), 0))
    lat_blk = lambda n: pl.BlockSpec((1, rows, cols * n), lambda bi, j: (bi, 0, lc(j)))
    grid_view = lambda a: a.reshape(b, rows, GRID_W * a.shape[2])
    out = pl.pallas_call(
        functools.partial(_gla_scan_kernel, rev, dirn, n_ctx), grid=(b, n_ctx + n_lat),
        in_specs=[ctx_blk(qk_c), ctx_blk(v_c), ctx_blk(lr_c),
                  lat_blk(qk_l.shape[2]), lat_blk(v_l.shape[2]), lat_blk(lr_l.shape[2]),
                  pl.BlockSpec((1,) + w_la.shape[1:], lambda bi, j: (dirn, 0, 0)),
                  pl.BlockSpec((1,) + b_la.shape[1:], lambda bi, j: (dirn, 0, 0))],
        out_specs=lat_blk(hv),
        out_shape=jax.ShapeDtypeStruct((b, rows, GRID_W * hv), F32),
        scratch_shapes=[pltpu.VMEM((GLA_HEADS, GLA_DV, GLA_DK), F32)],
        compiler_params=_cparams("parallel", "arbitrary"), name="gla_scan_bwd" if rev else "gla_scan_fwd",
    )(qk_c, v_c, lr_c, grid_view(qk_l), grid_view(v_l), grid_view(lr_l), w_la, b_la)
    return out.reshape(b, l, hv)


def _head_norm_gate(o, gate, g, n_heads, dv):
    parts = []
    for h in range(n_heads):
        oh = o[:, h * dv:(h + 1) * dv]
        gh = gate[:, h * dv:(h + 1) * dv]
        yh = oh * lax.rsqrt(jnp.mean(oh * oh, axis=-1, keepdims=True) + EPS) * g
        parts.append(yh * (gh * jax.nn.sigmoid(gh)))
    return parts


def _mix_out_kernel(x_ref, dnf_ref, dnb_ref, z_ref, glf_ref, glb_ref, r_ref, mod_ref, dng_ref,
                    glg_ref, n2g_ref, wout_ref, wq_ref, x1_ref, h2_ref, q_ref):
    parts = (_head_norm_gate(dnf_ref[0] + dnb_ref[0], z_ref[0], dng_ref[...], DN_HEADS, DN_DV)
             + _head_norm_gate(glf_ref[0] + glb_ref[0], r_ref[0], glg_ref[...], GLA_HEADS, GLA_DV))
    y = jnp.dot(jnp.concatenate(parts, axis=-1).astype(BF16), wout_ref[...], preferred_element_type=F32)
    x1 = x_ref[0] + mod_ref[0, 2:3, :] * y
    x1_ref[0] = x1
    n = x1 * lax.rsqrt(jnp.mean(x1 * x1, axis=-1, keepdims=True) + EPS) * n2g_ref[...]
    h2 = n * (1.0 + mod_ref[0, 4:5, :]) + mod_ref[0, 3:4, :]
    h2_ref[0] = h2
    h_hi, h_lo = _split(h2)
    q_ref[0] = (jnp.dot(h_hi, wq_ref[0], preferred_element_type=F32)
                + jnp.dot(h_lo, wq_ref[0], preferred_element_type=F32)
                + jnp.dot(h_hi, wq_ref[1], preferred_element_type=F32))


def mix_out_tc(x, dn_f, dn_b, z, gl_f, gl_b, r, mod_l, dn_g, gla_g, n2_g, w_out, w_q):
    b, l, d = x.shape
    tok = lambda n: pl.BlockSpec((1, TOK_TILE, n), lambda bi, i: (bi, i, 0))
    full = lambda a: pl.BlockSpec(a.shape, lambda bi, i: (0,) * a.ndim)
    dn_g, gla_g, n2_g = dn_g.reshape(1, -1), gla_g.reshape(1, -1), n2_g.reshape(1, -1)
    nq = w_q.shape[2]
    return pl.pallas_call(
        _mix_out_kernel, grid=(b, l // TOK_TILE),
        in_specs=[tok(d), tok(dn_f.shape[2]), tok(dn_b.shape[2]), tok(z.shape[2]),
                  tok(gl_f.shape[2]), tok(gl_b.shape[2]), tok(r.shape[2]),
                  pl.BlockSpec((1,) + mod_l.shape[1:], lambda bi, i: (bi, 0, 0)),
                  full(dn_g), full(gla_g), full(n2_g), full(w_out), full(w_q)],
        out_specs=[tok(d), tok(d), tok(nq)],
        out_shape=[jax.ShapeDtypeStruct((b, l, d), F32), jax.ShapeDtypeStruct((b, l, d), F32),
                   jax.ShapeDtypeStruct((b, l, nq), F32)],
        compiler_params=_cparams("parallel", "arbitrary"), name="mix_out",
    )(x, dn_f, dn_b, z, gl_f, gl_b, r, mod_l, dn_g, gla_g, n2_g, w_out, w_q)


def _top_rows(s, k, payload=None):
    n = s.shape[0]
    row = lax.broadcasted_iota(jnp.int32, s.shape, 0).astype(F32)
    vals, picked = [], []
    for _ in range(k):
        m = jnp.max(s, axis=0, keepdims=True)
        first = jnp.min(jnp.where(s == m, row, float(n)), axis=0, keepdims=True)
        sel = row == first
        vals.append(m)
        if payload is None:
            picked.append(first)
        else:
            picked.append(jnp.max(jnp.where(sel, payload, -1.0), axis=0, keepdims=True))
        s = jnp.where(sel, -jnp.inf, s)
    return jnp.concatenate(vals, axis=0), jnp.concatenate(picked, axis=0)


def _candidate_rows(s0, i0, s1, i1):
    k = s0.shape[0]
    wide = SUBLANES
    blocks_s = [s0[0:1] + s1]
    blocks_i = [i0[0:1] * float(PEER_NKEYS) + i1]
    col = lax.broadcasted_iota(jnp.int32, (wide, s0.shape[1]), 0)
    for i in range(1, wide):
        keep = col < (k // (i + 1))
        blocks_s.append(jnp.where(keep, s0[i:i + 1] + s1[0:wide], -jnp.inf))
        blocks_i.append(i0[i:i + 1] * float(PEER_NKEYS) + i1[0:wide])
    blocks_s.append(s0[wide:k] + s1[0:1])
    blocks_i.append(i0[wide:k] * float(PEER_NKEYS) + i1[0:1])
    return jnp.concatenate(blocks_s, axis=0), jnp.concatenate(blocks_i, axis=0)


def _select_kernel(q_ref, k_ref, idx_ref, gate_ref, idx_s, gate_s):
    half = PEER_DQ // 2

    def head(h, carry):
        tops = []
        for p in range(2):
            qp = q_ref[:, pl.ds(pl.multiple_of(h * PEER_DQ + p * half, half), half)]
            s = lax.dot_general(k_ref[h, p], qp, (((1,), (1,)), ((), ())),
                                preferred_element_type=F32,
                                precision=lax.Precision.HIGHEST)
            tops.append(_top_rows(s, PEER_TOPK))
        (s0, i0), (s1, i1) = tops
        cand_s, cand_i = _candidate_rows(s0, i0, s1, i1)
        best_s, idx = _top_rows(cand_s, PEER_TOPK, payload=cand_i)
        e = jnp.exp(best_s - best_s[0:1])
        r0 = pl.multiple_of(h * PEER_TOPK, PEER_TOPK)
        idx_s[pl.ds(r0, PEER_TOPK), :] = idx
        gate_s[pl.ds(r0, PEER_TOPK), :] = e / jnp.sum(e, axis=0, keepdims=True)
        return carry

    lax.fori_loop(0, PEER_HEADS, head, 0)
    idx_ref[...] = idx_s[...].T.astype(jnp.int32)
    gate_ref[...] = gate_s[...].T


def peer_select_tc(q, keys):
    n_tok = q.shape[0]
    out_spec = pl.BlockSpec((SELECT_TILE, PICKS), lambda i: (i, 0))
    return pl.pallas_call(
        _select_kernel,
        grid=(n_tok // SELECT_TILE,),
        in_specs=[pl.BlockSpec((SELECT_TILE, q.shape[1]), lambda i: (i, 0)),
                  pl.BlockSpec(keys.shape, lambda i: (0, 0, 0, 0))],
        out_specs=[out_spec, out_spec],
        out_shape=[jax.ShapeDtypeStruct((n_tok, PICKS), jnp.int32),
                   jax.ShapeDtypeStruct((n_tok, PICKS), F32)],
        scratch_shapes=[pltpu.VMEM((PICKS, SELECT_TILE), F32), pltpu.VMEM((PICKS, SELECT_TILE), F32)],
        compiler_params=_cparams("parallel"), name="peer_select",
    )(q, keys)


def _sc_mesh():
    return plsc.VectorSubcoreMesh(core_axis_name="c", subcore_axis_name="s")


def _sc_pipeline(body, n_steps, in_specs, out_specs, operands):
    pltpu.emit_pipeline(
        body, grid=(n_steps,), in_specs=in_specs, out_specs=out_specs,
        core_axis_name=("c", "s"), dimension_semantics=(pltpu.PARALLEL,),
        trace_scopes=False,
    )(*operands)


def pack_pairs(t):
    half = t.shape[1] // 2
    bits = lax.bitcast_convert_type(t, jnp.uint32)
    rne = lambda b: (b + jnp.uint32(0x7FFF) + ((b >> 16) & jnp.uint32(1))) >> 16
    word = (rne(bits[:, half:]) << 16) | rne(bits[:, :half])
    return lax.bitcast_convert_type(word, jnp.int32)


def peer_act_partial_sc(u_q, idx2, h_q):
    n_groups, (n_tok, half) = idx2.shape[0], h_q.shape
    nsub = TOK_STEP * GROUPS_PER_TOK
    n_chunks = half // SC_LANES
    as_pairs = lambda w: plsc.bitcast(w, BF16)

    @functools.partial(
        pl.kernel, mesh=_sc_mesh(), compiler_params=pltpu.CompilerParams(needs_layout_passes=False),
        out_type=jax.ShapeDtypeStruct((n_groups, PICK_GROUP * SC_LANES), F32),
        scratch_types=[pltpu.VMEM((GATHER_BUFS, PICK_GROUP, half), jnp.int32),
                       pltpu.SemaphoreType.DMA((GATHER_BUFS,))],
    )
    def k(u_hbm, i_hbm, h_hbm, o_hbm, rows, sems):
        def body(i_v, h_v, o_v):
            def fetch(j, slot):
                return pltpu.make_async_copy(u_hbm.at[i_v.at[j]], rows.at[slot], sems.at[slot])

            for ahead in range(GATHER_BUFS - 1):
                fetch(ahead, ahead).start()

            def sub(j, carry):
                slot = j % GATHER_BUFS

                nxt = j + GATHER_BUFS - 1

                @pl.when(nxt < nsub)
                def _():
                    fetch(nxt, nxt % GATHER_BUFS).start()

                fetch(j, slot).wait()
                t = j // GROUPS_PER_TOK

                def picks(g, carry2):
                    kb = g * ACT_UNROLL
                    accs = [None] * ACT_UNROLL
                    for c0 in range(0, n_chunks, ACT_BF16_TERMS):
                        cols = [pl.ds((c0 + cc) * SC_LANES, SC_LANES) for cc in range(ACT_BF16_TERMS)]
                        hs = [as_pairs(h_v[t, col]) for col in cols]
                        for i in range(ACT_UNROLL):
                            s = None
                            for col, hv in zip(cols, hs):
                                p = as_pairs(rows[slot, kb + i, col]) * hv
                                s = p if s is None else s + p
                            lo, hi = plsc.unpack(s, format=plsc.PackFormat.INTERLEAVED)
                            p32 = lo + hi
                            accs[i] = p32 if accs[i] is None else accs[i] + p32
                    for i in range(ACT_UNROLL):
                        o_v[j, pl.ds((kb + i) * SC_LANES, SC_LANES)] = accs[i]
                    return carry2

                lax.fori_loop(0, PICK_GROUP // ACT_UNROLL, picks, 0)
                return carry

            lax.fori_loop(0, nsub, sub, 0)

        _sc_pipeline(
            body, n_tok // TOK_STEP,
            [pl.BlockSpec((nsub, PICK_GROUP), lambda i: (i, 0)),
             pl.BlockSpec((TOK_STEP, half), lambda i: (i, 0))],
            [pl.BlockSpec((nsub, PICK_GROUP * SC_LANES), lambda i: (i, 0))],
            (i_hbm, h_hbm, o_hbm))

    return k(u_q, idx2, h_q)


def peer_combine_sc(v_q, idx2, coef_q):
    n_groups = idx2.shape[0]
    half = v_q.shape[1]
    nsub = COMB_GROUP_STEP
    n_chunks = half // SC_LANES
    as_pairs = lambda w: plsc.bitcast(w, BF16)

    @functools.partial(
        pl.kernel, mesh=_sc_mesh(), compiler_params=pltpu.CompilerParams(needs_layout_passes=False),
        out_type=jax.ShapeDtypeStruct((n_groups, 2 * half), F32),
        scratch_types=[pltpu.VMEM((GATHER_BUFS, PICK_GROUP, half), jnp.int32),
                       pltpu.SemaphoreType.DMA((GATHER_BUFS,))],
    )
    def k(v_hbm, i_hbm, c_hbm, o_hbm, rows, sems):
        def body(i_v, c_v, o_v):
            def fetch(j, slot):
                return pltpu.make_async_copy(v_hbm.at[i_v.at[j]], rows.at[slot], sems.at[slot])

            for ahead in range(GATHER_BUFS - 1):
                fetch(ahead, ahead).start()

            def sub(j, carry0):
                slot = j % GATHER_BUFS

                nxt = j + GATHER_BUFS - 1

                @pl.when(nxt < nsub)
                def _():
                    fetch(nxt, nxt % GATHER_BUFS).start()

                fetch(j, slot).wait()
                cks = [as_pairs(c_v[j, pl.ds(kk * SC_LANES, SC_LANES)]) for kk in range(PICK_GROUP)]

                def chunk_pair(c2, carry):
                    sums = []
                    for cc in range(COMB_CHUNKS):
                        l = (c2 * COMB_CHUNKS + cc) * SC_LANES
                        tot_lo, tot_hi = None, None
                        for k0 in range(0, PICK_GROUP, COMB_BF16_TERMS):
                            s = None
                            for kk in range(k0, k0 + COMB_BF16_TERMS):
                                p = cks[kk] * as_pairs(rows[slot, kk, pl.ds(l, SC_LANES)])
                                s = p if s is None else s + p
                            lo, hi = plsc.unpack(s, format=plsc.PackFormat.INTERLEAVED)
                            tot_lo = lo if tot_lo is None else tot_lo + lo
                            tot_hi = hi if tot_hi is None else tot_hi + hi
                        sums.append((l, tot_lo, tot_hi))
                    for l, tot_lo, tot_hi in sums:
                        o_v[j, pl.ds(l, SC_LANES)] = tot_lo
                        o_v[j, pl.ds(half + l, SC_LANES)] = tot_hi
                    return carry

                lax.fori_loop(0, n_chunks // COMB_CHUNKS, chunk_pair, 0)
                return carry0

            lax.fori_loop(0, nsub, sub, 0)

        _sc_pipeline(
            body, n_groups // nsub,
            [pl.BlockSpec((nsub, PICK_GROUP), lambda i: (i, 0)),
             pl.BlockSpec((nsub, PICK_GROUP * SC_LANES), lambda i: (i, 0))],
            [pl.BlockSpec((nsub, 2 * half), lambda i: (i, 0))],
            (i_hbm, c_hbm, o_hbm))

    return k(v_q, idx2, coef_q)


def _segment_matrix():
    r = lax.broadcasted_iota(jnp.int32, (PICK_GROUP * SC_LANES, PICK_GROUP), 0) // SC_LANES
    c = lax.broadcasted_iota(jnp.int32, (PICK_GROUP * SC_LANES, PICK_GROUP), 1)
    return (r == c).astype(F32)


def _coef_kernel(part_ref, gate_ref, o_ref):
    seg = _segment_matrix()
    act = jnp.dot(part_ref[...], seg, preferred_element_type=F32, precision=lax.Precision.HIGHEST)
    coef = gate_ref[...] * (0.5 * act * (1.0 + lax.erf(act * (2.0 ** -0.5))))
    wide = lax.dot_general(coef, seg, (((1,), (1,)), ((), ())), preferred_element_type=F32,
                           precision=lax.Precision.HIGHEST)
    bits = pltpu.bitcast(wide, jnp.uint32)
    r = (bits + jnp.uint32(0x7FFF) + ((bits >> 16) & jnp.uint32(1))) >> 16
    o_ref[...] = pltpu.bitcast(r | (r << 16), jnp.int32)


def peer_coef_tc(part, gate2):
    n_groups, width = part.shape
    tile = 1024
    return pl.pallas_call(
        _coef_kernel,
        grid=(n_groups // tile,),
        in_specs=[pl.BlockSpec((tile, width), lambda i: (i, 0)),
                  pl.BlockSpec((tile, PICK_GROUP), lambda i: (i, 0))],
        out_specs=pl.BlockSpec((tile, width), lambda i: (i, 0)),
        out_shape=jax.ShapeDtypeStruct((n_groups, width), jnp.int32),
        compiler_params=_cparams("parallel"), name="peer_coef",
    )(part, gate2)


def _final_kernel(x_ref, y_ref, mod_ref, g_ref, o_ref):
    d = x_ref.shape[2]
    y = y_ref[0, :, 0:d]
    for p in range(1, y_ref.shape[2] // d):
        y = y + y_ref[0, :, p * d:(p + 1) * d]
    x = x_ref[0] + mod_ref[0, 5:6, :] * y
    o_ref[0] = x * lax.rsqrt(jnp.mean(x * x, axis=-1, keepdims=True) + EPS) * g_ref[...]


def final_tc(x1, y_parts, mod_l, final_g):
    b, l, d = x1.shape
    tok = pl.BlockSpec((1, TOK_TILE, d), lambda bi, i: (bi, i, 0))
    return pl.pallas_call(
        _final_kernel, grid=(b, l // TOK_TILE),
        in_specs=[tok, pl.BlockSpec((1, TOK_TILE, y_parts.shape[2]), lambda bi, i: (bi, i, 0)),
                  pl.BlockSpec((1,) + mod_l.shape[1:], lambda bi, i: (bi, 0, 0)),
                  pl.BlockSpec((1, d), lambda bi, i: (0, 0))],
        out_specs=tok, out_shape=jax.ShapeDtypeStruct((b, l, d), F32),
        compiler_params=_cparams("parallel", "arbitrary"), name="final_norm",
    )(x1, y_parts, mod_l, final_g.reshape(1, d))


def _pad_cols(w, n):
    return jnp.pad(w, ((0, 0), (0, n - w.shape[1])))


def _slice_sizes(b):
    if b % SLICE_BATCH or SLICE_BATCH % 2 or b < 2 * SLICE_BATCH:
        return [1] * b
    half = SLICE_BATCH // 2
    return [half] + [SLICE_BATCH] * (b // SLICE_BATCH - 1) + [half]


def _exact_zero(v):
    return jnp.minimum(jnp.abs(v), 0.0)


def forward(x, c, ctx, c_ctx, w_ada, b_ada, norm1_g, norm2_g, w_in, conv_w, dn_a_log,
            dn_dt_bias, dn_norm_g, gla_wa2, gla_ba, gla_norm_g, w_out, peer_wq, peer_keys,
            peer_u, peer_v, final_g):
    b, l, d = x.shape
    c_all = jnp.concatenate([c, c_ctx[None]], axis=0)
    c_all = jnp.pad(c_all, ((0, (-c_all.shape[0]) % SUBLANES), (0, 0)))
    mod = adaln_mod(c_all, w_ada, b_ada)
    mod_l = mod[:b].reshape(b, 6, d)
    mod_c = jnp.broadcast_to(mod[b].reshape(1, 6, d), (b, 6, d))
    o = DN_QKV
    hv = DN_HEADS * DN_DV
    w_dn_qkv, w_dn_z = w_in[:, :o], w_in[:, o:o + hv]
    w_dn_ba = _pad_cols(w_in[:, o + hv:DN_COLS], LANES)
    g0 = DN_COLS
    gqk, gv = 2 * GLA_HEADS * GLA_DK, GLA_HEADS * GLA_DV
    w_gl_qk, w_gl_v = w_in[:, g0:g0 + gqk], w_in[:, g0 + gqk:g0 + gqk + gv]
    w_gl_r = w_in[:, g0 + gqk + gv:g0 + gqk + 2 * gv]
    w_gl_lr = _pad_cols(w_in[:, g0 + gqk + 2 * gv:], LANES)
    w_lat = [w.astype(BF16) for w in (w_dn_qkv, w_dn_ba, w_gl_qk, w_gl_v, w_gl_lr, w_dn_z, w_gl_r)]
    w_ctx = w_lat[:5]
    w_la = jnp.zeros((2, LANES, GLA_HEADS * GLA_DK), F32)
    for dd in range(2):
        w_la = w_la.at[dd, dd * GLA_LR:(dd + 1) * GLA_LR].set(gla_wa2[dd])
    w_la = w_la.astype(BF16)
    b_la = gla_ba.reshape(2, 1, GLA_HEADS * GLA_DK)
    w_out_b, w_q_b = w_out.astype(BF16), jnp.stack(_split(peer_wq))
    u_q, v_q = pack_pairs(peer_u), pack_pairs(peer_v)

    def mixer(xg, ctxg, mod_cg, mod_g, after_select, after_combine):
        mod_cg, mod_g = mod_cg + after_select, mod_g + after_select
        c_qkv, c_ba, c_qk, c_v, c_lr = in_projection(ctxg, norm1_g, mod_cg[:, 0:2], w_ctx)
        l_qkv, l_ba, l_qk, l_v, l_lr, l_z, l_r = in_projection(xg, norm1_g, mod_g[:, 0:2], w_lat)
        feat_c, feat_l = dn_features_tc(c_qkv, conv_w), dn_features_tc(l_qkv, conv_w)
        dn_f = dn_scan_tc(feat_c, feat_l, c_ba, l_ba, dn_a_log, dn_dt_bias, rev=False)
        dn_b = dn_scan_tc(feat_c, feat_l, c_ba, l_ba, dn_a_log, dn_dt_bias, rev=True)
        gl_f = gla_scan_tc(c_qk, c_v, c_lr, l_qk, l_v, l_lr, w_la, b_la, rev=False)
        gl_b = gla_scan_tc(c_qk, c_v, c_lr, l_qk, l_v, l_lr, w_la, b_la, rev=True)
        return mix_out_tc(xg, dn_f, dn_b, l_z, gl_f, gl_b, l_r, mod_g, dn_norm_g + after_combine,
                          gla_norm_g, norm2_g, w_out_b, w_q_b)

    def select_and_act(h2, q, after_coef):
        n_tok = h2.shape[0] * l
        idx, gate = peer_select_tc(q.reshape(n_tok, -1), peer_keys + after_coef)
        idx2 = idx.reshape(n_tok * GROUPS_PER_TOK, PICK_GROUP)
        gate2 = gate.reshape(n_tok * GROUPS_PER_TOK, PICK_GROUP)
        return idx2, gate2, peer_act_partial_sc(u_q, idx2, pack_pairs(h2.reshape(n_tok, d)))

    def coef_and_combine(st, after_mixer):
        coef_q = peer_coef_tc(st["part"], st["gate2"] + after_mixer)
        st["y"] = peer_combine_sc(v_q, st["idx2"], coef_q)
        return _exact_zero(coef_q[0, 0].astype(F32))

    zero = jnp.zeros((), F32)
    slices, z_sel, z_coef = [], zero, zero
    i = 0
    for bg in _slice_sizes(b):
        g = len(slices)
        z_comb = _exact_zero(slices[g - 2]["y"][0, 0]) if g >= 2 else zero
        mod_g = mod_l[i:i + bg]
        x1, h2, q = mixer(x[i:i + bg], ctx[i:i + bg], mod_c[i:i + bg], mod_g, z_sel, z_comb)
        i += bg
        if g >= 1:
            z_coef = coef_and_combine(slices[g - 1], _exact_zero(x1[0, 0, 0]))
        idx2, gate2, part = select_and_act(h2, q, z_coef)
        z_sel = _exact_zero(gate2[0, 0])
        slices.append(dict(x1=x1, mod=mod_g, idx2=idx2, gate2=gate2, part=part))
    coef_and_combine(slices[-1], zero)
    outs = [final_tc(st["x1"], st["y"].reshape(st["x1"].shape[:2] + (-1,)), st["mod"], final_g) for st in slices]
    return jnp.concatenate(outs, axis=0)


def kernel(x, c, ctx, c_ctx, w_ada, b_ada, norm1_g, norm2_g, w_in, conv_w, dn_a_log,
           dn_dt_bias, dn_norm_g, gla_wa2, gla_ba, gla_norm_g, w_out, peer_wq, peer_keys,
           peer_u, peer_v, final_g):
    assert w_ada.shape[0] == 1, "single-layer block: the context stream is only consumed, never updated"
    return forward(x, c, ctx, c_ctx, w_ada[0], b_ada[0], norm1_g[0], norm2_g[0], w_in[0], conv_w[0],
                   dn_a_log[0], dn_dt_bias[0], dn_norm_g[0], gla_wa2[0], gla_ba[0], gla_norm_g[0],
                   w_out[0], peer_wq[0], peer_keys[0], peer_u[0], peer_v[0], final_g)
```

```python
import functools

import jax
import jax.numpy as jnp
from jax import lax
from jax.experimental import pallas as pl
from jax.experimental.pallas import tpu as pltpu
from jax.experimental.pallas import tpu_sc as plsc

GRID_W = 64
DN_HEADS = 4
DN_DK = 128
DN_DV = 128
CONV_W = 5
GLA_HEADS = 4
GLA_DK = 64
GLA_DV = 128
GLA_LR = 16
GLA_TAU = 16.0
CHUNK = 64
PEER_HEADS = 8
PEER_NKEYS = 128
PEER_DQ = 256
PEER_TOPK = 16
EPS = 1e-6
DN_QKV = 2 * DN_HEADS * DN_DK + DN_HEADS * DN_DV
DN_COLS = DN_QKV + DN_HEADS * DN_DV + 4 * DN_HEADS

SUBLANES = 8
LANES = 128
SC_LANES = 16
VMEM_LIMIT_BYTES = 48 * 1024 * 1024

TOK_TILE = 256
SELECT_TILE = 256
GLA_SUB = 16
PICK_GROUP = 32
COMB_GROUP_STEP = 16
GATHER_BUFS = 3
TOK_STEP = 8
COMB_BF16_TERMS = 4
COMB_CHUNKS = 4
ACT_BF16_TERMS = 4
ACT_UNROLL = 16
DRAIN_HALVES = 3
SLICE_BATCH = 2
PICKS = PEER_HEADS * PEER_TOPK
GROUPS_PER_TOK = PICKS // PICK_GROUP

F32 = jnp.float32
BF16 = jnp.bfloat16


def _cparams(*semantics):
    return pltpu.CompilerParams(dimension_semantics=semantics, vmem_limit_bytes=VMEM_LIMIT_BYTES)


def _dot(a, b):
    return jnp.dot(a.astype(BF16), b.astype(BF16), preferred_element_type=F32)


def _dot_nt(a, b):
    return lax.dot_general(a.astype(BF16), b.astype(BF16), (((1,), (1,)), ((), ())),
                           preferred_element_type=F32)


def _dot_tn(a, b):
    return lax.dot_general(a.astype(BF16), b.astype(BF16), (((0,), (0,)), ((), ())),
                           preferred_element_type=F32)


def _split(x):
    hi = x.astype(BF16)
    return hi, (x - hi.astype(F32)).astype(BF16)


def _mask_dot(mask_bf16, x):
    hi, lo = _split(x)
    return (jnp.dot(mask_bf16, hi, preferred_element_type=F32)
            + jnp.dot(mask_bf16, lo, preferred_element_type=F32))


def _softplus(x):
    return jnp.maximum(x, 0.0) + jnp.log(1.0 + jnp.exp(-jnp.abs(x)))


def _tri_masks(rev):
    r = lax.broadcasted_iota(jnp.int32, (CHUNK, CHUNK), 0)
    c = lax.broadcasted_iota(jnp.int32, (CHUNK, CHUNK), 1)
    d = (c - r) if rev else (r - c)
    return d >= 0, d > 0


def _mod_kernel(c_ref, w_ref, b_ref, o_ref):
    c = c_ref[...]
    s = c * jax.nn.sigmoid(c)
    o_ref[...] = jnp.dot(s, w_ref[...], preferred_element_type=F32,
                         precision=lax.Precision.HIGHEST) + b_ref[...]


def adaln_mod(c_all, w_ada, b_ada):
    r, d = c_all.shape
    n = w_ada.shape[1]
    tn = 512
    return pl.pallas_call(
        _mod_kernel, grid=(n // tn,),
        in_specs=[pl.BlockSpec((r, d), lambda j: (0, 0)),
                  pl.BlockSpec((d, tn), lambda j: (0, j)),
                  pl.BlockSpec((1, tn), lambda j: (0, j))],
        out_specs=pl.BlockSpec((r, tn), lambda j: (0, j)),
        out_shape=jax.ShapeDtypeStruct((r, n), F32),
        compiler_params=_cparams("arbitrary"), name="adaln_mod",
    )(c_all, w_ada, b_ada.reshape(1, n))


def _inproj_kernel(x_ref, g_ref, mod_ref, *refs):
    n_out = len(refs) // 2
    x = x_ref[0]
    y = x * lax.rsqrt(jnp.mean(x * x, axis=-1, keepdims=True) + EPS) * g_ref[...]
    h = (y * (1.0 + mod_ref[0, 1:2, :]) + mod_ref[0, 0:1, :]).astype(BF16)
    for w_ref, o_ref in zip(refs[:n_out], refs[n_out:]):
        o_ref[0] = jnp.dot(h, w_ref[...], preferred_element_type=F32)


def in_projection(x, norm_g, mod, weights):
    b, l, d = x.shape
    w_specs = [pl.BlockSpec(w.shape, lambda bi, i: (0, 0)) for w in weights]
    o_specs = [pl.BlockSpec((1, TOK_TILE, w.shape[1]), lambda bi, i: (bi, i, 0)) for w in weights]
    return pl.pallas_call(
        _inproj_kernel, grid=(b, l // TOK_TILE),
        in_specs=[pl.BlockSpec((1, TOK_TILE, d), lambda bi, i: (bi, i, 0)),
                  pl.BlockSpec((1, d), lambda bi, i: (0, 0)),
                  pl.BlockSpec((1, 2, d), lambda bi, i: (bi, 0, 0))] + w_specs,
        out_specs=o_specs,
        out_shape=[jax.ShapeDtypeStruct((b, l, w.shape[1]), F32) for w in weights],
        compiler_params=_cparams("parallel", "arbitrary"), name="in_projection",
    )(x, norm_g.reshape(1, d), mod, *weights)


def _dn_feature_kernel(x_ref, w_ref, o_ref):
    x = x_ref[0]
    n = x.shape[0]
    t = lax.broadcasted_iota(jnp.int32, (n, 1), 0)
    pad = CONV_W // 2
    acc = w_ref[0, pad:pad + 1, :] * x
    for j in range(CONV_W):
        s = j - pad
        if s == 0:
            continue
        xs = pltpu.roll(x, (-s) % n, axis=0)
        bad = (t < -s) if s < 0 else (t >= n - s)
        acc = acc + w_ref[0, j:j + 1, :] * jnp.where(bad, 0.0, xs)
    y = acc * jax.nn.sigmoid(acc)
    kind = pl.program_id(1) // DN_HEADS
    inv = lax.rsqrt(jnp.sum(y * y, axis=-1, keepdims=True) + EPS)
    scale = jnp.where(kind == 0, inv * DN_DK ** -0.5, jnp.where(kind == 1, inv, 1.0))
    o_ref[0] = y * scale


def dn_features_tc(qkv, conv_w):
    b, l, n = qkv.shape
    nblk = n // LANES
    w = jnp.zeros((nblk, SUBLANES, LANES), F32).at[:, :CONV_W].set(
        conv_w.reshape(CONV_W, nblk, LANES).transpose(1, 0, 2))
    return pl.pallas_call(
        _dn_feature_kernel, grid=(b, nblk),
        in_specs=[pl.BlockSpec((1, l, LANES), lambda bi, j: (bi, 0, j)),
                  pl.BlockSpec((1, SUBLANES, LANES), lambda bi, j: (j, 0, 0))],
        out_specs=pl.BlockSpec((1, l, LANES), lambda bi, j: (bi, 0, j)),
        out_shape=jax.ShapeDtypeStruct((b, l, n), F32),
        compiler_params=_cparams("parallel", "arbitrary"), name="dn_features",
    )(qkv, w)


def _scan_chunks(rev, n_ctx, n_lat):
    if rev:
        ctx = lambda j: jnp.maximum(n_ctx - 1 - j, 0)
        lat = lambda j: jnp.where(j < n_ctx, n_lat - 1, n_lat - 1 - (j - n_ctx))
    else:
        ctx = lambda j: jnp.minimum(j, n_ctx - 1)
        lat = lambda j: jnp.maximum(j - n_ctx, 0)
    return ctx, lat


def _dn_scan_kernel(rev, dirn, n_ctx, alog_ref, dtb_ref, fc_ref, fl_ref, bac_ref, bal_ref, o_ref, s_ref):
    step = pl.program_id(1)

    @pl.when(step == 0)
    def _():
        s_ref[...] = jnp.zeros_like(s_ref)

    in_ctx = step < n_ctx
    f = jnp.where(in_ctx, fc_ref[0], fl_ref[0])
    ba = jnp.where(in_ctx, bac_ref[0], bal_ref[0])
    nh, hd, n = DN_HEADS, DN_HEADS * DN_DK, DN_HEADS * CHUNK
    stack = lambda base, w: jnp.concatenate([f[:, base + h * w:base + (h + 1) * w] for h in range(nh)], axis=0)
    q_s, k_s, v_s = stack(0, DN_DK), stack(hd, DN_DK), stack(2 * hd, DN_DV)
    incl, _ = _tri_masks(rev)
    beta_all = jax.nn.sigmoid(ba)
    g_all = -jnp.exp(alog_ref[...]) * _softplus(ba + dtb_ref[...])
    gc_all = _mask_dot(incl.astype(BF16), g_all)
    gc_t = jnp.concatenate([gc_all, gc_all], axis=0).T
    g_tot = jnp.sum(g_all, axis=0, keepdims=True)
    cb = [dirn * nh + h for h in range(nh)]
    cg = [2 * nh + c for c in cb]
    col = lambda a, cs: jnp.concatenate([a[:, c:c + 1] for c in cs], axis=0)
    beta_c, gc_c = col(beta_all, cb), col(gc_all, cg)
    gtot_c = jnp.concatenate([jnp.broadcast_to(g_tot[:, c:c + 1], (CHUNK, 1)) for c in cg], axis=0)
    gc_r = jnp.concatenate([gc_t[c:c + 1, :CHUNK] for c in cg], axis=1)
    r = lax.broadcasted_iota(jnp.int32, (n, n), 0)
    c = lax.broadcasted_iota(jnp.int32, (n, n), 1)
    same = (r // CHUNK) == (c // CHUNK)
    d = (c - r) if rev else (r - c)
    incl_bd, strict_bd = same & (d >= 0), same & (d > 0)
    eye = (r == c).astype(F32)
    decay = jnp.where(incl_bd, jnp.exp(jnp.where(incl_bd, gc_c - gc_r, 0.0)), 0.0)
    kb_s = k_s * beta_c
    lower = jnp.where(strict_bd, _dot_nt(kb_s, k_s) * decay, 0.0)
    eg_c = jnp.exp(gc_c)
    inv = eye - lower
    pw = lower
    for _ in range(5):
        pw = _dot(pw, pw)
        inv = inv + _dot(inv, pw)
    sol = _dot(inv, jnp.concatenate([v_s * beta_c, kb_s * eg_c], axis=-1))
    u_s, w_s = sol[:, :DN_DV], sol[:, DN_DV:]
    k_dec = k_s * jnp.exp(gtot_c - gc_c)
    rb = lax.broadcasted_iota(jnp.int32, (n, DN_DK), 0) // CHUNK
    expand = lambda x: jnp.concatenate([jnp.where(rb == h, x, 0.0) for h in range(nh)], axis=1)
    s = s_ref[...]
    v_new = u_s - _dot(expand(w_s), s)
    a_qk = _dot_nt(q_s, k_s) * decay
    o_s = _dot(expand(q_s * eg_c), s) + _dot(a_qk, v_new)
    gl_rows = jnp.concatenate([jnp.broadcast_to(jnp.exp(g_tot[:, cc:cc + 1]), (DN_DK, 1)) for cc in cg], axis=0)
    s_ref[...] = s * gl_rows + _dot_tn(expand(k_dec), v_new)
    o_ref[0] = jnp.concatenate([o_s[h * CHUNK:(h + 1) * CHUNK] for h in range(nh)], axis=1)


def dn_scan_tc(feat_c, feat_l, ba_c, ba_l, a_log, dt_bias, rev):
    b, l, nf = feat_l.shape
    n_ctx, n_lat = feat_c.shape[1] // CHUNK, l // CHUNK
    dirn = 1 if rev else 0
    cc, lc = _scan_chunks(rev, n_ctx, n_lat)
    lanes = lambda p: jnp.zeros((1, LANES), F32).at[0, 2 * DN_HEADS:4 * DN_HEADS].set(p.reshape(-1))
    vec = pl.BlockSpec((1, LANES), lambda bi, j: (0, 0))
    return pl.pallas_call(
        functools.partial(_dn_scan_kernel, rev, dirn, n_ctx), grid=(b, n_ctx + n_lat),
        in_specs=[vec, vec,
                  pl.BlockSpec((1, CHUNK, nf), lambda bi, j: (bi, cc(j), 0)),
                  pl.BlockSpec((1, CHUNK, nf), lambda bi, j: (bi, lc(j), 0)),
                  pl.BlockSpec((1, CHUNK, LANES), lambda bi, j: (bi, cc(j), 0)),
                  pl.BlockSpec((1, CHUNK, LANES), lambda bi, j: (bi, lc(j), 0))],
        out_specs=pl.BlockSpec((1, CHUNK, DN_HEADS * DN_DV), lambda bi, j: (bi, lc(j), 0)),
        out_shape=jax.ShapeDtypeStruct((b, l, DN_HEADS * DN_DV), F32),
        scratch_shapes=[pltpu.VMEM((DN_HEADS * DN_DK, DN_DV), F32)],
        compiler_params=_cparams("parallel", "arbitrary"), name="dn_scan_bwd" if rev else "dn_scan_fwd",
    )(lanes(a_log), lanes(dt_bias), feat_c, feat_l, ba_c, ba_l)


def _from_grid_cols(blk, n):
    cols = blk.shape[1] // n
    return jnp.concatenate([blk[:, i * n:(i + 1) * n] for i in range(cols)], axis=0)


def _gla_scan_kernel(rev, dirn, n_ctx, qkc_ref, vc_ref, lrc_ref, qkl_ref, vl_ref, lrl_ref,
                     wla_ref, bla_ref, o_ref, s_ref):
    step = pl.program_id(1)

    @pl.when(step == 0)
    def _():
        s_ref[...] = jnp.zeros_like(s_ref)

    in_ctx = step < n_ctx
    hk, hv = GLA_HEADS * GLA_DK, GLA_HEADS * GLA_DV
    qk = jnp.where(in_ctx, qkc_ref[0], _from_grid_cols(qkl_ref[0], 2 * hk))
    vv = jnp.where(in_ctx, vc_ref[0], _from_grid_cols(vl_ref[0], hv))
    lr = jnp.where(in_ctx, lrc_ref[0], _from_grid_cols(lrl_ref[0], LANES))
    incl, _ = _tri_masks(rev)
    incl_b = incl.astype(BF16)
    pre = _dot(lr, wla_ref[0]) + bla_ref[0]
    la_all = -_softplus(-pre) * (1.0 / GLA_TAU)
    bc_all = _mask_dot(incl_b, la_all)
    b_tot_all = jnp.sum(la_all, axis=0, keepdims=True)
    outs = []
    for h in range(GLA_HEADS):
        q = qk[:, h * GLA_DK:(h + 1) * GLA_DK] * GLA_DK ** -0.5
        k = qk[:, hk + h * GLA_DK:hk + (h + 1) * GLA_DK]
        v = vv[:, h * GLA_DV:(h + 1) * GLA_DV]
        bc = bc_all[:, h * GLA_DK:(h + 1) * GLA_DK]
        b_tot = b_tot_all[:, h * GLA_DK:(h + 1) * GLA_DK]
        st = s_ref[h]
        o = _dot_nt(q * jnp.exp(bc), st)
        parts = []
        for i in range(CHUNK // GLA_SUB):
            lo_r, hi_r = i * GLA_SUB, (i + 1) * GLA_SUB
            if rev:
                ref = bc[hi_r - 1:hi_r]
                c0, c1 = lo_r, CHUNK
            else:
                ref = bc[lo_r:lo_r + 1]
                c0, c1 = 0, hi_r
            qi = q[lo_r:hi_r] * jnp.exp(bc[lo_r:hi_r] - ref)
            ki = k[c0:c1] * jnp.exp(ref - bc[c0:c1])
            att = _dot_nt(qi, ki)
            rg = lax.broadcasted_iota(jnp.int32, (GLA_SUB, c1 - c0), 0) + lo_r
            cg = lax.broadcasted_iota(jnp.int32, (GLA_SUB, c1 - c0), 1) + c0
            keep = (cg > rg) if rev else (cg < rg)
            parts.append(_dot(jnp.where(keep, att, 0.0), v[c0:c1]))
        diag = jnp.sum(q * k, axis=-1, keepdims=True) * v
        outs.append(o + jnp.concatenate(parts, axis=0) + diag)
        k_dec = k * jnp.exp(b_tot - bc)
        s_ref[h] = st * jnp.exp(b_tot) + _dot_tn(v, k_dec)
    o = jnp.concatenate(outs, axis=-1)
    rows = o_ref.shape[1]
    o_ref[0] = jnp.concatenate([o[i * rows:(i + 1) * rows] for i in range(CHUNK // rows)], axis=-1)


def gla_scan_tc(qk_c, v_c, lr_c, qk_l, v_l, lr_l, w_la, b_la, rev):
    b, l, _ = qk_l.shape
    rows = l // GRID_W
    cols = CHUNK // rows
    n_ctx, n_lat = qk_c.shape[1] // CHUNK, l // CHUNK
    dirn = 1 if rev else 0
    cc, lc = _scan_chunks(rev, n_ctx, n_lat)
    hv = GLA_HEADS * GLA_DV
    ctx_blk = lambda a: pl.BlockSpec((1, CHUNK, a.shape[2]), lambda bi, j: (bi, cc(j), 0))
    lat_blk = lambda n: pl.BlockSpec((1, rows, cols * n), lambda bi, j: (bi, 0, lc(j)))
    grid_view = lambda a: a.reshape(b, rows, GRID_W * a.shape[2])
    out = pl.pallas_call(
        functools.partial(_gla_scan_kernel, rev, dirn, n_ctx), grid=(b, n_ctx + n_lat),
        in_specs=[ctx_blk(qk_c), ctx_blk(v_c), ctx_blk(lr_c),
                  lat_blk(qk_l.shape[2]), lat_blk(v_l.shape[2]), lat_blk(lr_l.shape[2]),
                  pl.BlockSpec((1,) + w_la.shape[1:], lambda bi, j: (dirn, 0, 0)),
                  pl.BlockSpec((1,) + b_la.shape[1:], lambda bi, j: (dirn, 0, 0))],
        out_specs=lat_blk(hv),
        out_shape=jax.ShapeDtypeStruct((b, rows, GRID_W * hv), F32),
        scratch_shapes=[pltpu.VMEM((GLA_HEADS, GLA_DV, GLA_DK), F32)],
        compiler_params=_cparams("parallel", "arbitrary"), name="gla_scan_bwd" if rev else "gla_scan_fwd",
    )(qk_c, v_c, lr_c, grid_view(qk_l), grid_view(v_l), grid_view(lr_l), w_la, b_la)
    return out.reshape(b, l, hv)


def _head_norm_gate(o, gate, g, n_heads, dv):
    parts = []
    for h in range(n_heads):
        oh = o[:, h * dv:(h + 1) * dv]
        gh = gate[:, h * dv:(h + 1) * dv]
        yh = oh * lax.rsqrt(jnp.mean(oh * oh, axis=-1, keepdims=True) + EPS) * g
        parts.append(yh * (gh * jax.nn.sigmoid(gh)))
    return parts


def _mix_out_kernel(x_ref, dnf_ref, dnb_ref, z_ref, glf_ref, glb_ref, r_ref, mod_ref, dng_ref,
                    glg_ref, n2g_ref, wout_ref, wq_ref, x1_ref, h2_ref, q_ref):
    parts = (_head_norm_gate(dnf_ref[0] + dnb_ref[0], z_ref[0], dng_ref[...], DN_HEADS, DN_DV)
             + _head_norm_gate(glf_ref[0] + glb_ref[0], r_ref[0], glg_ref[...], GLA_HEADS, GLA_DV))
    y = jnp.dot(jnp.concatenate(parts, axis=-1).astype(BF16), wout_ref[...], preferred_element_type=F32)
    x1 = x_ref[0] + mod_ref[0, 2:3, :] * y
    x1_ref[0] = x1
    n = x1 * lax.rsqrt(jnp.mean(x1 * x1, axis=-1, keepdims=True) + EPS) * n2g_ref[...]
    h2 = n * (1.0 + mod_ref[0, 4:5, :]) + mod_ref[0, 3:4, :]
    h2_ref[0] = h2
    h_hi, h_lo = _split(h2)
    q_ref[0] = (jnp.dot(h_hi, wq_ref[0], preferred_element_type=F32)
                + jnp.dot(h_lo, wq_ref[0], preferred_element_type=F32)
                + jnp.dot(h_hi, wq_ref[1], preferred_element_type=F32))


def mix_out_tc(x, dn_f, dn_b, z, gl_f, gl_b, r, mod_l, dn_g, gla_g, n2_g, w_out, w_q):
    b, l, d = x.shape
    tok = lambda n: pl.BlockSpec((1, TOK_TILE, n), lambda bi, i: (bi, i, 0))
    full = lambda a: pl.BlockSpec(a.shape, lambda bi, i: (0,) * a.ndim)
    dn_g, gla_g, n2_g = dn_g.reshape(1, -1), gla_g.reshape(1, -1), n2_g.reshape(1, -1)
    nq = w_q.shape[2]
    return pl.pallas_call(
        _mix_out_kernel, grid=(b, l // TOK_TILE),
        in_specs=[tok(d), tok(dn_f.shape[2]), tok(dn_b.shape[2]), tok(z.shape[2]),
                  tok(gl_f.shape[2]), tok(gl_b.shape[2]), tok(r.shape[2]),
                  pl.BlockSpec((1,) + mod_l.shape[1:], lambda bi, i: (bi, 0, 0)),
                  full(dn_g), full(gla_g), full(n2_g), full(w_out), full(w_q)],
        out_specs=[tok(d), tok(d), tok(nq)],
        out_shape=[jax.ShapeDtypeStruct((b, l, d), F32), jax.ShapeDtypeStruct((b, l, d), F32),
                   jax.ShapeDtypeStruct((b, l, nq), F32)],
        compiler_params=_cparams("parallel", "arbitrary"), name="mix_out",
    )(x, dn_f, dn_b, z, gl_f, gl_b, r, mod_l, dn_g, gla_g, n2_g, w_out, w_q)


def _top_rows(s, k, payload=None):
    n = s.shape[0]
    row = lax.broadcasted_iota(jnp.int32, s.shape, 0).astype(F32)
    vals, picked = [], []
    for _ in range(k):
        m = jnp.max(s, axis=0, keepdims=True)
        first = jnp.min(jnp.where(s == m, row, float(n)), axis=0, keepdims=True)
        sel = row == first
        vals.append(m)
        if payload is None:
            picked.append(first)
        else:
            picked.append(jnp.max(jnp.where(sel, payload, -1.0), axis=0, keepdims=True))
        s = jnp.where(sel, -jnp.inf, s)
    return jnp.concatenate(vals, axis=0), jnp.concatenate(picked, axis=0)


def _candidate_rows(s0, i0, s1, i1):
    k = s0.shape[0]
    wide = SUBLANES
    blocks_s = [s0[0:1] + s1]
    blocks_i = [i0[0:1] * float(PEER_NKEYS) + i1]
    col = lax.broadcasted_iota(jnp.int32, (wide, s0.shape[1]), 0)
    for i in range(1, wide):
        keep = col < (k // (i + 1))
        blocks_s.append(jnp.where(keep, s0[i:i + 1] + s1[0:wide], -jnp.inf))
        blocks_i.append(i0[i:i + 1] * float(PEER_NKEYS) + i1[0:wide])
    blocks_s.append(s0[wide:k] + s1[0:1])
    blocks_i.append(i0[wide:k] * float(PEER_NKEYS) + i1[0:1])
    return jnp.concatenate(blocks_s, axis=0), jnp.concatenate(blocks_i, axis=0)


def _select_kernel(q_ref, k_ref, idx_ref, gate_ref, idx_s, gate_s):
    half = PEER_DQ // 2

    def head(h, carry):
        tops = []
        for p in range(2):
            qp = q_ref[:, pl.ds(pl.multiple_of(h * PEER_DQ + p * half, half), half)]
            s = lax.dot_general(k_ref[h, p], qp, (((1,), (1,)), ((), ())),
                                preferred_element_type=F32,
                                precision=lax.Precision.HIGHEST)
            tops.append(_top_rows(s, PEER_TOPK))
        (s0, i0), (s1, i1) = tops
        cand_s, cand_i = _candidate_rows(s0, i0, s1, i1)
        best_s, idx = _top_rows(cand_s, PEER_TOPK, payload=cand_i)
        e = jnp.exp(best_s - best_s[0:1])
        r0 = pl.multiple_of(h * PEER_TOPK, PEER_TOPK)
        idx_s[pl.ds(r0, PEER_TOPK), :] = idx
        gate_s[pl.ds(r0, PEER_TOPK), :] = e / jnp.sum(e, axis=0, keepdims=True)
        return carry

    lax.fori_loop(0, PEER_HEADS, head, 0)
    idx_ref[...] = idx_s[...].T.astype(jnp.int32)
    gate_ref[...] = gate_s[...].T


def peer_select_tc(q, keys):
    n_tok = q.shape[0]
    out_spec = pl.BlockSpec((SELECT_TILE, PICKS), lambda i: (i, 0))
    return pl.pallas_call(
        _select_kernel,
        grid=(n_tok // SELECT_TILE,),
        in_specs=[pl.BlockSpec((SELECT_TILE, q.shape[1]), lambda i: (i, 0)),
                  pl.BlockSpec(keys.shape, lambda i: (0, 0, 0, 0))],
        out_specs=[out_spec, out_spec],
        out_shape=[jax.ShapeDtypeStruct((n_tok, PICKS), jnp.int32),
                   jax.ShapeDtypeStruct((n_tok, PICKS), F32)],
        scratch_shapes=[pltpu.VMEM((PICKS, SELECT_TILE), F32), pltpu.VMEM((PICKS, SELECT_TILE), F32)],
        compiler_params=_cparams("parallel"), name="peer_select",
    )(q, keys)


def _sc_mesh():
    return plsc.VectorSubcoreMesh(core_axis_name="c", subcore_axis_name="s")


def _sc_pipeline(body, n_steps, in_specs, out_specs, operands):
    pltpu.emit_pipeline(
        body, grid=(n_steps,), in_specs=in_specs, out_specs=out_specs,
        core_axis_name=("c", "s"), dimension_semantics=(pltpu.PARALLEL,),
        trace_scopes=False,
    )(*operands)


def pack_pairs(t):
    half = t.shape[1] // 2
    bits = lax.bitcast_convert_type(t, jnp.uint32)
    rne = lambda b: (b + jnp.uint32(0x7FFF) + ((b >> 16) & jnp.uint32(1))) >> 16
    word = (rne(bits[:, half:]) << 16) | rne(bits[:, :half])
    return lax.bitcast_convert_type(word, jnp.int32)


def peer_act_partial_sc(u_q, idx2, h_q):
    n_groups, (n_tok, half) = idx2.shape[0], h_q.shape
    nsub = TOK_STEP * GROUPS_PER_TOK
    n_chunks = half // SC_LANES
    as_pairs = lambda w: plsc.bitcast(w, BF16)

    @functools.partial(
        pl.kernel, mesh=_sc_mesh(), compiler_params=pltpu.CompilerParams(needs_layout_passes=False),
        out_type=jax.ShapeDtypeStruct((n_groups, PICK_GROUP * SC_LANES), F32),
        scratch_types=[pltpu.VMEM((GATHER_BUFS, PICK_GROUP, half), jnp.int32),
                       pltpu.SemaphoreType.DMA((GATHER_BUFS,))],
    )
    def k(u_hbm, i_hbm, h_hbm, o_hbm, rows, sems):
        def body(i_v, h_v, o_v):
            def fetch(j, slot):
                return pltpu.make_async_copy(u_hbm.at[i_v.at[j]], rows.at[slot], sems.at[slot])

            for ahead in range(GATHER_BUFS - 1):
                fetch(ahead, ahead).start()

            def sub(j, carry):
                slot = j % GATHER_BUFS

                nxt = j + GATHER_BUFS - 1

                @pl.when(nxt < nsub)
                def _():
                    fetch(nxt, nxt % GATHER_BUFS).start()

                fetch(j, slot).wait()
                t = j // GROUPS_PER_TOK

                def picks(g, carry2):
                    kb = g * ACT_UNROLL
                    accs = [None] * ACT_UNROLL
                    for c0 in range(0, n_chunks, ACT_BF16_TERMS):
                        cols = [pl.ds((c0 + cc) * SC_LANES, SC_LANES) for cc in range(ACT_BF16_TERMS)]
                        hs = [as_pairs(h_v[t, col]) for col in cols]
                        for i in range(ACT_UNROLL):
                            s = None
                            for col, hv in zip(cols, hs):
                                p = as_pairs(rows[slot, kb + i, col]) * hv
                                s = p if s is None else s + p
                            lo, hi = plsc.unpack(s, format=plsc.PackFormat.INTERLEAVED)
                            p32 = lo + hi
                            accs[i] = p32 if accs[i] is None else accs[i] + p32
                    for i in range(ACT_UNROLL):
                        o_v[j, pl.ds((kb + i) * SC_LANES, SC_LANES)] = accs[i]
                    return carry2

                lax.fori_loop(0, PICK_GROUP // ACT_UNROLL, picks, 0)
                return carry

            lax.fori_loop(0, nsub, sub, 0)

        _sc_pipeline(
            body, n_tok // TOK_STEP,
            [pl.BlockSpec((nsub, PICK_GROUP), lambda i: (i, 0)),
             pl.BlockSpec((TOK_STEP, half), lambda i: (i, 0))],
            [pl.BlockSpec((nsub, PICK_GROUP * SC_LANES), lambda i: (i, 0))],
            (i_hbm, h_hbm, o_hbm))

    return k(u_q, idx2, h_q)


def peer_combine_sc(v_q, idx2, coef_q):
    n_groups = idx2.shape[0]
    half = v_q.shape[1]
    nsub = COMB_GROUP_STEP
    n_chunks = half // SC_LANES
    as_pairs = lambda w: plsc.bitcast(w, BF16)

    @functools.partial(
        pl.kernel, mesh=_sc_mesh(), compiler_params=pltpu.CompilerParams(needs_layout_passes=False),
        out_type=jax.ShapeDtypeStruct((n_groups, 2 * half), F32),
        scratch_types=[pltpu.VMEM((GATHER_BUFS, PICK_GROUP, half), jnp.int32),
                       pltpu.SemaphoreType.DMA((GATHER_BUFS,))],
    )
    def k(v_hbm, i_hbm, c_hbm, o_hbm, rows, sems):
        def body(i_v, c_v, o_v):
            def fetch(j, slot):
                return pltpu.make_async_copy(v_hbm.at[i_v.at[j]], rows.at[slot], sems.at[slot])

            for ahead in range(GATHER_BUFS - 1):
                fetch(ahead, ahead).start()

            def sub(j, carry0):
                slot = j % GATHER_BUFS

                nxt = j + GATHER_BUFS - 1

                @pl.when(nxt < nsub)
                def _():
                    fetch(nxt, nxt % GATHER_BUFS).start()

                fetch(j, slot).wait()
                cks = [as_pairs(c_v[j, pl.ds(kk * SC_LANES, SC_LANES)]) for kk in range(PICK_GROUP)]

                def chunk_pair(c2, carry):
                    sums = []
                    for cc in range(COMB_CHUNKS):
                        l = (c2 * COMB_CHUNKS + cc) * SC_LANES
                        tot_lo, tot_hi = None, None
                        for k0 in range(0, PICK_GROUP, COMB_BF16_TERMS):
                            s = None
                            for kk in range(k0, k0 + COMB_BF16_TERMS):
                                p = cks[kk] * as_pairs(rows[slot, kk, pl.ds(l, SC_LANES)])
                                s = p if s is None else s + p
                            lo, hi = plsc.unpack(s, format=plsc.PackFormat.INTERLEAVED)
                            tot_lo = lo if tot_lo is None else tot_lo + lo
                            tot_hi = hi if tot_hi is None else tot_hi + hi
                        sums.append((l, tot_lo, tot_hi))
                    for l, tot_lo, tot_hi in sums:
                        o_v[j, pl.ds(l, SC_LANES)] = tot_lo
                        o_v[j, pl.ds(half + l, SC_LANES)] = tot_hi
                    return carry

                lax.fori_loop(0, n_chunks // COMB_CHUNKS, chunk_pair, 0)
                return carry0

            lax.fori_loop(0, nsub, sub, 0)

        _sc_pipeline(
            body, n_groups // nsub,
            [pl.BlockSpec((nsub, PICK_GROUP), lambda i: (i, 0)),
             pl.BlockSpec((nsub, PICK_GROUP * SC_LANES), lambda i: (i, 0))],
            [pl.BlockSpec((nsub, 2 * half), lambda i: (i, 0))],
            (i_hbm, c_hbm, o_hbm))

    return k(v_q, idx2, coef_q)


def _segment_matrix():
    r = lax.broadcasted_iota(jnp.int32, (PICK_GROUP * SC_LANES, PICK_GROUP), 0) // SC_LANES
    c = lax.broadcasted_iota(jnp.int32, (PICK_GROUP * SC_LANES, PICK_GROUP), 1)
    return (r == c).astype(F32)


def _coef_kernel(part_ref, gate_ref, o_ref):
    seg = _segment_matrix()
    act = jnp.dot(part_ref[...], seg, preferred_element_type=F32, precision=lax.Precision.HIGHEST)
    coef = gate_ref[...] * (0.5 * act * (1.0 + lax.erf(act * (2.0 ** -0.5))))
    wide = lax.dot_general(coef, seg, (((1,), (1,)), ((), ())), preferred_element_type=F32,
                           precision=lax.Precision.HIGHEST)
    bits = pltpu.bitcast(wide, jnp.uint32)
    r = (bits + jnp.uint32(0x7FFF) + ((bits >> 16) & jnp.uint32(1))) >> 16
    o_ref[...] = pltpu.bitcast(r | (r << 16), jnp.int32)


def peer_coef_tc(part, gate2):
    n_groups, width = part.shape
    tile = 1024
    return pl.pallas_call(
        _coef_kernel,
        grid=(n_groups // tile,),
        in_specs=[pl.BlockSpec((tile, width), lambda i: (i, 0)),
                  pl.BlockSpec((tile, PICK_GROUP), lambda i: (i, 0))],
        out_specs=pl.BlockSpec((tile, width), lambda i: (i, 0)),
        out_shape=jax.ShapeDtypeStruct((n_groups, width), jnp.int32),
        compiler_params=_cparams("parallel"), name="peer_coef",
    )(part, gate2)


def _final_kernel(x_ref, y_ref, mod_ref, g_ref, o_ref):
    d = x_ref.shape[2]
    y = y_ref[0, :, 0:d]
    for p in range(1, y_ref.shape[2] // d):
        y = y + y_ref[0, :, p * d:(p + 1) * d]
    x = x_ref[0] + mod_ref[0, 5:6, :] * y
    o_ref[0] = x * lax.rsqrt(jnp.mean(x * x, axis=-1, keepdims=True) + EPS) * g_ref[...]


def final_tc(x1, y_parts, mod_l, final_g):
    b, l, d = x1.shape
    tok = pl.BlockSpec((1, TOK_TILE, d), lambda bi, i: (bi, i, 0))
    return pl.pallas_call(
        _final_kernel, grid=(b, l // TOK_TILE),
        in_specs=[tok, pl.BlockSpec((1, TOK_TILE, y_parts.shape[2]), lambda bi, i: (bi, i, 0)),
                  pl.BlockSpec((1,) + mod_l.shape[1:], lambda bi, i: (bi, 0, 0)),
                  pl.BlockSpec((1, d), lambda bi, i: (0, 0))],
        out_specs=tok, out_shape=jax.ShapeDtypeStruct((b, l, d), F32),
        compiler_params=_cparams("parallel", "arbitrary"), name="final_norm",
    )(x1, y_parts, mod_l, final_g.reshape(1, d))


def _pad_cols(w, n):
    return jnp.pad(w, ((0, 0), (0, n - w.shape[1])))


def _slice_sizes(b):
    half = SLICE_BATCH // 2
    full = (b - (1 + DRAIN_HALVES) * half) // SLICE_BATCH
    sizes = [half] + [SLICE_BATCH] * full + [half] * DRAIN_HALVES
    return sizes if SLICE_BATCH % 2 == 0 and full >= 0 and sum(sizes) == b else [1] * b


def _exact_zero(v):
    return jnp.minimum(jnp.abs(v), 0.0)


def forward(x, c, ctx, c_ctx, w_ada, b_ada, norm1_g, norm2_g, w_in, conv_w, dn_a_log,
            dn_dt_bias, dn_norm_g, gla_wa2, gla_ba, gla_norm_g, w_out, peer_wq, peer_keys,
            peer_u, peer_v, final_g):
    b, l, d = x.shape
    c_all = jnp.concatenate([c, c_ctx[None]], axis=0)
    c_all = jnp.pad(c_all, ((0, (-c_all.shape[0]) % SUBLANES), (0, 0)))
    mod = adaln_mod(c_all, w_ada, b_ada)
    mod_l = mod[:b].reshape(b, 6, d)
    mod_c = jnp.broadcast_to(mod[b].reshape(1, 6, d), (b, 6, d))
    o = DN_QKV
    hv = DN_HEADS * DN_DV
    w_dn_qkv, w_dn_z = w_in[:, :o], w_in[:, o:o + hv]
    w_dn_ba = _pad_cols(w_in[:, o + hv:DN_COLS], LANES)
    g0 = DN_COLS
    gqk, gv = 2 * GLA_HEADS * GLA_DK, GLA_HEADS * GLA_DV
    w_gl_qk, w_gl_v = w_in[:, g0:g0 + gqk], w_in[:, g0 + gqk:g0 + gqk + gv]
    w_gl_r = w_in[:, g0 + gqk + gv:g0 + gqk + 2 * gv]
    w_gl_lr = _pad_cols(w_in[:, g0 + gqk + 2 * gv:], LANES)
    w_lat = [w.astype(BF16) for w in (w_dn_qkv, w_dn_ba, w_gl_qk, w_gl_v, w_gl_lr, w_dn_z, w_gl_r)]
    w_ctx = w_lat[:5]
    w_la = jnp.zeros((2, LANES, GLA_HEADS * GLA_DK), F32)
    for dd in range(2):
        w_la = w_la.at[dd, dd * GLA_LR:(dd + 1) * GLA_LR].set(gla_wa2[dd])
    w_la = w_la.astype(BF16)
    b_la = gla_ba.reshape(2, 1, GLA_HEADS * GLA_DK)
    w_out_b, w_q_b = w_out.astype(BF16), jnp.stack(_split(peer_wq))
    u_q, v_q = pack_pairs(peer_u), pack_pairs(peer_v)

    def mixer(xg, ctxg, mod_cg, mod_g, after_select, after_combine):
        mod_cg, mod_g = mod_cg + after_select, mod_g + after_select
        c_qkv, c_ba, c_qk, c_v, c_lr = in_projection(ctxg, norm1_g, mod_cg[:, 0:2], w_ctx)
        l_qkv, l_ba, l_qk, l_v, l_lr, l_z, l_r = in_projection(xg, norm1_g, mod_g[:, 0:2], w_lat)
        feat_c, feat_l = dn_features_tc(c_qkv, conv_w), dn_features_tc(l_qkv, conv_w)
        dn_f = dn_scan_tc(feat_c, feat_l, c_ba, l_ba, dn_a_log, dn_dt_bias, rev=False)
        dn_b = dn_scan_tc(feat_c, feat_l, c_ba, l_ba, dn_a_log, dn_dt_bias, rev=True)
        gl_f = gla_scan_tc(c_qk, c_v, c_lr, l_qk, l_v, l_lr, w_la, b_la, rev=False)
        gl_b = gla_scan_tc(c_qk, c_v, c_lr, l_qk, l_v, l_lr, w_la, b_la, rev=True)
        return mix_out_tc(xg, dn_f, dn_b, l_z, gl_f, gl_b, l_r, mod_g, dn_norm_g + after_combine,
                          gla_norm_g, norm2_g, w_out_b, w_q_b)

    def select_and_act(h2, q, after_coef):
        n_tok = h2.shape[0] * l
        idx, gate = peer_select_tc(q.reshape(n_tok, -1), peer_keys + after_coef)
        idx2 = idx.reshape(n_tok * GROUPS_PER_TOK, PICK_GROUP)
        gate2 = gate.reshape(n_tok * GROUPS_PER_TOK, PICK_GROUP)
        return idx2, gate2, peer_act_partial_sc(u_q, idx2, pack_pairs(h2.reshape(n_tok, d)))

    def coef_and_combine(st, after_mixer):
        coef_q = peer_coef_tc(st["part"], st["gate2"] + after_mixer)
        st["y"] = peer_combine_sc(v_q, st["idx2"], coef_q)
        return _exact_zero(coef_q[0, 0].astype(F32))

    zero = jnp.zeros((), F32)
    slices, z_sel, z_coef = [], zero, zero
    i = 0
    for bg in _slice_sizes(b):
        g = len(slices)
        z_comb = _exact_zero(slices[g - 2]["y"][0, 0]) if g >= 2 else zero
        mod_g = mod_l[i:i + bg]
        x1, h2, q = mixer(x[i:i + bg], ctx[i:i + bg], mod_c[i:i + bg], mod_g, z_sel, z_comb)
        i += bg
        if g >= 1:
            z_coef = coef_and_combine(slices[g - 1], _exact_zero(x1[0, 0, 0]))
        idx2, gate2, part = select_and_act(h2, q, z_coef)
        z_sel = _exact_zero(gate2[0, 0])
        slices.append(dict(x1=x1, mod=mod_g, idx2=idx2, gate2=gate2, part=part))
    coef_and_combine(slices[-1], zero)
    outs = [final_tc(st["x1"], st["y"].reshape(st["x1"].shape[:2] + (-1,)), st["mod"], final_g) for st in slices]
    return jnp.concatenate(outs, axis=0)


def kernel(x, c, ctx, c_ctx, w_ada, b_ada, norm1_g, norm2_g, w_in, conv_w, dn_a_log,
           dn_dt_bias, dn_norm_g, gla_wa2, gla_ba, gla_norm_g, w_out, peer_wq, peer_keys,
           peer_u, peer_v, final_g):
    assert w_ada.shape[0] == 1, "single-layer block: the context stream is only consumed, never updated"
    return forward(x, c, ctx, c_ctx, w_ada[0], b_ada[0], norm1_g[0], norm2_g[0], w_in[0], conv_w[0],
                   dn_a_log[0], dn_dt_bias[0], dn_norm_g[0], gla_wa2[0], gla_ba[0], gla_norm_g[0],
                   w_out[0], peer_wq[0], peer_keys[0], peer_u[0], peer_v[0], final_g)
```

```python
import functools

import jax
import jax.numpy as jnp
from jax import lax
from jax.experimental import pallas as pl
from jax.experimental.pallas import tpu as pltpu
from jax.experimental.pallas import tpu_sc as plsc

GRID_W = 64
DN_HEADS = 4
DN_DK = 128
DN_DV = 128
CONV_W = 5
GLA_HEADS = 4
GLA_DK = 64
GLA_DV = 128
GLA_LR = 16
GLA_TAU = 16.0
CHUNK = 64
PEER_HEADS = 8
PEER_NKEYS = 128
PEER_DQ = 256
PEER_TOPK = 16
EPS = 1e-6
DN_QKV = 2 * DN_HEADS * DN_DK + DN_HEADS * DN_DV
DN_COLS = DN_QKV + DN_HEADS * DN_DV + 4 * DN_HEADS

SUBLANES = 8
LANES = 128
SC_LANES = 16
VMEM_LIMIT_BYTES = 48 * 1024 * 1024

TOK_TILE = 256
SELECT_TILE = 512
GLA_SUB = 16
PICK_GROUP = 32
COMB_GROUP_STEP = 16
GATHER_BUFS = 3
TOK_STEP = 8
COMB_BF16_TERMS = 4
COMB_CHUNKS = 4
ACT_BF16_TERMS = 4
ACT_UNROLL = 16
SLICE_BATCH = 2
PICKS = PEER_HEADS * PEER_TOPK
GROUPS_PER_TOK = PICKS // PICK_GROUP

F32 = jnp.float32
BF16 = jnp.bfloat16


def _cparams(*semantics):
    return pltpu.CompilerParams(dimension_semantics=semantics, vmem_limit_bytes=VMEM_LIMIT_BYTES)


def _dot(a, b):
    return jnp.dot(a.astype(BF16), b.astype(BF16), preferred_element_type=F32)


def _dot_nt(a, b):
    return lax.dot_general(a.astype(BF16), b.astype(BF16), (((1,), (1,)), ((), ())),
                           preferred_element_type=F32)


def _dot_tn(a, b):
    return lax.dot_general(a.astype(BF16), b.astype(BF16), (((0,), (0,)), ((), ())),
                           preferred_element_type=F32)


def _split(x):
    hi = x.astype(BF16)
    return hi, (x - hi.astype(F32)).astype(BF16)


def _mask_dot(mask_bf16, x):
    hi, lo = _split(x)
    return (jnp.dot(mask_bf16, hi, preferred_element_type=F32)
            + jnp.dot(mask_bf16, lo, preferred_element_type=F32))


def _softplus(x):
    return jnp.maximum(x, 0.0) + jnp.log(1.0 + jnp.exp(-jnp.abs(x)))


def _tri_masks(rev):
    r = lax.broadcasted_iota(jnp.int32, (CHUNK, CHUNK), 0)
    c = lax.broadcasted_iota(jnp.int32, (CHUNK, CHUNK), 1)
    d = (c - r) if rev else (r - c)
    return d >= 0, d > 0


def _mod_kernel(c_ref, w_ref, b_ref, o_ref):
    c = c_ref[...]
    s = c * jax.nn.sigmoid(c)
    o_ref[...] = jnp.dot(s, w_ref[...], preferred_element_type=F32,
                         precision=lax.Precision.HIGHEST) + b_ref[...]


def adaln_mod(c_all, w_ada, b_ada):
    r, d = c_all.shape
    n = w_ada.shape[1]
    tn = 512
    return pl.pallas_call(
        _mod_kernel, grid=(n // tn,),
        in_specs=[pl.BlockSpec((r, d), lambda j: (0, 0)),
                  pl.BlockSpec((d, tn), lambda j: (0, j)),
                  pl.BlockSpec((1, tn), lambda j: (0, j))],
        out_specs=pl.BlockSpec((r, tn), lambda j: (0, j)),
        out_shape=jax.ShapeDtypeStruct((r, n), F32),
        compiler_params=_cparams("arbitrary"), name="adaln_mod",
    )(c_all, w_ada, b_ada.reshape(1, n))


def _inproj_kernel(x_ref, g_ref, mod_ref, *refs):
    n_out = len(refs) // 2
    x = x_ref[0]
    y = x * lax.rsqrt(jnp.mean(x * x, axis=-1, keepdims=True) + EPS) * g_ref[...]
    h = (y * (1.0 + mod_ref[0, 1:2, :]) + mod_ref[0, 0:1, :]).astype(BF16)
    for w_ref, o_ref in zip(refs[:n_out], refs[n_out:]):
        o_ref[0] = jnp.dot(h, w_ref[...], preferred_element_type=F32)


def in_projection(x, norm_g, mod, weights):
    b, l, d = x.shape
    w_specs = [pl.BlockSpec(w.shape, lambda bi, i: (0, 0)) for w in weights]
    o_specs = [pl.BlockSpec((1, TOK_TILE, w.shape[1]), lambda bi, i: (bi, i, 0)) for w in weights]
    return pl.pallas_call(
        _inproj_kernel, grid=(b, l // TOK_TILE),
        in_specs=[pl.BlockSpec((1, TOK_TILE, d), lambda bi, i: (bi, i, 0)),
                  pl.BlockSpec((1, d), lambda bi, i: (0, 0)),
                  pl.BlockSpec((1, 2, d), lambda bi, i: (bi, 0, 0))] + w_specs,
        out_specs=o_specs,
        out_shape=[jax.ShapeDtypeStruct((b, l, w.shape[1]), F32) for w in weights],
        compiler_params=_cparams("parallel", "arbitrary"), name="in_projection",
    )(x, norm_g.reshape(1, d), mod, *weights)


def _dn_feature_kernel(x_ref, w_ref, o_ref):
    x = x_ref[0]
    n = x.shape[0]
    t = lax.broadcasted_iota(jnp.int32, (n, 1), 0)
    pad = CONV_W // 2
    acc = w_ref[0, pad:pad + 1, :] * x
    for j in range(CONV_W):
        s = j - pad
        if s == 0:
            continue
        xs = pltpu.roll(x, (-s) % n, axis=0)
        bad = (t < -s) if s < 0 else (t >= n - s)
        acc = acc + w_ref[0, j:j + 1, :] * jnp.where(bad, 0.0, xs)
    y = acc * jax.nn.sigmoid(acc)
    kind = pl.program_id(1) // DN_HEADS
    inv = lax.rsqrt(jnp.sum(y * y, axis=-1, keepdims=True) + EPS)
    scale = jnp.where(kind == 0, inv * DN_DK ** -0.5, jnp.where(kind == 1, inv, 1.0))
    o_ref[0] = y * scale


def dn_features_tc(qkv, conv_w):
    b, l, n = qkv.shape
    nblk = n // LANES
    w = jnp.zeros((nblk, SUBLANES, LANES), F32).at[:, :CONV_W].set(
        conv_w.reshape(CONV_W, nblk, LANES).transpose(1, 0, 2))
    return pl.pallas_call(
        _dn_feature_kernel, grid=(b, nblk),
        in_specs=[pl.BlockSpec((1, l, LANES), lambda bi, j: (bi, 0, j)),
                  pl.BlockSpec((1, SUBLANES, LANES), lambda bi, j: (j, 0, 0))],
        out_specs=pl.BlockSpec((1, l, LANES), lambda bi, j: (bi, 0, j)),
        out_shape=jax.ShapeDtypeStruct((b, l, n), F32),
        compiler_params=_cparams("parallel", "arbitrary"), name="dn_features",
    )(qkv, w)


def _scan_chunks(rev, n_ctx, n_lat):
    if rev:
        ctx = lambda j: jnp.maximum(n_ctx - 1 - j, 0)
        lat = lambda j: jnp.where(j < n_ctx, n_lat - 1, n_lat - 1 - (j - n_ctx))
    else:
        ctx = lambda j: jnp.minimum(j, n_ctx - 1)
        lat = lambda j: jnp.maximum(j - n_ctx, 0)
    return ctx, lat


def _dn_scan_kernel(rev, dirn, n_ctx, alog_ref, dtb_ref, fc_ref, fl_ref, bac_ref, bal_ref, o_ref, s_ref):
    step = pl.program_id(1)

    @pl.when(step == 0)
    def _():
        s_ref[...] = jnp.zeros_like(s_ref)

    in_ctx = step < n_ctx
    f = jnp.where(in_ctx, fc_ref[0], fl_ref[0])
    ba = jnp.where(in_ctx, bac_ref[0], bal_ref[0])
    nh, hd, n = DN_HEADS, DN_HEADS * DN_DK, DN_HEADS * CHUNK
    stack = lambda base, w: jnp.concatenate([f[:, base + h * w:base + (h + 1) * w] for h in range(nh)], axis=0)
    q_s, k_s, v_s = stack(0, DN_DK), stack(hd, DN_DK), stack(2 * hd, DN_DV)
    incl, _ = _tri_masks(rev)
    beta_all = jax.nn.sigmoid(ba)
    g_all = -jnp.exp(alog_ref[...]) * _softplus(ba + dtb_ref[...])
    gc_all = _mask_dot(incl.astype(BF16), g_all)
    gc_t = jnp.concatenate([gc_all, gc_all], axis=0).T
    g_tot = jnp.sum(g_all, axis=0, keepdims=True)
    cb = [dirn * nh + h for h in range(nh)]
    cg = [2 * nh + c for c in cb]
    col = lambda a, cs: jnp.concatenate([a[:, c:c + 1] for c in cs], axis=0)
    beta_c, gc_c = col(beta_all, cb), col(gc_all, cg)
    gtot_c = jnp.concatenate([jnp.broadcast_to(g_tot[:, c:c + 1], (CHUNK, 1)) for c in cg], axis=0)
    gc_r = jnp.concatenate([gc_t[c:c + 1, :CHUNK] for c in cg], axis=1)
    r = lax.broadcasted_iota(jnp.int32, (n, n), 0)
    c = lax.broadcasted_iota(jnp.int32, (n, n), 1)
    same = (r // CHUNK) == (c // CHUNK)
    d = (c - r) if rev else (r - c)
    incl_bd, strict_bd = same & (d >= 0), same & (d > 0)
    eye = (r == c).astype(F32)
    decay = jnp.where(incl_bd, jnp.exp(jnp.where(incl_bd, gc_c - gc_r, 0.0)), 0.0)
    kb_s = k_s * beta_c
    lower = jnp.where(strict_bd, _dot_nt(kb_s, k_s) * decay, 0.0)
    eg_c = jnp.exp(gc_c)
    inv = eye - lower
    pw = lower
    for _ in range(5):
        pw = _dot(pw, pw)
        inv = inv + _dot(inv, pw)
    sol = _dot(inv, jnp.concatenate([v_s * beta_c, kb_s * eg_c], axis=-1))
    u_s, w_s = sol[:, :DN_DV], sol[:, DN_DV:]
    k_dec = k_s * jnp.exp(gtot_c - gc_c)
    rb = lax.broadcasted_iota(jnp.int32, (n, DN_DK), 0) // CHUNK
    expand = lambda x: jnp.concatenate([jnp.where(rb == h, x, 0.0) for h in range(nh)], axis=1)
    s = s_ref[...]
    v_new = u_s - _dot(expand(w_s), s)
    a_qk = _dot_nt(q_s, k_s) * decay
    o_s = _dot(expand(q_s * eg_c), s) + _dot(a_qk, v_new)
    gl_rows = jnp.concatenate([jnp.broadcast_to(jnp.exp(g_tot[:, cc:cc + 1]), (DN_DK, 1)) for cc in cg], axis=0)
    s_ref[...] = s * gl_rows + _dot_tn(expand(k_dec), v_new)
    o_ref[0] = jnp.concatenate([o_s[h * CHUNK:(h + 1) * CHUNK] for h in range(nh)], axis=1)


def dn_scan_tc(feat_c, feat_l, ba_c, ba_l, a_log, dt_bias, rev):
    b, l, nf = feat_l.shape
    n_ctx, n_lat = feat_c.shape[1] // CHUNK, l // CHUNK
    dirn = 1 if rev else 0
    cc, lc = _scan_chunks(rev, n_ctx, n_lat)
    lanes = lambda p: jnp.zeros((1, LANES), F32).at[0, 2 * DN_HEADS:4 * DN_HEADS].set(p.reshape(-1))
    vec = pl.BlockSpec((1, LANES), lambda bi, j: (0, 0))
    return pl.pallas_call(
        functools.partial(_dn_scan_kernel, rev, dirn, n_ctx), grid=(b, n_ctx + n_lat),
        in_specs=[vec, vec,
                  pl.BlockSpec((1, CHUNK, nf), lambda bi, j: (bi, cc(j), 0)),
                  pl.BlockSpec((1, CHUNK, nf), lambda bi, j: (bi, lc(j), 0)),
                  pl.BlockSpec((1, CHUNK, LANES), lambda bi, j: (bi, cc(j), 0)),
                  pl.BlockSpec((1, CHUNK, LANES), lambda bi, j: (bi, lc(j), 0))],
        out_specs=pl.BlockSpec((1, CHUNK, DN_HEADS * DN_DV), lambda bi, j: (bi, lc(j), 0)),
        out_shape=jax.ShapeDtypeStruct((b, l, DN_HEADS * DN_DV), F32),
        scratch_shapes=[pltpu.VMEM((DN_HEADS * DN_DK, DN_DV), F32)],
        compiler_params=_cparams("parallel", "arbitrary"), name="dn_scan_bwd" if rev else "dn_scan_fwd",
    )(lanes(a_log), lanes(dt_bias), feat_c, feat_l, ba_c, ba_l)


def _from_grid_cols(blk, n):
    cols = blk.shape[1] // n
    return jnp.concatenate([blk[:, i * n:(i + 1) * n] for i in range(cols)], axis=0)


def _gla_scan_kernel(rev, dirn, n_ctx, qkc_ref, vc_ref, lrc_ref, qkl_ref, vl_ref, lrl_ref,
                     wla_ref, bla_ref, o_ref, s_ref):
    step = pl.program_id(1)

    @pl.when(step == 0)
    def _():
        s_ref[...] = jnp.zeros_like(s_ref)

    in_ctx = step < n_ctx
    hk, hv = GLA_HEADS * GLA_DK, GLA_HEADS * GLA_DV
    qk = jnp.where(in_ctx, qkc_ref[0], _from_grid_cols(qkl_ref[0], 2 * hk))
    vv = jnp.where(in_ctx, vc_ref[0], _from_grid_cols(vl_ref[0], hv))
    lr = jnp.where(in_ctx, lrc_ref[0], _from_grid_cols(lrl_ref[0], LANES))
    incl, _ = _tri_masks(rev)
    incl_b = incl.astype(BF16)
    pre = _dot(lr, wla_ref[0]) + bla_ref[0]
    la_all = -_softplus(-pre) * (1.0 / GLA_TAU)
    bc_all = _mask_dot(incl_b, la_all)
    b_tot_all = jnp.sum(la_all, axis=0, keepdims=True)
    outs = []
    for h in range(GLA_HEADS):
        q = qk[:, h * GLA_DK:(h + 1) * GLA_DK] * GLA_DK ** -0.5
        k = qk[:, hk + h * GLA_DK:hk + (h + 1) * GLA_DK]
        v = vv[:, h * GLA_DV:(h + 1) * GLA_DV]
        bc = bc_all[:, h * GLA_DK:(h + 1) * GLA_DK]
        b_tot = b_tot_all[:, h * GLA_DK:(h + 1) * GLA_DK]
        st = s_ref[h]
        o = _dot_nt(q * jnp.exp(bc), st)
        parts = []
        for i in range(CHUNK // GLA_SUB):
            lo_r, hi_r = i * GLA_SUB, (i + 1) * GLA_SUB
            if rev:
                ref = bc[hi_r - 1:hi_r]
                c0, c1 = lo_r, CHUNK
            else:
                ref = bc[lo_r:lo_r + 1]
                c0, c1 = 0, hi_r
            qi = q[lo_r:hi_r] * jnp.exp(bc[lo_r:hi_r] - ref)
            ki = k[c0:c1] * jnp.exp(ref - bc[c0:c1])
            att = _dot_nt(qi, ki)
            rg = lax.broadcasted_iota(jnp.int32, (GLA_SUB, c1 - c0), 0) + lo_r
            cg = lax.broadcasted_iota(jnp.int32, (GLA_SUB, c1 - c0), 1) + c0
            keep = (cg > rg) if rev else (cg < rg)
            parts.append(_dot(jnp.where(keep, att, 0.0), v[c0:c1]))
        diag = jnp.sum(q * k, axis=-1, keepdims=True) * v
        outs.append(o + jnp.concatenate(parts, axis=0) + diag)
        k_dec = k * jnp.exp(b_tot - bc)
        s_ref[h] = st * jnp.exp(b_tot) + _dot_tn(v, k_dec)
    o = jnp.concatenate(outs, axis=-1)
    rows = o_ref.shape[1]
    o_ref[0] = jnp.concatenate([o[i * rows:(i + 1) * rows] for i in range(CHUNK // rows)], axis=-1)


def gla_scan_tc(qk_c, v_c, lr_c, qk_l, v_l, lr_l, w_la, b_la, rev):
    b, l, _ = qk_l.shape
    rows = l // GRID_W
    cols = CHUNK // rows
    n_ctx, n_lat = qk_c.shape[1] // CHUNK, l // CHUNK
    dirn = 1 if rev else 0
    cc, lc = _scan_chunks(rev, n_ctx, n_lat)
    hv = GLA_HEADS * GLA_DV
    ctx_blk = lambda a: pl.BlockSpec((1, CHUNK, a.shape[2]), lambda bi, j: (bi, cc(j), 0))
    lat_blk = lambda n: pl.BlockSpec((1, rows, cols * n), lambda bi, j: (bi, 0, lc(j)))
    grid_view = lambda a: a.reshape(b, rows, GRID_W * a.shape[2])
    out = pl.pallas_call(
        functools.partial(_gla_scan_kernel, rev, dirn, n_ctx), grid=(b, n_ctx + n_lat),
        in_specs=[ctx_blk(qk_c), ctx_blk(v_c), ctx_blk(lr_c),
                  lat_blk(qk_l.shape[2]), lat_blk(v_l.shape[2]), lat_blk(lr_l.shape[2]),
                  pl.BlockSpec((1,) + w_la.shape[1:], lambda bi, j: (dirn, 0, 0)),
                  pl.BlockSpec((1,) + b_la.shape[1:], lambda bi, j: (dirn, 0, 0))],
        out_specs=lat_blk(hv),
        out_shape=jax.ShapeDtypeStruct((b, rows, GRID_W * hv), F32),
        scratch_shapes=[pltpu.VMEM((GLA_HEADS, GLA_DV, GLA_DK), F32)],
        compiler_params=_cparams("parallel", "arbitrary"), name="gla_scan_bwd" if rev else "gla_scan_fwd",
    )(qk_c, v_c, lr_c, grid_view(qk_l), grid_view(v_l), grid_view(lr_l), w_la, b_la)
    return out.reshape(b, l, hv)


def _head_norm_gate(o, gate, g, n_heads, dv):
    parts = []
    for h in range(n_heads):
        oh = o[:, h * dv:(h + 1) * dv]
        gh = gate[:, h * dv:(h + 1) * dv]
        yh = oh * lax.rsqrt(jnp.mean(oh * oh, axis=-1, keepdims=True) + EPS) * g
        parts.append(yh * (gh * jax.nn.sigmoid(gh)))
    return parts


def _mix_out_kernel(x_ref, dnf_ref, dnb_ref, z_ref, glf_ref, glb_ref, r_ref, mod_ref, dng_ref,
                    glg_ref, n2g_ref, wout_ref, wq_ref, x1_ref, h2_ref, q_ref):
    parts = (_head_norm_gate(dnf_ref[0] + dnb_ref[0], z_ref[0], dng_ref[...], DN_HEADS, DN_DV)
             + _head_norm_gate(glf_ref[0] + glb_ref[0], r_ref[0], glg_ref[...], GLA_HEADS, GLA_DV))
    y = jnp.dot(jnp.concatenate(parts, axis=-1).astype(BF16), wout_ref[...], preferred_element_type=F32)
    x1 = x_ref[0] + mod_ref[0, 2:3, :] * y
    x1_ref[0] = x1
    n = x1 * lax.rsqrt(jnp.mean(x1 * x1, axis=-1, keepdims=True) + EPS) * n2g_ref[...]
    h2 = n * (1.0 + mod_ref[0, 4:5, :]) + mod_ref[0, 3:4, :]
    h2_ref[0] = h2
    h_hi, h_lo = _split(h2)
    q_ref[0] = (jnp.dot(h_hi, wq_ref[0], preferred_element_type=F32)
                + jnp.dot(h_lo, wq_ref[0], preferred_element_type=F32)
                + jnp.dot(h_hi, wq_ref[1], preferred_element_type=F32))


def mix_out_tc(x, dn_f, dn_b, z, gl_f, gl_b, r, mod_l, dn_g, gla_g, n2_g, w_out, w_q):
    b, l, d = x.shape
    tok = lambda n: pl.BlockSpec((1, TOK_TILE, n), lambda bi, i: (bi, i, 0))
    full = lambda a: pl.BlockSpec(a.shape, lambda bi, i: (0,) * a.ndim)
    dn_g, gla_g, n2_g = dn_g.reshape(1, -1), gla_g.reshape(1, -1), n2_g.reshape(1, -1)
    nq = w_q.shape[2]
    return pl.pallas_call(
        _mix_out_kernel, grid=(b, l // TOK_TILE),
        in_specs=[tok(d), tok(dn_f.shape[2]), tok(dn_b.shape[2]), tok(z.shape[2]),
                  tok(gl_f.shape[2]), tok(gl_b.shape[2]), tok(r.shape[2]),
                  pl.BlockSpec((1,) + mod_l.shape[1:], lambda bi, i: (bi, 0, 0)),
                  full(dn_g), full(gla_g), full(n2_g), full(w_out), full(w_q)],
        out_specs=[tok(d), tok(d), tok(nq)],
        out_shape=[jax.ShapeDtypeStruct((b, l, d), F32), jax.ShapeDtypeStruct((b, l, d), F32),
                   jax.ShapeDtypeStruct((b, l, nq), F32)],
        compiler_params=_cparams("parallel", "arbitrary"), name="mix_out",
    )(x, dn_f, dn_b, z, gl_f, gl_b, r, mod_l, dn_g, gla_g, n2_g, w_out, w_q)


def _top_rows(s, k, payload=None):
    n = s.shape[0]
    row = lax.broadcasted_iota(jnp.int32, s.shape, 0).astype(F32)
    vals, picked = [], []
    for _ in range(k):
        m = jnp.max(s, axis=0, keepdims=True)
        first = jnp.min(jnp.where(s == m, row, float(n)), axis=0, keepdims=True)
        sel = row == first
        vals.append(m)
        if payload is None:
            picked.append(first)
        else:
            picked.append(jnp.max(jnp.where(sel, payload, -1.0), axis=0, keepdims=True))
        s = jnp.where(sel, -jnp.inf, s)
    return jnp.concatenate(vals, axis=0), jnp.concatenate(picked, axis=0)


def _candidate_rows(s0, i0, s1, i1):
    k = s0.shape[0]
    wide = SUBLANES
    blocks_s = [s0[0:1] + s1]
    blocks_i = [i0[0:1] * float(PEER_NKEYS) + i1]
    col = lax.broadcasted_iota(jnp.int32, (wide, s0.shape[1]), 0)
    for i in range(1, wide):
        keep = col < (k // (i + 1))
        blocks_s.append(jnp.where(keep, s0[i:i + 1] + s1[0:wide], -jnp.inf))
        blocks_i.append(i0[i:i + 1] * float(PEER_NKEYS) + i1[0:wide])
    blocks_s.append(s0[wide:k] + s1[0:1])
    blocks_i.append(i0[wide:k] * float(PEER_NKEYS) + i1[0:1])
    return jnp.concatenate(blocks_s, axis=0), jnp.concatenate(blocks_i, axis=0)


def _select_kernel(q_ref, k_ref, idx_ref, gate_ref, idx_s, gate_s):
    half = PEER_DQ // 2

    def head(h, carry):
        tops = []
        for p in range(2):
            qp = q_ref[:, pl.ds(pl.multiple_of(h * PEER_DQ + p * half, half), half)]
            s = lax.dot_general(k_ref[h, p], qp, (((1,), (1,)), ((), ())),
                                preferred_element_type=F32,
                                precision=lax.Precision.HIGHEST)
            tops.append(_top_rows(s, PEER_TOPK))
        (s0, i0), (s1, i1) = tops
        cand_s, cand_i = _candidate_rows(s0, i0, s1, i1)
        best_s, idx = _top_rows(cand_s, PEER_TOPK, payload=cand_i)
        e = jnp.exp(best_s - best_s[0:1])
        r0 = pl.multiple_of(h * PEER_TOPK, PEER_TOPK)
        idx_s[pl.ds(r0, PEER_TOPK), :] = idx
        gate_s[pl.ds(r0, PEER_TOPK), :] = e / jnp.sum(e, axis=0, keepdims=True)
        return carry

    lax.fori_loop(0, PEER_HEADS, head, 0)
    idx_ref[...] = idx_s[...].T.astype(jnp.int32)
    gate_ref[...] = gate_s[...].T


def peer_select_tc(q, keys):
    n_tok = q.shape[0]
    out_spec = pl.BlockSpec((SELECT_TILE, PICKS), lambda i: (i, 0))
    return pl.pallas_call(
        _select_kernel,
        grid=(n_tok // SELECT_TILE,),
        in_specs=[pl.BlockSpec((SELECT_TILE, q.shape[1]), lambda i: (i, 0)),
                  pl.BlockSpec(keys.shape, lambda i: (0, 0, 0, 0))],
        out_specs=[out_spec, out_spec],
        out_shape=[jax.ShapeDtypeStruct((n_tok, PICKS), jnp.int32),
                   jax.ShapeDtypeStruct((n_tok, PICKS), F32)],
        scratch_shapes=[pltpu.VMEM((PICKS, SELECT_TILE), F32), pltpu.VMEM((PICKS, SELECT_TILE), F32)],
        compiler_params=_cparams("parallel"), name="peer_select",
    )(q, keys)


def _sc_mesh():
    return plsc.VectorSubcoreMesh(core_axis_name="c", subcore_axis_name="s")


def _sc_pipeline(body, n_steps, in_specs, out_specs, operands):
    pltpu.emit_pipeline(
        body, grid=(n_steps,), in_specs=in_specs, out_specs=out_specs,
        core_axis_name=("c", "s"), dimension_semantics=(pltpu.PARALLEL,),
        trace_scopes=False,
    )(*operands)


def pack_pairs(t):
    half = t.shape[1] // 2
    bits = lax.bitcast_convert_type(t, jnp.uint32)
    rne = lambda b: (b + jnp.uint32(0x7FFF) + ((b >> 16) & jnp.uint32(1))) >> 16
    word = (rne(bits[:, half:]) << 16) | rne(bits[:, :half])
    return lax.bitcast_convert_type(word, jnp.int32)


def peer_act_partial_sc(u_q, idx2, h_q):
    n_groups, (n_tok, half) = idx2.shape[0], h_q.shape
    nsub = TOK_STEP * GROUPS_PER_TOK
    n_chunks = half // SC_LANES
    as_pairs = lambda w: plsc.bitcast(w, BF16)

    @functools.partial(
        pl.kernel, mesh=_sc_mesh(), compiler_params=pltpu.CompilerParams(needs_layout_passes=False),
        out_type=jax.ShapeDtypeStruct((n_groups, PICK_GROUP * SC_LANES), F32),
        scratch_types=[pltpu.VMEM((GATHER_BUFS, PICK_GROUP, half), jnp.int32),
                       pltpu.SemaphoreType.DMA((GATHER_BUFS,))],
    )
    def k(u_hbm, i_hbm, h_hbm, o_hbm, rows, sems):
        def body(i_v, h_v, o_v):
            def fetch(j, slot):
                return pltpu.make_async_copy(u_hbm.at[i_v.at[j]], rows.at[slot], sems.at[slot])

            for ahead in range(GATHER_BUFS - 1):
                fetch(ahead, ahead).start()

            def sub(j, carry):
                slot = j % GATHER_BUFS

                nxt = j + GATHER_BUFS - 1

                @pl.when(nxt < nsub)
                def _():
                    fetch(nxt, nxt % GATHER_BUFS).start()

                fetch(j, slot).wait()
                t = j // GROUPS_PER_TOK

                def picks(g, carry2):
                    kb = g * ACT_UNROLL
                    accs = [None] * ACT_UNROLL
                    for c0 in range(0, n_chunks, ACT_BF16_TERMS):
                        cols = [pl.ds((c0 + cc) * SC_LANES, SC_LANES) for cc in range(ACT_BF16_TERMS)]
                        hs = [as_pairs(h_v[t, col]) for col in cols]
                        for i in range(ACT_UNROLL):
                            s = None
                            for col, hv in zip(cols, hs):
                                p = as_pairs(rows[slot, kb + i, col]) * hv
                                s = p if s is None else s + p
                            lo, hi = plsc.unpack(s, format=plsc.PackFormat.INTERLEAVED)
                            p32 = lo + hi
                            accs[i] = p32 if accs[i] is None else accs[i] + p32
                    for i in range(ACT_UNROLL):
                        o_v[j, pl.ds((kb + i) * SC_LANES, SC_LANES)] = accs[i]
                    return carry2

                lax.fori_loop(0, PICK_GROUP // ACT_UNROLL, picks, 0)
                return carry

            lax.fori_loop(0, nsub, sub, 0)

        _sc_pipeline(
            body, n_tok // TOK_STEP,
            [pl.BlockSpec((nsub, PICK_GROUP), lambda i: (i, 0)),
             pl.BlockSpec((TOK_STEP, half), lambda i: (i, 0))],
            [pl.BlockSpec((nsub, PICK_GROUP * SC_LANES), lambda i: (i, 0))],
            (i_hbm, h_hbm, o_hbm))

    return k(u_q, idx2, h_q)


def peer_combine_sc(v_q, idx2, coef_q):
    n_groups = idx2.shape[0]
    half = v_q.shape[1]
    nsub = COMB_GROUP_STEP
    n_chunks = half // SC_LANES
    as_pairs = lambda w: plsc.bitcast(w, BF16)

    @functools.partial(
        pl.kernel, mesh=_sc_mesh(), compiler_params=pltpu.CompilerParams(needs_layout_passes=False),
        out_type=jax.ShapeDtypeStruct((n_groups, 2 * half), F32),
        scratch_types=[pltpu.VMEM((GATHER_BUFS, PICK_GROUP, half), jnp.int32),
                       pltpu.SemaphoreType.DMA((GATHER_BUFS,))],
    )
    def k(v_hbm, i_hbm, c_hbm, o_hbm, rows, sems):
        def body(i_v, c_v, o_v):
            def fetch(j, slot):
                return pltpu.make_async_copy(v_hbm.at[i_v.at[j]], rows.at[slot], sems.at[slot])

            for ahead in range(GATHER_BUFS - 1):
                fetch(ahead, ahead).start()

            def sub(j, carry0):
                slot = j % GATHER_BUFS

                nxt = j + GATHER_BUFS - 1

                @pl.when(nxt < nsub)
                def _():
                    fetch(nxt, nxt % GATHER_BUFS).start()

                fetch(j, slot).wait()
                cks = [as_pairs(c_v[j, pl.ds(kk * SC_LANES, SC_LANES)]) for kk in range(PICK_GROUP)]

                def chunk_pair(c2, carry):
                    sums = []
                    for cc in range(COMB_CHUNKS):
                        l = (c2 * COMB_CHUNKS + cc) * SC_LANES
                        tot_lo, tot_hi = None, None
                        for k0 in range(0, PICK_GROUP, COMB_BF16_TERMS):
                            s = None
                            for kk in range(k0, k0 + COMB_BF16_TERMS):
                                p = cks[kk] * as_pairs(rows[slot, kk, pl.ds(l, SC_LANES)])
                                s = p if s is None else s + p
                            lo, hi = plsc.unpack(s, format=plsc.PackFormat.INTERLEAVED)
                            tot_lo = lo if tot_lo is None else tot_lo + lo
                            tot_hi = hi if tot_hi is None else tot_hi + hi
                        sums.append((l, tot_lo, tot_hi))
                    for l, tot_lo, tot_hi in sums:
                        o_v[j, pl.ds(l, SC_LANES)] = tot_lo
                        o_v[j, pl.ds(half + l, SC_LANES)] = tot_hi
                    return carry

                lax.fori_loop(0, n_chunks // COMB_CHUNKS, chunk_pair, 0)
                return carry0

            lax.fori_loop(0, nsub, sub, 0)

        _sc_pipeline(
            body, n_groups // nsub,
            [pl.BlockSpec((nsub, PICK_GROUP), lambda i: (i, 0)),
             pl.BlockSpec((nsub, PICK_GROUP * SC_LANES), lambda i: (i, 0))],
            [pl.BlockSpec((nsub, 2 * half), lambda i: (i, 0))],
            (i_hbm, c_hbm, o_hbm))

    return k(v_q, idx2, coef_q)


def _segment_matrix():
    r = lax.broadcasted_iota(jnp.int32, (PICK_GROUP * SC_LANES, PICK_GROUP), 0) // SC_LANES
    c = lax.broadcasted_iota(jnp.int32, (PICK_GROUP * SC_LANES, PICK_GROUP), 1)
    return (r == c).astype(F32)


def _coef_kernel(part_ref, gate_ref, o_ref):
    seg = _segment_matrix()
    act = jnp.dot(part_ref[...], seg, preferred_element_type=F32, precision=lax.Precision.HIGHEST)
    coef = gate_ref[...] * (0.5 * act * (1.0 + lax.erf(act * (2.0 ** -0.5))))
    wide = lax.dot_general(coef, seg, (((1,), (1,)), ((), ())), preferred_element_type=F32,
                           precision=lax.Precision.HIGHEST)
    bits = pltpu.bitcast(wide, jnp.uint32)
    r = (bits + jnp.uint32(0x7FFF) + ((bits >> 16) & jnp.uint32(1))) >> 16
    o_ref[...] = pltpu.bitcast(r | (r << 16), jnp.int32)


def peer_coef_tc(part, gate2):
    n_groups, width = part.shape
    tile = 1024
    return pl.pallas_call(
        _coef_kernel,
        grid=(n_groups // tile,),
        in_specs=[pl.BlockSpec((tile, width), lambda i: (i, 0)),
                  pl.BlockSpec((tile, PICK_GROUP), lambda i: (i, 0))],
        out_specs=pl.BlockSpec((tile, width), lambda i: (i, 0)),
        out_shape=jax.ShapeDtypeStruct((n_groups, width), jnp.int32),
        compiler_params=_cparams("parallel"), name="peer_coef",
    )(part, gate2)


def _final_kernel(x_ref, y_ref, mod_ref, g_ref, o_ref):
    d = x_ref.shape[2]
    y = y_ref[0, :, 0:d]
    for p in range(1, y_ref.shape[2] // d):
        y = y + y_ref[0, :, p * d:(p + 1) * d]
    x = x_ref[0] + mod_ref[0, 5:6, :] * y
    o_ref[0] = x * lax.rsqrt(jnp.mean(x * x, axis=-1, keepdims=True) + EPS) * g_ref[...]


def final_tc(x1, y_parts, mod_l, final_g):
    b, l, d = x1.shape
    tok = pl.BlockSpec((1, TOK_TILE, d), lambda bi, i: (bi, i, 0))
    return pl.pallas_call(
        _final_kernel, grid=(b, l // TOK_TILE),
        in_specs=[tok, pl.BlockSpec((1, TOK_TILE, y_parts.shape[2]), lambda bi, i: (bi, i, 0)),
                  pl.BlockSpec((1,) + mod_l.shape[1:], lambda bi, i: (bi, 0, 0)),
                  pl.BlockSpec((1, d), lambda bi, i: (0, 0))],
        out_specs=tok, out_shape=jax.ShapeDtypeStruct((b, l, d), F32),
        compiler_params=_cparams("parallel", "arbitrary"), name="final_norm",
    )(x1, y_parts, mod_l, final_g.reshape(1, d))


def _pad_cols(w, n):
    return jnp.pad(w, ((0, 0), (0, n - w.shape[1])))


def _slice_sizes(b):
    if b % SLICE_BATCH or SLICE_BATCH % 2 or b < 2 * SLICE_BATCH:
        return [1] * b
    half = SLICE_BATCH // 2
    return [half] + [SLICE_BATCH] * (b // SLICE_BATCH - 1) + [half]


def _exact_zero(v):
    return jnp.minimum(jnp.abs(v), 0.0)


def forward(x, c, ctx, c_ctx, w_ada, b_ada, norm1_g, norm2_g, w_in, conv_w, dn_a_log,
            dn_dt_bias, dn_norm_g, gla_wa2, gla_ba, gla_norm_g, w_out, peer_wq, peer_keys,
            peer_u, peer_v, final_g):
    b, l, d = x.shape
    c_all = jnp.concatenate([c, c_ctx[None]], axis=0)
    c_all = jnp.pad(c_all, ((0, (-c_all.shape[0]) % SUBLANES), (0, 0)))
    mod = adaln_mod(c_all, w_ada, b_ada)
    mod_l = mod[:b].reshape(b, 6, d)
    mod_c = jnp.broadcast_to(mod[b].reshape(1, 6, d), (b, 6, d))
    o = DN_QKV
    hv = DN_HEADS * DN_DV
    w_dn_qkv, w_dn_z = w_in[:, :o], w_in[:, o:o + hv]
    w_dn_ba = _pad_cols(w_in[:, o + hv:DN_COLS], LANES)
    g0 = DN_COLS
    gqk, gv = 2 * GLA_HEADS * GLA_DK, GLA_HEADS * GLA_DV
    w_gl_qk, w_gl_v = w_in[:, g0:g0 + gqk], w_in[:, g0 + gqk:g0 + gqk + gv]
    w_gl_r = w_in[:, g0 + gqk + gv:g0 + gqk + 2 * gv]
    w_gl_lr = _pad_cols(w_in[:, g0 + gqk + 2 * gv:], LANES)
    w_lat = [w.astype(BF16) for w in (w_dn_qkv, w_dn_ba, w_gl_qk, w_gl_v, w_gl_lr, w_dn_z, w_gl_r)]
    w_ctx = w_lat[:5]
    w_la = jnp.zeros((2, LANES, GLA_HEADS * GLA_DK), F32)
    for dd in range(2):
        w_la = w_la.at[dd, dd * GLA_LR:(dd + 1) * GLA_LR].set(gla_wa2[dd])
    w_la = w_la.astype(BF16)
    b_la = gla_ba.reshape(2, 1, GLA_HEADS * GLA_DK)
    w_out_b, w_q_b = w_out.astype(BF16), jnp.stack(_split(peer_wq))
    u_q, v_q = pack_pairs(peer_u), pack_pairs(peer_v)

    def mixer(xg, ctxg, mod_cg, mod_g, after_select, after_combine):
        mod_cg, mod_g = mod_cg + after_select, mod_g + after_select
        c_qkv, c_ba, c_qk, c_v, c_lr = in_projection(ctxg, norm1_g, mod_cg[:, 0:2], w_ctx)
        l_qkv, l_ba, l_qk, l_v, l_lr, l_z, l_r = in_projection(xg, norm1_g, mod_g[:, 0:2], w_lat)
        feat_c, feat_l = dn_features_tc(c_qkv, conv_w), dn_features_tc(l_qkv, conv_w)
        dn_f = dn_scan_tc(feat_c, feat_l, c_ba, l_ba, dn_a_log, dn_dt_bias, rev=False)
        dn_b = dn_scan_tc(feat_c, feat_l, c_ba, l_ba, dn_a_log, dn_dt_bias, rev=True)
        gl_f = gla_scan_tc(c_qk, c_v, c_lr, l_qk, l_v, l_lr, w_la, b_la, rev=False)
        gl_b = gla_scan_tc(c_qk, c_v, c_lr, l_qk, l_v, l_lr, w_la, b_la, rev=True)
        return mix_out_tc(xg, dn_f, dn_b, l_z, gl_f, gl_b, l_r, mod_g, dn_norm_g + after_combine,
                          gla_norm_g, norm2_g, w_out_b, w_q_b)

    def select_and_act(h2, q, after_coef):
        n_tok = h2.shape[0] * l
        idx, gate = peer_select_tc(q.reshape(n_tok, -1), peer_keys + after_coef)
        idx2 = idx.reshape(n_tok * GROUPS_PER_TOK, PICK_GROUP)
        gate2 = gate.reshape(n_tok * GROUPS_PER_TOK, PICK_GROUP)
        return idx2, gate2, peer_act_partial_sc(u_q, idx2, pack_pairs(h2.reshape(n_tok, d)))

    def coef_and_combine(st, after_mixer):
        coef_q = peer_coef_tc(st["part"], st["gate2"] + after_mixer)
        st["y"] = peer_combine_sc(v_q, st["idx2"], coef_q)
        return _exact_zero(coef_q[0, 0].astype(F32))

    zero = jnp.zeros((), F32)
    slices, z_sel, z_coef = [], zero, zero
    i = 0
    for bg in _slice_sizes(b):
        g = len(slices)
        z_comb = _exact_zero(slices[g - 2]["y"][0, 0]) if g >= 2 else zero
        mod_g = mod_l[i:i + bg]
        x1, h2, q = mixer(x[i:i + bg], ctx[i:i + bg], mod_c[i:i + bg], mod_g, z_sel, z_comb)
        i += bg
        if g >= 1:
            z_coef = coef_and_combine(slices[g - 1], _exact_zero(x1[0, 0, 0]))
        idx2, gate2, part = select_and_act(h2, q, z_coef)
        z_sel = _exact_zero(gate2[0, 0])
        slices.append(dict(x1=x1, mod=mod_g, idx2=idx2, gate2=gate2, part=part))
    coef_and_combine(slices[-1], zero)
    outs = [final_tc(st["x1"], st["y"].reshape(st["x1"].shape[:2] + (-1,)), st["mod"], final_g) for st in slices]
    return jnp.concatenate(outs, axis=0)


def kernel(x, c, ctx, c_ctx, w_ada, b_ada, norm1_g, norm2_g, w_in, conv_w, dn_a_log,
           dn_dt_bias, dn_norm_g, gla_wa2, gla_ba, gla_norm_g, w_out, peer_wq, peer_keys,
           peer_u, peer_v, final_g):
    assert w_ada.shape[0] == 1, "single-layer block: the context stream is only consumed, never updated"
    return forward(x, c, ctx, c_ctx, w_ada[0], b_ada[0], norm1_g[0], norm2_g[0], w_in[0], conv_w[0],
                   dn_a_log[0], dn_dt_bias[0], dn_norm_g[0], gla_wa2[0], gla_ba[0], gla_norm_g[0],
                   w_out[0], peer_wq[0], peer_keys[0], peer_u[0], peer_v[0], final_g)
```

```python
import functools

import jax
import jax.numpy as jnp
from jax import lax
from jax.experimental import pallas as pl
from jax.experimental.pallas import tpu as pltpu
from jax.experimental.pallas import tpu_sc as plsc

GRID_W = 64
DN_HEADS = 4
DN_DK = 128
DN_DV = 128
CONV_W = 5
GLA_HEADS = 4
GLA_DK = 64
GLA_DV = 128
GLA_LR = 16
GLA_TAU = 16.0
CHUNK = 64
PEER_HEADS = 8
PEER_NKEYS = 128
PEER_DQ = 256
PEER_TOPK = 16
EPS = 1e-6
DN_QKV = 2 * DN_HEADS * DN_DK + DN_HEADS * DN_DV
DN_COLS = DN_QKV + DN_HEADS * DN_DV + 4 * DN_HEADS

SUBLANES = 8
LANES = 128
SC_LANES = 16
VMEM_LIMIT_BYTES = 48 * 1024 * 1024

TOK_TILE = 256
SELECT_TILE = 1024
GLA_SUB = 16
PICK_GROUP = 32
COMB_GROUP_STEP = 16
GATHER_BUFS = 3
TOK_STEP = 8
COMB_BF16_TERMS = 4
COMB_CHUNKS = 4
ACT_BF16_TERMS = 4
ACT_UNROLL = 16
SLICE_BATCH = 2
PICKS = PEER_HEADS * PEER_TOPK
GROUPS_PER_TOK = PICKS // PICK_GROUP

F32 = jnp.float32
BF16 = jnp.bfloat16


def _cparams(*semantics):
    return pltpu.CompilerParams(dimension_semantics=semantics, vmem_limit_bytes=VMEM_LIMIT_BYTES)


def _dot(a, b):
    return jnp.dot(a.astype(BF16), b.astype(BF16), preferred_element_type=F32)


def _dot_nt(a, b):
    return lax.dot_general(a.astype(BF16), b.astype(BF16), (((1,), (1,)), ((), ())),
                           preferred_element_type=F32)


def _dot_tn(a, b):
    return lax.dot_general(a.astype(BF16), b.astype(BF16), (((0,), (0,)), ((), ())),
                           preferred_element_type=F32)


def _split(x):
    hi = x.astype(BF16)
    return hi, (x - hi.astype(F32)).astype(BF16)


def _mask_dot(mask_bf16, x):
    hi, lo = _split(x)
    return (jnp.dot(mask_bf16, hi, preferred_element_type=F32)
            + jnp.dot(mask_bf16, lo, preferred_element_type=F32))


def _softplus(x):
    return jnp.maximum(x, 0.0) + jnp.log(1.0 + jnp.exp(-jnp.abs(x)))


def _tri_masks(rev):
    r = lax.broadcasted_iota(jnp.int32, (CHUNK, CHUNK), 0)
    c = lax.broadcasted_iota(jnp.int32, (CHUNK, CHUNK), 1)
    d = (c - r) if rev else (r - c)
    return d >= 0, d > 0


def _mod_kernel(c_ref, w_ref, b_ref, o_ref):
    c = c_ref[...]
    s = c * jax.nn.sigmoid(c)
    o_ref[...] = jnp.dot(s, w_ref[...], preferred_element_type=F32,
                         precision=lax.Precision.HIGHEST) + b_ref[...]


def adaln_mod(c_all, w_ada, b_ada):
    r, d = c_all.shape
    n = w_ada.shape[1]
    tn = 512
    return pl.pallas_call(
        _mod_kernel, grid=(n // tn,),
        in_specs=[pl.BlockSpec((r, d), lambda j: (0, 0)),
                  pl.BlockSpec((d, tn), lambda j: (0, j)),
                  pl.BlockSpec((1, tn), lambda j: (0, j))],
        out_specs=pl.BlockSpec((r, tn), lambda j: (0, j)),
        out_shape=jax.ShapeDtypeStruct((r, n), F32),
        compiler_params=_cparams("arbitrary"), name="adaln_mod",
    )(c_all, w_ada, b_ada.reshape(1, n))


def _inproj_kernel(x_ref, g_ref, mod_ref, *refs):
    n_out = len(refs) // 2
    x = x_ref[0]
    y = x * lax.rsqrt(jnp.mean(x * x, axis=-1, keepdims=True) + EPS) * g_ref[...]
    h = (y * (1.0 + mod_ref[0, 1:2, :]) + mod_ref[0, 0:1, :]).astype(BF16)
    for w_ref, o_ref in zip(refs[:n_out], refs[n_out:]):
        o_ref[0] = jnp.dot(h, w_ref[...], preferred_element_type=F32)


def in_projection(x, norm_g, mod, weights):
    b, l, d = x.shape
    w_specs = [pl.BlockSpec(w.shape, lambda bi, i: (0, 0)) for w in weights]
    o_specs = [pl.BlockSpec((1, TOK_TILE, w.shape[1]), lambda bi, i: (bi, i, 0)) for w in weights]
    return pl.pallas_call(
        _inproj_kernel, grid=(b, l // TOK_TILE),
        in_specs=[pl.BlockSpec((1, TOK_TILE, d), lambda bi, i: (bi, i, 0)),
                  pl.BlockSpec((1, d), lambda bi, i: (0, 0)),
                  pl.BlockSpec((1, 2, d), lambda bi, i: (bi, 0, 0))] + w_specs,
        out_specs=o_specs,
        out_shape=[jax.ShapeDtypeStruct((b, l, w.shape[1]), F32) for w in weights],
        compiler_params=_cparams("parallel", "arbitrary"), name="in_projection",
    )(x, norm_g.reshape(1, d), mod, *weights)


def _dn_feature_kernel(x_ref, w_ref, o_ref):
    x = x_ref[0]
    n = x.shape[0]
    t = lax.broadcasted_iota(jnp.int32, (n, 1), 0)
    pad = CONV_W // 2
    acc = w_ref[0, pad:pad + 1, :] * x
    for j in range(CONV_W):
        s = j - pad
        if s == 0:
            continue
        xs = pltpu.roll(x, (-s) % n, axis=0)
        bad = (t < -s) if s < 0 else (t >= n - s)
        acc = acc + w_ref[0, j:j + 1, :] * jnp.where(bad, 0.0, xs)
    y = acc * jax.nn.sigmoid(acc)
    kind = pl.program_id(1) // DN_HEADS
    inv = lax.rsqrt(jnp.sum(y * y, axis=-1, keepdims=True) + EPS)
    scale = jnp.where(kind == 0, inv * DN_DK ** -0.5, jnp.where(kind == 1, inv, 1.0))
    o_ref[0] = y * scale


def dn_features_tc(qkv, conv_w):
    b, l, n = qkv.shape
    nblk = n // LANES
    w = jnp.zeros((nblk, SUBLANES, LANES), F32).at[:, :CONV_W].set(
        conv_w.reshape(CONV_W, nblk, LANES).transpose(1, 0, 2))
    return pl.pallas_call(
        _dn_feature_kernel, grid=(b, nblk),
        in_specs=[pl.BlockSpec((1, l, LANES), lambda bi, j: (bi, 0, j)),
                  pl.BlockSpec((1, SUBLANES, LANES), lambda bi, j: (j, 0, 0))],
        out_specs=pl.BlockSpec((1, l, LANES), lambda bi, j: (bi, 0, j)),
        out_shape=jax.ShapeDtypeStruct((b, l, n), F32),
        compiler_params=_cparams("parallel", "arbitrary"), name="dn_features",
    )(qkv, w)


def _scan_chunks(rev, n_ctx, n_lat):
    if rev:
        ctx = lambda j: jnp.maximum(n_ctx - 1 - j, 0)
        lat = lambda j: jnp.where(j < n_ctx, n_lat - 1, n_lat - 1 - (j - n_ctx))
    else:
        ctx = lambda j: jnp.minimum(j, n_ctx - 1)
        lat = lambda j: jnp.maximum(j - n_ctx, 0)
    return ctx, lat


def _dn_scan_kernel(rev, dirn, n_ctx, alog_ref, dtb_ref, fc_ref, fl_ref, bac_ref, bal_ref, o_ref, s_ref):
    step = pl.program_id(1)

    @pl.when(step == 0)
    def _():
        s_ref[...] = jnp.zeros_like(s_ref)

    in_ctx = step < n_ctx
    f = jnp.where(in_ctx, fc_ref[0], fl_ref[0])
    ba = jnp.where(in_ctx, bac_ref[0], bal_ref[0])
    nh, hd, n = DN_HEADS, DN_HEADS * DN_DK, DN_HEADS * CHUNK
    stack = lambda base, w: jnp.concatenate([f[:, base + h * w:base + (h + 1) * w] for h in range(nh)], axis=0)
    q_s, k_s, v_s = stack(0, DN_DK), stack(hd, DN_DK), stack(2 * hd, DN_DV)
    incl, _ = _tri_masks(rev)
    beta_all = jax.nn.sigmoid(ba)
    g_all = -jnp.exp(alog_ref[...]) * _softplus(ba + dtb_ref[...])
    gc_all = _mask_dot(incl.astype(BF16), g_all)
    gc_t = jnp.concatenate([gc_all, gc_all], axis=0).T
    g_tot = jnp.sum(g_all, axis=0, keepdims=True)
    cb = [dirn * nh + h for h in range(nh)]
    cg = [2 * nh + c for c in cb]
    col = lambda a, cs: jnp.concatenate([a[:, c:c + 1] for c in cs], axis=0)
    beta_c, gc_c = col(beta_all, cb), col(gc_all, cg)
    gtot_c = jnp.concatenate([jnp.broadcast_to(g_tot[:, c:c + 1], (CHUNK, 1)) for c in cg], axis=0)
    gc_r = jnp.concatenate([gc_t[c:c + 1, :CHUNK] for c in cg], axis=1)
    r = lax.broadcasted_iota(jnp.int32, (n, n), 0)
    c = lax.broadcasted_iota(jnp.int32, (n, n), 1)
    same = (r // CHUNK) == (c // CHUNK)
    d = (c - r) if rev else (r - c)
    incl_bd, strict_bd = same & (d >= 0), same & (d > 0)
    eye = (r == c).astype(F32)
    decay = jnp.where(incl_bd, jnp.exp(jnp.where(incl_bd, gc_c - gc_r, 0.0)), 0.0)
    kb_s = k_s * beta_c
    lower = jnp.where(strict_bd, _dot_nt(kb_s, k_s) * decay, 0.0)
    eg_c = jnp.exp(gc_c)
    inv = eye - lower
    pw = lower
    for _ in range(5):
        pw = _dot(pw, pw)
        inv = inv + _dot(inv, pw)
    sol = _dot(inv, jnp.concatenate([v_s * beta_c, kb_s * eg_c], axis=-1))
    u_s, w_s = sol[:, :DN_DV], sol[:, DN_DV:]
    k_dec = k_s * jnp.exp(gtot_c - gc_c)
    rb = lax.broadcasted_iota(jnp.int32, (n, DN_DK), 0) // CHUNK
    expand = lambda x: jnp.concatenate([jnp.where(rb == h, x, 0.0) for h in range(nh)], axis=1)
    s = s_ref[...]
    v_new = u_s - _dot(expand(w_s), s)
    a_qk = _dot_nt(q_s, k_s) * decay
    o_s = _dot(expand(q_s * eg_c), s) + _dot(a_qk, v_new)
    gl_rows = jnp.concatenate([jnp.broadcast_to(jnp.exp(g_tot[:, cc:cc + 1]), (DN_DK, 1)) for cc in cg], axis=0)
    s_ref[...] = s * gl_rows + _dot_tn(expand(k_dec), v_new)
    o_ref[0] = jnp.concatenate([o_s[h * CHUNK:(h + 1) * CHUNK] for h in range(nh)], axis=1)


def dn_scan_tc(feat_c, feat_l, ba_c, ba_l, a_log, dt_bias, rev):
    b, l, nf = feat_l.shape
    n_ctx, n_lat = feat_c.shape[1] // CHUNK, l // CHUNK
    dirn = 1 if rev else 0
    cc, lc = _scan_chunks(rev, n_ctx, n_lat)
    lanes = lambda p: jnp.zeros((1, LANES), F32).at[0, 2 * DN_HEADS:4 * DN_HEADS].set(p.reshape(-1))
    vec = pl.BlockSpec((1, LANES), lambda bi, j: (0, 0))
    return pl.pallas_call(
        functools.partial(_dn_scan_kernel, rev, dirn, n_ctx), grid=(b, n_ctx + n_lat),
        in_specs=[vec, vec,
                  pl.BlockSpec((1, CHUNK, nf), lambda bi, j: (bi, cc(j), 0)),
                  pl.BlockSpec((1, CHUNK, nf), lambda bi, j: (bi, lc(j), 0)),
                  pl.BlockSpec((1, CHUNK, LANES), lambda bi, j: (bi, cc(j), 0)),
                  pl.BlockSpec((1, CHUNK, LANES), lambda bi, j: (bi, lc(j), 0))],
        out_specs=pl.BlockSpec((1, CHUNK, DN_HEADS * DN_DV), lambda bi, j: (bi, lc(j), 0)),
        out_shape=jax.ShapeDtypeStruct((b, l, DN_HEADS * DN_DV), F32),
        scratch_shapes=[pltpu.VMEM((DN_HEADS * DN_DK, DN_DV), F32)],
        compiler_params=_cparams("parallel", "arbitrary"), name="dn_scan_bwd" if rev else "dn_scan_fwd",
    )(lanes(a_log), lanes(dt_bias), feat_c, feat_l, ba_c, ba_l)


def _from_grid_cols(blk, n):
    cols = blk.shape[1] // n
    return jnp.concatenate([blk[:, i * n:(i + 1) * n] for i in range(cols)], axis=0)


def _gla_scan_kernel(rev, dirn, n_ctx, qkc_ref, vc_ref, lrc_ref, qkl_ref, vl_ref, lrl_ref,
                     wla_ref, bla_ref, o_ref, s_ref):
    step = pl.program_id(1)

    @pl.when(step == 0)
    def _():
        s_ref[...] = jnp.zeros_like(s_ref)

    in_ctx = step < n_ctx
    hk, hv = GLA_HEADS * GLA_DK, GLA_HEADS * GLA_DV
    qk = jnp.where(in_ctx, qkc_ref[0], _from_grid_cols(qkl_ref[0], 2 * hk))
    vv = jnp.where(in_ctx, vc_ref[0], _from_grid_cols(vl_ref[0], hv))
    lr = jnp.where(in_ctx, lrc_ref[0], _from_grid_cols(lrl_ref[0], LANES))
    incl, _ = _tri_masks(rev)
    incl_b = incl.astype(BF16)
    pre = _dot(lr, wla_ref[0]) + bla_ref[0]
    la_all = -_softplus(-pre) * (1.0 / GLA_TAU)
    bc_all = _mask_dot(incl_b, la_all)
    b_tot_all = jnp.sum(la_all, axis=0, keepdims=True)
    outs = []
    for h in range(GLA_HEADS):
        q = qk[:, h * GLA_DK:(h + 1) * GLA_DK] * GLA_DK ** -0.5
        k = qk[:, hk + h * GLA_DK:hk + (h + 1) * GLA_DK]
        v = vv[:, h * GLA_DV:(h + 1) * GLA_DV]
        bc = bc_all[:, h * GLA_DK:(h + 1) * GLA_DK]
        b_tot = b_tot_all[:, h * GLA_DK:(h + 1) * GLA_DK]
        st = s_ref[h]
        o = _dot_nt(q * jnp.exp(bc), st)
        parts = []
        for i in range(CHUNK // GLA_SUB):
            lo_r, hi_r = i * GLA_SUB, (i + 1) * GLA_SUB
            if rev:
                ref = bc[hi_r - 1:hi_r]
                c0, c1 = lo_r, CHUNK
            else:
                ref = bc[lo_r:lo_r + 1]
                c0, c1 = 0, hi_r
            qi = q[lo_r:hi_r] * jnp.exp(bc[lo_r:hi_r] - ref)
            ki = k[c0:c1] * jnp.exp(ref - bc[c0:c1])
            att = _dot_nt(qi, ki)
            rg = lax.broadcasted_iota(jnp.int32, (GLA_SUB, c1 - c0), 0) + lo_r
            cg = lax.broadcasted_iota(jnp.int32, (GLA_SUB, c1 - c0), 1) + c0
            keep = (cg > rg) if rev else (cg < rg)
            parts.append(_dot(jnp.where(keep, att, 0.0), v[c0:c1]))
        diag = jnp.sum(q * k, axis=-1, keepdims=True) * v
        outs.append(o + jnp.concatenate(parts, axis=0) + diag)
        k_dec = k * jnp.exp(b_tot - bc)
        s_ref[h] = st * jnp.exp(b_tot) + _dot_tn(v, k_dec)
    o = jnp.concatenate(outs, axis=-1)
    rows = o_ref.shape[1]
    o_ref[0] = jnp.concatenate([o[i * rows:(i + 1) * rows] for i in range(CHUNK // rows)], axis=-1)


def gla_scan_tc(qk_c, v_c, lr_c, qk_l, v_l, lr_l, w_la, b_la, rev):
    b, l, _ = qk_l.shape
    rows = l // GRID_W
    cols = CHUNK // rows
    n_ctx, n_lat = qk_c.shape[1] // CHUNK, l // CHUNK
    dirn = 1 if rev else 0
    cc, lc = _scan_chunks(rev, n_ctx, n_lat)
    hv = GLA_HEADS * GLA_DV
    ctx_blk = lambda a: pl.BlockSpec((1, CHUNK, a.shape[2]), lambda bi, j: (bi, cc(j), 0))
    lat_blk = lambda n: pl.BlockSpec((1, rows, cols * n), lambda bi, j: (bi, 0, lc(j)))
    grid_view = lambda a: a.reshape(b, rows, GRID_W * a.shape[2])
    out = pl.pallas_call(
        functools.partial(_gla_scan_kernel, rev, dirn, n_ctx), grid=(b, n_ctx + n_lat),
        in_specs=[ctx_blk(qk_c), ctx_blk(v_c), ctx_blk(lr_c),
                  lat_blk(qk_l.shape[2]), lat_blk(v_l.shape[2]), lat_blk(lr_l.shape[2]),
                  pl.BlockSpec((1,) + w_la.shape[1:], lambda bi, j: (dirn, 0, 0)),
                  pl.BlockSpec((1,) + b_la.shape[1:], lambda bi, j: (dirn, 0, 0))],
        out_specs=lat_blk(hv),
        out_shape=jax.ShapeDtypeStruct((b, rows, GRID_W * hv), F32),
        scratch_shapes=[pltpu.VMEM((GLA_HEADS, GLA_DV, GLA_DK), F32)],
        compiler_params=_cparams("parallel", "arbitrary"), name="gla_scan_bwd" if rev else "gla_scan_fwd",
    )(qk_c, v_c, lr_c, grid_view(qk_l), grid_view(v_l), grid_view(lr_l), w_la, b_la)
    return out.reshape(b, l, hv)


def _head_norm_gate(o, gate, g, n_heads, dv):
    parts = []
    for h in range(n_heads):
        oh = o[:, h * dv:(h + 1) * dv]
        gh = gate[:, h * dv:(h + 1) * dv]
        yh = oh * lax.rsqrt(jnp.mean(oh * oh, axis=-1, keepdims=True) + EPS) * g
        parts.append(yh * (gh * jax.nn.sigmoid(gh)))
    return parts


def _mix_out_kernel(x_ref, dnf_ref, dnb_ref, z_ref, glf_ref, glb_ref, r_ref, mod_ref, dng_ref,
                    glg_ref, n2g_ref, wout_ref, wq_ref, x1_ref, h2_ref, q_ref):
    parts = (_head_norm_gate(dnf_ref[0] + dnb_ref[0], z_ref[0], dng_ref[...], DN_HEADS, DN_DV)
             + _head_norm_gate(glf_ref[0] + glb_ref[0], r_ref[0], glg_ref[...], GLA_HEADS, GLA_DV))
    y = jnp.dot(jnp.concatenate(parts, axis=-1).astype(BF16), wout_ref[...], preferred_element_type=F32)
    x1 = x_ref[0] + mod_ref[0, 2:3, :] * y
    x1_ref[0] = x1
    n = x1 * lax.rsqrt(jnp.mean(x1 * x1, axis=-1, keepdims=True) + EPS) * n2g_ref[...]
    h2 = n * (1.0 + mod_ref[0, 4:5, :]) + mod_ref[0, 3:4, :]
    h2_ref[0] = h2
    h_hi, h_lo = _split(h2)
    q_ref[0] = (jnp.dot(h_hi, wq_ref[0], preferred_element_type=F32)
                + jnp.dot(h_lo, wq_ref[0], preferred_element_type=F32)
                + jnp.dot(h_hi, wq_ref[1], preferred_element_type=F32))


def mix_out_tc(x, dn_f, dn_b, z, gl_f, gl_b, r, mod_l, dn_g, gla_g, n2_g, w_out, w_q):
    b, l, d = x.shape
    tok = lambda n: pl.BlockSpec((1, TOK_TILE, n), lambda bi, i: (bi, i, 0))
    full = lambda a: pl.BlockSpec(a.shape, lambda bi, i: (0,) * a.ndim)
    dn_g, gla_g, n2_g = dn_g.reshape(1, -1), gla_g.reshape(1, -1), n2_g.reshape(1, -1)
    nq = w_q.shape[2]
    return pl.pallas_call(
        _mix_out_kernel, grid=(b, l // TOK_TILE),
        in_specs=[tok(d), tok(dn_f.shape[2]), tok(dn_b.shape[2]), tok(z.shape[2]),
                  tok(gl_f.shape[2]), tok(gl_b.shape[2]), tok(r.shape[2]),
                  pl.BlockSpec((1,) + mod_l.shape[1:], lambda bi, i: (bi, 0, 0)),
                  full(dn_g), full(gla_g), full(n2_g), full(w_out), full(w_q)],
        out_specs=[tok(d), tok(d), tok(nq)],
        out_shape=[jax.ShapeDtypeStruct((b, l, d), F32), jax.ShapeDtypeStruct((b, l, d), F32),
                   jax.ShapeDtypeStruct((b, l, nq), F32)],
        compiler_params=_cparams("parallel", "arbitrary"), name="mix_out",
    )(x, dn_f, dn_b, z, gl_f, gl_b, r, mod_l, dn_g, gla_g, n2_g, w_out, w_q)


def _top_rows(s, k, payload=None):
    n = s.shape[0]
    row = lax.broadcasted_iota(jnp.int32, s.shape, 0).astype(F32)
    vals, picked = [], []
    for _ in range(k):
        m = jnp.max(s, axis=0, keepdims=True)
        first = jnp.min(jnp.where(s == m, row, float(n)), axis=0, keepdims=True)
        sel = row == first
        vals.append(m)
        if payload is None:
            picked.append(first)
        else:
            picked.append(jnp.max(jnp.where(sel, payload, -1.0), axis=0, keepdims=True))
        s = jnp.where(sel, -jnp.inf, s)
    return jnp.concatenate(vals, axis=0), jnp.concatenate(picked, axis=0)


def _candidate_rows(s0, i0, s1, i1):
    k = s0.shape[0]
    wide = SUBLANES
    blocks_s = [s0[0:1] + s1]
    blocks_i = [i0[0:1] * float(PEER_NKEYS) + i1]
    col = lax.broadcasted_iota(jnp.int32, (wide, s0.shape[1]), 0)
    for i in range(1, wide):
        keep = col < (k // (i + 1))
        blocks_s.append(jnp.where(keep, s0[i:i + 1] + s1[0:wide], -jnp.inf))
        blocks_i.append(i0[i:i + 1] * float(PEER_NKEYS) + i1[0:wide])
    blocks_s.append(s0[wide:k] + s1[0:1])
    blocks_i.append(i0[wide:k] * float(PEER_NKEYS) + i1[0:1])
    return jnp.concatenate(blocks_s, axis=0), jnp.concatenate(blocks_i, axis=0)


def _select_kernel(q_ref, k_ref, idx_ref, gate_ref, idx_s, gate_s):
    half = PEER_DQ // 2

    def head(h, carry):
        tops = []
        for p in range(2):
            qp = q_ref[:, pl.ds(pl.multiple_of(h * PEER_DQ + p * half, half), half)]
            s = lax.dot_general(k_ref[h, p], qp, (((1,), (1,)), ((), ())),
                                preferred_element_type=F32,
                                precision=lax.Precision.HIGHEST)
            tops.append(_top_rows(s, PEER_TOPK))
        (s0, i0), (s1, i1) = tops
        cand_s, cand_i = _candidate_rows(s0, i0, s1, i1)
        best_s, idx = _top_rows(cand_s, PEER_TOPK, payload=cand_i)
        e = jnp.exp(best_s - best_s[0:1])
        r0 = pl.multiple_of(h * PEER_TOPK, PEER_TOPK)
        idx_s[pl.ds(r0, PEER_TOPK), :] = idx
        gate_s[pl.ds(r0, PEER_TOPK), :] = e / jnp.sum(e, axis=0, keepdims=True)
        return carry

    lax.fori_loop(0, PEER_HEADS, head, 0)
    idx_ref[...] = idx_s[...].T.astype(jnp.int32)
    gate_ref[...] = gate_s[...].T


def peer_select_tc(q, keys):
    n_tok = q.shape[0]
    out_spec = pl.BlockSpec((SELECT_TILE, PICKS), lambda i: (i, 0))
    return pl.pallas_call(
        _select_kernel,
        grid=(n_tok // SELECT_TILE,),
        in_specs=[pl.BlockSpec((SELECT_TILE, q.shape[1]), lambda i: (i, 0)),
                  pl.BlockSpec(keys.shape, lambda i: (0, 0, 0, 0))],
        out_specs=[out_spec, out_spec],
        out_shape=[jax.ShapeDtypeStruct((n_tok, PICKS), jnp.int32),
                   jax.ShapeDtypeStruct((n_tok, PICKS), F32)],
        scratch_shapes=[pltpu.VMEM((PICKS, SELECT_TILE), F32), pltpu.VMEM((PICKS, SELECT_TILE), F32)],
        compiler_params=_cparams("parallel"), name="peer_select",
    )(q, keys)


def _sc_mesh():
    return plsc.VectorSubcoreMesh(core_axis_name="c", subcore_axis_name="s")


def _sc_pipeline(body, n_steps, in_specs, out_specs, operands):
    pltpu.emit_pipeline(
        body, grid=(n_steps,), in_specs=in_specs, out_specs=out_specs,
        core_axis_name=("c", "s"), dimension_semantics=(pltpu.PARALLEL,),
        trace_scopes=False,
    )(*operands)


def pack_pairs(t):
    half = t.shape[1] // 2
    bits = lax.bitcast_convert_type(t, jnp.uint32)
    rne = lambda b: (b + jnp.uint32(0x7FFF) + ((b >> 16) & jnp.uint32(1))) >> 16
    word = (rne(bits[:, half:]) << 16) | rne(bits[:, :half])
    return lax.bitcast_convert_type(word, jnp.int32)


def peer_act_partial_sc(u_q, idx2, h_q):
    n_groups, (n_tok, half) = idx2.shape[0], h_q.shape
    nsub = TOK_STEP * GROUPS_PER_TOK
    n_chunks = half // SC_LANES
    as_pairs = lambda w: plsc.bitcast(w, BF16)

    @functools.partial(
        pl.kernel, mesh=_sc_mesh(), compiler_params=pltpu.CompilerParams(needs_layout_passes=False),
        out_type=jax.ShapeDtypeStruct((n_groups, PICK_GROUP * SC_LANES), F32),
        scratch_types=[pltpu.VMEM((GATHER_BUFS, PICK_GROUP, half), jnp.int32),
                       pltpu.SemaphoreType.DMA((GATHER_BUFS,))],
    )
    def k(u_hbm, i_hbm, h_hbm, o_hbm, rows, sems):
        def body(i_v, h_v, o_v):
            def fetch(j, slot):
                return pltpu.make_async_copy(u_hbm.at[i_v.at[j]], rows.at[slot], sems.at[slot])

            for ahead in range(GATHER_BUFS - 1):
                fetch(ahead, ahead).start()

            def sub(j, carry):
                slot = j % GATHER_BUFS

                nxt = j + GATHER_BUFS - 1

                @pl.when(nxt < nsub)
                def _():
                    fetch(nxt, nxt % GATHER_BUFS).start()

                fetch(j, slot).wait()
                t = j // GROUPS_PER_TOK

                def picks(g, carry2):
                    kb = g * ACT_UNROLL
                    accs = [None] * ACT_UNROLL
                    for c0 in range(0, n_chunks, ACT_BF16_TERMS):
                        cols = [pl.ds((c0 + cc) * SC_LANES, SC_LANES) for cc in range(ACT_BF16_TERMS)]
                        hs = [as_pairs(h_v[t, col]) for col in cols]
                        for i in range(ACT_UNROLL):
                            s = None
                            for col, hv in zip(cols, hs):
                                p = as_pairs(rows[slot, kb + i, col]) * hv
                                s = p if s is None else s + p
                            lo, hi = plsc.unpack(s, format=plsc.PackFormat.INTERLEAVED)
                            p32 = lo + hi
                            accs[i] = p32 if accs[i] is None else accs[i] + p32
                    for i in range(ACT_UNROLL):
                        o_v[j, pl.ds((kb + i) * SC_LANES, SC_LANES)] = accs[i]
                    return carry2

                lax.fori_loop(0, PICK_GROUP // ACT_UNROLL, picks, 0)
                return carry

            lax.fori_loop(0, nsub, sub, 0)

        _sc_pipeline(
            body, n_tok // TOK_STEP,
            [pl.BlockSpec((nsub, PICK_GROUP), lambda i: (i, 0)),
             pl.BlockSpec((TOK_STEP, half), lambda i: (i, 0))],
            [pl.BlockSpec((nsub, PICK_GROUP * SC_LANES), lambda i: (i, 0))],
            (i_hbm, h_hbm, o_hbm))

    return k(u_q, idx2, h_q)


def peer_combine_sc(v_q, idx2, coef_q):
    n_groups = idx2.shape[0]
    half = v_q.shape[1]
    nsub = COMB_GROUP_STEP
    n_chunks = half // SC_LANES
    as_pairs = lambda w: plsc.bitcast(w, BF16)

    @functools.partial(
        pl.kernel, mesh=_sc_mesh(), compiler_params=pltpu.CompilerParams(needs_layout_passes=False),
        out_type=jax.ShapeDtypeStruct((n_groups, 2 * half), F32),
        scratch_types=[pltpu.VMEM((GATHER_BUFS, PICK_GROUP, half), jnp.int32),
                       pltpu.SemaphoreType.DMA((GATHER_BUFS,))],
    )
    def k(v_hbm, i_hbm, c_hbm, o_hbm, rows, sems):
        def body(i_v, c_v, o_v):
            def fetch(j, slot):
                return pltpu.make_async_copy(v_hbm.at[i_v.at[j]], rows.at[slot], sems.at[slot])

            for ahead in range(GATHER_BUFS - 1):
                fetch(ahead, ahead).start()

            def sub(j, carry0):
                slot = j % GATHER_BUFS

                nxt = j + GATHER_BUFS - 1

                @pl.when(nxt < nsub)
                def _():
                    fetch(nxt, nxt % GATHER_BUFS).start()

                fetch(j, slot).wait()
                cks = [as_pairs(c_v[j, pl.ds(kk * SC_LANES, SC_LANES)]) for kk in range(PICK_GROUP)]

                def chunk_pair(c2, carry):
                    sums = []
                    for cc in range(COMB_CHUNKS):
                        l = (c2 * COMB_CHUNKS + cc) * SC_LANES
                        tot_lo, tot_hi = None, None
                        for k0 in range(0, PICK_GROUP, COMB_BF16_TERMS):
                            s = None
                            for kk in range(k0, k0 + COMB_BF16_TERMS):
                                p = cks[kk] * as_pairs(rows[slot, kk, pl.ds(l, SC_LANES)])
                                s = p if s is None else s + p
                            lo, hi = plsc.unpack(s, format=plsc.PackFormat.INTERLEAVED)
                            tot_lo = lo if tot_lo is None else tot_lo + lo
                            tot_hi = hi if tot_hi is None else tot_hi + hi
                        sums.append((l, tot_lo, tot_hi))
                    for l, tot_lo, tot_hi in sums:
                        o_v[j, pl.ds(l, SC_LANES)] = tot_lo
                        o_v[j, pl.ds(half + l, SC_LANES)] = tot_hi
                    return carry

                lax.fori_loop(0, n_chunks // COMB_CHUNKS, chunk_pair, 0)
                return carry0

            lax.fori_loop(0, nsub, sub, 0)

        _sc_pipeline(
            body, n_groups // nsub,
            [pl.BlockSpec((nsub, PICK_GROUP), lambda i: (i, 0)),
             pl.BlockSpec((nsub, PICK_GROUP * SC_LANES), lambda i: (i, 0))],
            [pl.BlockSpec((nsub, 2 * half), lambda i: (i, 0))],
            (i_hbm, c_hbm, o_hbm))

    return k(v_q, idx2, coef_q)


def _segment_matrix():
    r = lax.broadcasted_iota(jnp.int32, (PICK_GROUP * SC_LANES, PICK_GROUP), 0) // SC_LANES
    c = lax.broadcasted_iota(jnp.int32, (PICK_GROUP * SC_LANES, PICK_GROUP), 1)
    return (r == c).astype(F32)


def _coef_kernel(part_ref, gate_ref, o_ref):
    seg = _segment_matrix()
    act = jnp.dot(part_ref[...], seg, preferred_element_type=F32, precision=lax.Precision.HIGHEST)
    coef = gate_ref[...] * (0.5 * act * (1.0 + lax.erf(act * (2.0 ** -0.5))))
    wide = lax.dot_general(coef, seg, (((1,), (1,)), ((), ())), preferred_element_type=F32,
                           precision=lax.Precision.HIGHEST)
    bits = pltpu.bitcast(wide, jnp.uint32)
    r = (bits + jnp.uint32(0x7FFF) + ((bits >> 16) & jnp.uint32(1))) >> 16
    o_ref[...] = pltpu.bitcast(r | (r << 16), jnp.int32)


def peer_coef_tc(part, gate2):
    n_groups, width = part.shape
    tile = 1024
    return pl.pallas_call(
        _coef_kernel,
        grid=(n_groups // tile,),
        in_specs=[pl.BlockSpec((tile, width), lambda i: (i, 0)),
                  pl.BlockSpec((tile, PICK_GROUP), lambda i: (i, 0))],
        out_specs=pl.BlockSpec((tile, width), lambda i: (i, 0)),
        out_shape=jax.ShapeDtypeStruct((n_groups, width), jnp.int32),
        compiler_params=_cparams("parallel"), name="peer_coef",
    )(part, gate2)


def _final_kernel(x_ref, y_ref, mod_ref, g_ref, o_ref):
    d = x_ref.shape[2]
    y = y_ref[0, :, 0:d]
    for p in range(1, y_ref.shape[2] // d):
        y = y + y_ref[0, :, p * d:(p + 1) * d]
    x = x_ref[0] + mod_ref[0, 5:6, :] * y
    o_ref[0] = x * lax.rsqrt(jnp.mean(x * x, axis=-1, keepdims=True) + EPS) * g_ref[...]


def final_tc(x1, y_parts, mod_l, final_g):
    b, l, d = x1.shape
    tok = pl.BlockSpec((1, TOK_TILE, d), lambda bi, i: (bi, i, 0))
    return pl.pallas_call(
        _final_kernel, grid=(b, l // TOK_TILE),
        in_specs=[tok, pl.BlockSpec((1, TOK_TILE, y_parts.shape[2]), lambda bi, i: (bi, i, 0)),
                  pl.BlockSpec((1,) + mod_l.shape[1:], lambda bi, i: (bi, 0, 0)),
                  pl.BlockSpec((1, d), lambda bi, i: (0, 0))],
        out_specs=tok, out_shape=jax.ShapeDtypeStruct((b, l, d), F32),
        compiler_params=_cparams("parallel", "arbitrary"), name="final_norm",
    )(x1, y_parts, mod_l, final_g.reshape(1, d))


def _pad_cols(w, n):
    return jnp.pad(w, ((0, 0), (0, n - w.shape[1])))


def _slice_sizes(b):
    if b % SLICE_BATCH or SLICE_BATCH % 2 or b < 2 * SLICE_BATCH:
        return [1] * b
    half = SLICE_BATCH // 2
    return [half] + [SLICE_BATCH] * (b // SLICE_BATCH - 1) + [half]


def _exact_zero(v):
    return jnp.minimum(jnp.abs(v), 0.0)


def forward(x, c, ctx, c_ctx, w_ada, b_ada, norm1_g, norm2_g, w_in, conv_w, dn_a_log,
            dn_dt_bias, dn_norm_g, gla_wa2, gla_ba, gla_norm_g, w_out, peer_wq, peer_keys,
            peer_u, peer_v, final_g):
    b, l, d = x.shape
    c_all = jnp.concatenate([c, c_ctx[None]], axis=0)
    c_all = jnp.pad(c_all, ((0, (-c_all.shape[0]) % SUBLANES), (0, 0)))
    mod = adaln_mod(c_all, w_ada, b_ada)
    mod_l = mod[:b].reshape(b, 6, d)
    mod_c = jnp.broadcast_to(mod[b].reshape(1, 6, d), (b, 6, d))
    o = DN_QKV
    hv = DN_HEADS * DN_DV
    w_dn_qkv, w_dn_z = w_in[:, :o], w_in[:, o:o + hv]
    w_dn_ba = _pad_cols(w_in[:, o + hv:DN_COLS], LANES)
    g0 = DN_COLS
    gqk, gv = 2 * GLA_HEADS * GLA_DK, GLA_HEADS * GLA_DV
    w_gl_qk, w_gl_v = w_in[:, g0:g0 + gqk], w_in[:, g0 + gqk:g0 + gqk + gv]
    w_gl_r = w_in[:, g0 + gqk + gv:g0 + gqk + 2 * gv]
    w_gl_lr = _pad_cols(w_in[:, g0 + gqk + 2 * gv:], LANES)
    w_lat = [w.astype(BF16) for w in (w_dn_qkv, w_dn_ba, w_gl_qk, w_gl_v, w_gl_lr, w_dn_z, w_gl_r)]
    w_ctx = w_lat[:5]
    w_la = jnp.zeros((2, LANES, GLA_HEADS * GLA_DK), F32)
    for dd in range(2):
        w_la = w_la.at[dd, dd * GLA_LR:(dd + 1) * GLA_LR].set(gla_wa2[dd])
    w_la = w_la.astype(BF16)
    b_la = gla_ba.reshape(2, 1, GLA_HEADS * GLA_DK)
    w_out_b, w_q_b = w_out.astype(BF16), jnp.stack(_split(peer_wq))
    u_q, v_q = pack_pairs(peer_u), pack_pairs(peer_v)

    def mixer(xg, ctxg, mod_cg, mod_g, after_select, after_combine):
        mod_cg, mod_g = mod_cg + after_select, mod_g + after_select
        c_qkv, c_ba, c_qk, c_v, c_lr = in_projection(ctxg, norm1_g, mod_cg[:, 0:2], w_ctx)
        l_qkv, l_ba, l_qk, l_v, l_lr, l_z, l_r = in_projection(xg, norm1_g, mod_g[:, 0:2], w_lat)
        feat_c, feat_l = dn_features_tc(c_qkv, conv_w), dn_features_tc(l_qkv, conv_w)
        dn_f = dn_scan_tc(feat_c, feat_l, c_ba, l_ba, dn_a_log, dn_dt_bias, rev=False)
        dn_b = dn_scan_tc(feat_c, feat_l, c_ba, l_ba, dn_a_log, dn_dt_bias, rev=True)
        gl_f = gla_scan_tc(c_qk, c_v, c_lr, l_qk, l_v, l_lr, w_la, b_la, rev=False)
        gl_b = gla_scan_tc(c_qk, c_v, c_lr, l_qk, l_v, l_lr, w_la, b_la, rev=True)
        return mix_out_tc(xg, dn_f, dn_b, l_z, gl_f, gl_b, l_r, mod_g, dn_norm_g + after_combine,
                          gla_norm_g, norm2_g, w_out_b, w_q_b)

    def select_and_act(h2, q, after_coef):
        n_tok = h2.shape[0] * l
        idx, gate = peer_select_tc(q.reshape(n_tok, -1), peer_keys + after_coef)
        idx2 = idx.reshape(n_tok * GROUPS_PER_TOK, PICK_GROUP)
        gate2 = gate.reshape(n_tok * GROUPS_PER_TOK, PICK_GROUP)
        return idx2, gate2, peer_act_partial_sc(u_q, idx2, pack_pairs(h2.reshape(n_tok, d)))

    def coef_and_combine(st, after_mixer):
        coef_q = peer_coef_tc(st["part"], st["gate2"] + after_mixer)
        st["y"] = peer_combine_sc(v_q, st["idx2"], coef_q)
        return _exact_zero(coef_q[0, 0].astype(F32))

    zero = jnp.zeros((), F32)
    slices, z_sel, z_coef = [], zero, zero
    i = 0
    for bg in _slice_sizes(b):
        g = len(slices)
        z_comb = _exact_zero(slices[g - 2]["y"][0, 0]) if g >= 2 else zero
        mod_g = mod_l[i:i + bg]
        x1, h2, q = mixer(x[i:i + bg], ctx[i:i + bg], mod_c[i:i + bg], mod_g, z_sel, z_comb)
        i += bg
        if g >= 1:
            z_coef = coef_and_combine(slices[g - 1], _exact_zero(x1[0, 0, 0]))
        idx2, gate2, part = select_and_act(h2, q, z_coef)
        z_sel = _exact_zero(gate2[0, 0])
        slices.append(dict(x1=x1, mod=mod_g, idx2=idx2, gate2=gate2, part=part))
    coef_and_combine(slices[-1], zero)
    outs = [final_tc(st["x1"], st["y"].reshape(st["x1"].shape[:2] + (-1,)), st["mod"], final_g) for st in slices]
    return jnp.concatenate(outs, axis=0)


def kernel(x, c, ctx, c_ctx, w_ada, b_ada, norm1_g, norm2_g, w_in, conv_w, dn_a_log,
           dn_dt_bias, dn_norm_g, gla_wa2, gla_ba, gla_norm_g, w_out, peer_wq, peer_keys,
           peer_u, peer_v, final_g):
    assert w_ada.shape[0] == 1, "single-layer block: the context stream is only consumed, never updated"
    return forward(x, c, ctx, c_ctx, w_ada[0], b_ada[0], norm1_g[0], norm2_g[0], w_in[0], conv_w[0],
                   dn_a_log[0], dn_dt_bias[0], dn_norm_g[0], gla_wa2[0], gla_ba[0], gla_norm_g[0],
                   w_out[0], peer_wq[0], peer_keys[0], peer_u[0], peer_v[0], final_g)
```
